```python
import math
import jax, jax.numpy as jnp
from jax import lax
import numpy as np

D_MODEL = 1024
BATCH = 16
SEQ = 256
DEPTH = 1
DEC_BATCH = 2
DEC_SEQ = 2048
PAST_LEN = 256

GRID_W = 64
MLA_HEADS = 8
MLA_NOPE = 64
MLA_ROPE = 32
MLA_QK = MLA_NOPE + MLA_ROPE
MLA_V = 64
Q_LORA = 384
KV_LORA = 256
GLA_HEADS = 4
GLA_DK = 64
GLA_DV = 128
GATE_RANK = 16
GATE_NORM = 16.0
CHUNK = 64
D_MIX = MLA_HEADS * MLA_V + GLA_HEADS * GLA_DV
D_FF = 2816
QBLOCK = 128
ROPE_BASE = 10000.0
EPS = 1e-6
IN_SIZES = (Q_LORA, KV_LORA, MLA_ROPE, GLA_HEADS * GLA_DK, GLA_HEADS * GLA_DK,
            GLA_HEADS * GLA_DV, GATE_RANK, GATE_RANK, GLA_HEADS * GLA_DV)
IN_COLS = Q_LORA + KV_LORA + MLA_ROPE + 2 * GLA_HEADS * GLA_DK + 2 * GLA_HEADS * GLA_DV + 2 * GATE_RANK

kernel_name = "hybrid_mla_gla_diffusion_step"


def _rmsnorm(x, w):
    xf = x.astype(jnp.float32)
    xf = xf * lax.rsqrt(jnp.mean(xf * xf, axis=-1, keepdims=True) + EPS)
    return (xf * w.astype(jnp.float32)).astype(x.dtype)


def _split_in(z):
    idx = []
    acc = 0
    for s in IN_SIZES[:-1]:
        acc += s
        idx.append(acc)
    return jnp.split(z, idx, axis=-1)


def _axial_rope_tables(n_tokens):
    rows = n_tokens // GRID_W
    t = jnp.arange(rows * GRID_W)
    row = (t // GRID_W).astype(jnp.float32)
    col = (t % GRID_W).astype(jnp.float32)
    half = MLA_ROPE // 2
    inv = ROPE_BASE ** (-jnp.arange(0, half, 2, dtype=jnp.float32) / half)
    ang_r = row[:, None] * inv
    ang_c = col[:, None] * inv
    ang = jnp.concatenate([ang_r, ang_r, ang_c, ang_c], axis=-1)
    return jnp.cos(ang), jnp.sin(ang)


def _rotate_half(x):
    x1, x2 = jnp.split(x, 2, axis=-1)
    return jnp.concatenate([-x2, x1], axis=-1)


def _apply_axial_rope(x, cos, sin):
    xf = x.astype(jnp.float32)
    half = MLA_ROPE // 2
    rot = jnp.concatenate([_rotate_half(xf[..., :half]), _rotate_half(xf[..., half:])], axis=-1)
    return (xf * cos + rot * sin).astype(x.dtype)


def _mla_decompress(ckv, k_rope, w_ukv):
    b, s, _ = ckv.shape
    kv = (ckv @ w_ukv).reshape(b, s, MLA_HEADS, MLA_NOPE + MLA_V)
    k_nope, v = kv[..., :MLA_NOPE], kv[..., MLA_NOPE:]
    k_r = jnp.broadcast_to(k_rope[:, :, None, :], (b, s, MLA_HEADS, MLA_ROPE))
    return jnp.concatenate([k_nope, k_r], axis=-1), v


def _blocked_attention(q, k, v):
    b, t, h, d = q.shape
    nb = t // QBLOCK
    qb = q.reshape(b, nb, QBLOCK, h, d).transpose(1, 0, 2, 3, 4)
    scale = MLA_QK ** -0.5

    def one_block(qi):
        s = jnp.einsum('bqhd,bkhd->bhqk', qi, k).astype(jnp.float32) * scale
        p = jax.nn.softmax(s, axis=-1).astype(v.dtype)
        return jnp.einsum('bhqk,bkhd->bqhd', p, v)

    out = lax.map(one_block, qb)
    return out.transpose(1, 0, 2, 3, 4).reshape(b, t, h, v.shape[-1])


def _gla_chunk(q, k, v, g, s0):
    b, h, t, dk = q.shape
    dv = v.shape[-1]
    n = t // CHUNK
    f32 = jnp.float32
    q = q.astype(f32).reshape(b, h, n, CHUNK, dk) * (dk ** -0.5)
    k = k.astype(f32).reshape(b, h, n, CHUNK, dk)
    v = v.astype(f32).reshape(b, h, n, CHUNK, dv)
    cum = jnp.cumsum(g.astype(f32).reshape(b, h, n, CHUNK, dk), axis=3)
    cum_last = cum[:, :, :, -1:, :]
    qe = q * jnp.exp(cum)
    ke = k * jnp.exp(-cum)
    kd = k * jnp.exp(cum_last - cum)
    mask = jnp.tril(jnp.ones((CHUNK, CHUNK), dtype=bool))
    att = jnp.where(mask, jnp.einsum('bhncd,bhnsd->bhncs', qe, ke), 0.0)
    o_intra = jnp.einsum('bhncs,bhnse->bhnce', att, v)
    decay = jnp.exp(cum_last[:, :, :, 0, :])

    def step(state, inp):
        qe_n, kd_n, v_n, dec_n = inp
        o_n = jnp.einsum('bhcd,bhde->bhce', qe_n, state)
        state = dec_n[..., None] * state + jnp.einsum('bhcd,bhce->bhde', kd_n, v_n)
        return state, o_n

    xs = (jnp.moveaxis(qe, 2, 0), jnp.moveaxis(kd, 2, 0), jnp.moveaxis(v, 2, 0), jnp.moveaxis(decay, 2, 0))
    s_final, o_inter = lax.scan(step, s0.astype(f32), xs)
    o = o_intra + jnp.moveaxis(o_inter, 0, 2)
    return o.reshape(b, h, t, dv), s_final


def _gla_gate(low, w2, b2):
    return jax.nn.log_sigmoid((low @ w2 + b2).astype(jnp.float32)) / GATE_NORM


def _layer(x, cond, lp, rope, ctx):
    (w_ada, b_ada, norm_attn, w_in, q_norm, w_uq, kv_norm, w_ukv, w_gate_f, b_gate_f,
     w_gate_b, b_gate_b, gla_norm, w_out, norm_ffn, w_ffn_in, w_ffn_out) = lp
    b, t, _ = x.shape
    mod = jax.nn.silu(cond) @ w_ada + b_ada
    sh1, sc1, gt1, sh2, sc2, gt2 = jnp.split(mod, 6, axis=-1)
    h = _rmsnorm(x, norm_attn) * (1 + sc1) + sh1
    q_lat, kv_lat, k_rope, gq, gk, gv, gf_low, gb_low, g_out = _split_in(h @ w_in)

    q = (_rmsnorm(q_lat, q_norm) @ w_uq).reshape(b, t, MLA_HEADS, MLA_QK)
    ckv = _rmsnorm(kv_lat, kv_norm)
    if rope is not None:
        cos, sin = rope
        q = jnp.concatenate([q[..., :MLA_NOPE],
                             _apply_axial_rope(q[..., MLA_NOPE:], cos[:, None, :], sin[:, None, :])], axis=-1)
        k_rope_pos = _apply_axial_rope(k_rope, cos, sin)
    else:
        k_rope_pos = k_rope
    k, v = _mla_decompress(ckv, k_rope_pos, w_ukv)
    if ctx is not None:
        ckv_ctx, krope_ctx, s_f0, s_b0 = ctx
        k_c, v_c = _mla_decompress(ckv_ctx.astype(x.dtype), krope_ctx.astype(x.dtype), w_ukv)
        k = jnp.concatenate([k_c, k], axis=1)
        v = jnp.concatenate([v_c, v], axis=1)
    else:
        s_f0 = jnp.zeros((b, GLA_HEADS, GLA_DK, GLA_DV), jnp.float32)
        s_b0 = jnp.zeros((b, GLA_HEADS, GLA_DK, GLA_DV), jnp.float32)
    attn = _blocked_attention(q, k, v).reshape(b, t, MLA_HEADS * MLA_V)

    def heads(a, d):
        return a.reshape(b, t, GLA_HEADS, d).transpose(0, 2, 1, 3)
    qg, kg, vg = heads(gq, GLA_DK), heads(gk, GLA_DK), heads(gv, GLA_DV)
    g_f = heads(_gla_gate(gf_low, w_gate_f, b_gate_f), GLA_DK)
    g_b = heads(_gla_gate(gb_low, w_gate_b, b_gate_b), GLA_DK)
    o_f, s_f = _gla_chunk(qg, kg, vg, g_f, s_f0)
    o_b, s_b = _gla_chunk(jnp.flip(qg, 2), jnp.flip(kg, 2), jnp.flip(vg, 2), jnp.flip(g_b, 2), s_b0)
    o = (o_f + jnp.flip(o_b, 2)).transpose(0, 2, 1, 3).astype(x.dtype)
    o = _rmsnorm(o, gla_norm).reshape(b, t, GLA_HEADS * GLA_DV) * jax.nn.silu(g_out)

    mix = jnp.concatenate([attn, o], axis=-1) @ w_out
    x = x + gt1 * mix
    h2 = _rmsnorm(x, norm_ffn) * (1 + sc2) + sh2
    a, g = jnp.split(h2 @ w_ffn_in, 2, axis=-1)
    x = x + gt2 * ((jax.nn.silu(a) * g) @ w_ffn_out)
    return x, (ckv, k_rope, s_f, s_b)


def setup_inputs(seed: int = 0) -> dict:
    key = jax.random.key(seed)
    ks = jax.random.split(key, 32)
    f32 = jnp.float32

    def nrm(k, shape, scale):
        return jax.random.normal(k, shape, f32) * scale

    def gain(k, shape):
        return 1.0 + 0.1 * jax.random.normal(k, shape, f32)

    L = DEPTH
    return {
        "x_prompt": nrm(ks[0], (BATCH, SEQ, D_MODEL), 1.0),
        "x_sample": nrm(ks[1], (DEC_BATCH, DEC_SEQ, D_MODEL), 1.0),
        "cache_kv_latent": nrm(ks[2], (DEC_BATCH, L, PAST_LEN, KV_LORA), 1.0),
        "cache_k_rope": nrm(ks[3], (DEC_BATCH, L, PAST_LEN, MLA_ROPE), 1.0),
        "state_gla_fwd": nrm(ks[4], (DEC_BATCH, L, GLA_HEADS, GLA_DK, GLA_DV), 0.3),
        "state_gla_bwd": nrm(ks[5], (DEC_BATCH, L, GLA_HEADS, GLA_DK, GLA_DV), 0.3),
        "c": nrm(ks[6], (DEC_BATCH, D_MODEL), 1.0),
        "c_ctx": nrm(ks[7], (D_MODEL,), 1.0),
        "w_ada": nrm(ks[8], (L, D_MODEL, 6 * D_MODEL), D_MODEL ** -0.5),
        "b_ada": nrm(ks[9], (L, 6 * D_MODEL), 0.02),
        "norm_attn": gain(ks[10], (L, D_MODEL)),
        "w_in": nrm(ks[11], (L, D_MODEL, IN_COLS), D_MODEL ** -0.5),
        "mla_q_norm": gain(ks[12], (L, Q_LORA)),
        "w_uq": nrm(ks[13], (L, Q_LORA, MLA_HEADS * MLA_QK), Q_LORA ** -0.5),
        "mla_kv_norm": gain(ks[14], (L, KV_LORA)),
        "w_ukv": nrm(ks[15], (L, KV_LORA, MLA_HEADS * (MLA_NOPE + MLA_V)), KV_LORA ** -0.5),
        "w_gate_f": nrm(ks[16], (L, GATE_RANK, GLA_HEADS * GLA_DK), GATE_RANK ** -0.5),
        "b_gate_f": nrm(ks[17], (L, GLA_HEADS * GLA_DK), 0.1),
        "w_gate_b": nrm(ks[18], (L, GATE_RANK, GLA_HEADS * GLA_DK), GATE_RANK ** -0.5),
        "b_gate_b": nrm(ks[19], (L, GLA_HEADS * GLA_DK), 0.1),
        "gla_norm": gain(ks[20], (L, GLA_DV)),
        "w_out": nrm(ks[21], (L, D_MIX, D_MODEL), D_MIX ** -0.5),
        "norm_ffn": gain(ks[22], (L, D_MODEL)),
        "w_ffn_in": nrm(ks[23], (L, D_MODEL, 2 * D_FF), D_MODEL ** -0.5),
        "w_ffn_out": nrm(ks[24], (L, D_FF, D_MODEL), D_FF ** -0.5),
        "final_norm": gain(ks[25], (D_MODEL,)),
    }


def reference(x_prompt, x_sample, cache_kv_latent, cache_k_rope, state_gla_fwd, state_gla_bwd,
              c, c_ctx, w_ada, b_ada, norm_attn, w_in, mla_q_norm, w_uq, mla_kv_norm, w_ukv,
              w_gate_f, b_gate_f, w_gate_b, b_gate_b, gla_norm, w_out, norm_ffn, w_ffn_in,
              w_ffn_out, final_norm):
    rope = _axial_rope_tables(x_sample.shape[1])
    cond_ctx = c_ctx[None, None, :]
    cond_lat = c[:, None, :]
    xp, xs = x_prompt, x_sample
    kv_list, kr_list, sf_list, sb_list = [], [], [], []
    for l in range(DEPTH):
        lp = (w_ada[l], b_ada[l], norm_attn[l], w_in[l], mla_q_norm[l], w_uq[l], mla_kv_norm[l],
              w_ukv[l], w_gate_f[l], b_gate_f[l], w_gate_b[l], b_gate_b[l], gla_norm[l], w_out[l],
              norm_ffn[l], w_ffn_in[l], w_ffn_out[l])
        xp, (ckv, kr, sf, sb) = _layer(xp, cond_ctx, lp, None, None)
        kv_list.append(ckv)
        kr_list.append(kr)
        sf_list.append(sf.astype(x_prompt.dtype))
        sb_list.append(sb.astype(x_prompt.dtype))
        ctx = (cache_kv_latent[:, l], cache_k_rope[:, l], state_gla_fwd[:, l], state_gla_bwd[:, l])
        xs, _ = _layer(xs, cond_lat, lp, rope, ctx)
    y_prompt = _rmsnorm(xp, final_norm)
    y_sample = _rmsnorm(xs, final_norm)
    new_kv_latent = jnp.stack(kv_list, axis=1)
    new_k_rope = jnp.stack(kr_list, axis=1)
    new_state_fwd = jnp.stack(sf_list, axis=1)
    new_state_bwd = jnp.stack(sb_list, axis=1)
    return (y_prompt, y_sample, new_kv_latent, new_k_rope, new_state_fwd, new_state_bwd)
```

```python
import functools

import numpy as np
import jax
import jax.numpy as jnp
from jax import lax
from jax.experimental import pallas as pl
from jax.experimental.pallas import tpu as pltpu

F32 = jnp.float32
BF16 = jnp.bfloat16

GRID_W = 64
MLA_HEADS = 8
MLA_NOPE = 64
MLA_ROPE = 32
MLA_QK = MLA_NOPE + MLA_ROPE
MLA_V = 64
Q_LORA = 384
KV_LORA = 256
GLA_HEADS = 4
GLA_DK = 64
GLA_DV = 128
GATE_RANK = 16
GATE_NORM = 16.0
CHUNK = 64
D_FF = 2816
ROPE_BASE = 10000.0
EPS = 1e-6

LANES = 128
HEAD_PAD = LANES
ROPE_LANE0 = MLA_NOPE
GQK = GLA_HEADS * GLA_DK
GV = GLA_HEADS * GLA_DV
QPAD = MLA_HEADS * HEAD_PAD
VALL = MLA_HEADS * MLA_V

Z_Q = 0
Z_KV = Z_Q + Q_LORA
Z_GQ = Z_KV + KV_LORA
Z_GK = Z_GQ + GQK
Z_GV = Z_GK + GQK
Z_GO = Z_GV + GV
Z_MISC = Z_GO + GV
Z_COLS = Z_MISC + LANES

FF_CHUNK = 256
N_FF_CHUNKS = D_FF // FF_CHUNK

VMEM_LIMIT = 56 * 1024 * 1024

_NT = (((1,), (1,)), ((), ()))
_TN = (((0,), (0,)), ((), ()))


def _rms(x, w):
    return x * lax.rsqrt(jnp.mean(x * x, axis=-1, keepdims=True) + EPS) * w


def _sigmoid(x):
    return 1.0 / (1.0 + jnp.exp(-x))


def _log_sigmoid(x):
    return jnp.minimum(x, 0.0) - jnp.log1p(jnp.exp(-jnp.abs(x)))


def _const_spec(shape):
    nd = len(shape)
    return pl.BlockSpec(shape, lambda *_: (0,) * nd, pipeline_mode=pl.Buffered(1))


def _ada_kernel(cond_ref, w_ref, b_ref, o_ref):
    c = cond_ref[...]
    s = (c * _sigmoid(c)).astype(BF16)
    o_ref[...] = jnp.dot(s, w_ref[...].astype(BF16), preferred_element_type=F32) + b_ref[...]


def _ada(cond8, w_ada, b_ada):
    d = w_ada.shape[0]
    n = w_ada.shape[1]
    bn = d
    return pl.pallas_call(
        _ada_kernel,
        out_shape=jax.ShapeDtypeStruct((8, n), F32),
        grid=(n // bn,),
        in_specs=[pl.BlockSpec((8, d), lambda j: (0, 0)),
                  pl.BlockSpec((d, bn), lambda j: (0, j)),
                  pl.BlockSpec((1, bn), lambda j: (0, j))],
        out_specs=pl.BlockSpec((8, bn), lambda j: (0, j)),
        name="ada_mod",
        compiler_params=pltpu.CompilerParams(dimension_semantics=("arbitrary",)),
    )(cond8, w_ada, b_ada.reshape(1, n))


def _inproj_kernel(*refs, latent):
    (x_ref, mod_ref, nw_ref, win_ref, qn_ref, wuq_ref, kvn_ref, wukv_ref, wg_ref, bg_ref) = refs[:10]
    if latent:
        cos_ref, sa_ref, sb_ref = refs[10:13]
        outs = refs[13:]
    else:
        outs = refs[10:]
    q_ref, k_ref, v_ref, gq_ref, gk_ref, gv_ref, gf_ref, gb_ref, go_ref = outs[:9]

    x = x_ref[...]
    h = _rms(x, nw_ref[...]) * (1.0 + mod_ref[1:2, :]) + mod_ref[0:1, :]
    z = jnp.dot(h.astype(BF16), win_ref[...], preferred_element_type=F32)

    qn = _rms(z[:, Z_Q:Z_Q + Q_LORA], qn_ref[...])
    q = jnp.dot(qn.astype(BF16), wuq_ref[...], preferred_element_type=F32)
    ckv = _rms(z[:, Z_KV:Z_KV + KV_LORA], kvn_ref[...])
    kv = jnp.dot(ckv.astype(BF16), wukv_ref[...], preferred_element_type=F32)
    misc = z[:, Z_MISC:Z_MISC + LANES]

    if latent:
        cos, sa, sb = cos_ref[...], sa_ref[...], sb_ref[...]

        def rope(t):
            return t * cos + pltpu.roll(t, LANES - 8, 1) * sa + pltpu.roll(t, 8, 1) * sb
    else:
        def rope(t):
            return t

    scale = MLA_QK ** -0.5
    lane = lax.broadcasted_iota(jnp.int32, misc.shape, 1)
    in_rope = (lane >= ROPE_LANE0) & (lane < ROPE_LANE0 + MLA_ROPE)
    krope = rope(misc)
    for hd in range(MLA_HEADS):
        sl = slice(hd * HEAD_PAD, (hd + 1) * HEAD_PAD)
        q_ref[:, sl] = (rope(q[:, sl]) * scale).astype(BF16)
        k_ref[:, sl] = jnp.where(in_rope, krope, kv[:, sl]).astype(BF16)
    v_ref[...] = kv[:, QPAD:].astype(BF16)

    gq_ref[...] = z[:, Z_GQ:Z_GQ + GQK]
    gk_ref[...] = z[:, Z_GK:Z_GK + GQK]
    gv_ref[...] = z[:, Z_GV:Z_GV + GV].astype(BF16)
    go_ref[...] = z[:, Z_GO:Z_GO + GV]
    gpre = jnp.dot(misc.astype(BF16), wg_ref[...], preferred_element_type=F32) + bg_ref[...]
    gate = _log_sigmoid(gpre) * (1.0 / GATE_NORM)
    gf_ref[...] = gate[:, :GQK]
    gb_ref[...] = gate[:, GQK:]

    if not latent:
        ckv_ref, kr_ref = outs[9:]
        ckv_ref[...] = ckv
        kr_ref[...] = misc[:, ROPE_LANE0:ROPE_LANE0 + MLA_ROPE]


def _inproj(x2d, mod3, mod_row_fn, weights, rope_tabs, tm, tiles_per_seq):
    n_tok, d = x2d.shape
    latent = rope_tabs is not None
    nw, win, qn, wuq, kvn, wukv, wg, bg = weights
    row = lambda i: (i, 0)
    in_specs = [pl.BlockSpec((tm, d), row),
                pl.BlockSpec((None, 6, d), lambda i: (mod_row_fn(i), 0, 0)),
                _const_spec(nw.shape), _const_spec(win.shape), _const_spec(qn.shape),
                _const_spec(wuq.shape), _const_spec(kvn.shape), _const_spec(wukv.shape),
                _const_spec(wg.shape), _const_spec(bg.shape)]
    args = [x2d, mod3, nw, win, qn, wuq, kvn, wukv, wg, bg]
    if latent:
        tab = pl.BlockSpec((tm, LANES), lambda i: (i % tiles_per_seq, 0))
        in_specs += [tab, tab, tab]
        args += list(rope_tabs)
    out_cols = [(QPAD, BF16), (QPAD, BF16), (VALL, BF16), (GQK, F32), (GQK, F32), (GV, BF16),
                (GQK, F32), (GQK, F32), (GV, F32)]
    if not latent:
        out_cols += [(KV_LORA, F32), (MLA_ROPE, F32)]
    out_shape = [jax.ShapeDtypeStruct((n_tok, c), dt) for c, dt in out_cols]
    out_specs = [pl.BlockSpec((tm, c), row) for c, _ in out_cols]
    return pl.pallas_call(
        functools.partial(_inproj_kernel, latent=latent),
        out_shape=out_shape,
        grid=(n_tok // tm,),
        in_specs=in_specs,
        out_specs=out_specs,
        name="inproj_lat" if latent else "inproj_ctx",
        compiler_params=pltpu.CompilerParams(dimension_semantics=("arbitrary",),
                                             vmem_limit_bytes=VMEM_LIMIT),
    )(*args)


def _decomp_kernel(ckv_ref, kr_ref, wukv_ref, k_ref, v_ref):
    kv = jnp.dot(ckv_ref[...].astype(BF16), wukv_ref[...], preferred_element_type=F32)
    kr = kr_ref[...]
    lane = lax.broadcasted_iota(jnp.int32, kr.shape, 1)
    in_rope = (lane >= ROPE_LANE0) & (lane < ROPE_LANE0 + MLA_ROPE)
    for hd in range(MLA_HEADS):
        sl = slice(hd * HEAD_PAD, (hd + 1) * HEAD_PAD)
        k_ref[:, sl] = jnp.where(in_rope, kr, kv[:, sl]).astype(BF16)
    v_ref[...] = kv[:, QPAD:].astype(BF16)


def _decomp(ckv, kr_pad, wukv):
    b, s, _ = ckv.shape
    return pl.pallas_call(
        _decomp_kernel,
        out_shape=[jax.ShapeDtypeStruct((b, s, QPAD), BF16), jax.ShapeDtypeStruct((b, s, VALL), BF16)],
        grid=(b,),
        in_specs=[pl.BlockSpec((None, s, KV_LORA), lambda i: (i, 0, 0)),
                  pl.BlockSpec((None, s, LANES), lambda i: (i, 0, 0)),
                  _const_spec(wukv.shape)],
        out_specs=[pl.BlockSpec((None, s, QPAD), lambda i: (i, 0, 0)),
                   pl.BlockSpec((None, s, VALL), lambda i: (i, 0, 0))],
        name="ctx_decompress",
        compiler_params=pltpu.CompilerParams(dimension_semantics=("arbitrary",)),
    )(ckv, kr_pad, wukv)


def _attn_kernel(*refs, has_ctx):
    if has_ctx:
        q_ref, kc_ref, vc_ref, k_ref, v_ref, o_ref = refs
    else:
        q_ref, k_ref, v_ref, o_ref = refs
    for hd in range(MLA_HEADS):
        sl = slice(hd * HEAD_PAD, (hd + 1) * HEAD_PAD)
        vs = slice(hd * MLA_V, (hd + 1) * MLA_V)
        qh = q_ref[:, sl]
        s = lax.dot_general(qh, k_ref[:, sl], _NT, preferred_element_type=F32)
        m = jnp.max(s, axis=-1, keepdims=True)
        if has_ctx:
            sc = lax.dot_general(qh, kc_ref[:, sl], _NT, preferred_element_type=F32)
            m = jnp.maximum(m, jnp.max(sc, axis=-1, keepdims=True))
        p = jnp.exp(s - m)
        l = jnp.sum(p, axis=-1, keepdims=True)
        o = jnp.dot(p.astype(BF16), v_ref[:, vs], preferred_element_type=F32)
        if has_ctx:
            pc = jnp.exp(sc - m)
            l = l + jnp.sum(pc, axis=-1, keepdims=True)
            o = o + jnp.dot(pc.astype(BF16), vc_ref[:, vs], preferred_element_type=F32)
        o_ref[:, vs] = (o / l).astype(BF16)


def _attention(q, k, v, ctx_kv, tq):
    b, t, _ = q.shape
    has_ctx = ctx_kv is not None
    in_specs = [pl.BlockSpec((None, tq, QPAD), lambda i, j: (i, j, 0))]
    args = [q]
    if has_ctx:
        kc, vc = ctx_kv
        s = kc.shape[1]
        in_specs += [pl.BlockSpec((None, s, QPAD), lambda i, j: (i, 0, 0)),
                     pl.BlockSpec((None, s, VALL), lambda i, j: (i, 0, 0))]
        args += [kc, vc]
    in_specs += [pl.BlockSpec((None, t, QPAD), lambda i, j: (i, 0, 0)),
                 pl.BlockSpec((None, t, VALL), lambda i, j: (i, 0, 0))]
    args += [k, v]
    return pl.pallas_call(
        functools.partial(_attn_kernel, has_ctx=has_ctx),
        out_shape=jax.ShapeDtypeStruct((b, t, VALL), BF16),
        grid=(b, t // tq),
        in_specs=in_specs,
        out_specs=pl.BlockSpec((None, tq, VALL), lambda i, j: (i, j, 0)),
        name="mla_attn_lat" if has_ctx else "mla_attn_ctx",
        compiler_params=pltpu.CompilerParams(dimension_semantics=("arbitrary", "arbitrary"),
                                             vmem_limit_bytes=VMEM_LIMIT),
    )(*args)


def _gla_kernel(gq_ref, gk_ref, gv_ref, gf_ref, gb_ref, go_ref, sf0_ref, sb0_ref, gn_ref,
                o_ref, sf_ref, sb_ref,
                cum_ref, db_ref, oacc_ref, *, n_chunks):
    ri = lax.broadcasted_iota(jnp.int32, (CHUNK, CHUNK), 0)
    ci = lax.broadcasted_iota(jnp.int32, (CHUNK, CHUNK), 1)
    lower = ri >= ci
    upper = ci >= ri
    g_refs = (gf_ref, gb_ref)
    chunk_row = lax.broadcasted_iota(jnp.int32, (LANES, GQK), 0)
    masks = (lower, upper)

    for d in range(2):
        tri = jnp.where(masks[d], 1.0, 0.0).astype(BF16)

        total_row = CHUNK - 1 if d == 0 else 0

        def cum_body(n, totals, d=d, tri=tri, total_row=total_row):
            rows = pl.ds(pl.multiple_of(n * CHUNK, CHUNK), CHUNK)
            g = g_refs[d][rows, :]
            hi = g.astype(BF16)
            lo = (g - hi.astype(F32)).astype(BF16)
            cum = (jnp.dot(tri, hi, preferred_element_type=F32)
                   + jnp.dot(tri, lo, preferred_element_type=F32))
            cum_ref[d, rows, :] = cum
            return jnp.where(chunk_row == n, cum[total_row:total_row + 1, :], totals)

        totals = lax.fori_loop(0, n_chunks, cum_body, jnp.zeros((LANES, GQK), F32))
        dec_t = jnp.exp(totals).T
        for n in range(n_chunks):
            db_ref[d, n] = jnp.broadcast_to(dec_t[:, n:n + 1], (GQK, GLA_DV))

    def chunk(n, d, state_ref):
        rows = pl.ds(pl.multiple_of(n * CHUNK, CHUNK), CHUNK)
        cum = cum_ref[d, rows, :]
        total = cum[CHUNK - 1:CHUNK, :] if d == 0 else cum[0:1, :]
        q = gq_ref[rows, :] * (GLA_DK ** -0.5)
        k = gk_ref[rows, :]
        v = gv_ref[rows, :]
        qe = (q * jnp.exp(cum)).astype(BF16)
        ke = (k * jnp.exp(-cum)).astype(BF16)
        kd = (k * jnp.exp(total - cum)).astype(BF16)
        outs = []
        for hd in range(GLA_HEADS):
            ks = slice(hd * GLA_DK, (hd + 1) * GLA_DK)
            vs = slice(hd * GLA_DV, (hd + 1) * GLA_DV)
            att = lax.dot_general(qe[:, ks], ke[:, ks], _NT, preferred_element_type=F32)
            att = jnp.where(masks[d], att, 0.0).astype(BF16)
            st = state_ref[hd]
            o = (jnp.dot(att, v[:, vs], preferred_element_type=F32)
                 + jnp.dot(qe[:, ks], st.astype(BF16), preferred_element_type=F32))
            upd = lax.dot_general(kd[:, ks], v[:, vs], _TN, preferred_element_type=F32)
            state_ref[hd] = db_ref[d, n, ks, :] * st + upd
            outs.append(o)
        return rows, outs

    sf_ref[...] = sf0_ref[...]
    sb_ref[...] = sb0_ref[...]

    def fwd_body(n, carry):
        rows, outs = chunk(n, 0, sf_ref)
        for hd in range(GLA_HEADS):
            oacc_ref[rows, hd * GLA_DV:(hd + 1) * GLA_DV] = outs[hd]
        return carry

    lax.fori_loop(0, n_chunks, fwd_body, 0)

    gn = gn_ref[...]

    def bwd_body(i, carry):
        rows, outs = chunk(n_chunks - 1 - i, 1, sb_ref)
        for hd in range(GLA_HEADS):
            vs = slice(hd * GLA_DV, (hd + 1) * GLA_DV)
            o = _rms(oacc_ref[rows, vs] + outs[hd], gn)
            go = go_ref[rows, vs]
            o_ref[rows, vs] = (o * (go * _sigmoid(go))).astype(BF16)
        return carry

    lax.fori_loop(0, n_chunks, bwd_body, 0)


def _gla(gq, gk, gv, gf, gb, go, sf0, sb0, gn):
    b, t, _ = gq.shape
    n_chunks = t // CHUNK
    mode = pl.Buffered(1) if t * GV * 4 >= (1 << 22) else None
    seq = lambda c: pl.BlockSpec((None, t, c), lambda i: (i, 0, 0), pipeline_mode=mode)
    st = pl.BlockSpec((None, GLA_HEADS, GLA_DK, GLA_DV), lambda i: (i, 0, 0, 0))
    return pl.pallas_call(
        functools.partial(_gla_kernel, n_chunks=n_chunks),
        out_shape=[jax.ShapeDtypeStruct((b, t, GV), BF16),
                   jax.ShapeDtypeStruct((b, GLA_HEADS, GLA_DK, GLA_DV), F32),
                   jax.ShapeDtypeStruct((b, GLA_HEADS, GLA_DK, GLA_DV), F32)],
        grid=(b,),
        in_specs=[seq(GQK), seq(GQK), seq(GV), seq(GQK), seq(GQK), seq(GV), st, st,
                  _const_spec(gn.shape)],
        out_specs=[pl.BlockSpec((None, t, GV), lambda i: (i, 0, 0)), st, st],
        scratch_shapes=[pltpu.VMEM((2, t, GQK), F32),
                        pltpu.VMEM((2, n_chunks, GQK, GLA_DV), F32),
                        pltpu.VMEM((t, GV), F32)],
        name="gla_%d" % t,
        compiler_params=pltpu.CompilerParams(dimension_semantics=("arbitrary",),
                                             vmem_limit_bytes=VMEM_LIMIT),
    )(gq, gk, gv, gf, gb, go, sf0, sb0, gn)


def _ffn_kernel(x_ref, at_ref, gl_ref, mod_ref, wout_ref, nf_ref, wfi_ref, wfo_ref, fn_ref,
                y_ref, act_ref):
    mix = (jnp.dot(at_ref[...], wout_ref[0:VALL, :], preferred_element_type=F32)
           + jnp.dot(gl_ref[...], wout_ref[VALL:, :], preferred_element_type=F32))
    x1 = x_ref[...] + mod_ref[2:3, :] * mix
    h2 = (_rms(x1, nf_ref[...]) * (1.0 + mod_ref[4:5, :]) + mod_ref[3:4, :]).astype(BF16)
    for j in range(N_FF_CHUNKS):
        ag = jnp.dot(h2, wfi_ref[:, j * 2 * FF_CHUNK:(j + 1) * 2 * FF_CHUNK],
                     preferred_element_type=F32)
        a = ag[:, :FF_CHUNK]
        act_ref[:, j * FF_CHUNK:(j + 1) * FF_CHUNK] = (a * _sigmoid(a) * ag[:, FF_CHUNK:]).astype(BF16)
    ff = jnp.dot(act_ref[...], wfo_ref[...], preferred_element_type=F32)
    x2 = x1 + mod_ref[5:6, :] * ff
    y_ref[...] = _rms(x2, fn_ref[...])


def _ffn(x2d, attn, gla, mod3, mod_row_fn, weights, tm, name):
    n_tok, d = x2d.shape
    wout, nf, wfi, wfo, fn = weights
    row = lambda i: (i, 0)
    return pl.pallas_call(
        _ffn_kernel,
        out_shape=jax.ShapeDtypeStruct((n_tok, d), F32),
        grid=(n_tok // tm,),
        in_specs=[pl.BlockSpec((tm, d), row),
                  pl.BlockSpec((tm, VALL), row),
                  pl.BlockSpec((tm, GV), row),
                  pl.BlockSpec((None, 6, d), lambda i: (mod_row_fn(i), 0, 0)),
                  _const_spec(wout.shape), _const_spec(nf.shape), _const_spec(wfi.shape),
                  _const_spec(wfo.shape), _const_spec(fn.shape)],
        out_specs=pl.BlockSpec((tm, d), row),
        scratch_shapes=[pltpu.VMEM((tm, D_FF), BF16)],
        name=name,
        compiler_params=pltpu.CompilerParams(dimension_semantics=("arbitrary",),
                                             vmem_limit_bytes=VMEM_LIMIT),
    )(x2d, attn, gla, mod3, wout, nf, wfi, wfo, fn)


def _rope_tables(n_tokens):
    t = np.arange(n_tokens)
    row = (t // GRID_W).astype(np.float32)
    col = (t % GRID_W).astype(np.float32)
    half = MLA_ROPE // 2
    inv = (np.float32(ROPE_BASE) ** (-np.arange(0, half, 2, dtype=np.float32) / np.float32(half))).astype(np.float32)
    ang_r = row[:, None] * inv
    ang_c = col[:, None] * inv
    ang = np.concatenate([ang_r, ang_r, ang_c, ang_c], axis=-1).astype(np.float32)
    cos, sin = np.cos(ang), np.sin(ang)
    first = (np.arange(MLA_ROPE) % half) < (half // 2)
    cos_t = np.ones((n_tokens, LANES), np.float32)
    sa_t = np.zeros((n_tokens, LANES), np.float32)
    sb_t = np.zeros((n_tokens, LANES), np.float32)
    cos_t[:, ROPE_LANE0:ROPE_LANE0 + MLA_ROPE] = cos
    sa_t[:, ROPE_LANE0:ROPE_LANE0 + MLA_ROPE] = np.where(first, -sin, 0.0)
    sb_t[:, ROPE_LANE0:ROPE_LANE0 + MLA_ROPE] = np.where(first, 0.0, sin)
    return jnp.asarray(cos_t), jnp.asarray(sa_t), jnp.asarray(sb_t)


def _prep_weights(w_in, w_uq, w_ukv, w_gate_f, b_gate_f, w_gate_b, b_gate_b, w_ffn_in):
    d = w_in.shape[0]
    o_kr = Q_LORA + KV_LORA
    o_gq = o_kr + MLA_ROPE
    o_gf = o_gq + 2 * GQK + GV
    o_go = o_gf + 2 * GATE_RANK
    zeros32 = jnp.zeros((d, 32), w_in.dtype)
    win = jnp.concatenate([w_in[:, :o_kr], w_in[:, o_gq:o_gf], w_in[:, o_go:],
                           w_in[:, o_gf:o_go], zeros32, w_in[:, o_kr:o_gq], zeros32], axis=1).astype(BF16)
    wuq = jnp.pad(w_uq.reshape(Q_LORA, MLA_HEADS, MLA_QK),
                  ((0, 0), (0, 0), (0, HEAD_PAD - MLA_QK))).reshape(Q_LORA, QPAD).astype(BF16)
    w3 = w_ukv.reshape(KV_LORA, MLA_HEADS, MLA_NOPE + MLA_V)
    wk = jnp.pad(w3[:, :, :MLA_NOPE], ((0, 0), (0, 0), (0, HEAD_PAD - MLA_NOPE))).reshape(KV_LORA, QPAD)
    wv = w3[:, :, MLA_NOPE:].reshape(KV_LORA, VALL)
    wukv = jnp.concatenate([wk, wv], axis=1).astype(BF16)
    wg = jnp.zeros((LANES, 2 * GQK), F32)
    wg = wg.at[0:GATE_RANK, 0:GQK].set(w_gate_f).at[GATE_RANK:2 * GATE_RANK, GQK:].set(w_gate_b).astype(BF16)
    bg = jnp.concatenate([b_gate_f, b_gate_b]).reshape(1, 2 * GQK)
    wa = w_ffn_in[:, :D_FF].reshape(d, N_FF_CHUNKS, FF_CHUNK)
    wgt = w_ffn_in[:, D_FF:].reshape(d, N_FF_CHUNKS, FF_CHUNK)
    wfi = jnp.concatenate([wa, wgt], axis=2).reshape(d, 2 * D_FF).astype(BF16)
    return win, wuq, wukv, wg, bg, wfi


def kernel(x_prompt, x_sample, cache_kv_latent, cache_k_rope, state_gla_fwd, state_gla_bwd, c, c_ctx, w_ada, b_ada, norm_attn, w_in, mla_q_norm, w_uq, mla_kv_norm, w_ukv, w_gate_f, b_gate_f, w_gate_b, b_gate_b, gla_norm, w_out, norm_ffn, w_ffn_in, w_ffn_out, final_norm):
    batch, seq, d = x_prompt.shape
    dec_batch, dec_seq, _ = x_sample.shape
    depth = w_ada.shape[0]
    assert depth == 1

    l = 0
    cond8 = jnp.zeros((8, d), F32).at[0].set(c_ctx).at[1:1 + dec_batch].set(c)
    mod3 = _ada(cond8, w_ada[l], b_ada[l]).reshape(8, 6, d)

    win, wuq, wukv, wg, bg, wfi = _prep_weights(w_in[l], w_uq[l], w_ukv[l], w_gate_f[l], b_gate_f[l],
                                                w_gate_b[l], b_gate_b[l], w_ffn_in[l])
    in_w = (norm_attn[l].reshape(1, d), win, mla_q_norm[l].reshape(1, Q_LORA), wuq,
            mla_kv_norm[l].reshape(1, KV_LORA), wukv, wg, bg)
    ffn_w = (w_out[l].astype(BF16), norm_ffn[l].reshape(1, d), wfi, w_ffn_out[l].astype(BF16),
             final_norm.reshape(1, d))
    gn = gla_norm[l].reshape(1, GLA_DV)
    tm = 256

    xp = x_prompt.reshape(batch * seq, d)
    (q, k, v, gq, gk, gv, gf, gb, go, ckv, kr) = _inproj(xp, mod3, lambda i: 0, in_w, None, tm, seq // tm)
    r3 = lambda a, b_, t: a.reshape(b_, t, a.shape[-1])
    attn = _attention(r3(q, batch, seq), r3(k, batch, seq), r3(v, batch, seq), None, min(seq, 256))
    zeros_state = jnp.zeros((batch, GLA_HEADS, GLA_DK, GLA_DV), F32)
    o_gla, sf, sb = _gla(r3(gq, batch, seq), r3(gk, batch, seq), r3(gv, batch, seq), r3(gf, batch, seq),
                         r3(gb, batch, seq), r3(go, batch, seq), zeros_state, zeros_state, gn)
    y_prompt = _ffn(xp, attn.reshape(batch * seq, VALL), o_gla.reshape(batch * seq, GV), mod3,
                    lambda i: 0, ffn_w, 512, "out_ffn_ctx").reshape(batch, seq, d)

    xs = x_sample.reshape(dec_batch * dec_seq, d)
    tiles = dec_seq // tm
    (q, k, v, gq, gk, gv, gf, gb, go) = _inproj(xs, mod3, lambda i: 1 + i // tiles, in_w,
                                                 _rope_tables(dec_seq), tm, tiles)
    kr_pad = jnp.pad(cache_k_rope[:, l], ((0, 0), (0, 0), (ROPE_LANE0, LANES - ROPE_LANE0 - MLA_ROPE)))
    kc, vc = _decomp(cache_kv_latent[:, l], kr_pad, wukv)
    attn = _attention(r3(q, dec_batch, dec_seq), r3(k, dec_batch, dec_seq), r3(v, dec_batch, dec_seq),
                      (kc, vc), 256)
    o_gla, _, _ = _gla(r3(gq, dec_batch, dec_seq), r3(gk, dec_batch, dec_seq), r3(gv, dec_batch, dec_seq),
                       r3(gf, dec_batch, dec_seq), r3(gb, dec_batch, dec_seq), r3(go, dec_batch, dec_seq),
                       state_gla_fwd[:, l].astype(F32), state_gla_bwd[:, l].astype(F32), gn)
    ftiles = dec_seq // 512
    y_sample = _ffn(xs, attn.reshape(dec_batch * dec_seq, VALL), o_gla.reshape(dec_batch * dec_seq, GV),
                    mod3, lambda i: 1 + i // ftiles, ffn_w, 512, "out_ffn_lat").reshape(dec_batch, dec_seq, d)

    new_kv_latent = ckv.reshape(batch, 1, seq, KV_LORA)
    new_k_rope = kr.reshape(batch, 1, seq, MLA_ROPE)
    new_state_fwd = sf.reshape(batch, 1, GLA_HEADS, GLA_DK, GLA_DV).astype(x_prompt.dtype)
    new_state_bwd = sb.reshape(batch, 1, GLA_HEADS, GLA_DK, GLA_DV).astype(x_prompt.dtype)
    return (y_prompt, y_sample, new_kv_latent, new_k_rope, new_state_fwd, new_state_bwd)
```

```python
import functools

import numpy as np
import jax
import jax.numpy as jnp
from jax import lax
from jax.experimental import pallas as pl
from jax.experimental.pallas import tpu as pltpu

F32 = jnp.float32
BF16 = jnp.bfloat16

GRID_W = 64
MLA_HEADS = 8
MLA_NOPE = 64
MLA_ROPE = 32
MLA_QK = MLA_NOPE + MLA_ROPE
MLA_V = 64
Q_LORA = 384
KV_LORA = 256
GLA_HEADS = 4
GLA_DK = 64
GLA_DV = 128
GATE_RANK = 16
GATE_NORM = 16.0
CHUNK = 64
D_FF = 2816
ROPE_BASE = 10000.0
EPS = 1e-6

LANES = 128
HEAD_PAD = LANES
ROPE_LANE0 = MLA_NOPE
GQK = GLA_HEADS * GLA_DK
GV = GLA_HEADS * GLA_DV
QPAD = MLA_HEADS * HEAD_PAD
VALL = MLA_HEADS * MLA_V

W_KR = Q_LORA + KV_LORA
W_GQ = W_KR + MLA_ROPE
W_GF = W_GQ + 2 * GQK + GV
W_GO = W_GF + 2 * GATE_RANK
W_COLS = W_GO + GV

Z_Q = 0
Z_KV = Z_Q + Q_LORA
Z_GQ = Z_KV + KV_LORA
Z_GK = Z_GQ + GQK
Z_GV = Z_GK + GQK
Z_GO = Z_GV + GV
Z_MISC = Z_GO + GV
Z_COLS = Z_MISC + LANES

FF_CHUNK = 256
N_FF_CHUNKS = D_FF // FF_CHUNK

GLA_TILE = 256
CHUNKS_PER_TILE = GLA_TILE // CHUNK

VMEM_LIMIT = 56 * 1024 * 1024

_NT = (((1,), (1,)), ((), ()))


def _rms(x, w):
    return x * lax.rsqrt(jnp.mean(x * x, axis=-1, keepdims=True) + EPS) * w


def _sigmoid(x):
    return 1.0 / (1.0 + jnp.exp(-x))


def _log_sigmoid(x):
    return jnp.minimum(x, 0.0) - jnp.log1p(jnp.exp(-jnp.abs(x)))


def _const_spec(shape):
    nd = len(shape)
    return pl.BlockSpec(shape, lambda *_: (0,) * nd, pipeline_mode=pl.Buffered(1))


def _mod_rows(mod_ref, r):
    return [mod_ref[k, pl.ds(r, 1), :] for k in range(6)]


def _ada_kernel(cond_ref, w_ref, b_ref, o_ref):
    c = cond_ref[...]
    s = (c * _sigmoid(c)).astype(BF16)
    o_ref[...] = jnp.dot(s, w_ref[...].astype(BF16), preferred_element_type=F32) + b_ref[...]


def _ada(cond8, w_ada, b_ada):
    d = w_ada.shape[0]
    n = w_ada.shape[1]
    return pl.pallas_call(
        _ada_kernel,
        out_shape=jax.ShapeDtypeStruct((n // d, 8, d), F32),
        grid=(n // d,),
        in_specs=[pl.BlockSpec((8, d), lambda j: (0, 0)),
                  pl.BlockSpec((d, d), lambda j: (0, j)),
                  pl.BlockSpec((1, d), lambda j: (0, j))],
        out_specs=pl.BlockSpec((None, 8, d), lambda j: (j, 0, 0)),
        name="ada_mod",
        compiler_params=pltpu.CompilerParams(dimension_semantics=("arbitrary",)),
    )(cond8, w_ada, b_ada.reshape(1, n))


def _prep_kernel(win_ref, wuq_ref, wukv_ref, wgf_ref, wgb_ref, win_o, wuq_o, wukv_o, wg_o):
    w = win_ref[...]
    rows = w.shape[0]
    win_o[:, Z_Q:Z_GQ] = w[:, 0:W_KR].astype(BF16)
    win_o[:, Z_GQ:Z_GO] = w[:, W_GQ:W_GF].astype(BF16)
    win_o[:, Z_GO:Z_MISC] = w[:, W_GO:W_COLS].astype(BF16)
    z32 = jnp.zeros((rows, 32), F32)
    misc = jnp.concatenate([w[:, W_GF:W_GO], z32, w[:, W_KR:W_GQ], z32], axis=1)
    win_o[:, Z_MISC:Z_COLS] = misc.astype(BF16)

    u = wuq_ref[...]
    zq = jnp.zeros((u.shape[0], HEAD_PAD - MLA_QK), F32)
    for hd in range(MLA_HEADS):
        blk = jnp.concatenate([u[:, hd * MLA_QK:(hd + 1) * MLA_QK], zq], axis=1)
        wuq_o[:, hd * HEAD_PAD:(hd + 1) * HEAD_PAD] = blk.astype(BF16)

    kv = wukv_ref[...]
    per = MLA_NOPE + MLA_V
    lane = lax.broadcasted_iota(jnp.int32, (kv.shape[0], per), 1)
    for hd in range(MLA_HEADS):
        blk = kv[:, hd * per:(hd + 1) * per]
        wukv_o[:, hd * HEAD_PAD:(hd + 1) * HEAD_PAD] = jnp.where(lane < MLA_NOPE, blk, 0.0).astype(BF16)
    for p in range(MLA_HEADS // 2):
        pair = jnp.concatenate([kv[:, (2 * p) * per + MLA_NOPE:(2 * p + 1) * per],
                                kv[:, (2 * p + 1) * per + MLA_NOPE:(2 * p + 2) * per]], axis=1)
        wukv_o[:, QPAD + p * LANES:QPAD + (p + 1) * LANES] = pair.astype(BF16)

    wg_o[...] = jnp.zeros(wg_o.shape, BF16)
    wg_o[0:GATE_RANK, 0:GQK] = wgf_ref[...].astype(BF16)
    wg_o[GATE_RANK:2 * GATE_RANK, GQK:2 * GQK] = wgb_ref[...].astype(BF16)


def _prep_in_weights(w_in, w_uq, w_ukv, w_gate_f, w_gate_b):
    d = w_in.shape[0]
    steps = 4
    rb = lambda r, c: pl.BlockSpec((r // steps, c), lambda i: (i, 0))
    full = lambda shape: pl.BlockSpec(shape, lambda i: (0, 0))
    return pl.pallas_call(
        _prep_kernel,
        out_shape=[jax.ShapeDtypeStruct((d, Z_COLS), BF16),
                   jax.ShapeDtypeStruct((Q_LORA, QPAD), BF16),
                   jax.ShapeDtypeStruct((KV_LORA, QPAD + VALL), BF16),
                   jax.ShapeDtypeStruct((LANES, 2 * GQK), BF16)],
        grid=(steps,),
        in_specs=[rb(d, W_COLS), rb(Q_LORA, MLA_HEADS * MLA_QK), rb(KV_LORA, QPAD),
                  full(w_gate_f.shape), full(w_gate_b.shape)],
        out_specs=[rb(d, Z_COLS), rb(Q_LORA, QPAD), rb(KV_LORA, QPAD + VALL),
                   full((LANES, 2 * GQK))],
        name="weight_prep",
        compiler_params=pltpu.CompilerParams(dimension_semantics=("arbitrary",)),
    )(w_in, w_uq, w_ukv, w_gate_f, w_gate_b)


def _inproj_kernel(*refs, latent, mod_row):
    (x_ref, mod_ref, nw_ref, win_ref, qn_ref, wuq_ref, kvn_ref, wukv_ref, wg_ref, bg_ref) = refs[:10]
    if latent:
        cos_ref, sa_ref, sb_ref = refs[10:13]
        outs = refs[13:]
    else:
        outs = refs[10:]
    q_ref, k_ref, v_ref, gq_ref, gk_ref, gv_ref, gf_ref, gb_ref, go_ref = outs[:9]

    sh1, sc1 = _mod_rows(mod_ref, mod_row(pl.program_id(0)))[:2]
    x = x_ref[...]
    h = _rms(x, nw_ref[...]) * (1.0 + sc1) + sh1
    z = jnp.dot(h.astype(BF16), win_ref[...], preferred_element_type=F32)

    qn = _rms(z[:, Z_Q:Z_Q + Q_LORA], qn_ref[...])
    q = jnp.dot(qn.astype(BF16), wuq_ref[...], preferred_element_type=F32)
    ckv = _rms(z[:, Z_KV:Z_KV + KV_LORA], kvn_ref[...])
    kv = jnp.dot(ckv.astype(BF16), wukv_ref[...], preferred_element_type=F32)
    misc = z[:, Z_MISC:Z_MISC + LANES]

    if latent:
        cos, sa, sb = cos_ref[...], sa_ref[...], sb_ref[...]

        def rope(t):
            return t * cos + pltpu.roll(t, LANES - 8, 1) * sa + pltpu.roll(t, 8, 1) * sb
    else:
        def rope(t):
            return t

    scale = MLA_QK ** -0.5
    lane = lax.broadcasted_iota(jnp.int32, misc.shape, 1)
    in_rope = (lane >= ROPE_LANE0) & (lane < ROPE_LANE0 + MLA_ROPE)
    krope = rope(misc)
    for hd in range(MLA_HEADS):
        sl = slice(hd * HEAD_PAD, (hd + 1) * HEAD_PAD)
        q_ref[:, sl] = (rope(q[:, sl]) * scale).astype(BF16)
        k_ref[:, sl] = jnp.where(in_rope, krope, kv[:, sl]).astype(BF16)
    v_ref[...] = kv[:, QPAD:].astype(BF16)

    gq_ref[...] = z[:, Z_GQ:Z_GQ + GQK]
    gk_ref[...] = z[:, Z_GK:Z_GK + GQK]
    gv_ref[...] = z[:, Z_GV:Z_GV + GV].astype(BF16)
    go_ref[...] = z[:, Z_GO:Z_GO + GV]
    gpre = jnp.dot(misc.astype(BF16), wg_ref[...], preferred_element_type=F32) + bg_ref[...]
    gate = _log_sigmoid(gpre) * (1.0 / GATE_NORM)
    gf_ref[...] = gate[:, :GQK]
    gb_ref[...] = gate[:, GQK:]

    if not latent:
        ckv_ref, kr_ref = outs[9:]
        ckv_ref[...] = ckv
        kr_ref[...] = misc[:, ROPE_LANE0:ROPE_LANE0 + MLA_ROPE]


def _inproj(x2d, mod, mod_row, weights, rope_tabs, tm, tiles_per_seq):
    n_tok, d = x2d.shape
    latent = rope_tabs is not None
    nw, win, qn, wuq, kvn, wukv, wg, bg = weights
    row = lambda i: (i, 0)
    in_specs = [pl.BlockSpec((tm, d), row), _const_spec(mod.shape),
                _const_spec(nw.shape), _const_spec(win.shape), _const_spec(qn.shape),
                _const_spec(wuq.shape), _const_spec(kvn.shape), _const_spec(wukv.shape),
                _const_spec(wg.shape), _const_spec(bg.shape)]
    args = [x2d, mod, nw, win, qn, wuq, kvn, wukv, wg, bg]
    if latent:
        tab = pl.BlockSpec((tm, LANES), lambda i: (i % tiles_per_seq, 0))
        in_specs += [tab, tab, tab]
        args += list(rope_tabs)
    out_cols = [(QPAD, BF16), (QPAD, BF16), (VALL, BF16), (GQK, F32), (GQK, F32), (GV, BF16),
                (GQK, F32), (GQK, F32), (GV, F32)]
    if not latent:
        out_cols += [(KV_LORA, F32), (MLA_ROPE, F32)]
    out_shape = [jax.ShapeDtypeStruct((n_tok, c), dt) for c, dt in out_cols]
    out_specs = [pl.BlockSpec((tm, c), row) for c, _ in out_cols]
    return pl.pallas_call(
        functools.partial(_inproj_kernel, latent=latent, mod_row=mod_row),
        out_shape=out_shape,
        grid=(n_tok // tm,),
        in_specs=in_specs,
        out_specs=out_specs,
        name="inproj_lat" if latent else "inproj_ctx",
        compiler_params=pltpu.CompilerParams(dimension_semantics=("arbitrary",),
                                             vmem_limit_bytes=VMEM_LIMIT),
    )(*args)


def _decomp_kernel(ckv_ref, kr_ref, wukv_ref, k_ref, v_ref):
    kv = jnp.dot(ckv_ref[...].astype(BF16), wukv_ref[...], preferred_element_type=F32)
    kr = kr_ref[...]
    lane = lax.broadcasted_iota(jnp.int32, kr.shape, 1)
    in_rope = (lane >= ROPE_LANE0) & (lane < ROPE_LANE0 + MLA_ROPE)
    for hd in range(MLA_HEADS):
        sl = slice(hd * HEAD_PAD, (hd + 1) * HEAD_PAD)
        k_ref[:, sl] = jnp.where(in_rope, kr, kv[:, sl]).astype(BF16)
    v_ref[...] = kv[:, QPAD:].astype(BF16)


def _decomp(ckv, kr_pad, wukv):
    b, s, _ = ckv.shape
    return pl.pallas_call(
        _decomp_kernel,
        out_shape=[jax.ShapeDtypeStruct((b, s, QPAD), BF16), jax.ShapeDtypeStruct((b, s, VALL), BF16)],
        grid=(b,),
        in_specs=[pl.BlockSpec((None, s, KV_LORA), lambda i: (i, 0, 0)),
                  pl.BlockSpec((None, s, LANES), lambda i: (i, 0, 0)),
                  _const_spec(wukv.shape)],
        out_specs=[pl.BlockSpec((None, s, QPAD), lambda i: (i, 0, 0)),
                   pl.BlockSpec((None, s, VALL), lambda i: (i, 0, 0))],
        name="ctx_decompress",
        compiler_params=pltpu.CompilerParams(dimension_semantics=("arbitrary",)),
    )(ckv, kr_pad, wukv)


def _attn_kernel(*refs, has_ctx):
    if has_ctx:
        q_ref, kc_ref, vc_ref, k_ref, v_ref, o_ref = refs
    else:
        q_ref, k_ref, v_ref, o_ref = refs
    for hd in range(MLA_HEADS):
        sl = slice(hd * HEAD_PAD, (hd + 1) * HEAD_PAD)
        vs = slice(hd * MLA_V, (hd + 1) * MLA_V)
        qh = q_ref[:, sl]
        s = lax.dot_general(qh, k_ref[:, sl], _NT, preferred_element_type=F32)
        m = jnp.max(s, axis=-1, keepdims=True)
        if has_ctx:
            sc = lax.dot_general(qh, kc_ref[:, sl], _NT, preferred_element_type=F32)
            m = jnp.maximum(m, jnp.max(sc, axis=-1, keepdims=True))
        p = jnp.exp(s - m)
        l = jnp.sum(p, axis=-1, keepdims=True)
        o = jnp.dot(p.astype(BF16), v_ref[:, vs], preferred_element_type=F32)
        if has_ctx:
            pc = jnp.exp(sc - m)
            l = l + jnp.sum(pc, axis=-1, keepdims=True)
            o = o + jnp.dot(pc.astype(BF16), vc_ref[:, vs], preferred_element_type=F32)
        o_ref[:, vs] = (o / l).astype(BF16)


def _attention(q, k, v, ctx_kv, tq):
    b, t, _ = q.shape
    has_ctx = ctx_kv is not None
    in_specs = [pl.BlockSpec((None, tq, QPAD), lambda i, j: (i, j, 0))]
    args = [q]
    if has_ctx:
        kc, vc = ctx_kv
        s = kc.shape[1]
        in_specs += [pl.BlockSpec((None, s, QPAD), lambda i, j: (i, 0, 0)),
                     pl.BlockSpec((None, s, VALL), lambda i, j: (i, 0, 0))]
        args += [kc, vc]
    in_specs += [pl.BlockSpec((None, t, QPAD), lambda i, j: (i, 0, 0)),
                 pl.BlockSpec((None, t, VALL), lambda i, j: (i, 0, 0))]
    args += [k, v]
    return pl.pallas_call(
        functools.partial(_attn_kernel, has_ctx=has_ctx),
        out_shape=jax.ShapeDtypeStruct((b, t, VALL), BF16),
        grid=(b, t // tq),
        in_specs=in_specs,
        out_specs=pl.BlockSpec((None, tq, VALL), lambda i, j: (i, j, 0)),
        name="mla_attn_lat" if has_ctx else "mla_attn_ctx",
        compiler_params=pltpu.CompilerParams(dimension_semantics=("arbitrary", "arbitrary"),
                                             vmem_limit_bytes=VMEM_LIMIT),
    )(*args)


def _gla_kernel(*refs, n_tiles, zero_init):
    gq_ref, gk_ref, gv_ref, gf_ref, gb_ref, go_ref = refs[:6]
    if zero_init:
        gn_ref, o_ref, sf_ref, sb_ref, cum_ref, db_ref, oacc_ref, bdqk_ref, tri_ref, hm_ref = refs[6:]
    else:
        (sf0_ref, sb0_ref, gn_ref, o_ref, sf_ref, sb_ref, cum_ref, db_ref, oacc_ref,
         bdqk_ref, tri_ref, hm_ref) = refs[6:]
    n_chunks = n_tiles * CHUNKS_PER_TILE
    g_refs = (gf_ref, gb_ref)
    state_refs = (sf_ref, sb_ref)

    ri = lax.broadcasted_iota(jnp.int32, (GLA_TILE, GLA_TILE), 0)
    ci = lax.broadcasted_iota(jnp.int32, (GLA_TILE, GLA_TILE), 1)
    same_chunk = (ri // CHUNK) == (ci // CHUNK)
    tile_tri = (same_chunk & (ri >= ci), same_chunk & (ci >= ri))
    chunk_row = lax.broadcasted_iota(jnp.int32, (LANES, GQK), 0)
    bdqk_ref[...] = jnp.where(same_chunk, 1.0, 0.0).astype(BF16)
    for d in range(2):
        tri_ref[d] = jnp.where(tile_tri[d], 1.0, 0.0)
    hm_ref[...] = jnp.where(
        lax.broadcasted_iota(jnp.int32, (GLA_HEADS * GLA_TILE, GQK), 0) // GLA_TILE
        == lax.broadcasted_iota(jnp.int32, (GLA_HEADS * GLA_TILE, GQK), 1) // GLA_DK, 1.0, 0.0).astype(BF16)

    def tile_rows(t):
        return pl.ds(pl.multiple_of(t * GLA_TILE, GLA_TILE), GLA_TILE)

    def total_row(c, d):
        return c * CHUNK + (CHUNK - 1 if d == 0 else 0)

    for d in range(2):
        tri = tri_ref[d].astype(BF16)

        def cum_body(t, totals, d=d, tri=tri):
            rows = tile_rows(t)
            g = g_refs[d][rows, :]
            hi = g.astype(BF16)
            lo = (g - hi.astype(F32)).astype(BF16)
            cum = (jnp.dot(tri, hi, preferred_element_type=F32)
                   + jnp.dot(tri, lo, preferred_element_type=F32))
            cum_ref[d, rows, :] = cum
            for c in range(CHUNKS_PER_TILE):
                r = total_row(c, d)
                totals = jnp.where(chunk_row == t * CHUNKS_PER_TILE + c, cum[r:r + 1, :], totals)
            return totals

        totals = lax.fori_loop(0, n_tiles, cum_body, jnp.zeros((LANES, GQK), F32))
        dec_t = jnp.exp(totals).T
        for n in range(n_chunks):
            db_ref[d, n] = jnp.broadcast_to(dec_t[:, n:n + 1], (GQK, GLA_DV))

    def tile_dir(t, d):
        rows = tile_rows(t)
        cum = cum_ref[d, rows, :]
        q = gq_ref[rows, :] * (GLA_DK ** -0.5)
        k = gk_ref[rows, :]
        v = gv_ref[rows, :]
        tot = jnp.concatenate(
            [jnp.broadcast_to(cum[total_row(c, d):total_row(c, d) + 1, :], (CHUNK, GQK))
             for c in range(CHUNKS_PER_TILE)], axis=0)
        qe = (q * jnp.exp(cum)).astype(BF16)
        ke = (k * jnp.exp(-cum)).astype(BF16)
        kd_t = (k * jnp.exp(tot - cum)).T.astype(BF16)
        bd_qk = bdqk_ref[...] > 0
        tri = tri_ref[d] > 0

        qm = jnp.where(hm_ref[...] > 0, jnp.tile(qe, (GLA_HEADS, 1)), 0.0)
        att = lax.dot_general(qm, ke, _NT, preferred_element_type=F32)

        intra, upd = [], []
        for hd in range(GLA_HEADS):
            vh = v[:, hd * GLA_DV:(hd + 1) * GLA_DV]
            a_h = jnp.where(tri, att[hd * GLA_TILE:(hd + 1) * GLA_TILE, :], 0.0).astype(BF16)
            intra.append(jnp.dot(a_h, vh, preferred_element_type=F32))
            kd_h = jnp.tile(kd_t[hd * GLA_DK:(hd + 1) * GLA_DK, :], (CHUNKS_PER_TILE, 1))
            upd.append(jnp.dot(jnp.where(bd_qk, kd_h, 0.0), vh, preferred_element_type=F32))

        state = [state_refs[d][hd] for hd in range(GLA_HEADS)]
        order = range(CHUNKS_PER_TILE) if d == 0 else range(CHUNKS_PER_TILE - 1, -1, -1)
        seen = {}
        for c in order:
            seen[c] = jnp.concatenate(state, axis=0).astype(BF16)
            decay = db_ref[d, t * CHUNKS_PER_TILE + c]
            for hd in range(GLA_HEADS):
                ks = slice(hd * GLA_DK, (hd + 1) * GLA_DK)
                state[hd] = decay[ks, :] * state[hd] + upd[hd][c * CHUNK:(c + 1) * CHUNK, :]
        for hd in range(GLA_HEADS):
            state_refs[d][hd] = state[hd]

        for c in range(CHUNKS_PER_TILE):
            cr = slice(c * CHUNK, (c + 1) * CHUNK)
            q_c = jnp.concatenate([qm[hd * GLA_TILE + c * CHUNK:hd * GLA_TILE + (c + 1) * CHUNK, :]
                                   for hd in range(GLA_HEADS)], axis=0)
            inter = jnp.dot(q_c, seen[c], preferred_element_type=F32)
            o = jnp.concatenate([intra[hd][cr, :] + inter[hd * CHUNK:(hd + 1) * CHUNK, :]
                                 for hd in range(GLA_HEADS)], axis=1)
            oacc_ref[d, pl.ds(pl.multiple_of(t * GLA_TILE + c * CHUNK, CHUNK), CHUNK), :] = o

    if zero_init:
        sf_ref[...] = jnp.zeros(sf_ref.shape, F32)
        sb_ref[...] = jnp.zeros(sb_ref.shape, F32)
    else:
        sf_ref[...] = sf0_ref[...]
        sb_ref[...] = sb0_ref[...]

    def main_body(t, carry):
        tile_dir(t, 0)
        tile_dir(n_tiles - 1 - t, 1)
        return carry

    lax.fori_loop(0, n_tiles, main_body, 0)

    gn = gn_ref[...]

    def epilogue_body(t, carry):
        rows = tile_rows(t)
        for hd in range(GLA_HEADS):
            vs = slice(hd * GLA_DV, (hd + 1) * GLA_DV)
            o = _rms(oacc_ref[0, rows, vs] + oacc_ref[1, rows, vs], gn)
            go = go_ref[rows, vs]
            o_ref[rows, vs] = (o * (go * _sigmoid(go))).astype(BF16)
        return carry

    lax.fori_loop(0, n_tiles, epilogue_body, 0)


def _gla(gq, gk, gv, gf, gb, go, init_states, gn):
    b, t, _ = gq.shape
    n_tiles = t // GLA_TILE
    zero_init = init_states is None
    mode = pl.Buffered(1) if t * GV * 4 >= (1 << 22) else None
    seq = lambda c: pl.BlockSpec((None, t, c), lambda i: (i, 0, 0), pipeline_mode=mode)
    st = pl.BlockSpec((None, GLA_HEADS, GLA_DK, GLA_DV), lambda i: (i, 0, 0, 0))
    in_specs = [seq(GQK), seq(GQK), seq(GV), seq(GQK), seq(GQK), seq(GV)]
    args = [gq, gk, gv, gf, gb, go]
    if not zero_init:
        in_specs += [st, st]
        args += list(init_states)
    in_specs.append(_const_spec(gn.shape))
    args.append(gn)
    return pl.pallas_call(
        functools.partial(_gla_kernel, n_tiles=n_tiles, zero_init=zero_init),
        out_shape=[jax.ShapeDtypeStruct((b, t, GV), BF16),
                   jax.ShapeDtypeStruct((b, GLA_HEADS, GLA_DK, GLA_DV), F32),
                   jax.ShapeDtypeStruct((b, GLA_HEADS, GLA_DK, GLA_DV), F32)],
        grid=(b,),
        in_specs=in_specs,
        out_specs=[pl.BlockSpec((None, t, GV), lambda i: (i, 0, 0)), st, st],
        scratch_shapes=[pltpu.VMEM((2, t, GQK), F32),
                        pltpu.VMEM((2, n_tiles * CHUNKS_PER_TILE, GQK, GLA_DV), F32),
                        pltpu.VMEM((2, t, GV), F32),
                        pltpu.VMEM((GQK, GLA_TILE), BF16),
                        pltpu.VMEM((2, GLA_TILE, GLA_TILE), F32),
                        pltpu.VMEM((GLA_HEADS * GLA_TILE, GQK), BF16)],
        name="gla_%d" % t,
        compiler_params=pltpu.CompilerParams(dimension_semantics=("arbitrary",),
                                             vmem_limit_bytes=VMEM_LIMIT),
    )(*args)


def _ffn_kernel(x_ref, at_ref, gl_ref, mod_ref, wout_ref, nf_ref, wfi_ref, wfo_ref, fn_ref,
                y_ref, act_ref, *, mod_row):
    _, _, gt1, sh2, sc2, gt2 = _mod_rows(mod_ref, mod_row(pl.program_id(0)))
    mix = (jnp.dot(at_ref[...], wout_ref[0:VALL, :], preferred_element_type=F32)
           + jnp.dot(gl_ref[...], wout_ref[VALL:, :], preferred_element_type=F32))
    x1 = x_ref[...] + gt1 * mix
    h2 = (_rms(x1, nf_ref[...]) * (1.0 + sc2) + sh2).astype(BF16)
    for j in range(N_FF_CHUNKS):
        cs = slice(j * FF_CHUNK, (j + 1) * FF_CHUNK)
        a = jnp.dot(h2, wfi_ref[:, cs], preferred_element_type=F32)
        g = jnp.dot(h2, wfi_ref[:, D_FF + j * FF_CHUNK:D_FF + (j + 1) * FF_CHUNK],
                    preferred_element_type=F32)
        act_ref[:, cs] = (a * _sigmoid(a) * g).astype(BF16)
    ff = jnp.dot(act_ref[...], wfo_ref[...], preferred_element_type=F32)
    x2 = x1 + gt2 * ff
    y_ref[...] = _rms(x2, fn_ref[...])


def _ffn(x2d, attn, gla, mod, mod_row, weights, tm, name):
    n_tok, d = x2d.shape
    wout, nf, wfi, wfo, fn = weights
    row = lambda i: (i, 0)
    return pl.pallas_call(
        functools.partial(_ffn_kernel, mod_row=mod_row),
        out_shape=jax.ShapeDtypeStruct((n_tok, d), F32),
        grid=(n_tok // tm,),
        in_specs=[pl.BlockSpec((tm, d), row),
                  pl.BlockSpec((tm, VALL), row),
                  pl.BlockSpec((tm, GV), row),
                  _const_spec(mod.shape),
                  _const_spec(wout.shape), _const_spec(nf.shape), _const_spec(wfi.shape),
                  _const_spec(wfo.shape), _const_spec(fn.shape)],
        out_specs=pl.BlockSpec((tm, d), row),
        scratch_shapes=[pltpu.VMEM((tm, D_FF), BF16)],
        name=name,
        compiler_params=pltpu.CompilerParams(dimension_semantics=("arbitrary",),
                                             vmem_limit_bytes=VMEM_LIMIT),
    )(x2d, attn, gla, mod, wout, nf, wfi, wfo, fn)


def _rope_tables(n_tokens):
    t = np.arange(n_tokens)
    row = (t // GRID_W).astype(np.float32)
    col = (t % GRID_W).astype(np.float32)
    half = MLA_ROPE // 2
    inv = (np.float32(ROPE_BASE) ** (-np.arange(0, half, 2, dtype=np.float32) / np.float32(half))).astype(np.float32)
    ang_r = row[:, None] * inv
    ang_c = col[:, None] * inv
    ang = np.concatenate([ang_r, ang_r, ang_c, ang_c], axis=-1).astype(np.float32)
    cos, sin = np.cos(ang), np.sin(ang)
    first = (np.arange(MLA_ROPE) % half) < (half // 2)
    cos_t = np.ones((n_tokens, LANES), np.float32)
    sa_t = np.zeros((n_tokens, LANES), np.float32)
    sb_t = np.zeros((n_tokens, LANES), np.float32)
    cos_t[:, ROPE_LANE0:ROPE_LANE0 + MLA_ROPE] = cos
    sa_t[:, ROPE_LANE0:ROPE_LANE0 + MLA_ROPE] = np.where(first, -sin, 0.0)
    sb_t[:, ROPE_LANE0:ROPE_LANE0 + MLA_ROPE] = np.where(first, 0.0, sin)
    return jnp.asarray(cos_t), jnp.asarray(sa_t), jnp.asarray(sb_t)


def kernel(x_prompt, x_sample, cache_kv_latent, cache_k_rope, state_gla_fwd, state_gla_bwd, c, c_ctx, w_ada, b_ada, norm_attn, w_in, mla_q_norm, w_uq, mla_kv_norm, w_ukv, w_gate_f, b_gate_f, w_gate_b, b_gate_b, gla_norm, w_out, norm_ffn, w_ffn_in, w_ffn_out, final_norm):
    batch, seq, d = x_prompt.shape
    dec_batch, dec_seq, _ = x_sample.shape
    assert w_ada.shape[0] == 1 and w_in.shape[-1] == W_COLS and w_ffn_in.shape[-1] == 2 * D_FF
    l = 0

    cond8 = jnp.concatenate([c_ctx[None, :], c, jnp.zeros((8 - 1 - dec_batch, d), F32)], axis=0)
    mod = _ada(cond8, w_ada[l], b_ada[l])

    win, wuq, wukv, wg = _prep_in_weights(w_in[l], w_uq[l], w_ukv[l], w_gate_f[l], w_gate_b[l])
    bg = jnp.concatenate([b_gate_f[l], b_gate_b[l]]).reshape(1, 2 * GQK)
    in_w = (norm_attn[l].reshape(1, d), win, mla_q_norm[l].reshape(1, Q_LORA), wuq,
            mla_kv_norm[l].reshape(1, KV_LORA), wukv, wg, bg)
    ffn_w = (w_out[l].astype(BF16), norm_ffn[l].reshape(1, d), w_ffn_in[l].astype(BF16),
             w_ffn_out[l].astype(BF16), final_norm.reshape(1, d))
    gn = gla_norm[l].reshape(1, GLA_DV)
    tm, tm_ffn = 256, 512
    r3 = lambda a, b_, t: a.reshape(b_, t, a.shape[-1])

    xp = x_prompt.reshape(batch * seq, d)
    (q, k, v, gq, gk, gv, gf, gb, go, ckv, kr) = _inproj(xp, mod, lambda i: 0, in_w, None, tm, seq // tm)
    attn = _attention(r3(q, batch, seq), r3(k, batch, seq), r3(v, batch, seq), None, min(seq, 256))
    o_gla, sf, sb = _gla(r3(gq, batch, seq), r3(gk, batch, seq), r3(gv, batch, seq), r3(gf, batch, seq),
                         r3(gb, batch, seq), r3(go, batch, seq), None, gn)
    y_prompt = _ffn(xp, attn.reshape(batch * seq, VALL), o_gla.reshape(batch * seq, GV), mod,
                    lambda i: 0, ffn_w, tm_ffn, "out_ffn_ctx").reshape(batch, seq, d)

    xs = x_sample.reshape(dec_batch * dec_seq, d)
    tiles = dec_seq // tm
    (q, k, v, gq, gk, gv, gf, gb, go) = _inproj(xs, mod, lambda i: 1 + i // tiles, in_w,
                                                 _rope_tables(dec_seq), tm, tiles)
    kr_pad = jnp.pad(cache_k_rope[:, l], ((0, 0), (0, 0), (ROPE_LANE0, LANES - ROPE_LANE0 - MLA_ROPE)))
    kc, vc = _decomp(cache_kv_latent[:, l], kr_pad, wukv)
    attn = _attention(r3(q, dec_batch, dec_seq), r3(k, dec_batch, dec_seq), r3(v, dec_batch, dec_seq),
                      (kc, vc), 256)
    o_gla, _, _ = _gla(r3(gq, dec_batch, dec_seq), r3(gk, dec_batch, dec_seq), r3(gv, dec_batch, dec_seq),
                       r3(gf, dec_batch, dec_seq), r3(gb, dec_batch, dec_seq), r3(go, dec_batch, dec_seq),
                       (state_gla_fwd[:, l].astype(F32), state_gla_bwd[:, l].astype(F32)), gn)
    ftiles = dec_seq // tm_ffn
    y_sample = _ffn(xs, attn.reshape(dec_batch * dec_seq, VALL), o_gla.reshape(dec_batch * dec_seq, GV),
                    mod, lambda i: 1 + i // ftiles, ffn_w, tm_ffn, "out_ffn_lat").reshape(dec_batch, dec_seq, d)

    new_kv_latent = ckv.reshape(batch, 1, seq, KV_LORA)
    new_k_rope = kr.reshape(batch, 1, seq, MLA_ROPE)
    new_state_fwd = sf.reshape(batch, 1, GLA_HEADS, GLA_DK, GLA_DV).astype(x_prompt.dtype)
    new_state_bwd = sb.reshape(batch, 1, GLA_HEADS, GLA_DK, GLA_DV).astype(x_prompt.dtype)
    return (y_prompt, y_sample, new_kv_latent, new_k_rope, new_state_fwd, new_state_bwd)
```

```python
import functools

import numpy as np
import jax
import jax.numpy as jnp
from jax import lax
from jax.experimental import pallas as pl
from jax.experimental.pallas import tpu as pltpu

F32 = jnp.float32
BF16 = jnp.bfloat16

GRID_W = 64
MLA_HEADS = 8
MLA_NOPE = 64
MLA_ROPE = 32
MLA_QK = MLA_NOPE + MLA_ROPE
MLA_V = 64
Q_LORA = 384
KV_LORA = 256
GLA_HEADS = 4
GLA_DK = 64
GLA_DV = 128
GATE_RANK = 16
GATE_NORM = 16.0
CHUNK = 64
D_FF = 2816
ROPE_BASE = 10000.0
EPS = 1e-6
LOG2_E = 1.4426950408889634

LANES = 128
HEAD_PAD = LANES
ROPE_LANE0 = MLA_NOPE
GQK = GLA_HEADS * GLA_DK
GV = GLA_HEADS * GLA_DV
QPAD = MLA_HEADS * HEAD_PAD
VALL = MLA_HEADS * MLA_V
ONES_ROWS = 16
KEY_BLOCK = 256

W_KR = Q_LORA + KV_LORA
W_GQ = W_KR + MLA_ROPE
W_GF = W_GQ + 2 * GQK + GV
W_GO = W_GF + 2 * GATE_RANK
W_COLS = W_GO + GV

Z_Q = 0
Z_KV = Z_Q + Q_LORA
Z_GQ = Z_KV + KV_LORA
Z_GK = Z_GQ + GQK
Z_GV = Z_GK + GQK
Z_GO = Z_GV + GV
Z_MISC = Z_GO + GV
Z_COLS = Z_MISC + LANES

FF_CHUNK = 256
N_FF_CHUNKS = D_FF // FF_CHUNK

GLA_TILE = 256
CHUNKS_PER_TILE = GLA_TILE // CHUNK

VMEM_LIMIT = 56 * 1024 * 1024

_NT = (((1,), (1,)), ((), ()))


def _rms(x, w):
    return x * lax.rsqrt(jnp.mean(x * x, axis=-1, keepdims=True) + EPS) * w


def _sigmoid(x):
    return 1.0 / (1.0 + jnp.exp(-x))


def _log_sigmoid(x):
    return jnp.minimum(x, 0.0) - jnp.log1p(jnp.exp(-jnp.abs(x)))


def _const_spec(shape):
    nd = len(shape)
    return pl.BlockSpec(shape, lambda *_: (0,) * nd, pipeline_mode=pl.Buffered(1))


def _mod_rows(mod_ref, r):
    return [mod_ref[k, pl.ds(r, 1), :] for k in range(6)]


def _ada_kernel(cond_ref, w_ref, b_ref, o_ref):
    c = cond_ref[...]
    s = (c * _sigmoid(c)).astype(BF16)
    o_ref[...] = jnp.dot(s, w_ref[...].astype(BF16), preferred_element_type=F32) + b_ref[...]


def _ada(cond8, w_ada, b_ada):
    d = w_ada.shape[0]
    n = w_ada.shape[1]
    return pl.pallas_call(
        _ada_kernel,
        out_shape=jax.ShapeDtypeStruct((n // d, 8, d), F32),
        grid=(n // d,),
        in_specs=[pl.BlockSpec((8, d), lambda j: (0, 0)),
                  pl.BlockSpec((d, d), lambda j: (0, j)),
                  pl.BlockSpec((1, d), lambda j: (0, j))],
        out_specs=pl.BlockSpec((None, 8, d), lambda j: (j, 0, 0)),
        name="ada_mod",
        compiler_params=pltpu.CompilerParams(dimension_semantics=("arbitrary",)),
    )(cond8, w_ada, b_ada.reshape(1, n))


def _prep_kernel(wint_ref, wuq_ref, wukv_ref, wgf_ref, wgb_ref, win_o, wuq_o, wk_o, wvt_o, wg_o):
    cols = wint_ref.shape[1]
    for dst, src, n in ((Z_Q, 0, W_KR), (Z_GQ, W_GQ, W_GF - W_GQ), (Z_GO, W_GO, GV)):
        win_o[:, dst:dst + n] = wint_ref[src:src + n, :].T.astype(BF16)
    z32 = jnp.zeros((32, cols), F32)
    misc_t = jnp.concatenate([wint_ref[W_GF:W_GO, :], z32, wint_ref[W_KR:W_GQ, :], z32], axis=0)
    win_o[:, Z_MISC:Z_COLS] = misc_t.T.astype(BF16)

    u = wuq_ref[...]
    zq = jnp.zeros((u.shape[0], HEAD_PAD - MLA_QK), F32)
    for hd in range(MLA_HEADS):
        blk = jnp.concatenate([u[:, hd * MLA_QK:(hd + 1) * MLA_QK], zq], axis=1)
        wuq_o[:, hd * HEAD_PAD:(hd + 1) * HEAD_PAD] = blk.astype(BF16)

    @pl.when(pl.program_id(0) == 0)
    def _():
        kv = wukv_ref[...]
        per = MLA_NOPE + MLA_V
        lane = lax.broadcasted_iota(jnp.int32, (kv.shape[0], per), 1)
        for hd in range(MLA_HEADS):
            blk = kv[:, hd * per:(hd + 1) * per]
            wk_o[:, hd * HEAD_PAD:(hd + 1) * HEAD_PAD] = jnp.where(lane < MLA_NOPE, blk, 0.0).astype(BF16)
        wv = jnp.concatenate([kv[:, hd * per + MLA_NOPE:(hd + 1) * per] for hd in range(MLA_HEADS)], axis=1)
        wvt_o[...] = wv.T.astype(BF16)

        wg_o[...] = jnp.zeros(wg_o.shape, BF16)
        wg_o[0:GATE_RANK, 0:GQK] = wgf_ref[...].astype(BF16)
        wg_o[GATE_RANK:2 * GATE_RANK, GQK:2 * GQK] = wgb_ref[...].astype(BF16)


def _prep_in_weights(w_in, w_uq, w_ukv, w_gate_f, w_gate_b):
    d = w_in.shape[1]
    steps = 4
    w_in_t = jnp.swapaxes(w_in, 1, 2)
    rb3 = lambda r, c: pl.BlockSpec((None, r // steps, c), lambda i: (0, i, 0))
    rb = lambda r, c: pl.BlockSpec((r // steps, c), lambda i: (i, 0))
    full3 = lambda shape: pl.BlockSpec((None,) + tuple(shape[1:]), lambda i: (0, 0, 0))
    full = lambda shape: pl.BlockSpec(shape, lambda i: (0, 0))
    return pl.pallas_call(
        _prep_kernel,
        out_shape=[jax.ShapeDtypeStruct((d, Z_COLS), BF16),
                   jax.ShapeDtypeStruct((Q_LORA, QPAD), BF16),
                   jax.ShapeDtypeStruct((KV_LORA, QPAD), BF16),
                   jax.ShapeDtypeStruct((VALL, KV_LORA), BF16),
                   jax.ShapeDtypeStruct((LANES, 2 * GQK), BF16)],
        grid=(steps,),
        in_specs=[pl.BlockSpec((None, W_COLS, d // steps), lambda i: (0, 0, i)),
                  rb3(Q_LORA, MLA_HEADS * MLA_QK), full3(w_ukv.shape),
                  full3(w_gate_f.shape), full3(w_gate_b.shape)],
        out_specs=[rb(d, Z_COLS), rb(Q_LORA, QPAD), full((KV_LORA, QPAD)), full((VALL, KV_LORA)),
                   full((LANES, 2 * GQK))],
        name="weight_prep",
        compiler_params=pltpu.CompilerParams(dimension_semantics=("arbitrary",)),
    )(w_in_t, w_uq, w_ukv, w_gate_f, w_gate_b)


def _inproj_kernel(*refs, latent, mod_row):
    (x_ref, mod_ref, nw_ref, win_ref, qn_ref, wuq_ref, kvn_ref, wk_ref, wvt_ref, wg_ref, bg_ref) = refs[:11]
    if latent:
        cos_ref, sa_ref, sb_ref = refs[11:14]
        outs = refs[14:]
    else:
        outs = refs[11:]
    q_ref, k_ref, vt_ref, gq_ref, gk_ref, gv_ref, gf_ref, gb_ref, go_ref = outs[:9]

    sh1, sc1 = _mod_rows(mod_ref, mod_row(pl.program_id(0)))[:2]
    x = x_ref[...]
    h = _rms(x, nw_ref[...]) * (1.0 + sc1) + sh1
    z = jnp.dot(h.astype(BF16), win_ref[...], preferred_element_type=F32)

    qn = _rms(z[:, Z_Q:Z_Q + Q_LORA], qn_ref[...])
    q = jnp.dot(qn.astype(BF16), wuq_ref[...], preferred_element_type=F32)
    ckv = _rms(z[:, Z_KV:Z_KV + KV_LORA], kvn_ref[...])
    ckv_b = ckv.astype(BF16)
    kn = jnp.dot(ckv_b, wk_ref[...], preferred_element_type=F32)
    vt_ref[...] = lax.dot_general(wvt_ref[...], ckv_b, _NT,
                                  preferred_element_type=F32).astype(BF16)
    misc = z[:, Z_MISC:Z_MISC + LANES]

    if latent:
        cos, sa, sb = cos_ref[...], sa_ref[...], sb_ref[...]

        def rope(t):
            return t * cos + pltpu.roll(t, LANES - 8, 1) * sa + pltpu.roll(t, 8, 1) * sb
    else:
        def rope(t):
            return t

    scale = MLA_QK ** -0.5 * LOG2_E
    lane = lax.broadcasted_iota(jnp.int32, misc.shape, 1)
    in_rope = (lane >= ROPE_LANE0) & (lane < ROPE_LANE0 + MLA_ROPE)
    krope = rope(misc)
    for hd in range(MLA_HEADS):
        sl = slice(hd * HEAD_PAD, (hd + 1) * HEAD_PAD)
        q_ref[:, sl] = (rope(q[:, sl]) * scale).astype(BF16)
        k_ref[:, sl] = jnp.where(in_rope, krope, kn[:, sl]).astype(BF16)

    gq_ref[...] = z[:, Z_GQ:Z_GQ + GQK]
    gk_ref[...] = z[:, Z_GK:Z_GK + GQK]
    gv_ref[...] = z[:, Z_GV:Z_GV + GV].astype(BF16)
    go_ref[...] = z[:, Z_GO:Z_GO + GV]
    gpre = jnp.dot(misc.astype(BF16), wg_ref[...], preferred_element_type=F32) + bg_ref[...]
    gate = _log_sigmoid(gpre) * (1.0 / GATE_NORM)
    gf_ref[...] = gate[:, :GQK]
    gb_ref[...] = gate[:, GQK:]

    if not latent:
        ckv_ref, kr_ref = outs[9:]
        ckv_ref[...] = ckv
        kr_ref[...] = misc[:, ROPE_LANE0:ROPE_LANE0 + MLA_ROPE]


def _inproj(x2d, mod, mod_row, weights, rope_tabs, tm, tiles_per_seq):
    n_tok, d = x2d.shape
    latent = rope_tabs is not None
    nw, win, qn, wuq, kvn, wk, wvt, wg, bg = weights
    row = lambda i: (i, 0)
    in_specs = [pl.BlockSpec((tm, d), row), _const_spec(mod.shape),
                _const_spec(nw.shape), _const_spec(win.shape), _const_spec(qn.shape),
                _const_spec(wuq.shape), _const_spec(kvn.shape), _const_spec(wk.shape),
                _const_spec(wvt.shape), _const_spec(wg.shape), _const_spec(bg.shape)]
    args = [x2d, mod, nw, win, qn, wuq, kvn, wk, wvt, wg, bg]
    if latent:
        tab = pl.BlockSpec((tm, LANES), lambda i: (i % tiles_per_seq, 0))
        in_specs += [tab, tab, tab]
        args += list(rope_tabs)
    out_cols = [(QPAD, BF16), (QPAD, BF16), None, (GQK, F32), (GQK, F32), (GV, BF16),
                (GQK, F32), (GQK, F32), (GV, F32)]
    if not latent:
        out_cols += [(KV_LORA, F32), (MLA_ROPE, F32)]
    out_shape = [jax.ShapeDtypeStruct((n_tok, oc[0]), oc[1]) if oc else
                 jax.ShapeDtypeStruct((VALL, n_tok), BF16) for oc in out_cols]
    out_specs = [pl.BlockSpec((tm, oc[0]), row) if oc else
                 pl.BlockSpec((VALL, tm), lambda i: (0, i)) for oc in out_cols]
    return pl.pallas_call(
        functools.partial(_inproj_kernel, latent=latent, mod_row=mod_row),
        out_shape=out_shape,
        grid=(n_tok // tm,),
        in_specs=in_specs,
        out_specs=out_specs,
        name="inproj_lat" if latent else "inproj_ctx",
        compiler_params=pltpu.CompilerParams(dimension_semantics=("arbitrary",),
                                             vmem_limit_bytes=VMEM_LIMIT),
    )(*args)


def _decomp_kernel(ckv_ref, kr_ref, wk_ref, wvt_ref, k_ref, vt_ref):
    ckv_b = ckv_ref[...].astype(BF16)
    kn = jnp.dot(ckv_b, wk_ref[...], preferred_element_type=F32)
    kr = kr_ref[...]
    lane = lax.broadcasted_iota(jnp.int32, kr.shape, 1)
    in_rope = (lane >= ROPE_LANE0) & (lane < ROPE_LANE0 + MLA_ROPE)
    for hd in range(MLA_HEADS):
        sl = slice(hd * HEAD_PAD, (hd + 1) * HEAD_PAD)
        k_ref[:, sl] = jnp.where(in_rope, kr, kn[:, sl]).astype(BF16)
    vt_ref[...] = lax.dot_general(wvt_ref[...], ckv_b, _NT, preferred_element_type=F32).astype(BF16)


def _decomp(ckv, kr_pad, wk, wvt):
    b, s, _ = ckv.shape
    return pl.pallas_call(
        _decomp_kernel,
        out_shape=[jax.ShapeDtypeStruct((b, s, QPAD), BF16), jax.ShapeDtypeStruct((VALL, b * s), BF16)],
        grid=(b,),
        in_specs=[pl.BlockSpec((None, s, KV_LORA), lambda i: (i, 0, 0)),
                  pl.BlockSpec((None, s, LANES), lambda i: (i, 0, 0)),
                  _const_spec(wk.shape), _const_spec(wvt.shape)],
        out_specs=[pl.BlockSpec((None, s, QPAD), lambda i: (i, 0, 0)),
                   pl.BlockSpec((VALL, s), lambda i: (0, i))],
        name="ctx_decompress",
        compiler_params=pltpu.CompilerParams(dimension_semantics=("arbitrary",)),
    )(ckv, kr_pad, wk, wvt)


def _attn_kernel(*refs, has_ctx):
    if has_ctx:
        q_ref, kc_ref, vct_ref, k_ref, vt_ref, o_ref, st_ref = refs
    else:
        q_ref, k_ref, vt_ref, o_ref, st_ref = refs
    tq = q_ref.shape[0]
    blocks = [(kc_ref, vct_ref, 0)] if has_ctx else []
    blocks += [(k_ref, vt_ref, r) for r in range(0, k_ref.shape[0], KEY_BLOCK)]
    ones = jnp.ones((ONES_ROWS, KEY_BLOCK), BF16)

    col_max = [None] * MLA_HEADS
    pair = []
    for stage in range(MLA_HEADS + 1):
        ha, hb = stage, stage - 1
        run_max = None
        acc = jnp.zeros((MLA_V + ONES_ROWS, tq), F32)
        for j, (kr, vr, r0) in enumerate(blocks):
            rows = slice(j * KEY_BLOCK, (j + 1) * KEY_BLOCK)
            if ha < MLA_HEADS:
                sl = slice(ha * HEAD_PAD, (ha + 1) * HEAD_PAD)
                st = lax.dot_general(kr[r0:r0 + KEY_BLOCK, sl], q_ref[:, sl], _NT,
                                     preferred_element_type=F32)
                st_ref[ha % 2, rows, :] = st
                blk_max = jnp.max(st.reshape(KEY_BLOCK // 8, 8, tq), axis=0)
                run_max = blk_max if run_max is None else jnp.maximum(run_max, blk_max)
            if hb >= 0:
                p = jnp.exp2(st_ref[hb % 2, rows, :] - col_max[hb]).astype(BF16)
                v_aug = jnp.concatenate([vr[hb * MLA_V:(hb + 1) * MLA_V, r0:r0 + KEY_BLOCK], ones], axis=0)
                acc = acc + jnp.dot(v_aug, p, preferred_element_type=F32)
        if ha < MLA_HEADS:
            col_max[ha] = jnp.max(run_max, axis=0, keepdims=True)
        if hb >= 0:
            pair.append(acc[:MLA_V, :] / acc[MLA_V:MLA_V + 1, :])
            if len(pair) == 2:
                o_ref[:, (hb - 1) * MLA_V:(hb + 1) * MLA_V] = jnp.concatenate(pair, axis=0).T.astype(BF16)
                pair = []


def _attention(q, k, vt, ctx_kv, tq):
    b, t, _ = q.shape
    has_ctx = ctx_kv is not None
    in_specs = [pl.BlockSpec((None, tq, QPAD), lambda i, j: (i, j, 0))]
    args = [q]
    if has_ctx:
        kc, vct = ctx_kv
        s = kc.shape[1]
        in_specs += [pl.BlockSpec((None, s, QPAD), lambda i, j: (i, 0, 0)),
                     pl.BlockSpec((VALL, s), lambda i, j: (0, i))]
        args += [kc, vct]
    in_specs += [pl.BlockSpec((None, t, QPAD), lambda i, j: (i, 0, 0)),
                 pl.BlockSpec((VALL, t), lambda i, j: (0, i))]
    args += [k, vt]
    return pl.pallas_call(
        functools.partial(_attn_kernel, has_ctx=has_ctx),
        out_shape=jax.ShapeDtypeStruct((b, t, VALL), BF16),
        grid=(b, t // tq),
        in_specs=in_specs,
        out_specs=pl.BlockSpec((None, tq, VALL), lambda i, j: (i, j, 0)),
        scratch_shapes=[pltpu.VMEM((2, t + (s if has_ctx else 0), tq), F32)],
        name="mla_attn_lat" if has_ctx else "mla_attn_ctx",
        compiler_params=pltpu.CompilerParams(dimension_semantics=("arbitrary", "arbitrary"),
                                             vmem_limit_bytes=VMEM_LIMIT),
    )(*args)


def _gla_kernel(*refs, n_tiles, zero_init):
    gq_ref, gk_ref, gv_ref, gf_ref, gb_ref, go_ref = refs[:6]
    if zero_init:
        gn_ref, o_ref, sf_ref, sb_ref, cum_ref, db_ref, oacc_ref, bdqk_ref, tri_ref, hm_ref = refs[6:]
    else:
        (sf0_ref, sb0_ref, gn_ref, o_ref, sf_ref, sb_ref, cum_ref, db_ref, oacc_ref,
         bdqk_ref, tri_ref, hm_ref) = refs[6:]
    n_chunks = n_tiles * CHUNKS_PER_TILE
    g_refs = (gf_ref, gb_ref)
    state_refs = (sf_ref, sb_ref)

    ri = lax.broadcasted_iota(jnp.int32, (GLA_TILE, GLA_TILE), 0)
    ci = lax.broadcasted_iota(jnp.int32, (GLA_TILE, GLA_TILE), 1)
    same_chunk = (ri // CHUNK) == (ci // CHUNK)
    tile_tri = (same_chunk & (ri >= ci), same_chunk & (ci >= ri))
    chunk_row = lax.broadcasted_iota(jnp.int32, (LANES, GQK), 0)
    bdqk_ref[...] = jnp.where(same_chunk, 1.0, 0.0).astype(BF16)
    for d in range(2):
        tri_ref[d] = jnp.where(tile_tri[d], 1.0, 0.0)
    hm_ref[...] = jnp.where(
        lax.broadcasted_iota(jnp.int32, (GLA_HEADS * GLA_TILE, GQK), 0) // GLA_TILE
        == lax.broadcasted_iota(jnp.int32, (GLA_HEADS * GLA_TILE, GQK), 1) // GLA_DK, 1.0, 0.0).astype(BF16)

    def tile_rows(t):
        return pl.ds(pl.multiple_of(t * GLA_TILE, GLA_TILE), GLA_TILE)

    def total_row(c, d):
        return c * CHUNK + (CHUNK - 1 if d == 0 else 0)

    for d in range(2):
        tri = tri_ref[d].astype(BF16)

        def cum_body(t, totals, d=d, tri=tri):
            rows = tile_rows(t)
            g = g_refs[d][rows, :]
            hi = g.astype(BF16)
            lo = (g - hi.astype(F32)).astype(BF16)
            cum = (jnp.dot(tri, hi, preferred_element_type=F32)
                   + jnp.dot(tri, lo, preferred_element_type=F32))
            cum_ref[d, rows, :] = cum
            for c in range(CHUNKS_PER_TILE):
                r = total_row(c, d)
                totals = jnp.where(chunk_row == t * CHUNKS_PER_TILE + c, cum[r:r + 1, :], totals)
            return totals

        totals = lax.fori_loop(0, n_tiles, cum_body, jnp.zeros((LANES, GQK), F32))
        dec_t = jnp.exp(totals).T
        for n in range(n_chunks):
            db_ref[d, n] = jnp.broadcast_to(dec_t[:, n:n + 1], (GQK, GLA_DV))

    def tile_dir(t, d):
        rows = tile_rows(t)
        cum = cum_ref[d, rows, :]
        q = gq_ref[rows, :] * (GLA_DK ** -0.5)
        k = gk_ref[rows, :]
        v = gv_ref[rows, :]
        tot = jnp.concatenate(
            [jnp.broadcast_to(cum[total_row(c, d):total_row(c, d) + 1, :], (CHUNK, GQK))
             for c in range(CHUNKS_PER_TILE)], axis=0)
        qe = (q * jnp.exp(cum)).astype(BF16)
        ke = (k * jnp.exp(-cum)).astype(BF16)
        kd_t = (k * jnp.exp(tot - cum)).T.astype(BF16)
        bd_qk = bdqk_ref[...] > 0
        tri = tri_ref[d] > 0

        qm = jnp.where(hm_ref[...] > 0, jnp.tile(qe, (GLA_HEADS, 1)), 0.0)
        att = lax.dot_general(qm, ke, _NT, preferred_element_type=F32)

        intra, upd = [], []
        for hd in range(GLA_HEADS):
            vh = v[:, hd * GLA_DV:(hd + 1) * GLA_DV]
            a_h = jnp.where(tri, att[hd * GLA_TILE:(hd + 1) * GLA_TILE, :], 0.0).astype(BF16)
            intra.append(jnp.dot(a_h, vh, preferred_element_type=F32))
            kd_h = jnp.tile(kd_t[hd * GLA_DK:(hd + 1) * GLA_DK, :], (CHUNKS_PER_TILE, 1))
            upd.append(jnp.dot(jnp.where(bd_qk, kd_h, 0.0), vh, preferred_element_type=F32))

        state = [state_refs[d][hd] for hd in range(GLA_HEADS)]
        order = range(CHUNKS_PER_TILE) if d == 0 else range(CHUNKS_PER_TILE - 1, -1, -1)
        seen = {}
        for c in order:
            seen[c] = jnp.concatenate(state, axis=0).astype(BF16)
            decay = db_ref[d, t * CHUNKS_PER_TILE + c]
            for hd in range(GLA_HEADS):
                ks = slice(hd * GLA_DK, (hd + 1) * GLA_DK)
                state[hd] = decay[ks, :] * state[hd] + upd[hd][c * CHUNK:(c + 1) * CHUNK, :]
        for hd in range(GLA_HEADS):
            state_refs[d][hd] = state[hd]

        for c in range(CHUNKS_PER_TILE):
            cr = slice(c * CHUNK, (c + 1) * CHUNK)
            q_c = jnp.concatenate([qm[hd * GLA_TILE + c * CHUNK:hd * GLA_TILE + (c + 1) * CHUNK, :]
                                   for hd in range(GLA_HEADS)], axis=0)
            inter = jnp.dot(q_c, seen[c], preferred_element_type=F32)
            o = jnp.concatenate([intra[hd][cr, :] + inter[hd * CHUNK:(hd + 1) * CHUNK, :]
                                 for hd in range(GLA_HEADS)], axis=1)
            oacc_ref[d, pl.ds(pl.multiple_of(t * GLA_TILE + c * CHUNK, CHUNK), CHUNK), :] = o

    if zero_init:
        sf_ref[...] = jnp.zeros(sf_ref.shape, F32)
        sb_ref[...] = jnp.zeros(sb_ref.shape, F32)
    else:
        sf_ref[...] = sf0_ref[...]
        sb_ref[...] = sb0_ref[...]

    def main_body(t, carry):
        tile_dir(t, 0)
        tile_dir(n_tiles - 1 - t, 1)
        return carry

    lax.fori_loop(0, n_tiles, main_body, 0)

    gn = gn_ref[...]

    def epilogue_body(t, carry):
        rows = tile_rows(t)
        for hd in range(GLA_HEADS):
            vs = slice(hd * GLA_DV, (hd + 1) * GLA_DV)
            o = _rms(oacc_ref[0, rows, vs] + oacc_ref[1, rows, vs], gn)
            go = go_ref[rows, vs]
            o_ref[rows, vs] = (o * (go * _sigmoid(go))).astype(BF16)
        return carry

    lax.fori_loop(0, n_tiles, epilogue_body, 0)


def _gla(gq, gk, gv, gf, gb, go, init_states, gn):
    b, t, _ = gq.shape
    n_tiles = t // GLA_TILE
    zero_init = init_states is None
    mode = pl.Buffered(1) if t * GV * 4 >= (1 << 22) else None
    seq = lambda c: pl.BlockSpec((None, t, c), lambda i: (i, 0, 0), pipeline_mode=mode)
    st = pl.BlockSpec((None, GLA_HEADS, GLA_DK, GLA_DV), lambda i: (i, 0, 0, 0))
    in_specs = [seq(GQK), seq(GQK), seq(GV), seq(GQK), seq(GQK), seq(GV)]
    args = [gq, gk, gv, gf, gb, go]
    if not zero_init:
        in_specs += [st, st]
        args += list(init_states)
    in_specs.append(_const_spec(gn.shape))
    args.append(gn)
    return pl.pallas_call(
        functools.partial(_gla_kernel, n_tiles=n_tiles, zero_init=zero_init),
        out_shape=[jax.ShapeDtypeStruct((b, t, GV), BF16),
                   jax.ShapeDtypeStruct((b, GLA_HEADS, GLA_DK, GLA_DV), F32),
                   jax.ShapeDtypeStruct((b, GLA_HEADS, GLA_DK, GLA_DV), F32)],
        grid=(b,),
        in_specs=in_specs,
        out_specs=[pl.BlockSpec((None, t, GV), lambda i: (i, 0, 0)), st, st],
        scratch_shapes=[pltpu.VMEM((2, t, GQK), F32),
                        pltpu.VMEM((2, n_tiles * CHUNKS_PER_TILE, GQK, GLA_DV), F32),
                        pltpu.VMEM((2, t, GV), F32),
                        pltpu.VMEM((GQK, GLA_TILE), BF16),
                        pltpu.VMEM((2, GLA_TILE, GLA_TILE), F32),
                        pltpu.VMEM((GLA_HEADS * GLA_TILE, GQK), BF16)],
        name="gla_%d" % t,
        compiler_params=pltpu.CompilerParams(dimension_semantics=("arbitrary",),
                                             vmem_limit_bytes=VMEM_LIMIT),
    )(*args)


def _ffn_kernel(x_ref, at_ref, gl_ref, mod_ref, wout_ref, nf_ref, wfi_ref, wfo_ref, fn_ref,
                y_ref, act_ref, *, mod_row):
    _, _, gt1, sh2, sc2, gt2 = _mod_rows(mod_ref, mod_row(pl.program_id(0)))
    mix = (jnp.dot(at_ref[...], wout_ref[0:VALL, :], preferred_element_type=F32)
           + jnp.dot(gl_ref[...], wout_ref[VALL:, :], preferred_element_type=F32))
    x1 = x_ref[...] + gt1 * mix
    h2 = (_rms(x1, nf_ref[...]) * (1.0 + sc2) + sh2).astype(BF16)
    for j in range(N_FF_CHUNKS):
        cs = slice(j * FF_CHUNK, (j + 1) * FF_CHUNK)
        a = jnp.dot(h2, wfi_ref[:, cs], preferred_element_type=F32)
        g = jnp.dot(h2, wfi_ref[:, D_FF + j * FF_CHUNK:D_FF + (j + 1) * FF_CHUNK],
                    preferred_element_type=F32)
        act_ref[:, cs] = (a * _sigmoid(a) * g).astype(BF16)
    ff = jnp.dot(act_ref[...], wfo_ref[...], preferred_element_type=F32)
    x2 = x1 + gt2 * ff
    y_ref[...] = _rms(x2, fn_ref[...])


def _ffn(x2d, attn, gla, mod, mod_row, weights, tm, name):
    n_tok, d = x2d.shape
    wout, nf, wfi, wfo, fn = weights
    row = lambda i: (i, 0)
    return pl.pallas_call(
        functools.partial(_ffn_kernel, mod_row=mod_row),
        out_shape=jax.ShapeDtypeStruct((n_tok, d), F32),
        grid=(n_tok // tm,),
        in_specs=[pl.BlockSpec((tm, d), row),
                  pl.BlockSpec((tm, VALL), row),
                  pl.BlockSpec((tm, GV), row),
                  _const_spec(mod.shape),
                  _const_spec(wout.shape), _const_spec(nf.shape), _const_spec(wfi.shape),
                  _const_spec(wfo.shape), _const_spec(fn.shape)],
        out_specs=pl.BlockSpec((tm, d), row),
        scratch_shapes=[pltpu.VMEM((tm, D_FF), BF16)],
        name=name,
        compiler_params=pltpu.CompilerParams(dimension_semantics=("arbitrary",),
                                             vmem_limit_bytes=VMEM_LIMIT),
    )(x2d, attn, gla, mod, wout, nf, wfi, wfo, fn)


def _rope_tables(n_tokens):
    t = np.arange(n_tokens)
    row = (t // GRID_W).astype(np.float32)
    col = (t % GRID_W).astype(np.float32)
    half = MLA_ROPE // 2
    inv = (np.float32(ROPE_BASE) ** (-np.arange(0, half, 2, dtype=np.float32) / np.float32(half))).astype(np.float32)
    ang_r = row[:, None] * inv
    ang_c = col[:, None] * inv
    ang = np.concatenate([ang_r, ang_r, ang_c, ang_c], axis=-1).astype(np.float32)
    cos, sin = np.cos(ang), np.sin(ang)
    first = (np.arange(MLA_ROPE) % half) < (half // 2)
    cos_t = np.ones((n_tokens, LANES), np.float32)
    sa_t = np.zeros((n_tokens, LANES), np.float32)
    sb_t = np.zeros((n_tokens, LANES), np.float32)
    cos_t[:, ROPE_LANE0:ROPE_LANE0 + MLA_ROPE] = cos
    sa_t[:, ROPE_LANE0:ROPE_LANE0 + MLA_ROPE] = np.where(first, -sin, 0.0)
    sb_t[:, ROPE_LANE0:ROPE_LANE0 + MLA_ROPE] = np.where(first, 0.0, sin)
    return jnp.asarray(cos_t), jnp.asarray(sa_t), jnp.asarray(sb_t)


def kernel(x_prompt, x_sample, cache_kv_latent, cache_k_rope, state_gla_fwd, state_gla_bwd, c, c_ctx, w_ada, b_ada, norm_attn, w_in, mla_q_norm, w_uq, mla_kv_norm, w_ukv, w_gate_f, b_gate_f, w_gate_b, b_gate_b, gla_norm, w_out, norm_ffn, w_ffn_in, w_ffn_out, final_norm):
    batch, seq, d = x_prompt.shape
    dec_batch, dec_seq, _ = x_sample.shape
    assert w_ada.shape[0] == 1 and w_in.shape[-1] == W_COLS and w_ffn_in.shape[-1] == 2 * D_FF
    l = 0

    cond8 = jnp.concatenate([c_ctx[None, :], c, jnp.zeros((8 - 1 - dec_batch, d), F32)], axis=0)
    mod = _ada(cond8, w_ada[l], b_ada[l])

    win, wuq, wk, wvt, wg = _prep_in_weights(w_in, w_uq, w_ukv, w_gate_f, w_gate_b)
    bg = jnp.concatenate([b_gate_f[l], b_gate_b[l]]).reshape(1, 2 * GQK)
    in_w = (norm_attn[l].reshape(1, d), win, mla_q_norm[l].reshape(1, Q_LORA), wuq,
            mla_kv_norm[l].reshape(1, KV_LORA), wk, wvt, wg, bg)
    ffn_w = (w_out[l].astype(BF16), norm_ffn[l].reshape(1, d), w_ffn_in[l].astype(BF16),
             w_ffn_out[l].astype(BF16), final_norm.reshape(1, d))
    gn = gla_norm[l].reshape(1, GLA_DV)
    tm, tm_ffn = 512, 512
    r3 = lambda a, b_, t: a.reshape(b_, t, a.shape[-1])

    xp = x_prompt.reshape(batch * seq, d)
    (q, k, vt, gq, gk, gv, gf, gb, go, ckv, kr) = _inproj(xp, mod, lambda i: 0, in_w, None, tm, seq // tm)
    attn = _attention(r3(q, batch, seq), r3(k, batch, seq), vt, None, min(seq, 256))
    o_gla, sf, sb = _gla(r3(gq, batch, seq), r3(gk, batch, seq), r3(gv, batch, seq), r3(gf, batch, seq),
                         r3(gb, batch, seq), r3(go, batch, seq), None, gn)
    y_prompt = _ffn(xp, attn.reshape(batch * seq, VALL), o_gla.reshape(batch * seq, GV), mod,
                    lambda i: 0, ffn_w, tm_ffn, "out_ffn_ctx").reshape(batch, seq, d)

    xs = x_sample.reshape(dec_batch * dec_seq, d)
    tiles = dec_seq // tm
    (q, k, vt, gq, gk, gv, gf, gb, go) = _inproj(xs, mod, lambda i: 1 + i // tiles, in_w,
                                                  _rope_tables(dec_seq), tm, tiles)
    kr_pad = jnp.pad(cache_k_rope[:, l], ((0, 0), (0, 0), (ROPE_LANE0, LANES - ROPE_LANE0 - MLA_ROPE)))
    kc, vct = _decomp(cache_kv_latent[:, l], kr_pad, wk, wvt)
    attn = _attention(r3(q, dec_batch, dec_seq), r3(k, dec_batch, dec_seq), vt, (kc, vct), 256)
    o_gla, _, _ = _gla(r3(gq, dec_batch, dec_seq), r3(gk, dec_batch, dec_seq), r3(gv, dec_batch, dec_seq),
                       r3(gf, dec_batch, dec_seq), r3(gb, dec_batch, dec_seq), r3(go, dec_batch, dec_seq),
                       (state_gla_fwd[:, l].astype(F32), state_gla_bwd[:, l].astype(F32)), gn)
    ftiles = dec_seq // tm_ffn
    y_sample = _ffn(xs, attn.reshape(dec_batch * dec_seq, VALL), o_gla.reshape(dec_batch * dec_seq, GV),
                    mod, lambda i: 1 + i // ftiles, ffn_w, tm_ffn, "out_ffn_lat").reshape(dec_batch, dec_seq, d)

    new_kv_latent = ckv.reshape(batch, 1, seq, KV_LORA)
    new_k_rope = kr.reshape(batch, 1, seq, MLA_ROPE)
    new_state_fwd = sf.reshape(batch, 1, GLA_HEADS, GLA_DK, GLA_DV).astype(x_prompt.dtype)
    new_state_bwd = sb.reshape(batch, 1, GLA_HEADS, GLA_DK, GLA_DV).astype(x_prompt.dtype)
    return (y_prompt, y_sample, new_kv_latent, new_k_rope, new_state_fwd, new_state_bwd)
```

```python
import functools

import numpy as np
import jax
import jax.numpy as jnp
from jax import lax
from jax.experimental import pallas as pl
from jax.experimental.pallas import tpu as pltpu

F32 = jnp.float32
BF16 = jnp.bfloat16

GRID_W = 64
MLA_HEADS = 8
MLA_NOPE = 64
MLA_ROPE = 32
MLA_QK = MLA_NOPE + MLA_ROPE
MLA_V = 64
Q_LORA = 384
KV_LORA = 256
GLA_HEADS = 4
GLA_DK = 64
GLA_DV = 128
GATE_RANK = 16
GATE_NORM = 16.0
CHUNK = 64
D_FF = 2816
ROPE_BASE = 10000.0
EPS = 1e-6
LOG2_E = 1.4426950408889634

LANES = 128
HEAD_PAD = LANES
ROPE_LANE0 = MLA_NOPE
GQK = GLA_HEADS * GLA_DK
GV = GLA_HEADS * GLA_DV
QPAD = MLA_HEADS * HEAD_PAD
VALL = MLA_HEADS * MLA_V
ONES_ROWS = 16
KEY_BLOCK = 256

W_KR = Q_LORA + KV_LORA
W_GQ = W_KR + MLA_ROPE
W_GF = W_GQ + 2 * GQK + GV
W_GO = W_GF + 2 * GATE_RANK
W_COLS = W_GO + GV

Z_Q = 0
Z_KV = Z_Q + Q_LORA
Z_GQ = Z_KV + KV_LORA
Z_GK = Z_GQ + GQK
Z_GV = Z_GK + GQK
Z_GO = Z_GV + GV
Z_MISC = Z_GO + GV
Z_COLS = Z_MISC + LANES

FF_CHUNK = 256
N_FF_CHUNKS = D_FF // FF_CHUNK
D_MIX = VALL + GV
WOUT_CHUNK = 128
N_WOUT_CHUNKS = D_MIX // WOUT_CHUNK

GLA_TILE = 256
CHUNKS_PER_TILE = GLA_TILE // CHUNK

VMEM_LIMIT = 56 * 1024 * 1024

_NT = (((1,), (1,)), ((), ()))


def _rms(x, w):
    return x * lax.rsqrt(jnp.mean(x * x, axis=-1, keepdims=True) + EPS) * w


def _sigmoid(x):
    return 1.0 / (1.0 + jnp.exp(-x))


def _log_sigmoid(x):
    return jnp.minimum(x, 0.0) - jnp.log1p(jnp.exp(-jnp.abs(x)))


def _const_spec(shape):
    nd = len(shape)
    return pl.BlockSpec(shape, lambda *_: (0,) * nd, pipeline_mode=pl.Buffered(1))


def _mod_rows(mod_ref, r):
    return [mod_ref[k, pl.ds(r, 1), :] for k in range(6)]


def _ada_kernel(cond_ref, w_ref, b_ref, o_ref):
    c = cond_ref[...]
    s = (c * _sigmoid(c)).astype(BF16)
    o_ref[...] = jnp.dot(s, w_ref[...].astype(BF16), preferred_element_type=F32) + b_ref[...]


def _ada(cond8, w_ada, b_ada):
    d = w_ada.shape[0]
    n = w_ada.shape[1]
    return pl.pallas_call(
        _ada_kernel,
        out_shape=jax.ShapeDtypeStruct((n // d, 8, d), F32),
        grid=(n // d,),
        in_specs=[pl.BlockSpec((8, d), lambda j: (0, 0)),
                  pl.BlockSpec((d, d), lambda j: (0, j)),
                  pl.BlockSpec((1, d), lambda j: (0, j))],
        out_specs=pl.BlockSpec((None, 8, d), lambda j: (j, 0, 0)),
        name="ada_mod",
        compiler_params=pltpu.CompilerParams(dimension_semantics=("arbitrary",)),
    )(cond8, w_ada, b_ada.reshape(1, n))


def _prep_kernel(wint_ref, wuq_ref, wukv_ref, wgf_ref, wgb_ref, win_o, wuq_o, wk_o, wvt_o, wg_o):
    cols = wint_ref.shape[1]
    for dst, src, n in ((Z_Q, 0, W_KR), (Z_GQ, W_GQ, W_GF - W_GQ), (Z_GO, W_GO, GV)):
        win_o[:, dst:dst + n] = wint_ref[src:src + n, :].T.astype(BF16)
    z32 = jnp.zeros((32, cols), F32)
    misc_t = jnp.concatenate([wint_ref[W_GF:W_GO, :], z32, wint_ref[W_KR:W_GQ, :], z32], axis=0)
    win_o[:, Z_MISC:Z_COLS] = misc_t.T.astype(BF16)

    u = wuq_ref[...]
    zq = jnp.zeros((u.shape[0], HEAD_PAD - MLA_QK), F32)
    for hd in range(MLA_HEADS):
        blk = jnp.concatenate([u[:, hd * MLA_QK:(hd + 1) * MLA_QK], zq], axis=1)
        wuq_o[:, hd * HEAD_PAD:(hd + 1) * HEAD_PAD] = blk.astype(BF16)

    @pl.when(pl.program_id(0) == 0)
    def _():
        kv = wukv_ref[...]
        per = MLA_NOPE + MLA_V
        lane = lax.broadcasted_iota(jnp.int32, (kv.shape[0], per), 1)
        for hd in range(MLA_HEADS):
            blk = kv[:, hd * per:(hd + 1) * per]
            wk_o[:, hd * HEAD_PAD:(hd + 1) * HEAD_PAD] = jnp.where(lane < MLA_NOPE, blk, 0.0).astype(BF16)
        wv = jnp.concatenate([kv[:, hd * per + MLA_NOPE:(hd + 1) * per] for hd in range(MLA_HEADS)], axis=1)
        wvt_o[...] = wv.T.astype(BF16)

        wg_o[...] = jnp.zeros(wg_o.shape, BF16)
        wg_o[0:GATE_RANK, 0:GQK] = wgf_ref[...].astype(BF16)
        wg_o[GATE_RANK:2 * GATE_RANK, GQK:2 * GQK] = wgb_ref[...].astype(BF16)


def _prep_in_weights(w_in, w_uq, w_ukv, w_gate_f, w_gate_b):
    d = w_in.shape[1]
    steps = 4
    w_in_t = jnp.swapaxes(w_in, 1, 2)
    rb3 = lambda r, c: pl.BlockSpec((None, r // steps, c), lambda i: (0, i, 0))
    rb = lambda r, c: pl.BlockSpec((r // steps, c), lambda i: (i, 0))
    full3 = lambda shape: pl.BlockSpec((None,) + tuple(shape[1:]), lambda i: (0, 0, 0))
    full = lambda shape: pl.BlockSpec(shape, lambda i: (0, 0))
    return pl.pallas_call(
        _prep_kernel,
        out_shape=[jax.ShapeDtypeStruct((d, Z_COLS), BF16),
                   jax.ShapeDtypeStruct((Q_LORA, QPAD), BF16),
                   jax.ShapeDtypeStruct((KV_LORA, QPAD), BF16),
                   jax.ShapeDtypeStruct((VALL, KV_LORA), BF16),
                   jax.ShapeDtypeStruct((LANES, 2 * GQK), BF16)],
        grid=(steps,),
        in_specs=[pl.BlockSpec((None, W_COLS, d // steps), lambda i: (0, 0, i)),
                  rb3(Q_LORA, MLA_HEADS * MLA_QK), full3(w_ukv.shape),
                  full3(w_gate_f.shape), full3(w_gate_b.shape)],
        out_specs=[rb(d, Z_COLS), rb(Q_LORA, QPAD), full((KV_LORA, QPAD)), full((VALL, KV_LORA)),
                   full((LANES, 2 * GQK))],
        name="weight_prep",
        compiler_params=pltpu.CompilerParams(dimension_semantics=("arbitrary",)),
    )(w_in_t, w_uq, w_ukv, w_gate_f, w_gate_b)


def _inproj_kernel(*refs, latent, mod_row):
    (x_ref, mod_ref, nw_ref, win_ref, qn_ref, wuq_ref, kvn_ref, wk_ref, wvt_ref, wg_ref, bg_ref) = refs[:11]
    if latent:
        cos_ref, sa_ref, sb_ref = refs[11:14]
        outs = refs[14:]
    else:
        outs = refs[11:]
    q_ref, k_ref, vt_ref, gq_ref, gk_ref, gv_ref, gf_ref, gb_ref, go_ref = outs[:9]

    sh1, sc1 = _mod_rows(mod_ref, mod_row(pl.program_id(0)))[:2]
    x = x_ref[...]
    h = _rms(x, nw_ref[...]) * (1.0 + sc1) + sh1
    z = jnp.dot(h.astype(BF16), win_ref[...], preferred_element_type=F32)

    qn = _rms(z[:, Z_Q:Z_Q + Q_LORA], qn_ref[...])
    q = jnp.dot(qn.astype(BF16), wuq_ref[...], preferred_element_type=F32)
    ckv = _rms(z[:, Z_KV:Z_KV + KV_LORA], kvn_ref[...])
    ckv_b = ckv.astype(BF16)
    kn = jnp.dot(ckv_b, wk_ref[...], preferred_element_type=F32)
    vt_ref[...] = lax.dot_general(wvt_ref[...], ckv_b, _NT,
                                  preferred_element_type=F32).astype(BF16)
    misc = z[:, Z_MISC:Z_MISC + LANES]

    if latent:
        cos, sa, sb = cos_ref[...], sa_ref[...], sb_ref[...]

        def rope(t):
            return t * cos + pltpu.roll(t, LANES - 8, 1) * sa + pltpu.roll(t, 8, 1) * sb
    else:
        def rope(t):
            return t

    scale = MLA_QK ** -0.5 * LOG2_E
    lane = lax.broadcasted_iota(jnp.int32, misc.shape, 1)
    in_rope = (lane >= ROPE_LANE0) & (lane < ROPE_LANE0 + MLA_ROPE)
    krope = rope(misc)
    for hd in range(MLA_HEADS):
        sl = slice(hd * HEAD_PAD, (hd + 1) * HEAD_PAD)
        q_ref[:, sl] = (rope(q[:, sl]) * scale).astype(BF16)
        k_ref[:, sl] = jnp.where(in_rope, krope, kn[:, sl]).astype(BF16)

    gq_ref[...] = z[:, Z_GQ:Z_GQ + GQK]
    gk_ref[...] = z[:, Z_GK:Z_GK + GQK]
    gv_ref[...] = z[:, Z_GV:Z_GV + GV].astype(BF16)
    go_ref[...] = z[:, Z_GO:Z_GO + GV]
    gpre = jnp.dot(misc.astype(BF16), wg_ref[...], preferred_element_type=F32) + bg_ref[...]
    gate = _log_sigmoid(gpre) * (1.0 / GATE_NORM)
    gf_ref[...] = gate[:, :GQK]
    gb_ref[...] = gate[:, GQK:]

    if not latent:
        ckv_ref, kr_ref = outs[9:]
        ckv_ref[...] = ckv
        kr_ref[...] = misc[:, ROPE_LANE0:ROPE_LANE0 + MLA_ROPE]


def _inproj(x2d, mod, mod_row, weights, rope_tabs, tm, tiles_per_seq):
    n_tok, d = x2d.shape
    latent = rope_tabs is not None
    nw, win, qn, wuq, kvn, wk, wvt, wg, bg = weights
    row = lambda i: (i, 0)
    in_specs = [pl.BlockSpec((tm, d), row), _const_spec(mod.shape),
                _const_spec(nw.shape), _const_spec(win.shape), _const_spec(qn.shape),
                _const_spec(wuq.shape), _const_spec(kvn.shape), _const_spec(wk.shape),
                _const_spec(wvt.shape), _const_spec(wg.shape), _const_spec(bg.shape)]
    args = [x2d, mod, nw, win, qn, wuq, kvn, wk, wvt, wg, bg]
    if latent:
        tab = pl.BlockSpec((tm, LANES), lambda i: (i % tiles_per_seq, 0))
        in_specs += [tab, tab, tab]
        args += list(rope_tabs)
    out_cols = [(QPAD, BF16), (QPAD, BF16), None, (GQK, F32), (GQK, F32), (GV, BF16),
                (GQK, F32), (GQK, F32), (GV, F32)]
    if not latent:
        out_cols += [(KV_LORA, F32), (MLA_ROPE, F32)]
    out_shape = [jax.ShapeDtypeStruct((n_tok, oc[0]), oc[1]) if oc else
                 jax.ShapeDtypeStruct((VALL, n_tok), BF16) for oc in out_cols]
    out_specs = [pl.BlockSpec((tm, oc[0]), row) if oc else
                 pl.BlockSpec((VALL, tm), lambda i: (0, i)) for oc in out_cols]
    return pl.pallas_call(
        functools.partial(_inproj_kernel, latent=latent, mod_row=mod_row),
        out_shape=out_shape,
        grid=(n_tok // tm,),
        in_specs=in_specs,
        out_specs=out_specs,
        name="inproj_lat" if latent else "inproj_ctx",
        compiler_params=pltpu.CompilerParams(dimension_semantics=("arbitrary",),
                                             vmem_limit_bytes=VMEM_LIMIT),
    )(*args)


def _decomp_kernel(ckv_ref, kr_ref, wk_ref, wvt_ref, k_ref, vt_ref):
    ckv_b = ckv_ref[...].astype(BF16)
    kn = jnp.dot(ckv_b, wk_ref[...], preferred_element_type=F32)
    kr = kr_ref[...]
    lane = lax.broadcasted_iota(jnp.int32, kr.shape, 1)
    in_rope = (lane >= ROPE_LANE0) & (lane < ROPE_LANE0 + MLA_ROPE)
    for hd in range(MLA_HEADS):
        sl = slice(hd * HEAD_PAD, (hd + 1) * HEAD_PAD)
        k_ref[:, sl] = jnp.where(in_rope, kr, kn[:, sl]).astype(BF16)
    vt_ref[...] = lax.dot_general(wvt_ref[...], ckv_b, _NT, preferred_element_type=F32).astype(BF16)


def _decomp(ckv, kr_pad, wk, wvt):
    b, s, _ = ckv.shape
    return pl.pallas_call(
        _decomp_kernel,
        out_shape=[jax.ShapeDtypeStruct((b, s, QPAD), BF16), jax.ShapeDtypeStruct((VALL, b * s), BF16)],
        grid=(b,),
        in_specs=[pl.BlockSpec((None, s, KV_LORA), lambda i: (i, 0, 0)),
                  pl.BlockSpec((None, s, LANES), lambda i: (i, 0, 0)),
                  _const_spec(wk.shape), _const_spec(wvt.shape)],
        out_specs=[pl.BlockSpec((None, s, QPAD), lambda i: (i, 0, 0)),
                   pl.BlockSpec((VALL, s), lambda i: (0, i))],
        name="ctx_decompress",
        compiler_params=pltpu.CompilerParams(dimension_semantics=("arbitrary",)),
    )(ckv, kr_pad, wk, wvt)


def _attn_kernel(*refs, has_ctx):
    if has_ctx:
        q_ref, kc_ref, vct_ref, k_ref, vt_ref, o_ref, st_ref = refs
    else:
        q_ref, k_ref, vt_ref, o_ref, st_ref = refs
    tq = q_ref.shape[0]
    blocks = [(kc_ref, vct_ref, 0)] if has_ctx else []
    blocks += [(k_ref, vt_ref, r) for r in range(0, k_ref.shape[0], KEY_BLOCK)]
    ones = jnp.ones((ONES_ROWS, KEY_BLOCK), BF16)

    col_max = [None] * MLA_HEADS
    pair = []
    for stage in range(MLA_HEADS + 1):
        ha, hb = stage, stage - 1
        run_max = None
        acc = jnp.zeros((MLA_V + ONES_ROWS, tq), F32)
        for j, (kr, vr, r0) in enumerate(blocks):
            rows = slice(j * KEY_BLOCK, (j + 1) * KEY_BLOCK)
            if ha < MLA_HEADS:
                sl = slice(ha * HEAD_PAD, (ha + 1) * HEAD_PAD)
                st = lax.dot_general(kr[r0:r0 + KEY_BLOCK, sl], q_ref[:, sl], _NT,
                                     preferred_element_type=F32)
                st_ref[ha % 2, rows, :] = st
                blk_max = jnp.max(st.reshape(KEY_BLOCK // 8, 8, tq), axis=0)
                run_max = blk_max if run_max is None else jnp.maximum(run_max, blk_max)
            if hb >= 0:
                p = jnp.exp2(st_ref[hb % 2, rows, :] - col_max[hb]).astype(BF16)
                v_aug = jnp.concatenate([vr[hb * MLA_V:(hb + 1) * MLA_V, r0:r0 + KEY_BLOCK], ones], axis=0)
                acc = acc + jnp.dot(v_aug, p, preferred_element_type=F32)
        if ha < MLA_HEADS:
            col_max[ha] = jnp.max(run_max, axis=0, keepdims=True)
        if hb >= 0:
            pair.append(acc[:MLA_V, :] / acc[MLA_V:MLA_V + 1, :])
            if len(pair) == 2:
                o_ref[:, (hb - 1) * MLA_V:(hb + 1) * MLA_V] = jnp.concatenate(pair, axis=0).T.astype(BF16)
                pair = []


def _attention(q, k, vt, ctx_kv, tq):
    b, t, _ = q.shape
    has_ctx = ctx_kv is not None
    in_specs = [pl.BlockSpec((None, tq, QPAD), lambda i, j: (i, j, 0))]
    args = [q]
    if has_ctx:
        kc, vct = ctx_kv
        s = kc.shape[1]
        in_specs += [pl.BlockSpec((None, s, QPAD), lambda i, j: (i, 0, 0)),
                     pl.BlockSpec((VALL, s), lambda i, j: (0, i))]
        args += [kc, vct]
    in_specs += [pl.BlockSpec((None, t, QPAD), lambda i, j: (i, 0, 0)),
                 pl.BlockSpec((VALL, t), lambda i, j: (0, i))]
    args += [k, vt]
    return pl.pallas_call(
        functools.partial(_attn_kernel, has_ctx=has_ctx),
        out_shape=jax.ShapeDtypeStruct((b, t, VALL), BF16),
        grid=(b, t // tq),
        in_specs=in_specs,
        out_specs=pl.BlockSpec((None, tq, VALL), lambda i, j: (i, j, 0)),
        scratch_shapes=[pltpu.VMEM((2, t + (s if has_ctx else 0), tq), F32)],
        name="mla_attn_lat" if has_ctx else "mla_attn_ctx",
        compiler_params=pltpu.CompilerParams(dimension_semantics=("arbitrary", "arbitrary"),
                                             vmem_limit_bytes=VMEM_LIMIT),
    )(*args)


def _gla_kernel(*refs, n_tiles, zero_init):
    gq_ref, gk_ref, gv_ref, gf_ref, gb_ref, go_ref = refs[:6]
    if zero_init:
        gn_ref, o_ref, sf_ref, sb_ref, oacc_ref, bdqk_ref, tri_ref, hm_ref = refs[6:]
    else:
        (sf0_ref, sb0_ref, gn_ref, o_ref, sf_ref, sb_ref, oacc_ref,
         bdqk_ref, tri_ref, hm_ref) = refs[6:]
    g_refs = (gf_ref, gb_ref)
    state_refs = (sf_ref, sb_ref)

    @pl.when(pl.program_id(0) == 0)
    def _():
        ri = lax.broadcasted_iota(jnp.int32, (GLA_TILE, GLA_TILE), 0)
        ci = lax.broadcasted_iota(jnp.int32, (GLA_TILE, GLA_TILE), 1)
        same_chunk = (ri // CHUNK) == (ci // CHUNK)
        bdqk_ref[...] = jnp.where(same_chunk, 1.0, 0.0).astype(BF16)
        tri_ref[0] = jnp.where(same_chunk & (ri >= ci), 1.0, 0.0)
        tri_ref[1] = jnp.where(same_chunk & (ci >= ri), 1.0, 0.0)
        hm_ref[...] = jnp.where(
            lax.broadcasted_iota(jnp.int32, (GLA_HEADS * GLA_TILE, GQK), 0) // GLA_TILE
            == lax.broadcasted_iota(jnp.int32, (GLA_HEADS * GLA_TILE, GQK), 1) // GLA_DK,
            1.0, 0.0).astype(BF16)

    row8 = lax.broadcasted_iota(jnp.int32, (8, GQK), 0)

    def tile_rows(t):
        return pl.ds(pl.multiple_of(t * GLA_TILE, GLA_TILE), GLA_TILE)

    def total_row(c, d):
        return c * CHUNK + (CHUNK - 1 if d == 0 else 0)

    def tile_dir(t, d):
        rows = tile_rows(t)
        g = g_refs[d][rows, :]
        g_hi = g.astype(BF16)
        g_lo = (g - g_hi.astype(F32)).astype(BF16)
        tri_b = tri_ref[d].astype(BF16)
        cum = (jnp.dot(tri_b, g_hi, preferred_element_type=F32)
               + jnp.dot(tri_b, g_lo, preferred_element_type=F32))
        totals = [cum[total_row(c, d):total_row(c, d) + 1, :] for c in range(CHUNKS_PER_TILE)]
        tot8 = jnp.zeros((8, GQK), F32)
        for c in range(CHUNKS_PER_TILE):
            tot8 = jnp.where(row8 == c, totals[c], tot8)
        dec_t = jnp.concatenate([jnp.exp(tot8), jnp.zeros((LANES - 8, GQK), F32)], axis=0).T
        q = gq_ref[rows, :] * (GLA_DK ** -0.5)
        k = gk_ref[rows, :]
        v = gv_ref[rows, :]
        tot = jnp.concatenate([jnp.broadcast_to(tc, (CHUNK, GQK)) for tc in totals], axis=0)
        qe = (q * jnp.exp(cum)).astype(BF16)
        ke = (k * jnp.exp(-cum)).astype(BF16)
        kd_t = (k * jnp.exp(tot - cum)).T.astype(BF16)
        bd_qk = bdqk_ref[...] > 0
        tri = tri_ref[d] > 0

        qm = jnp.where(hm_ref[...] > 0, jnp.tile(qe, (GLA_HEADS, 1)), 0.0)
        att = lax.dot_general(qm, ke, _NT, preferred_element_type=F32)

        intra, upd = [], []
        for hd in range(GLA_HEADS):
            vh = v[:, hd * GLA_DV:(hd + 1) * GLA_DV]
            a_h = jnp.where(tri, att[hd * GLA_TILE:(hd + 1) * GLA_TILE, :], 0.0).astype(BF16)
            intra.append(jnp.dot(a_h, vh, preferred_element_type=F32))
            kd_h = jnp.tile(kd_t[hd * GLA_DK:(hd + 1) * GLA_DK, :], (CHUNKS_PER_TILE, 1))
            upd.append(jnp.dot(jnp.where(bd_qk, kd_h, 0.0), vh, preferred_element_type=F32))

        state = [state_refs[d][hd] for hd in range(GLA_HEADS)]
        order = range(CHUNKS_PER_TILE) if d == 0 else range(CHUNKS_PER_TILE - 1, -1, -1)
        seen = {}
        for c in order:
            seen[c] = jnp.concatenate(state, axis=0).astype(BF16)
            decay = jnp.broadcast_to(dec_t[:, c:c + 1], (GQK, GLA_DV))
            for hd in range(GLA_HEADS):
                ks = slice(hd * GLA_DK, (hd + 1) * GLA_DK)
                state[hd] = decay[ks, :] * state[hd] + upd[hd][c * CHUNK:(c + 1) * CHUNK, :]
        for hd in range(GLA_HEADS):
            state_refs[d][hd] = state[hd]

        for c in range(CHUNKS_PER_TILE):
            cr = slice(c * CHUNK, (c + 1) * CHUNK)
            q_c = jnp.concatenate([qm[hd * GLA_TILE + c * CHUNK:hd * GLA_TILE + (c + 1) * CHUNK, :]
                                   for hd in range(GLA_HEADS)], axis=0)
            inter = jnp.dot(q_c, seen[c], preferred_element_type=F32)
            o = jnp.concatenate([intra[hd][cr, :] + inter[hd * CHUNK:(hd + 1) * CHUNK, :]
                                 for hd in range(GLA_HEADS)], axis=1)
            oacc_ref[d, pl.ds(pl.multiple_of(t * GLA_TILE + c * CHUNK, CHUNK), CHUNK), :] = o

    if zero_init:
        sf_ref[...] = jnp.zeros(sf_ref.shape, F32)
        sb_ref[...] = jnp.zeros(sb_ref.shape, F32)
    else:
        sf_ref[...] = sf0_ref[...]
        sb_ref[...] = sb0_ref[...]

    def main_body(t, carry):
        tile_dir(t, 0)
        tile_dir(n_tiles - 1 - t, 1)
        return carry

    lax.fori_loop(0, n_tiles, main_body, 0)

    gn = gn_ref[...]

    def epilogue_body(t, carry):
        rows = tile_rows(t)
        for hd in range(GLA_HEADS):
            vs = slice(hd * GLA_DV, (hd + 1) * GLA_DV)
            o = _rms(oacc_ref[0, rows, vs] + oacc_ref[1, rows, vs], gn)
            go = go_ref[rows, vs]
            o_ref[rows, vs] = (o * (go * _sigmoid(go))).astype(BF16)
        return carry

    lax.fori_loop(0, n_tiles, epilogue_body, 0)


def _gla(gq, gk, gv, gf, gb, go, init_states, gn):
    b, t, _ = gq.shape
    n_tiles = t // GLA_TILE
    zero_init = init_states is None
    seq = lambda c: pl.BlockSpec((None, t, c), lambda i: (i, 0, 0))
    st = pl.BlockSpec((None, GLA_HEADS, GLA_DK, GLA_DV), lambda i: (i, 0, 0, 0))
    in_specs = [seq(GQK), seq(GQK), seq(GV), seq(GQK), seq(GQK), seq(GV)]
    args = [gq, gk, gv, gf, gb, go]
    if not zero_init:
        in_specs += [st, st]
        args += list(init_states)
    in_specs.append(_const_spec(gn.shape))
    args.append(gn)
    return pl.pallas_call(
        functools.partial(_gla_kernel, n_tiles=n_tiles, zero_init=zero_init),
        out_shape=[jax.ShapeDtypeStruct((b, t, GV), BF16),
                   jax.ShapeDtypeStruct((b, GLA_HEADS, GLA_DK, GLA_DV), F32),
                   jax.ShapeDtypeStruct((b, GLA_HEADS, GLA_DK, GLA_DV), F32)],
        grid=(b,),
        in_specs=in_specs,
        out_specs=[pl.BlockSpec((None, t, GV), lambda i: (i, 0, 0)), st, st],
        scratch_shapes=[pltpu.VMEM((2, t, GV), F32),
                        pltpu.VMEM((GQK, GLA_TILE), BF16),
                        pltpu.VMEM((2, GLA_TILE, GLA_TILE), F32),
                        pltpu.VMEM((GLA_HEADS * GLA_TILE, GQK), BF16)],
        name="gla_%d" % t,
        compiler_params=pltpu.CompilerParams(dimension_semantics=("arbitrary",),
                                             vmem_limit_bytes=VMEM_LIMIT),
    )(*args)


def _ffn_kernel(xp_ref, xs_ref, atp_ref, ats_ref, glp_ref, gls_ref, mod_ref, wout_ref, nf_ref,
                wa_ref, wg_ref, wfo_ref, fn_ref, yp_ref, ys_ref,
                wout_s, wfi_s, wfo_s, act_ref, *, ctx_tiles, tiles_per_seq):
    s = pl.program_id(0)

    @pl.when(s < N_FF_CHUNKS)
    def _():
        wfi_s[s, :, 0:FF_CHUNK] = wa_ref[...].astype(BF16)
        wfi_s[s, :, FF_CHUNK:2 * FF_CHUNK] = wg_ref[...].astype(BF16)
        wfo_s[pl.ds(pl.multiple_of(s * FF_CHUNK, FF_CHUNK), FF_CHUNK), :] = wfo_ref[...].astype(BF16)

        @pl.when(s < N_WOUT_CHUNKS)
        def _():
            wout_s[pl.ds(pl.multiple_of(s * WOUT_CHUNK, WOUT_CHUNK), WOUT_CHUNK), :] = (
                wout_ref[...].astype(BF16))

    def tile(x_ref, at_ref, gl_ref, y_ref, mod_row):
        _, _, gt1, sh2, sc2, gt2 = _mod_rows(mod_ref, mod_row)
        mix = (jnp.dot(at_ref[...], wout_s[0:VALL, :], preferred_element_type=F32)
               + jnp.dot(gl_ref[...], wout_s[VALL:, :], preferred_element_type=F32))
        x1 = x_ref[...] + gt1 * mix
        h2 = (_rms(x1, nf_ref[...]) * (1.0 + sc2) + sh2).astype(BF16)
        for j in range(N_FF_CHUNKS):
            ag = jnp.dot(h2, wfi_s[j], preferred_element_type=F32)
            a = ag[:, :FF_CHUNK]
            act_ref[:, j * FF_CHUNK:(j + 1) * FF_CHUNK] = (a * _sigmoid(a) * ag[:, FF_CHUNK:]).astype(BF16)
        ff = jnp.dot(act_ref[...], wfo_s[...], preferred_element_type=F32)
        x2 = x1 + gt2 * ff
        y_ref[...] = _rms(x2, fn_ref[...])

    t = s - N_FF_CHUNKS

    @pl.when((t >= 0) & (t < ctx_tiles))
    def _():
        tile(xp_ref, atp_ref, glp_ref, yp_ref, 0)

    @pl.when(t >= ctx_tiles)
    def _():
        tile(xs_ref, ats_ref, gls_ref, ys_ref, 1 + (t - ctx_tiles) // tiles_per_seq)


def _ffn(xp, xs, attn_p, attn_s, gla_p, gla_s, mod, w_out, nf, w_ffn_in, w_ffn_out, fn, tm, tiles_per_seq):
    d = xp.shape[1]
    ctx_tiles = xp.shape[0] // tm
    lat_tiles = xs.shape[0] // tm
    nw = N_FF_CHUNKS
    ctx_map = lambda s: (jnp.clip(s - nw, 0, ctx_tiles - 1), 0)
    lat_map = lambda s: (jnp.clip(s - nw - ctx_tiles, 0, lat_tiles - 1), 0)
    tile = lambda c, m: pl.BlockSpec((tm, c), m)
    return pl.pallas_call(
        functools.partial(_ffn_kernel, ctx_tiles=ctx_tiles, tiles_per_seq=tiles_per_seq),
        out_shape=[jax.ShapeDtypeStruct(xp.shape, F32), jax.ShapeDtypeStruct(xs.shape, F32)],
        grid=(nw + ctx_tiles + lat_tiles,),
        in_specs=[tile(d, ctx_map), tile(d, lat_map), tile(VALL, ctx_map), tile(VALL, lat_map),
                  tile(GV, ctx_map), tile(GV, lat_map), _const_spec(mod.shape),
                  pl.BlockSpec((None, WOUT_CHUNK, d), lambda s: (0, jnp.minimum(s, N_WOUT_CHUNKS - 1), 0)),
                  _const_spec(nf.shape),
                  pl.BlockSpec((None, d, FF_CHUNK), lambda s: (0, 0, jnp.minimum(s, nw - 1))),
                  pl.BlockSpec((None, d, FF_CHUNK), lambda s: (0, 0, nw + jnp.minimum(s, nw - 1))),
                  pl.BlockSpec((None, FF_CHUNK, d), lambda s: (0, jnp.minimum(s, nw - 1), 0)),
                  _const_spec(fn.shape)],
        out_specs=[tile(d, ctx_map), tile(d, lat_map)],
        scratch_shapes=[pltpu.VMEM((D_MIX, d), BF16),
                        pltpu.VMEM((N_FF_CHUNKS, d, 2 * FF_CHUNK), BF16),
                        pltpu.VMEM((D_FF, d), BF16),
                        pltpu.VMEM((tm, D_FF), BF16)],
        name="out_ffn",
        compiler_params=pltpu.CompilerParams(dimension_semantics=("arbitrary",),
                                             vmem_limit_bytes=VMEM_LIMIT),
    )(xp, xs, attn_p, attn_s, gla_p, gla_s, mod, w_out, nf, w_ffn_in, w_ffn_in, w_ffn_out, fn)


def _rope_tables(n_tokens):
    t = np.arange(n_tokens)
    row = (t // GRID_W).astype(np.float32)
    col = (t % GRID_W).astype(np.float32)
    half = MLA_ROPE // 2
    inv = (np.float32(ROPE_BASE) ** (-np.arange(0, half, 2, dtype=np.float32) / np.float32(half))).astype(np.float32)
    ang_r = row[:, None] * inv
    ang_c = col[:, None] * inv
    ang = np.concatenate([ang_r, ang_r, ang_c, ang_c], axis=-1).astype(np.float32)
    cos, sin = np.cos(ang), np.sin(ang)
    first = (np.arange(MLA_ROPE) % half) < (half // 2)
    cos_t = np.ones((n_tokens, LANES), np.float32)
    sa_t = np.zeros((n_tokens, LANES), np.float32)
    sb_t = np.zeros((n_tokens, LANES), np.float32)
    cos_t[:, ROPE_LANE0:ROPE_LANE0 + MLA_ROPE] = cos
    sa_t[:, ROPE_LANE0:ROPE_LANE0 + MLA_ROPE] = np.where(first, -sin, 0.0)
    sb_t[:, ROPE_LANE0:ROPE_LANE0 + MLA_ROPE] = np.where(first, 0.0, sin)
    return jnp.asarray(cos_t), jnp.asarray(sa_t), jnp.asarray(sb_t)


def kernel(x_prompt, x_sample, cache_kv_latent, cache_k_rope, state_gla_fwd, state_gla_bwd, c, c_ctx, w_ada, b_ada, norm_attn, w_in, mla_q_norm, w_uq, mla_kv_norm, w_ukv, w_gate_f, b_gate_f, w_gate_b, b_gate_b, gla_norm, w_out, norm_ffn, w_ffn_in, w_ffn_out, final_norm):
    batch, seq, d = x_prompt.shape
    dec_batch, dec_seq, _ = x_sample.shape
    assert w_ada.shape[0] == 1 and w_in.shape[-1] == W_COLS and w_ffn_in.shape[-1] == 2 * D_FF
    l = 0

    cond8 = jnp.concatenate([c_ctx[None, :], c, jnp.zeros((8 - 1 - dec_batch, d), F32)], axis=0)
    mod = _ada(cond8, w_ada[l], b_ada[l])

    win, wuq, wk, wvt, wg = _prep_in_weights(w_in, w_uq, w_ukv, w_gate_f, w_gate_b)
    bg = jnp.concatenate([b_gate_f[l], b_gate_b[l]]).reshape(1, 2 * GQK)
    in_w = (norm_attn[l].reshape(1, d), win, mla_q_norm[l].reshape(1, Q_LORA), wuq,
            mla_kv_norm[l].reshape(1, KV_LORA), wk, wvt, wg, bg)
    gn = gla_norm[l].reshape(1, GLA_DV)
    tm, tm_ffn = 512, 512
    r3 = lambda a, b_, t: a.reshape(b_, t, a.shape[-1])

    xp = x_prompt.reshape(batch * seq, d)
    (q, k, vt, gq, gk, gv, gf, gb, go, ckv, kr) = _inproj(xp, mod, lambda i: 0, in_w, None, tm, seq // tm)
    attn_p = _attention(r3(q, batch, seq), r3(k, batch, seq), vt, None, min(seq, 256))
    gla_p, sf, sb = _gla(r3(gq, batch, seq), r3(gk, batch, seq), r3(gv, batch, seq), r3(gf, batch, seq),
                         r3(gb, batch, seq), r3(go, batch, seq), None, gn)

    xs = x_sample.reshape(dec_batch * dec_seq, d)
    tiles = dec_seq // tm
    (q, k, vt, gq, gk, gv, gf, gb, go) = _inproj(xs, mod, lambda i: 1 + i // tiles, in_w,
                                                  _rope_tables(dec_seq), tm, tiles)
    kr_pad = jnp.pad(cache_k_rope[:, l], ((0, 0), (0, 0), (ROPE_LANE0, LANES - ROPE_LANE0 - MLA_ROPE)))
    kc, vct = _decomp(cache_kv_latent[:, l], kr_pad, wk, wvt)
    attn_s = _attention(r3(q, dec_batch, dec_seq), r3(k, dec_batch, dec_seq), vt, (kc, vct), 256)
    gla_s, _, _ = _gla(r3(gq, dec_batch, dec_seq), r3(gk, dec_batch, dec_seq), r3(gv, dec_batch, dec_seq),
                       r3(gf, dec_batch, dec_seq), r3(gb, dec_batch, dec_seq), r3(go, dec_batch, dec_seq),
                       (state_gla_fwd[:, l].astype(F32), state_gla_bwd[:, l].astype(F32)), gn)

    flat = lambda a: a.reshape(-1, a.shape[-1])
    y_prompt, y_sample = _ffn(xp, xs, flat(attn_p), flat(attn_s), flat(gla_p), flat(gla_s), mod,
                              w_out, norm_ffn[l].reshape(1, d), w_ffn_in, w_ffn_out,
                              final_norm.reshape(1, d), tm_ffn, dec_seq // tm_ffn)
    y_prompt = y_prompt.reshape(batch, seq, d)
    y_sample = y_sample.reshape(dec_batch, dec_seq, d)

    new_kv_latent = ckv.reshape(batch, 1, seq, KV_LORA)
    new_k_rope = kr.reshape(batch, 1, seq, MLA_ROPE)
    new_state_fwd = sf.reshape(batch, 1, GLA_HEADS, GLA_DK, GLA_DV).astype(x_prompt.dtype)
    new_state_bwd = sb.reshape(batch, 1, GLA_HEADS, GLA_DK, GLA_DV).astype(x_prompt.dtype)
    return (y_prompt, y_sample, new_kv_latent, new_k_rope, new_state_fwd, new_state_bwd)
```

```python
import functools

import numpy as np
import jax
import jax.numpy as jnp
from jax import lax
from jax.experimental import pallas as pl
from jax.experimental.pallas import tpu as pltpu

F32 = jnp.float32
BF16 = jnp.bfloat16

GRID_W = 64
MLA_HEADS = 8
MLA_NOPE = 64
MLA_ROPE = 32
MLA_QK = MLA_NOPE + MLA_ROPE
MLA_V = 64
Q_LORA = 384
KV_LORA = 256
GLA_HEADS = 4
GLA_DK = 64
GLA_DV = 128
GATE_RANK = 16
GATE_NORM = 16.0
CHUNK = 64
D_FF = 2816
ROPE_BASE = 10000.0
EPS = 1e-6
LOG2_E = 1.4426950408889634

LANES = 128
HEAD_PAD = LANES
ROPE_LANE0 = MLA_NOPE
GQK = GLA_HEADS * GLA_DK
GV = GLA_HEADS * GLA_DV
QPAD = MLA_HEADS * HEAD_PAD
VALL = MLA_HEADS * MLA_V
ONES_ROWS = 16
KEY_BLOCK = 512

W_KR = Q_LORA + KV_LORA
W_GQ = W_KR + MLA_ROPE
W_GF = W_GQ + 2 * GQK + GV
W_GO = W_GF + 2 * GATE_RANK
W_COLS = W_GO + GV

Z_Q = 0
Z_KV = Z_Q + Q_LORA
Z_GQ = Z_KV + KV_LORA
Z_GK = Z_GQ + GQK
Z_GV = Z_GK + GQK
Z_GO = Z_GV + GV
Z_MISC = Z_GO + GV
Z_COLS = Z_MISC + LANES

FF_CHUNK = 256
N_FF_CHUNKS = D_FF // FF_CHUNK
D_MIX = VALL + GV
WOUT_CHUNK = 128
N_WOUT_CHUNKS = D_MIX // WOUT_CHUNK

GLA_TILE = 256
CHUNKS_PER_TILE = GLA_TILE // CHUNK
ADA_ROWS = 128
GLA_CTX_SEQS = 4

VMEM_LIMIT = 56 * 1024 * 1024

_NT = (((1,), (1,)), ((), ()))


def _rms(x, w):
    return x * lax.rsqrt(jnp.mean(x * x, axis=-1, keepdims=True) + EPS) * w


def _sigmoid(x):
    return 1.0 / (1.0 + jnp.exp(-x))


def _log_sigmoid(x):
    return jnp.minimum(x, 0.0) - jnp.log1p(jnp.exp(-jnp.abs(x)))


def _const_spec(shape):
    nd = len(shape)
    return pl.BlockSpec(shape, lambda *_: (0,) * nd, pipeline_mode=pl.Buffered(1))


def _mod_rows(mod_ref, r):
    return [mod_ref[k, pl.ds(r, 1), :] for k in range(6)]


def _ada_kernel(cond_ref, w_ref, b_ref, o_ref):
    k = pl.program_id(0)
    d = o_ref.shape[2]
    c = cond_ref[...]
    s = (c * _sigmoid(c)).astype(BF16)
    part = jnp.dot(s, w_ref[...].astype(BF16), preferred_element_type=F32)
    for j in range(o_ref.shape[0]):
        sl = slice(j * d, (j + 1) * d)

        @pl.when(k == 0)
        def _():
            o_ref[j] = part[:, sl] + b_ref[:, sl]

        @pl.when(k > 0)
        def _():
            o_ref[j] += part[:, sl]


def _ada(cond8, w_ada, b_ada):
    d = w_ada.shape[0]
    n = w_ada.shape[1]
    steps = d // ADA_ROWS
    cond_k = cond8.reshape(8, steps, ADA_ROWS).transpose(1, 0, 2)
    return pl.pallas_call(
        _ada_kernel,
        out_shape=jax.ShapeDtypeStruct((n // d, 8, d), F32),
        grid=(steps,),
        in_specs=[pl.BlockSpec((None, 8, ADA_ROWS), lambda k: (k, 0, 0)),
                  pl.BlockSpec((ADA_ROWS, n), lambda k: (k, 0)),
                  pl.BlockSpec((1, n), lambda k: (0, 0))],
        out_specs=pl.BlockSpec((n // d, 8, d), lambda k: (0, 0, 0)),
        name="ada_mod",
        compiler_params=pltpu.CompilerParams(dimension_semantics=("arbitrary",)),
    )(cond_k, w_ada, b_ada.reshape(1, n))


def _prep_kernel(wint_ref, wuq_ref, wukv_ref, wgf_ref, wgb_ref, win_o, wuq_o, wk_o, wvt_o, wg_o):
    cols = wint_ref.shape[1]
    for dst, src, n in ((Z_Q, 0, W_KR), (Z_GQ, W_GQ, W_GF - W_GQ), (Z_GO, W_GO, GV)):
        win_o[:, dst:dst + n] = wint_ref[src:src + n, :].T.astype(BF16)
    z32 = jnp.zeros((32, cols), F32)
    misc_t = jnp.concatenate([wint_ref[W_GF:W_GO, :], z32, wint_ref[W_KR:W_GQ, :], z32], axis=0)
    win_o[:, Z_MISC:Z_COLS] = misc_t.T.astype(BF16)

    u = wuq_ref[...]
    zq = jnp.zeros((u.shape[0], HEAD_PAD - MLA_QK), F32)
    for hd in range(MLA_HEADS):
        blk = jnp.concatenate([u[:, hd * MLA_QK:(hd + 1) * MLA_QK], zq], axis=1)
        wuq_o[:, hd * HEAD_PAD:(hd + 1) * HEAD_PAD] = blk.astype(BF16)

    @pl.when(pl.program_id(0) == 0)
    def _():
        kv = wukv_ref[...]
        per = MLA_NOPE + MLA_V
        lane = lax.broadcasted_iota(jnp.int32, (kv.shape[0], per), 1)
        for hd in range(MLA_HEADS):
            blk = kv[:, hd * per:(hd + 1) * per]
            wk_o[:, hd * HEAD_PAD:(hd + 1) * HEAD_PAD] = jnp.where(lane < MLA_NOPE, blk, 0.0).astype(BF16)
        wv = jnp.concatenate([kv[:, hd * per + MLA_NOPE:(hd + 1) * per] for hd in range(MLA_HEADS)], axis=1)
        wvt_o[...] = wv.T.astype(BF16)

        wg_o[...] = jnp.zeros(wg_o.shape, BF16)
        wg_o[0:GATE_RANK, 0:GQK] = wgf_ref[...].astype(BF16)
        wg_o[GATE_RANK:2 * GATE_RANK, GQK:2 * GQK] = wgb_ref[...].astype(BF16)


def _prep_in_weights(w_in, w_uq, w_ukv, w_gate_f, w_gate_b):
    d = w_in.shape[1]
    steps = 4
    w_in_t = jnp.swapaxes(w_in, 1, 2)
    rb3 = lambda r, c: pl.BlockSpec((None, r // steps, c), lambda i: (0, i, 0))
    rb = lambda r, c: pl.BlockSpec((r // steps, c), lambda i: (i, 0))
    full3 = lambda shape: pl.BlockSpec((None,) + tuple(shape[1:]), lambda i: (0, 0, 0))
    full = lambda shape: pl.BlockSpec(shape, lambda i: (0, 0))
    return pl.pallas_call(
        _prep_kernel,
        out_shape=[jax.ShapeDtypeStruct((d, Z_COLS), BF16),
                   jax.ShapeDtypeStruct((Q_LORA, QPAD), BF16),
                   jax.ShapeDtypeStruct((KV_LORA, QPAD), BF16),
                   jax.ShapeDtypeStruct((VALL, KV_LORA), BF16),
                   jax.ShapeDtypeStruct((LANES, 2 * GQK), BF16)],
        grid=(steps,),
        in_specs=[pl.BlockSpec((None, W_COLS, d // steps), lambda i: (0, 0, i)),
                  rb3(Q_LORA, MLA_HEADS * MLA_QK), full3(w_ukv.shape),
                  full3(w_gate_f.shape), full3(w_gate_b.shape)],
        out_specs=[rb(d, Z_COLS), rb(Q_LORA, QPAD), full((KV_LORA, QPAD)), full((VALL, KV_LORA)),
                   full((LANES, 2 * GQK))],
        name="weight_prep",
        compiler_params=pltpu.CompilerParams(dimension_semantics=("arbitrary",)),
    )(w_in_t, w_uq, w_ukv, w_gate_f, w_gate_b)


def _inproj_kernel(*refs, latent, mod_row):
    (x_ref, mod_ref, nw_ref, win_ref, qn_ref, wuq_ref, kvn_ref, wk_ref, wvt_ref, wg_ref, bg_ref) = refs[:11]
    if latent:
        cos_ref, sa_ref, sb_ref = refs[11:14]
        outs = refs[14:]
    else:
        outs = refs[11:]
    q_ref, k_ref, vt_ref, gq_ref, gk_ref, gv_ref, gf_ref, gb_ref, go_ref = outs[:9]

    sh1, sc1 = _mod_rows(mod_ref, mod_row(pl.program_id(0)))[:2]
    x = x_ref[...]
    h = _rms(x, nw_ref[...]) * (1.0 + sc1) + sh1
    z = jnp.dot(h.astype(BF16), win_ref[...], preferred_element_type=F32)

    qn = _rms(z[:, Z_Q:Z_Q + Q_LORA], qn_ref[...])
    q = jnp.dot(qn.astype(BF16), wuq_ref[...], preferred_element_type=F32)
    ckv = _rms(z[:, Z_KV:Z_KV + KV_LORA], kvn_ref[...])
    ckv_b = ckv.astype(BF16)
    kn = jnp.dot(ckv_b, wk_ref[...], preferred_element_type=F32)
    vt_ref[...] = lax.dot_general(wvt_ref[...], ckv_b, _NT,
                                  preferred_element_type=F32).astype(BF16)
    misc = z[:, Z_MISC:Z_MISC + LANES]

    if latent:
        cos, sa, sb = cos_ref[...], sa_ref[...], sb_ref[...]

        def rope(t):
            return t * cos + pltpu.roll(t, LANES - 8, 1) * sa + pltpu.roll(t, 8, 1) * sb
    else:
        def rope(t):
            return t

    scale = MLA_QK ** -0.5 * LOG2_E
    lane = lax.broadcasted_iota(jnp.int32, misc.shape, 1)
    in_rope = (lane >= ROPE_LANE0) & (lane < ROPE_LANE0 + MLA_ROPE)
    krope = rope(misc)
    for hd in range(MLA_HEADS):
        sl = slice(hd * HEAD_PAD, (hd + 1) * HEAD_PAD)
        q_ref[:, sl] = (rope(q[:, sl]) * scale).astype(BF16)
        k_ref[:, sl] = jnp.where(in_rope, krope, kn[:, sl]).astype(BF16)

    gq_ref[...] = z[:, Z_GQ:Z_GQ + GQK]
    gk_ref[...] = z[:, Z_GK:Z_GK + GQK]
    gv_ref[...] = z[:, Z_GV:Z_GV + GV].astype(BF16)
    go_ref[...] = z[:, Z_GO:Z_GO + GV]
    gpre = jnp.dot(misc.astype(BF16), wg_ref[...], preferred_element_type=F32) + bg_ref[...]
    gate = _log_sigmoid(gpre) * (1.0 / GATE_NORM)
    gf_ref[...] = gate[:, :GQK]
    gb_ref[...] = gate[:, GQK:]

    if not latent:
        ckv_ref, kr_ref = outs[9:]
        ckv_ref[...] = ckv
        kr_ref[...] = misc[:, ROPE_LANE0:ROPE_LANE0 + MLA_ROPE]


def _inproj(x2d, mod, mod_row, weights, rope_tabs, tm, tiles_per_seq):
    n_tok, d = x2d.shape
    latent = rope_tabs is not None
    nw, win, qn, wuq, kvn, wk, wvt, wg, bg = weights
    row = lambda i: (i, 0)
    in_specs = [pl.BlockSpec((tm, d), row), _const_spec(mod.shape),
                _const_spec(nw.shape), _const_spec(win.shape), _const_spec(qn.shape),
                _const_spec(wuq.shape), _const_spec(kvn.shape), _const_spec(wk.shape),
                _const_spec(wvt.shape), _const_spec(wg.shape), _const_spec(bg.shape)]
    args = [x2d, mod, nw, win, qn, wuq, kvn, wk, wvt, wg, bg]
    if latent:
        tab = pl.BlockSpec((tm, LANES), lambda i: (i % tiles_per_seq, 0))
        in_specs += [tab, tab, tab]
        args += list(rope_tabs)
    out_cols = [(QPAD, BF16), (QPAD, BF16), None, (GQK, F32), (GQK, F32), (GV, BF16),
                (GQK, F32), (GQK, F32), (GV, F32)]
    if not latent:
        out_cols += [(KV_LORA, F32), (MLA_ROPE, F32)]
    out_shape = [jax.ShapeDtypeStruct((n_tok, oc[0]), oc[1]) if oc else
                 jax.ShapeDtypeStruct((VALL, n_tok), BF16) for oc in out_cols]
    out_specs = [pl.BlockSpec((tm, oc[0]), row) if oc else
                 pl.BlockSpec((VALL, tm), lambda i: (0, i)) for oc in out_cols]
    return pl.pallas_call(
        functools.partial(_inproj_kernel, latent=latent, mod_row=mod_row),
        out_shape=out_shape,
        grid=(n_tok // tm,),
        in_specs=in_specs,
        out_specs=out_specs,
        name="inproj_lat" if latent else "inproj_ctx",
        compiler_params=pltpu.CompilerParams(dimension_semantics=("arbitrary",),
                                             vmem_limit_bytes=VMEM_LIMIT),
    )(*args)


def _decomp_kernel(ckv_ref, kr_ref, wk_ref, wvt_ref, k_ref, vt_ref):
    ckv_b = ckv_ref[...].astype(BF16)
    kn = jnp.dot(ckv_b, wk_ref[...], preferred_element_type=F32)
    kr = kr_ref[...]
    lane = lax.broadcasted_iota(jnp.int32, kr.shape, 1)
    in_rope = (lane >= ROPE_LANE0) & (lane < ROPE_LANE0 + MLA_ROPE)
    for hd in range(MLA_HEADS):
        sl = slice(hd * HEAD_PAD, (hd + 1) * HEAD_PAD)
        k_ref[:, sl] = jnp.where(in_rope, kr, kn[:, sl]).astype(BF16)
    vt_ref[...] = lax.dot_general(wvt_ref[...], ckv_b, _NT, preferred_element_type=F32).astype(BF16)


def _decomp(ckv, kr_pad, wk, wvt):
    b, s, _ = ckv.shape
    return pl.pallas_call(
        _decomp_kernel,
        out_shape=[jax.ShapeDtypeStruct((b, s, QPAD), BF16), jax.ShapeDtypeStruct((VALL, b * s), BF16)],
        grid=(b,),
        in_specs=[pl.BlockSpec((None, s, KV_LORA), lambda i: (i, 0, 0)),
                  pl.BlockSpec((None, s, LANES), lambda i: (i, 0, 0)),
                  _const_spec(wk.shape), _const_spec(wvt.shape)],
        out_specs=[pl.BlockSpec((None, s, QPAD), lambda i: (i, 0, 0)),
                   pl.BlockSpec((VALL, s), lambda i: (0, i))],
        name="ctx_decompress",
        compiler_params=pltpu.CompilerParams(dimension_semantics=("arbitrary",)),
    )(ckv, kr_pad, wk, wvt)


def _attn_kernel(*refs, has_ctx):
    if has_ctx:
        q_ref, kc_ref, vct_ref, k_ref, vt_ref, o_ref, st_ref = refs
    else:
        q_ref, k_ref, vt_ref, o_ref, st_ref = refs
    tq = q_ref.shape[0]
    blocks, row0 = [], 0
    for kr, vr in ([(kc_ref, vct_ref)] if has_ctx else []) + [(k_ref, vt_ref)]:
        n_keys = kr.shape[0]
        size = min(KEY_BLOCK, n_keys)
        for r in range(0, n_keys, size):
            blocks.append((kr, vr, r, size, row0))
            row0 += size

    col_max = [None] * MLA_HEADS
    pair = []
    for stage in range(MLA_HEADS + 1):
        ha, hb = stage, stage - 1
        run_max = None
        acc = jnp.zeros((MLA_V + ONES_ROWS, tq), F32)
        for kr, vr, r0, size, srow in blocks:
            rows = slice(srow, srow + size)
            if ha < MLA_HEADS:
                sl = slice(ha * HEAD_PAD, (ha + 1) * HEAD_PAD)
                st = lax.dot_general(kr[r0:r0 + size, sl], q_ref[:, sl], _NT,
                                     preferred_element_type=F32)
                st_ref[ha % 2, rows, :] = st
                blk_max = jnp.max(st.reshape(size // 8, 8, tq), axis=0)
                run_max = blk_max if run_max is None else jnp.maximum(run_max, blk_max)
            if hb >= 0:
                p = jnp.exp2(st_ref[hb % 2, rows, :] - col_max[hb]).astype(BF16)
                v_aug = jnp.concatenate([vr[hb * MLA_V:(hb + 1) * MLA_V, r0:r0 + size],
                                         jnp.ones((ONES_ROWS, size), BF16)], axis=0)
                acc = acc + jnp.dot(v_aug, p, preferred_element_type=F32)
        if ha < MLA_HEADS:
            col_max[ha] = jnp.max(run_max, axis=0, keepdims=True)
        if hb >= 0:
            pair.append(acc[:MLA_V, :] / acc[MLA_V:MLA_V + 1, :])
            if len(pair) == 2:
                o_ref[:, (hb - 1) * MLA_V:(hb + 1) * MLA_V] = jnp.concatenate(pair, axis=0).T.astype(BF16)
                pair = []


def _attention(q, k, vt, ctx_kv, tq):
    b, t, _ = q.shape
    has_ctx = ctx_kv is not None
    in_specs = [pl.BlockSpec((None, tq, QPAD), lambda i, j: (i, j, 0))]
    args = [q]
    if has_ctx:
        kc, vct = ctx_kv
        s = kc.shape[1]
        in_specs += [pl.BlockSpec((None, s, QPAD), lambda i, j: (i, 0, 0)),
                     pl.BlockSpec((VALL, s), lambda i, j: (0, i))]
        args += [kc, vct]
    in_specs += [pl.BlockSpec((None, t, QPAD), lambda i, j: (i, 0, 0)),
                 pl.BlockSpec((VALL, t), lambda i, j: (0, i))]
    args += [k, vt]
    return pl.pallas_call(
        functools.partial(_attn_kernel, has_ctx=has_ctx),
        out_shape=jax.ShapeDtypeStruct((b, t, VALL), BF16),
        grid=(b, t // tq),
        in_specs=in_specs,
        out_specs=pl.BlockSpec((None, tq, VALL), lambda i, j: (i, j, 0)),
        scratch_shapes=[pltpu.VMEM((2, t + (s if has_ctx else 0), tq), F32)],
        name="mla_attn_lat" if has_ctx else "mla_attn_ctx",
        compiler_params=pltpu.CompilerParams(dimension_semantics=("arbitrary", "arbitrary"),
                                             vmem_limit_bytes=VMEM_LIMIT),
    )(*args)


def _gla_kernel(*refs, n_tiles, n_seqs, zero_init):
    gq_ref, gk_ref, gv_ref, gf_ref, gb_ref, go_ref = refs[:6]
    if zero_init:
        gn_ref, o_ref, sf_ref, sb_ref, oacc_ref, bdqk_ref, tri_ref, hm_ref = refs[6:]
    else:
        (sf0_ref, sb0_ref, gn_ref, o_ref, sf_ref, sb_ref, oacc_ref,
         bdqk_ref, tri_ref, hm_ref) = refs[6:]
    g_refs = (gf_ref, gb_ref)
    state_refs = (sf_ref, sb_ref)

    @pl.when(pl.program_id(0) == 0)
    def _():
        ri = lax.broadcasted_iota(jnp.int32, (GLA_TILE, GLA_TILE), 0)
        ci = lax.broadcasted_iota(jnp.int32, (GLA_TILE, GLA_TILE), 1)
        same_chunk = (ri // CHUNK) == (ci // CHUNK)
        bdqk_ref[...] = jnp.where(same_chunk, 1.0, 0.0).astype(BF16)
        tri_ref[0] = jnp.where(same_chunk & (ri >= ci), 1.0, 0.0)
        tri_ref[1] = jnp.where(same_chunk & (ci >= ri), 1.0, 0.0)
        hm_ref[...] = jnp.where(
            lax.broadcasted_iota(jnp.int32, (GLA_HEADS * GLA_TILE, GQK), 0) // GLA_TILE
            == lax.broadcasted_iota(jnp.int32, (GLA_HEADS * GLA_TILE, GQK), 1) // GLA_DK,
            1.0, 0.0).astype(BF16)

    row8 = lax.broadcasted_iota(jnp.int32, (8, GQK), 0)

    def tile_rows(t):
        return pl.ds(pl.multiple_of(t * GLA_TILE, GLA_TILE), GLA_TILE)

    def total_row(c, d):
        return c * CHUNK + (CHUNK - 1 if d == 0 else 0)

    def tile_dir(b, t, d):
        rows = tile_rows(t)
        g = g_refs[d][b, rows, :]
        g_hi = g.astype(BF16)
        g_lo = (g - g_hi.astype(F32)).astype(BF16)
        tri_b = tri_ref[d].astype(BF16)
        cum = (jnp.dot(tri_b, g_hi, preferred_element_type=F32)
               + jnp.dot(tri_b, g_lo, preferred_element_type=F32))
        totals = [cum[total_row(c, d):total_row(c, d) + 1, :] for c in range(CHUNKS_PER_TILE)]
        tot8 = jnp.zeros((8, GQK), F32)
        for c in range(CHUNKS_PER_TILE):
            tot8 = jnp.where(row8 == c, totals[c], tot8)
        dec_t = jnp.concatenate([jnp.exp(tot8), jnp.zeros((LANES - 8, GQK), F32)], axis=0).T
        q = gq_ref[b, rows, :] * (GLA_DK ** -0.5)
        k = gk_ref[b, rows, :]
        v = gv_ref[b, rows, :]
        tot = jnp.concatenate([jnp.broadcast_to(tc, (CHUNK, GQK)) for tc in totals], axis=0)
        qe = (q * jnp.exp(cum)).astype(BF16)
        ke = (k * jnp.exp(-cum)).astype(BF16)
        kd_t = (k * jnp.exp(tot - cum)).T.astype(BF16)
        bd_qk = bdqk_ref[...] > 0
        tri = tri_ref[d] > 0

        qm = jnp.where(hm_ref[...] > 0, jnp.tile(qe, (GLA_HEADS, 1)), 0.0)
        att = lax.dot_general(qm, ke, _NT, preferred_element_type=F32)

        intra, upd = [], []
        for hd in range(GLA_HEADS):
            vh = v[:, hd * GLA_DV:(hd + 1) * GLA_DV]
            a_h = jnp.where(tri, att[hd * GLA_TILE:(hd + 1) * GLA_TILE, :], 0.0).astype(BF16)
            intra.append(jnp.dot(a_h, vh, preferred_element_type=F32))
            kd_h = jnp.tile(kd_t[hd * GLA_DK:(hd + 1) * GLA_DK, :], (CHUNKS_PER_TILE, 1))
            upd.append(jnp.dot(jnp.where(bd_qk, kd_h, 0.0), vh, preferred_element_type=F32))

        state = [state_refs[d][b, hd] for hd in range(GLA_HEADS)]
        order = range(CHUNKS_PER_TILE) if d == 0 else range(CHUNKS_PER_TILE - 1, -1, -1)
        seen = {}
        for c in order:
            seen[c] = jnp.concatenate(state, axis=0).astype(BF16)
            decay = jnp.broadcast_to(dec_t[:, c:c + 1], (GQK, GLA_DV))
            for hd in range(GLA_HEADS):
                ks = slice(hd * GLA_DK, (hd + 1) * GLA_DK)
                state[hd] = decay[ks, :] * state[hd] + upd[hd][c * CHUNK:(c + 1) * CHUNK, :]
        for hd in range(GLA_HEADS):
            state_refs[d][b, hd] = state[hd]

        for c in range(CHUNKS_PER_TILE):
            cr = slice(c * CHUNK, (c + 1) * CHUNK)
            q_c = jnp.concatenate([qm[hd * GLA_TILE + c * CHUNK:hd * GLA_TILE + (c + 1) * CHUNK, :]
                                   for hd in range(GLA_HEADS)], axis=0)
            inter = jnp.dot(q_c, seen[c], preferred_element_type=F32)
            o = jnp.concatenate([intra[hd][cr, :] + inter[hd * CHUNK:(hd + 1) * CHUNK, :]
                                 for hd in range(GLA_HEADS)], axis=1)
            oacc_ref[d, b, pl.ds(pl.multiple_of(t * GLA_TILE + c * CHUNK, CHUNK), CHUNK), :] = o

    if zero_init:
        sf_ref[...] = jnp.zeros(sf_ref.shape, F32)
        sb_ref[...] = jnp.zeros(sb_ref.shape, F32)
    else:
        sf_ref[...] = sf0_ref[...]
        sb_ref[...] = sb0_ref[...]

    def main_body(t, carry):
        for b in range(n_seqs):
            tile_dir(b, t, 0)
            tile_dir(b, n_tiles - 1 - t, 1)
        return carry

    lax.fori_loop(0, n_tiles, main_body, 0)

    gn = gn_ref[...]

    def epilogue_body(t, carry):
        rows = tile_rows(t)
        for b in range(n_seqs):
            for hd in range(GLA_HEADS):
                vs = slice(hd * GLA_DV, (hd + 1) * GLA_DV)
                o = _rms(oacc_ref[0, b, rows, vs] + oacc_ref[1, b, rows, vs], gn)
                go = go_ref[b, rows, vs]
                o_ref[b, rows, vs] = (o * (go * _sigmoid(go))).astype(BF16)
        return carry

    lax.fori_loop(0, n_tiles, epilogue_body, 0)


def _gla(gq, gk, gv, gf, gb, go, init_states, gn, n_seqs):
    b, t, _ = gq.shape
    n_tiles = t // GLA_TILE
    zero_init = init_states is None
    seq = lambda c: pl.BlockSpec((n_seqs, t, c), lambda i: (i, 0, 0))
    st = pl.BlockSpec((n_seqs, GLA_HEADS, GLA_DK, GLA_DV), lambda i: (i, 0, 0, 0))
    in_specs = [seq(GQK), seq(GQK), seq(GV), seq(GQK), seq(GQK), seq(GV)]
    args = [gq, gk, gv, gf, gb, go]
    if not zero_init:
        in_specs += [st, st]
        args += list(init_states)
    in_specs.append(_const_spec(gn.shape))
    args.append(gn)
    return pl.pallas_call(
        functools.partial(_gla_kernel, n_tiles=n_tiles, n_seqs=n_seqs, zero_init=zero_init),
        out_shape=[jax.ShapeDtypeStruct((b, t, GV), BF16),
                   jax.ShapeDtypeStruct((b, GLA_HEADS, GLA_DK, GLA_DV), F32),
                   jax.ShapeDtypeStruct((b, GLA_HEADS, GLA_DK, GLA_DV), F32)],
        grid=(b // n_seqs,),
        in_specs=in_specs,
        out_specs=[seq(GV), st, st],
        scratch_shapes=[pltpu.VMEM((2, n_seqs, t, GV), F32),
                        pltpu.VMEM((GQK, GLA_TILE), BF16),
                        pltpu.VMEM((2, GLA_TILE, GLA_TILE), F32),
                        pltpu.VMEM((GLA_HEADS * GLA_TILE, GQK), BF16)],
        name="gla_%d" % t,
        compiler_params=pltpu.CompilerParams(dimension_semantics=("arbitrary",),
                                             vmem_limit_bytes=VMEM_LIMIT),
    )(*args)


def _ffn_kernel(xp_ref, xs_ref, atp_ref, ats_ref, glp_ref, gls_ref, mod_ref, wout_ref, nf_ref,
                wa_ref, wg_ref, wfo_ref, fn_ref, yp_ref, ys_ref,
                wout_s, wfi_s, wfo_s, act_ref, *, ctx_tiles, tiles_per_seq):
    s = pl.program_id(0)

    @pl.when(s < N_FF_CHUNKS)
    def _():
        wfi_s[s, :, 0:FF_CHUNK] = wa_ref[...].astype(BF16)
        wfi_s[s, :, FF_CHUNK:2 * FF_CHUNK] = wg_ref[...].astype(BF16)
        wfo_s[pl.ds(pl.multiple_of(s * FF_CHUNK, FF_CHUNK), FF_CHUNK), :] = wfo_ref[...].astype(BF16)

        @pl.when(s < N_WOUT_CHUNKS)
        def _():
            wout_s[pl.ds(pl.multiple_of(s * WOUT_CHUNK, WOUT_CHUNK), WOUT_CHUNK), :] = (
                wout_ref[...].astype(BF16))

    def tile(x_ref, at_ref, gl_ref, y_ref, mod_row):
        _, _, gt1, sh2, sc2, gt2 = _mod_rows(mod_ref, mod_row)
        mix = (jnp.dot(at_ref[...], wout_s[0:VALL, :], preferred_element_type=F32)
               + jnp.dot(gl_ref[...], wout_s[VALL:, :], preferred_element_type=F32))
        x1 = x_ref[...] + gt1 * mix
        h2 = (_rms(x1, nf_ref[...]) * (1.0 + sc2) + sh2).astype(BF16)
        for j in range(N_FF_CHUNKS):
            ag = jnp.dot(h2, wfi_s[j], preferred_element_type=F32)
            a = ag[:, :FF_CHUNK]
            act_ref[:, j * FF_CHUNK:(j + 1) * FF_CHUNK] = (a * _sigmoid(a) * ag[:, FF_CHUNK:]).astype(BF16)
        ff = jnp.dot(act_ref[...], wfo_s[...], preferred_element_type=F32)
        x2 = x1 + gt2 * ff
        y_ref[...] = _rms(x2, fn_ref[...])

    t = s - N_FF_CHUNKS

    @pl.when((t >= 0) & (t < ctx_tiles))
    def _():
        tile(xp_ref, atp_ref, glp_ref, yp_ref, 0)

    @pl.when(t >= ctx_tiles)
    def _():
        tile(xs_ref, ats_ref, gls_ref, ys_ref, 1 + (t - ctx_tiles) // tiles_per_seq)


def _ffn(xp, xs, attn_p, attn_s, gla_p, gla_s, mod, w_out, nf, w_ffn_in, w_ffn_out, fn, tm, tiles_per_seq):
    d = xp.shape[1]
    ctx_tiles = xp.shape[0] // tm
    lat_tiles = xs.shape[0] // tm
    nw = N_FF_CHUNKS
    ctx_map = lambda s: (jnp.clip(s - nw, 0, ctx_tiles - 1), 0)
    lat_map = lambda s: (jnp.clip(s - nw - ctx_tiles, 0, lat_tiles - 1), 0)
    tile = lambda c, m: pl.BlockSpec((tm, c), m)
    return pl.pallas_call(
        functools.partial(_ffn_kernel, ctx_tiles=ctx_tiles, tiles_per_seq=tiles_per_seq),
        out_shape=[jax.ShapeDtypeStruct(xp.shape, F32), jax.ShapeDtypeStruct(xs.shape, F32)],
        grid=(nw + ctx_tiles + lat_tiles,),
        in_specs=[tile(d, ctx_map), tile(d, lat_map), tile(VALL, ctx_map), tile(VALL, lat_map),
                  tile(GV, ctx_map), tile(GV, lat_map), _const_spec(mod.shape),
                  pl.BlockSpec((None, WOUT_CHUNK, d), lambda s: (0, jnp.minimum(s, N_WOUT_CHUNKS - 1), 0)),
                  _const_spec(nf.shape),
                  pl.BlockSpec((None, d, FF_CHUNK), lambda s: (0, 0, jnp.minimum(s, nw - 1))),
                  pl.BlockSpec((None, d, FF_CHUNK), lambda s: (0, 0, nw + jnp.minimum(s, nw - 1))),
                  pl.BlockSpec((None, FF_CHUNK, d), lambda s: (0, jnp.minimum(s, nw - 1), 0)),
                  _const_spec(fn.shape)],
        out_specs=[tile(d, ctx_map), tile(d, lat_map)],
        scratch_shapes=[pltpu.VMEM((D_MIX, d), BF16),
                        pltpu.VMEM((N_FF_CHUNKS, d, 2 * FF_CHUNK), BF16),
                        pltpu.VMEM((D_FF, d), BF16),
                        pltpu.VMEM((tm, D_FF), BF16)],
        name="out_ffn",
        compiler_params=pltpu.CompilerParams(dimension_semantics=("arbitrary",),
                                             vmem_limit_bytes=VMEM_LIMIT),
    )(xp, xs, attn_p, attn_s, gla_p, gla_s, mod, w_out, nf, w_ffn_in, w_ffn_in, w_ffn_out, fn)


def _rope_tables(n_tokens):
    t = np.arange(n_tokens)
    row = (t // GRID_W).astype(np.float32)
    col = (t % GRID_W).astype(np.float32)
    half = MLA_ROPE // 2
    inv = (np.float32(ROPE_BASE) ** (-np.arange(0, half, 2, dtype=np.float32) / np.float32(half))).astype(np.float32)
    ang_r = row[:, None] * inv
    ang_c = col[:, None] * inv
    ang = np.concatenate([ang_r, ang_r, ang_c, ang_c], axis=-1).astype(np.float32)
    cos, sin = np.cos(ang), np.sin(ang)
    first = (np.arange(MLA_ROPE) % half) < (half // 2)
    cos_t = np.ones((n_tokens, LANES), np.float32)
    sa_t = np.zeros((n_tokens, LANES), np.float32)
    sb_t = np.zeros((n_tokens, LANES), np.float32)
    cos_t[:, ROPE_LANE0:ROPE_LANE0 + MLA_ROPE] = cos
    sa_t[:, ROPE_LANE0:ROPE_LANE0 + MLA_ROPE] = np.where(first, -sin, 0.0)
    sb_t[:, ROPE_LANE0:ROPE_LANE0 + MLA_ROPE] = np.where(first, 0.0, sin)
    return jnp.asarray(cos_t), jnp.asarray(sa_t), jnp.asarray(sb_t)


def kernel(x_prompt, x_sample, cache_kv_latent, cache_k_rope, state_gla_fwd, state_gla_bwd, c, c_ctx, w_ada, b_ada, norm_attn, w_in, mla_q_norm, w_uq, mla_kv_norm, w_ukv, w_gate_f, b_gate_f, w_gate_b, b_gate_b, gla_norm, w_out, norm_ffn, w_ffn_in, w_ffn_out, final_norm):
    batch, seq, d = x_prompt.shape
    dec_batch, dec_seq, _ = x_sample.shape
    assert w_ada.shape[0] == 1 and w_in.shape[-1] == W_COLS and w_ffn_in.shape[-1] == 2 * D_FF
    l = 0

    cond8 = jnp.concatenate([c_ctx[None, :], c, jnp.zeros((8 - 1 - dec_batch, d), F32)], axis=0)
    mod = _ada(cond8, w_ada[l], b_ada[l])

    win, wuq, wk, wvt, wg = _prep_in_weights(w_in, w_uq, w_ukv, w_gate_f, w_gate_b)
    bg = jnp.concatenate([b_gate_f[l], b_gate_b[l]]).reshape(1, 2 * GQK)
    in_w = (norm_attn[l].reshape(1, d), win, mla_q_norm[l].reshape(1, Q_LORA), wuq,
            mla_kv_norm[l].reshape(1, KV_LORA), wk, wvt, wg, bg)
    gn = gla_norm[l].reshape(1, GLA_DV)
    tm, tm_ffn = 512, 512
    r3 = lambda a, b_, t: a.reshape(b_, t, a.shape[-1])

    xp = x_prompt.reshape(batch * seq, d)
    (q, k, vt, gq, gk, gv, gf, gb, go, ckv, kr) = _inproj(xp, mod, lambda i: 0, in_w, None, tm, seq // tm)
    attn_p = _attention(r3(q, batch, seq), r3(k, batch, seq), vt, None, min(seq, 256))
    gla_p, sf, sb = _gla(r3(gq, batch, seq), r3(gk, batch, seq), r3(gv, batch, seq), r3(gf, batch, seq),
                         r3(gb, batch, seq), r3(go, batch, seq), None, gn, GLA_CTX_SEQS)

    xs = x_sample.reshape(dec_batch * dec_seq, d)
    tiles = dec_seq // tm
    (q, k, vt, gq, gk, gv, gf, gb, go) = _inproj(xs, mod, lambda i: 1 + i // tiles, in_w,
                                                  _rope_tables(dec_seq), tm, tiles)
    kr_pad = jnp.pad(cache_k_rope[:, l], ((0, 0), (0, 0), (ROPE_LANE0, LANES - ROPE_LANE0 - MLA_ROPE)))
    kc, vct = _decomp(cache_kv_latent[:, l], kr_pad, wk, wvt)
    attn_s = _attention(r3(q, dec_batch, dec_seq), r3(k, dec_batch, dec_seq), vt, (kc, vct), 256)
    gla_s, _, _ = _gla(r3(gq, dec_batch, dec_seq), r3(gk, dec_batch, dec_seq), r3(gv, dec_batch, dec_seq),
                       r3(gf, dec_batch, dec_seq), r3(gb, dec_batch, dec_seq), r3(go, dec_batch, dec_seq),
                       (state_gla_fwd[:, l].astype(F32), state_gla_bwd[:, l].astype(F32)), gn, 1)

    flat = lambda a: a.reshape(-1, a.shape[-1])
    y_prompt, y_sample = _ffn(xp, xs, flat(attn_p), flat(attn_s), flat(gla_p), flat(gla_s), mod,
                              w_out, norm_ffn[l].reshape(1, d), w_ffn_in, w_ffn_out,
                              final_norm.reshape(1, d), tm_ffn, dec_seq // tm_ffn)
    y_prompt = y_prompt.reshape(batch, seq, d)
    y_sample = y_sample.reshape(dec_batch, dec_seq, d)

    new_kv_latent = ckv.reshape(batch, 1, seq, KV_LORA)
    new_k_rope = kr.reshape(batch, 1, seq, MLA_ROPE)
    new_state_fwd = sf.reshape(batch, 1, GLA_HEADS, GLA_DK, GLA_DV).astype(x_prompt.dtype)
    new_state_bwd = sb.reshape(batch, 1, GLA_HEADS, GLA_DK, GLA_DV).astype(x_prompt.dtype)
    return (y_prompt, y_sample, new_kv_latent, new_k_rope, new_state_fwd, new_state_bwd)
```

```python
import functools

import numpy as np
import jax
import jax.numpy as jnp
from jax import lax
from jax.experimental import pallas as pl
from jax.experimental.pallas import tpu as pltpu

F32 = jnp.float32
BF16 = jnp.bfloat16

GRID_W = 64
MLA_HEADS = 8
MLA_NOPE = 64
MLA_ROPE = 32
MLA_QK = MLA_NOPE + MLA_ROPE
MLA_V = 64
Q_LORA = 384
KV_LORA = 256
GLA_HEADS = 4
GLA_DK = 64
GLA_DV = 128
GATE_RANK = 16
GATE_NORM = 16.0
CHUNK = 64
D_FF = 2816
ROPE_BASE = 10000.0
EPS = 1e-6
LOG2_E = 1.4426950408889634

LANES = 128
HEAD_PAD = LANES
ROPE_LANE0 = MLA_NOPE
GQK = GLA_HEADS * GLA_DK
GV = GLA_HEADS * GLA_DV
QPAD = MLA_HEADS * HEAD_PAD
VALL = MLA_HEADS * MLA_V
ONES_ROWS = 16
KEY_BLOCK = 1024

W_KR = Q_LORA + KV_LORA
W_GQ = W_KR + MLA_ROPE
W_GF = W_GQ + 2 * GQK + GV
W_GO = W_GF + 2 * GATE_RANK
W_COLS = W_GO + GV

Z_Q = 0
Z_KV = Z_Q + Q_LORA
Z_GQ = Z_KV + KV_LORA
Z_GK = Z_GQ + GQK
Z_GV = Z_GK + GQK
Z_GO = Z_GV + GV
Z_MISC = Z_GO + GV
Z_COLS = Z_MISC + LANES

FF_CHUNK = 256
N_FF_CHUNKS = D_FF // FF_CHUNK
D_MIX = VALL + GV
WOUT_CHUNK = 128
N_WOUT_CHUNKS = D_MIX // WOUT_CHUNK

GLA_TILE = 256
CHUNKS_PER_TILE = GLA_TILE // CHUNK
ADA_ROWS = 128
GLA_CTX_SEQS = 4

VMEM_LIMIT = 56 * 1024 * 1024

_NT = (((1,), (1,)), ((), ()))


def _rms(x, w):
    return x * lax.rsqrt(jnp.mean(x * x, axis=-1, keepdims=True) + EPS) * w


def _sigmoid(x):
    return 1.0 / (1.0 + jnp.exp(-x))


def _log_sigmoid(x):
    return jnp.minimum(x, 0.0) - jnp.log1p(jnp.exp(-jnp.abs(x)))


def _const_spec(shape):
    nd = len(shape)
    return pl.BlockSpec(shape, lambda *_: (0,) * nd, pipeline_mode=pl.Buffered(1))


def _mod_rows(mod_ref, r):
    return [mod_ref[k, pl.ds(r, 1), :] for k in range(6)]


def _ada_kernel(cond_ref, w_ref, b_ref, o_ref):
    k = pl.program_id(0)
    d = o_ref.shape[2]
    c = cond_ref[...]
    s = (c * _sigmoid(c)).astype(BF16)
    part = jnp.dot(s, w_ref[...].astype(BF16), preferred_element_type=F32)
    for j in range(o_ref.shape[0]):
        sl = slice(j * d, (j + 1) * d)

        @pl.when(k == 0)
        def _():
            o_ref[j] = part[:, sl] + b_ref[:, sl]

        @pl.when(k > 0)
        def _():
            o_ref[j] += part[:, sl]


def _ada(cond8, w_ada, b_ada):
    d = w_ada.shape[0]
    n = w_ada.shape[1]
    steps = d // ADA_ROWS
    cond_k = cond8.reshape(8, steps, ADA_ROWS).transpose(1, 0, 2)
    return pl.pallas_call(
        _ada_kernel,
        out_shape=jax.ShapeDtypeStruct((n // d, 8, d), F32),
        grid=(steps,),
        in_specs=[pl.BlockSpec((None, 8, ADA_ROWS), lambda k: (k, 0, 0)),
                  pl.BlockSpec((ADA_ROWS, n), lambda k: (k, 0)),
                  pl.BlockSpec((1, n), lambda k: (0, 0))],
        out_specs=pl.BlockSpec((n // d, 8, d), lambda k: (0, 0, 0)),
        name="ada_mod",
        compiler_params=pltpu.CompilerParams(dimension_semantics=("arbitrary",)),
    )(cond_k, w_ada, b_ada.reshape(1, n))


def _prep_kernel(wint_ref, wuq_ref, wukv_ref, wgf_ref, wgb_ref, win_o, wuq_o, wk_o, wvt_o, wg_o):
    cols = wint_ref.shape[1]
    for dst, src, n in ((Z_Q, 0, W_KR), (Z_GQ, W_GQ, W_GF - W_GQ), (Z_GO, W_GO, GV)):
        win_o[:, dst:dst + n] = wint_ref[src:src + n, :].T.astype(BF16)
    z32 = jnp.zeros((32, cols), F32)
    misc_t = jnp.concatenate([wint_ref[W_GF:W_GO, :], z32, wint_ref[W_KR:W_GQ, :], z32], axis=0)
    win_o[:, Z_MISC:Z_COLS] = misc_t.T.astype(BF16)

    u = wuq_ref[...]
    zq = jnp.zeros((u.shape[0], HEAD_PAD - MLA_QK), F32)
    for hd in range(MLA_HEADS):
        blk = jnp.concatenate([u[:, hd * MLA_QK:(hd + 1) * MLA_QK], zq], axis=1)
        wuq_o[:, hd * HEAD_PAD:(hd + 1) * HEAD_PAD] = blk.astype(BF16)

    @pl.when(pl.program_id(0) == 0)
    def _():
        kv = wukv_ref[...]
        per = MLA_NOPE + MLA_V
        lane = lax.broadcasted_iota(jnp.int32, (kv.shape[0], per), 1)
        for hd in range(MLA_HEADS):
            blk = kv[:, hd * per:(hd + 1) * per]
            wk_o[:, hd * HEAD_PAD:(hd + 1) * HEAD_PAD] = jnp.where(lane < MLA_NOPE, blk, 0.0).astype(BF16)
        wv = jnp.concatenate([kv[:, hd * per + MLA_NOPE:(hd + 1) * per] for hd in range(MLA_HEADS)], axis=1)
        wvt_o[...] = wv.T.astype(BF16)

        wg_o[...] = jnp.zeros(wg_o.shape, BF16)
        wg_o[0:GATE_RANK, 0:GQK] = wgf_ref[...].astype(BF16)
        wg_o[GATE_RANK:2 * GATE_RANK, GQK:2 * GQK] = wgb_ref[...].astype(BF16)


def _prep_in_weights(w_in, w_uq, w_ukv, w_gate_f, w_gate_b):
    d = w_in.shape[1]
    steps = 4
    w_in_t = jnp.swapaxes(w_in, 1, 2)
    rb3 = lambda r, c: pl.BlockSpec((None, r // steps, c), lambda i: (0, i, 0))
    rb = lambda r, c: pl.BlockSpec((r // steps, c), lambda i: (i, 0))
    full3 = lambda shape: pl.BlockSpec((None,) + tuple(shape[1:]), lambda i: (0, 0, 0))
    full = lambda shape: pl.BlockSpec(shape, lambda i: (0, 0))
    return pl.pallas_call(
        _prep_kernel,
        out_shape=[jax.ShapeDtypeStruct((d, Z_COLS), BF16),
                   jax.ShapeDtypeStruct((Q_LORA, QPAD), BF16),
                   jax.ShapeDtypeStruct((KV_LORA, QPAD), BF16),
                   jax.ShapeDtypeStruct((VALL, KV_LORA), BF16),
                   jax.ShapeDtypeStruct((LANES, 2 * GQK), BF16)],
        grid=(steps,),
        in_specs=[pl.BlockSpec((None, W_COLS, d // steps), lambda i: (0, 0, i)),
                  rb3(Q_LORA, MLA_HEADS * MLA_QK), full3(w_ukv.shape),
                  full3(w_gate_f.shape), full3(w_gate_b.shape)],
        out_specs=[rb(d, Z_COLS), rb(Q_LORA, QPAD), full((KV_LORA, QPAD)), full((VALL, KV_LORA)),
                   full((LANES, 2 * GQK))],
        name="weight_prep",
        compiler_params=pltpu.CompilerParams(dimension_semantics=("arbitrary",)),
    )(w_in_t, w_uq, w_ukv, w_gate_f, w_gate_b)


def _inproj_kernel(*refs, latent, mod_row):
    (x_ref, mod_ref, nw_ref, win_ref, qn_ref, wuq_ref, kvn_ref, wk_ref, wvt_ref, wg_ref, bg_ref) = refs[:11]
    if latent:
        cos_ref, sa_ref, sb_ref = refs[11:14]
        outs = refs[14:]
    else:
        outs = refs[11:]
    q_ref, k_ref, vt_ref, gq_ref, gk_ref, gv_ref, gf_ref, gb_ref, go_ref = outs[:9]

    sh1, sc1 = _mod_rows(mod_ref, mod_row(pl.program_id(0)))[:2]
    x = x_ref[...]
    h = _rms(x, nw_ref[...]) * (1.0 + sc1) + sh1
    z = jnp.dot(h.astype(BF16), win_ref[...], preferred_element_type=F32)

    qn = _rms(z[:, Z_Q:Z_Q + Q_LORA], qn_ref[...])
    q = jnp.dot(qn.astype(BF16), wuq_ref[...], preferred_element_type=F32)
    ckv = _rms(z[:, Z_KV:Z_KV + KV_LORA], kvn_ref[...])
    ckv_b = ckv.astype(BF16)
    kn = jnp.dot(ckv_b, wk_ref[...], preferred_element_type=F32)
    vt_ref[...] = lax.dot_general(wvt_ref[...], ckv_b, _NT,
                                  preferred_element_type=F32).astype(BF16)
    misc = z[:, Z_MISC:Z_MISC + LANES]

    if latent:
        cos, sa, sb = cos_ref[...], sa_ref[...], sb_ref[...]

        def rope(t):
            return t * cos + pltpu.roll(t, LANES - 8, 1) * sa + pltpu.roll(t, 8, 1) * sb
    else:
        def rope(t):
            return t

    scale = MLA_QK ** -0.5 * LOG2_E
    lane = lax.broadcasted_iota(jnp.int32, misc.shape, 1)
    in_rope = (lane >= ROPE_LANE0) & (lane < ROPE_LANE0 + MLA_ROPE)
    krope = rope(misc)
    for hd in range(MLA_HEADS):
        sl = slice(hd * HEAD_PAD, (hd + 1) * HEAD_PAD)
        q_ref[:, sl] = (rope(q[:, sl]) * scale).astype(BF16)
        k_ref[:, sl] = jnp.where(in_rope, krope, kn[:, sl]).astype(BF16)

    gq_ref[...] = z[:, Z_GQ:Z_GQ + GQK]
    gk_ref[...] = z[:, Z_GK:Z_GK + GQK]
    gv_ref[...] = z[:, Z_GV:Z_GV + GV].astype(BF16)
    go_ref[...] = z[:, Z_GO:Z_GO + GV]
    gpre = jnp.dot(misc.astype(BF16), wg_ref[...], preferred_element_type=F32) + bg_ref[...]
    gate = _log_sigmoid(gpre) * (1.0 / GATE_NORM)
    gf_ref[...] = gate[:, :GQK]
    gb_ref[...] = gate[:, GQK:]

    if not latent:
        ckv_ref, kr_ref = outs[9:]
        ckv_ref[...] = ckv
        kr_ref[...] = misc[:, ROPE_LANE0:ROPE_LANE0 + MLA_ROPE]


def _inproj(x2d, mod, mod_row, weights, rope_tabs, tm, tiles_per_seq):
    n_tok, d = x2d.shape
    latent = rope_tabs is not None
    nw, win, qn, wuq, kvn, wk, wvt, wg, bg = weights
    row = lambda i: (i, 0)
    in_specs = [pl.BlockSpec((tm, d), row), _const_spec(mod.shape),
                _const_spec(nw.shape), _const_spec(win.shape), _const_spec(qn.shape),
                _const_spec(wuq.shape), _const_spec(kvn.shape), _const_spec(wk.shape),
                _const_spec(wvt.shape), _const_spec(wg.shape), _const_spec(bg.shape)]
    args = [x2d, mod, nw, win, qn, wuq, kvn, wk, wvt, wg, bg]
    if latent:
        tab = pl.BlockSpec((tm, LANES), lambda i: (i % tiles_per_seq, 0))
        in_specs += [tab, tab, tab]
        args += list(rope_tabs)
    out_cols = [(QPAD, BF16), (QPAD, BF16), None, (GQK, F32), (GQK, F32), (GV, BF16),
                (GQK, F32), (GQK, F32), (GV, F32)]
    if not latent:
        out_cols += [(KV_LORA, F32), (MLA_ROPE, F32)]
    out_shape = [jax.ShapeDtypeStruct((n_tok, oc[0]), oc[1]) if oc else
                 jax.ShapeDtypeStruct((VALL, n_tok), BF16) for oc in out_cols]
    out_specs = [pl.BlockSpec((tm, oc[0]), row) if oc else
                 pl.BlockSpec((VALL, tm), lambda i: (0, i)) for oc in out_cols]
    return pl.pallas_call(
        functools.partial(_inproj_kernel, latent=latent, mod_row=mod_row),
        out_shape=out_shape,
        grid=(n_tok // tm,),
        in_specs=in_specs,
        out_specs=out_specs,
        name="inproj_lat" if latent else "inproj_ctx",
        compiler_params=pltpu.CompilerParams(dimension_semantics=("arbitrary",),
                                             vmem_limit_bytes=VMEM_LIMIT),
    )(*args)


def _decomp_kernel(ckv_ref, kr_ref, wk_ref, wvt_ref, k_ref, vt_ref):
    ckv_b = ckv_ref[...].astype(BF16)
    kn = jnp.dot(ckv_b, wk_ref[...], preferred_element_type=F32)
    kr = kr_ref[...]
    lane = lax.broadcasted_iota(jnp.int32, kr.shape, 1)
    in_rope = (lane >= ROPE_LANE0) & (lane < ROPE_LANE0 + MLA_ROPE)
    for hd in range(MLA_HEADS):
        sl = slice(hd * HEAD_PAD, (hd + 1) * HEAD_PAD)
        k_ref[:, sl] = jnp.where(in_rope, kr, kn[:, sl]).astype(BF16)
    vt_ref[...] = lax.dot_general(wvt_ref[...], ckv_b, _NT, preferred_element_type=F32).astype(BF16)


def _decomp(ckv, kr_pad, wk, wvt):
    b, s, _ = ckv.shape
    return pl.pallas_call(
        _decomp_kernel,
        out_shape=[jax.ShapeDtypeStruct((b, s, QPAD), BF16), jax.ShapeDtypeStruct((VALL, b * s), BF16)],
        grid=(b,),
        in_specs=[pl.BlockSpec((None, s, KV_LORA), lambda i: (i, 0, 0)),
                  pl.BlockSpec((None, s, LANES), lambda i: (i, 0, 0)),
                  _const_spec(wk.shape), _const_spec(wvt.shape)],
        out_specs=[pl.BlockSpec((None, s, QPAD), lambda i: (i, 0, 0)),
                   pl.BlockSpec((VALL, s), lambda i: (0, i))],
        name="ctx_decompress",
        compiler_params=pltpu.CompilerParams(dimension_semantics=("arbitrary",)),
    )(ckv, kr_pad, wk, wvt)


def _attn_kernel(*refs, has_ctx):
    if has_ctx:
        q_ref, kc_ref, vct_ref, k_ref, vt_ref, o_ref, st_ref, p_ref = refs
    else:
        q_ref, k_ref, vt_ref, o_ref, st_ref, p_ref = refs
    tq = q_ref.shape[0]
    blocks, row0 = [], 0
    for kr, vr in ([(kc_ref, vct_ref)] if has_ctx else []) + [(k_ref, vt_ref)]:
        n_keys = kr.shape[0]
        size = min(KEY_BLOCK, n_keys)
        for r in range(0, n_keys, size):
            blocks.append((kr, vr, r, size, row0))
            row0 += size

    col_max = [None] * MLA_HEADS
    pair = []
    for stage in range(MLA_HEADS + 2):
        ha, hb, hc = stage, stage - 1, stage - 2
        run_max = None
        acc = jnp.zeros((MLA_V + ONES_ROWS, tq), F32)
        for kr, vr, r0, size, srow in blocks:
            rows = slice(srow, srow + size)
            if ha < MLA_HEADS:
                sl = slice(ha * HEAD_PAD, (ha + 1) * HEAD_PAD)
                st = lax.dot_general(kr[r0:r0 + size, sl], q_ref[:, sl], _NT,
                                     preferred_element_type=F32)
                st_ref[ha % 2, rows, :] = st
                blk_max = jnp.max(st.reshape(size // 8, 8, tq), axis=0)
                run_max = blk_max if run_max is None else jnp.maximum(run_max, blk_max)
            if 0 <= hb < MLA_HEADS:
                p_ref[hb % 2, rows, :] = jnp.exp2(st_ref[hb % 2, rows, :] - col_max[hb]).astype(BF16)
            if hc >= 0:
                v_aug = jnp.concatenate([vr[hc * MLA_V:(hc + 1) * MLA_V, r0:r0 + size],
                                         jnp.ones((ONES_ROWS, size), BF16)], axis=0)
                acc = acc + jnp.dot(v_aug, p_ref[hc % 2, rows, :],
                                    preferred_element_type=F32)
        if ha < MLA_HEADS:
            col_max[ha] = jnp.max(run_max, axis=0, keepdims=True)
        if hc >= 0:
            pair.append(acc[:MLA_V, :] / acc[MLA_V:MLA_V + 1, :])
            if len(pair) == 2:
                o_ref[:, (hc - 1) * MLA_V:(hc + 1) * MLA_V] = jnp.concatenate(pair, axis=0).T.astype(BF16)
                pair = []


def _attention(q, k, vt, ctx_kv, tq):
    b, t, _ = q.shape
    has_ctx = ctx_kv is not None
    in_specs = [pl.BlockSpec((None, tq, QPAD), lambda i, j: (i, j, 0))]
    args = [q]
    if has_ctx:
        kc, vct = ctx_kv
        s = kc.shape[1]
        in_specs += [pl.BlockSpec((None, s, QPAD), lambda i, j: (i, 0, 0)),
                     pl.BlockSpec((VALL, s), lambda i, j: (0, i))]
        args += [kc, vct]
    in_specs += [pl.BlockSpec((None, t, QPAD), lambda i, j: (i, 0, 0)),
                 pl.BlockSpec((VALL, t), lambda i, j: (0, i))]
    args += [k, vt]
    return pl.pallas_call(
        functools.partial(_attn_kernel, has_ctx=has_ctx),
        out_shape=jax.ShapeDtypeStruct((b, t, VALL), BF16),
        grid=(b, t // tq),
        in_specs=in_specs,
        out_specs=pl.BlockSpec((None, tq, VALL), lambda i, j: (i, j, 0)),
        scratch_shapes=[pltpu.VMEM((2, t + (s if has_ctx else 0), tq), F32),
                        pltpu.VMEM((2, t + (s if has_ctx else 0), tq), BF16)],
        name="mla_attn_lat" if has_ctx else "mla_attn_ctx",
        compiler_params=pltpu.CompilerParams(dimension_semantics=("arbitrary", "arbitrary"),
                                             vmem_limit_bytes=VMEM_LIMIT),
    )(*args)


def _gla_kernel(*refs, n_tiles, n_seqs, zero_init):
    gq_ref, gk_ref, gv_ref, gf_ref, gb_ref, go_ref = refs[:6]
    if zero_init:
        gn_ref, o_ref, sf_ref, sb_ref, oacc_ref, bdqk_ref, tri_ref, hm_ref = refs[6:]
    else:
        (sf0_ref, sb0_ref, gn_ref, o_ref, sf_ref, sb_ref, oacc_ref,
         bdqk_ref, tri_ref, hm_ref) = refs[6:]
    g_refs = (gf_ref, gb_ref)
    state_refs = (sf_ref, sb_ref)

    @pl.when(pl.program_id(0) == 0)
    def _():
        ri = lax.broadcasted_iota(jnp.int32, (GLA_TILE, GLA_TILE), 0)
        ci = lax.broadcasted_iota(jnp.int32, (GLA_TILE, GLA_TILE), 1)
        same_chunk = (ri // CHUNK) == (ci // CHUNK)
        bdqk_ref[...] = jnp.where(same_chunk, 1.0, 0.0).astype(BF16)
        tri_ref[0] = jnp.where(same_chunk & (ri >= ci), 1.0, 0.0)
        tri_ref[1] = jnp.where(same_chunk & (ci >= ri), 1.0, 0.0)
        hm_ref[...] = jnp.where(
            lax.broadcasted_iota(jnp.int32, (GLA_HEADS * GLA_TILE, GQK), 0) // GLA_TILE
            == lax.broadcasted_iota(jnp.int32, (GLA_HEADS * GLA_TILE, GQK), 1) // GLA_DK,
            1.0, 0.0).astype(BF16)

    row8 = lax.broadcasted_iota(jnp.int32, (8, GQK), 0)

    def tile_rows(t):
        return pl.ds(pl.multiple_of(t * GLA_TILE, GLA_TILE), GLA_TILE)

    def total_row(c, d):
        return c * CHUNK + (CHUNK - 1 if d == 0 else 0)

    def tile_dir(b, t, d):
        rows = tile_rows(t)
        g = g_refs[d][b, rows, :]
        g_hi = g.astype(BF16)
        g_lo = (g - g_hi.astype(F32)).astype(BF16)
        tri_b = tri_ref[d].astype(BF16)
        cum = (jnp.dot(tri_b, g_hi, preferred_element_type=F32)
               + jnp.dot(tri_b, g_lo, preferred_element_type=F32))
        totals = [cum[total_row(c, d):total_row(c, d) + 1, :] for c in range(CHUNKS_PER_TILE)]
        tot8 = jnp.zeros((8, GQK), F32)
        for c in range(CHUNKS_PER_TILE):
            tot8 = jnp.where(row8 == c, totals[c], tot8)
        dec_t = jnp.concatenate([jnp.exp(tot8), jnp.zeros((LANES - 8, GQK), F32)], axis=0).T
        q = gq_ref[b, rows, :] * (GLA_DK ** -0.5)
        k = gk_ref[b, rows, :]
        v = gv_ref[b, rows, :]
        tot = jnp.concatenate([jnp.broadcast_to(tc, (CHUNK, GQK)) for tc in totals], axis=0)
        qe = (q * jnp.exp(cum)).astype(BF16)
        ke = (k * jnp.exp(-cum)).astype(BF16)
        kd_t = (k * jnp.exp(tot - cum)).T.astype(BF16)
        bd_qk = bdqk_ref[...] > 0
        tri = tri_ref[d] > 0

        qm = jnp.where(hm_ref[...] > 0, jnp.tile(qe, (GLA_HEADS, 1)), 0.0)
        att = lax.dot_general(qm, ke, _NT, preferred_element_type=F32)

        intra, upd = [], []
        for hd in range(GLA_HEADS):
            vh = v[:, hd * GLA_DV:(hd + 1) * GLA_DV]
            a_h = jnp.where(tri, att[hd * GLA_TILE:(hd + 1) * GLA_TILE, :], 0.0).astype(BF16)
            intra.append(jnp.dot(a_h, vh, preferred_element_type=F32))
            kd_h = jnp.tile(kd_t[hd * GLA_DK:(hd + 1) * GLA_DK, :], (CHUNKS_PER_TILE, 1))
            upd.append(jnp.dot(jnp.where(bd_qk, kd_h, 0.0), vh, preferred_element_type=F32))

        state = [state_refs[d][b, hd] for hd in range(GLA_HEADS)]
        order = range(CHUNKS_PER_TILE) if d == 0 else range(CHUNKS_PER_TILE - 1, -1, -1)
        seen = {}
        for c in order:
            seen[c] = jnp.concatenate(state, axis=0).astype(BF16)
            decay = jnp.broadcast_to(dec_t[:, c:c + 1], (GQK, GLA_DV))
            for hd in range(GLA_HEADS):
                ks = slice(hd * GLA_DK, (hd + 1) * GLA_DK)
                state[hd] = decay[ks, :] * state[hd] + upd[hd][c * CHUNK:(c + 1) * CHUNK, :]
        for hd in range(GLA_HEADS):
            state_refs[d][b, hd] = state[hd]

        for c in range(CHUNKS_PER_TILE):
            cr = slice(c * CHUNK, (c + 1) * CHUNK)
            q_c = jnp.concatenate([qm[hd * GLA_TILE + c * CHUNK:hd * GLA_TILE + (c + 1) * CHUNK, :]
                                   for hd in range(GLA_HEADS)], axis=0)
            inter = jnp.dot(q_c, seen[c], preferred_element_type=F32)
            o = jnp.concatenate([intra[hd][cr, :] + inter[hd * CHUNK:(hd + 1) * CHUNK, :]
                                 for hd in range(GLA_HEADS)], axis=1)
            oacc_ref[d, b, pl.ds(pl.multiple_of(t * GLA_TILE + c * CHUNK, CHUNK), CHUNK), :] = o

    if zero_init:
        sf_ref[...] = jnp.zeros(sf_ref.shape, F32)
        sb_ref[...] = jnp.zeros(sb_ref.shape, F32)
    else:
        sf_ref[...] = sf0_ref[...]
        sb_ref[...] = sb0_ref[...]

    def main_body(t, carry):
        for b in range(n_seqs):
            tile_dir(b, t, 0)
            tile_dir(b, n_tiles - 1 - t, 1)
        return carry

    lax.fori_loop(0, n_tiles, main_body, 0)

    gn = gn_ref[...]

    def epilogue_body(t, carry):
        rows = tile_rows(t)
        for b in range(n_seqs):
            for hd in range(GLA_HEADS):
                vs = slice(hd * GLA_DV, (hd + 1) * GLA_DV)
                o = _rms(oacc_ref[0, b, rows, vs] + oacc_ref[1, b, rows, vs], gn)
                go = go_ref[b, rows, vs]
                o_ref[b, rows, vs] = (o * (go * _sigmoid(go))).astype(BF16)
        return carry

    lax.fori_loop(0, n_tiles, epilogue_body, 0)


def _gla(gq, gk, gv, gf, gb, go, init_states, gn, n_seqs):
    b, t, _ = gq.shape
    n_tiles = t // GLA_TILE
    zero_init = init_states is None
    seq = lambda c: pl.BlockSpec((n_seqs, t, c), lambda i: (i, 0, 0))
    st = pl.BlockSpec((n_seqs, GLA_HEADS, GLA_DK, GLA_DV), lambda i: (i, 0, 0, 0))
    in_specs = [seq(GQK), seq(GQK), seq(GV), seq(GQK), seq(GQK), seq(GV)]
    args = [gq, gk, gv, gf, gb, go]
    if not zero_init:
        in_specs += [st, st]
        args += list(init_states)
    in_specs.append(_const_spec(gn.shape))
    args.append(gn)
    return pl.pallas_call(
        functools.partial(_gla_kernel, n_tiles=n_tiles, n_seqs=n_seqs, zero_init=zero_init),
        out_shape=[jax.ShapeDtypeStruct((b, t, GV), BF16),
                   jax.ShapeDtypeStruct((b, GLA_HEADS, GLA_DK, GLA_DV), F32),
                   jax.ShapeDtypeStruct((b, GLA_HEADS, GLA_DK, GLA_DV), F32)],
        grid=(b // n_seqs,),
        in_specs=in_specs,
        out_specs=[seq(GV), st, st],
        scratch_shapes=[pltpu.VMEM((2, n_seqs, t, GV), F32),
                        pltpu.VMEM((GQK, GLA_TILE), BF16),
                        pltpu.VMEM((2, GLA_TILE, GLA_TILE), F32),
                        pltpu.VMEM((GLA_HEADS * GLA_TILE, GQK), BF16)],
        name="gla_%d" % t,
        compiler_params=pltpu.CompilerParams(dimension_semantics=("arbitrary",),
                                             vmem_limit_bytes=VMEM_LIMIT),
    )(*args)


def _ffn_kernel(xp_ref, xs_ref, atp_ref, ats_ref, glp_ref, gls_ref, mod_ref, wout_ref, nf_ref,
                wa_ref, wg_ref, wfo_ref, fn_ref, yp_ref, ys_ref,
                wout_s, wfi_s, wfo_s, act_ref, *, ctx_tiles, tiles_per_seq):
    s = pl.program_id(0)

    @pl.when(s < N_FF_CHUNKS)
    def _():
        wfi_s[s, :, 0:FF_CHUNK] = wa_ref[...].astype(BF16)
        wfi_s[s, :, FF_CHUNK:2 * FF_CHUNK] = wg_ref[...].astype(BF16)
        wfo_s[pl.ds(pl.multiple_of(s * FF_CHUNK, FF_CHUNK), FF_CHUNK), :] = wfo_ref[...].astype(BF16)

        @pl.when(s < N_WOUT_CHUNKS)
        def _():
            wout_s[pl.ds(pl.multiple_of(s * WOUT_CHUNK, WOUT_CHUNK), WOUT_CHUNK), :] = (
                wout_ref[...].astype(BF16))

    def tile(x_ref, at_ref, gl_ref, y_ref, mod_row):
        _, _, gt1, sh2, sc2, gt2 = _mod_rows(mod_ref, mod_row)
        mix = (jnp.dot(at_ref[...], wout_s[0:VALL, :], preferred_element_type=F32)
               + jnp.dot(gl_ref[...], wout_s[VALL:, :], preferred_element_type=F32))
        x1 = x_ref[...] + gt1 * mix
        h2 = (_rms(x1, nf_ref[...]) * (1.0 + sc2) + sh2).astype(BF16)
        for j in range(N_FF_CHUNKS):
            ag = jnp.dot(h2, wfi_s[j], preferred_element_type=F32)
            a = ag[:, :FF_CHUNK]
            act_ref[:, j * FF_CHUNK:(j + 1) * FF_CHUNK] = (a * _sigmoid(a) * ag[:, FF_CHUNK:]).astype(BF16)
        ff = jnp.dot(act_ref[...], wfo_s[...], preferred_element_type=F32)
        x2 = x1 + gt2 * ff
        y_ref[...] = _rms(x2, fn_ref[...])

    t = s - N_FF_CHUNKS

    @pl.when((t >= 0) & (t < ctx_tiles))
    def _():
        tile(xp_ref, atp_ref, glp_ref, yp_ref, 0)

    @pl.when(t >= ctx_tiles)
    def _():
        tile(xs_ref, ats_ref, gls_ref, ys_ref, 1 + (t - ctx_tiles) // tiles_per_seq)


def _ffn(xp, xs, attn_p, attn_s, gla_p, gla_s, mod, w_out, nf, w_ffn_in, w_ffn_out, fn, tm, tiles_per_seq):
    d = xp.shape[1]
    ctx_tiles = xp.shape[0] // tm
    lat_tiles = xs.shape[0] // tm
    nw = N_FF_CHUNKS
    ctx_map = lambda s: (jnp.clip(s - nw, 0, ctx_tiles - 1), 0)
    lat_map = lambda s: (jnp.clip(s - nw - ctx_tiles, 0, lat_tiles - 1), 0)
    tile = lambda c, m: pl.BlockSpec((tm, c), m)
    return pl.pallas_call(
        functools.partial(_ffn_kernel, ctx_tiles=ctx_tiles, tiles_per_seq=tiles_per_seq),
        out_shape=[jax.ShapeDtypeStruct(xp.shape, F32), jax.ShapeDtypeStruct(xs.shape, F32)],
        grid=(nw + ctx_tiles + lat_tiles,),
        in_specs=[tile(d, ctx_map), tile(d, lat_map), tile(VALL, ctx_map), tile(VALL, lat_map),
                  tile(GV, ctx_map), tile(GV, lat_map), _const_spec(mod.shape),
                  pl.BlockSpec((None, WOUT_CHUNK, d), lambda s: (0, jnp.minimum(s, N_WOUT_CHUNKS - 1), 0)),
                  _const_spec(nf.shape),
                  pl.BlockSpec((None, d, FF_CHUNK), lambda s: (0, 0, jnp.minimum(s, nw - 1))),
                  pl.BlockSpec((None, d, FF_CHUNK), lambda s: (0, 0, nw + jnp.minimum(s, nw - 1))),
                  pl.BlockSpec((None, FF_CHUNK, d), lambda s: (0, jnp.minimum(s, nw - 1), 0)),
                  _const_spec(fn.shape)],
        out_specs=[tile(d, ctx_map), tile(d, lat_map)],
        scratch_shapes=[pltpu.VMEM((D_MIX, d), BF16),
                        pltpu.VMEM((N_FF_CHUNKS, d, 2 * FF_CHUNK), BF16),
                        pltpu.VMEM((D_FF, d), BF16),
                        pltpu.VMEM((tm, D_FF), BF16)],
        name="out_ffn",
        compiler_params=pltpu.CompilerParams(dimension_semantics=("arbitrary",),
                                             vmem_limit_bytes=VMEM_LIMIT),
    )(xp, xs, attn_p, attn_s, gla_p, gla_s, mod, w_out, nf, w_ffn_in, w_ffn_in, w_ffn_out, fn)


def _rope_tables(n_tokens):
    t = np.arange(n_tokens)
    row = (t // GRID_W).astype(np.float32)
    col = (t % GRID_W).astype(np.float32)
    half = MLA_ROPE // 2
    inv = (np.float32(ROPE_BASE) ** (-np.arange(0, half, 2, dtype=np.float32) / np.float32(half))).astype(np.float32)
    ang_r = row[:, None] * inv
    ang_c = col[:, None] * inv
    ang = np.concatenate([ang_r, ang_r, ang_c, ang_c], axis=-1).astype(np.float32)
    cos, sin = np.cos(ang), np.sin(ang)
    first = (np.arange(MLA_ROPE) % half) < (half // 2)
    cos_t = np.ones((n_tokens, LANES), np.float32)
    sa_t = np.zeros((n_tokens, LANES), np.float32)
    sb_t = np.zeros((n_tokens, LANES), np.float32)
    cos_t[:, ROPE_LANE0:ROPE_LANE0 + MLA_ROPE] = cos
    sa_t[:, ROPE_LANE0:ROPE_LANE0 + MLA_ROPE] = np.where(first, -sin, 0.0)
    sb_t[:, ROPE_LANE0:ROPE_LANE0 + MLA_ROPE] = np.where(first, 0.0, sin)
    return jnp.asarray(cos_t), jnp.asarray(sa_t), jnp.asarray(sb_t)


def kernel(x_prompt, x_sample, cache_kv_latent, cache_k_rope, state_gla_fwd, state_gla_bwd, c, c_ctx, w_ada, b_ada, norm_attn, w_in, mla_q_norm, w_uq, mla_kv_norm, w_ukv, w_gate_f, b_gate_f, w_gate_b, b_gate_b, gla_norm, w_out, norm_ffn, w_ffn_in, w_ffn_out, final_norm):
    batch, seq, d = x_prompt.shape
    dec_batch, dec_seq, _ = x_sample.shape
    assert w_ada.shape[0] == 1 and w_in.shape[-1] == W_COLS and w_ffn_in.shape[-1] == 2 * D_FF
    l = 0

    cond8 = jnp.concatenate([c_ctx[None, :], c, jnp.zeros((8 - 1 - dec_batch, d), F32)], axis=0)
    mod = _ada(cond8, w_ada[l], b_ada[l])

    win, wuq, wk, wvt, wg = _prep_in_weights(w_in, w_uq, w_ukv, w_gate_f, w_gate_b)
    bg = jnp.concatenate([b_gate_f[l], b_gate_b[l]]).reshape(1, 2 * GQK)
    in_w = (norm_attn[l].reshape(1, d), win, mla_q_norm[l].reshape(1, Q_LORA), wuq,
            mla_kv_norm[l].reshape(1, KV_LORA), wk, wvt, wg, bg)
    gn = gla_norm[l].reshape(1, GLA_DV)
    tm, tm_ffn = 512, 512
    r3 = lambda a, b_, t: a.reshape(b_, t, a.shape[-1])

    xp = x_prompt.reshape(batch * seq, d)
    (q, k, vt, gq, gk, gv, gf, gb, go, ckv, kr) = _inproj(xp, mod, lambda i: 0, in_w, None, tm, seq // tm)
    attn_p = _attention(r3(q, batch, seq), r3(k, batch, seq), vt, None, min(seq, 256))
    gla_p, sf, sb = _gla(r3(gq, batch, seq), r3(gk, batch, seq), r3(gv, batch, seq), r3(gf, batch, seq),
                         r3(gb, batch, seq), r3(go, batch, seq), None, gn, GLA_CTX_SEQS)

    xs = x_sample.reshape(dec_batch * dec_seq, d)
    tiles = dec_seq // tm
    (q, k, vt, gq, gk, gv, gf, gb, go) = _inproj(xs, mod, lambda i: 1 + i // tiles, in_w,
                                                  _rope_tables(dec_seq), tm, tiles)
    kr_pad = jnp.pad(cache_k_rope[:, l], ((0, 0), (0, 0), (ROPE_LANE0, LANES - ROPE_LANE0 - MLA_ROPE)))
    kc, vct = _decomp(cache_kv_latent[:, l], kr_pad, wk, wvt)
    attn_s = _attention(r3(q, dec_batch, dec_seq), r3(k, dec_batch, dec_seq), vt, (kc, vct), 256)
    gla_s, _, _ = _gla(r3(gq, dec_batch, dec_seq), r3(gk, dec_batch, dec_seq), r3(gv, dec_batch, dec_seq),
                       r3(gf, dec_batch, dec_seq), r3(gb, dec_batch, dec_seq), r3(go, dec_batch, dec_seq),
                       (state_gla_fwd[:, l].astype(F32), state_gla_bwd[:, l].astype(F32)), gn, 1)

    flat = lambda a: a.reshape(-1, a.shape[-1])
    y_prompt, y_sample = _ffn(xp, xs, flat(attn_p), flat(attn_s), flat(gla_p), flat(gla_s), mod,
                              w_out, norm_ffn[l].reshape(1, d), w_ffn_in, w_ffn_out,
                              final_norm.reshape(1, d), tm_ffn, dec_seq // tm_ffn)
    y_prompt = y_prompt.reshape(batch, seq, d)
    y_sample = y_sample.reshape(dec_batch, dec_seq, d)

    new_kv_latent = ckv.reshape(batch, 1, seq, KV_LORA)
    new_k_rope = kr.reshape(batch, 1, seq, MLA_ROPE)
    new_state_fwd = sf.reshape(batch, 1, GLA_HEADS, GLA_DK, GLA_DV).astype(x_prompt.dtype)
    new_state_bwd = sb.reshape(batch, 1, GLA_HEADS, GLA_DK, GLA_DV).astype(x_prompt.dtype)
    return (y_prompt, y_sample, new_kv_latent, new_k_rope, new_state_fwd, new_state_bwd)
```

```python
import functools

import numpy as np
import jax
import jax.numpy as jnp
from jax import lax
from jax.experimental import pallas as pl
from jax.experimental.pallas import tpu as pltpu

F32 = jnp.float32
BF16 = jnp.bfloat16

GRID_W = 64
MLA_HEADS = 8
MLA_NOPE = 64
MLA_ROPE = 32
MLA_QK = MLA_NOPE + MLA_ROPE
MLA_V = 64
Q_LORA = 384
KV_LORA = 256
GLA_HEADS = 4
GLA_DK = 64
GLA_DV = 128
GATE_RANK = 16
GATE_NORM = 16.0
CHUNK = 64
D_FF = 2816
ROPE_BASE = 10000.0
EPS = 1e-6
LOG2_E = 1.4426950408889634

LANES = 128
HEAD_PAD = LANES
ROPE_LANE0 = MLA_NOPE
GQK = GLA_HEADS * GLA_DK
GV = GLA_HEADS * GLA_DV
QPAD = MLA_HEADS * HEAD_PAD
VALL = MLA_HEADS * MLA_V
ONES_ROWS = 16
KEY_BLOCK = 1024

W_KR = Q_LORA + KV_LORA
W_GQ = W_KR + MLA_ROPE
W_GF = W_GQ + 2 * GQK + GV
W_GO = W_GF + 2 * GATE_RANK
W_COLS = W_GO + GV

Z_Q = 0
Z_KV = Z_Q + Q_LORA
Z_GQ = Z_KV + KV_LORA
Z_GK = Z_GQ + GQK
Z_GV = Z_GK + GQK
Z_GO = Z_GV + GV
Z_MISC = Z_GO + GV
Z_COLS = Z_MISC + LANES

FF_CHUNK = 256
N_FF_CHUNKS = D_FF // FF_CHUNK
D_MIX = VALL + GV
WOUT_CHUNK = 128
N_WOUT_CHUNKS = D_MIX // WOUT_CHUNK

GLA_TILE = 256
CHUNKS_PER_TILE = GLA_TILE // CHUNK
ADA_ROWS = 128
INPROJ_SUB = 256
ATTN_CTX_SEQS = 4
GLA_CTX_SEQS = 4

VMEM_LIMIT = 56 * 1024 * 1024

_NT = (((1,), (1,)), ((), ()))


def _rms(x, w):
    return x * lax.rsqrt(jnp.mean(x * x, axis=-1, keepdims=True) + EPS) * w


def _sigmoid(x):
    return 1.0 / (1.0 + jnp.exp(-x))


def _log_sigmoid(x):
    return jnp.minimum(x, 0.0) - jnp.log1p(jnp.exp(-jnp.abs(x)))


def _interleave(chains):
    pending, active = list(chains), []
    while pending or active:
        if pending:
            active.append(pending.pop(0))
        for chain in list(active):
            try:
                next(chain)
            except StopIteration:
                active.remove(chain)


def _const_spec(shape):
    nd = len(shape)
    return pl.BlockSpec(shape, lambda *_: (0,) * nd, pipeline_mode=pl.Buffered(1))


def _mod_rows(mod_ref, r):
    return [mod_ref[k, pl.ds(r, 1), :] for k in range(6)]


def _ada_kernel(cond_ref, w_ref, b_ref, o_ref):
    k = pl.program_id(0)
    d = o_ref.shape[2]
    c = cond_ref[...]
    s = (c * _sigmoid(c)).astype(BF16)
    part = jnp.dot(s, w_ref[...].astype(BF16), preferred_element_type=F32)
    for j in range(o_ref.shape[0]):
        sl = slice(j * d, (j + 1) * d)

        @pl.when(k == 0)
        def _():
            o_ref[j] = part[:, sl] + b_ref[:, sl]

        @pl.when(k > 0)
        def _():
            o_ref[j] += part[:, sl]


def _ada(cond8, w_ada, b_ada):
    d = w_ada.shape[0]
    n = w_ada.shape[1]
    steps = d // ADA_ROWS
    cond_k = cond8.reshape(8, steps, ADA_ROWS).transpose(1, 0, 2)
    return pl.pallas_call(
        _ada_kernel,
        out_shape=jax.ShapeDtypeStruct((n // d, 8, d), F32),
        grid=(steps,),
        in_specs=[pl.BlockSpec((None, 8, ADA_ROWS), lambda k: (k, 0, 0)),
                  pl.BlockSpec((ADA_ROWS, n), lambda k: (k, 0)),
                  pl.BlockSpec((1, n), lambda k: (0, 0))],
        out_specs=pl.BlockSpec((n // d, 8, d), lambda k: (0, 0, 0)),
        name="ada_mod",
        compiler_params=pltpu.CompilerParams(dimension_semantics=("arbitrary",)),
    )(cond_k, w_ada, b_ada.reshape(1, n))


def _prep_kernel(wint_ref, wuq_ref, wukv_ref, wgf_ref, wgb_ref, win_o, wuq_o, wk_o, wvt_o, wg_o):
    cols = wint_ref.shape[1]
    for dst, src, n in ((Z_Q, 0, W_KR), (Z_GQ, W_GQ, W_GF - W_GQ), (Z_GO, W_GO, GV)):
        win_o[:, dst:dst + n] = wint_ref[src:src + n, :].T.astype(BF16)
    z32 = jnp.zeros((32, cols), F32)
    misc_t = jnp.concatenate([wint_ref[W_GF:W_GO, :], z32, wint_ref[W_KR:W_GQ, :], z32], axis=0)
    win_o[:, Z_MISC:Z_COLS] = misc_t.T.astype(BF16)

    u = wuq_ref[...]
    zq = jnp.zeros((u.shape[0], HEAD_PAD - MLA_QK), F32)
    for hd in range(MLA_HEADS):
        blk = jnp.concatenate([u[:, hd * MLA_QK:(hd + 1) * MLA_QK], zq], axis=1)
        wuq_o[:, hd * HEAD_PAD:(hd + 1) * HEAD_PAD] = blk.astype(BF16)

    @pl.when(pl.program_id(0) == 0)
    def _():
        kv = wukv_ref[...]
        per = MLA_NOPE + MLA_V
        lane = lax.broadcasted_iota(jnp.int32, (kv.shape[0], per), 1)
        for hd in range(MLA_HEADS):
            blk = kv[:, hd * per:(hd + 1) * per]
            wk_o[:, hd * HEAD_PAD:(hd + 1) * HEAD_PAD] = jnp.where(lane < MLA_NOPE, blk, 0.0).astype(BF16)
        wv = jnp.concatenate([kv[:, hd * per + MLA_NOPE:(hd + 1) * per] for hd in range(MLA_HEADS)], axis=1)
        wvt_o[...] = wv.T.astype(BF16)

        wg_o[...] = jnp.zeros(wg_o.shape, BF16)
        wg_o[0:GATE_RANK, 0:GQK] = wgf_ref[...].astype(BF16)
        wg_o[GATE_RANK:2 * GATE_RANK, GQK:2 * GQK] = wgb_ref[...].astype(BF16)


def _prep_in_weights(w_in, w_uq, w_ukv, w_gate_f, w_gate_b):
    d = w_in.shape[1]
    steps = 4
    w_in_t = jnp.swapaxes(w_in, 1, 2)
    rb3 = lambda r, c: pl.BlockSpec((None, r // steps, c), lambda i: (0, i, 0))
    rb = lambda r, c: pl.BlockSpec((r // steps, c), lambda i: (i, 0))
    full3 = lambda shape: pl.BlockSpec((None,) + tuple(shape[1:]), lambda i: (0, 0, 0))
    full = lambda shape: pl.BlockSpec(shape, lambda i: (0, 0))
    return pl.pallas_call(
        _prep_kernel,
        out_shape=[jax.ShapeDtypeStruct((d, Z_COLS), BF16),
                   jax.ShapeDtypeStruct((Q_LORA, QPAD), BF16),
                   jax.ShapeDtypeStruct((KV_LORA, QPAD), BF16),
                   jax.ShapeDtypeStruct((VALL, KV_LORA), BF16),
                   jax.ShapeDtypeStruct((LANES, 2 * GQK), BF16)],
        grid=(steps,),
        in_specs=[pl.BlockSpec((None, W_COLS, d // steps), lambda i: (0, 0, i)),
                  rb3(Q_LORA, MLA_HEADS * MLA_QK), full3(w_ukv.shape),
                  full3(w_gate_f.shape), full3(w_gate_b.shape)],
        out_specs=[rb(d, Z_COLS), rb(Q_LORA, QPAD), full((KV_LORA, QPAD)), full((VALL, KV_LORA)),
                   full((LANES, 2 * GQK))],
        name="weight_prep",
        compiler_params=pltpu.CompilerParams(dimension_semantics=("arbitrary",)),
    )(w_in_t, w_uq, w_ukv, w_gate_f, w_gate_b)


def _inproj_kernel(*refs, latent, mod_row):
    (x_ref, mod_ref, nw_ref, win_ref, qn_ref, wuq_ref, kvn_ref, wk_ref, wvt_ref, wg_ref, bg_ref) = refs[:11]
    if latent:
        cos_ref, sa_ref, sb_ref = refs[11:14]
        outs = refs[14:]
    else:
        outs = refs[11:]
    q_ref, k_ref, vt_ref, gq_ref, gk_ref, gv_ref, gf_ref, gb_ref, go_ref = outs[:9]

    sh1, sc1 = _mod_rows(mod_ref, mod_row(pl.program_id(0)))[:2]
    scale = MLA_QK ** -0.5 * LOG2_E
    lane = lax.broadcasted_iota(jnp.int32, (INPROJ_SUB, LANES), 1)
    in_rope = (lane >= ROPE_LANE0) & (lane < ROPE_LANE0 + MLA_ROPE)

    def sub_tile(r0):
        rows = slice(r0, r0 + INPROJ_SUB)
        h = (_rms(x_ref[rows, :], nw_ref[...]) * (1.0 + sc1) + sh1).astype(BF16)
        yield
        z_all = jnp.dot(h, win_ref[...], preferred_element_type=F32)
        z = lambda lo, n: z_all[:, lo:lo + n]
        yield
        qn = _rms(z(Z_Q, Q_LORA), qn_ref[...]).astype(BF16)
        ckv = _rms(z(Z_KV, KV_LORA), kvn_ref[...])
        ckv_b = ckv.astype(BF16)
        misc = z(Z_MISC, LANES)
        yield
        q = jnp.dot(qn, wuq_ref[...], preferred_element_type=F32)
        kn = jnp.dot(ckv_b, wk_ref[...], preferred_element_type=F32)
        vt_ref[:, rows] = lax.dot_general(wvt_ref[...], ckv_b, _NT,
                                          preferred_element_type=F32).astype(BF16)
        gpre = jnp.dot(misc.astype(BF16), wg_ref[...], preferred_element_type=F32) + bg_ref[...]
        yield
        if latent:
            cos, sa, sb = cos_ref[rows, :], sa_ref[rows, :], sb_ref[rows, :]

            def rope(t):
                return t * cos + pltpu.roll(t, LANES - 8, 1) * sa + pltpu.roll(t, 8, 1) * sb
        else:
            def rope(t):
                return t

        krope = rope(misc)
        for hd in range(MLA_HEADS):
            sl = slice(hd * HEAD_PAD, (hd + 1) * HEAD_PAD)
            q_ref[rows, sl] = (rope(q[:, sl]) * scale).astype(BF16)
            k_ref[rows, sl] = jnp.where(in_rope, krope, kn[:, sl]).astype(BF16)
        gq_ref[rows, :] = z(Z_GQ, GQK)
        gk_ref[rows, :] = z(Z_GK, GQK)
        gv_ref[rows, :] = z(Z_GV, GV).astype(BF16)
        go_ref[rows, :] = z(Z_GO, GV)
        gate = _log_sigmoid(gpre) * (1.0 / GATE_NORM)
        gf_ref[rows, :] = gate[:, :GQK]
        gb_ref[rows, :] = gate[:, GQK:]
        if not latent:
            ckv_ref, kr_ref = outs[9:]
            ckv_ref[rows, :] = ckv
            kr_ref[rows, :] = misc[:, ROPE_LANE0:ROPE_LANE0 + MLA_ROPE]

    _interleave([sub_tile(r0) for r0 in range(0, x_ref.shape[0], INPROJ_SUB)])


def _inproj(x2d, mod, mod_row, weights, rope_tabs, tm, tiles_per_seq):
    n_tok, d = x2d.shape
    latent = rope_tabs is not None
    nw, win, qn, wuq, kvn, wk, wvt, wg, bg = weights
    row = lambda i: (i, 0)
    in_specs = [pl.BlockSpec((tm, d), row), _const_spec(mod.shape),
                _const_spec(nw.shape), _const_spec(win.shape), _const_spec(qn.shape),
                _const_spec(wuq.shape), _const_spec(kvn.shape), _const_spec(wk.shape),
                _const_spec(wvt.shape), _const_spec(wg.shape), _const_spec(bg.shape)]
    args = [x2d, mod, nw, win, qn, wuq, kvn, wk, wvt, wg, bg]
    if latent:
        tab = pl.BlockSpec((tm, LANES), lambda i: (i % tiles_per_seq, 0))
        in_specs += [tab, tab, tab]
        args += list(rope_tabs)
    out_cols = [(QPAD, BF16), (QPAD, BF16), None, (GQK, F32), (GQK, F32), (GV, BF16),
                (GQK, F32), (GQK, F32), (GV, F32)]
    if not latent:
        out_cols += [(KV_LORA, F32), (MLA_ROPE, F32)]
    out_shape = [jax.ShapeDtypeStruct((n_tok, oc[0]), oc[1]) if oc else
                 jax.ShapeDtypeStruct((VALL, n_tok), BF16) for oc in out_cols]
    out_specs = [pl.BlockSpec((tm, oc[0]), row) if oc else
                 pl.BlockSpec((VALL, tm), lambda i: (0, i)) for oc in out_cols]
    return pl.pallas_call(
        functools.partial(_inproj_kernel, latent=latent, mod_row=mod_row),
        out_shape=out_shape,
        grid=(n_tok // tm,),
        in_specs=in_specs,
        out_specs=out_specs,
        name="inproj_lat" if latent else "inproj_ctx",
        compiler_params=pltpu.CompilerParams(dimension_semantics=("arbitrary",),
                                             vmem_limit_bytes=VMEM_LIMIT),
    )(*args)


def _decomp_kernel(ckv_ref, kr_ref, wk_ref, wvt_ref, k_ref, vt_ref):
    ckv_b = ckv_ref[...].astype(BF16)
    kn = jnp.dot(ckv_b, wk_ref[...], preferred_element_type=F32)
    kr = kr_ref[...]
    lane = lax.broadcasted_iota(jnp.int32, kr.shape, 1)
    in_rope = (lane >= ROPE_LANE0) & (lane < ROPE_LANE0 + MLA_ROPE)
    for hd in range(MLA_HEADS):
        sl = slice(hd * HEAD_PAD, (hd + 1) * HEAD_PAD)
        k_ref[:, sl] = jnp.where(in_rope, kr, kn[:, sl]).astype(BF16)
    vt_ref[...] = lax.dot_general(wvt_ref[...], ckv_b, _NT, preferred_element_type=F32).astype(BF16)


def _decomp(ckv, kr_pad, wk, wvt):
    b, s, _ = ckv.shape
    return pl.pallas_call(
        _decomp_kernel,
        out_shape=[jax.ShapeDtypeStruct((b, s, QPAD), BF16), jax.ShapeDtypeStruct((VALL, b * s), BF16)],
        grid=(b,),
        in_specs=[pl.BlockSpec((None, s, KV_LORA), lambda i: (i, 0, 0)),
                  pl.BlockSpec((None, s, LANES), lambda i: (i, 0, 0)),
                  _const_spec(wk.shape), _const_spec(wvt.shape)],
        out_specs=[pl.BlockSpec((None, s, QPAD), lambda i: (i, 0, 0)),
                   pl.BlockSpec((VALL, s), lambda i: (0, i))],
        name="ctx_decompress",
        compiler_params=pltpu.CompilerParams(dimension_semantics=("arbitrary",)),
    )(ckv, kr_pad, wk, wvt)


def _attn_kernel(*refs, has_ctx, n_seqs):
    if has_ctx:
        q_ref, kc_ref, vct_ref, k_ref, vt_ref, o_ref, st_ref, p_ref = refs
    else:
        q_ref, k_ref, vt_ref, o_ref, st_ref, p_ref = refs
    tq = q_ref.shape[1]

    def key_blocks(bi):
        srcs = [(kc_ref, vct_ref)] if has_ctx else []
        blocks, row0 = [], 0
        for kr, vr in srcs + [(k_ref, vt_ref)]:
            n_keys = kr.shape[1]
            size = min(KEY_BLOCK, n_keys)
            for r in range(0, n_keys, size):
                blocks.append((kr, vr, r, bi * n_keys + r, size, row0))
                row0 += size
        return blocks

    units = [(bi, hd) for bi in range(n_seqs) for hd in range(MLA_HEADS)]
    col_max = [None] * len(units)
    pair = []
    for stage in range(len(units) + 2):
        ua, ub, uc = stage, stage - 1, stage - 2
        run_max = None
        acc = jnp.zeros((MLA_V + ONES_ROWS, tq), F32)
        for j in range(len(key_blocks(0))):
            if ua < len(units):
                bi, hd = units[ua]
                kr, _, r0, _, size, srow = key_blocks(bi)[j]
                sl = slice(hd * HEAD_PAD, (hd + 1) * HEAD_PAD)
                st = lax.dot_general(kr[bi, r0:r0 + size, sl], q_ref[bi, :, sl], _NT,
                                     preferred_element_type=F32)
                st_ref[ua % 2, srow:srow + size, :] = st
                blk_max = jnp.max(st.reshape(size // 8, 8, tq), axis=0)
                run_max = blk_max if run_max is None else jnp.maximum(run_max, blk_max)
            if 0 <= ub < len(units):
                _, _, _, _, size, srow = key_blocks(0)[j]
                p_ref[ub % 2, srow:srow + size, :] = jnp.exp2(
                    st_ref[ub % 2, srow:srow + size, :] - col_max[ub]).astype(BF16)
            if uc >= 0:
                bi, hd = units[uc]
                _, vr, _, c0, size, srow = key_blocks(bi)[j]
                v_aug = jnp.concatenate([vr[hd * MLA_V:(hd + 1) * MLA_V, c0:c0 + size],
                                         jnp.ones((ONES_ROWS, size), BF16)], axis=0)
                acc = acc + jnp.dot(v_aug, p_ref[uc % 2, srow:srow + size, :],
                                    preferred_element_type=F32)
        if ua < len(units):
            col_max[ua] = jnp.max(run_max, axis=0, keepdims=True)
        if uc >= 0:
            bi, hd = units[uc]
            pair.append(acc[:MLA_V, :] / acc[MLA_V:MLA_V + 1, :])
            if len(pair) == 2:
                o_ref[bi, :, (hd - 1) * MLA_V:(hd + 1) * MLA_V] = jnp.concatenate(pair, axis=0).T.astype(BF16)
                pair = []


def _attention(q, k, vt, ctx_kv, tq, n_seqs):
    b, t, _ = q.shape
    assert n_seqs == 1 or tq == t
    has_ctx = ctx_kv is not None
    in_specs = [pl.BlockSpec((n_seqs, tq, QPAD), lambda i, j: (i, j, 0))]
    args = [q]
    if has_ctx:
        kc, vct = ctx_kv
        s = kc.shape[1]
        in_specs += [pl.BlockSpec((n_seqs, s, QPAD), lambda i, j: (i, 0, 0)),
                     pl.BlockSpec((VALL, n_seqs * s), lambda i, j: (0, i))]
        args += [kc, vct]
    in_specs += [pl.BlockSpec((n_seqs, t, QPAD), lambda i, j: (i, 0, 0)),
                 pl.BlockSpec((VALL, n_seqs * t), lambda i, j: (0, i))]
    args += [k, vt]
    return pl.pallas_call(
        functools.partial(_attn_kernel, has_ctx=has_ctx, n_seqs=n_seqs),
        out_shape=jax.ShapeDtypeStruct((b, t, VALL), BF16),
        grid=(b // n_seqs, t // tq),
        in_specs=in_specs,
        out_specs=pl.BlockSpec((n_seqs, tq, VALL), lambda i, j: (i, j, 0)),
        scratch_shapes=[pltpu.VMEM((2, t + (s if has_ctx else 0), tq), F32),
                        pltpu.VMEM((2, t + (s if has_ctx else 0), tq), BF16)],
        name="mla_attn_lat" if has_ctx else "mla_attn_ctx",
        compiler_params=pltpu.CompilerParams(dimension_semantics=("arbitrary", "arbitrary"),
                                             vmem_limit_bytes=VMEM_LIMIT),
    )(*args)


def _gla_kernel(*refs, n_tiles, n_seqs, zero_init):
    gq_ref, gk_ref, gv_ref, gf_ref, gb_ref, go_ref = refs[:6]
    if zero_init:
        gn_ref, o_ref, sf_ref, sb_ref, oacc_ref, bdqk_ref, tri_ref, hm_ref = refs[6:]
    else:
        (sf0_ref, sb0_ref, gn_ref, o_ref, sf_ref, sb_ref, oacc_ref,
         bdqk_ref, tri_ref, hm_ref) = refs[6:]
    g_refs = (gf_ref, gb_ref)
    state_refs = (sf_ref, sb_ref)

    @pl.when(pl.program_id(0) == 0)
    def _():
        ri = lax.broadcasted_iota(jnp.int32, (GLA_TILE, GLA_TILE), 0)
        ci = lax.broadcasted_iota(jnp.int32, (GLA_TILE, GLA_TILE), 1)
        same_chunk = (ri // CHUNK) == (ci // CHUNK)
        bdqk_ref[...] = jnp.where(same_chunk, 1.0, 0.0).astype(BF16)
        tri_ref[0] = jnp.where(same_chunk & (ri >= ci), 1.0, 0.0)
        tri_ref[1] = jnp.where(same_chunk & (ci >= ri), 1.0, 0.0)
        hm_ref[...] = jnp.where(
            lax.broadcasted_iota(jnp.int32, (GLA_HEADS * GLA_TILE, GQK), 0) // GLA_TILE
            == lax.broadcasted_iota(jnp.int32, (GLA_HEADS * GLA_TILE, GQK), 1) // GLA_DK,
            1.0, 0.0).astype(BF16)

    row8 = lax.broadcasted_iota(jnp.int32, (8, GQK), 0)

    def tile_rows(t):
        return pl.ds(pl.multiple_of(t * GLA_TILE, GLA_TILE), GLA_TILE)

    def total_row(c, d):
        return c * CHUNK + (CHUNK - 1 if d == 0 else 0)

    def tile_dir(b, t, d):
        rows = tile_rows(t)
        g = g_refs[d][b, rows, :]
        g_hi = g.astype(BF16)
        g_lo = (g - g_hi.astype(F32)).astype(BF16)
        tri_b = tri_ref[d].astype(BF16)
        cum = (jnp.dot(tri_b, g_hi, preferred_element_type=F32)
               + jnp.dot(tri_b, g_lo, preferred_element_type=F32))
        yield
        totals = [cum[total_row(c, d):total_row(c, d) + 1, :] for c in range(CHUNKS_PER_TILE)]
        tot8 = jnp.zeros((8, GQK), F32)
        for c in range(CHUNKS_PER_TILE):
            tot8 = jnp.where(row8 == c, totals[c], tot8)
        dec_t = jnp.concatenate([jnp.exp(tot8), jnp.zeros((LANES - 8, GQK), F32)], axis=0).T
        q = gq_ref[b, rows, :] * (GLA_DK ** -0.5)
        k = gk_ref[b, rows, :]
        v = gv_ref[b, rows, :]
        tot = jnp.concatenate([jnp.broadcast_to(tc, (CHUNK, GQK)) for tc in totals], axis=0)
        qe = (q * jnp.exp(cum)).astype(BF16)
        ke = (k * jnp.exp(-cum)).astype(BF16)
        kd_t = (k * jnp.exp(tot - cum)).T.astype(BF16)
        bd_qk = bdqk_ref[...] > 0
        tri = tri_ref[d] > 0

        qm = jnp.where(hm_ref[...] > 0, jnp.tile(qe, (GLA_HEADS, 1)), 0.0)
        yield
        att = lax.dot_general(qm, ke, _NT, preferred_element_type=F32)

        yield
        intra, upd = [], []
        for hd in range(GLA_HEADS):
            vh = v[:, hd * GLA_DV:(hd + 1) * GLA_DV]
            a_h = jnp.where(tri, att[hd * GLA_TILE:(hd + 1) * GLA_TILE, :], 0.0).astype(BF16)
            intra.append(jnp.dot(a_h, vh, preferred_element_type=F32))
            kd_h = jnp.tile(kd_t[hd * GLA_DK:(hd + 1) * GLA_DK, :], (CHUNKS_PER_TILE, 1))
            upd.append(jnp.dot(jnp.where(bd_qk, kd_h, 0.0), vh, preferred_element_type=F32))

        yield
        state = [state_refs[d][b, hd] for hd in range(GLA_HEADS)]
        order = range(CHUNKS_PER_TILE) if d == 0 else range(CHUNKS_PER_TILE - 1, -1, -1)
        seen = {}
        for c in order:
            seen[c] = jnp.concatenate(state, axis=0).astype(BF16)
            decay = jnp.broadcast_to(dec_t[:, c:c + 1], (GQK, GLA_DV))
            for hd in range(GLA_HEADS):
                ks = slice(hd * GLA_DK, (hd + 1) * GLA_DK)
                state[hd] = decay[ks, :] * state[hd] + upd[hd][c * CHUNK:(c + 1) * CHUNK, :]
        for hd in range(GLA_HEADS):
            state_refs[d][b, hd] = state[hd]

        yield
        for c in range(CHUNKS_PER_TILE):
            cr = slice(c * CHUNK, (c + 1) * CHUNK)
            q_c = jnp.concatenate([qm[hd * GLA_TILE + c * CHUNK:hd * GLA_TILE + (c + 1) * CHUNK, :]
                                   for hd in range(GLA_HEADS)], axis=0)
            inter = jnp.dot(q_c, seen[c], preferred_element_type=F32)
            o = jnp.concatenate([intra[hd][cr, :] + inter[hd * CHUNK:(hd + 1) * CHUNK, :]
                                 for hd in range(GLA_HEADS)], axis=1)
            oacc_ref[d, b, pl.ds(pl.multiple_of(t * GLA_TILE + c * CHUNK, CHUNK), CHUNK), :] = o

    if zero_init:
        sf_ref[...] = jnp.zeros(sf_ref.shape, F32)
        sb_ref[...] = jnp.zeros(sb_ref.shape, F32)
    else:
        sf_ref[...] = sf0_ref[...]
        sb_ref[...] = sb0_ref[...]

    tiles_per_step = 2 if n_tiles % 2 == 0 else 1

    def main_body(i, carry):
        chains = []
        for u in range(tiles_per_step):
            t = i * tiles_per_step + u
            for b in range(n_seqs):
                chains += [tile_dir(b, t, 0), tile_dir(b, n_tiles - 1 - t, 1)]
        _interleave(chains)
        return carry

    lax.fori_loop(0, n_tiles // tiles_per_step, main_body, 0)

    gn = gn_ref[...]

    def epilogue_body(t, carry):
        rows = tile_rows(t)
        for b in range(n_seqs):
            for hd in range(GLA_HEADS):
                vs = slice(hd * GLA_DV, (hd + 1) * GLA_DV)
                o = _rms(oacc_ref[0, b, rows, vs] + oacc_ref[1, b, rows, vs], gn)
                go = go_ref[b, rows, vs]
                o_ref[b, rows, vs] = (o * (go * _sigmoid(go))).astype(BF16)
        return carry

    lax.fori_loop(0, n_tiles, epilogue_body, 0)


def _gla(gq, gk, gv, gf, gb, go, init_states, gn, n_seqs):
    b, t, _ = gq.shape
    n_tiles = t // GLA_TILE
    zero_init = init_states is None
    seq = lambda c: pl.BlockSpec((n_seqs, t, c), lambda i: (i, 0, 0))
    st = pl.BlockSpec((n_seqs, GLA_HEADS, GLA_DK, GLA_DV), lambda i: (i, 0, 0, 0))
    in_specs = [seq(GQK), seq(GQK), seq(GV), seq(GQK), seq(GQK), seq(GV)]
    args = [gq, gk, gv, gf, gb, go]
    if not zero_init:
        in_specs += [st, st]
        args += list(init_states)
    in_specs.append(_const_spec(gn.shape))
    args.append(gn)
    return pl.pallas_call(
        functools.partial(_gla_kernel, n_tiles=n_tiles, n_seqs=n_seqs, zero_init=zero_init),
        out_shape=[jax.ShapeDtypeStruct((b, t, GV), BF16),
                   jax.ShapeDtypeStruct((b, GLA_HEADS, GLA_DK, GLA_DV), F32),
                   jax.ShapeDtypeStruct((b, GLA_HEADS, GLA_DK, GLA_DV), F32)],
        grid=(b // n_seqs,),
        in_specs=in_specs,
        out_specs=[seq(GV), st, st],
        scratch_shapes=[pltpu.VMEM((2, n_seqs, t, GV), F32),
                        pltpu.VMEM((GQK, GLA_TILE), BF16),
                        pltpu.VMEM((2, GLA_TILE, GLA_TILE), F32),
                        pltpu.VMEM((GLA_HEADS * GLA_TILE, GQK), BF16)],
        name="gla_%d" % t,
        compiler_params=pltpu.CompilerParams(dimension_semantics=("arbitrary",),
                                             vmem_limit_bytes=VMEM_LIMIT),
    )(*args)


def _ffn_kernel(xp_ref, xs_ref, atp_ref, ats_ref, glp_ref, gls_ref, mod_ref, wout_ref, nf_ref,
                wa_ref, wg_ref, wfo_ref, fn_ref, yp_ref, ys_ref,
                wout_s, wfi_s, wfo_s, act_ref, *, ctx_tiles, tiles_per_seq):
    s = pl.program_id(0)

    @pl.when(s < N_FF_CHUNKS)
    def _():
        wfi_s[s, :, 0:FF_CHUNK] = wa_ref[...].astype(BF16)
        wfi_s[s, :, FF_CHUNK:2 * FF_CHUNK] = wg_ref[...].astype(BF16)
        wfo_s[pl.ds(pl.multiple_of(s * FF_CHUNK, FF_CHUNK), FF_CHUNK), :] = wfo_ref[...].astype(BF16)

        @pl.when(s < N_WOUT_CHUNKS)
        def _():
            wout_s[pl.ds(pl.multiple_of(s * WOUT_CHUNK, WOUT_CHUNK), WOUT_CHUNK), :] = (
                wout_ref[...].astype(BF16))

    def tile(x_ref, at_ref, gl_ref, y_ref, mod_row):
        _, _, gt1, sh2, sc2, gt2 = _mod_rows(mod_ref, mod_row)
        mix = (jnp.dot(at_ref[...], wout_s[0:VALL, :], preferred_element_type=F32)
               + jnp.dot(gl_ref[...], wout_s[VALL:, :], preferred_element_type=F32))
        x1 = x_ref[...] + gt1 * mix
        h2 = (_rms(x1, nf_ref[...]) * (1.0 + sc2) + sh2).astype(BF16)
        for j in range(N_FF_CHUNKS):
            ag = jnp.dot(h2, wfi_s[j], preferred_element_type=F32)
            a = ag[:, :FF_CHUNK]
            act_ref[:, j * FF_CHUNK:(j + 1) * FF_CHUNK] = (a * _sigmoid(a) * ag[:, FF_CHUNK:]).astype(BF16)
        ff = jnp.dot(act_ref[...], wfo_s[...], preferred_element_type=F32)
        x2 = x1 + gt2 * ff
        y_ref[...] = _rms(x2, fn_ref[...])

    t = s - N_FF_CHUNKS

    @pl.when((t >= 0) & (t < ctx_tiles))
    def _():
        tile(xp_ref, atp_ref, glp_ref, yp_ref, 0)

    @pl.when(t >= ctx_tiles)
    def _():
        tile(xs_ref, ats_ref, gls_ref, ys_ref, 1 + (t - ctx_tiles) // tiles_per_seq)


def _ffn(xp, xs, attn_p, attn_s, gla_p, gla_s, mod, w_out, nf, w_ffn_in, w_ffn_out, fn, tm, tiles_per_seq):
    d = xp.shape[1]
    ctx_tiles = xp.shape[0] // tm
    lat_tiles = xs.shape[0] // tm
    nw = N_FF_CHUNKS
    ctx_map = lambda s: (jnp.clip(s - nw, 0, ctx_tiles - 1), 0)
    lat_map = lambda s: (jnp.clip(s - nw - ctx_tiles, 0, lat_tiles - 1), 0)
    tile = lambda c, m: pl.BlockSpec((tm, c), m)
    return pl.pallas_call(
        functools.partial(_ffn_kernel, ctx_tiles=ctx_tiles, tiles_per_seq=tiles_per_seq),
        out_shape=[jax.ShapeDtypeStruct(xp.shape, F32), jax.ShapeDtypeStruct(xs.shape, F32)],
        grid=(nw + ctx_tiles + lat_tiles,),
        in_specs=[tile(d, ctx_map), tile(d, lat_map), tile(VALL, ctx_map), tile(VALL, lat_map),
                  tile(GV, ctx_map), tile(GV, lat_map), _const_spec(mod.shape),
                  pl.BlockSpec((None, WOUT_CHUNK, d), lambda s: (0, jnp.minimum(s, N_WOUT_CHUNKS - 1), 0)),
                  _const_spec(nf.shape),
                  pl.BlockSpec((None, d, FF_CHUNK), lambda s: (0, 0, jnp.minimum(s, nw - 1))),
                  pl.BlockSpec((None, d, FF_CHUNK), lambda s: (0, 0, nw + jnp.minimum(s, nw - 1))),
                  pl.BlockSpec((None, FF_CHUNK, d), lambda s: (0, jnp.minimum(s, nw - 1), 0)),
                  _const_spec(fn.shape)],
        out_specs=[tile(d, ctx_map), tile(d, lat_map)],
        scratch_shapes=[pltpu.VMEM((D_MIX, d), BF16),
                        pltpu.VMEM((N_FF_CHUNKS, d, 2 * FF_CHUNK), BF16),
                        pltpu.VMEM((D_FF, d), BF16),
                        pltpu.VMEM((tm, D_FF), BF16)],
        name="out_ffn",
        compiler_params=pltpu.CompilerParams(dimension_semantics=("arbitrary",),
                                             vmem_limit_bytes=VMEM_LIMIT),
    )(xp, xs, attn_p, attn_s, gla_p, gla_s, mod, w_out, nf, w_ffn_in, w_ffn_in, w_ffn_out, fn)


def _rope_tables(n_tokens):
    t = np.arange(n_tokens)
    row = (t // GRID_W).astype(np.float32)
    col = (t % GRID_W).astype(np.float32)
    half = MLA_ROPE // 2
    inv = (np.float32(ROPE_BASE) ** (-np.arange(0, half, 2, dtype=np.float32) / np.float32(half))).astype(np.float32)
    ang_r = row[:, None] * inv
    ang_c = col[:, None] * inv
    ang = np.concatenate([ang_r, ang_r, ang_c, ang_c], axis=-1).astype(np.float32)
    cos, sin = np.cos(ang), np.sin(ang)
    first = (np.arange(MLA_ROPE) % half) < (half // 2)
    cos_t = np.ones((n_tokens, LANES), np.float32)
    sa_t = np.zeros((n_tokens, LANES), np.float32)
    sb_t = np.zeros((n_tokens, LANES), np.float32)
    cos_t[:, ROPE_LANE0:ROPE_LANE0 + MLA_ROPE] = cos
    sa_t[:, ROPE_LANE0:ROPE_LANE0 + MLA_ROPE] = np.where(first, -sin, 0.0)
    sb_t[:, ROPE_LANE0:ROPE_LANE0 + MLA_ROPE] = np.where(first, 0.0, sin)
    return jnp.asarray(cos_t), jnp.asarray(sa_t), jnp.asarray(sb_t)


def kernel(x_prompt, x_sample, cache_kv_latent, cache_k_rope, state_gla_fwd, state_gla_bwd, c, c_ctx, w_ada, b_ada, norm_attn, w_in, mla_q_norm, w_uq, mla_kv_norm, w_ukv, w_gate_f, b_gate_f, w_gate_b, b_gate_b, gla_norm, w_out, norm_ffn, w_ffn_in, w_ffn_out, final_norm):
    batch, seq, d = x_prompt.shape
    dec_batch, dec_seq, _ = x_sample.shape
    assert w_ada.shape[0] == 1 and w_in.shape[-1] == W_COLS and w_ffn_in.shape[-1] == 2 * D_FF
    l = 0

    cond8 = jnp.concatenate([c_ctx[None, :], c, jnp.zeros((8 - 1 - dec_batch, d), F32)], axis=0)
    mod = _ada(cond8, w_ada[l], b_ada[l])

    win, wuq, wk, wvt, wg = _prep_in_weights(w_in, w_uq, w_ukv, w_gate_f, w_gate_b)
    bg = jnp.concatenate([b_gate_f[l], b_gate_b[l]]).reshape(1, 2 * GQK)
    in_w = (norm_attn[l].reshape(1, d), win, mla_q_norm[l].reshape(1, Q_LORA), wuq,
            mla_kv_norm[l].reshape(1, KV_LORA), wk, wvt, wg, bg)
    gn = gla_norm[l].reshape(1, GLA_DV)
    tm, tm_ffn = 512, 512
    r3 = lambda a, b_, t: a.reshape(b_, t, a.shape[-1])

    xp = x_prompt.reshape(batch * seq, d)
    (q, k, vt, gq, gk, gv, gf, gb, go, ckv, kr) = _inproj(xp, mod, lambda i: 0, in_w, None, tm, seq // tm)
    attn_p = _attention(r3(q, batch, seq), r3(k, batch, seq), vt, None, seq, ATTN_CTX_SEQS)
    gla_p, sf, sb = _gla(r3(gq, batch, seq), r3(gk, batch, seq), r3(gv, batch, seq), r3(gf, batch, seq),
                         r3(gb, batch, seq), r3(go, batch, seq), None, gn, GLA_CTX_SEQS)

    xs = x_sample.reshape(dec_batch * dec_seq, d)
    tiles = dec_seq // tm
    (q, k, vt, gq, gk, gv, gf, gb, go) = _inproj(xs, mod, lambda i: 1 + i // tiles, in_w,
                                                  _rope_tables(dec_seq), tm, tiles)
    kr_pad = jnp.pad(cache_k_rope[:, l], ((0, 0), (0, 0), (ROPE_LANE0, LANES - ROPE_LANE0 - MLA_ROPE)))
    kc, vct = _decomp(cache_kv_latent[:, l], kr_pad, wk, wvt)
    attn_s = _attention(r3(q, dec_batch, dec_seq), r3(k, dec_batch, dec_seq), vt, (kc, vct), 256, 1)
    gla_s, _, _ = _gla(r3(gq, dec_batch, dec_seq), r3(gk, dec_batch, dec_seq), r3(gv, dec_batch, dec_seq),
                       r3(gf, dec_batch, dec_seq), r3(gb, dec_batch, dec_seq), r3(go, dec_batch, dec_seq),
                       (state_gla_fwd[:, l].astype(F32), state_gla_bwd[:, l].astype(F32)), gn, 1)

    flat = lambda a: a.reshape(-1, a.shape[-1])
    y_prompt, y_sample = _ffn(xp, xs, flat(attn_p), flat(attn_s), flat(gla_p), flat(gla_s), mod,
                              w_out, norm_ffn[l].reshape(1, d), w_ffn_in, w_ffn_out,
                              final_norm.reshape(1, d), tm_ffn, dec_seq // tm_ffn)
    y_prompt = y_prompt.reshape(batch, seq, d)
    y_sample = y_sample.reshape(dec_batch, dec_seq, d)

    new_kv_latent = ckv.reshape(batch, 1, seq, KV_LORA)
    new_k_rope = kr.reshape(batch, 1, seq, MLA_ROPE)
    new_state_fwd = sf.reshape(batch, 1, GLA_HEADS, GLA_DK, GLA_DV).astype(x_prompt.dtype)
    new_state_bwd = sb.reshape(batch, 1, GLA_HEADS, GLA_DK, GLA_DV).astype(x_prompt.dtype)
    return (y_prompt, y_sample, new_kv_latent, new_k_rope, new_state_fwd, new_state_bwd)
```

```python
import functools

import numpy as np
import jax
import jax.numpy as jnp
from jax import lax
from jax.experimental import pallas as pl
from jax.experimental.pallas import tpu as pltpu

F32 = jnp.float32
BF16 = jnp.bfloat16

GRID_W = 64
MLA_HEADS = 8
MLA_NOPE = 64
MLA_ROPE = 32
MLA_QK = MLA_NOPE + MLA_ROPE
MLA_V = 64
Q_LORA = 384
KV_LORA = 256
GLA_HEADS = 4
GLA_DK = 64
GLA_DV = 128
GATE_RANK = 16
GATE_NORM = 16.0
CHUNK = 64
D_FF = 2816
ROPE_BASE = 10000.0
EPS = 1e-6
LOG2_E = 1.4426950408889634

LANES = 128
HEAD_PAD = LANES
ROPE_LANE0 = MLA_NOPE
GQK = GLA_HEADS * GLA_DK
GV = GLA_HEADS * GLA_DV
QPAD = MLA_HEADS * HEAD_PAD
VALL = MLA_HEADS * MLA_V
ONES_ROWS = 16
KEY_BLOCK = 1024

W_KR = Q_LORA + KV_LORA
W_GQ = W_KR + MLA_ROPE
W_GF = W_GQ + 2 * GQK + GV
W_GO = W_GF + 2 * GATE_RANK
W_COLS = W_GO + GV

Z_Q = 0
Z_KV = Z_Q + Q_LORA
Z_GQ = Z_KV + KV_LORA
Z_GK = Z_GQ + GQK
Z_GV = Z_GK + GQK
Z_GO = Z_GV + GV
Z_MISC = Z_GO + GV
Z_COLS = Z_MISC + LANES

FF_CHUNK = 256
N_FF_CHUNKS = D_FF // FF_CHUNK

GLA_TILE = 256
CHUNKS_PER_TILE = GLA_TILE // CHUNK
ADA_ROWS = 128
INPROJ_SUB = 512
Q_TILE = 256
ATTN_LAT_QUERIES = 512
ATTN_CTX_SEQS = 4
GLA_CTX_SEQS = 4

VMEM_LIMIT = 56 * 1024 * 1024

_NT = (((1,), (1,)), ((), ()))


def _rms(x, w):
    return x * lax.rsqrt(jnp.mean(x * x, axis=-1, keepdims=True) + EPS) * w


def _sigmoid(x):
    return 1.0 / (1.0 + jnp.exp(-x))


def _log_sigmoid(x):
    return jnp.minimum(x, 0.0) - jnp.log1p(jnp.exp(-jnp.abs(x)))


def _interleave(chains):
    pending, active = list(chains), []
    while pending or active:
        if pending:
            active.append(pending.pop(0))
        for chain in list(active):
            try:
                next(chain)
            except StopIteration:
                active.remove(chain)


def _const_spec(shape):
    nd = len(shape)
    return pl.BlockSpec(shape, lambda *_: (0,) * nd, pipeline_mode=pl.Buffered(1))


def _mod_rows(mod_ref, r):
    return [mod_ref[k, pl.ds(r, 1), :] for k in range(6)]


def _ada_kernel(cond_ref, w_ref, b_ref, o_ref):
    k = pl.program_id(0)
    d = o_ref.shape[2]
    c = cond_ref[...]
    s = (c * _sigmoid(c)).astype(BF16)
    part = jnp.dot(s, w_ref[...].astype(BF16), preferred_element_type=F32)
    for j in range(o_ref.shape[0]):
        sl = slice(j * d, (j + 1) * d)

        @pl.when(k == 0)
        def _():
            o_ref[j] = part[:, sl] + b_ref[:, sl]

        @pl.when(k > 0)
        def _():
            o_ref[j] += part[:, sl]


def _ada(cond8, w_ada, b_ada):
    d = w_ada.shape[0]
    n = w_ada.shape[1]
    steps = d // ADA_ROWS
    cond_k = cond8.reshape(8, steps, ADA_ROWS).transpose(1, 0, 2)
    return pl.pallas_call(
        _ada_kernel,
        out_shape=jax.ShapeDtypeStruct((n // d, 8, d), F32),
        grid=(steps,),
        in_specs=[pl.BlockSpec((None, 8, ADA_ROWS), lambda k: (k, 0, 0)),
                  pl.BlockSpec((ADA_ROWS, n), lambda k: (k, 0)),
                  pl.BlockSpec((1, n), lambda k: (0, 0))],
        out_specs=pl.BlockSpec((n // d, 8, d), lambda k: (0, 0, 0)),
        name="ada_mod",
        compiler_params=pltpu.CompilerParams(dimension_semantics=("arbitrary",)),
    )(cond_k, w_ada, b_ada.reshape(1, n))


def _prep_kernel(wint_ref, wuq_ref, wukv_ref, wgf_ref, wgb_ref, win_o, wuq_o, wk_o, wvt_o, wg_o):
    cols = wint_ref.shape[1]
    for dst, src, n in ((Z_Q, 0, W_KR), (Z_GQ, W_GQ, W_GF - W_GQ), (Z_GO, W_GO, GV)):
        win_o[:, dst:dst + n] = wint_ref[src:src + n, :].T.astype(BF16)
    z32 = jnp.zeros((32, cols), F32)
    misc_t = jnp.concatenate([wint_ref[W_GF:W_GO, :], z32, wint_ref[W_KR:W_GQ, :], z32], axis=0)
    win_o[:, Z_MISC:Z_COLS] = misc_t.T.astype(BF16)

    u = wuq_ref[...]
    zq = jnp.zeros((u.shape[0], HEAD_PAD - MLA_QK), F32)
    for hd in range(MLA_HEADS):
        blk = jnp.concatenate([u[:, hd * MLA_QK:(hd + 1) * MLA_QK], zq], axis=1)
        wuq_o[:, hd * HEAD_PAD:(hd + 1) * HEAD_PAD] = blk.astype(BF16)

    @pl.when(pl.program_id(0) == 0)
    def _():
        kv = wukv_ref[...]
        per = MLA_NOPE + MLA_V
        lane = lax.broadcasted_iota(jnp.int32, (kv.shape[0], per), 1)
        for hd in range(MLA_HEADS):
            blk = kv[:, hd * per:(hd + 1) * per]
            wk_o[:, hd * HEAD_PAD:(hd + 1) * HEAD_PAD] = jnp.where(lane < MLA_NOPE, blk, 0.0).astype(BF16)
        wv = jnp.concatenate([kv[:, hd * per + MLA_NOPE:(hd + 1) * per] for hd in range(MLA_HEADS)], axis=1)
        wvt_o[...] = wv.T.astype(BF16)

        wg_o[...] = jnp.zeros(wg_o.shape, BF16)
        wg_o[0:GATE_RANK, 0:GQK] = wgf_ref[...].astype(BF16)
        wg_o[GATE_RANK:2 * GATE_RANK, GQK:2 * GQK] = wgb_ref[...].astype(BF16)


def _prep_in_weights(w_in, w_uq, w_ukv, w_gate_f, w_gate_b):
    d = w_in.shape[1]
    steps = 4
    w_in_t = jnp.swapaxes(w_in, 1, 2)
    rb3 = lambda r, c: pl.BlockSpec((None, r // steps, c), lambda i: (0, i, 0))
    rb = lambda r, c: pl.BlockSpec((r // steps, c), lambda i: (i, 0))
    full3 = lambda shape: pl.BlockSpec((None,) + tuple(shape[1:]), lambda i: (0, 0, 0))
    full = lambda shape: pl.BlockSpec(shape, lambda i: (0, 0))
    return pl.pallas_call(
        _prep_kernel,
        out_shape=[jax.ShapeDtypeStruct((d, Z_COLS), BF16),
                   jax.ShapeDtypeStruct((Q_LORA, QPAD), BF16),
                   jax.ShapeDtypeStruct((KV_LORA, QPAD), BF16),
                   jax.ShapeDtypeStruct((VALL, KV_LORA), BF16),
                   jax.ShapeDtypeStruct((LANES, 2 * GQK), BF16)],
        grid=(steps,),
        in_specs=[pl.BlockSpec((None, W_COLS, d // steps), lambda i: (0, 0, i)),
                  rb3(Q_LORA, MLA_HEADS * MLA_QK), full3(w_ukv.shape),
                  full3(w_gate_f.shape), full3(w_gate_b.shape)],
        out_specs=[rb(d, Z_COLS), rb(Q_LORA, QPAD), full((KV_LORA, QPAD)), full((VALL, KV_LORA)),
                   full((LANES, 2 * GQK))],
        name="weight_prep",
        compiler_params=pltpu.CompilerParams(dimension_semantics=("arbitrary",)),
    )(w_in_t, w_uq, w_ukv, w_gate_f, w_gate_b)


def _inproj_kernel(*refs, latent, mod_row):
    (x_ref, mod_ref, nw_ref, win_ref, qn_ref, wuq_ref, kvn_ref, wk_ref, wvt_ref, wg_ref, bg_ref) = refs[:11]
    if latent:
        cos_ref, sa_ref, sb_ref = refs[11:14]
        outs = refs[14:]
    else:
        outs = refs[11:]
    q_ref, k_ref, vt_ref, gq_ref, gk_ref, gv_ref, gf_ref, gb_ref, go_ref = outs[:9]

    sh1, sc1 = _mod_rows(mod_ref, mod_row(pl.program_id(0)))[:2]
    scale = MLA_QK ** -0.5 * LOG2_E
    lane = lax.broadcasted_iota(jnp.int32, (INPROJ_SUB, LANES), 1)
    in_rope = (lane >= ROPE_LANE0) & (lane < ROPE_LANE0 + MLA_ROPE)

    def sub_tile(r0):
        rows = slice(r0, r0 + INPROJ_SUB)
        h = (_rms(x_ref[rows, :], nw_ref[...]) * (1.0 + sc1) + sh1).astype(BF16)
        yield
        z_all = jnp.dot(h, win_ref[...], preferred_element_type=F32)
        z = lambda lo, n: z_all[:, lo:lo + n]
        yield
        qn = _rms(z(Z_Q, Q_LORA), qn_ref[...]).astype(BF16)
        ckv = _rms(z(Z_KV, KV_LORA), kvn_ref[...])
        ckv_b = ckv.astype(BF16)
        misc = z(Z_MISC, LANES)
        yield
        q = jnp.dot(qn, wuq_ref[...], preferred_element_type=F32)
        kn = jnp.dot(ckv_b, wk_ref[...], preferred_element_type=F32)
        vt_ref[:, rows] = lax.dot_general(wvt_ref[...], ckv_b, _NT,
                                          preferred_element_type=F32).astype(BF16)
        gpre = jnp.dot(misc.astype(BF16), wg_ref[...], preferred_element_type=F32) + bg_ref[...]
        yield
        if latent:
            cos, sa, sb = cos_ref[rows, :], sa_ref[rows, :], sb_ref[rows, :]

            def rope(t):
                return t * cos + pltpu.roll(t, LANES - 8, 1) * sa + pltpu.roll(t, 8, 1) * sb
        else:
            def rope(t):
                return t

        krope = rope(misc)
        for hd in range(MLA_HEADS):
            sl = slice(hd * HEAD_PAD, (hd + 1) * HEAD_PAD)
            q_ref[rows, sl] = (rope(q[:, sl]) * scale).astype(BF16)
            k_ref[rows, sl] = jnp.where(in_rope, krope, kn[:, sl]).astype(BF16)
        gq_ref[rows, :] = z(Z_GQ, GQK)
        gk_ref[rows, :] = z(Z_GK, GQK)
        gv_ref[rows, :] = z(Z_GV, GV).astype(BF16)
        go_ref[rows, :] = z(Z_GO, GV)
        gate = _log_sigmoid(gpre) * (1.0 / GATE_NORM)
        gf_ref[rows, :] = gate[:, :GQK]
        gb_ref[rows, :] = gate[:, GQK:]
        if not latent:
            ckv_ref, kr_ref = outs[9:]
            ckv_ref[rows, :] = ckv
            kr_ref[rows, :] = misc[:, ROPE_LANE0:ROPE_LANE0 + MLA_ROPE]

    _interleave([sub_tile(r0) for r0 in range(0, x_ref.shape[0], INPROJ_SUB)])


def _inproj(x2d, mod, mod_row, weights, rope_tabs, tm, tiles_per_seq):
    n_tok, d = x2d.shape
    latent = rope_tabs is not None
    nw, win, qn, wuq, kvn, wk, wvt, wg, bg = weights
    row = lambda i: (i, 0)
    in_specs = [pl.BlockSpec((tm, d), row), _const_spec(mod.shape),
                _const_spec(nw.shape), _const_spec(win.shape), _const_spec(qn.shape),
                _const_spec(wuq.shape), _const_spec(kvn.shape), _const_spec(wk.shape),
                _const_spec(wvt.shape), _const_spec(wg.shape), _const_spec(bg.shape)]
    args = [x2d, mod, nw, win, qn, wuq, kvn, wk, wvt, wg, bg]
    if latent:
        tab = pl.BlockSpec((tm, LANES), lambda i: (i % tiles_per_seq, 0))
        in_specs += [tab, tab, tab]
        args += list(rope_tabs)
    out_cols = [(QPAD, BF16), (QPAD, BF16), None, (GQK, F32), (GQK, F32), (GV, BF16),
                (GQK, F32), (GQK, F32), (GV, F32)]
    if not latent:
        out_cols += [(KV_LORA, F32), (MLA_ROPE, F32)]
    out_shape = [jax.ShapeDtypeStruct((n_tok, oc[0]), oc[1]) if oc else
                 jax.ShapeDtypeStruct((VALL, n_tok), BF16) for oc in out_cols]
    out_specs = [pl.BlockSpec((tm, oc[0]), row) if oc else
                 pl.BlockSpec((VALL, tm), lambda i: (0, i)) for oc in out_cols]
    return pl.pallas_call(
        functools.partial(_inproj_kernel, latent=latent, mod_row=mod_row),
        out_shape=out_shape,
        grid=(n_tok // tm,),
        in_specs=in_specs,
        out_specs=out_specs,
        name="inproj_lat" if latent else "inproj_ctx",
        compiler_params=pltpu.CompilerParams(dimension_semantics=("arbitrary",),
                                             vmem_limit_bytes=VMEM_LIMIT),
    )(*args)


def _decomp_kernel(ckv_ref, kr_ref, wk_ref, wvt_ref, k_ref, vt_ref):
    ckv_b = ckv_ref[...].astype(BF16)
    kn = jnp.dot(ckv_b, wk_ref[...], preferred_element_type=F32)
    kr = kr_ref[...]
    lane = lax.broadcasted_iota(jnp.int32, kr.shape, 1)
    in_rope = (lane >= ROPE_LANE0) & (lane < ROPE_LANE0 + MLA_ROPE)
    for hd in range(MLA_HEADS):
        sl = slice(hd * HEAD_PAD, (hd + 1) * HEAD_PAD)
        k_ref[:, sl] = jnp.where(in_rope, kr, kn[:, sl]).astype(BF16)
    vt_ref[...] = lax.dot_general(wvt_ref[...], ckv_b, _NT, preferred_element_type=F32).astype(BF16)


def _decomp(ckv, kr_pad, wk, wvt):
    b, s, _ = ckv.shape
    return pl.pallas_call(
        _decomp_kernel,
        out_shape=[jax.ShapeDtypeStruct((b, s, QPAD), BF16), jax.ShapeDtypeStruct((VALL, b * s), BF16)],
        grid=(b,),
        in_specs=[pl.BlockSpec((None, s, KV_LORA), lambda i: (i, 0, 0)),
                  pl.BlockSpec((None, s, LANES), lambda i: (i, 0, 0)),
                  _const_spec(wk.shape), _const_spec(wvt.shape)],
        out_specs=[pl.BlockSpec((None, s, QPAD), lambda i: (i, 0, 0)),
                   pl.BlockSpec((VALL, s), lambda i: (0, i))],
        name="ctx_decompress",
        compiler_params=pltpu.CompilerParams(dimension_semantics=("arbitrary",)),
    )(ckv, kr_pad, wk, wvt)


def _attn_kernel(*refs, has_ctx, n_seqs, n_side):
    n_in = 5 if has_ctx else 3
    side_in, refs = refs[n_in:n_in + n_side], refs[:n_in] + refs[n_in + n_side:]
    side_out, refs = refs[n_in + 1:n_in + 1 + n_side], refs[:n_in + 1] + refs[n_in + 1 + n_side:]
    if has_ctx:
        q_ref, kc_ref, vct_ref, k_ref, vt_ref, o_ref, st_ref, p_ref = refs
    else:
        q_ref, k_ref, vt_ref, o_ref, st_ref, p_ref = refs
    tq = Q_TILE

    for src, dst in zip(side_in, side_out):
        dst[...] = src[...].astype(BF16)

    def key_blocks(bi):
        srcs = [(kc_ref, vct_ref)] if has_ctx else []
        blocks, row0 = [], 0
        for kr, vr in srcs + [(k_ref, vt_ref)]:
            n_keys = kr.shape[1]
            size = min(KEY_BLOCK, n_keys)
            for r in range(0, n_keys, size):
                blocks.append((kr, vr, r, bi * n_keys + r, size, row0))
                row0 += size
        return blocks

    units = [(bi, slice(q0, q0 + tq), hd) for bi in range(n_seqs)
             for q0 in range(0, q_ref.shape[1], tq) for hd in range(MLA_HEADS)]
    col_max = [None] * len(units)
    pair = []
    for stage in range(len(units) + 2):
        ua, ub, uc = stage, stage - 1, stage - 2
        run_max = None
        acc = jnp.zeros((MLA_V + ONES_ROWS, tq), F32)
        for j in range(len(key_blocks(0))):
            if ua < len(units):
                bi, qrows, hd = units[ua]
                kr, _, r0, _, size, srow = key_blocks(bi)[j]
                sl = slice(hd * HEAD_PAD, (hd + 1) * HEAD_PAD)
                st = lax.dot_general(kr[bi, r0:r0 + size, sl], q_ref[bi, qrows, sl], _NT,
                                     preferred_element_type=F32)
                st_ref[ua % 2, srow:srow + size, :] = st
                blk_max = jnp.max(st.reshape(size // 8, 8, tq), axis=0)
                run_max = blk_max if run_max is None else jnp.maximum(run_max, blk_max)
            if 0 <= ub < len(units):
                _, _, _, _, size, srow = key_blocks(0)[j]
                p_ref[ub % 2, srow:srow + size, :] = jnp.exp2(
                    st_ref[ub % 2, srow:srow + size, :] - col_max[ub]).astype(BF16)
            if uc >= 0:
                bi, _, hd = units[uc]
                _, vr, _, c0, size, srow = key_blocks(bi)[j]
                v_aug = jnp.concatenate([vr[hd * MLA_V:(hd + 1) * MLA_V, c0:c0 + size],
                                         jnp.ones((ONES_ROWS, size), BF16)], axis=0)
                acc = acc + jnp.dot(v_aug, p_ref[uc % 2, srow:srow + size, :],
                                    preferred_element_type=F32)
        if ua < len(units):
            col_max[ua] = jnp.max(run_max, axis=0, keepdims=True)
        if uc >= 0:
            bi, qrows, hd = units[uc]
            pair.append(acc[:MLA_V, :] / acc[MLA_V:MLA_V + 1, :])
            if len(pair) == 2:
                o_ref[bi, qrows, (hd - 1) * MLA_V:(hd + 1) * MLA_V] = (
                    jnp.concatenate(pair, axis=0).T.astype(BF16))
                pair = []


def _attention(q, k, vt, ctx_kv, tq, n_seqs, side_weights=()):
    b, t, _ = q.shape
    steps = (b // n_seqs) * (t // tq)
    assert (n_seqs == 1 or tq == t) and tq % Q_TILE == 0
    has_ctx = ctx_kv is not None
    in_specs = [pl.BlockSpec((n_seqs, tq, QPAD), lambda i, j: (i, j, 0))]
    args = [q]
    if has_ctx:
        kc, vct = ctx_kv
        s = kc.shape[1]
        in_specs += [pl.BlockSpec((n_seqs, s, QPAD), lambda i, j: (i, 0, 0)),
                     pl.BlockSpec((VALL, n_seqs * s), lambda i, j: (0, i))]
        args += [kc, vct]
    in_specs += [pl.BlockSpec((n_seqs, t, QPAD), lambda i, j: (i, 0, 0)),
                 pl.BlockSpec((VALL, n_seqs * t), lambda i, j: (0, i))]
    args += [k, vt]
    out_shape = [jax.ShapeDtypeStruct((b, t, VALL), BF16)]
    out_specs = [pl.BlockSpec((n_seqs, tq, VALL), lambda i, j: (i, j, 0))]
    nj = t // tq
    for w in side_weights:
        _, rows, cols = w.shape
        assert rows % (16 * steps) == 0
        in_specs.append(pl.BlockSpec((None, rows // steps, cols), lambda i, j: (0, i * nj + j, 0)))
        args.append(w)
        out_shape.append(jax.ShapeDtypeStruct((rows, cols), BF16))
        out_specs.append(pl.BlockSpec((rows // steps, cols), lambda i, j: (i * nj + j, 0)))
    return pl.pallas_call(
        functools.partial(_attn_kernel, has_ctx=has_ctx, n_seqs=n_seqs, n_side=len(side_weights)),
        out_shape=out_shape,
        grid=(b // n_seqs, t // tq),
        in_specs=in_specs,
        out_specs=out_specs,
        scratch_shapes=[pltpu.VMEM((2, t + (s if has_ctx else 0), Q_TILE), F32),
                        pltpu.VMEM((2, t + (s if has_ctx else 0), Q_TILE), BF16)],
        name="mla_attn_lat" if has_ctx else "mla_attn_ctx",
        compiler_params=pltpu.CompilerParams(dimension_semantics=("arbitrary", "arbitrary"),
                                             vmem_limit_bytes=VMEM_LIMIT),
    )(*args)


def _gla_kernel(*refs, n_tiles, n_seqs, zero_init):
    gq_ref, gk_ref, gv_ref, gf_ref, gb_ref, go_ref = refs[:6]
    if zero_init:
        gn_ref, o_ref, sf_ref, sb_ref, oacc_ref, bdqk_ref, tri_ref, hm_ref = refs[6:]
    else:
        (sf0_ref, sb0_ref, gn_ref, o_ref, sf_ref, sb_ref, oacc_ref,
         bdqk_ref, tri_ref, hm_ref) = refs[6:]
    g_refs = (gf_ref, gb_ref)
    state_refs = (sf_ref, sb_ref)

    @pl.when(pl.program_id(0) == 0)
    def _():
        ri = lax.broadcasted_iota(jnp.int32, (GLA_TILE, GLA_TILE), 0)
        ci = lax.broadcasted_iota(jnp.int32, (GLA_TILE, GLA_TILE), 1)
        same_chunk = (ri // CHUNK) == (ci // CHUNK)
        bdqk_ref[...] = jnp.where(same_chunk, 1.0, 0.0).astype(BF16)
        tri_ref[0] = jnp.where(same_chunk & (ri >= ci), 1.0, 0.0)
        tri_ref[1] = jnp.where(same_chunk & (ci >= ri), 1.0, 0.0)
        hm_ref[...] = jnp.where(
            lax.broadcasted_iota(jnp.int32, (GLA_HEADS * GLA_TILE, GQK), 0) // GLA_TILE
            == lax.broadcasted_iota(jnp.int32, (GLA_HEADS * GLA_TILE, GQK), 1) // GLA_DK,
            1.0, 0.0).astype(BF16)

    row8 = lax.broadcasted_iota(jnp.int32, (8, GQK), 0)

    def tile_rows(t):
        return pl.ds(pl.multiple_of(t * GLA_TILE, GLA_TILE), GLA_TILE)

    def total_row(c, d):
        return c * CHUNK + (CHUNK - 1 if d == 0 else 0)

    def tile_dir(b, t, d):
        rows = tile_rows(t)
        g = g_refs[d][b, rows, :]
        g_hi = g.astype(BF16)
        g_lo = (g - g_hi.astype(F32)).astype(BF16)
        tri_b = tri_ref[d].astype(BF16)
        cum = (jnp.dot(tri_b, g_hi, preferred_element_type=F32)
               + jnp.dot(tri_b, g_lo, preferred_element_type=F32))
        yield
        totals = [cum[total_row(c, d):total_row(c, d) + 1, :] for c in range(CHUNKS_PER_TILE)]
        tot8 = jnp.zeros((8, GQK), F32)
        for c in range(CHUNKS_PER_TILE):
            tot8 = jnp.where(row8 == c, totals[c], tot8)
        dec_t = jnp.concatenate([jnp.exp(tot8), jnp.zeros((LANES - 8, GQK), F32)], axis=0).T
        q = gq_ref[b, rows, :] * (GLA_DK ** -0.5)
        k = gk_ref[b, rows, :]
        v = gv_ref[b, rows, :]
        tot = jnp.concatenate([jnp.broadcast_to(tc, (CHUNK, GQK)) for tc in totals], axis=0)
        qe = (q * jnp.exp(cum)).astype(BF16)
        ke = (k * jnp.exp(-cum)).astype(BF16)
        kd_t = (k * jnp.exp(tot - cum)).T.astype(BF16)
        bd_qk = bdqk_ref[...] > 0
        tri = tri_ref[d] > 0

        qm = jnp.where(hm_ref[...] > 0, jnp.tile(qe, (GLA_HEADS, 1)), 0.0)
        yield
        att = lax.dot_general(qm, ke, _NT, preferred_element_type=F32)

        yield
        intra, upd = [], []
        for hd in range(GLA_HEADS):
            vh = v[:, hd * GLA_DV:(hd + 1) * GLA_DV]
            a_h = jnp.where(tri, att[hd * GLA_TILE:(hd + 1) * GLA_TILE, :], 0.0).astype(BF16)
            intra.append(jnp.dot(a_h, vh, preferred_element_type=F32))
            kd_h = jnp.tile(kd_t[hd * GLA_DK:(hd + 1) * GLA_DK, :], (CHUNKS_PER_TILE, 1))
            upd.append(jnp.dot(jnp.where(bd_qk, kd_h, 0.0), vh, preferred_element_type=F32))

        yield
        state = [state_refs[d][b, hd] for hd in range(GLA_HEADS)]
        order = range(CHUNKS_PER_TILE) if d == 0 else range(CHUNKS_PER_TILE - 1, -1, -1)
        seen = {}
        for c in order:
            seen[c] = jnp.concatenate(state, axis=0).astype(BF16)
            decay = jnp.broadcast_to(dec_t[:, c:c + 1], (GQK, GLA_DV))
            for hd in range(GLA_HEADS):
                ks = slice(hd * GLA_DK, (hd + 1) * GLA_DK)
                state[hd] = decay[ks, :] * state[hd] + upd[hd][c * CHUNK:(c + 1) * CHUNK, :]
        for hd in range(GLA_HEADS):
            state_refs[d][b, hd] = state[hd]

        yield
        for c in range(CHUNKS_PER_TILE):
            cr = slice(c * CHUNK, (c + 1) * CHUNK)
            q_c = jnp.concatenate([qm[hd * GLA_TILE + c * CHUNK:hd * GLA_TILE + (c + 1) * CHUNK, :]
                                   for hd in range(GLA_HEADS)], axis=0)
            inter = jnp.dot(q_c, seen[c], preferred_element_type=F32)
            o = jnp.concatenate([intra[hd][cr, :] + inter[hd * CHUNK:(hd + 1) * CHUNK, :]
                                 for hd in range(GLA_HEADS)], axis=1)
            oacc_ref[d, b, pl.ds(pl.multiple_of(t * GLA_TILE + c * CHUNK, CHUNK), CHUNK), :] = o

    if zero_init:
        sf_ref[...] = jnp.zeros(sf_ref.shape, F32)
        sb_ref[...] = jnp.zeros(sb_ref.shape, F32)
    else:
        sf_ref[...] = sf0_ref[...]
        sb_ref[...] = sb0_ref[...]

    tiles_per_step = 2 if n_tiles % 2 == 0 else 1

    def main_body(i, carry):
        chains = []
        for u in range(tiles_per_step):
            t = i * tiles_per_step + u
            for b in range(n_seqs):
                chains += [tile_dir(b, t, 0), tile_dir(b, n_tiles - 1 - t, 1)]
        _interleave(chains)
        return carry

    lax.fori_loop(0, n_tiles // tiles_per_step, main_body, 0)

    gn = gn_ref[...]

    def epilogue_body(t, carry):
        rows = tile_rows(t)
        for b in range(n_seqs):
            for hd in range(GLA_HEADS):
                vs = slice(hd * GLA_DV, (hd + 1) * GLA_DV)
                o = _rms(oacc_ref[0, b, rows, vs] + oacc_ref[1, b, rows, vs], gn)
                go = go_ref[b, rows, vs]
                o_ref[b, rows, vs] = (o * (go * _sigmoid(go))).astype(BF16)
        return carry

    lax.fori_loop(0, n_tiles, epilogue_body, 0)


def _gla(gq, gk, gv, gf, gb, go, init_states, gn, n_seqs):
    b, t, _ = gq.shape
    n_tiles = t // GLA_TILE
    zero_init = init_states is None
    seq = lambda c: pl.BlockSpec((n_seqs, t, c), lambda i: (i, 0, 0))
    st = pl.BlockSpec((n_seqs, GLA_HEADS, GLA_DK, GLA_DV), lambda i: (i, 0, 0, 0))
    in_specs = [seq(GQK), seq(GQK), seq(GV), seq(GQK), seq(GQK), seq(GV)]
    args = [gq, gk, gv, gf, gb, go]
    if not zero_init:
        in_specs += [st, st]
        args += list(init_states)
    in_specs.append(_const_spec(gn.shape))
    args.append(gn)
    return pl.pallas_call(
        functools.partial(_gla_kernel, n_tiles=n_tiles, n_seqs=n_seqs, zero_init=zero_init),
        out_shape=[jax.ShapeDtypeStruct((b, t, GV), BF16),
                   jax.ShapeDtypeStruct((b, GLA_HEADS, GLA_DK, GLA_DV), F32),
                   jax.ShapeDtypeStruct((b, GLA_HEADS, GLA_DK, GLA_DV), F32)],
        grid=(b // n_seqs,),
        in_specs=in_specs,
        out_specs=[seq(GV), st, st],
        scratch_shapes=[pltpu.VMEM((2, n_seqs, t, GV), F32),
                        pltpu.VMEM((GQK, GLA_TILE), BF16),
                        pltpu.VMEM((2, GLA_TILE, GLA_TILE), F32),
                        pltpu.VMEM((GLA_HEADS * GLA_TILE, GQK), BF16)],
        name="gla_%d" % t,
        compiler_params=pltpu.CompilerParams(dimension_semantics=("arbitrary",),
                                             vmem_limit_bytes=VMEM_LIMIT),
    )(*args)


def _ffn_kernel(xp_ref, xs_ref, atp_ref, ats_ref, glp_ref, gls_ref, mod_ref, wout_ref, nf_ref,
                wfi_ref, wfo_ref, fn_ref, yp_ref, ys_ref, act_ref, *, ctx_tiles, tiles_per_seq):
    def tile(x_ref, at_ref, gl_ref, y_ref, mod_row):
        _, _, gt1, sh2, sc2, gt2 = _mod_rows(mod_ref, mod_row)
        mix = (jnp.dot(at_ref[...], wout_ref[0:VALL, :], preferred_element_type=F32)
               + jnp.dot(gl_ref[...], wout_ref[VALL:, :], preferred_element_type=F32))
        x1 = x_ref[...] + gt1 * mix
        h2 = (_rms(x1, nf_ref[...]) * (1.0 + sc2) + sh2).astype(BF16)
        for j in range(N_FF_CHUNKS):
            cs = slice(j * FF_CHUNK, (j + 1) * FF_CHUNK)
            a = jnp.dot(h2, wfi_ref[:, cs], preferred_element_type=F32)
            g = jnp.dot(h2, wfi_ref[:, D_FF + j * FF_CHUNK:D_FF + (j + 1) * FF_CHUNK],
                        preferred_element_type=F32)
            act_ref[:, cs] = (a * _sigmoid(a) * g).astype(BF16)
        ff = jnp.dot(act_ref[...], wfo_ref[...], preferred_element_type=F32)
        x2 = x1 + gt2 * ff
        y_ref[...] = _rms(x2, fn_ref[...])

    t = pl.program_id(0)

    @pl.when(t < ctx_tiles)
    def _():
        tile(xp_ref, atp_ref, glp_ref, yp_ref, 0)

    @pl.when(t >= ctx_tiles)
    def _():
        tile(xs_ref, ats_ref, gls_ref, ys_ref, 1 + (t - ctx_tiles) // tiles_per_seq)


def _ffn(xp, xs, attn_p, attn_s, gla_p, gla_s, mod, wout, nf, wfi, wfo, fn, tm, tiles_per_seq):
    d = xp.shape[1]
    ctx_tiles = xp.shape[0] // tm
    lat_tiles = xs.shape[0] // tm
    ctx_map = lambda s: (jnp.minimum(s, ctx_tiles - 1), 0)
    lat_map = lambda s: (jnp.maximum(s - ctx_tiles, 0), 0)
    tile = lambda c, m: pl.BlockSpec((tm, c), m)
    return pl.pallas_call(
        functools.partial(_ffn_kernel, ctx_tiles=ctx_tiles, tiles_per_seq=tiles_per_seq),
        out_shape=[jax.ShapeDtypeStruct(xp.shape, F32), jax.ShapeDtypeStruct(xs.shape, F32)],
        grid=(ctx_tiles + lat_tiles,),
        in_specs=[tile(d, ctx_map), tile(d, lat_map), tile(VALL, ctx_map), tile(VALL, lat_map),
                  tile(GV, ctx_map), tile(GV, lat_map), _const_spec(mod.shape),
                  _const_spec(wout.shape), _const_spec(nf.shape), _const_spec(wfi.shape),
                  _const_spec(wfo.shape), _const_spec(fn.shape)],
        out_specs=[tile(d, ctx_map), tile(d, lat_map)],
        scratch_shapes=[pltpu.VMEM((tm, D_FF), BF16)],
        name="out_ffn",
        compiler_params=pltpu.CompilerParams(dimension_semantics=("arbitrary",),
                                             vmem_limit_bytes=VMEM_LIMIT),
    )(xp, xs, attn_p, attn_s, gla_p, gla_s, mod, wout, nf, wfi, wfo, fn)


def _rope_tables(n_tokens):
    t = np.arange(n_tokens)
    row = (t // GRID_W).astype(np.float32)
    col = (t % GRID_W).astype(np.float32)
    half = MLA_ROPE // 2
    inv = (np.float32(ROPE_BASE) ** (-np.arange(0, half, 2, dtype=np.float32) / np.float32(half))).astype(np.float32)
    ang_r = row[:, None] * inv
    ang_c = col[:, None] * inv
    ang = np.concatenate([ang_r, ang_r, ang_c, ang_c], axis=-1).astype(np.float32)
    cos, sin = np.cos(ang), np.sin(ang)
    first = (np.arange(MLA_ROPE) % half) < (half // 2)
    cos_t = np.ones((n_tokens, LANES), np.float32)
    sa_t = np.zeros((n_tokens, LANES), np.float32)
    sb_t = np.zeros((n_tokens, LANES), np.float32)
    cos_t[:, ROPE_LANE0:ROPE_LANE0 + MLA_ROPE] = cos
    sa_t[:, ROPE_LANE0:ROPE_LANE0 + MLA_ROPE] = np.where(first, -sin, 0.0)
    sb_t[:, ROPE_LANE0:ROPE_LANE0 + MLA_ROPE] = np.where(first, 0.0, sin)
    return jnp.asarray(cos_t), jnp.asarray(sa_t), jnp.asarray(sb_t)


def kernel(x_prompt, x_sample, cache_kv_latent, cache_k_rope, state_gla_fwd, state_gla_bwd, c, c_ctx, w_ada, b_ada, norm_attn, w_in, mla_q_norm, w_uq, mla_kv_norm, w_ukv, w_gate_f, b_gate_f, w_gate_b, b_gate_b, gla_norm, w_out, norm_ffn, w_ffn_in, w_ffn_out, final_norm):
    batch, seq, d = x_prompt.shape
    dec_batch, dec_seq, _ = x_sample.shape
    assert w_ada.shape[0] == 1 and w_in.shape[-1] == W_COLS and w_ffn_in.shape[-1] == 2 * D_FF
    l = 0

    cond8 = jnp.concatenate([c_ctx[None, :], c, jnp.zeros((8 - 1 - dec_batch, d), F32)], axis=0)
    mod = _ada(cond8, w_ada[l], b_ada[l])

    win, wuq, wk, wvt, wg = _prep_in_weights(w_in, w_uq, w_ukv, w_gate_f, w_gate_b)
    bg = jnp.concatenate([b_gate_f[l], b_gate_b[l]]).reshape(1, 2 * GQK)
    in_w = (norm_attn[l].reshape(1, d), win, mla_q_norm[l].reshape(1, Q_LORA), wuq,
            mla_kv_norm[l].reshape(1, KV_LORA), wk, wvt, wg, bg)
    gn = gla_norm[l].reshape(1, GLA_DV)
    tm, tm_ffn = 512, 512
    r3 = lambda a, b_, t: a.reshape(b_, t, a.shape[-1])

    xp = x_prompt.reshape(batch * seq, d)
    (q, k, vt, gq, gk, gv, gf, gb, go, ckv, kr) = _inproj(xp, mod, lambda i: 0, in_w, None, tm, seq // tm)
    (attn_p,) = _attention(r3(q, batch, seq), r3(k, batch, seq), vt, None, seq, ATTN_CTX_SEQS)
    gla_p, sf, sb = _gla(r3(gq, batch, seq), r3(gk, batch, seq), r3(gv, batch, seq), r3(gf, batch, seq),
                         r3(gb, batch, seq), r3(go, batch, seq), None, gn, GLA_CTX_SEQS)

    xs = x_sample.reshape(dec_batch * dec_seq, d)
    tiles = dec_seq // tm
    (q, k, vt, gq, gk, gv, gf, gb, go) = _inproj(xs, mod, lambda i: 1 + i // tiles, in_w,
                                                  _rope_tables(dec_seq), tm, tiles)
    kr_pad = jnp.pad(cache_k_rope[:, l], ((0, 0), (0, 0), (ROPE_LANE0, LANES - ROPE_LANE0 - MLA_ROPE)))
    kc, vct = _decomp(cache_kv_latent[:, l], kr_pad, wk, wvt)
    attn_s, wout, wfi, wfo = _attention(r3(q, dec_batch, dec_seq), r3(k, dec_batch, dec_seq), vt, (kc, vct),
                                        ATTN_LAT_QUERIES, 1, (w_out, w_ffn_in, w_ffn_out))
    gla_s, _, _ = _gla(r3(gq, dec_batch, dec_seq), r3(gk, dec_batch, dec_seq), r3(gv, dec_batch, dec_seq),
                       r3(gf, dec_batch, dec_seq), r3(gb, dec_batch, dec_seq), r3(go, dec_batch, dec_seq),
                       (state_gla_fwd[:, l].astype(F32), state_gla_bwd[:, l].astype(F32)), gn, 1)

    flat = lambda a: a.reshape(-1, a.shape[-1])
    y_prompt, y_sample = _ffn(xp, xs, flat(attn_p), flat(attn_s), flat(gla_p), flat(gla_s), mod,
                              wout, norm_ffn[l].reshape(1, d), wfi, wfo,
                              final_norm.reshape(1, d), tm_ffn, dec_seq // tm_ffn)
    y_prompt = y_prompt.reshape(batch, seq, d)
    y_sample = y_sample.reshape(dec_batch, dec_seq, d)

    new_kv_latent = ckv.reshape(batch, 1, seq, KV_LORA)
    new_k_rope = kr.reshape(batch, 1, seq, MLA_ROPE)
    new_state_fwd = sf.reshape(batch, 1, GLA_HEADS, GLA_DK, GLA_DV).astype(x_prompt.dtype)
    new_state_bwd = sb.reshape(batch, 1, GLA_HEADS, GLA_DK, GLA_DV).astype(x_prompt.dtype)
    return (y_prompt, y_sample, new_kv_latent, new_k_rope, new_state_fwd, new_state_bwd)
```

```python
import functools

import numpy as np
import jax
import jax.numpy as jnp
from jax import lax
from jax.experimental import pallas as pl
from jax.experimental.pallas import tpu as pltpu

F32 = jnp.float32
BF16 = jnp.bfloat16

GRID_W = 64
MLA_HEADS = 8
MLA_NOPE = 64
MLA_ROPE = 32
MLA_QK = MLA_NOPE + MLA_ROPE
MLA_V = 64
Q_LORA = 384
KV_LORA = 256
GLA_HEADS = 4
GLA_DK = 64
GLA_DV = 128
GATE_RANK = 16
GATE_NORM = 16.0
CHUNK = 64
D_FF = 2816
ROPE_BASE = 10000.0
EPS = 1e-6
LOG2_E = 1.4426950408889634

LANES = 128
HEAD_PAD = LANES
ROPE_LANE0 = MLA_NOPE
GQK = GLA_HEADS * GLA_DK
GV = GLA_HEADS * GLA_DV
QPAD = MLA_HEADS * HEAD_PAD
VALL = MLA_HEADS * MLA_V
ONES_ROWS = 16
KEY_BLOCK = 1024

W_KR = Q_LORA + KV_LORA
W_GQ = W_KR + MLA_ROPE
W_GF = W_GQ + 2 * GQK + GV
W_GO = W_GF + 2 * GATE_RANK
W_COLS = W_GO + GV

Z_Q = 0
Z_KV = Z_Q + Q_LORA
Z_GQ = Z_KV + KV_LORA
Z_GK = Z_GQ + GQK
Z_GV = Z_GK + GQK
Z_GO = Z_GV + GV
Z_MISC = Z_GO + GV
Z_COLS = Z_MISC + LANES

FF_CHUNK = 256
N_FF_CHUNKS = D_FF // FF_CHUNK

GLA_TILE = 256
CHUNKS_PER_TILE = GLA_TILE // CHUNK
ADA_ROWS = 128
INPROJ_SUB = 512
Q_TILE = 256
ATTN_LAT_QUERIES = 512
ATTN_CTX_SEQS = 4
GLA_CTX_SEQS = 4

VMEM_LIMIT = 56 * 1024 * 1024

_NT = (((1,), (1,)), ((), ()))


def _rms(x, w):
    return x * lax.rsqrt(jnp.mean(x * x, axis=-1, keepdims=True) + EPS) * w


def _sigmoid(x):
    return 1.0 / (1.0 + jnp.exp(-x))


def _log_sigmoid(x):
    return jnp.minimum(x, 0.0) - jnp.log1p(jnp.exp(-jnp.abs(x)))


def _interleave(chains):
    pending, active = list(chains), []
    while pending or active:
        if pending:
            active.append(pending.pop(0))
        for chain in list(active):
            try:
                next(chain)
            except StopIteration:
                active.remove(chain)


def _const_spec(shape):
    nd = len(shape)
    return pl.BlockSpec(shape, lambda *_: (0,) * nd, pipeline_mode=pl.Buffered(1))


def _mod_rows(mod_ref, r):
    return [mod_ref[k, pl.ds(r, 1), :] for k in range(mod_ref.shape[0])]


def _ada_accumulate(step, cond_ref, w_ref, b_ref, o_ref, first):
    d = o_ref.shape[2]
    c = cond_ref[...]
    s = (c * _sigmoid(c)).astype(BF16)
    part = jnp.dot(s, w_ref[...].astype(BF16), preferred_element_type=F32)
    for j in range(w_ref.shape[1] // d):
        sl = slice(j * d, (j + 1) * d)

        @pl.when(step == 0)
        def _():
            o_ref[first + j] = part[:, sl] + b_ref[:, sl]

        @pl.when(step > 0)
        def _():
            o_ref[first + j] += part[:, sl]


def _ada_kernel(cond_ref, w_ref, b_ref, o_ref):
    _ada_accumulate(pl.program_id(0), cond_ref, w_ref, b_ref, o_ref, 0)


def _ada_cond_chunks(cond8, steps):
    return cond8.reshape(8, steps, cond8.shape[1] // steps).transpose(1, 0, 2)


def _ada(cond8, w_ada, b_ada, n_chunks):
    d = w_ada.shape[0]
    steps = d // ADA_ROWS
    return pl.pallas_call(
        _ada_kernel,
        out_shape=jax.ShapeDtypeStruct((n_chunks, 8, d), F32),
        grid=(steps,),
        in_specs=[pl.BlockSpec((None, 8, ADA_ROWS), lambda k: (k, 0, 0)),
                  pl.BlockSpec((ADA_ROWS, n_chunks * d), lambda k: (k, 0)),
                  pl.BlockSpec((1, n_chunks * d), lambda k: (0, 0))],
        out_specs=pl.BlockSpec((n_chunks, 8, d), lambda k: (0, 0, 0)),
        name="ada_mod",
        compiler_params=pltpu.CompilerParams(dimension_semantics=("arbitrary",)),
    )(_ada_cond_chunks(cond8, steps), w_ada, b_ada.reshape(1, -1))


def _prep_kernel(wint_ref, wuq_ref, wukv_ref, wgf_ref, wgb_ref, win_o, wuq_o, wk_o, wvt_o, wg_o):
    cols = wint_ref.shape[1]
    for dst, src, n in ((Z_Q, 0, W_KR), (Z_GQ, W_GQ, W_GF - W_GQ), (Z_GO, W_GO, GV)):
        win_o[:, dst:dst + n] = wint_ref[src:src + n, :].T.astype(BF16)
    z32 = jnp.zeros((32, cols), F32)
    misc_t = jnp.concatenate([wint_ref[W_GF:W_GO, :], z32, wint_ref[W_KR:W_GQ, :], z32], axis=0)
    win_o[:, Z_MISC:Z_COLS] = misc_t.T.astype(BF16)

    u = wuq_ref[...]
    zq = jnp.zeros((u.shape[0], HEAD_PAD - MLA_QK), F32)
    for hd in range(MLA_HEADS):
        blk = jnp.concatenate([u[:, hd * MLA_QK:(hd + 1) * MLA_QK], zq], axis=1)
        wuq_o[:, hd * HEAD_PAD:(hd + 1) * HEAD_PAD] = blk.astype(BF16)

    @pl.when(pl.program_id(0) == 0)
    def _():
        kv = wukv_ref[...]
        per = MLA_NOPE + MLA_V
        lane = lax.broadcasted_iota(jnp.int32, (kv.shape[0], per), 1)
        for hd in range(MLA_HEADS):
            blk = kv[:, hd * per:(hd + 1) * per]
            wk_o[:, hd * HEAD_PAD:(hd + 1) * HEAD_PAD] = jnp.where(lane < MLA_NOPE, blk, 0.0).astype(BF16)
        wv = jnp.concatenate([kv[:, hd * per + MLA_NOPE:(hd + 1) * per] for hd in range(MLA_HEADS)], axis=1)
        wvt_o[...] = wv.T.astype(BF16)

        wg_o[...] = jnp.zeros(wg_o.shape, BF16)
        wg_o[0:GATE_RANK, 0:GQK] = wgf_ref[...].astype(BF16)
        wg_o[GATE_RANK:2 * GATE_RANK, GQK:2 * GQK] = wgb_ref[...].astype(BF16)


def _prep_in_weights(w_in, w_uq, w_ukv, w_gate_f, w_gate_b):
    d = w_in.shape[1]
    steps = 4
    w_in_t = jnp.swapaxes(w_in, 1, 2)
    rb3 = lambda r, c: pl.BlockSpec((None, r // steps, c), lambda i: (0, i, 0))
    rb = lambda r, c: pl.BlockSpec((r // steps, c), lambda i: (i, 0))
    full3 = lambda shape: pl.BlockSpec((None,) + tuple(shape[1:]), lambda i: (0, 0, 0))
    full = lambda shape: pl.BlockSpec(shape, lambda i: (0, 0))
    return pl.pallas_call(
        _prep_kernel,
        out_shape=[jax.ShapeDtypeStruct((d, Z_COLS), BF16),
                   jax.ShapeDtypeStruct((Q_LORA, QPAD), BF16),
                   jax.ShapeDtypeStruct((KV_LORA, QPAD), BF16),
                   jax.ShapeDtypeStruct((VALL, KV_LORA), BF16),
                   jax.ShapeDtypeStruct((LANES, 2 * GQK), BF16)],
        grid=(steps,),
        in_specs=[pl.BlockSpec((None, W_COLS, d // steps), lambda i: (0, 0, i)),
                  rb3(Q_LORA, MLA_HEADS * MLA_QK), full3(w_ukv.shape),
                  full3(w_gate_f.shape), full3(w_gate_b.shape)],
        out_specs=[rb(d, Z_COLS), rb(Q_LORA, QPAD), full((KV_LORA, QPAD)), full((VALL, KV_LORA)),
                   full((LANES, 2 * GQK))],
        name="weight_prep",
        compiler_params=pltpu.CompilerParams(dimension_semantics=("arbitrary",)),
    )(w_in_t, w_uq, w_ukv, w_gate_f, w_gate_b)


def _inproj_kernel(*refs, latent, mod_row):
    (x_ref, mod_ref, nw_ref, win_ref, qn_ref, wuq_ref, kvn_ref, wk_ref, wvt_ref, wg_ref, bg_ref) = refs[:11]
    if latent:
        cos_ref, sa_ref, sb_ref = refs[11:14]
        outs = refs[14:]
    else:
        outs = refs[11:]
    q_ref, k_ref, vt_ref, gq_ref, gk_ref, gv_ref, gf_ref, gb_ref, go_ref = outs[:9]

    sh1, sc1 = _mod_rows(mod_ref, mod_row(pl.program_id(0)))
    scale = MLA_QK ** -0.5 * LOG2_E
    lane = lax.broadcasted_iota(jnp.int32, (INPROJ_SUB, LANES), 1)
    in_rope = (lane >= ROPE_LANE0) & (lane < ROPE_LANE0 + MLA_ROPE)

    def sub_tile(r0):
        rows = slice(r0, r0 + INPROJ_SUB)
        h = (_rms(x_ref[rows, :], nw_ref[...]) * (1.0 + sc1) + sh1).astype(BF16)
        yield
        z_all = jnp.dot(h, win_ref[...], preferred_element_type=F32)
        z = lambda lo, n: z_all[:, lo:lo + n]
        yield
        qn = _rms(z(Z_Q, Q_LORA), qn_ref[...]).astype(BF16)
        ckv = _rms(z(Z_KV, KV_LORA), kvn_ref[...])
        ckv_b = ckv.astype(BF16)
        misc = z(Z_MISC, LANES)
        yield
        q = jnp.dot(qn, wuq_ref[...], preferred_element_type=F32)
        kn = jnp.dot(ckv_b, wk_ref[...], preferred_element_type=F32)
        vt_ref[:, rows] = lax.dot_general(wvt_ref[...], ckv_b, _NT,
                                          preferred_element_type=F32).astype(BF16)
        gpre = jnp.dot(misc.astype(BF16), wg_ref[...], preferred_element_type=F32) + bg_ref[...]
        yield
        if latent:
            cos, sa, sb = cos_ref[rows, :], sa_ref[rows, :], sb_ref[rows, :]

            def rope(t):
                return t * cos + pltpu.roll(t, LANES - 8, 1) * sa + pltpu.roll(t, 8, 1) * sb
        else:
            def rope(t):
                return t

        krope = rope(misc)
        for hd in range(MLA_HEADS):
            sl = slice(hd * HEAD_PAD, (hd + 1) * HEAD_PAD)
            q_ref[rows, sl] = (rope(q[:, sl]) * scale).astype(BF16)
            k_ref[rows, sl] = jnp.where(in_rope, krope, kn[:, sl]).astype(BF16)
        gq_ref[rows, :] = z(Z_GQ, GQK)
        gk_ref[rows, :] = z(Z_GK, GQK)
        gv_ref[rows, :] = z(Z_GV, GV).astype(BF16)
        go_ref[rows, :] = z(Z_GO, GV)
        gate = _log_sigmoid(gpre) * (1.0 / GATE_NORM)
        gf_ref[rows, :] = gate[:, :GQK]
        gb_ref[rows, :] = gate[:, GQK:]
        if not latent:
            ckv_ref, kr_ref = outs[9:]
            ckv_ref[rows, :] = ckv
            kr_ref[rows, :] = misc[:, ROPE_LANE0:ROPE_LANE0 + MLA_ROPE]

    _interleave([sub_tile(r0) for r0 in range(0, x_ref.shape[0], INPROJ_SUB)])


def _inproj(x2d, mod, mod_row, weights, rope_tabs, tm, tiles_per_seq):
    n_tok, d = x2d.shape
    latent = rope_tabs is not None
    nw, win, qn, wuq, kvn, wk, wvt, wg, bg = weights
    row = lambda i: (i, 0)
    in_specs = [pl.BlockSpec((tm, d), row), _const_spec(mod.shape),
                _const_spec(nw.shape), _const_spec(win.shape), _const_spec(qn.shape),
                _const_spec(wuq.shape), _const_spec(kvn.shape), _const_spec(wk.shape),
                _const_spec(wvt.shape), _const_spec(wg.shape), _const_spec(bg.shape)]
    args = [x2d, mod, nw, win, qn, wuq, kvn, wk, wvt, wg, bg]
    if latent:
        tab = pl.BlockSpec((tm, LANES), lambda i: (i % tiles_per_seq, 0))
        in_specs += [tab, tab, tab]
        args += list(rope_tabs)
    out_cols = [(QPAD, BF16), (QPAD, BF16), None, (GQK, F32), (GQK, F32), (GV, BF16),
                (GQK, F32), (GQK, F32), (GV, F32)]
    if not latent:
        out_cols += [(KV_LORA, F32), (MLA_ROPE, F32)]
    out_shape = [jax.ShapeDtypeStruct((n_tok, oc[0]), oc[1]) if oc else
                 jax.ShapeDtypeStruct((VALL, n_tok), BF16) for oc in out_cols]
    out_specs = [pl.BlockSpec((tm, oc[0]), row) if oc else
                 pl.BlockSpec((VALL, tm), lambda i: (0, i)) for oc in out_cols]
    return pl.pallas_call(
        functools.partial(_inproj_kernel, latent=latent, mod_row=mod_row),
        out_shape=out_shape,
        grid=(n_tok // tm,),
        in_specs=in_specs,
        out_specs=out_specs,
        name="inproj_lat" if latent else "inproj_ctx",
        compiler_params=pltpu.CompilerParams(dimension_semantics=("arbitrary",),
                                             vmem_limit_bytes=VMEM_LIMIT),
    )(*args)


def _decomp_kernel(ckv_ref, kr_ref, wk_ref, wvt_ref, k_ref, vt_ref):
    ckv_b = ckv_ref[...].astype(BF16)
    kn = jnp.dot(ckv_b, wk_ref[...], preferred_element_type=F32)
    kr = kr_ref[...]
    lane = lax.broadcasted_iota(jnp.int32, kr.shape, 1)
    in_rope = (lane >= ROPE_LANE0) & (lane < ROPE_LANE0 + MLA_ROPE)
    for hd in range(MLA_HEADS):
        sl = slice(hd * HEAD_PAD, (hd + 1) * HEAD_PAD)
        k_ref[:, sl] = jnp.where(in_rope, kr, kn[:, sl]).astype(BF16)
    vt_ref[...] = lax.dot_general(wvt_ref[...], ckv_b, _NT, preferred_element_type=F32).astype(BF16)


def _decomp(ckv, kr_pad, wk, wvt):
    b, s, _ = ckv.shape
    return pl.pallas_call(
        _decomp_kernel,
        out_shape=[jax.ShapeDtypeStruct((b, s, QPAD), BF16), jax.ShapeDtypeStruct((VALL, b * s), BF16)],
        grid=(b,),
        in_specs=[pl.BlockSpec((None, s, KV_LORA), lambda i: (i, 0, 0)),
                  pl.BlockSpec((None, s, LANES), lambda i: (i, 0, 0)),
                  _const_spec(wk.shape), _const_spec(wvt.shape)],
        out_specs=[pl.BlockSpec((None, s, QPAD), lambda i: (i, 0, 0)),
                   pl.BlockSpec((VALL, s), lambda i: (0, i))],
        name="ctx_decompress",
        compiler_params=pltpu.CompilerParams(dimension_semantics=("arbitrary",)),
    )(ckv, kr_pad, wk, wvt)


def _attn_kernel(*refs, has_ctx, n_seqs, n_side, ada):
    n_in = 5 if has_ctx else 3
    n_ada = 5 if ada else 0
    side_in, ada_in = refs[n_in:n_in + n_side], refs[n_in + n_side:n_in + n_side + n_ada]
    refs = refs[:n_in] + refs[n_in + n_side + n_ada:]
    side_out, ada_out = refs[n_in + 1:n_in + 1 + n_side], refs[n_in + 1 + n_side:n_in + 1 + n_side + (1 if ada else 0)]
    refs = refs[:n_in + 1] + refs[n_in + 1 + n_side + (1 if ada else 0):]
    if has_ctx:
        q_ref, kc_ref, vct_ref, k_ref, vt_ref, o_ref, st_ref, p_ref = refs
    else:
        q_ref, k_ref, vt_ref, o_ref, st_ref, p_ref = refs
    tq = Q_TILE

    for src, dst in zip(side_in, side_out):
        dst[...] = src[...].astype(BF16)
    if ada:
        step = pl.program_id(0) * pl.num_programs(1) + pl.program_id(1)
        cond_ref, wa_ref, wb_ref, ba_ref, bb_ref = ada_in
        _ada_accumulate(step, cond_ref, wa_ref, ba_ref, ada_out[0], 0)
        _ada_accumulate(step, cond_ref, wb_ref, bb_ref, ada_out[0], 2)

    def key_blocks(bi):
        srcs = [(kc_ref, vct_ref)] if has_ctx else []
        blocks, row0 = [], 0
        for kr, vr in srcs + [(k_ref, vt_ref)]:
            n_keys = kr.shape[1]
            size = min(KEY_BLOCK, n_keys)
            for r in range(0, n_keys, size):
                blocks.append((kr, vr, r, bi * n_keys + r, size, row0))
                row0 += size
        return blocks

    units = [(bi, slice(q0, q0 + tq), hd) for bi in range(n_seqs)
             for q0 in range(0, q_ref.shape[1], tq) for hd in range(MLA_HEADS)]
    col_max = [None] * len(units)
    pair = []
    for stage in range(len(units) + 2):
        ua, ub, uc = stage, stage - 1, stage - 2
        run_max = None
        acc = jnp.zeros((MLA_V + ONES_ROWS, tq), F32)
        for j in range(len(key_blocks(0))):
            if ua < len(units):
                bi, qrows, hd = units[ua]
                kr, _, r0, _, size, srow = key_blocks(bi)[j]
                sl = slice(hd * HEAD_PAD, (hd + 1) * HEAD_PAD)
                st = lax.dot_general(kr[bi, r0:r0 + size, sl], q_ref[bi, qrows, sl], _NT,
                                     preferred_element_type=F32)
                st_ref[ua % 2, srow:srow + size, :] = st
                blk_max = jnp.max(st.reshape(size // 8, 8, tq), axis=0)
                run_max = blk_max if run_max is None else jnp.maximum(run_max, blk_max)
            if 0 <= ub < len(units):
                _, _, _, _, size, srow = key_blocks(0)[j]
                p_ref[ub % 2, srow:srow + size, :] = jnp.exp2(
                    st_ref[ub % 2, srow:srow + size, :] - col_max[ub]).astype(BF16)
            if uc >= 0:
                bi, _, hd = units[uc]
                _, vr, _, c0, size, srow = key_blocks(bi)[j]
                v_aug = jnp.concatenate([vr[hd * MLA_V:(hd + 1) * MLA_V, c0:c0 + size],
                                         jnp.ones((ONES_ROWS, size), BF16)], axis=0)
                acc = acc + jnp.dot(v_aug, p_ref[uc % 2, srow:srow + size, :],
                                    preferred_element_type=F32)
        if ua < len(units):
            col_max[ua] = jnp.max(run_max, axis=0, keepdims=True)
        if uc >= 0:
            bi, qrows, hd = units[uc]
            pair.append(acc[:MLA_V, :] / acc[MLA_V:MLA_V + 1, :])
            if len(pair) == 2:
                o_ref[bi, qrows, (hd - 1) * MLA_V:(hd + 1) * MLA_V] = (
                    jnp.concatenate(pair, axis=0).T.astype(BF16))
                pair = []


def _attention(q, k, vt, ctx_kv, tq, n_seqs, side_weights=(), ada_args=None):
    b, t, _ = q.shape
    steps = (b // n_seqs) * (t // tq)
    assert (n_seqs == 1 or tq == t) and tq % Q_TILE == 0
    has_ctx = ctx_kv is not None
    in_specs = [pl.BlockSpec((n_seqs, tq, QPAD), lambda i, j: (i, j, 0))]
    args = [q]
    if has_ctx:
        kc, vct = ctx_kv
        s = kc.shape[1]
        in_specs += [pl.BlockSpec((n_seqs, s, QPAD), lambda i, j: (i, 0, 0)),
                     pl.BlockSpec((VALL, n_seqs * s), lambda i, j: (0, i))]
        args += [kc, vct]
    in_specs += [pl.BlockSpec((n_seqs, t, QPAD), lambda i, j: (i, 0, 0)),
                 pl.BlockSpec((VALL, n_seqs * t), lambda i, j: (0, i))]
    args += [k, vt]
    out_shape = [jax.ShapeDtypeStruct((b, t, VALL), BF16)]
    out_specs = [pl.BlockSpec((n_seqs, tq, VALL), lambda i, j: (i, j, 0))]
    nj = t // tq
    for w in side_weights:
        _, rows, cols = w.shape
        assert rows % (16 * steps) == 0
        in_specs.append(pl.BlockSpec((None, rows // steps, cols), lambda i, j: (0, i * nj + j, 0)))
        args.append(w)
        out_shape.append(jax.ShapeDtypeStruct((rows, cols), BF16))
        out_specs.append(pl.BlockSpec((rows // steps, cols), lambda i, j: (i * nj + j, 0)))
    if ada_args is not None:
        cond8, w_ada, b_ada = ada_args
        d = w_ada.shape[1]
        step_of = lambda i, j: i * nj + j
        in_specs += [pl.BlockSpec((None, 8, d // steps), lambda i, j: (step_of(i, j), 0, 0)),
                     pl.BlockSpec((None, d // steps, 2 * d), lambda i, j: (0, step_of(i, j), 1)),
                     pl.BlockSpec((None, d // steps, 2 * d), lambda i, j: (0, step_of(i, j), 2)),
                     pl.BlockSpec((1, 2 * d), lambda i, j: (0, 1)),
                     pl.BlockSpec((1, 2 * d), lambda i, j: (0, 2))]
        args += [_ada_cond_chunks(cond8, steps), w_ada, w_ada, b_ada, b_ada]
        out_shape.append(jax.ShapeDtypeStruct((4, 8, d), F32))
        out_specs.append(pl.BlockSpec((4, 8, d), lambda i, j: (0, 0, 0)))
    return pl.pallas_call(
        functools.partial(_attn_kernel, has_ctx=has_ctx, n_seqs=n_seqs, n_side=len(side_weights),
                          ada=ada_args is not None),
        out_shape=out_shape,
        grid=(b // n_seqs, t // tq),
        in_specs=in_specs,
        out_specs=out_specs,
        scratch_shapes=[pltpu.VMEM((2, t + (s if has_ctx else 0), Q_TILE), F32),
                        pltpu.VMEM((2, t + (s if has_ctx else 0), Q_TILE), BF16)],
        name="mla_attn_lat" if has_ctx else "mla_attn_ctx",
        compiler_params=pltpu.CompilerParams(dimension_semantics=("arbitrary", "arbitrary"),
                                             vmem_limit_bytes=VMEM_LIMIT),
    )(*args)


def _gla_kernel(*refs, n_tiles, n_seqs, zero_init):
    gq_ref, gk_ref, gv_ref, gf_ref, gb_ref, go_ref = refs[:6]
    if zero_init:
        gn_ref, o_ref, sf_ref, sb_ref, oacc_ref, bdqk_ref, tri_ref, hm_ref = refs[6:]
    else:
        (sf0_ref, sb0_ref, gn_ref, o_ref, sf_ref, sb_ref, oacc_ref,
         bdqk_ref, tri_ref, hm_ref) = refs[6:]
    g_refs = (gf_ref, gb_ref)
    state_refs = (sf_ref, sb_ref)

    @pl.when(pl.program_id(0) == 0)
    def _():
        ri = lax.broadcasted_iota(jnp.int32, (GLA_TILE, GLA_TILE), 0)
        ci = lax.broadcasted_iota(jnp.int32, (GLA_TILE, GLA_TILE), 1)
        same_chunk = (ri // CHUNK) == (ci // CHUNK)
        bdqk_ref[...] = jnp.where(same_chunk, 1.0, 0.0).astype(BF16)
        tri_ref[0] = jnp.where(same_chunk & (ri >= ci), 1.0, 0.0)
        tri_ref[1] = jnp.where(same_chunk & (ci >= ri), 1.0, 0.0)
        hm_ref[...] = jnp.where(
            lax.broadcasted_iota(jnp.int32, (GLA_HEADS * GLA_TILE, GQK), 0) // GLA_TILE
            == lax.broadcasted_iota(jnp.int32, (GLA_HEADS * GLA_TILE, GQK), 1) // GLA_DK,
            1.0, 0.0).astype(BF16)

    row8 = lax.broadcasted_iota(jnp.int32, (8, GQK), 0)

    def tile_rows(t):
        return pl.ds(pl.multiple_of(t * GLA_TILE, GLA_TILE), GLA_TILE)

    def total_row(c, d):
        return c * CHUNK + (CHUNK - 1 if d == 0 else 0)

    def tile_dir(b, t, d):
        rows = tile_rows(t)
        g = g_refs[d][b, rows, :]
        g_hi = g.astype(BF16)
        g_lo = (g - g_hi.astype(F32)).astype(BF16)
        tri_b = tri_ref[d].astype(BF16)
        cum = (jnp.dot(tri_b, g_hi, preferred_element_type=F32)
               + jnp.dot(tri_b, g_lo, preferred_element_type=F32))
        yield
        totals = [cum[total_row(c, d):total_row(c, d) + 1, :] for c in range(CHUNKS_PER_TILE)]
        tot8 = jnp.zeros((8, GQK), F32)
        for c in range(CHUNKS_PER_TILE):
            tot8 = jnp.where(row8 == c, totals[c], tot8)
        dec_t = jnp.concatenate([jnp.exp(tot8), jnp.zeros((LANES - 8, GQK), F32)], axis=0).T
        q = gq_ref[b, rows, :] * (GLA_DK ** -0.5)
        k = gk_ref[b, rows, :]
        v = gv_ref[b, rows, :]
        tot = jnp.concatenate([jnp.broadcast_to(tc, (CHUNK, GQK)) for tc in totals], axis=0)
        qe = (q * jnp.exp(cum)).astype(BF16)
        ke = (k * jnp.exp(-cum)).astype(BF16)
        kd_t = (k * jnp.exp(tot - cum)).T.astype(BF16)
        bd_qk = bdqk_ref[...] > 0
        tri = tri_ref[d] > 0

        qm = jnp.where(hm_ref[...] > 0, jnp.tile(qe, (GLA_HEADS, 1)), 0.0)
        yield
        att = lax.dot_general(qm, ke, _NT, preferred_element_type=F32)

        yield
        intra, upd = [], []
        for hd in range(GLA_HEADS):
            vh = v[:, hd * GLA_DV:(hd + 1) * GLA_DV]
            a_h = jnp.where(tri, att[hd * GLA_TILE:(hd + 1) * GLA_TILE, :], 0.0).astype(BF16)
            intra.append(jnp.dot(a_h, vh, preferred_element_type=F32))
            kd_h = jnp.tile(kd_t[hd * GLA_DK:(hd + 1) * GLA_DK, :], (CHUNKS_PER_TILE, 1))
            upd.append(jnp.dot(jnp.where(bd_qk, kd_h, 0.0), vh, preferred_element_type=F32))

        yield
        state = [state_refs[d][b, hd] for hd in range(GLA_HEADS)]
        order = range(CHUNKS_PER_TILE) if d == 0 else range(CHUNKS_PER_TILE - 1, -1, -1)
        seen = {}
        for c in order:
            seen[c] = jnp.concatenate(state, axis=0).astype(BF16)
            decay = jnp.broadcast_to(dec_t[:, c:c + 1], (GQK, GLA_DV))
            for hd in range(GLA_HEADS):
                ks = slice(hd * GLA_DK, (hd + 1) * GLA_DK)
                state[hd] = decay[ks, :] * state[hd] + upd[hd][c * CHUNK:(c + 1) * CHUNK, :]
        for hd in range(GLA_HEADS):
            state_refs[d][b, hd] = state[hd]

        yield
        for c in range(CHUNKS_PER_TILE):
            cr = slice(c * CHUNK, (c + 1) * CHUNK)
            q_c = jnp.concatenate([qm[hd * GLA_TILE + c * CHUNK:hd * GLA_TILE + (c + 1) * CHUNK, :]
                                   for hd in range(GLA_HEADS)], axis=0)
            inter = jnp.dot(q_c, seen[c], preferred_element_type=F32)
            o = jnp.concatenate([intra[hd][cr, :] + inter[hd * CHUNK:(hd + 1) * CHUNK, :]
                                 for hd in range(GLA_HEADS)], axis=1)
            oacc_ref[d, b, pl.ds(pl.multiple_of(t * GLA_TILE + c * CHUNK, CHUNK), CHUNK), :] = o

    if zero_init:
        sf_ref[...] = jnp.zeros(sf_ref.shape, F32)
        sb_ref[...] = jnp.zeros(sb_ref.shape, F32)
    else:
        sf_ref[...] = sf0_ref[...]
        sb_ref[...] = sb0_ref[...]

    tiles_per_step = 2 if n_tiles % 2 == 0 else 1

    def main_body(i, carry):
        chains = []
        for u in range(tiles_per_step):
            t = i * tiles_per_step + u
            for b in range(n_seqs):
                chains += [tile_dir(b, t, 0), tile_dir(b, n_tiles - 1 - t, 1)]
        _interleave(chains)
        return carry

    lax.fori_loop(0, n_tiles // tiles_per_step, main_body, 0)

    gn = gn_ref[...]

    def epilogue_body(t, carry):
        rows = tile_rows(t)
        for b in range(n_seqs):
            for hd in range(GLA_HEADS):
                vs = slice(hd * GLA_DV, (hd + 1) * GLA_DV)
                o = _rms(oacc_ref[0, b, rows, vs] + oacc_ref[1, b, rows, vs], gn)
                go = go_ref[b, rows, vs]
                o_ref[b, rows, vs] = (o * (go * _sigmoid(go))).astype(BF16)
        return carry

    lax.fori_loop(0, n_tiles, epilogue_body, 0)


def _gla(gq, gk, gv, gf, gb, go, init_states, gn, n_seqs):
    b, t, _ = gq.shape
    n_tiles = t // GLA_TILE
    zero_init = init_states is None
    seq = lambda c: pl.BlockSpec((n_seqs, t, c), lambda i: (i, 0, 0))
    st = pl.BlockSpec((n_seqs, GLA_HEADS, GLA_DK, GLA_DV), lambda i: (i, 0, 0, 0))
    in_specs = [seq(GQK), seq(GQK), seq(GV), seq(GQK), seq(GQK), seq(GV)]
    args = [gq, gk, gv, gf, gb, go]
    if not zero_init:
        in_specs += [st, st]
        args += list(init_states)
    in_specs.append(_const_spec(gn.shape))
    args.append(gn)
    return pl.pallas_call(
        functools.partial(_gla_kernel, n_tiles=n_tiles, n_seqs=n_seqs, zero_init=zero_init),
        out_shape=[jax.ShapeDtypeStruct((b, t, GV), BF16),
                   jax.ShapeDtypeStruct((b, GLA_HEADS, GLA_DK, GLA_DV), F32),
                   jax.ShapeDtypeStruct((b, GLA_HEADS, GLA_DK, GLA_DV), F32)],
        grid=(b // n_seqs,),
        in_specs=in_specs,
        out_specs=[seq(GV), st, st],
        scratch_shapes=[pltpu.VMEM((2, n_seqs, t, GV), F32),
                        pltpu.VMEM((GQK, GLA_TILE), BF16),
                        pltpu.VMEM((2, GLA_TILE, GLA_TILE), F32),
                        pltpu.VMEM((GLA_HEADS * GLA_TILE, GQK), BF16)],
        name="gla_%d" % t,
        compiler_params=pltpu.CompilerParams(dimension_semantics=("arbitrary",),
                                             vmem_limit_bytes=VMEM_LIMIT),
    )(*args)


def _ffn_kernel(xp_ref, xs_ref, atp_ref, ats_ref, glp_ref, gls_ref, mod_ref, wout_ref, nf_ref,
                wfi_ref, wfo_ref, fn_ref, yp_ref, ys_ref, act_ref, *, ctx_tiles, tiles_per_seq):
    def tile(x_ref, at_ref, gl_ref, y_ref, mod_row):
        gt1, sh2, sc2, gt2 = _mod_rows(mod_ref, mod_row)
        mix = (jnp.dot(at_ref[...], wout_ref[0:VALL, :], preferred_element_type=F32)
               + jnp.dot(gl_ref[...], wout_ref[VALL:, :], preferred_element_type=F32))
        x1 = x_ref[...] + gt1 * mix
        h2 = (_rms(x1, nf_ref[...]) * (1.0 + sc2) + sh2).astype(BF16)
        for j in range(N_FF_CHUNKS):
            cs = slice(j * FF_CHUNK, (j + 1) * FF_CHUNK)
            a = jnp.dot(h2, wfi_ref[:, cs], preferred_element_type=F32)
            g = jnp.dot(h2, wfi_ref[:, D_FF + j * FF_CHUNK:D_FF + (j + 1) * FF_CHUNK],
                        preferred_element_type=F32)
            act_ref[:, cs] = (a * _sigmoid(a) * g).astype(BF16)
        ff = jnp.dot(act_ref[...], wfo_ref[...], preferred_element_type=F32)
        x2 = x1 + gt2 * ff
        y_ref[...] = _rms(x2, fn_ref[...])

    t = pl.program_id(0)

    @pl.when(t < ctx_tiles)
    def _():
        tile(xp_ref, atp_ref, glp_ref, yp_ref, 0)

    @pl.when(t >= ctx_tiles)
    def _():
        tile(xs_ref, ats_ref, gls_ref, ys_ref, 1 + (t - ctx_tiles) // tiles_per_seq)


def _ffn(xp, xs, attn_p, attn_s, gla_p, gla_s, mod, wout, nf, wfi, wfo, fn, tm, tiles_per_seq):
    d = xp.shape[1]
    ctx_tiles = xp.shape[0] // tm
    lat_tiles = xs.shape[0] // tm
    ctx_map = lambda s: (jnp.minimum(s, ctx_tiles - 1), 0)
    lat_map = lambda s: (jnp.maximum(s - ctx_tiles, 0), 0)
    tile = lambda c, m: pl.BlockSpec((tm, c), m)
    return pl.pallas_call(
        functools.partial(_ffn_kernel, ctx_tiles=ctx_tiles, tiles_per_seq=tiles_per_seq),
        out_shape=[jax.ShapeDtypeStruct(xp.shape, F32), jax.ShapeDtypeStruct(xs.shape, F32)],
        grid=(ctx_tiles + lat_tiles,),
        in_specs=[tile(d, ctx_map), tile(d, lat_map), tile(VALL, ctx_map), tile(VALL, lat_map),
                  tile(GV, ctx_map), tile(GV, lat_map), _const_spec(mod.shape),
                  _const_spec(wout.shape), _const_spec(nf.shape), _const_spec(wfi.shape),
                  _const_spec(wfo.shape), _const_spec(fn.shape)],
        out_specs=[tile(d, ctx_map), tile(d, lat_map)],
        scratch_shapes=[pltpu.VMEM((tm, D_FF), BF16)],
        name="out_ffn",
        compiler_params=pltpu.CompilerParams(dimension_semantics=("arbitrary",),
                                             vmem_limit_bytes=VMEM_LIMIT),
    )(xp, xs, attn_p, attn_s, gla_p, gla_s, mod, wout, nf, wfi, wfo, fn)


def _rope_tables(n_tokens):
    t = np.arange(n_tokens)
    row = (t // GRID_W).astype(np.float32)
    col = (t % GRID_W).astype(np.float32)
    half = MLA_ROPE // 2
    inv = (np.float32(ROPE_BASE) ** (-np.arange(0, half, 2, dtype=np.float32) / np.float32(half))).astype(np.float32)
    ang_r = row[:, None] * inv
    ang_c = col[:, None] * inv
    ang = np.concatenate([ang_r, ang_r, ang_c, ang_c], axis=-1).astype(np.float32)
    cos, sin = np.cos(ang), np.sin(ang)
    first = (np.arange(MLA_ROPE) % half) < (half // 2)
    cos_t = np.ones((n_tokens, LANES), np.float32)
    sa_t = np.zeros((n_tokens, LANES), np.float32)
    sb_t = np.zeros((n_tokens, LANES), np.float32)
    cos_t[:, ROPE_LANE0:ROPE_LANE0 + MLA_ROPE] = cos
    sa_t[:, ROPE_LANE0:ROPE_LANE0 + MLA_ROPE] = np.where(first, -sin, 0.0)
    sb_t[:, ROPE_LANE0:ROPE_LANE0 + MLA_ROPE] = np.where(first, 0.0, sin)
    return jnp.asarray(cos_t), jnp.asarray(sa_t), jnp.asarray(sb_t)


def kernel(x_prompt, x_sample, cache_kv_latent, cache_k_rope, state_gla_fwd, state_gla_bwd, c, c_ctx, w_ada, b_ada, norm_attn, w_in, mla_q_norm, w_uq, mla_kv_norm, w_ukv, w_gate_f, b_gate_f, w_gate_b, b_gate_b, gla_norm, w_out, norm_ffn, w_ffn_in, w_ffn_out, final_norm):
    batch, seq, d = x_prompt.shape
    dec_batch, dec_seq, _ = x_sample.shape
    assert w_ada.shape[0] == 1 and w_in.shape[-1] == W_COLS and w_ffn_in.shape[-1] == 2 * D_FF
    l = 0

    cond8 = jnp.concatenate([c_ctx[None, :], c, jnp.zeros((8 - 1 - dec_batch, d), F32)], axis=0)
    mod_in = _ada(cond8, w_ada[l], b_ada[l], 2)

    win, wuq, wk, wvt, wg = _prep_in_weights(w_in, w_uq, w_ukv, w_gate_f, w_gate_b)
    bg = jnp.concatenate([b_gate_f[l], b_gate_b[l]]).reshape(1, 2 * GQK)
    in_w = (norm_attn[l].reshape(1, d), win, mla_q_norm[l].reshape(1, Q_LORA), wuq,
            mla_kv_norm[l].reshape(1, KV_LORA), wk, wvt, wg, bg)
    gn = gla_norm[l].reshape(1, GLA_DV)
    tm, tm_ffn = 512, 512
    r3 = lambda a, b_, t: a.reshape(b_, t, a.shape[-1])

    xp = x_prompt.reshape(batch * seq, d)
    (q, k, vt, gq, gk, gv, gf, gb, go, ckv, kr) = _inproj(xp, mod_in, lambda i: 0, in_w, None, tm, seq // tm)
    attn_p, mod_ffn = _attention(r3(q, batch, seq), r3(k, batch, seq), vt, None, seq, ATTN_CTX_SEQS,
                                 ada_args=(cond8, w_ada, b_ada))
    gla_p, sf, sb = _gla(r3(gq, batch, seq), r3(gk, batch, seq), r3(gv, batch, seq), r3(gf, batch, seq),
                         r3(gb, batch, seq), r3(go, batch, seq), None, gn, GLA_CTX_SEQS)

    xs = x_sample.reshape(dec_batch * dec_seq, d)
    tiles = dec_seq // tm
    (q, k, vt, gq, gk, gv, gf, gb, go) = _inproj(xs, mod_in, lambda i: 1 + i // tiles, in_w,
                                                  _rope_tables(dec_seq), tm, tiles)
    kr_pad = jnp.pad(cache_k_rope[:, l], ((0, 0), (0, 0), (ROPE_LANE0, LANES - ROPE_LANE0 - MLA_ROPE)))
    kc, vct = _decomp(cache_kv_latent[:, l], kr_pad, wk, wvt)
    attn_s, wout, wfi, wfo = _attention(r3(q, dec_batch, dec_seq), r3(k, dec_batch, dec_seq), vt, (kc, vct),
                                        ATTN_LAT_QUERIES, 1, (w_out, w_ffn_in, w_ffn_out))
    gla_s, _, _ = _gla(r3(gq, dec_batch, dec_seq), r3(gk, dec_batch, dec_seq), r3(gv, dec_batch, dec_seq),
                       r3(gf, dec_batch, dec_seq), r3(gb, dec_batch, dec_seq), r3(go, dec_batch, dec_seq),
                       (state_gla_fwd[:, l].astype(F32), state_gla_bwd[:, l].astype(F32)), gn, 1)

    flat = lambda a: a.reshape(-1, a.shape[-1])
    y_prompt, y_sample = _ffn(xp, xs, flat(attn_p), flat(attn_s), flat(gla_p), flat(gla_s), mod_ffn,
                              wout, norm_ffn[l].reshape(1, d), wfi, wfo,
                              final_norm.reshape(1, d), tm_ffn, dec_seq // tm_ffn)
    y_prompt = y_prompt.reshape(batch, seq, d)
    y_sample = y_sample.reshape(dec_batch, dec_seq, d)

    new_kv_latent = ckv.reshape(batch, 1, seq, KV_LORA)
    new_k_rope = kr.reshape(batch, 1, seq, MLA_ROPE)
    new_state_fwd = sf.reshape(batch, 1, GLA_HEADS, GLA_DK, GLA_DV).astype(x_prompt.dtype)
    new_state_bwd = sb.reshape(batch, 1, GLA_HEADS, GLA_DK, GLA_DV).astype(x_prompt.dtype)
    return (y_prompt, y_sample, new_kv_latent, new_k_rope, new_state_fwd, new_state_bwd)
```

```python
import functools

import numpy as np
import jax
import jax.numpy as jnp
from jax import lax
from jax.experimental import pallas as pl
from jax.experimental.pallas import tpu as pltpu

F32 = jnp.float32
BF16 = jnp.bfloat16

GRID_W = 64
MLA_HEADS = 8
MLA_NOPE = 64
MLA_ROPE = 32
MLA_QK = MLA_NOPE + MLA_ROPE
MLA_V = 64
Q_LORA = 384
KV_LORA = 256
GLA_HEADS = 4
GLA_DK = 64
GLA_DV = 128
GATE_RANK = 16
GATE_NORM = 16.0
CHUNK = 64
D_FF = 2816
ROPE_BASE = 10000.0
EPS = 1e-6
LOG2_E = 1.4426950408889634

LANES = 128
HEAD_PAD = LANES
ROPE_LANE0 = MLA_NOPE
GQK = GLA_HEADS * GLA_DK
GV = GLA_HEADS * GLA_DV
QPAD = MLA_HEADS * HEAD_PAD
VALL = MLA_HEADS * MLA_V
ONES_ROWS = 16
KEY_BLOCK = 1024

W_KR = Q_LORA + KV_LORA
W_GQ = W_KR + MLA_ROPE
W_GF = W_GQ + 2 * GQK + GV
W_GO = W_GF + 2 * GATE_RANK
W_COLS = W_GO + GV

Z_Q = 0
Z_KV = Z_Q + Q_LORA
Z_GQ = Z_KV + KV_LORA
Z_GK = Z_GQ + GQK
Z_GV = Z_GK + GQK
Z_GO = Z_GV + GV
Z_MISC = Z_GO + GV
Z_COLS = Z_MISC + LANES

FF_CHUNK = 256
N_FF_CHUNKS = D_FF // FF_CHUNK

GLA_TILE = 256
CHUNKS_PER_TILE = GLA_TILE // CHUNK
ADA_ROWS = 128
INPROJ_SUB = 512
Q_TILE = 256
ATTN_LAT_QUERIES = 512
ATTN_CTX_SEQS = 4
GLA_CTX_SEQS = 4

VMEM_LIMIT = 56 * 1024 * 1024

_NT = (((1,), (1,)), ((), ()))


def _rms(x, w):
    return x * lax.rsqrt(jnp.mean(x * x, axis=-1, keepdims=True) + EPS) * w


def _sigmoid(x):
    return 1.0 / (1.0 + jnp.exp(-x))


def _log_sigmoid(x):
    return jnp.minimum(x, 0.0) - jnp.log1p(jnp.exp(-jnp.abs(x)))


def _interleave(chains):
    pending, active = list(chains), []
    while pending or active:
        if pending:
            active.append(pending.pop(0))
        for chain in list(active):
            try:
                next(chain)
            except StopIteration:
                active.remove(chain)


def _const_spec(shape):
    nd = len(shape)
    return pl.BlockSpec(shape, lambda *_: (0,) * nd, pipeline_mode=pl.Buffered(1))


def _mod_rows(mod_ref, r):
    return [mod_ref[k, pl.ds(r, 1), :] for k in range(mod_ref.shape[0])]


def _ada_kernel(cctx_ref, c_ref, w_ref, b_ref, o_ref):
    k = pl.program_id(0)
    d = o_ref.shape[2]
    row = lax.broadcasted_iota(jnp.int32, (8, cctx_ref.shape[1]), 0)
    cond = jnp.where(row == 0, cctx_ref[...], 0.0)
    for r in range(c_ref.shape[0]):
        cond = jnp.where(row == 1 + r, c_ref[r:r + 1, :], cond)
    s = (cond * _sigmoid(cond)).astype(BF16)
    part = jnp.dot(s, w_ref[...].astype(BF16), preferred_element_type=F32)
    for j in range(o_ref.shape[0]):
        sl = slice(j * d, (j + 1) * d)

        @pl.when(k == 0)
        def _():
            o_ref[j] = part[:, sl] + b_ref[:, sl]

        @pl.when(k > 0)
        def _():
            o_ref[j] += part[:, sl]


def _ada(c_ctx, c, w_ada, b_ada):
    d = w_ada.shape[0]
    n = w_ada.shape[1]
    assert 1 + c.shape[0] <= 8
    return pl.pallas_call(
        _ada_kernel,
        out_shape=jax.ShapeDtypeStruct((n // d, 8, d), F32),
        grid=(d // ADA_ROWS,),
        in_specs=[pl.BlockSpec((1, ADA_ROWS), lambda k: (0, k)),
                  pl.BlockSpec((c.shape[0], ADA_ROWS), lambda k: (0, k)),
                  pl.BlockSpec((ADA_ROWS, n), lambda k: (k, 0)),
                  pl.BlockSpec((1, n), lambda k: (0, 0))],
        out_specs=pl.BlockSpec((n // d, 8, d), lambda k: (0, 0, 0)),
        name="ada_mod",
        compiler_params=pltpu.CompilerParams(dimension_semantics=("arbitrary",)),
    )(c_ctx.reshape(1, d), c, w_ada, b_ada.reshape(1, n))


def _prep_kernel(wint_ref, wuq_ref, wukv_ref, wgf_ref, wgb_ref, win_o, wuq_o, wk_o, wvt_o, wg_o):
    cols = wint_ref.shape[1]
    for dst, src, n in ((Z_Q, 0, W_KR), (Z_GQ, W_GQ, W_GF - W_GQ), (Z_GO, W_GO, GV)):
        win_o[:, dst:dst + n] = wint_ref[src:src + n, :].T.astype(BF16)
    z32 = jnp.zeros((32, cols), F32)
    misc_t = jnp.concatenate([wint_ref[W_GF:W_GO, :], z32, wint_ref[W_KR:W_GQ, :], z32], axis=0)
    win_o[:, Z_MISC:Z_COLS] = misc_t.T.astype(BF16)

    u = wuq_ref[...]
    zq = jnp.zeros((u.shape[0], HEAD_PAD - MLA_QK), F32)
    for hd in range(MLA_HEADS):
        blk = jnp.concatenate([u[:, hd * MLA_QK:(hd + 1) * MLA_QK], zq], axis=1)
        wuq_o[:, hd * HEAD_PAD:(hd + 1) * HEAD_PAD] = blk.astype(BF16)

    @pl.when(pl.program_id(0) == 0)
    def _():
        kv = wukv_ref[...]
        per = MLA_NOPE + MLA_V
        lane = lax.broadcasted_iota(jnp.int32, (kv.shape[0], per), 1)
        for hd in range(MLA_HEADS):
            blk = kv[:, hd * per:(hd + 1) * per]
            wk_o[:, hd * HEAD_PAD:(hd + 1) * HEAD_PAD] = jnp.where(lane < MLA_NOPE, blk, 0.0).astype(BF16)
        wv = jnp.concatenate([kv[:, hd * per + MLA_NOPE:(hd + 1) * per] for hd in range(MLA_HEADS)], axis=1)
        wvt_o[...] = wv.T.astype(BF16)

        wg_o[...] = jnp.zeros(wg_o.shape, BF16)
        wg_o[0:GATE_RANK, 0:GQK] = wgf_ref[...].astype(BF16)
        wg_o[GATE_RANK:2 * GATE_RANK, GQK:2 * GQK] = wgb_ref[...].astype(BF16)


def _prep_in_weights(w_in, w_uq, w_ukv, w_gate_f, w_gate_b):
    d = w_in.shape[1]
    steps = 4
    w_in_t = jnp.swapaxes(w_in, 1, 2)
    rb3 = lambda r, c: pl.BlockSpec((None, r // steps, c), lambda i: (0, i, 0))
    rb = lambda r, c: pl.BlockSpec((r // steps, c), lambda i: (i, 0))
    full3 = lambda shape: pl.BlockSpec((None,) + tuple(shape[1:]), lambda i: (0, 0, 0))
    full = lambda shape: pl.BlockSpec(shape, lambda i: (0, 0))
    return pl.pallas_call(
        _prep_kernel,
        out_shape=[jax.ShapeDtypeStruct((d, Z_COLS), BF16),
                   jax.ShapeDtypeStruct((Q_LORA, QPAD), BF16),
                   jax.ShapeDtypeStruct((KV_LORA, QPAD), BF16),
                   jax.ShapeDtypeStruct((VALL, KV_LORA), BF16),
                   jax.ShapeDtypeStruct((LANES, 2 * GQK), BF16)],
        grid=(steps,),
        in_specs=[pl.BlockSpec((None, W_COLS, d // steps), lambda i: (0, 0, i)),
                  rb3(Q_LORA, MLA_HEADS * MLA_QK), full3(w_ukv.shape),
                  full3(w_gate_f.shape), full3(w_gate_b.shape)],
        out_specs=[rb(d, Z_COLS), rb(Q_LORA, QPAD), full((KV_LORA, QPAD)), full((VALL, KV_LORA)),
                   full((LANES, 2 * GQK))],
        name="weight_prep",
        compiler_params=pltpu.CompilerParams(dimension_semantics=("arbitrary",)),
    )(w_in_t, w_uq, w_ukv, w_gate_f, w_gate_b)


def _inproj_kernel(*refs, latent, mod_row):
    (x_ref, mod_ref, nw_ref, win_ref, qn_ref, wuq_ref, kvn_ref, wk_ref, wvt_ref, wg_ref, bgf_ref, bgb_ref) = refs[:12]
    if latent:
        cos_ref, sa_ref, sb_ref = refs[12:15]
        outs = refs[15:]
    else:
        outs = refs[12:]
    q_ref, k_ref, vt_ref, gq_ref, gk_ref, gv_ref, gf_ref, gb_ref, go_ref = outs[:9]

    sh1, sc1 = _mod_rows(mod_ref, mod_row(pl.program_id(0)))[:2]
    scale = MLA_QK ** -0.5 * LOG2_E
    lane = lax.broadcasted_iota(jnp.int32, (INPROJ_SUB, LANES), 1)
    in_rope = (lane >= ROPE_LANE0) & (lane < ROPE_LANE0 + MLA_ROPE)

    def sub_tile(r0):
        rows = slice(r0, r0 + INPROJ_SUB)
        h = (_rms(x_ref[rows, :], nw_ref[...]) * (1.0 + sc1) + sh1).astype(BF16)
        yield
        z_all = jnp.dot(h, win_ref[...], preferred_element_type=F32)
        z = lambda lo, n: z_all[:, lo:lo + n]
        yield
        qn = _rms(z(Z_Q, Q_LORA), qn_ref[...]).astype(BF16)
        ckv = _rms(z(Z_KV, KV_LORA), kvn_ref[...])
        ckv_b = ckv.astype(BF16)
        misc = z(Z_MISC, LANES)
        yield
        q = jnp.dot(qn, wuq_ref[...], preferred_element_type=F32)
        kn = jnp.dot(ckv_b, wk_ref[...], preferred_element_type=F32)
        vt_ref[:, rows] = lax.dot_general(wvt_ref[...], ckv_b, _NT,
                                          preferred_element_type=F32).astype(BF16)
        gpre = jnp.dot(misc.astype(BF16), wg_ref[...], preferred_element_type=F32)
        yield
        if latent:
            cos, sa, sb = cos_ref[rows, :], sa_ref[rows, :], sb_ref[rows, :]

            def rope(t):
                return t * cos + pltpu.roll(t, LANES - 8, 1) * sa + pltpu.roll(t, 8, 1) * sb
        else:
            def rope(t):
                return t

        krope = rope(misc)
        for hd in range(MLA_HEADS):
            sl = slice(hd * HEAD_PAD, (hd + 1) * HEAD_PAD)
            q_ref[rows, sl] = (rope(q[:, sl]) * scale).astype(BF16)
            k_ref[rows, sl] = jnp.where(in_rope, krope, kn[:, sl]).astype(BF16)
        gq_ref[rows, :] = z(Z_GQ, GQK)
        gk_ref[rows, :] = z(Z_GK, GQK)
        gv_ref[rows, :] = z(Z_GV, GV).astype(BF16)
        go_ref[rows, :] = z(Z_GO, GV)
        gf_ref[rows, :] = _log_sigmoid(gpre[:, :GQK] + bgf_ref[...]) * (1.0 / GATE_NORM)
        gb_ref[rows, :] = _log_sigmoid(gpre[:, GQK:] + bgb_ref[...]) * (1.0 / GATE_NORM)
        if not latent:
            ckv_ref, krt_ref = outs[9:]
            ckv_ref[rows, :] = ckv
            misc_t = misc.T
            n = krt_ref.shape[2]
            for b in range(INPROJ_SUB // n):
                krt_ref[r0 // n + b] = misc_t[ROPE_LANE0:ROPE_LANE0 + MLA_ROPE, b * n:(b + 1) * n]

    _interleave([sub_tile(r0) for r0 in range(0, x_ref.shape[0], INPROJ_SUB)])


def _inproj(x2d, mod, mod_row, weights, rope_tabs, tm, seq_len):
    n_tok, d = x2d.shape
    latent = rope_tabs is not None
    tiles_per_seq = max(seq_len // tm, 1)
    nw, win, qn, wuq, kvn, wk, wvt, wg, bgf, bgb = weights
    row = lambda i: (i, 0)
    in_specs = [pl.BlockSpec((tm, d), row), _const_spec(mod.shape),
                _const_spec(nw.shape), _const_spec(win.shape), _const_spec(qn.shape),
                _const_spec(wuq.shape), _const_spec(kvn.shape), _const_spec(wk.shape),
                _const_spec(wvt.shape), _const_spec(wg.shape), _const_spec(bgf.shape),
                _const_spec(bgb.shape)]
    args = [x2d, mod, nw, win, qn, wuq, kvn, wk, wvt, wg, bgf, bgb]
    if latent:
        tab = pl.BlockSpec((tm, LANES), lambda i: (i % tiles_per_seq, 0))
        in_specs += [tab, tab, tab]
        args += list(rope_tabs)
    out_cols = [(QPAD, BF16), (QPAD, BF16), None, (GQK, F32), (GQK, F32), (GV, BF16),
                (GQK, F32), (GQK, F32), (GV, F32)]
    if not latent:
        out_cols += [(KV_LORA, F32)]
    out_shape = [jax.ShapeDtypeStruct((n_tok, oc[0]), oc[1]) if oc else
                 jax.ShapeDtypeStruct((VALL, n_tok), BF16) for oc in out_cols]
    out_specs = [pl.BlockSpec((tm, oc[0]), row) if oc else
                 pl.BlockSpec((VALL, tm), lambda i: (0, i)) for oc in out_cols]
    if not latent:
        seq = seq_len
        out_shape.append(jax.ShapeDtypeStruct((n_tok // seq, MLA_ROPE, seq), F32))
        out_specs.append(pl.BlockSpec((tm // seq, MLA_ROPE, seq), lambda i: (i, 0, 0)))
    return pl.pallas_call(
        functools.partial(_inproj_kernel, latent=latent, mod_row=mod_row),
        out_shape=out_shape,
        grid=(n_tok // tm,),
        in_specs=in_specs,
        out_specs=out_specs,
        name="inproj_lat" if latent else "inproj_ctx",
        compiler_params=pltpu.CompilerParams(dimension_semantics=("arbitrary",),
                                             vmem_limit_bytes=VMEM_LIMIT),
    )(*args)


def _decomp_kernel(ckv_ref, krt_ref, wk_ref, wvt_ref, k_ref, vt_ref):
    ckv_b = ckv_ref[...].astype(BF16)
    kn = jnp.dot(ckv_b, wk_ref[...], preferred_element_type=F32)
    n_keys = krt_ref.shape[1]
    kr = jnp.concatenate([jnp.zeros((ROPE_LANE0, n_keys), F32), krt_ref[...],
                          jnp.zeros((LANES - ROPE_LANE0 - MLA_ROPE, n_keys), F32)], axis=0).T
    lane = lax.broadcasted_iota(jnp.int32, kr.shape, 1)
    in_rope = (lane >= ROPE_LANE0) & (lane < ROPE_LANE0 + MLA_ROPE)
    for hd in range(MLA_HEADS):
        sl = slice(hd * HEAD_PAD, (hd + 1) * HEAD_PAD)
        k_ref[:, sl] = jnp.where(in_rope, kr, kn[:, sl]).astype(BF16)
    vt_ref[...] = lax.dot_general(wvt_ref[...], ckv_b, _NT, preferred_element_type=F32).astype(BF16)


def _decomp(ckv, kr_t, wk, wvt):
    b, s, _ = ckv.shape
    return pl.pallas_call(
        _decomp_kernel,
        out_shape=[jax.ShapeDtypeStruct((b, s, QPAD), BF16), jax.ShapeDtypeStruct((VALL, b * s), BF16)],
        grid=(b,),
        in_specs=[pl.BlockSpec((None, s, KV_LORA), lambda i: (i, 0, 0)),
                  pl.BlockSpec((None, MLA_ROPE, s), lambda i: (i, 0, 0)),
                  _const_spec(wk.shape), _const_spec(wvt.shape)],
        out_specs=[pl.BlockSpec((None, s, QPAD), lambda i: (i, 0, 0)),
                   pl.BlockSpec((VALL, s), lambda i: (0, i))],
        name="ctx_decompress",
        compiler_params=pltpu.CompilerParams(dimension_semantics=("arbitrary",)),
    )(ckv, kr_t, wk, wvt)


def _attn_kernel(*refs, has_ctx, n_seqs, n_side):
    n_in = 5 if has_ctx else 3
    side_in, refs = refs[n_in:n_in + n_side], refs[:n_in] + refs[n_in + n_side:]
    side_out, refs = refs[n_in + 1:n_in + 1 + n_side], refs[:n_in + 1] + refs[n_in + 1 + n_side:]
    if has_ctx:
        q_ref, kc_ref, vct_ref, k_ref, vt_ref, o_ref, st_ref, p_ref = refs
    else:
        q_ref, k_ref, vt_ref, o_ref, st_ref, p_ref = refs
    tq = Q_TILE

    for src, dst in zip(side_in, side_out):
        dst[...] = src[...].astype(BF16)

    def key_blocks(bi):
        srcs = [(kc_ref, vct_ref)] if has_ctx else []
        blocks, row0 = [], 0
        for kr, vr in srcs + [(k_ref, vt_ref)]:
            n_keys = kr.shape[1]
            size = min(KEY_BLOCK, n_keys)
            for r in range(0, n_keys, size):
                blocks.append((kr, vr, r, bi * n_keys + r, size, row0))
                row0 += size
        return blocks

    units = [(bi, slice(q0, q0 + tq), hd) for bi in range(n_seqs)
             for q0 in range(0, q_ref.shape[1], tq) for hd in range(MLA_HEADS)]
    col_max = [None] * len(units)
    pair = []
    for stage in range(len(units) + 2):
        ua, ub, uc = stage, stage - 1, stage - 2
        run_max = None
        acc = jnp.zeros((MLA_V + ONES_ROWS, tq), F32)
        for j in range(len(key_blocks(0))):
            if ua < len(units):
                bi, qrows, hd = units[ua]
                kr, _, r0, _, size, srow = key_blocks(bi)[j]
                sl = slice(hd * HEAD_PAD, (hd + 1) * HEAD_PAD)
                st = lax.dot_general(kr[bi, r0:r0 + size, sl], q_ref[bi, qrows, sl], _NT,
                                     preferred_element_type=F32)
                st_ref[ua % 2, srow:srow + size, :] = st
                blk_max = jnp.max(st.reshape(size // 8, 8, tq), axis=0)
                run_max = blk_max if run_max is None else jnp.maximum(run_max, blk_max)
            if 0 <= ub < len(units):
                _, _, _, _, size, srow = key_blocks(0)[j]
                p_ref[ub % 2, srow:srow + size, :] = jnp.exp2(
                    st_ref[ub % 2, srow:srow + size, :] - col_max[ub]).astype(BF16)
            if uc >= 0:
                bi, _, hd = units[uc]
                _, vr, _, c0, size, srow = key_blocks(bi)[j]
                v_aug = jnp.concatenate([vr[hd * MLA_V:(hd + 1) * MLA_V, c0:c0 + size],
                                         jnp.ones((ONES_ROWS, size), BF16)], axis=0)
                acc = acc + jnp.dot(v_aug, p_ref[uc % 2, srow:srow + size, :],
                                    preferred_element_type=F32)
        if ua < len(units):
            col_max[ua] = jnp.max(run_max, axis=0, keepdims=True)
        if uc >= 0:
            bi, qrows, hd = units[uc]
            pair.append(acc[:MLA_V, :] / acc[MLA_V:MLA_V + 1, :])
            if len(pair) == 2:
                o_ref[bi, qrows, (hd - 1) * MLA_V:(hd + 1) * MLA_V] = (
                    jnp.concatenate(pair, axis=0).T.astype(BF16))
                pair = []


def _attention(q, k, vt, ctx_kv, tq, n_seqs, side_weights=()):
    b, t, _ = q.shape
    steps = (b // n_seqs) * (t // tq)
    assert (n_seqs == 1 or tq == t) and tq % Q_TILE == 0
    has_ctx = ctx_kv is not None
    in_specs = [pl.BlockSpec((n_seqs, tq, QPAD), lambda i, j: (i, j, 0))]
    args = [q]
    if has_ctx:
        kc, vct = ctx_kv
        s = kc.shape[1]
        in_specs += [pl.BlockSpec((n_seqs, s, QPAD), lambda i, j: (i, 0, 0)),
                     pl.BlockSpec((VALL, n_seqs * s), lambda i, j: (0, i))]
        args += [kc, vct]
    in_specs += [pl.BlockSpec((n_seqs, t, QPAD), lambda i, j: (i, 0, 0)),
                 pl.BlockSpec((VALL, n_seqs * t), lambda i, j: (0, i))]
    args += [k, vt]
    out_shape = [jax.ShapeDtypeStruct((b, t, VALL), BF16)]
    out_specs = [pl.BlockSpec((n_seqs, tq, VALL), lambda i, j: (i, j, 0))]
    nj = t // tq
    for w in side_weights:
        _, rows, cols = w.shape
        assert rows % (16 * steps) == 0
        in_specs.append(pl.BlockSpec((None, rows // steps, cols), lambda i, j: (0, i * nj + j, 0)))
        args.append(w)
        out_shape.append(jax.ShapeDtypeStruct((rows, cols), BF16))
        out_specs.append(pl.BlockSpec((rows // steps, cols), lambda i, j: (i * nj + j, 0)))
    return pl.pallas_call(
        functools.partial(_attn_kernel, has_ctx=has_ctx, n_seqs=n_seqs, n_side=len(side_weights)),
        out_shape=out_shape,
        grid=(b // n_seqs, t // tq),
        in_specs=in_specs,
        out_specs=out_specs,
        scratch_shapes=[pltpu.VMEM((2, t + (s if has_ctx else 0), Q_TILE), F32),
                        pltpu.VMEM((2, t + (s if has_ctx else 0), Q_TILE), BF16)],
        name="mla_attn_lat" if has_ctx else "mla_attn_ctx",
        compiler_params=pltpu.CompilerParams(dimension_semantics=("arbitrary", "arbitrary"),
                                             vmem_limit_bytes=VMEM_LIMIT),
    )(*args)


def _gla_kernel(*refs, n_tiles, n_seqs, zero_init):
    gq_ref, gk_ref, gv_ref, gf_ref, gb_ref, go_ref = refs[:6]
    if zero_init:
        gn_ref, o_ref, sf_ref, sb_ref, oacc_ref, bdqk_ref, tri_ref, hm_ref = refs[6:]
    else:
        (sf0_ref, sb0_ref, gn_ref, o_ref, sf_ref, sb_ref, oacc_ref,
         bdqk_ref, tri_ref, hm_ref) = refs[6:]
    g_refs = (gf_ref, gb_ref)
    state_refs = (sf_ref, sb_ref)

    @pl.when(pl.program_id(0) == 0)
    def _():
        ri = lax.broadcasted_iota(jnp.int32, (GLA_TILE, GLA_TILE), 0)
        ci = lax.broadcasted_iota(jnp.int32, (GLA_TILE, GLA_TILE), 1)
        same_chunk = (ri // CHUNK) == (ci // CHUNK)
        bdqk_ref[...] = jnp.where(same_chunk, 1.0, 0.0).astype(BF16)
        tri_ref[0] = jnp.where(same_chunk & (ri >= ci), 1.0, 0.0)
        tri_ref[1] = jnp.where(same_chunk & (ci >= ri), 1.0, 0.0)
        hm_ref[...] = jnp.where(
            lax.broadcasted_iota(jnp.int32, (GLA_HEADS * GLA_TILE, GQK), 0) // GLA_TILE
            == lax.broadcasted_iota(jnp.int32, (GLA_HEADS * GLA_TILE, GQK), 1) // GLA_DK,
            1.0, 0.0).astype(BF16)

    row8 = lax.broadcasted_iota(jnp.int32, (8, GQK), 0)

    def tile_rows(t):
        return pl.ds(pl.multiple_of(t * GLA_TILE, GLA_TILE), GLA_TILE)

    def total_row(c, d):
        return c * CHUNK + (CHUNK - 1 if d == 0 else 0)

    def tile_dir(b, t, d):
        rows = tile_rows(t)
        g = g_refs[d][b, rows, :]
        g_hi = g.astype(BF16)
        g_lo = (g - g_hi.astype(F32)).astype(BF16)
        tri_b = tri_ref[d].astype(BF16)
        cum = (jnp.dot(tri_b, g_hi, preferred_element_type=F32)
               + jnp.dot(tri_b, g_lo, preferred_element_type=F32))
        yield
        totals = [cum[total_row(c, d):total_row(c, d) + 1, :] for c in range(CHUNKS_PER_TILE)]
        tot8 = jnp.zeros((8, GQK), F32)
        for c in range(CHUNKS_PER_TILE):
            tot8 = jnp.where(row8 == c, totals[c], tot8)
        dec_t = jnp.concatenate([jnp.exp(tot8), jnp.zeros((LANES - 8, GQK), F32)], axis=0).T
        q = gq_ref[b, rows, :] * (GLA_DK ** -0.5)
        k = gk_ref[b, rows, :]
        v = gv_ref[b, rows, :]
        tot = jnp.concatenate([jnp.broadcast_to(tc, (CHUNK, GQK)) for tc in totals], axis=0)
        qe = (q * jnp.exp(cum)).astype(BF16)
        ke = (k * jnp.exp(-cum)).astype(BF16)
        kd_t = (k * jnp.exp(tot - cum)).T.astype(BF16)
        bd_qk = bdqk_ref[...] > 0
        tri = tri_ref[d] > 0

        qm = jnp.where(hm_ref[...] > 0, jnp.tile(qe, (GLA_HEADS, 1)), 0.0)
        yield
        att = lax.dot_general(qm, ke, _NT, preferred_element_type=F32)

        yield
        intra, upd = [], []
        for hd in range(GLA_HEADS):
            vh = v[:, hd * GLA_DV:(hd + 1) * GLA_DV]
            a_h = jnp.where(tri, att[hd * GLA_TILE:(hd + 1) * GLA_TILE, :], 0.0).astype(BF16)
            intra.append(jnp.dot(a_h, vh, preferred_element_type=F32))
            kd_h = jnp.tile(kd_t[hd * GLA_DK:(hd + 1) * GLA_DK, :], (CHUNKS_PER_TILE, 1))
            upd.append(jnp.dot(jnp.where(bd_qk, kd_h, 0.0), vh, preferred_element_type=F32))

        yield
        state = [state_refs[d][b, hd] for hd in range(GLA_HEADS)]
        order = range(CHUNKS_PER_TILE) if d == 0 else range(CHUNKS_PER_TILE - 1, -1, -1)
        seen = {}
        for c in order:
            seen[c] = jnp.concatenate(state, axis=0).astype(BF16)
            decay = jnp.broadcast_to(dec_t[:, c:c + 1], (GQK, GLA_DV))
            for hd in range(GLA_HEADS):
                ks = slice(hd * GLA_DK, (hd + 1) * GLA_DK)
                state[hd] = decay[ks, :] * state[hd] + upd[hd][c * CHUNK:(c + 1) * CHUNK, :]
        for hd in range(GLA_HEADS):
            state_refs[d][b, hd] = state[hd]

        yield
        for c in range(CHUNKS_PER_TILE):
            cr = slice(c * CHUNK, (c + 1) * CHUNK)
            q_c = jnp.concatenate([qm[hd * GLA_TILE + c * CHUNK:hd * GLA_TILE + (c + 1) * CHUNK, :]
                                   for hd in range(GLA_HEADS)], axis=0)
            inter = jnp.dot(q_c, seen[c], preferred_element_type=F32)
            o = jnp.concatenate([intra[hd][cr, :] + inter[hd * CHUNK:(hd + 1) * CHUNK, :]
                                 for hd in range(GLA_HEADS)], axis=1)
            oacc_ref[d, b, pl.ds(pl.multiple_of(t * GLA_TILE + c * CHUNK, CHUNK), CHUNK), :] = o

    if zero_init:
        sf_ref[...] = jnp.zeros(sf_ref.shape, F32)
        sb_ref[...] = jnp.zeros(sb_ref.shape, F32)
    else:
        sf_ref[...] = sf0_ref[...]
        sb_ref[...] = sb0_ref[...]

    tiles_per_step = 2 if n_tiles % 2 == 0 else 1

    def main_body(i, carry):
        chains = []
        for u in range(tiles_per_step):
            t = i * tiles_per_step + u
            for b in range(n_seqs):
                chains += [tile_dir(b, t, 0), tile_dir(b, n_tiles - 1 - t, 1)]
        _interleave(chains)
        return carry

    lax.fori_loop(0, n_tiles // tiles_per_step, main_body, 0)

    gn = gn_ref[...]

    def epilogue_body(t, carry):
        rows = tile_rows(t)
        for b in range(n_seqs):
            for hd in range(GLA_HEADS):
                vs = slice(hd * GLA_DV, (hd + 1) * GLA_DV)
                o = _rms(oacc_ref[0, b, rows, vs] + oacc_ref[1, b, rows, vs], gn)
                go = go_ref[b, rows, vs]
                o_ref[b, rows, vs] = (o * (go * _sigmoid(go))).astype(BF16)
        return carry

    lax.fori_loop(0, n_tiles, epilogue_body, 0)


def _gla(gq, gk, gv, gf, gb, go, init_states, gn, n_seqs):
    b, t, _ = gq.shape
    n_tiles = t // GLA_TILE
    zero_init = init_states is None
    seq = lambda c: pl.BlockSpec((n_seqs, t, c), lambda i: (i, 0, 0))
    st = pl.BlockSpec((n_seqs, GLA_HEADS, GLA_DK, GLA_DV), lambda i: (i, 0, 0, 0))
    in_specs = [seq(GQK), seq(GQK), seq(GV), seq(GQK), seq(GQK), seq(GV)]
    args = [gq, gk, gv, gf, gb, go]
    if not zero_init:
        in_specs += [st, st]
        args += list(init_states)
    in_specs.append(_const_spec(gn.shape))
    args.append(gn)
    return pl.pallas_call(
        functools.partial(_gla_kernel, n_tiles=n_tiles, n_seqs=n_seqs, zero_init=zero_init),
        out_shape=[jax.ShapeDtypeStruct((b, t, GV), BF16),
                   jax.ShapeDtypeStruct((b, GLA_HEADS, GLA_DK, GLA_DV), F32),
                   jax.ShapeDtypeStruct((b, GLA_HEADS, GLA_DK, GLA_DV), F32)],
        grid=(b // n_seqs,),
        in_specs=in_specs,
        out_specs=[seq(GV), st, st],
        scratch_shapes=[pltpu.VMEM((2, n_seqs, t, GV), F32),
                        pltpu.VMEM((GQK, GLA_TILE), BF16),
                        pltpu.VMEM((2, GLA_TILE, GLA_TILE), F32),
                        pltpu.VMEM((GLA_HEADS * GLA_TILE, GQK), BF16)],
        name="gla_%d" % t,
        compiler_params=pltpu.CompilerParams(dimension_semantics=("arbitrary",),
                                             vmem_limit_bytes=VMEM_LIMIT),
    )(*args)


def _ffn_kernel(xp_ref, xs_ref, atp_ref, ats_ref, glp_ref, gls_ref, mod_ref, wout_ref, nf_ref,
                wfi_ref, wfo_ref, fn_ref, yp_ref, ys_ref, act_ref, *, ctx_tiles, tiles_per_seq):
    def tile(x_ref, at_ref, gl_ref, y_ref, mod_row):
        gt1, sh2, sc2, gt2 = _mod_rows(mod_ref, mod_row)[2:]
        mix = (jnp.dot(at_ref[...], wout_ref[0:VALL, :], preferred_element_type=F32)
               + jnp.dot(gl_ref[...], wout_ref[VALL:, :], preferred_element_type=F32))
        x1 = x_ref[...] + gt1 * mix
        h2 = (_rms(x1, nf_ref[...]) * (1.0 + sc2) + sh2).astype(BF16)
        for j in range(N_FF_CHUNKS):
            cs = slice(j * FF_CHUNK, (j + 1) * FF_CHUNK)
            a = jnp.dot(h2, wfi_ref[:, cs], preferred_element_type=F32)
            g = jnp.dot(h2, wfi_ref[:, D_FF + j * FF_CHUNK:D_FF + (j + 1) * FF_CHUNK],
                        preferred_element_type=F32)
            act_ref[:, cs] = (a * _sigmoid(a) * g).astype(BF16)
        ff = jnp.dot(act_ref[...], wfo_ref[...], preferred_element_type=F32)
        x2 = x1 + gt2 * ff
        y_ref[...] = _rms(x2, fn_ref[...])

    t = pl.program_id(0)

    @pl.when(t < ctx_tiles)
    def _():
        tile(xp_ref, atp_ref, glp_ref, yp_ref, 0)

    @pl.when(t >= ctx_tiles)
    def _():
        tile(xs_ref, ats_ref, gls_ref, ys_ref, 1 + (t - ctx_tiles) // tiles_per_seq)


def _ffn(xp, xs, attn_p, attn_s, gla_p, gla_s, mod, wout, nf, wfi, wfo, fn, tm, tiles_per_seq):
    d = xp.shape[1]
    ctx_tiles = xp.shape[0] // tm
    lat_tiles = xs.shape[0] // tm
    ctx_map = lambda s: (jnp.minimum(s, ctx_tiles - 1), 0)
    lat_map = lambda s: (jnp.maximum(s - ctx_tiles, 0), 0)
    tile = lambda c, m: pl.BlockSpec((tm, c), m)
    return pl.pallas_call(
        functools.partial(_ffn_kernel, ctx_tiles=ctx_tiles, tiles_per_seq=tiles_per_seq),
        out_shape=[jax.ShapeDtypeStruct(xp.shape, F32), jax.ShapeDtypeStruct(xs.shape, F32)],
        grid=(ctx_tiles + lat_tiles,),
        in_specs=[tile(d, ctx_map), tile(d, lat_map), tile(VALL, ctx_map), tile(VALL, lat_map),
                  tile(GV, ctx_map), tile(GV, lat_map), _const_spec(mod.shape),
                  _const_spec(wout.shape), _const_spec(nf.shape), _const_spec(wfi.shape),
                  _const_spec(wfo.shape), _const_spec(fn.shape)],
        out_specs=[tile(d, ctx_map), tile(d, lat_map)],
        scratch_shapes=[pltpu.VMEM((tm, D_FF), BF16)],
        name="out_ffn",
        compiler_params=pltpu.CompilerParams(dimension_semantics=("arbitrary",),
                                             vmem_limit_bytes=VMEM_LIMIT),
    )(xp, xs, attn_p, attn_s, gla_p, gla_s, mod, wout, nf, wfi, wfo, fn)


def _rope_tables(n_tokens):
    t = np.arange(n_tokens)
    row = (t // GRID_W).astype(np.float32)
    col = (t % GRID_W).astype(np.float32)
    half = MLA_ROPE // 2
    inv = (np.float32(ROPE_BASE) ** (-np.arange(0, half, 2, dtype=np.float32) / np.float32(half))).astype(np.float32)
    ang_r = row[:, None] * inv
    ang_c = col[:, None] * inv
    ang = np.concatenate([ang_r, ang_r, ang_c, ang_c], axis=-1).astype(np.float32)
    cos, sin = np.cos(ang), np.sin(ang)
    first = (np.arange(MLA_ROPE) % half) < (half // 2)
    cos_t = np.ones((n_tokens, LANES), np.float32)
    sa_t = np.zeros((n_tokens, LANES), np.float32)
    sb_t = np.zeros((n_tokens, LANES), np.float32)
    cos_t[:, ROPE_LANE0:ROPE_LANE0 + MLA_ROPE] = cos
    sa_t[:, ROPE_LANE0:ROPE_LANE0 + MLA_ROPE] = np.where(first, -sin, 0.0)
    sb_t[:, ROPE_LANE0:ROPE_LANE0 + MLA_ROPE] = np.where(first, 0.0, sin)
    return jnp.asarray(cos_t), jnp.asarray(sa_t), jnp.asarray(sb_t)


def kernel(x_prompt, x_sample, cache_kv_latent, cache_k_rope, state_gla_fwd, state_gla_bwd, c, c_ctx, w_ada, b_ada, norm_attn, w_in, mla_q_norm, w_uq, mla_kv_norm, w_ukv, w_gate_f, b_gate_f, w_gate_b, b_gate_b, gla_norm, w_out, norm_ffn, w_ffn_in, w_ffn_out, final_norm):
    batch, seq, d = x_prompt.shape
    dec_batch, dec_seq, _ = x_sample.shape
    assert w_ada.shape[0] == 1 and w_in.shape[-1] == W_COLS and w_ffn_in.shape[-1] == 2 * D_FF
    l = 0

    mod = _ada(c_ctx, c, w_ada[l], b_ada[l])

    win, wuq, wk, wvt, wg = _prep_in_weights(w_in, w_uq, w_ukv, w_gate_f, w_gate_b)
    in_w = (norm_attn[l].reshape(1, d), win, mla_q_norm[l].reshape(1, Q_LORA), wuq,
            mla_kv_norm[l].reshape(1, KV_LORA), wk, wvt, wg, b_gate_f, b_gate_b)
    gn = gla_norm[l].reshape(1, GLA_DV)
    tm, tm_ffn = 1024, 512
    r3 = lambda a, b_, t: a.reshape(b_, t, a.shape[-1])

    xp = x_prompt.reshape(batch * seq, d)
    (q, k, vt, gq, gk, gv, gf, gb, go, ckv, kr_t) = _inproj(xp, mod, lambda i: 0, in_w, None, tm, seq)
    (attn_p,) = _attention(r3(q, batch, seq), r3(k, batch, seq), vt, None, seq, ATTN_CTX_SEQS)
    gla_p, sf, sb = _gla(r3(gq, batch, seq), r3(gk, batch, seq), r3(gv, batch, seq), r3(gf, batch, seq),
                         r3(gb, batch, seq), r3(go, batch, seq), None, gn, GLA_CTX_SEQS)

    xs = x_sample.reshape(dec_batch * dec_seq, d)
    tiles = dec_seq // tm
    (q, k, vt, gq, gk, gv, gf, gb, go) = _inproj(xs, mod, lambda i: 1 + i // tiles, in_w,
                                                  _rope_tables(dec_seq), tm, dec_seq)
    kc, vct = _decomp(cache_kv_latent[:, l], jnp.swapaxes(cache_k_rope[:, l], 1, 2), wk, wvt)
    attn_s, wout, wfi, wfo = _attention(r3(q, dec_batch, dec_seq), r3(k, dec_batch, dec_seq), vt, (kc, vct),
                                        ATTN_LAT_QUERIES, 1, (w_out, w_ffn_in, w_ffn_out))
    gla_s, _, _ = _gla(r3(gq, dec_batch, dec_seq), r3(gk, dec_batch, dec_seq), r3(gv, dec_batch, dec_seq),
                       r3(gf, dec_batch, dec_seq), r3(gb, dec_batch, dec_seq), r3(go, dec_batch, dec_seq),
                       (state_gla_fwd[:, l].astype(F32), state_gla_bwd[:, l].astype(F32)), gn, 1)

    flat = lambda a: a.reshape(-1, a.shape[-1])
    y_prompt, y_sample = _ffn(xp, xs, flat(attn_p), flat(attn_s), flat(gla_p), flat(gla_s), mod,
                              wout, norm_ffn[l].reshape(1, d), wfi, wfo,
                              final_norm.reshape(1, d), tm_ffn, dec_seq // tm_ffn)
    y_prompt = y_prompt.reshape(batch, seq, d)
    y_sample = y_sample.reshape(dec_batch, dec_seq, d)

    new_kv_latent = ckv.reshape(batch, 1, seq, KV_LORA)
    new_k_rope = jnp.swapaxes(kr_t, 1, 2).reshape(batch, 1, seq, MLA_ROPE)
    new_state_fwd = sf.reshape(batch, 1, GLA_HEADS, GLA_DK, GLA_DV).astype(x_prompt.dtype)
    new_state_bwd = sb.reshape(batch, 1, GLA_HEADS, GLA_DK, GLA_DV).astype(x_prompt.dtype)
    return (y_prompt, y_sample, new_kv_latent, new_k_rope, new_state_fwd, new_state_bwd)
```

```python
import functools

import numpy as np
import jax
import jax.numpy as jnp
from jax import lax
from jax.experimental import pallas as pl
from jax.experimental.pallas import tpu as pltpu

F32 = jnp.float32
BF16 = jnp.bfloat16

GRID_W = 64
MLA_HEADS = 8
MLA_NOPE = 64
MLA_ROPE = 32
MLA_QK = MLA_NOPE + MLA_ROPE
MLA_V = 64
Q_LORA = 384
KV_LORA = 256
GLA_HEADS = 4
GLA_DK = 64
GLA_DV = 128
GATE_RANK = 16
GATE_NORM = 16.0
CHUNK = 64
D_FF = 2816
ROPE_BASE = 10000.0
EPS = 1e-6
LOG2_E = 1.4426950408889634

LANES = 128
HEAD_PAD = LANES
ROPE_LANE0 = MLA_NOPE
GQK = GLA_HEADS * GLA_DK
GV = GLA_HEADS * GLA_DV
QPAD = MLA_HEADS * HEAD_PAD
VALL = MLA_HEADS * MLA_V
ONES_ROWS = 16
KEY_BLOCK = 1024

W_KR = Q_LORA + KV_LORA
W_GQ = W_KR + MLA_ROPE
W_GF = W_GQ + 2 * GQK + GV
W_GO = W_GF + 2 * GATE_RANK
W_COLS = W_GO + GV

Z_Q = 0
Z_KV = Z_Q + Q_LORA
Z_GQ = Z_KV + KV_LORA
Z_GK = Z_GQ + GQK
Z_GV = Z_GK + GQK
Z_GO = Z_GV + GV
Z_MISC = Z_GO + GV
Z_COLS = Z_MISC + LANES

FF_CHUNK = 256
N_FF_CHUNKS = D_FF // FF_CHUNK

GLA_TILE = 256
CHUNKS_PER_TILE = GLA_TILE // CHUNK
ADA_ROWS = 128
INPROJ_SUB = 512
Q_TILE = 256
ATTN_LAT_QUERIES = 512
ATTN_CTX_SEQS = 4
GLA_CTX_SEQS = 4

VMEM_LIMIT = 56 * 1024 * 1024

_NT = (((1,), (1,)), ((), ()))


def _rms(x, w):
    return x * lax.rsqrt(jnp.mean(x * x, axis=-1, keepdims=True) + EPS) * w


def _sigmoid(x):
    return 1.0 / (1.0 + jnp.exp(-x))


def _log_sigmoid(x):
    return jnp.minimum(x, 0.0) - jnp.log1p(jnp.exp(-jnp.abs(x)))


def _interleave(chains):
    pending, active = list(chains), []
    while pending or active:
        if pending:
            active.append(pending.pop(0))
        for chain in list(active):
            try:
                next(chain)
            except StopIteration:
                active.remove(chain)


def _const_spec(shape):
    nd = len(shape)
    return pl.BlockSpec(shape, lambda *_: (0,) * nd, pipeline_mode=pl.Buffered(1))


def _mod_rows(mod_ref, r):
    return [mod_ref[k, pl.ds(r, 1), :] for k in range(mod_ref.shape[0])]


def _ada_kernel(cctx_ref, c_ref, w_ref, b_ref, o_ref):
    k = pl.program_id(0)
    d = o_ref.shape[2]
    row = lax.broadcasted_iota(jnp.int32, (8, cctx_ref.shape[1]), 0)
    cond = jnp.where(row == 0, cctx_ref[...], 0.0)
    for r in range(c_ref.shape[0]):
        cond = jnp.where(row == 1 + r, c_ref[r:r + 1, :], cond)
    s = (cond * _sigmoid(cond)).astype(BF16)
    part = jnp.dot(s, w_ref[...].astype(BF16), preferred_element_type=F32)
    for j in range(o_ref.shape[0]):
        sl = slice(j * d, (j + 1) * d)

        @pl.when(k == 0)
        def _():
            o_ref[j] = part[:, sl] + b_ref[:, sl]

        @pl.when(k > 0)
        def _():
            o_ref[j] += part[:, sl]


def _ada(c_ctx, c, w_ada, b_ada):
    d = w_ada.shape[0]
    n = w_ada.shape[1]
    assert 1 + c.shape[0] <= 8
    return pl.pallas_call(
        _ada_kernel,
        out_shape=jax.ShapeDtypeStruct((n // d, 8, d), F32),
        grid=(d // ADA_ROWS,),
        in_specs=[pl.BlockSpec((1, ADA_ROWS), lambda k: (0, k)),
                  pl.BlockSpec((c.shape[0], ADA_ROWS), lambda k: (0, k)),
                  pl.BlockSpec((ADA_ROWS, n), lambda k: (k, 0)),
                  pl.BlockSpec((1, n), lambda k: (0, 0))],
        out_specs=pl.BlockSpec((n // d, 8, d), lambda k: (0, 0, 0)),
        name="ada_mod",
        compiler_params=pltpu.CompilerParams(dimension_semantics=("arbitrary",)),
    )(c_ctx.reshape(1, d), c, w_ada, b_ada.reshape(1, n))


def _prep_kernel(wint_ref, wuq_ref, wukv_ref, wgf_ref, wgb_ref, win_o, wuq_o, wk_o, wvt_o, wg_o):
    cols = wint_ref.shape[1]
    for dst, src, n in ((Z_Q, 0, W_KR), (Z_GQ, W_GQ, W_GF - W_GQ), (Z_GO, W_GO, GV)):
        win_o[:, dst:dst + n] = wint_ref[src:src + n, :].T.astype(BF16)
    z32 = jnp.zeros((32, cols), F32)
    misc_t = jnp.concatenate([wint_ref[W_GF:W_GO, :], z32, wint_ref[W_KR:W_GQ, :], z32], axis=0)
    win_o[:, Z_MISC:Z_COLS] = misc_t.T.astype(BF16)

    u = wuq_ref[...]
    zq = jnp.zeros((u.shape[0], HEAD_PAD - MLA_QK), F32)
    for hd in range(MLA_HEADS):
        blk = jnp.concatenate([u[:, hd * MLA_QK:(hd + 1) * MLA_QK], zq], axis=1)
        wuq_o[:, hd * HEAD_PAD:(hd + 1) * HEAD_PAD] = blk.astype(BF16)

    @pl.when(pl.program_id(0) == 0)
    def _():
        kv = wukv_ref[...]
        per = MLA_NOPE + MLA_V
        lane = lax.broadcasted_iota(jnp.int32, (kv.shape[0], per), 1)
        for hd in range(MLA_HEADS):
            blk = kv[:, hd * per:(hd + 1) * per]
            wk_o[:, hd * HEAD_PAD:(hd + 1) * HEAD_PAD] = jnp.where(lane < MLA_NOPE, blk, 0.0).astype(BF16)
        wv = jnp.concatenate([kv[:, hd * per + MLA_NOPE:(hd + 1) * per] for hd in range(MLA_HEADS)], axis=1)
        wvt_o[...] = wv.T.astype(BF16)

        wg_o[...] = jnp.zeros(wg_o.shape, BF16)
        wg_o[0:GATE_RANK, 0:GQK] = wgf_ref[...].astype(BF16)
        wg_o[GATE_RANK:2 * GATE_RANK, GQK:2 * GQK] = wgb_ref[...].astype(BF16)


def _prep_in_weights(w_in, w_uq, w_ukv, w_gate_f, w_gate_b):
    d = w_in.shape[1]
    steps = 4
    w_in_t = jnp.swapaxes(w_in, 1, 2)
    rb3 = lambda r, c: pl.BlockSpec((None, r // steps, c), lambda i: (0, i, 0))
    rb = lambda r, c: pl.BlockSpec((r // steps, c), lambda i: (i, 0))
    full3 = lambda shape: pl.BlockSpec((None,) + tuple(shape[1:]), lambda i: (0, 0, 0))
    full = lambda shape: pl.BlockSpec(shape, lambda i: (0, 0))
    return pl.pallas_call(
        _prep_kernel,
        out_shape=[jax.ShapeDtypeStruct((d, Z_COLS), BF16),
                   jax.ShapeDtypeStruct((Q_LORA, QPAD), BF16),
                   jax.ShapeDtypeStruct((KV_LORA, QPAD), BF16),
                   jax.ShapeDtypeStruct((VALL, KV_LORA), BF16),
                   jax.ShapeDtypeStruct((LANES, 2 * GQK), BF16)],
        grid=(steps,),
        in_specs=[pl.BlockSpec((None, W_COLS, d // steps), lambda i: (0, 0, i)),
                  rb3(Q_LORA, MLA_HEADS * MLA_QK), full3(w_ukv.shape),
                  full3(w_gate_f.shape), full3(w_gate_b.shape)],
        out_specs=[rb(d, Z_COLS), rb(Q_LORA, QPAD), full((KV_LORA, QPAD)), full((VALL, KV_LORA)),
                   full((LANES, 2 * GQK))],
        name="weight_prep",
        compiler_params=pltpu.CompilerParams(dimension_semantics=("arbitrary",)),
    )(w_in_t, w_uq, w_ukv, w_gate_f, w_gate_b)


def _inproj_kernel(*refs, latent, mod_row):
    (x_ref, mod_ref, nw_ref, win_ref, qn_ref, wuq_ref, kvn_ref, wk_ref, wvt_ref, wg_ref, bgf_ref, bgb_ref) = refs[:12]
    if latent:
        cos_ref, sa_ref, sb_ref = refs[12:15]
        outs = refs[15:]
    else:
        outs = refs[12:]
    q_ref, k_ref, vt_ref, gq_ref, gk_ref, gv_ref, gf_ref, gb_ref, go_ref = outs[:9]

    sh1, sc1 = _mod_rows(mod_ref, mod_row(pl.program_id(0)))[:2]
    scale = MLA_QK ** -0.5 * LOG2_E
    lane = lax.broadcasted_iota(jnp.int32, (INPROJ_SUB, LANES), 1)
    in_rope = (lane >= ROPE_LANE0) & (lane < ROPE_LANE0 + MLA_ROPE)

    def sub_tile(r0):
        rows = slice(r0, r0 + INPROJ_SUB)
        h = (_rms(x_ref[rows, :], nw_ref[...]) * (1.0 + sc1) + sh1).astype(BF16)
        yield
        z_all = jnp.dot(h, win_ref[...], preferred_element_type=F32)
        z = lambda lo, n: z_all[:, lo:lo + n]
        yield
        qn = _rms(z(Z_Q, Q_LORA), qn_ref[...]).astype(BF16)
        ckv = _rms(z(Z_KV, KV_LORA), kvn_ref[...])
        ckv_b = ckv.astype(BF16)
        misc = z(Z_MISC, LANES)
        yield
        q = jnp.dot(qn, wuq_ref[...], preferred_element_type=F32)
        kn = jnp.dot(ckv_b, wk_ref[...], preferred_element_type=F32)
        vt_ref[:, rows] = lax.dot_general(wvt_ref[...], ckv_b, _NT,
                                          preferred_element_type=F32).astype(BF16)
        gpre = jnp.dot(misc.astype(BF16), wg_ref[...], preferred_element_type=F32)
        yield
        if latent:
            cos, sa, sb = cos_ref[rows, :], sa_ref[rows, :], sb_ref[rows, :]

            def rope(t):
                return t * cos + pltpu.roll(t, LANES - 8, 1) * sa + pltpu.roll(t, 8, 1) * sb
        else:
            def rope(t):
                return t

        krope = rope(misc)
        for hd in range(MLA_HEADS):
            sl = slice(hd * HEAD_PAD, (hd + 1) * HEAD_PAD)
            q_ref[rows, sl] = (rope(q[:, sl]) * scale).astype(BF16)
            k_ref[rows, sl] = jnp.where(in_rope, krope, kn[:, sl]).astype(BF16)
        gq_ref[rows, :] = z(Z_GQ, GQK)
        gk_ref[rows, :] = z(Z_GK, GQK)
        gv_ref[rows, :] = z(Z_GV, GV).astype(BF16)
        go_ref[rows, :] = z(Z_GO, GV)
        gf_ref[rows, :] = _log_sigmoid(gpre[:, :GQK] + bgf_ref[...]) * (1.0 / GATE_NORM)
        gb_ref[rows, :] = _log_sigmoid(gpre[:, GQK:] + bgb_ref[...]) * (1.0 / GATE_NORM)
        if not latent:
            ckv_ref, krt_ref = outs[9:]
            ckv_ref[rows, :] = ckv
            misc_t = misc.T
            n = krt_ref.shape[2]
            for b in range(INPROJ_SUB // n):
                krt_ref[r0 // n + b] = misc_t[ROPE_LANE0:ROPE_LANE0 + MLA_ROPE, b * n:(b + 1) * n]

    _interleave([sub_tile(r0) for r0 in range(0, x_ref.shape[0], INPROJ_SUB)])


def _inproj(x2d, mod, mod_row, weights, rope_tabs, tm, seq_len):
    n_tok, d = x2d.shape
    latent = rope_tabs is not None
    tiles_per_seq = max(seq_len // tm, 1)
    nw, win, qn, wuq, kvn, wk, wvt, wg, bgf, bgb = weights
    row = lambda i: (i, 0)
    in_specs = [pl.BlockSpec((tm, d), row), _const_spec(mod.shape),
                _const_spec(nw.shape), _const_spec(win.shape), _const_spec(qn.shape),
                _const_spec(wuq.shape), _const_spec(kvn.shape), _const_spec(wk.shape),
                _const_spec(wvt.shape), _const_spec(wg.shape), _const_spec(bgf.shape),
                _const_spec(bgb.shape)]
    args = [x2d, mod, nw, win, qn, wuq, kvn, wk, wvt, wg, bgf, bgb]
    if latent:
        tab = pl.BlockSpec((tm, LANES), lambda i: (i % tiles_per_seq, 0))
        in_specs += [tab, tab, tab]
        args += list(rope_tabs)
    out_cols = [(QPAD, BF16), (QPAD, BF16), None, (GQK, F32), (GQK, F32), (GV, BF16),
                (GQK, F32), (GQK, F32), (GV, F32)]
    if not latent:
        out_cols += [(KV_LORA, F32)]
    out_shape = [jax.ShapeDtypeStruct((n_tok, oc[0]), oc[1]) if oc else
                 jax.ShapeDtypeStruct((VALL, n_tok), BF16) for oc in out_cols]
    out_specs = [pl.BlockSpec((tm, oc[0]), row) if oc else
                 pl.BlockSpec((VALL, tm), lambda i: (0, i)) for oc in out_cols]
    if not latent:
        seq = seq_len
        out_shape.append(jax.ShapeDtypeStruct((n_tok // seq, MLA_ROPE, seq), F32))
        out_specs.append(pl.BlockSpec((tm // seq, MLA_ROPE, seq), lambda i: (i, 0, 0)))
    return pl.pallas_call(
        functools.partial(_inproj_kernel, latent=latent, mod_row=mod_row),
        out_shape=out_shape,
        grid=(n_tok // tm,),
        in_specs=in_specs,
        out_specs=out_specs,
        name="inproj_lat" if latent else "inproj_ctx",
        compiler_params=pltpu.CompilerParams(dimension_semantics=("arbitrary",),
                                             vmem_limit_bytes=VMEM_LIMIT),
    )(*args)


def _decomp_kernel(ckv_ref, krt_ref, wk_ref, wvt_ref, k_ref, vt_ref):
    ckv_b = ckv_ref[...].astype(BF16)
    kn = jnp.dot(ckv_b, wk_ref[...], preferred_element_type=F32)
    n_keys = krt_ref.shape[1]
    kr = jnp.concatenate([jnp.zeros((ROPE_LANE0, n_keys), F32), krt_ref[...],
                          jnp.zeros((LANES - ROPE_LANE0 - MLA_ROPE, n_keys), F32)], axis=0).T
    lane = lax.broadcasted_iota(jnp.int32, kr.shape, 1)
    in_rope = (lane >= ROPE_LANE0) & (lane < ROPE_LANE0 + MLA_ROPE)
    for hd in range(MLA_HEADS):
        sl = slice(hd * HEAD_PAD, (hd + 1) * HEAD_PAD)
        k_ref[:, sl] = jnp.where(in_rope, kr, kn[:, sl]).astype(BF16)
    vt_ref[...] = lax.dot_general(wvt_ref[...], ckv_b, _NT, preferred_element_type=F32).astype(BF16)


def _decomp(ckv, kr_t, wk, wvt):
    b, s, _ = ckv.shape
    return pl.pallas_call(
        _decomp_kernel,
        out_shape=[jax.ShapeDtypeStruct((b, s, QPAD), BF16), jax.ShapeDtypeStruct((VALL, b * s), BF16)],
        grid=(b,),
        in_specs=[pl.BlockSpec((None, s, KV_LORA), lambda i: (i, 0, 0)),
                  pl.BlockSpec((None, MLA_ROPE, s), lambda i: (i, 0, 0)),
                  _const_spec(wk.shape), _const_spec(wvt.shape)],
        out_specs=[pl.BlockSpec((None, s, QPAD), lambda i: (i, 0, 0)),
                   pl.BlockSpec((VALL, s), lambda i: (0, i))],
        name="ctx_decompress",
        compiler_params=pltpu.CompilerParams(dimension_semantics=("arbitrary",)),
    )(ckv, kr_t, wk, wvt)


def _attn_kernel(*refs, has_ctx, n_seqs, n_side):
    n_in = 5 if has_ctx else 3
    side_in, refs = refs[n_in:n_in + n_side], refs[:n_in] + refs[n_in + n_side:]
    side_out, refs = refs[n_in + 1:n_in + 1 + n_side], refs[:n_in + 1] + refs[n_in + 1 + n_side:]
    if has_ctx:
        q_ref, kc_ref, vct_ref, k_ref, vt_ref, o_ref, st_ref, p_ref = refs
    else:
        q_ref, k_ref, vt_ref, o_ref, st_ref, p_ref = refs
    tq = Q_TILE

    for src, dst in zip(side_in, side_out):
        dst[...] = src[...].astype(BF16)

    def key_blocks(bi):
        srcs = [(kc_ref, vct_ref)] if has_ctx else []
        blocks, row0 = [], 0
        for kr, vr in srcs + [(k_ref, vt_ref)]:
            n_keys = kr.shape[1]
            size = min(KEY_BLOCK, n_keys)
            for r in range(0, n_keys, size):
                blocks.append((kr, vr, r, bi * n_keys + r, size, row0))
                row0 += size
        return blocks

    units = [(bi, slice(q0, q0 + tq), hd) for bi in range(n_seqs)
             for q0 in range(0, q_ref.shape[1], tq) for hd in range(MLA_HEADS)]
    col_max = [None] * len(units)
    pair = []
    for stage in range(len(units) + 2):
        ua, ub, uc = stage, stage - 1, stage - 2
        run_max = None
        acc = jnp.zeros((MLA_V + ONES_ROWS, tq), F32)
        for j in range(len(key_blocks(0))):
            if ua < len(units):
                bi, qrows, hd = units[ua]
                kr, _, r0, _, size, srow = key_blocks(bi)[j]
                sl = slice(hd * HEAD_PAD, (hd + 1) * HEAD_PAD)
                st = lax.dot_general(kr[bi, r0:r0 + size, sl], q_ref[bi, qrows, sl], _NT,
                                     preferred_element_type=F32)
                st_ref[ua % 2, srow:srow + size, :] = st
                blk_max = jnp.max(st.reshape(size // 8, 8, tq), axis=0)
                run_max = blk_max if run_max is None else jnp.maximum(run_max, blk_max)
            if 0 <= ub < len(units):
                _, _, _, _, size, srow = key_blocks(0)[j]
                p_ref[ub % 2, srow:srow + size, :] = jnp.exp2(
                    st_ref[ub % 2, srow:srow + size, :] - col_max[ub]).astype(BF16)
            if uc >= 0:
                bi, _, hd = units[uc]
                _, vr, _, c0, size, srow = key_blocks(bi)[j]
                v_aug = jnp.concatenate([vr[hd * MLA_V:(hd + 1) * MLA_V, c0:c0 + size],
                                         jnp.ones((ONES_ROWS, size), BF16)], axis=0)
                acc = acc + jnp.dot(v_aug, p_ref[uc % 2, srow:srow + size, :],
                                    preferred_element_type=F32)
        if ua < len(units):
            col_max[ua] = jnp.max(run_max, axis=0, keepdims=True)
        if uc >= 0:
            bi, qrows, hd = units[uc]
            pair.append(acc[:MLA_V, :] / acc[MLA_V:MLA_V + 1, :])
            if len(pair) == 2:
                o_ref[bi, qrows, (hd - 1) * MLA_V:(hd + 1) * MLA_V] = (
                    jnp.concatenate(pair, axis=0).T.astype(BF16))
                pair = []


def _attention(q, k, vt, ctx_kv, tq, n_seqs, side_weights=()):
    b, t, _ = q.shape
    steps = (b // n_seqs) * (t // tq)
    assert (n_seqs == 1 or tq == t) and tq % Q_TILE == 0
    has_ctx = ctx_kv is not None
    in_specs = [pl.BlockSpec((n_seqs, tq, QPAD), lambda i, j: (i, j, 0))]
    args = [q]
    if has_ctx:
        kc, vct = ctx_kv
        s = kc.shape[1]
        in_specs += [pl.BlockSpec((n_seqs, s, QPAD), lambda i, j: (i, 0, 0)),
                     pl.BlockSpec((VALL, n_seqs * s), lambda i, j: (0, i))]
        args += [kc, vct]
    in_specs += [pl.BlockSpec((n_seqs, t, QPAD), lambda i, j: (i, 0, 0)),
                 pl.BlockSpec((VALL, n_seqs * t), lambda i, j: (0, i))]
    args += [k, vt]
    out_shape = [jax.ShapeDtypeStruct((b, t, VALL), BF16)]
    out_specs = [pl.BlockSpec((n_seqs, tq, VALL), lambda i, j: (i, j, 0))]
    nj = t // tq
    for w in side_weights:
        _, rows, cols = w.shape
        assert rows % (16 * steps) == 0
        in_specs.append(pl.BlockSpec((None, rows // steps, cols), lambda i, j: (0, i * nj + j, 0)))
        args.append(w)
        out_shape.append(jax.ShapeDtypeStruct((rows, cols), BF16))
        out_specs.append(pl.BlockSpec((rows // steps, cols), lambda i, j: (i * nj + j, 0)))
    return pl.pallas_call(
        functools.partial(_attn_kernel, has_ctx=has_ctx, n_seqs=n_seqs, n_side=len(side_weights)),
        out_shape=out_shape,
        grid=(b // n_seqs, t // tq),
        in_specs=in_specs,
        out_specs=out_specs,
        scratch_shapes=[pltpu.VMEM((2, t + (s if has_ctx else 0), Q_TILE), F32),
                        pltpu.VMEM((2, t + (s if has_ctx else 0), Q_TILE), BF16)],
        name="mla_attn_lat" if has_ctx else "mla_attn_ctx",
        compiler_params=pltpu.CompilerParams(dimension_semantics=("arbitrary", "arbitrary"),
                                             vmem_limit_bytes=VMEM_LIMIT),
    )(*args)


def _gla_kernel(*refs, n_tiles, n_seqs, zero_init):
    gq_ref, gk_ref, gv_ref, gf_ref, gb_ref, go_ref = refs[:6]
    if zero_init:
        gn_ref, o_ref, sf_ref, sb_ref, oacc_ref, bdqk_ref, tri_ref, hm_ref = refs[6:]
    else:
        (sf0_ref, sb0_ref, gn_ref, o_ref, sf_ref, sb_ref, oacc_ref,
         bdqk_ref, tri_ref, hm_ref) = refs[6:]
    g_refs = (gf_ref, gb_ref)
    state_refs = (sf_ref, sb_ref)

    @pl.when(pl.program_id(0) == 0)
    def _():
        ri = lax.broadcasted_iota(jnp.int32, (GLA_TILE, GLA_TILE), 0)
        ci = lax.broadcasted_iota(jnp.int32, (GLA_TILE, GLA_TILE), 1)
        same_chunk = (ri // CHUNK) == (ci // CHUNK)
        bdqk_ref[...] = jnp.where(same_chunk, 1.0, 0.0).astype(BF16)
        tri_ref[0] = jnp.where(same_chunk & (ri >= ci), 1.0, 0.0)
        tri_ref[1] = jnp.where(same_chunk & (ci >= ri), 1.0, 0.0)
        hm_ref[...] = jnp.where(
            lax.broadcasted_iota(jnp.int32, (GLA_HEADS * GLA_TILE, GQK), 0) // GLA_TILE
            == lax.broadcasted_iota(jnp.int32, (GLA_HEADS * GLA_TILE, GQK), 1) // GLA_DK,
            1.0, 0.0).astype(BF16)

    row8 = lax.broadcasted_iota(jnp.int32, (8, GQK), 0)

    def tile_rows(t):
        return pl.ds(pl.multiple_of(t * GLA_TILE, GLA_TILE), GLA_TILE)

    def total_row(c, d):
        return c * CHUNK + (CHUNK - 1 if d == 0 else 0)

    def tile_dir(b, t, d):
        rows = tile_rows(t)
        g = g_refs[d][b, rows, :]
        g_hi = g.astype(BF16)
        g_lo = (g - g_hi.astype(F32)).astype(BF16)
        tri_b = tri_ref[d].astype(BF16)
        cum = (jnp.dot(tri_b, g_hi, preferred_element_type=F32)
               + jnp.dot(tri_b, g_lo, preferred_element_type=F32))
        yield
        totals = [cum[total_row(c, d):total_row(c, d) + 1, :] for c in range(CHUNKS_PER_TILE)]
        tot8 = jnp.zeros((8, GQK), F32)
        for c in range(CHUNKS_PER_TILE):
            tot8 = jnp.where(row8 == c, totals[c], tot8)
        dec_t = jnp.concatenate([jnp.exp(tot8), jnp.zeros((LANES - 8, GQK), F32)], axis=0).T
        q = gq_ref[b, rows, :] * (GLA_DK ** -0.5)
        k = gk_ref[b, rows, :]
        v = gv_ref[b, rows, :]
        tot = jnp.concatenate([jnp.broadcast_to(tc, (CHUNK, GQK)) for tc in totals], axis=0)
        qe = (q * jnp.exp(cum)).astype(BF16)
        ke = (k * jnp.exp(-cum)).astype(BF16)
        kd_t = (k * jnp.exp(tot - cum)).T.astype(BF16)
        bd_qk = bdqk_ref[...] > 0
        tri = tri_ref[d] > 0

        qm = jnp.where(hm_ref[...] > 0, jnp.tile(qe, (GLA_HEADS, 1)), 0.0)
        yield
        att = lax.dot_general(qm, ke, _NT, preferred_element_type=F32)

        yield
        intra, upd = [], []
        for hd in range(GLA_HEADS):
            vh = v[:, hd * GLA_DV:(hd + 1) * GLA_DV]
            a_h = jnp.where(tri, att[hd * GLA_TILE:(hd + 1) * GLA_TILE, :], 0.0).astype(BF16)
            intra.append(jnp.dot(a_h, vh, preferred_element_type=F32))
            kd_h = jnp.tile(kd_t[hd * GLA_DK:(hd + 1) * GLA_DK, :], (CHUNKS_PER_TILE, 1))
            upd.append(jnp.dot(jnp.where(bd_qk, kd_h, 0.0), vh, preferred_element_type=F32))

        yield
        state = [state_refs[d][b, hd] for hd in range(GLA_HEADS)]
        order = range(CHUNKS_PER_TILE) if d == 0 else range(CHUNKS_PER_TILE - 1, -1, -1)
        seen = {}
        for c in order:
            seen[c] = jnp.concatenate(state, axis=0).astype(BF16)
            decay = jnp.broadcast_to(dec_t[:, c:c + 1], (GQK, GLA_DV))
            for hd in range(GLA_HEADS):
                ks = slice(hd * GLA_DK, (hd + 1) * GLA_DK)
                state[hd] = decay[ks, :] * state[hd] + upd[hd][c * CHUNK:(c + 1) * CHUNK, :]
        for hd in range(GLA_HEADS):
            state_refs[d][b, hd] = state[hd]

        yield
        for c in range(CHUNKS_PER_TILE):
            cr = slice(c * CHUNK, (c + 1) * CHUNK)
            q_c = jnp.concatenate([qm[hd * GLA_TILE + c * CHUNK:hd * GLA_TILE + (c + 1) * CHUNK, :]
                                   for hd in range(GLA_HEADS)], axis=0)
            inter = jnp.dot(q_c, seen[c], preferred_element_type=F32)
            o = jnp.concatenate([intra[hd][cr, :] + inter[hd * CHUNK:(hd + 1) * CHUNK, :]
                                 for hd in range(GLA_HEADS)], axis=1)
            oacc_ref[d, b, pl.ds(pl.multiple_of(t * GLA_TILE + c * CHUNK, CHUNK), CHUNK), :] = o

    if zero_init:
        sf_ref[...] = jnp.zeros(sf_ref.shape, F32)
        sb_ref[...] = jnp.zeros(sb_ref.shape, F32)
    else:
        sf_ref[...] = sf0_ref[...]
        sb_ref[...] = sb0_ref[...]

    tiles_per_step = 2 if n_tiles % 2 == 0 else 1

    def main_body(i, carry):
        chains = []
        for u in range(tiles_per_step):
            t = i * tiles_per_step + u
            for b in range(n_seqs):
                chains += [tile_dir(b, t, 0), tile_dir(b, n_tiles - 1 - t, 1)]
        _interleave(chains)
        return carry

    lax.fori_loop(0, n_tiles // tiles_per_step, main_body, 0)

    gn = gn_ref[...]

    def epilogue_body(t, carry):
        rows = tile_rows(t)
        for b in range(n_seqs):
            for hd in range(GLA_HEADS):
                vs = slice(hd * GLA_DV, (hd + 1) * GLA_DV)
                o = _rms(oacc_ref[0, b, rows, vs] + oacc_ref[1, b, rows, vs], gn)
                go = go_ref[b, rows, vs]
                o_ref[b, rows, vs] = (o * (go * _sigmoid(go))).astype(BF16)
        return carry

    lax.fori_loop(0, n_tiles, epilogue_body, 0)


def _gla(gq, gk, gv, gf, gb, go, init_states, gn, n_seqs):
    b, t, _ = gq.shape
    n_tiles = t // GLA_TILE
    zero_init = init_states is None
    seq = lambda c: pl.BlockSpec((n_seqs, t, c), lambda i: (i, 0, 0))
    st = pl.BlockSpec((n_seqs, GLA_HEADS, GLA_DK, GLA_DV), lambda i: (i, 0, 0, 0))
    in_specs = [seq(GQK), seq(GQK), seq(GV), seq(GQK), seq(GQK), seq(GV)]
    args = [gq, gk, gv, gf, gb, go]
    if not zero_init:
        in_specs += [st, st]
        args += list(init_states)
    in_specs.append(_const_spec(gn.shape))
    args.append(gn)
    return pl.pallas_call(
        functools.partial(_gla_kernel, n_tiles=n_tiles, n_seqs=n_seqs, zero_init=zero_init),
        out_shape=[jax.ShapeDtypeStruct((b, t, GV), BF16),
                   jax.ShapeDtypeStruct((b, GLA_HEADS, GLA_DK, GLA_DV), F32),
                   jax.ShapeDtypeStruct((b, GLA_HEADS, GLA_DK, GLA_DV), F32)],
        grid=(b // n_seqs,),
        in_specs=in_specs,
        out_specs=[seq(GV), st, st],
        scratch_shapes=[pltpu.VMEM((2, n_seqs, t, GV), F32),
                        pltpu.VMEM((GQK, GLA_TILE), BF16),
                        pltpu.VMEM((2, GLA_TILE, GLA_TILE), F32),
                        pltpu.VMEM((GLA_HEADS * GLA_TILE, GQK), BF16)],
        name="gla_%d" % t,
        compiler_params=pltpu.CompilerParams(dimension_semantics=("arbitrary",),
                                             vmem_limit_bytes=VMEM_LIMIT),
    )(*args)


def _ffn_kernel(xp_ref, xs_ref, atp_ref, ats_ref, glp_ref, gls_ref, mod_ref, wout_ref, nf_ref,
                wfi_ref, wfo_ref, fn_ref, yp_ref, ys_ref, act_ref, *, ctx_tiles, tiles_per_seq):
    def tile(x_ref, at_ref, gl_ref, y_ref, mod_row):
        gt1, sh2, sc2, gt2 = _mod_rows(mod_ref, mod_row)[2:]
        mix = (jnp.dot(at_ref[...], wout_ref[0:VALL, :], preferred_element_type=F32)
               + jnp.dot(gl_ref[...], wout_ref[VALL:, :], preferred_element_type=F32))
        x1 = x_ref[...] + gt1 * mix
        h2 = (_rms(x1, nf_ref[...]) * (1.0 + sc2) + sh2).astype(BF16)
        for j in range(N_FF_CHUNKS):
            cs = slice(j * FF_CHUNK, (j + 1) * FF_CHUNK)
            a = jnp.dot(h2, wfi_ref[:, cs], preferred_element_type=F32)
            g = jnp.dot(h2, wfi_ref[:, D_FF + j * FF_CHUNK:D_FF + (j + 1) * FF_CHUNK],
                        preferred_element_type=F32)
            act_ref[:, cs] = (a * _sigmoid(a) * g).astype(BF16)
        ff = jnp.dot(act_ref[...], wfo_ref[...], preferred_element_type=F32)
        x2 = x1 + gt2 * ff
        y_ref[...] = _rms(x2, fn_ref[...])

    t = pl.program_id(0)

    @pl.when(t < ctx_tiles)
    def _():
        tile(xp_ref, atp_ref, glp_ref, yp_ref, 0)

    @pl.when(t >= ctx_tiles)
    def _():
        tile(xs_ref, ats_ref, gls_ref, ys_ref, 1 + (t - ctx_tiles) // tiles_per_seq)


def _ffn(xp, xs, attn_p, attn_s, gla_p, gla_s, mod, wout, nf, wfi, wfo, fn, tm, tiles_per_seq):
    d = xp.shape[1]
    ctx_tiles = xp.shape[0] // tm
    lat_tiles = xs.shape[0] // tm
    ctx_map = lambda s: (jnp.minimum(s, ctx_tiles - 1), 0)
    lat_map = lambda s: (jnp.maximum(s - ctx_tiles, 0), 0)
    tile = lambda c, m: pl.BlockSpec((tm, c), m)
    return pl.pallas_call(
        functools.partial(_ffn_kernel, ctx_tiles=ctx_tiles, tiles_per_seq=tiles_per_seq),
        out_shape=[jax.ShapeDtypeStruct(xp.shape, F32), jax.ShapeDtypeStruct(xs.shape, F32)],
        grid=(ctx_tiles + lat_tiles,),
        in_specs=[tile(d, ctx_map), tile(d, lat_map), tile(VALL, ctx_map), tile(VALL, lat_map),
                  tile(GV, ctx_map), tile(GV, lat_map), _const_spec(mod.shape),
                  _const_spec(wout.shape), _const_spec(nf.shape), _const_spec(wfi.shape),
                  _const_spec(wfo.shape), _const_spec(fn.shape)],
        out_specs=[tile(d, ctx_map), tile(d, lat_map)],
        scratch_shapes=[pltpu.VMEM((tm, D_FF), BF16)],
        name="out_ffn",
        compiler_params=pltpu.CompilerParams(dimension_semantics=("arbitrary",),
                                             vmem_limit_bytes=VMEM_LIMIT),
    )(xp, xs, attn_p, attn_s, gla_p, gla_s, mod, wout, nf, wfi, wfo, fn)


def _rope_tables(n_tokens):
    t = np.arange(n_tokens)
    row = (t // GRID_W).astype(np.float32)
    col = (t % GRID_W).astype(np.float32)
    half = MLA_ROPE // 2
    inv = (np.float32(ROPE_BASE) ** (-np.arange(0, half, 2, dtype=np.float32) / np.float32(half))).astype(np.float32)
    ang_r = row[:, None] * inv
    ang_c = col[:, None] * inv
    ang = np.concatenate([ang_r, ang_r, ang_c, ang_c], axis=-1).astype(np.float32)
    cos, sin = np.cos(ang), np.sin(ang)
    first = (np.arange(MLA_ROPE) % half) < (half // 2)
    cos_t = np.ones((n_tokens, LANES), np.float32)
    sa_t = np.zeros((n_tokens, LANES), np.float32)
    sb_t = np.zeros((n_tokens, LANES), np.float32)
    cos_t[:, ROPE_LANE0:ROPE_LANE0 + MLA_ROPE] = cos
    sa_t[:, ROPE_LANE0:ROPE_LANE0 + MLA_ROPE] = np.where(first, -sin, 0.0)
    sb_t[:, ROPE_LANE0:ROPE_LANE0 + MLA_ROPE] = np.where(first, 0.0, sin)
    return jnp.asarray(cos_t), jnp.asarray(sa_t), jnp.asarray(sb_t)


def kernel(x_prompt, x_sample, cache_kv_latent, cache_k_rope, state_gla_fwd, state_gla_bwd, c, c_ctx, w_ada, b_ada, norm_attn, w_in, mla_q_norm, w_uq, mla_kv_norm, w_ukv, w_gate_f, b_gate_f, w_gate_b, b_gate_b, gla_norm, w_out, norm_ffn, w_ffn_in, w_ffn_out, final_norm):
    batch, seq, d = x_prompt.shape
    dec_batch, dec_seq, _ = x_sample.shape
    assert w_ada.shape[0] == 1 and w_in.shape[-1] == W_COLS and w_ffn_in.shape[-1] == 2 * D_FF
    l = 0

    mod = _ada(c_ctx, c, w_ada[l], b_ada[l])

    win, wuq, wk, wvt, wg = _prep_in_weights(w_in, w_uq, w_ukv, w_gate_f, w_gate_b)
    in_w = (norm_attn[l].reshape(1, d), win, mla_q_norm[l].reshape(1, Q_LORA), wuq,
            mla_kv_norm[l].reshape(1, KV_LORA), wk, wvt, wg, b_gate_f, b_gate_b)
    gn = gla_norm[l].reshape(1, GLA_DV)
    tm, tm_ffn = 512, 512
    r3 = lambda a, b_, t: a.reshape(b_, t, a.shape[-1])

    xp = x_prompt.reshape(batch * seq, d)
    (q, k, vt, gq, gk, gv, gf, gb, go, ckv, kr_t) = _inproj(xp, mod, lambda i: 0, in_w, None, tm, seq)
    (attn_p,) = _attention(r3(q, batch, seq), r3(k, batch, seq), vt, None, seq, ATTN_CTX_SEQS)
    gla_p, sf, sb = _gla(r3(gq, batch, seq), r3(gk, batch, seq), r3(gv, batch, seq), r3(gf, batch, seq),
                         r3(gb, batch, seq), r3(go, batch, seq), None, gn, GLA_CTX_SEQS)

    xs = x_sample.reshape(dec_batch * dec_seq, d)
    tiles = dec_seq // tm
    (q, k, vt, gq, gk, gv, gf, gb, go) = _inproj(xs, mod, lambda i: 1 + i // tiles, in_w,
                                                  _rope_tables(dec_seq), tm, dec_seq)
    kc, vct = _decomp(cache_kv_latent[:, l], jnp.swapaxes(cache_k_rope[:, l], 1, 2), wk, wvt)
    attn_s, wout, wfi, wfo = _attention(r3(q, dec_batch, dec_seq), r3(k, dec_batch, dec_seq), vt, (kc, vct),
                                        ATTN_LAT_QUERIES, 1, (w_out, w_ffn_in, w_ffn_out))
    gla_s, _, _ = _gla(r3(gq, dec_batch, dec_seq), r3(gk, dec_batch, dec_seq), r3(gv, dec_batch, dec_seq),
                       r3(gf, dec_batch, dec_seq), r3(gb, dec_batch, dec_seq), r3(go, dec_batch, dec_seq),
                       (state_gla_fwd[:, l].astype(F32), state_gla_bwd[:, l].astype(F32)), gn, 1)

    flat = lambda a: a.reshape(-1, a.shape[-1])
    y_prompt, y_sample = _ffn(xp, xs, flat(attn_p), flat(attn_s), flat(gla_p), flat(gla_s), mod,
                              wout, norm_ffn[l].reshape(1, d), wfi, wfo,
                              final_norm.reshape(1, d), tm_ffn, dec_seq // tm_ffn)
    y_prompt = y_prompt.reshape(batch, seq, d)
    y_sample = y_sample.reshape(dec_batch, dec_seq, d)

    new_kv_latent = ckv.reshape(batch, 1, seq, KV_LORA)
    new_k_rope = jnp.swapaxes(kr_t, 1, 2).reshape(batch, 1, seq, MLA_ROPE)
    new_state_fwd = sf.reshape(batch, 1, GLA_HEADS, GLA_DK, GLA_DV).astype(x_prompt.dtype)
    new_state_bwd = sb.reshape(batch, 1, GLA_HEADS, GLA_DK, GLA_DV).astype(x_prompt.dtype)
    return (y_prompt, y_sample, new_kv_latent, new_k_rope, new_state_fwd, new_state_bwd)
```

```python
import functools

import numpy as np
import jax
import jax.numpy as jnp
from jax import lax
from jax.experimental import pallas as pl
from jax.experimental.pallas import tpu as pltpu

F32 = jnp.float32
BF16 = jnp.bfloat16

GRID_W = 64
MLA_HEADS = 8
MLA_NOPE = 64
MLA_ROPE = 32
MLA_QK = MLA_NOPE + MLA_ROPE
MLA_V = 64
Q_LORA = 384
KV_LORA = 256
GLA_HEADS = 4
GLA_DK = 64
GLA_DV = 128
GATE_RANK = 16
GATE_NORM = 16.0
CHUNK = 64
D_FF = 2816
ROPE_BASE = 10000.0
EPS = 1e-6
LOG2_E = 1.4426950408889634

LANES = 128
HEAD_PAD = LANES
ROPE_LANE0 = MLA_NOPE
GQK = GLA_HEADS * GLA_DK
GV = GLA_HEADS * GLA_DV
QPAD = MLA_HEADS * HEAD_PAD
VALL = MLA_HEADS * MLA_V
ONES_ROWS = 16
KEY_BLOCK = 1024

W_KR = Q_LORA + KV_LORA
W_GQ = W_KR + MLA_ROPE
W_GF = W_GQ + 2 * GQK + GV
W_GO = W_GF + 2 * GATE_RANK
W_COLS = W_GO + GV

Z_Q = 0
Z_KV = Z_Q + Q_LORA
Z_GQ = Z_KV + KV_LORA
Z_GK = Z_GQ + GQK
Z_GV = Z_GK + GQK
Z_GO = Z_GV + GV
Z_MISC = Z_GO + GV
Z_COLS = Z_MISC + LANES

FF_CHUNK = 256
N_FF_CHUNKS = D_FF // FF_CHUNK

GLA_TILE = 256
CHUNKS_PER_TILE = GLA_TILE // CHUNK
ADA_ROWS = 128
INPROJ_SUB = 512
Q_TILE = 256
ATTN_LAT_QUERIES = 512
ATTN_CTX_SEQS = 4
GLA_CTX_SEQS = 4

VMEM_LIMIT = 56 * 1024 * 1024

_NT = (((1,), (1,)), ((), ()))


def _rms(x, w):
    return x * lax.rsqrt(jnp.mean(x * x, axis=-1, keepdims=True) + EPS) * w


def _sigmoid(x):
    return 1.0 / (1.0 + jnp.exp(-x))


def _log_sigmoid(x):
    return jnp.minimum(x, 0.0) - jnp.log1p(jnp.exp(-jnp.abs(x)))


def _interleave(chains):
    pending, active = list(chains), []
    while pending or active:
        if pending:
            active.append(pending.pop(0))
        for chain in list(active):
            try:
                next(chain)
            except StopIteration:
                active.remove(chain)


def _const_spec(shape):
    nd = len(shape)
    return pl.BlockSpec(shape, lambda *_: (0,) * nd, pipeline_mode=pl.Buffered(1))


def _mod_rows(mod_ref, r):
    return [mod_ref[k, pl.ds(r, 1), :] for k in range(mod_ref.shape[0])]


def _ada_kernel(cctx_ref, c_ref, w_ref, b_ref, o_ref):
    k = pl.program_id(0)
    d = o_ref.shape[2]
    row = lax.broadcasted_iota(jnp.int32, (8, cctx_ref.shape[1]), 0)
    cond = jnp.where(row == 0, cctx_ref[...], 0.0)
    for r in range(c_ref.shape[0]):
        cond = jnp.where(row == 1 + r, c_ref[r:r + 1, :], cond)
    s = (cond * _sigmoid(cond)).astype(BF16)
    part = jnp.dot(s, w_ref[...].astype(BF16), preferred_element_type=F32)
    for j in range(o_ref.shape[0]):
        sl = slice(j * d, (j + 1) * d)

        @pl.when(k == 0)
        def _():
            o_ref[j] = part[:, sl] + b_ref[:, sl]

        @pl.when(k > 0)
        def _():
            o_ref[j] += part[:, sl]


def _ada(c_ctx, c, w_ada, b_ada):
    d = w_ada.shape[0]
    n = w_ada.shape[1]
    assert 1 + c.shape[0] <= 8
    return pl.pallas_call(
        _ada_kernel,
        out_shape=jax.ShapeDtypeStruct((n // d, 8, d), F32),
        grid=(d // ADA_ROWS,),
        in_specs=[pl.BlockSpec((1, ADA_ROWS), lambda k: (0, k)),
                  pl.BlockSpec((c.shape[0], ADA_ROWS), lambda k: (0, k)),
                  pl.BlockSpec((ADA_ROWS, n), lambda k: (k, 0)),
                  pl.BlockSpec((1, n), lambda k: (0, 0))],
        out_specs=pl.BlockSpec((n // d, 8, d), lambda k: (0, 0, 0)),
        name="ada_mod",
        compiler_params=pltpu.CompilerParams(dimension_semantics=("arbitrary",)),
    )(c_ctx.reshape(1, d), c, w_ada, b_ada.reshape(1, n))


def _prep_kernel(wint_ref, wuq_ref, wukv_ref, wgf_ref, wgb_ref, win_o, wuq_o, wk_o, wvt_o, wg_o):
    cols = wint_ref.shape[1]
    for dst, src, n in ((Z_Q, 0, W_KR), (Z_GQ, W_GQ, W_GF - W_GQ), (Z_GO, W_GO, GV)):
        win_o[:, dst:dst + n] = wint_ref[src:src + n, :].T.astype(BF16)
    z32 = jnp.zeros((32, cols), F32)
    misc_t = jnp.concatenate([wint_ref[W_GF:W_GO, :], z32, wint_ref[W_KR:W_GQ, :], z32], axis=0)
    win_o[:, Z_MISC:Z_COLS] = misc_t.T.astype(BF16)

    u = wuq_ref[...]
    zq = jnp.zeros((u.shape[0], HEAD_PAD - MLA_QK), F32)
    for hd in range(MLA_HEADS):
        blk = jnp.concatenate([u[:, hd * MLA_QK:(hd + 1) * MLA_QK], zq], axis=1)
        wuq_o[:, hd * HEAD_PAD:(hd + 1) * HEAD_PAD] = blk.astype(BF16)

    @pl.when(pl.program_id(0) == 0)
    def _():
        kv = wukv_ref[...]
        per = MLA_NOPE + MLA_V
        lane = lax.broadcasted_iota(jnp.int32, (kv.shape[0], per), 1)
        for hd in range(MLA_HEADS):
            blk = kv[:, hd * per:(hd + 1) * per]
            wk_o[:, hd * HEAD_PAD:(hd + 1) * HEAD_PAD] = jnp.where(lane < MLA_NOPE, blk, 0.0).astype(BF16)
        wv = jnp.concatenate([kv[:, hd * per + MLA_NOPE:(hd + 1) * per] for hd in range(MLA_HEADS)], axis=1)
        wvt_o[...] = wv.T.astype(BF16)

        wg_o[...] = jnp.zeros(wg_o.shape, BF16)
        wg_o[0:GATE_RANK, 0:GQK] = wgf_ref[...].astype(BF16)
        wg_o[GATE_RANK:2 * GATE_RANK, GQK:2 * GQK] = wgb_ref[...].astype(BF16)


def _prep_in_weights(w_in, w_uq, w_ukv, w_gate_f, w_gate_b):
    d = w_in.shape[1]
    steps = 4
    w_in_t = jnp.swapaxes(w_in, 1, 2)
    rb3 = lambda r, c: pl.BlockSpec((None, r // steps, c), lambda i: (0, i, 0))
    rb = lambda r, c: pl.BlockSpec((r // steps, c), lambda i: (i, 0))
    full3 = lambda shape: pl.BlockSpec((None,) + tuple(shape[1:]), lambda i: (0, 0, 0))
    full = lambda shape: pl.BlockSpec(shape, lambda i: (0, 0))
    return pl.pallas_call(
        _prep_kernel,
        out_shape=[jax.ShapeDtypeStruct((d, Z_COLS), BF16),
                   jax.ShapeDtypeStruct((Q_LORA, QPAD), BF16),
                   jax.ShapeDtypeStruct((KV_LORA, QPAD), BF16),
                   jax.ShapeDtypeStruct((VALL, KV_LORA), BF16),
                   jax.ShapeDtypeStruct((LANES, 2 * GQK), BF16)],
        grid=(steps,),
        in_specs=[pl.BlockSpec((None, W_COLS, d // steps), lambda i: (0, 0, i)),
                  rb3(Q_LORA, MLA_HEADS * MLA_QK), full3(w_ukv.shape),
                  full3(w_gate_f.shape), full3(w_gate_b.shape)],
        out_specs=[rb(d, Z_COLS), rb(Q_LORA, QPAD), full((KV_LORA, QPAD)), full((VALL, KV_LORA)),
                   full((LANES, 2 * GQK))],
        name="weight_prep",
        compiler_params=pltpu.CompilerParams(dimension_semantics=("arbitrary",)),
    )(w_in_t, w_uq, w_ukv, w_gate_f, w_gate_b)


def _inproj_kernel(*refs, latent, mod_row):
    (x_ref, mod_ref, nw_ref, win_ref, qn_ref, wuq_ref, kvn_ref, wk_ref, wvt_ref, wg_ref, bgf_ref, bgb_ref) = refs[:12]
    if latent:
        cos_ref, sa_ref, sb_ref = refs[12:15]
        outs = refs[15:]
    else:
        outs = refs[12:]
    q_ref, k_ref, vt_ref, gq_ref, gk_ref, gv_ref, gf_ref, gb_ref, go_ref = outs[:9]

    sh1, sc1 = _mod_rows(mod_ref, mod_row(pl.program_id(0)))[:2]
    scale = MLA_QK ** -0.5 * LOG2_E
    lane = lax.broadcasted_iota(jnp.int32, (INPROJ_SUB, LANES), 1)
    in_rope = (lane >= ROPE_LANE0) & (lane < ROPE_LANE0 + MLA_ROPE)

    def sub_tile(r0):
        rows = slice(r0, r0 + INPROJ_SUB)
        h = (_rms(x_ref[rows, :], nw_ref[...]) * (1.0 + sc1) + sh1).astype(BF16)
        yield
        z_all = jnp.dot(h, win_ref[...], preferred_element_type=F32)
        z = lambda lo, n: z_all[:, lo:lo + n]
        yield
        qn = _rms(z(Z_Q, Q_LORA), qn_ref[...]).astype(BF16)
        ckv = _rms(z(Z_KV, KV_LORA), kvn_ref[...])
        ckv_b = ckv.astype(BF16)
        misc = z(Z_MISC, LANES)
        yield
        q = jnp.dot(qn, wuq_ref[...], preferred_element_type=F32)
        kn = jnp.dot(ckv_b, wk_ref[...], preferred_element_type=F32)
        vt_ref[:, rows] = lax.dot_general(wvt_ref[...], ckv_b, _NT,
                                          preferred_element_type=F32).astype(BF16)
        gpre = jnp.dot(misc.astype(BF16), wg_ref[...], preferred_element_type=F32)
        yield
        if latent:
            cos, sa, sb = cos_ref[rows, :], sa_ref[rows, :], sb_ref[rows, :]

            def rope(t):
                return t * cos + pltpu.roll(t, LANES - 8, 1) * sa + pltpu.roll(t, 8, 1) * sb
        else:
            def rope(t):
                return t

        krope = rope(misc)
        for hd in range(MLA_HEADS):
            sl = slice(hd * HEAD_PAD, (hd + 1) * HEAD_PAD)
            q_ref[rows, sl] = (rope(q[:, sl]) * scale).astype(BF16)
            k_ref[rows, sl] = jnp.where(in_rope, krope, kn[:, sl]).astype(BF16)
        gq_ref[rows, :] = z(Z_GQ, GQK)
        gk_ref[rows, :] = z(Z_GK, GQK)
        gv_ref[rows, :] = z(Z_GV, GV).astype(BF16)
        go_ref[rows, :] = z(Z_GO, GV)
        gf_ref[rows, :] = _log_sigmoid(gpre[:, :GQK] + bgf_ref[...]) * (1.0 / GATE_NORM)
        gb_ref[rows, :] = _log_sigmoid(gpre[:, GQK:] + bgb_ref[...]) * (1.0 / GATE_NORM)
        if not latent:
            ckv_ref, krt_ref = outs[9:]
            ckv_ref[rows, :] = ckv
            misc_t = misc.T
            n = krt_ref.shape[2]
            for b in range(INPROJ_SUB // n):
                krt_ref[r0 // n + b] = misc_t[ROPE_LANE0:ROPE_LANE0 + MLA_ROPE, b * n:(b + 1) * n]

    _interleave([sub_tile(r0) for r0 in range(0, x_ref.shape[0], INPROJ_SUB)])


def _inproj(x2d, mod, mod_row, weights, rope_tabs, tm, seq_len):
    n_tok, d = x2d.shape
    latent = rope_tabs is not None
    tiles_per_seq = max(seq_len // tm, 1)
    nw, win, qn, wuq, kvn, wk, wvt, wg, bgf, bgb = weights
    row = lambda i: (i, 0)
    in_specs = [pl.BlockSpec((tm, d), row), _const_spec(mod.shape),
                _const_spec(nw.shape), _const_spec(win.shape), _const_spec(qn.shape),
                _const_spec(wuq.shape), _const_spec(kvn.shape), _const_spec(wk.shape),
                _const_spec(wvt.shape), _const_spec(wg.shape), _const_spec(bgf.shape),
                _const_spec(bgb.shape)]
    args = [x2d, mod, nw, win, qn, wuq, kvn, wk, wvt, wg, bgf, bgb]
    if latent:
        tab = pl.BlockSpec((tm, LANES), lambda i: (i % tiles_per_seq, 0))
        in_specs += [tab, tab, tab]
        args += list(rope_tabs)
    out_cols = [(QPAD, BF16), (QPAD, BF16), None, (GQK, F32), (GQK, F32), (GV, BF16),
                (GQK, F32), (GQK, F32), (GV, F32)]
    if not latent:
        out_cols += [(KV_LORA, F32)]
    out_shape = [jax.ShapeDtypeStruct((n_tok, oc[0]), oc[1]) if oc else
                 jax.ShapeDtypeStruct((VALL, n_tok), BF16) for oc in out_cols]
    out_specs = [pl.BlockSpec((tm, oc[0]), row) if oc else
                 pl.BlockSpec((VALL, tm), lambda i: (0, i)) for oc in out_cols]
    if not latent:
        seq = seq_len
        out_shape.append(jax.ShapeDtypeStruct((n_tok // seq, MLA_ROPE, seq), F32))
        out_specs.append(pl.BlockSpec((tm // seq, MLA_ROPE, seq), lambda i: (i, 0, 0)))
    return pl.pallas_call(
        functools.partial(_inproj_kernel, latent=latent, mod_row=mod_row),
        out_shape=out_shape,
        grid=(n_tok // tm,),
        in_specs=in_specs,
        out_specs=out_specs,
        name="inproj_lat" if latent else "inproj_ctx",
        compiler_params=pltpu.CompilerParams(dimension_semantics=("arbitrary",),
                                             vmem_limit_bytes=VMEM_LIMIT),
    )(*args)


def _decomp_kernel(ckv_ref, krt_ref, wk_ref, wvt_ref, k_ref, vt_ref):
    ckv_b = ckv_ref[...].astype(BF16)
    kn = jnp.dot(ckv_b, wk_ref[...], preferred_element_type=F32)
    n_keys = krt_ref.shape[1]
    kr = jnp.concatenate([jnp.zeros((ROPE_LANE0, n_keys), F32), krt_ref[...],
                          jnp.zeros((LANES - ROPE_LANE0 - MLA_ROPE, n_keys), F32)], axis=0).T
    lane = lax.broadcasted_iota(jnp.int32, kr.shape, 1)
    in_rope = (lane >= ROPE_LANE0) & (lane < ROPE_LANE0 + MLA_ROPE)
    for hd in range(MLA_HEADS):
        sl = slice(hd * HEAD_PAD, (hd + 1) * HEAD_PAD)
        k_ref[:, sl] = jnp.where(in_rope, kr, kn[:, sl]).astype(BF16)
    vt_ref[...] = lax.dot_general(wvt_ref[...], ckv_b, _NT, preferred_element_type=F32).astype(BF16)


def _decomp(ckv, kr_t, wk, wvt):
    b, s, _ = ckv.shape
    return pl.pallas_call(
        _decomp_kernel,
        out_shape=[jax.ShapeDtypeStruct((b, s, QPAD), BF16), jax.ShapeDtypeStruct((VALL, b * s), BF16)],
        grid=(b,),
        in_specs=[pl.BlockSpec((None, s, KV_LORA), lambda i: (i, 0, 0)),
                  pl.BlockSpec((None, MLA_ROPE, s), lambda i: (i, 0, 0)),
                  _const_spec(wk.shape), _const_spec(wvt.shape)],
        out_specs=[pl.BlockSpec((None, s, QPAD), lambda i: (i, 0, 0)),
                   pl.BlockSpec((VALL, s), lambda i: (0, i))],
        name="ctx_decompress",
        compiler_params=pltpu.CompilerParams(dimension_semantics=("arbitrary",)),
    )(ckv, kr_t, wk, wvt)


def _attn_kernel(*refs, has_ctx, n_seqs, n_side):
    n_in = 5 if has_ctx else 3
    side_in, refs = refs[n_in:n_in + n_side], refs[:n_in] + refs[n_in + n_side:]
    side_out, refs = refs[n_in + 1:n_in + 1 + n_side], refs[:n_in + 1] + refs[n_in + 1 + n_side:]
    if has_ctx:
        q_ref, kc_ref, vct_ref, k_ref, vt_ref, o_ref, st_ref, p_ref = refs
    else:
        q_ref, k_ref, vt_ref, o_ref, st_ref, p_ref = refs
    tq = Q_TILE

    for src, dst in zip(side_in, side_out):
        dst[...] = src[...].astype(BF16)

    def key_blocks(bi):
        srcs = [(kc_ref, vct_ref)] if has_ctx else []
        blocks, row0 = [], 0
        for kr, vr in srcs + [(k_ref, vt_ref)]:
            n_keys = kr.shape[1]
            size = min(KEY_BLOCK, n_keys)
            for r in range(0, n_keys, size):
                blocks.append((kr, vr, r, bi * n_keys + r, size, row0))
                row0 += size
        return blocks

    units = [(bi, slice(q0, q0 + tq), hd) for bi in range(n_seqs)
             for q0 in range(0, q_ref.shape[1], tq) for hd in range(MLA_HEADS)]
    col_max = [None] * len(units)
    pair = []
    for stage in range(len(units) + 2):
        ua, ub, uc = stage, stage - 1, stage - 2
        run_max = None
        acc = jnp.zeros((MLA_V + ONES_ROWS, tq), F32)
        for j in range(len(key_blocks(0))):
            if ua < len(units):
                bi, qrows, hd = units[ua]
                kr, _, r0, _, size, srow = key_blocks(bi)[j]
                sl = slice(hd * HEAD_PAD, (hd + 1) * HEAD_PAD)
                st = lax.dot_general(kr[bi, r0:r0 + size, sl], q_ref[bi, qrows, sl], _NT,
                                     preferred_element_type=F32)
                st_ref[ua % 2, srow:srow + size, :] = st
                blk_max = jnp.max(st.reshape(size // 8, 8, tq), axis=0)
                run_max = blk_max if run_max is None else jnp.maximum(run_max, blk_max)
            if 0 <= ub < len(units):
                _, _, _, _, size, srow = key_blocks(0)[j]
                p_ref[ub % 2, srow:srow + size, :] = jnp.exp2(
                    st_ref[ub % 2, srow:srow + size, :] - col_max[ub]).astype(BF16)
            if uc >= 0:
                bi, _, hd = units[uc]
                _, vr, _, c0, size, srow = key_blocks(bi)[j]
                v_aug = jnp.concatenate([vr[hd * MLA_V:(hd + 1) * MLA_V, c0:c0 + size],
                                         jnp.ones((ONES_ROWS, size), BF16)], axis=0)
                acc = acc + jnp.dot(v_aug, p_ref[uc % 2, srow:srow + size, :],
                                    preferred_element_type=F32)
        if ua < len(units):
            col_max[ua] = jnp.max(run_max, axis=0, keepdims=True)
        if uc >= 0:
            bi, qrows, hd = units[uc]
            pair.append(acc[:MLA_V, :] / acc[MLA_V:MLA_V + 1, :])
            if len(pair) == 2:
                o_ref[bi, qrows, (hd - 1) * MLA_V:(hd + 1) * MLA_V] = (
                    jnp.concatenate(pair, axis=0).T.astype(BF16))
                pair = []


def _attention(q, k, vt, ctx_kv, tq, n_seqs, side_weights=()):
    b, t, _ = q.shape
    steps = (b // n_seqs) * (t // tq)
    assert (n_seqs == 1 or tq == t) and tq % Q_TILE == 0
    has_ctx = ctx_kv is not None
    in_specs = [pl.BlockSpec((n_seqs, tq, QPAD), lambda i, j: (i, j, 0))]
    args = [q]
    if has_ctx:
        kc, vct = ctx_kv
        s = kc.shape[1]
        in_specs += [pl.BlockSpec((n_seqs, s, QPAD), lambda i, j: (i, 0, 0)),
                     pl.BlockSpec((VALL, n_seqs * s), lambda i, j: (0, i))]
        args += [kc, vct]
    in_specs += [pl.BlockSpec((n_seqs, t, QPAD), lambda i, j: (i, 0, 0)),
                 pl.BlockSpec((VALL, n_seqs * t), lambda i, j: (0, i))]
    args += [k, vt]
    out_shape = [jax.ShapeDtypeStruct((b, t, VALL), BF16)]
    out_specs = [pl.BlockSpec((n_seqs, tq, VALL), lambda i, j: (i, j, 0))]
    nj = t // tq
    for w in side_weights:
        _, rows, cols = w.shape
        assert rows % (16 * steps) == 0
        in_specs.append(pl.BlockSpec((None, rows // steps, cols), lambda i, j: (0, i * nj + j, 0)))
        args.append(w)
        out_shape.append(jax.ShapeDtypeStruct((rows, cols), BF16))
        out_specs.append(pl.BlockSpec((rows // steps, cols), lambda i, j: (i * nj + j, 0)))
    return pl.pallas_call(
        functools.partial(_attn_kernel, has_ctx=has_ctx, n_seqs=n_seqs, n_side=len(side_weights)),
        out_shape=out_shape,
        grid=(b // n_seqs, t // tq),
        in_specs=in_specs,
        out_specs=out_specs,
        scratch_shapes=[pltpu.VMEM((2, t + (s if has_ctx else 0), Q_TILE), F32),
                        pltpu.VMEM((2, t + (s if has_ctx else 0), Q_TILE), BF16)],
        name="mla_attn_lat" if has_ctx else "mla_attn_ctx",
        compiler_params=pltpu.CompilerParams(dimension_semantics=("arbitrary", "arbitrary"),
                                             vmem_limit_bytes=VMEM_LIMIT),
    )(*args)


def _gla_kernel(*refs, n_tiles, n_seqs, zero_init):
    gq_ref, gk_ref, gv_ref, gf_ref, gb_ref, go_ref = refs[:6]
    if zero_init:
        gn_ref, o_ref, sf_ref, sb_ref, oacc_ref, bdqk_ref, tri_ref, hm_ref = refs[6:]
    else:
        (sf0_ref, sb0_ref, gn_ref, o_ref, sf_ref, sb_ref, oacc_ref,
         bdqk_ref, tri_ref, hm_ref) = refs[6:]
    g_refs = (gf_ref, gb_ref)
    state_refs = (sf_ref, sb_ref)

    @pl.when(pl.program_id(0) == 0)
    def _():
        ri = lax.broadcasted_iota(jnp.int32, (GLA_TILE, GLA_TILE), 0)
        ci = lax.broadcasted_iota(jnp.int32, (GLA_TILE, GLA_TILE), 1)
        same_chunk = (ri // CHUNK) == (ci // CHUNK)
        bdqk_ref[...] = jnp.where(same_chunk, 1.0, 0.0).astype(BF16)
        tri_ref[0] = jnp.where(same_chunk & (ri >= ci), 1.0, 0.0)
        tri_ref[1] = jnp.where(same_chunk & (ci >= ri), 1.0, 0.0)
        hm_ref[...] = jnp.where(
            lax.broadcasted_iota(jnp.int32, (GLA_HEADS * GLA_TILE, GQK), 0) // GLA_TILE
            == lax.broadcasted_iota(jnp.int32, (GLA_HEADS * GLA_TILE, GQK), 1) // GLA_DK,
            1.0, 0.0).astype(BF16)

    row8 = lax.broadcasted_iota(jnp.int32, (8, GQK), 0)

    def tile_rows(t):
        return pl.ds(pl.multiple_of(t * GLA_TILE, GLA_TILE), GLA_TILE)

    def total_row(c, d):
        return c * CHUNK + (CHUNK - 1 if d == 0 else 0)

    def tile_dir(b, t, d):
        rows = tile_rows(t)
        g = g_refs[d][b, rows, :]
        g_hi = g.astype(BF16)
        g_lo = (g - g_hi.astype(F32)).astype(BF16)
        tri_b = tri_ref[d].astype(BF16)
        cum = (jnp.dot(tri_b, g_hi, preferred_element_type=F32)
               + jnp.dot(tri_b, g_lo, preferred_element_type=F32))
        yield
        totals = [cum[total_row(c, d):total_row(c, d) + 1, :] for c in range(CHUNKS_PER_TILE)]
        tot8 = jnp.zeros((8, GQK), F32)
        for c in range(CHUNKS_PER_TILE):
            tot8 = jnp.where(row8 == c, totals[c], tot8)
        dec_t = jnp.concatenate([jnp.exp(tot8), jnp.zeros((LANES - 8, GQK), F32)], axis=0).T
        q = gq_ref[b, rows, :] * (GLA_DK ** -0.5)
        k = gk_ref[b, rows, :]
        v = gv_ref[b, rows, :]
        tot = jnp.concatenate([jnp.broadcast_to(tc, (CHUNK, GQK)) for tc in totals], axis=0)
        qe = (q * jnp.exp(cum)).astype(BF16)
        ke = (k * jnp.exp(-cum)).astype(BF16)
        kd_t = (k * jnp.exp(tot - cum)).T.astype(BF16)
        bd_qk = bdqk_ref[...] > 0
        tri = tri_ref[d] > 0

        qm = jnp.where(hm_ref[...] > 0, jnp.tile(qe, (GLA_HEADS, 1)), 0.0)
        yield
        att = lax.dot_general(qm, ke, _NT, preferred_element_type=F32)

        yield
        intra, upd = [], []
        for hd in range(GLA_HEADS):
            vh = v[:, hd * GLA_DV:(hd + 1) * GLA_DV]
            a_h = jnp.where(tri, att[hd * GLA_TILE:(hd + 1) * GLA_TILE, :], 0.0).astype(BF16)
            intra.append(jnp.dot(a_h, vh, preferred_element_type=F32))
            kd_h = jnp.tile(kd_t[hd * GLA_DK:(hd + 1) * GLA_DK, :], (CHUNKS_PER_TILE, 1))
            upd.append(jnp.dot(jnp.where(bd_qk, kd_h, 0.0), vh, preferred_element_type=F32))

        yield
        state = [state_refs[d][b, hd] for hd in range(GLA_HEADS)]
        order = range(CHUNKS_PER_TILE) if d == 0 else range(CHUNKS_PER_TILE - 1, -1, -1)
        seen = {}
        for c in order:
            seen[c] = jnp.concatenate(state, axis=0).astype(BF16)
            decay = jnp.broadcast_to(dec_t[:, c:c + 1], (GQK, GLA_DV))
            for hd in range(GLA_HEADS):
                ks = slice(hd * GLA_DK, (hd + 1) * GLA_DK)
                state[hd] = decay[ks, :] * state[hd] + upd[hd][c * CHUNK:(c + 1) * CHUNK, :]
        for hd in range(GLA_HEADS):
            state_refs[d][b, hd] = state[hd]

        yield
        for c in range(CHUNKS_PER_TILE):
            cr = slice(c * CHUNK, (c + 1) * CHUNK)
            q_c = jnp.concatenate([qm[hd * GLA_TILE + c * CHUNK:hd * GLA_TILE + (c + 1) * CHUNK, :]
                                   for hd in range(GLA_HEADS)], axis=0)
            inter = jnp.dot(q_c, seen[c], preferred_element_type=F32)
            o = jnp.concatenate([intra[hd][cr, :] + inter[hd * CHUNK:(hd + 1) * CHUNK, :]
                                 for hd in range(GLA_HEADS)], axis=1)
            oacc_ref[d, b, pl.ds(pl.multiple_of(t * GLA_TILE + c * CHUNK, CHUNK), CHUNK), :] = o

    if zero_init:
        sf_ref[...] = jnp.zeros(sf_ref.shape, F32)
        sb_ref[...] = jnp.zeros(sb_ref.shape, F32)
    else:
        sf_ref[...] = sf0_ref[...]
        sb_ref[...] = sb0_ref[...]

    tiles_per_step = 2 if n_tiles % 2 == 0 else 1

    def main_body(i, carry):
        chains = []
        for u in range(tiles_per_step):
            t = i * tiles_per_step + u
            for b in range(n_seqs):
                chains += [tile_dir(b, t, 0), tile_dir(b, n_tiles - 1 - t, 1)]
        _interleave(chains)
        return carry

    lax.fori_loop(0, n_tiles // tiles_per_step, main_body, 0)

    gn = gn_ref[...]

    def epilogue_body(t, carry):
        rows = tile_rows(t)
        for b in range(n_seqs):
            for hd in range(GLA_HEADS):
                vs = slice(hd * GLA_DV, (hd + 1) * GLA_DV)
                o = _rms(oacc_ref[0, b, rows, vs] + oacc_ref[1, b, rows, vs], gn)
                go = go_ref[b, rows, vs]
                o_ref[b, rows, vs] = (o * (go * _sigmoid(go))).astype(BF16)
        return carry

    lax.fori_loop(0, n_tiles, epilogue_body, 0)


def _gla(gq, gk, gv, gf, gb, go, init_states, gn, n_seqs):
    b, t, _ = gq.shape
    n_tiles = t // GLA_TILE
    zero_init = init_states is None
    seq = lambda c: pl.BlockSpec((n_seqs, t, c), lambda i: (i, 0, 0))
    st = pl.BlockSpec((n_seqs, GLA_HEADS, GLA_DK, GLA_DV), lambda i: (i, 0, 0, 0))
    in_specs = [seq(GQK), seq(GQK), seq(GV), seq(GQK), seq(GQK), seq(GV)]
    args = [gq, gk, gv, gf, gb, go]
    if not zero_init:
        in_specs += [st, st]
        args += list(init_states)
    in_specs.append(_const_spec(gn.shape))
    args.append(gn)
    return pl.pallas_call(
        functools.partial(_gla_kernel, n_tiles=n_tiles, n_seqs=n_seqs, zero_init=zero_init),
        out_shape=[jax.ShapeDtypeStruct((b, t, GV), BF16),
                   jax.ShapeDtypeStruct((b, GLA_HEADS, GLA_DK, GLA_DV), F32),
                   jax.ShapeDtypeStruct((b, GLA_HEADS, GLA_DK, GLA_DV), F32)],
        grid=(b // n_seqs,),
        in_specs=in_specs,
        out_specs=[seq(GV), st, st],
        scratch_shapes=[pltpu.VMEM((2, n_seqs, t, GV), F32),
                        pltpu.VMEM((GQK, GLA_TILE), BF16),
                        pltpu.VMEM((2, GLA_TILE, GLA_TILE), F32),
                        pltpu.VMEM((GLA_HEADS * GLA_TILE, GQK), BF16)],
        name="gla_%d" % t,
        compiler_params=pltpu.CompilerParams(dimension_semantics=("arbitrary",),
                                             vmem_limit_bytes=VMEM_LIMIT),
    )(*args)


def _ffn_kernel(xp_ref, xs_ref, atp_ref, ats_ref, glp_ref, gls_ref, mod_ref, wout_hbm, nf_ref,
                wfi_hbm, wfo_hbm, fn_ref, yp_ref, ys_ref, wout_s, wfi_s, wfo_s, act_ref, sems,
                *, ctx_tiles, tiles_per_seq):
    copy_out = pltpu.make_async_copy(wout_hbm, wout_s, sems.at[0])
    copy_fo = pltpu.make_async_copy(wfo_hbm, wfo_s, sems.at[1])

    def copy_fi(k):
        cols = pl.ds(k * FF_CHUNK, FF_CHUNK)
        return pltpu.make_async_copy(wfi_hbm.at[:, cols], wfi_s.at[:, cols], sems.at[2 + k])

    def tile(x_ref, at_ref, gl_ref, y_ref, mod_row, first=False):
        if first:
            copy_out.start()
            for j in range(N_FF_CHUNKS):
                copy_fi(j).start()
                copy_fi(N_FF_CHUNKS + j).start()
            copy_fo.start()
            copy_out.wait()
        gt1, sh2, sc2, gt2 = _mod_rows(mod_ref, mod_row)[2:]
        mix = (jnp.dot(at_ref[...], wout_s[0:VALL, :], preferred_element_type=F32)
               + jnp.dot(gl_ref[...], wout_s[VALL:, :], preferred_element_type=F32))
        x1 = x_ref[...] + gt1 * mix
        h2 = (_rms(x1, nf_ref[...]) * (1.0 + sc2) + sh2).astype(BF16)
        for j in range(N_FF_CHUNKS):
            if first:
                copy_fi(j).wait()
                copy_fi(N_FF_CHUNKS + j).wait()
            cs = slice(j * FF_CHUNK, (j + 1) * FF_CHUNK)
            a = jnp.dot(h2, wfi_s[:, cs], preferred_element_type=F32)
            g = jnp.dot(h2, wfi_s[:, D_FF + j * FF_CHUNK:D_FF + (j + 1) * FF_CHUNK],
                        preferred_element_type=F32)
            act_ref[:, cs] = (a * _sigmoid(a) * g).astype(BF16)
        if first:
            copy_fo.wait()
        ff = jnp.dot(act_ref[...], wfo_s[...], preferred_element_type=F32)
        x2 = x1 + gt2 * ff
        y_ref[...] = _rms(x2, fn_ref[...])

    t = pl.program_id(0)

    @pl.when(t == 0)
    def _():
        tile(xp_ref, atp_ref, glp_ref, yp_ref, 0, first=True)

    @pl.when((t > 0) & (t < ctx_tiles))
    def _():
        tile(xp_ref, atp_ref, glp_ref, yp_ref, 0)

    @pl.when(t >= ctx_tiles)
    def _():
        tile(xs_ref, ats_ref, gls_ref, ys_ref, 1 + (t - ctx_tiles) // tiles_per_seq)


def _ffn(xp, xs, attn_p, attn_s, gla_p, gla_s, mod, wout, nf, wfi, wfo, fn, tm, tiles_per_seq):
    d = xp.shape[1]
    ctx_tiles = xp.shape[0] // tm
    lat_tiles = xs.shape[0] // tm
    assert ctx_tiles >= 1
    ctx_map = lambda s: (jnp.minimum(s, ctx_tiles - 1), 0)
    lat_map = lambda s: (jnp.maximum(s - ctx_tiles, 0), 0)
    tile = lambda c, m: pl.BlockSpec((tm, c), m)
    hbm = pl.BlockSpec(memory_space=pl.ANY)
    return pl.pallas_call(
        functools.partial(_ffn_kernel, ctx_tiles=ctx_tiles, tiles_per_seq=tiles_per_seq),
        out_shape=[jax.ShapeDtypeStruct(xp.shape, F32), jax.ShapeDtypeStruct(xs.shape, F32)],
        grid=(ctx_tiles + lat_tiles,),
        in_specs=[tile(d, ctx_map), tile(d, lat_map), tile(VALL, ctx_map), tile(VALL, lat_map),
                  tile(GV, ctx_map), tile(GV, lat_map), _const_spec(mod.shape),
                  hbm, _const_spec(nf.shape), hbm, hbm, _const_spec(fn.shape)],
        out_specs=[tile(d, ctx_map), tile(d, lat_map)],
        scratch_shapes=[pltpu.VMEM(wout.shape, BF16), pltpu.VMEM(wfi.shape, BF16),
                        pltpu.VMEM(wfo.shape, BF16), pltpu.VMEM((tm, D_FF), BF16),
                        pltpu.SemaphoreType.DMA((2 + 2 * N_FF_CHUNKS,))],
        name="out_ffn",
        compiler_params=pltpu.CompilerParams(dimension_semantics=("arbitrary",),
                                             vmem_limit_bytes=VMEM_LIMIT),
    )(xp, xs, attn_p, attn_s, gla_p, gla_s, mod, wout, nf, wfi, wfo, fn)


def _rope_tables(n_tokens):
    t = np.arange(n_tokens)
    row = (t // GRID_W).astype(np.float32)
    col = (t % GRID_W).astype(np.float32)
    half = MLA_ROPE // 2
    inv = (np.float32(ROPE_BASE) ** (-np.arange(0, half, 2, dtype=np.float32) / np.float32(half))).astype(np.float32)
    ang_r = row[:, None] * inv
    ang_c = col[:, None] * inv
    ang = np.concatenate([ang_r, ang_r, ang_c, ang_c], axis=-1).astype(np.float32)
    cos, sin = np.cos(ang), np.sin(ang)
    first = (np.arange(MLA_ROPE) % half) < (half // 2)
    cos_t = np.ones((n_tokens, LANES), np.float32)
    sa_t = np.zeros((n_tokens, LANES), np.float32)
    sb_t = np.zeros((n_tokens, LANES), np.float32)
    cos_t[:, ROPE_LANE0:ROPE_LANE0 + MLA_ROPE] = cos
    sa_t[:, ROPE_LANE0:ROPE_LANE0 + MLA_ROPE] = np.where(first, -sin, 0.0)
    sb_t[:, ROPE_LANE0:ROPE_LANE0 + MLA_ROPE] = np.where(first, 0.0, sin)
    return jnp.asarray(cos_t), jnp.asarray(sa_t), jnp.asarray(sb_t)


def kernel(x_prompt, x_sample, cache_kv_latent, cache_k_rope, state_gla_fwd, state_gla_bwd, c, c_ctx, w_ada, b_ada, norm_attn, w_in, mla_q_norm, w_uq, mla_kv_norm, w_ukv, w_gate_f, b_gate_f, w_gate_b, b_gate_b, gla_norm, w_out, norm_ffn, w_ffn_in, w_ffn_out, final_norm):
    batch, seq, d = x_prompt.shape
    dec_batch, dec_seq, _ = x_sample.shape
    assert w_ada.shape[0] == 1 and w_in.shape[-1] == W_COLS and w_ffn_in.shape[-1] == 2 * D_FF
    l = 0

    mod = _ada(c_ctx, c, w_ada[l], b_ada[l])

    win, wuq, wk, wvt, wg = _prep_in_weights(w_in, w_uq, w_ukv, w_gate_f, w_gate_b)
    in_w = (norm_attn[l].reshape(1, d), win, mla_q_norm[l].reshape(1, Q_LORA), wuq,
            mla_kv_norm[l].reshape(1, KV_LORA), wk, wvt, wg, b_gate_f, b_gate_b)
    gn = gla_norm[l].reshape(1, GLA_DV)
    tm, tm_ffn = 512, 512
    r3 = lambda a, b_, t: a.reshape(b_, t, a.shape[-1])

    xp = x_prompt.reshape(batch * seq, d)
    (q, k, vt, gq, gk, gv, gf, gb, go, ckv, kr_t) = _inproj(xp, mod, lambda i: 0, in_w, None, tm, seq)
    (attn_p,) = _attention(r3(q, batch, seq), r3(k, batch, seq), vt, None, seq, ATTN_CTX_SEQS)
    gla_p, sf, sb = _gla(r3(gq, batch, seq), r3(gk, batch, seq), r3(gv, batch, seq), r3(gf, batch, seq),
                         r3(gb, batch, seq), r3(go, batch, seq), None, gn, GLA_CTX_SEQS)

    xs = x_sample.reshape(dec_batch * dec_seq, d)
    tiles = dec_seq // tm
    (q, k, vt, gq, gk, gv, gf, gb, go) = _inproj(xs, mod, lambda i: 1 + i // tiles, in_w,
                                                  _rope_tables(dec_seq), tm, dec_seq)
    kc, vct = _decomp(cache_kv_latent[:, l], jnp.swapaxes(cache_k_rope[:, l], 1, 2), wk, wvt)
    attn_s, wout, wfi, wfo = _attention(r3(q, dec_batch, dec_seq), r3(k, dec_batch, dec_seq), vt, (kc, vct),
                                        ATTN_LAT_QUERIES, 1, (w_out, w_ffn_in, w_ffn_out))
    gla_s, _, _ = _gla(r3(gq, dec_batch, dec_seq), r3(gk, dec_batch, dec_seq), r3(gv, dec_batch, dec_seq),
                       r3(gf, dec_batch, dec_seq), r3(gb, dec_batch, dec_seq), r3(go, dec_batch, dec_seq),
                       (state_gla_fwd[:, l].astype(F32), state_gla_bwd[:, l].astype(F32)), gn, 1)

    flat = lambda a: a.reshape(-1, a.shape[-1])
    y_prompt, y_sample = _ffn(xp, xs, flat(attn_p), flat(attn_s), flat(gla_p), flat(gla_s), mod,
                              wout, norm_ffn[l].reshape(1, d), wfi, wfo,
                              final_norm.reshape(1, d), tm_ffn, dec_seq // tm_ffn)
    y_prompt = y_prompt.reshape(batch, seq, d)
    y_sample = y_sample.reshape(dec_batch, dec_seq, d)

    new_kv_latent = ckv.reshape(batch, 1, seq, KV_LORA)
    new_k_rope = jnp.swapaxes(kr_t, 1, 2).reshape(batch, 1, seq, MLA_ROPE)
    new_state_fwd = sf.reshape(batch, 1, GLA_HEADS, GLA_DK, GLA_DV).astype(x_prompt.dtype)
    new_state_bwd = sb.reshape(batch, 1, GLA_HEADS, GLA_DK, GLA_DV).astype(x_prompt.dtype)
    return (y_prompt, y_sample, new_kv_latent, new_k_rope, new_state_fwd, new_state_bwd)
```

```python
import functools

import numpy as np
import jax
import jax.numpy as jnp
from jax import lax
from jax.experimental import pallas as pl
from jax.experimental.pallas import tpu as pltpu

F32 = jnp.float32
BF16 = jnp.bfloat16

GRID_W = 64
MLA_HEADS = 8
MLA_NOPE = 64
MLA_ROPE = 32
MLA_QK = MLA_NOPE + MLA_ROPE
MLA_V = 64
Q_LORA = 384
KV_LORA = 256
GLA_HEADS = 4
GLA_DK = 64
GLA_DV = 128
GATE_RANK = 16
GATE_NORM = 16.0
CHUNK = 64
D_FF = 2816
ROPE_BASE = 10000.0
EPS = 1e-6
LOG2_E = 1.4426950408889634

LANES = 128
HEAD_PAD = LANES
ROPE_LANE0 = MLA_NOPE
GQK = GLA_HEADS * GLA_DK
GV = GLA_HEADS * GLA_DV
QPAD = MLA_HEADS * HEAD_PAD
VALL = MLA_HEADS * MLA_V
ONES_ROWS = 16
KEY_BLOCK = 1024

W_KR = Q_LORA + KV_LORA
W_GQ = W_KR + MLA_ROPE
W_GF = W_GQ + 2 * GQK + GV
W_GO = W_GF + 2 * GATE_RANK
W_COLS = W_GO + GV

Z_Q = 0
Z_KV = Z_Q + Q_LORA
Z_GQ = Z_KV + KV_LORA
Z_GK = Z_GQ + GQK
Z_GV = Z_GK + GQK
Z_GO = Z_GV + GV
Z_MISC = Z_GO + GV
Z_COLS = Z_MISC + LANES

FF_CHUNK = 256
N_FF_CHUNKS = D_FF // FF_CHUNK

GLA_TILE = 256
CHUNKS_PER_TILE = GLA_TILE // CHUNK
ADA_ROWS = 128
INPROJ_SUB = 512
Q_TILE = 256
ATTN_LAT_QUERIES = 512
ATTN_CTX_SEQS = 4
GLA_CTX_SEQS = 4

VMEM_LIMIT = 56 * 1024 * 1024

_NT = (((1,), (1,)), ((), ()))


def _rms(x, w):
    return x * lax.rsqrt(jnp.mean(x * x, axis=-1, keepdims=True) + EPS) * w


def _sigmoid(x):
    return 1.0 / (1.0 + jnp.exp(-x))


def _log_sigmoid(x):
    return jnp.minimum(x, 0.0) - jnp.log1p(jnp.exp(-jnp.abs(x)))


def _interleave(chains):
    pending, active = list(chains), []
    while pending or active:
        if pending:
            active.append(pending.pop(0))
        for chain in list(active):
            try:
                next(chain)
            except StopIteration:
                active.remove(chain)


def _const_spec(shape):
    nd = len(shape)
    return pl.BlockSpec(shape, lambda *_: (0,) * nd, pipeline_mode=pl.Buffered(1))


def _mod_rows(mod_ref, r):
    return [mod_ref[k, pl.ds(r, 1), :] for k in range(mod_ref.shape[0])]


def _ada_kernel(cctx_ref, c_ref, w_ref, b_ref, o_ref):
    k = pl.program_id(0)
    d = o_ref.shape[2]
    row = lax.broadcasted_iota(jnp.int32, (8, cctx_ref.shape[1]), 0)
    cond = jnp.where(row == 0, cctx_ref[...], 0.0)
    for r in range(c_ref.shape[0]):
        cond = jnp.where(row == 1 + r, c_ref[r:r + 1, :], cond)
    s = (cond * _sigmoid(cond)).astype(BF16)
    part = jnp.dot(s, w_ref[...].astype(BF16), preferred_element_type=F32)
    for j in range(o_ref.shape[0]):
        sl = slice(j * d, (j + 1) * d)

        @pl.when(k == 0)
        def _():
            o_ref[j] = part[:, sl] + b_ref[:, sl]

        @pl.when(k > 0)
        def _():
            o_ref[j] += part[:, sl]


def _ada(c_ctx, c, w_ada, b_ada):
    d = w_ada.shape[0]
    n = w_ada.shape[1]
    assert 1 + c.shape[0] <= 8
    return pl.pallas_call(
        _ada_kernel,
        out_shape=jax.ShapeDtypeStruct((n // d, 8, d), F32),
        grid=(d // ADA_ROWS,),
        in_specs=[pl.BlockSpec((1, ADA_ROWS), lambda k: (0, k)),
                  pl.BlockSpec((c.shape[0], ADA_ROWS), lambda k: (0, k)),
                  pl.BlockSpec((ADA_ROWS, n), lambda k: (k, 0)),
                  pl.BlockSpec((1, n), lambda k: (0, 0))],
        out_specs=pl.BlockSpec((n // d, 8, d), lambda k: (0, 0, 0)),
        name="ada_mod",
        compiler_params=pltpu.CompilerParams(dimension_semantics=("arbitrary",)),
    )(c_ctx.reshape(1, d), c, w_ada, b_ada.reshape(1, n))


def _prep_kernel(wint_ref, wuq_ref, wukv_ref, wgf_ref, wgb_ref, win_o, wuq_o, wk_o, wvt_o, wg_o):
    cols = wint_ref.shape[1]
    for dst, src, n in ((Z_Q, 0, W_KR), (Z_GQ, W_GQ, W_GF - W_GQ), (Z_GO, W_GO, GV)):
        win_o[:, dst:dst + n] = wint_ref[src:src + n, :].T.astype(BF16)
    z32 = jnp.zeros((32, cols), F32)
    misc_t = jnp.concatenate([wint_ref[W_GF:W_GO, :], z32, wint_ref[W_KR:W_GQ, :], z32], axis=0)
    win_o[:, Z_MISC:Z_COLS] = misc_t.T.astype(BF16)

    u = wuq_ref[...]
    zq = jnp.zeros((u.shape[0], HEAD_PAD - MLA_QK), F32)
    for hd in range(MLA_HEADS):
        blk = jnp.concatenate([u[:, hd * MLA_QK:(hd + 1) * MLA_QK], zq], axis=1)
        wuq_o[:, hd * HEAD_PAD:(hd + 1) * HEAD_PAD] = blk.astype(BF16)

    @pl.when(pl.program_id(0) == 0)
    def _():
        kv = wukv_ref[...]
        per = MLA_NOPE + MLA_V
        lane = lax.broadcasted_iota(jnp.int32, (kv.shape[0], per), 1)
        for hd in range(MLA_HEADS):
            blk = kv[:, hd * per:(hd + 1) * per]
            wk_o[:, hd * HEAD_PAD:(hd + 1) * HEAD_PAD] = jnp.where(lane < MLA_NOPE, blk, 0.0).astype(BF16)
        wv = jnp.concatenate([kv[:, hd * per + MLA_NOPE:(hd + 1) * per] for hd in range(MLA_HEADS)], axis=1)
        wvt_o[...] = wv.T.astype(BF16)

        wg_o[...] = jnp.zeros(wg_o.shape, BF16)
        wg_o[0:GATE_RANK, 0:GQK] = wgf_ref[...].astype(BF16)
        wg_o[GATE_RANK:2 * GATE_RANK, GQK:2 * GQK] = wgb_ref[...].astype(BF16)


def _prep_in_weights(w_in, w_uq, w_ukv, w_gate_f, w_gate_b):
    d = w_in.shape[1]
    steps = 4
    w_in_t = jnp.swapaxes(w_in, 1, 2)
    rb3 = lambda r, c: pl.BlockSpec((None, r // steps, c), lambda i: (0, i, 0))
    rb = lambda r, c: pl.BlockSpec((r // steps, c), lambda i: (i, 0))
    full3 = lambda shape: pl.BlockSpec((None,) + tuple(shape[1:]), lambda i: (0, 0, 0))
    full = lambda shape: pl.BlockSpec(shape, lambda i: (0, 0))
    return pl.pallas_call(
        _prep_kernel,
        out_shape=[jax.ShapeDtypeStruct((d, Z_COLS), BF16),
                   jax.ShapeDtypeStruct((Q_LORA, QPAD), BF16),
                   jax.ShapeDtypeStruct((KV_LORA, QPAD), BF16),
                   jax.ShapeDtypeStruct((VALL, KV_LORA), BF16),
                   jax.ShapeDtypeStruct((LANES, 2 * GQK), BF16)],
        grid=(steps,),
        in_specs=[pl.BlockSpec((None, W_COLS, d // steps), lambda i: (0, 0, i)),
                  rb3(Q_LORA, MLA_HEADS * MLA_QK), full3(w_ukv.shape),
                  full3(w_gate_f.shape), full3(w_gate_b.shape)],
        out_specs=[rb(d, Z_COLS), rb(Q_LORA, QPAD), full((KV_LORA, QPAD)), full((VALL, KV_LORA)),
                   full((LANES, 2 * GQK))],
        name="weight_prep",
        compiler_params=pltpu.CompilerParams(dimension_semantics=("arbitrary",)),
    )(w_in_t, w_uq, w_ukv, w_gate_f, w_gate_b)


def _inproj_kernel(*refs, latent, mod_row):
    (x_ref, mod_ref, nw_ref, win_ref, qn_ref, wuq_ref, kvn_ref, wk_ref, wvt_ref, wg_ref, bgf_ref, bgb_ref) = refs[:12]
    if latent:
        cos_ref, sa_ref, sb_ref = refs[12:15]
        outs = refs[15:]
    else:
        outs = refs[12:]
    q_ref, k_ref, vt_ref, gq_ref, gk_ref, gv_ref, gf_ref, gb_ref, go_ref = outs[:9]

    sh1, sc1 = _mod_rows(mod_ref, mod_row(pl.program_id(0)))[:2]
    scale = MLA_QK ** -0.5 * LOG2_E
    lane = lax.broadcasted_iota(jnp.int32, (INPROJ_SUB, LANES), 1)
    in_rope = (lane >= ROPE_LANE0) & (lane < ROPE_LANE0 + MLA_ROPE)

    def sub_tile(r0):
        rows = slice(r0, r0 + INPROJ_SUB)
        h = (_rms(x_ref[rows, :], nw_ref[...]) * (1.0 + sc1) + sh1).astype(BF16)
        yield
        z_all = jnp.dot(h, win_ref[...], preferred_element_type=F32)
        z = lambda lo, n: z_all[:, lo:lo + n]
        yield
        qn = _rms(z(Z_Q, Q_LORA), qn_ref[...]).astype(BF16)
        ckv = _rms(z(Z_KV, KV_LORA), kvn_ref[...])
        ckv_b = ckv.astype(BF16)
        misc = z(Z_MISC, LANES)
        yield
        q = jnp.dot(qn, wuq_ref[...], preferred_element_type=F32)
        kn = jnp.dot(ckv_b, wk_ref[...], preferred_element_type=F32)
        vt_ref[:, rows] = lax.dot_general(wvt_ref[...], ckv_b, _NT,
                                          preferred_element_type=F32).astype(BF16)
        gpre = jnp.dot(misc.astype(BF16), wg_ref[...], preferred_element_type=F32)
        yield
        if latent:
            cos, sa, sb = cos_ref[rows, :], sa_ref[rows, :], sb_ref[rows, :]

            def rope(t):
                return t * cos + pltpu.roll(t, LANES - 8, 1) * sa + pltpu.roll(t, 8, 1) * sb
        else:
            def rope(t):
                return t

        krope = rope(misc)
        for hd in range(MLA_HEADS):
            sl = slice(hd * HEAD_PAD, (hd + 1) * HEAD_PAD)
            q_ref[rows, sl] = (rope(q[:, sl]) * scale).astype(BF16)
            k_ref[rows, sl] = jnp.where(in_rope, krope, kn[:, sl]).astype(BF16)
        gq_ref[rows, :] = z(Z_GQ, GQK)
        gk_ref[rows, :] = z(Z_GK, GQK)
        gv_ref[rows, :] = z(Z_GV, GV).astype(BF16)
        go_ref[rows, :] = z(Z_GO, GV)
        gf_ref[rows, :] = _log_sigmoid(gpre[:, :GQK] + bgf_ref[...]) * (1.0 / GATE_NORM)
        gb_ref[rows, :] = _log_sigmoid(gpre[:, GQK:] + bgb_ref[...]) * (1.0 / GATE_NORM)
        if not latent:
            ckv_ref, krt_ref = outs[9:]
            ckv_ref[rows, :] = ckv
            misc_t = misc.T
            n = krt_ref.shape[2]
            for b in range(INPROJ_SUB // n):
                krt_ref[r0 // n + b] = misc_t[ROPE_LANE0:ROPE_LANE0 + MLA_ROPE, b * n:(b + 1) * n]

    _interleave([sub_tile(r0) for r0 in range(0, x_ref.shape[0], INPROJ_SUB)])


def _inproj(x2d, mod, mod_row, weights, rope_tabs, tm, seq_len):
    n_tok, d = x2d.shape
    latent = rope_tabs is not None
    tiles_per_seq = max(seq_len // tm, 1)
    nw, win, qn, wuq, kvn, wk, wvt, wg, bgf, bgb = weights
    row = lambda i: (i, 0)
    in_specs = [pl.BlockSpec((tm, d), row), _const_spec(mod.shape),
                _const_spec(nw.shape), _const_spec(win.shape), _const_spec(qn.shape),
                _const_spec(wuq.shape), _const_spec(kvn.shape), _const_spec(wk.shape),
                _const_spec(wvt.shape), _const_spec(wg.shape), _const_spec(bgf.shape),
                _const_spec(bgb.shape)]
    args = [x2d, mod, nw, win, qn, wuq, kvn, wk, wvt, wg, bgf, bgb]
    if latent:
        tab = pl.BlockSpec((tm, LANES), lambda i: (i % tiles_per_seq, 0))
        in_specs += [tab, tab, tab]
        args += list(rope_tabs)
    out_cols = [(QPAD, BF16), (QPAD, BF16), None, (GQK, F32), (GQK, F32), (GV, BF16),
                (GQK, F32), (GQK, F32), (GV, F32)]
    if not latent:
        out_cols += [(KV_LORA, F32)]
    out_shape = [jax.ShapeDtypeStruct((n_tok, oc[0]), oc[1]) if oc else
                 jax.ShapeDtypeStruct((VALL, n_tok), BF16) for oc in out_cols]
    out_specs = [pl.BlockSpec((tm, oc[0]), row) if oc else
                 pl.BlockSpec((VALL, tm), lambda i: (0, i)) for oc in out_cols]
    if not latent:
        seq = seq_len
        out_shape.append(jax.ShapeDtypeStruct((n_tok // seq, MLA_ROPE, seq), F32))
        out_specs.append(pl.BlockSpec((tm // seq, MLA_ROPE, seq), lambda i: (i, 0, 0)))
    return pl.pallas_call(
        functools.partial(_inproj_kernel, latent=latent, mod_row=mod_row),
        out_shape=out_shape,
        grid=(n_tok // tm,),
        in_specs=in_specs,
        out_specs=out_specs,
        name="inproj_lat" if latent else "inproj_ctx",
        compiler_params=pltpu.CompilerParams(dimension_semantics=("arbitrary",),
                                             vmem_limit_bytes=VMEM_LIMIT),
    )(*args)


def _decomp_kernel(ckv_ref, krt_ref, wk_ref, wvt_ref, k_ref, vt_ref):
    ckv_b = ckv_ref[...].astype(BF16)
    kn = jnp.dot(ckv_b, wk_ref[...], preferred_element_type=F32)
    n_keys = krt_ref.shape[1]
    kr = jnp.concatenate([jnp.zeros((ROPE_LANE0, n_keys), F32), krt_ref[...],
                          jnp.zeros((LANES - ROPE_LANE0 - MLA_ROPE, n_keys), F32)], axis=0).T
    lane = lax.broadcasted_iota(jnp.int32, kr.shape, 1)
    in_rope = (lane >= ROPE_LANE0) & (lane < ROPE_LANE0 + MLA_ROPE)
    for hd in range(MLA_HEADS):
        sl = slice(hd * HEAD_PAD, (hd + 1) * HEAD_PAD)
        k_ref[:, sl] = jnp.where(in_rope, kr, kn[:, sl]).astype(BF16)
    vt_ref[...] = lax.dot_general(wvt_ref[...], ckv_b, _NT, preferred_element_type=F32).astype(BF16)


def _decomp(ckv, kr_t, wk, wvt):
    b, s, _ = ckv.shape
    return pl.pallas_call(
        _decomp_kernel,
        out_shape=[jax.ShapeDtypeStruct((b, s, QPAD), BF16), jax.ShapeDtypeStruct((VALL, b * s), BF16)],
        grid=(b,),
        in_specs=[pl.BlockSpec((None, s, KV_LORA), lambda i: (i, 0, 0)),
                  pl.BlockSpec((None, MLA_ROPE, s), lambda i: (i, 0, 0)),
                  _const_spec(wk.shape), _const_spec(wvt.shape)],
        out_specs=[pl.BlockSpec((None, s, QPAD), lambda i: (i, 0, 0)),
                   pl.BlockSpec((VALL, s), lambda i: (0, i))],
        name="ctx_decompress",
        compiler_params=pltpu.CompilerParams(dimension_semantics=("arbitrary",)),
    )(ckv, kr_t, wk, wvt)


def _attn_kernel(*refs, has_ctx, n_seqs, n_side):
    n_in = 5 if has_ctx else 3
    side_in, refs = refs[n_in:n_in + n_side], refs[:n_in] + refs[n_in + n_side:]
    side_out, refs = refs[n_in + 1:n_in + 1 + n_side], refs[:n_in + 1] + refs[n_in + 1 + n_side:]
    if has_ctx:
        q_ref, kc_ref, vct_ref, k_ref, vt_ref, o_ref, st_ref, p_ref = refs
    else:
        q_ref, k_ref, vt_ref, o_ref, st_ref, p_ref = refs
    tq = Q_TILE

    for src, dst in zip(side_in, side_out):
        dst[...] = src[...].astype(BF16)

    def key_blocks(bi):
        srcs = [(kc_ref, vct_ref)] if has_ctx else []
        blocks, row0 = [], 0
        for kr, vr in srcs + [(k_ref, vt_ref)]:
            n_keys = kr.shape[1]
            size = min(KEY_BLOCK, n_keys)
            for r in range(0, n_keys, size):
                blocks.append((kr, vr, r, bi * n_keys + r, size, row0))
                row0 += size
        return blocks

    units = [(bi, slice(q0, q0 + tq), hd) for bi in range(n_seqs)
             for q0 in range(0, q_ref.shape[1], tq) for hd in range(MLA_HEADS)]
    col_max = [None] * len(units)
    pair = []
    for stage in range(len(units) + 2):
        ua, ub, uc = stage, stage - 1, stage - 2
        run_max = None
        acc = jnp.zeros((MLA_V + ONES_ROWS, tq), F32)
        for j in range(len(key_blocks(0))):
            if ua < len(units):
                bi, qrows, hd = units[ua]
                kr, _, r0, _, size, srow = key_blocks(bi)[j]
                sl = slice(hd * HEAD_PAD, (hd + 1) * HEAD_PAD)
                st = lax.dot_general(kr[bi, r0:r0 + size, sl], q_ref[bi, qrows, sl], _NT,
                                     preferred_element_type=F32)
                st_ref[ua % 2, srow:srow + size, :] = st
                blk_max = jnp.max(st.reshape(size // 8, 8, tq), axis=0)
                run_max = blk_max if run_max is None else jnp.maximum(run_max, blk_max)
            if 0 <= ub < len(units):
                _, _, _, _, size, srow = key_blocks(0)[j]
                p_ref[ub % 2, srow:srow + size, :] = jnp.exp2(
                    st_ref[ub % 2, srow:srow + size, :] - col_max[ub]).astype(BF16)
            if uc >= 0:
                bi, _, hd = units[uc]
                _, vr, _, c0, size, srow = key_blocks(bi)[j]
                v_aug = jnp.concatenate([vr[hd * MLA_V:(hd + 1) * MLA_V, c0:c0 + size],
                                         jnp.ones((ONES_ROWS, size), BF16)], axis=0)
                acc = acc + jnp.dot(v_aug, p_ref[uc % 2, srow:srow + size, :],
                                    preferred_element_type=F32)
        if ua < len(units):
            col_max[ua] = jnp.max(run_max, axis=0, keepdims=True)
        if uc >= 0:
            bi, qrows, hd = units[uc]
            pair.append(acc[:MLA_V, :] / acc[MLA_V:MLA_V + 1, :])
            if len(pair) == 2:
                o_ref[bi, qrows, (hd - 1) * MLA_V:(hd + 1) * MLA_V] = (
                    jnp.concatenate(pair, axis=0).T.astype(BF16))
                pair = []


def _attention(q, k, vt, ctx_kv, tq, n_seqs, side_weights=()):
    b, t, _ = q.shape
    steps = (b // n_seqs) * (t // tq)
    assert (n_seqs == 1 or tq == t) and tq % Q_TILE == 0
    has_ctx = ctx_kv is not None
    in_specs = [pl.BlockSpec((n_seqs, tq, QPAD), lambda i, j: (i, j, 0))]
    args = [q]
    if has_ctx:
        kc, vct = ctx_kv
        s = kc.shape[1]
        in_specs += [pl.BlockSpec((n_seqs, s, QPAD), lambda i, j: (i, 0, 0)),
                     pl.BlockSpec((VALL, n_seqs * s), lambda i, j: (0, i))]
        args += [kc, vct]
    in_specs += [pl.BlockSpec((n_seqs, t, QPAD), lambda i, j: (i, 0, 0)),
                 pl.BlockSpec((VALL, n_seqs * t), lambda i, j: (0, i))]
    args += [k, vt]
    out_shape = [jax.ShapeDtypeStruct((b, t, VALL), BF16)]
    out_specs = [pl.BlockSpec((n_seqs, tq, VALL), lambda i, j: (i, j, 0))]
    nj = t // tq
    for w in side_weights:
        _, rows, cols = w.shape
        assert rows % (16 * steps) == 0
        in_specs.append(pl.BlockSpec((None, rows // steps, cols), lambda i, j: (0, i * nj + j, 0)))
        args.append(w)
        out_shape.append(jax.ShapeDtypeStruct((rows, cols), BF16))
        out_specs.append(pl.BlockSpec((rows // steps, cols), lambda i, j: (i * nj + j, 0)))
    return pl.pallas_call(
        functools.partial(_attn_kernel, has_ctx=has_ctx, n_seqs=n_seqs, n_side=len(side_weights)),
        out_shape=out_shape,
        grid=(b // n_seqs, t // tq),
        in_specs=in_specs,
        out_specs=out_specs,
        scratch_shapes=[pltpu.VMEM((2, t + (s if has_ctx else 0), Q_TILE), F32),
                        pltpu.VMEM((2, t + (s if has_ctx else 0), Q_TILE), BF16)],
        name="mla_attn_lat" if has_ctx else "mla_attn_ctx",
        compiler_params=pltpu.CompilerParams(dimension_semantics=("arbitrary", "arbitrary"),
                                             vmem_limit_bytes=VMEM_LIMIT),
    )(*args)


def _gla_kernel(*refs, n_tiles, n_seqs, zero_init):
    gq_ref, gk_ref, gv_ref, gf_ref, gb_ref, go_ref = refs[:6]
    if zero_init:
        gn_ref, o_ref, sf_ref, sb_ref, oacc_ref, bdqk_ref, tri_ref, hm_ref = refs[6:]
    else:
        (sf0_ref, sb0_ref, gn_ref, o_ref, sf_ref, sb_ref, oacc_ref,
         bdqk_ref, tri_ref, hm_ref) = refs[6:]
    g_refs = (gf_ref, gb_ref)
    state_refs = (sf_ref, sb_ref)

    @pl.when(pl.program_id(0) == 0)
    def _():
        ri = lax.broadcasted_iota(jnp.int32, (GLA_TILE, GLA_TILE), 0)
        ci = lax.broadcasted_iota(jnp.int32, (GLA_TILE, GLA_TILE), 1)
        same_chunk = (ri // CHUNK) == (ci // CHUNK)
        bdqk_ref[...] = jnp.where(same_chunk, 1.0, 0.0).astype(BF16)
        tri_ref[0] = jnp.where(same_chunk & (ri >= ci), 1.0, 0.0)
        tri_ref[1] = jnp.where(same_chunk & (ci >= ri), 1.0, 0.0)
        hm_ref[...] = jnp.where(
            lax.broadcasted_iota(jnp.int32, (GLA_HEADS * GLA_TILE, GQK), 0) // GLA_TILE
            == lax.broadcasted_iota(jnp.int32, (GLA_HEADS * GLA_TILE, GQK), 1) // GLA_DK,
            1.0, 0.0).astype(BF16)

    row8 = lax.broadcasted_iota(jnp.int32, (8, GQK), 0)

    def tile_rows(t):
        return pl.ds(pl.multiple_of(t * GLA_TILE, GLA_TILE), GLA_TILE)

    def total_row(c, d):
        return c * CHUNK + (CHUNK - 1 if d == 0 else 0)

    def tile_dir(b, t, d):
        rows = tile_rows(t)
        g = g_refs[d][b, rows, :]
        g_hi = g.astype(BF16)
        g_lo = (g - g_hi.astype(F32)).astype(BF16)
        tri_b = tri_ref[d].astype(BF16)
        cum = (jnp.dot(tri_b, g_hi, preferred_element_type=F32)
               + jnp.dot(tri_b, g_lo, preferred_element_type=F32))
        yield
        totals = [cum[total_row(c, d):total_row(c, d) + 1, :] for c in range(CHUNKS_PER_TILE)]
        tot8 = jnp.zeros((8, GQK), F32)
        for c in range(CHUNKS_PER_TILE):
            tot8 = jnp.where(row8 == c, totals[c], tot8)
        dec_t = jnp.concatenate([jnp.exp(tot8), jnp.zeros((LANES - 8, GQK), F32)], axis=0).T
        q = gq_ref[b, rows, :] * (GLA_DK ** -0.5)
        k = gk_ref[b, rows, :]
        v = gv_ref[b, rows, :]
        tot = jnp.concatenate([jnp.broadcast_to(tc, (CHUNK, GQK)) for tc in totals], axis=0)
        qe = (q * jnp.exp(cum)).astype(BF16)
        ke = (k * jnp.exp(-cum)).astype(BF16)
        kd_t = (k * jnp.exp(tot - cum)).T.astype(BF16)
        bd_qk = bdqk_ref[...] > 0
        tri = tri_ref[d] > 0

        qm = jnp.where(hm_ref[...] > 0, jnp.tile(qe, (GLA_HEADS, 1)), 0.0)
        yield
        att = lax.dot_general(qm, ke, _NT, preferred_element_type=F32)

        yield
        intra, upd = [], []
        for hd in range(GLA_HEADS):
            vh = v[:, hd * GLA_DV:(hd + 1) * GLA_DV]
            a_h = jnp.where(tri, att[hd * GLA_TILE:(hd + 1) * GLA_TILE, :], 0.0).astype(BF16)
            intra.append(jnp.dot(a_h, vh, preferred_element_type=F32))
            kd_h = jnp.tile(kd_t[hd * GLA_DK:(hd + 1) * GLA_DK, :], (CHUNKS_PER_TILE, 1))
            upd.append(jnp.dot(jnp.where(bd_qk, kd_h, 0.0), vh, preferred_element_type=F32))

        yield
        state = [state_refs[d][b, hd] for hd in range(GLA_HEADS)]
        order = range(CHUNKS_PER_TILE) if d == 0 else range(CHUNKS_PER_TILE - 1, -1, -1)
        seen = {}
        for c in order:
            seen[c] = jnp.concatenate(state, axis=0).astype(BF16)
            decay = jnp.broadcast_to(dec_t[:, c:c + 1], (GQK, GLA_DV))
            for hd in range(GLA_HEADS):
                ks = slice(hd * GLA_DK, (hd + 1) * GLA_DK)
                state[hd] = decay[ks, :] * state[hd] + upd[hd][c * CHUNK:(c + 1) * CHUNK, :]
        for hd in range(GLA_HEADS):
            state_refs[d][b, hd] = state[hd]

        yield
        for c in range(CHUNKS_PER_TILE):
            cr = slice(c * CHUNK, (c + 1) * CHUNK)
            q_c = jnp.concatenate([qm[hd * GLA_TILE + c * CHUNK:hd * GLA_TILE + (c + 1) * CHUNK, :]
                                   for hd in range(GLA_HEADS)], axis=0)
            inter = jnp.dot(q_c, seen[c], preferred_element_type=F32)
            o = jnp.concatenate([intra[hd][cr, :] + inter[hd * CHUNK:(hd + 1) * CHUNK, :]
                                 for hd in range(GLA_HEADS)], axis=1)
            oacc_ref[d, b, pl.ds(pl.multiple_of(t * GLA_TILE + c * CHUNK, CHUNK), CHUNK), :] = o

    if zero_init:
        sf_ref[...] = jnp.zeros(sf_ref.shape, F32)
        sb_ref[...] = jnp.zeros(sb_ref.shape, F32)
    else:
        sf_ref[...] = sf0_ref[...]
        sb_ref[...] = sb0_ref[...]

    tiles_per_step = 4 if n_tiles % 4 == 0 else (2 if n_tiles % 2 == 0 else 1)

    def main_body(i, carry):
        chains = []
        for u in range(tiles_per_step):
            t = i * tiles_per_step + u
            for b in range(n_seqs):
                chains += [tile_dir(b, t, 0), tile_dir(b, n_tiles - 1 - t, 1)]
        _interleave(chains)
        return carry

    lax.fori_loop(0, n_tiles // tiles_per_step, main_body, 0)

    gn = gn_ref[...]

    def epilogue_body(t, carry):
        rows = tile_rows(t)
        for b in range(n_seqs):
            for hd in range(GLA_HEADS):
                vs = slice(hd * GLA_DV, (hd + 1) * GLA_DV)
                o = _rms(oacc_ref[0, b, rows, vs] + oacc_ref[1, b, rows, vs], gn)
                go = go_ref[b, rows, vs]
                o_ref[b, rows, vs] = (o * (go * _sigmoid(go))).astype(BF16)
        return carry

    lax.fori_loop(0, n_tiles, epilogue_body, 0)


def _gla(gq, gk, gv, gf, gb, go, init_states, gn, n_seqs):
    b, t, _ = gq.shape
    n_tiles = t // GLA_TILE
    zero_init = init_states is None
    seq = lambda c: pl.BlockSpec((n_seqs, t, c), lambda i: (i, 0, 0))
    st = pl.BlockSpec((n_seqs, GLA_HEADS, GLA_DK, GLA_DV), lambda i: (i, 0, 0, 0))
    in_specs = [seq(GQK), seq(GQK), seq(GV), seq(GQK), seq(GQK), seq(GV)]
    args = [gq, gk, gv, gf, gb, go]
    if not zero_init:
        in_specs += [st, st]
        args += list(init_states)
    in_specs.append(_const_spec(gn.shape))
    args.append(gn)
    return pl.pallas_call(
        functools.partial(_gla_kernel, n_tiles=n_tiles, n_seqs=n_seqs, zero_init=zero_init),
        out_shape=[jax.ShapeDtypeStruct((b, t, GV), BF16),
                   jax.ShapeDtypeStruct((b, GLA_HEADS, GLA_DK, GLA_DV), F32),
                   jax.ShapeDtypeStruct((b, GLA_HEADS, GLA_DK, GLA_DV), F32)],
        grid=(b // n_seqs,),
        in_specs=in_specs,
        out_specs=[seq(GV), st, st],
        scratch_shapes=[pltpu.VMEM((2, n_seqs, t, GV), F32),
                        pltpu.VMEM((GQK, GLA_TILE), BF16),
                        pltpu.VMEM((2, GLA_TILE, GLA_TILE), F32),
                        pltpu.VMEM((GLA_HEADS * GLA_TILE, GQK), BF16)],
        name="gla_%d" % t,
        compiler_params=pltpu.CompilerParams(dimension_semantics=("arbitrary",),
                                             vmem_limit_bytes=VMEM_LIMIT),
    )(*args)


def _ffn_kernel(xp_ref, xs_ref, atp_ref, ats_ref, glp_ref, gls_ref, mod_ref, wout_ref, nf_ref,
                wfi_ref, wfo_ref, fn_ref, yp_ref, ys_ref, act_ref, *, ctx_tiles, tiles_per_seq):
    def tile(x_ref, at_ref, gl_ref, y_ref, mod_row):
        gt1, sh2, sc2, gt2 = _mod_rows(mod_ref, mod_row)[2:]
        mix = (jnp.dot(at_ref[...], wout_ref[0:VALL, :], preferred_element_type=F32)
               + jnp.dot(gl_ref[...], wout_ref[VALL:, :], preferred_element_type=F32))
        x1 = x_ref[...] + gt1 * mix
        h2 = (_rms(x1, nf_ref[...]) * (1.0 + sc2) + sh2).astype(BF16)
        for j in range(N_FF_CHUNKS):
            cs = slice(j * FF_CHUNK, (j + 1) * FF_CHUNK)
            a = jnp.dot(h2, wfi_ref[:, cs], preferred_element_type=F32)
            g = jnp.dot(h2, wfi_ref[:, D_FF + j * FF_CHUNK:D_FF + (j + 1) * FF_CHUNK],
                        preferred_element_type=F32)
            act_ref[:, cs] = (a * _sigmoid(a) * g).astype(BF16)
        ff = jnp.dot(act_ref[...], wfo_ref[...], preferred_element_type=F32)
        x2 = x1 + gt2 * ff
        y_ref[...] = _rms(x2, fn_ref[...])

    t = pl.program_id(0)

    @pl.when(t < ctx_tiles)
    def _():
        tile(xp_ref, atp_ref, glp_ref, yp_ref, 0)

    @pl.when(t >= ctx_tiles)
    def _():
        tile(xs_ref, ats_ref, gls_ref, ys_ref, 1 + (t - ctx_tiles) // tiles_per_seq)


def _ffn(xp, xs, attn_p, attn_s, gla_p, gla_s, mod, wout, nf, wfi, wfo, fn, tm, tiles_per_seq):
    d = xp.shape[1]
    ctx_tiles = xp.shape[0] // tm
    lat_tiles = xs.shape[0] // tm
    ctx_map = lambda s: (jnp.minimum(s, ctx_tiles - 1), 0)
    lat_map = lambda s: (jnp.maximum(s - ctx_tiles, 0), 0)
    tile = lambda c, m: pl.BlockSpec((tm, c), m)
    return pl.pallas_call(
        functools.partial(_ffn_kernel, ctx_tiles=ctx_tiles, tiles_per_seq=tiles_per_seq),
        out_shape=[jax.ShapeDtypeStruct(xp.shape, F32), jax.ShapeDtypeStruct(xs.shape, F32)],
        grid=(ctx_tiles + lat_tiles,),
        in_specs=[tile(d, ctx_map), tile(d, lat_map), tile(VALL, ctx_map), tile(VALL, lat_map),
                  tile(GV, ctx_map), tile(GV, lat_map), _const_spec(mod.shape),
                  _const_spec(wout.shape), _const_spec(nf.shape), _const_spec(wfi.shape),
                  _const_spec(wfo.shape), _const_spec(fn.shape)],
        out_specs=[tile(d, ctx_map), tile(d, lat_map)],
        scratch_shapes=[pltpu.VMEM((tm, D_FF), BF16)],
        name="out_ffn",
        compiler_params=pltpu.CompilerParams(dimension_semantics=("arbitrary",),
                                             vmem_limit_bytes=VMEM_LIMIT),
    )(xp, xs, attn_p, attn_s, gla_p, gla_s, mod, wout, nf, wfi, wfo, fn)


def _rope_tables(n_tokens):
    t = np.arange(n_tokens)
    row = (t // GRID_W).astype(np.float32)
    col = (t % GRID_W).astype(np.float32)
    half = MLA_ROPE // 2
    inv = (np.float32(ROPE_BASE) ** (-np.arange(0, half, 2, dtype=np.float32) / np.float32(half))).astype(np.float32)
    ang_r = row[:, None] * inv
    ang_c = col[:, None] * inv
    ang = np.concatenate([ang_r, ang_r, ang_c, ang_c], axis=-1).astype(np.float32)
    cos, sin = np.cos(ang), np.sin(ang)
    first = (np.arange(MLA_ROPE) % half) < (half // 2)
    cos_t = np.ones((n_tokens, LANES), np.float32)
    sa_t = np.zeros((n_tokens, LANES), np.float32)
    sb_t = np.zeros((n_tokens, LANES), np.float32)
    cos_t[:, ROPE_LANE0:ROPE_LANE0 + MLA_ROPE] = cos
    sa_t[:, ROPE_LANE0:ROPE_LANE0 + MLA_ROPE] = np.where(first, -sin, 0.0)
    sb_t[:, ROPE_LANE0:ROPE_LANE0 + MLA_ROPE] = np.where(first, 0.0, sin)
    return jnp.asarray(cos_t), jnp.asarray(sa_t), jnp.asarray(sb_t)


def kernel(x_prompt, x_sample, cache_kv_latent, cache_k_rope, state_gla_fwd, state_gla_bwd, c, c_ctx, w_ada, b_ada, norm_attn, w_in, mla_q_norm, w_uq, mla_kv_norm, w_ukv, w_gate_f, b_gate_f, w_gate_b, b_gate_b, gla_norm, w_out, norm_ffn, w_ffn_in, w_ffn_out, final_norm):
    batch, seq, d = x_prompt.shape
    dec_batch, dec_seq, _ = x_sample.shape
    assert w_ada.shape[0] == 1 and w_in.shape[-1] == W_COLS and w_ffn_in.shape[-1] == 2 * D_FF
    l = 0

    mod = _ada(c_ctx, c, w_ada[l], b_ada[l])

    win, wuq, wk, wvt, wg = _prep_in_weights(w_in, w_uq, w_ukv, w_gate_f, w_gate_b)
    in_w = (norm_attn[l].reshape(1, d), win, mla_q_norm[l].reshape(1, Q_LORA), wuq,
            mla_kv_norm[l].reshape(1, KV_LORA), wk, wvt, wg, b_gate_f, b_gate_b)
    gn = gla_norm[l].reshape(1, GLA_DV)
    tm, tm_ffn = 512, 512
    r3 = lambda a, b_, t: a.reshape(b_, t, a.shape[-1])

    xp = x_prompt.reshape(batch * seq, d)
    (q, k, vt, gq, gk, gv, gf, gb, go, ckv, kr_t) = _inproj(xp, mod, lambda i: 0, in_w, None, tm, seq)
    (attn_p,) = _attention(r3(q, batch, seq), r3(k, batch, seq), vt, None, seq, ATTN_CTX_SEQS)
    gla_p, sf, sb = _gla(r3(gq, batch, seq), r3(gk, batch, seq), r3(gv, batch, seq), r3(gf, batch, seq),
                         r3(gb, batch, seq), r3(go, batch, seq), None, gn, GLA_CTX_SEQS)

    xs = x_sample.reshape(dec_batch * dec_seq, d)
    tiles = dec_seq // tm
    (q, k, vt, gq, gk, gv, gf, gb, go) = _inproj(xs, mod, lambda i: 1 + i // tiles, in_w,
                                                  _rope_tables(dec_seq), tm, dec_seq)
    kc, vct = _decomp(cache_kv_latent[:, l], jnp.swapaxes(cache_k_rope[:, l], 1, 2), wk, wvt)
    attn_s, wout, wfi, wfo = _attention(r3(q, dec_batch, dec_seq), r3(k, dec_batch, dec_seq), vt, (kc, vct),
                                        ATTN_LAT_QUERIES, 1, (w_out, w_ffn_in, w_ffn_out))
    gla_s, _, _ = _gla(r3(gq, dec_batch, dec_seq), r3(gk, dec_batch, dec_seq), r3(gv, dec_batch, dec_seq),
                       r3(gf, dec_batch, dec_seq), r3(gb, dec_batch, dec_seq), r3(go, dec_batch, dec_seq),
                       (state_gla_fwd[:, l].astype(F32), state_gla_bwd[:, l].astype(F32)), gn, 1)

    flat = lambda a: a.reshape(-1, a.shape[-1])
    y_prompt, y_sample = _ffn(xp, xs, flat(attn_p), flat(attn_s), flat(gla_p), flat(gla_s), mod,
                              wout, norm_ffn[l].reshape(1, d), wfi, wfo,
                              final_norm.reshape(1, d), tm_ffn, dec_seq // tm_ffn)
    y_prompt = y_prompt.reshape(batch, seq, d)
    y_sample = y_sample.reshape(dec_batch, dec_seq, d)

    new_kv_latent = ckv.reshape(batch, 1, seq, KV_LORA)
    new_k_rope = jnp.swapaxes(kr_t, 1, 2).reshape(batch, 1, seq, MLA_ROPE)
    new_state_fwd = sf.reshape(batch, 1, GLA_HEADS, GLA_DK, GLA_DV).astype(x_prompt.dtype)
    new_state_bwd = sb.reshape(batch, 1, GLA_HEADS, GLA_DK, GLA_DV).astype(x_prompt.dtype)
    return (y_prompt, y_sample, new_kv_latent, new_k_rope, new_state_fwd, new_state_bwd)
```

```python
import functools

import numpy as np
import jax
import jax.numpy as jnp
from jax import lax
from jax.experimental import pallas as pl
from jax.experimental.pallas import tpu as pltpu

F32 = jnp.float32
BF16 = jnp.bfloat16

GRID_W = 64
MLA_HEADS = 8
MLA_NOPE = 64
MLA_ROPE = 32
MLA_QK = MLA_NOPE + MLA_ROPE
MLA_V = 64
Q_LORA = 384
KV_LORA = 256
GLA_HEADS = 4
GLA_DK = 64
GLA_DV = 128
GATE_RANK = 16
GATE_NORM = 16.0
CHUNK = 64
D_FF = 2816
ROPE_BASE = 10000.0
EPS = 1e-6
LOG2_E = 1.4426950408889634

LANES = 128
HEAD_PAD = LANES
ROPE_LANE0 = MLA_NOPE
GQK = GLA_HEADS * GLA_DK
GV = GLA_HEADS * GLA_DV
QPAD = MLA_HEADS * HEAD_PAD
VALL = MLA_HEADS * MLA_V
ONES_ROWS = 16
KEY_BLOCK = 1024

W_KR = Q_LORA + KV_LORA
W_GQ = W_KR + MLA_ROPE
W_GF = W_GQ + 2 * GQK + GV
W_GO = W_GF + 2 * GATE_RANK
W_COLS = W_GO + GV

Z_Q = 0
Z_KV = Z_Q + Q_LORA
Z_GQ = Z_KV + KV_LORA
Z_GK = Z_GQ + GQK
Z_GV = Z_GK + GQK
Z_GO = Z_GV + GV
Z_MISC = Z_GO + GV
Z_COLS = Z_MISC + LANES

FF_CHUNK = 256
N_FF_CHUNKS = D_FF // FF_CHUNK

GLA_TILE = 256
CHUNKS_PER_TILE = GLA_TILE // CHUNK
ADA_ROWS = 128
INPROJ_SUB = 512
Q_TILE = 256
ATTN_LAT_QUERIES = 512
ATTN_CTX_SEQS = 4
GLA_STATIC_TILES = 8
GLA_CTX_SEQS = 4

VMEM_LIMIT = 56 * 1024 * 1024

_NT = (((1,), (1,)), ((), ()))


def _rms(x, w):
    return x * lax.rsqrt(jnp.mean(x * x, axis=-1, keepdims=True) + EPS) * w


def _sigmoid(x):
    return 1.0 / (1.0 + jnp.exp(-x))


def _log_sigmoid(x):
    return jnp.minimum(x, 0.0) - jnp.log1p(jnp.exp(-jnp.abs(x)))


def _interleave(chains):
    pending, active = list(chains), []
    while pending or active:
        if pending:
            active.append(pending.pop(0))
        for chain in list(active):
            try:
                next(chain)
            except StopIteration:
                active.remove(chain)


def _const_spec(shape):
    nd = len(shape)
    return pl.BlockSpec(shape, lambda *_: (0,) * nd, pipeline_mode=pl.Buffered(1))


def _mod_rows(mod_ref, r):
    return [mod_ref[k, pl.ds(r, 1), :] for k in range(mod_ref.shape[0])]


def _ada_kernel(cctx_ref, c_ref, w_ref, b_ref, o_ref):
    k = pl.program_id(0)
    d = o_ref.shape[2]
    row = lax.broadcasted_iota(jnp.int32, (8, cctx_ref.shape[1]), 0)
    cond = jnp.where(row == 0, cctx_ref[...], 0.0)
    for r in range(c_ref.shape[0]):
        cond = jnp.where(row == 1 + r, c_ref[r:r + 1, :], cond)
    s = (cond * _sigmoid(cond)).astype(BF16)
    part = jnp.dot(s, w_ref[...].astype(BF16), preferred_element_type=F32)
    for j in range(o_ref.shape[0]):
        sl = slice(j * d, (j + 1) * d)

        @pl.when(k == 0)
        def _():
            o_ref[j] = part[:, sl] + b_ref[:, sl]

        @pl.when(k > 0)
        def _():
            o_ref[j] += part[:, sl]


def _ada(c_ctx, c, w_ada, b_ada):
    d = w_ada.shape[0]
    n = w_ada.shape[1]
    assert 1 + c.shape[0] <= 8
    return pl.pallas_call(
        _ada_kernel,
        out_shape=jax.ShapeDtypeStruct((n // d, 8, d), F32),
        grid=(d // ADA_ROWS,),
        in_specs=[pl.BlockSpec((1, ADA_ROWS), lambda k: (0, k)),
                  pl.BlockSpec((c.shape[0], ADA_ROWS), lambda k: (0, k)),
                  pl.BlockSpec((ADA_ROWS, n), lambda k: (k, 0)),
                  pl.BlockSpec((1, n), lambda k: (0, 0))],
        out_specs=pl.BlockSpec((n // d, 8, d), lambda k: (0, 0, 0)),
        name="ada_mod",
        compiler_params=pltpu.CompilerParams(dimension_semantics=("arbitrary",)),
    )(c_ctx.reshape(1, d), c, w_ada, b_ada.reshape(1, n))


def _prep_kernel(wint_ref, wuq_ref, wukv_ref, wgf_ref, wgb_ref, win_o, wuq_o, wk_o, wvt_o, wg_o):
    cols = wint_ref.shape[1]
    for dst, src, n in ((Z_Q, 0, W_KR), (Z_GQ, W_GQ, W_GF - W_GQ), (Z_GO, W_GO, GV)):
        win_o[:, dst:dst + n] = wint_ref[src:src + n, :].T.astype(BF16)
    z32 = jnp.zeros((32, cols), F32)
    misc_t = jnp.concatenate([wint_ref[W_GF:W_GO, :], z32, wint_ref[W_KR:W_GQ, :], z32], axis=0)
    win_o[:, Z_MISC:Z_COLS] = misc_t.T.astype(BF16)

    u = wuq_ref[...]
    zq = jnp.zeros((u.shape[0], HEAD_PAD - MLA_QK), F32)
    for hd in range(MLA_HEADS):
        blk = jnp.concatenate([u[:, hd * MLA_QK:(hd + 1) * MLA_QK], zq], axis=1)
        wuq_o[:, hd * HEAD_PAD:(hd + 1) * HEAD_PAD] = blk.astype(BF16)

    @pl.when(pl.program_id(0) == 0)
    def _():
        kv = wukv_ref[...]
        per = MLA_NOPE + MLA_V
        lane = lax.broadcasted_iota(jnp.int32, (kv.shape[0], per), 1)
        for hd in range(MLA_HEADS):
            blk = kv[:, hd * per:(hd + 1) * per]
            wk_o[:, hd * HEAD_PAD:(hd + 1) * HEAD_PAD] = jnp.where(lane < MLA_NOPE, blk, 0.0).astype(BF16)
        wv = jnp.concatenate([kv[:, hd * per + MLA_NOPE:(hd + 1) * per] for hd in range(MLA_HEADS)], axis=1)
        wvt_o[...] = wv.T.astype(BF16)

        wg_o[...] = jnp.zeros(wg_o.shape, BF16)
        wg_o[0:GATE_RANK, 0:GQK] = wgf_ref[...].astype(BF16)
        wg_o[GATE_RANK:2 * GATE_RANK, GQK:2 * GQK] = wgb_ref[...].astype(BF16)


def _prep_in_weights(w_in, w_uq, w_ukv, w_gate_f, w_gate_b):
    d = w_in.shape[1]
    steps = 4
    w_in_t = jnp.swapaxes(w_in, 1, 2)
    rb3 = lambda r, c: pl.BlockSpec((None, r // steps, c), lambda i: (0, i, 0))
    rb = lambda r, c: pl.BlockSpec((r // steps, c), lambda i: (i, 0))
    full3 = lambda shape: pl.BlockSpec((None,) + tuple(shape[1:]), lambda i: (0, 0, 0))
    full = lambda shape: pl.BlockSpec(shape, lambda i: (0, 0))
    return pl.pallas_call(
        _prep_kernel,
        out_shape=[jax.ShapeDtypeStruct((d, Z_COLS), BF16),
                   jax.ShapeDtypeStruct((Q_LORA, QPAD), BF16),
                   jax.ShapeDtypeStruct((KV_LORA, QPAD), BF16),
                   jax.ShapeDtypeStruct((VALL, KV_LORA), BF16),
                   jax.ShapeDtypeStruct((LANES, 2 * GQK), BF16)],
        grid=(steps,),
        in_specs=[pl.BlockSpec((None, W_COLS, d // steps), lambda i: (0, 0, i)),
                  rb3(Q_LORA, MLA_HEADS * MLA_QK), full3(w_ukv.shape),
                  full3(w_gate_f.shape), full3(w_gate_b.shape)],
        out_specs=[rb(d, Z_COLS), rb(Q_LORA, QPAD), full((KV_LORA, QPAD)), full((VALL, KV_LORA)),
                   full((LANES, 2 * GQK))],
        name="weight_prep",
        compiler_params=pltpu.CompilerParams(dimension_semantics=("arbitrary",)),
    )(w_in_t, w_uq, w_ukv, w_gate_f, w_gate_b)


def _inproj_kernel(*refs, latent, mod_row):
    (x_ref, mod_ref, nw_ref, win_ref, qn_ref, wuq_ref, kvn_ref, wk_ref, wvt_ref, wg_ref, bgf_ref, bgb_ref) = refs[:12]
    if latent:
        cos_ref, sa_ref, sb_ref = refs[12:15]
        outs = refs[15:]
    else:
        outs = refs[12:]
    q_ref, k_ref, vt_ref, gq_ref, gk_ref, gv_ref, gf_ref, gb_ref, go_ref = outs[:9]

    sh1, sc1 = _mod_rows(mod_ref, mod_row(pl.program_id(0)))[:2]
    scale = MLA_QK ** -0.5 * LOG2_E
    lane = lax.broadcasted_iota(jnp.int32, (INPROJ_SUB, LANES), 1)
    in_rope = (lane >= ROPE_LANE0) & (lane < ROPE_LANE0 + MLA_ROPE)

    def sub_tile(r0):
        rows = slice(r0, r0 + INPROJ_SUB)
        h = (_rms(x_ref[rows, :], nw_ref[...]) * (1.0 + sc1) + sh1).astype(BF16)
        yield
        z_all = jnp.dot(h, win_ref[...], preferred_element_type=F32)
        z = lambda lo, n: z_all[:, lo:lo + n]
        yield
        qn = _rms(z(Z_Q, Q_LORA), qn_ref[...]).astype(BF16)
        ckv = _rms(z(Z_KV, KV_LORA), kvn_ref[...])
        ckv_b = ckv.astype(BF16)
        misc = z(Z_MISC, LANES)
        yield
        q = jnp.dot(qn, wuq_ref[...], preferred_element_type=F32)
        kn = jnp.dot(ckv_b, wk_ref[...], preferred_element_type=F32)
        vt_ref[:, rows] = lax.dot_general(wvt_ref[...], ckv_b, _NT,
                                          preferred_element_type=F32).astype(BF16)
        gpre = jnp.dot(misc.astype(BF16), wg_ref[...], preferred_element_type=F32)
        yield
        if latent:
            cos, sa, sb = cos_ref[rows, :], sa_ref[rows, :], sb_ref[rows, :]

            def rope(t):
                return t * cos + pltpu.roll(t, LANES - 8, 1) * sa + pltpu.roll(t, 8, 1) * sb
        else:
            def rope(t):
                return t

        krope = rope(misc)
        for hd in range(MLA_HEADS):
            sl = slice(hd * HEAD_PAD, (hd + 1) * HEAD_PAD)
            q_ref[rows, sl] = (rope(q[:, sl]) * scale).astype(BF16)
            k_ref[rows, sl] = jnp.where(in_rope, krope, kn[:, sl]).astype(BF16)
        gq_ref[rows, :] = z(Z_GQ, GQK)
        gk_ref[rows, :] = z(Z_GK, GQK)
        gv_ref[rows, :] = z(Z_GV, GV).astype(BF16)
        go_ref[rows, :] = z(Z_GO, GV)
        gf_ref[rows, :] = _log_sigmoid(gpre[:, :GQK] + bgf_ref[...]) * (1.0 / GATE_NORM)
        gb_ref[rows, :] = _log_sigmoid(gpre[:, GQK:] + bgb_ref[...]) * (1.0 / GATE_NORM)
        if not latent:
            ckv_ref, krt_ref = outs[9:]
            ckv_ref[rows, :] = ckv
            misc_t = misc.T
            n = krt_ref.shape[2]
            for b in range(INPROJ_SUB // n):
                krt_ref[r0 // n + b] = misc_t[ROPE_LANE0:ROPE_LANE0 + MLA_ROPE, b * n:(b + 1) * n]

    _interleave([sub_tile(r0) for r0 in range(0, x_ref.shape[0], INPROJ_SUB)])


def _inproj(x2d, mod, mod_row, weights, rope_tabs, tm, seq_len):
    n_tok, d = x2d.shape
    latent = rope_tabs is not None
    tiles_per_seq = max(seq_len // tm, 1)
    nw, win, qn, wuq, kvn, wk, wvt, wg, bgf, bgb = weights
    row = lambda i: (i, 0)
    in_specs = [pl.BlockSpec((tm, d), row), _const_spec(mod.shape),
                _const_spec(nw.shape), _const_spec(win.shape), _const_spec(qn.shape),
                _const_spec(wuq.shape), _const_spec(kvn.shape), _const_spec(wk.shape),
                _const_spec(wvt.shape), _const_spec(wg.shape), _const_spec(bgf.shape),
                _const_spec(bgb.shape)]
    args = [x2d, mod, nw, win, qn, wuq, kvn, wk, wvt, wg, bgf, bgb]
    if latent:
        tab = pl.BlockSpec((tm, LANES), lambda i: (i % tiles_per_seq, 0))
        in_specs += [tab, tab, tab]
        args += list(rope_tabs)
    out_cols = [(QPAD, BF16), (QPAD, BF16), None, (GQK, F32), (GQK, F32), (GV, BF16),
                (GQK, F32), (GQK, F32), (GV, F32)]
    if not latent:
        out_cols += [(KV_LORA, F32)]
    out_shape = [jax.ShapeDtypeStruct((n_tok, oc[0]), oc[1]) if oc else
                 jax.ShapeDtypeStruct((VALL, n_tok), BF16) for oc in out_cols]
    out_specs = [pl.BlockSpec((tm, oc[0]), row) if oc else
                 pl.BlockSpec((VALL, tm), lambda i: (0, i)) for oc in out_cols]
    if not latent:
        seq = seq_len
        out_shape.append(jax.ShapeDtypeStruct((n_tok // seq, MLA_ROPE, seq), F32))
        out_specs.append(pl.BlockSpec((tm // seq, MLA_ROPE, seq), lambda i: (i, 0, 0)))
    return pl.pallas_call(
        functools.partial(_inproj_kernel, latent=latent, mod_row=mod_row),
        out_shape=out_shape,
        grid=(n_tok // tm,),
        in_specs=in_specs,
        out_specs=out_specs,
        name="inproj_lat" if latent else "inproj_ctx",
        compiler_params=pltpu.CompilerParams(dimension_semantics=("arbitrary",),
                                             vmem_limit_bytes=VMEM_LIMIT),
    )(*args)


def _decomp_kernel(ckv_ref, krt_ref, wk_ref, wvt_ref, k_ref, vt_ref):
    ckv_b = ckv_ref[...].astype(BF16)
    kn = jnp.dot(ckv_b, wk_ref[...], preferred_element_type=F32)
    n_keys = krt_ref.shape[1]
    kr = jnp.concatenate([jnp.zeros((ROPE_LANE0, n_keys), F32), krt_ref[...],
                          jnp.zeros((LANES - ROPE_LANE0 - MLA_ROPE, n_keys), F32)], axis=0).T
    lane = lax.broadcasted_iota(jnp.int32, kr.shape, 1)
    in_rope = (lane >= ROPE_LANE0) & (lane < ROPE_LANE0 + MLA_ROPE)
    for hd in range(MLA_HEADS):
        sl = slice(hd * HEAD_PAD, (hd + 1) * HEAD_PAD)
        k_ref[:, sl] = jnp.where(in_rope, kr, kn[:, sl]).astype(BF16)
    vt_ref[...] = lax.dot_general(wvt_ref[...], ckv_b, _NT, preferred_element_type=F32).astype(BF16)


def _decomp(ckv, kr_t, wk, wvt):
    b, s, _ = ckv.shape
    return pl.pallas_call(
        _decomp_kernel,
        out_shape=[jax.ShapeDtypeStruct((b, s, QPAD), BF16), jax.ShapeDtypeStruct((VALL, b * s), BF16)],
        grid=(b,),
        in_specs=[pl.BlockSpec((None, s, KV_LORA), lambda i: (i, 0, 0)),
                  pl.BlockSpec((None, MLA_ROPE, s), lambda i: (i, 0, 0)),
                  _const_spec(wk.shape), _const_spec(wvt.shape)],
        out_specs=[pl.BlockSpec((None, s, QPAD), lambda i: (i, 0, 0)),
                   pl.BlockSpec((VALL, s), lambda i: (0, i))],
        name="ctx_decompress",
        compiler_params=pltpu.CompilerParams(dimension_semantics=("arbitrary",)),
    )(ckv, kr_t, wk, wvt)


def _attn_kernel(*refs, has_ctx, n_seqs, n_side):
    n_in = 5 if has_ctx else 3
    side_in, refs = refs[n_in:n_in + n_side], refs[:n_in] + refs[n_in + n_side:]
    side_out, refs = refs[n_in + 1:n_in + 1 + n_side], refs[:n_in + 1] + refs[n_in + 1 + n_side:]
    if has_ctx:
        q_ref, kc_ref, vct_ref, k_ref, vt_ref, o_ref, st_ref, p_ref = refs
    else:
        q_ref, k_ref, vt_ref, o_ref, st_ref, p_ref = refs
    tq = Q_TILE

    for src, dst in zip(side_in, side_out):
        dst[...] = src[...].astype(BF16)

    def key_blocks(bi):
        srcs = [(kc_ref, vct_ref)] if has_ctx else []
        blocks, row0 = [], 0
        for kr, vr in srcs + [(k_ref, vt_ref)]:
            n_keys = kr.shape[1]
            size = min(KEY_BLOCK, n_keys)
            for r in range(0, n_keys, size):
                blocks.append((kr, vr, r, bi * n_keys + r, size, row0))
                row0 += size
        return blocks

    units = [(bi, slice(q0, q0 + tq), hd) for bi in range(n_seqs)
             for q0 in range(0, q_ref.shape[1], tq) for hd in range(MLA_HEADS)]
    col_max = [None] * len(units)
    pair = []
    for stage in range(len(units) + 2):
        ua, ub, uc = stage, stage - 1, stage - 2
        run_max = None
        acc = jnp.zeros((MLA_V + ONES_ROWS, tq), F32)
        for j in range(len(key_blocks(0))):
            if ua < len(units):
                bi, qrows, hd = units[ua]
                kr, _, r0, _, size, srow = key_blocks(bi)[j]
                sl = slice(hd * HEAD_PAD, (hd + 1) * HEAD_PAD)
                st = lax.dot_general(kr[bi, r0:r0 + size, sl], q_ref[bi, qrows, sl], _NT,
                                     preferred_element_type=F32)
                st_ref[ua % 2, srow:srow + size, :] = st
                blk_max = jnp.max(st.reshape(size // 8, 8, tq), axis=0)
                run_max = blk_max if run_max is None else jnp.maximum(run_max, blk_max)
            if 0 <= ub < len(units):
                _, _, _, _, size, srow = key_blocks(0)[j]
                p_ref[ub % 2, srow:srow + size, :] = jnp.exp2(
                    st_ref[ub % 2, srow:srow + size, :] - col_max[ub]).astype(BF16)
            if uc >= 0:
                bi, _, hd = units[uc]
                _, vr, _, c0, size, srow = key_blocks(bi)[j]
                v_aug = jnp.concatenate([vr[hd * MLA_V:(hd + 1) * MLA_V, c0:c0 + size],
                                         jnp.ones((ONES_ROWS, size), BF16)], axis=0)
                acc = acc + jnp.dot(v_aug, p_ref[uc % 2, srow:srow + size, :],
                                    preferred_element_type=F32)
        if ua < len(units):
            col_max[ua] = jnp.max(run_max, axis=0, keepdims=True)
        if uc >= 0:
            bi, qrows, hd = units[uc]
            pair.append(acc[:MLA_V, :] / acc[MLA_V:MLA_V + 1, :])
            if len(pair) == 2:
                o_ref[bi, qrows, (hd - 1) * MLA_V:(hd + 1) * MLA_V] = (
                    jnp.concatenate(pair, axis=0).T.astype(BF16))
                pair = []


def _attention(q, k, vt, ctx_kv, tq, n_seqs, side_weights=()):
    b, t, _ = q.shape
    steps = (b // n_seqs) * (t // tq)
    assert (n_seqs == 1 or tq == t) and tq % Q_TILE == 0
    has_ctx = ctx_kv is not None
    in_specs = [pl.BlockSpec((n_seqs, tq, QPAD), lambda i, j: (i, j, 0))]
    args = [q]
    if has_ctx:
        kc, vct = ctx_kv
        s = kc.shape[1]
        in_specs += [pl.BlockSpec((n_seqs, s, QPAD), lambda i, j: (i, 0, 0)),
                     pl.BlockSpec((VALL, n_seqs * s), lambda i, j: (0, i))]
        args += [kc, vct]
    in_specs += [pl.BlockSpec((n_seqs, t, QPAD), lambda i, j: (i, 0, 0)),
                 pl.BlockSpec((VALL, n_seqs * t), lambda i, j: (0, i))]
    args += [k, vt]
    out_shape = [jax.ShapeDtypeStruct((b, t, VALL), BF16)]
    out_specs = [pl.BlockSpec((n_seqs, tq, VALL), lambda i, j: (i, j, 0))]
    nj = t // tq
    for w in side_weights:
        _, rows, cols = w.shape
        assert rows % (16 * steps) == 0
        in_specs.append(pl.BlockSpec((None, rows // steps, cols), lambda i, j: (0, i * nj + j, 0)))
        args.append(w)
        out_shape.append(jax.ShapeDtypeStruct((rows, cols), BF16))
        out_specs.append(pl.BlockSpec((rows // steps, cols), lambda i, j: (i * nj + j, 0)))
    return pl.pallas_call(
        functools.partial(_attn_kernel, has_ctx=has_ctx, n_seqs=n_seqs, n_side=len(side_weights)),
        out_shape=out_shape,
        grid=(b // n_seqs, t // tq),
        in_specs=in_specs,
        out_specs=out_specs,
        scratch_shapes=[pltpu.VMEM((2, t + (s if has_ctx else 0), Q_TILE), F32),
                        pltpu.VMEM((2, t + (s if has_ctx else 0), Q_TILE), BF16)],
        name="mla_attn_lat" if has_ctx else "mla_attn_ctx",
        compiler_params=pltpu.CompilerParams(dimension_semantics=("arbitrary", "arbitrary"),
                                             vmem_limit_bytes=VMEM_LIMIT),
    )(*args)


def _gla_kernel(*refs, n_tiles, n_seqs, zero_init):
    gq_ref, gk_ref, gv_ref, gf_ref, gb_ref, go_ref = refs[:6]
    if zero_init:
        gn_ref, o_ref, sf_ref, sb_ref, oacc_ref, bdqk_ref, tri_ref, hm_ref = refs[6:]
    else:
        (sf0_ref, sb0_ref, gn_ref, o_ref, sf_ref, sb_ref, oacc_ref,
         bdqk_ref, tri_ref, hm_ref) = refs[6:]
    g_refs = (gf_ref, gb_ref)
    state_refs = (sf_ref, sb_ref)

    @pl.when(pl.program_id(0) == 0)
    def _():
        ri = lax.broadcasted_iota(jnp.int32, (GLA_TILE, GLA_TILE), 0)
        ci = lax.broadcasted_iota(jnp.int32, (GLA_TILE, GLA_TILE), 1)
        same_chunk = (ri // CHUNK) == (ci // CHUNK)
        bdqk_ref[...] = jnp.where(same_chunk, 1.0, 0.0).astype(BF16)
        tri_ref[0] = jnp.where(same_chunk & (ri >= ci), 1.0, 0.0)
        tri_ref[1] = jnp.where(same_chunk & (ci >= ri), 1.0, 0.0)
        hm_ref[...] = jnp.where(
            lax.broadcasted_iota(jnp.int32, (GLA_HEADS * GLA_TILE, GQK), 0) // GLA_TILE
            == lax.broadcasted_iota(jnp.int32, (GLA_HEADS * GLA_TILE, GQK), 1) // GLA_DK,
            1.0, 0.0).astype(BF16)

    row8 = lax.broadcasted_iota(jnp.int32, (8, GQK), 0)

    def row_slice(start, size):
        return pl.ds(start if isinstance(start, int) else pl.multiple_of(start, size), size)

    def tile_rows(t):
        return row_slice(t * GLA_TILE, GLA_TILE)

    finished = set()

    def total_row(c, d):
        return c * CHUNK + (CHUNK - 1 if d == 0 else 0)

    def tile_dir(b, t, d):
        rows = tile_rows(t)
        g = g_refs[d][b, rows, :]
        g_hi = g.astype(BF16)
        g_lo = (g - g_hi.astype(F32)).astype(BF16)
        tri_b = tri_ref[d].astype(BF16)
        cum = (jnp.dot(tri_b, g_hi, preferred_element_type=F32)
               + jnp.dot(tri_b, g_lo, preferred_element_type=F32))
        yield
        totals = [cum[total_row(c, d):total_row(c, d) + 1, :] for c in range(CHUNKS_PER_TILE)]
        tot8 = jnp.zeros((8, GQK), F32)
        for c in range(CHUNKS_PER_TILE):
            tot8 = jnp.where(row8 == c, totals[c], tot8)
        dec_t = jnp.concatenate([jnp.exp(tot8), jnp.zeros((LANES - 8, GQK), F32)], axis=0).T
        q = gq_ref[b, rows, :] * (GLA_DK ** -0.5)
        k = gk_ref[b, rows, :]
        v = gv_ref[b, rows, :]
        tot = jnp.concatenate([jnp.broadcast_to(tc, (CHUNK, GQK)) for tc in totals], axis=0)
        qe = (q * jnp.exp(cum)).astype(BF16)
        ke = (k * jnp.exp(-cum)).astype(BF16)
        kd_t = (k * jnp.exp(tot - cum)).T.astype(BF16)
        bd_qk = bdqk_ref[...] > 0
        tri = tri_ref[d] > 0

        qm = jnp.where(hm_ref[...] > 0, jnp.tile(qe, (GLA_HEADS, 1)), 0.0)
        yield
        att = lax.dot_general(qm, ke, _NT, preferred_element_type=F32)

        yield
        intra, upd = [], []
        for hd in range(GLA_HEADS):
            vh = v[:, hd * GLA_DV:(hd + 1) * GLA_DV]
            a_h = jnp.where(tri, att[hd * GLA_TILE:(hd + 1) * GLA_TILE, :], 0.0).astype(BF16)
            intra.append(jnp.dot(a_h, vh, preferred_element_type=F32))
            kd_h = jnp.tile(kd_t[hd * GLA_DK:(hd + 1) * GLA_DK, :], (CHUNKS_PER_TILE, 1))
            upd.append(jnp.dot(jnp.where(bd_qk, kd_h, 0.0), vh, preferred_element_type=F32))

        yield
        state = [state_refs[d][b, hd] for hd in range(GLA_HEADS)]
        order = range(CHUNKS_PER_TILE) if d == 0 else range(CHUNKS_PER_TILE - 1, -1, -1)
        seen = {}
        for c in order:
            seen[c] = jnp.concatenate(state, axis=0).astype(BF16)
            decay = jnp.broadcast_to(dec_t[:, c:c + 1], (GQK, GLA_DV))
            for hd in range(GLA_HEADS):
                ks = slice(hd * GLA_DK, (hd + 1) * GLA_DK)
                state[hd] = decay[ks, :] * state[hd] + upd[hd][c * CHUNK:(c + 1) * CHUNK, :]
        for hd in range(GLA_HEADS):
            state_refs[d][b, hd] = state[hd]

        yield
        for c in range(CHUNKS_PER_TILE):
            cr = slice(c * CHUNK, (c + 1) * CHUNK)
            q_c = jnp.concatenate([qm[hd * GLA_TILE + c * CHUNK:hd * GLA_TILE + (c + 1) * CHUNK, :]
                                   for hd in range(GLA_HEADS)], axis=0)
            inter = jnp.dot(q_c, seen[c], preferred_element_type=F32)
            o = jnp.concatenate([intra[hd][cr, :] + inter[hd * CHUNK:(hd + 1) * CHUNK, :]
                                 for hd in range(GLA_HEADS)], axis=1)
            oacc_ref[d, b, row_slice(t * GLA_TILE + c * CHUNK, CHUNK), :] = o
        if isinstance(t, int):
            finished.add((b, t, d))

    if zero_init:
        sf_ref[...] = jnp.zeros(sf_ref.shape, F32)
        sb_ref[...] = jnp.zeros(sb_ref.shape, F32)
    else:
        sf_ref[...] = sf0_ref[...]
        sb_ref[...] = sb0_ref[...]

    gn = gn_ref[...]

    def epilogue(t):
        while isinstance(t, int) and not all((b, t, d) in finished for b in range(n_seqs) for d in range(2)):
            yield
        rows = tile_rows(t)
        for b in range(n_seqs):
            for hd in range(GLA_HEADS):
                vs = slice(hd * GLA_DV, (hd + 1) * GLA_DV)
                o = _rms(oacc_ref[0, b, rows, vs] + oacc_ref[1, b, rows, vs], gn)
                go = go_ref[b, rows, vs]
                o_ref[b, rows, vs] = (o * (go * _sigmoid(go))).astype(BF16)
        return
        yield

    def main_chains(first_tile, n):
        chains = []
        for t in [first_tile + u for u in range(n)]:
            for b in range(n_seqs):
                chains += [tile_dir(b, t, 0), tile_dir(b, n_tiles - 1 - t, 1)]
        return chains

    if n_tiles <= GLA_STATIC_TILES:
        _interleave(main_chains(0, n_tiles) + [epilogue(t) for t in range(n_tiles)])
    else:
        per_step = next(c for c in (4, 2, 1) if n_tiles % c == 0)

        def main_body(i, carry):
            _interleave(main_chains(i * per_step, per_step))
            return carry

        lax.fori_loop(0, n_tiles // per_step, main_body, 0)

        def epilogue_body(t, carry):
            _interleave([epilogue(t)])
            return carry

        lax.fori_loop(0, n_tiles, epilogue_body, 0)


def _gla(gq, gk, gv, gf, gb, go, init_states, gn, n_seqs):
    b, t, _ = gq.shape
    n_tiles = t // GLA_TILE
    zero_init = init_states is None
    seq = lambda c: pl.BlockSpec((n_seqs, t, c), lambda i: (i, 0, 0))
    st = pl.BlockSpec((n_seqs, GLA_HEADS, GLA_DK, GLA_DV), lambda i: (i, 0, 0, 0))
    in_specs = [seq(GQK), seq(GQK), seq(GV), seq(GQK), seq(GQK), seq(GV)]
    args = [gq, gk, gv, gf, gb, go]
    if not zero_init:
        in_specs += [st, st]
        args += list(init_states)
    in_specs.append(_const_spec(gn.shape))
    args.append(gn)
    return pl.pallas_call(
        functools.partial(_gla_kernel, n_tiles=n_tiles, n_seqs=n_seqs, zero_init=zero_init),
        out_shape=[jax.ShapeDtypeStruct((b, t, GV), BF16),
                   jax.ShapeDtypeStruct((b, GLA_HEADS, GLA_DK, GLA_DV), F32),
                   jax.ShapeDtypeStruct((b, GLA_HEADS, GLA_DK, GLA_DV), F32)],
        grid=(b // n_seqs,),
        in_specs=in_specs,
        out_specs=[seq(GV), st, st],
        scratch_shapes=[pltpu.VMEM((2, n_seqs, t, GV), F32),
                        pltpu.VMEM((GQK, GLA_TILE), BF16),
                        pltpu.VMEM((2, GLA_TILE, GLA_TILE), F32),
                        pltpu.VMEM((GLA_HEADS * GLA_TILE, GQK), BF16)],
        name="gla_%d" % t,
        compiler_params=pltpu.CompilerParams(dimension_semantics=("arbitrary",),
                                             vmem_limit_bytes=VMEM_LIMIT),
    )(*args)


def _ffn_kernel(xp_ref, xs_ref, atp_ref, ats_ref, glp_ref, gls_ref, mod_ref, wout_ref, nf_ref,
                wfi_ref, wfo_ref, fn_ref, yp_ref, ys_ref, act_ref, *, ctx_tiles, tiles_per_seq):
    def tile(x_ref, at_ref, gl_ref, y_ref, mod_row):
        gt1, sh2, sc2, gt2 = _mod_rows(mod_ref, mod_row)[2:]
        mix = (jnp.dot(at_ref[...], wout_ref[0:VALL, :], preferred_element_type=F32)
               + jnp.dot(gl_ref[...], wout_ref[VALL:, :], preferred_element_type=F32))
        x1 = x_ref[...] + gt1 * mix
        h2 = (_rms(x1, nf_ref[...]) * (1.0 + sc2) + sh2).astype(BF16)
        for j in range(N_FF_CHUNKS):
            cs = slice(j * FF_CHUNK, (j + 1) * FF_CHUNK)
            a = jnp.dot(h2, wfi_ref[:, cs], preferred_element_type=F32)
            g = jnp.dot(h2, wfi_ref[:, D_FF + j * FF_CHUNK:D_FF + (j + 1) * FF_CHUNK],
                        preferred_element_type=F32)
            act_ref[:, cs] = (a * _sigmoid(a) * g).astype(BF16)
        ff = jnp.dot(act_ref[...], wfo_ref[...], preferred_element_type=F32)
        x2 = x1 + gt2 * ff
        y_ref[...] = _rms(x2, fn_ref[...])

    t = pl.program_id(0)

    @pl.when(t < ctx_tiles)
    def _():
        tile(xp_ref, atp_ref, glp_ref, yp_ref, 0)

    @pl.when(t >= ctx_tiles)
    def _():
        tile(xs_ref, ats_ref, gls_ref, ys_ref, 1 + (t - ctx_tiles) // tiles_per_seq)


def _ffn(xp, xs, attn_p, attn_s, gla_p, gla_s, mod, wout, nf, wfi, wfo, fn, tm, tiles_per_seq):
    d = xp.shape[1]
    ctx_tiles = xp.shape[0] // tm
    lat_tiles = xs.shape[0] // tm
    ctx_map = lambda s: (jnp.minimum(s, ctx_tiles - 1), 0)
    lat_map = lambda s: (jnp.maximum(s - ctx_tiles, 0), 0)
    tile = lambda c, m: pl.BlockSpec((tm, c), m)
    return pl.pallas_call(
        functools.partial(_ffn_kernel, ctx_tiles=ctx_tiles, tiles_per_seq=tiles_per_seq),
        out_shape=[jax.ShapeDtypeStruct(xp.shape, F32), jax.ShapeDtypeStruct(xs.shape, F32)],
        grid=(ctx_tiles + lat_tiles,),
        in_specs=[tile(d, ctx_map), tile(d, lat_map), tile(VALL, ctx_map), tile(VALL, lat_map),
                  tile(GV, ctx_map), tile(GV, lat_map), _const_spec(mod.shape),
                  _const_spec(wout.shape), _const_spec(nf.shape), _const_spec(wfi.shape),
                  _const_spec(wfo.shape), _const_spec(fn.shape)],
        out_specs=[tile(d, ctx_map), tile(d, lat_map)],
        scratch_shapes=[pltpu.VMEM((tm, D_FF), BF16)],
        name="out_ffn",
        compiler_params=pltpu.CompilerParams(dimension_semantics=("arbitrary",),
                                             vmem_limit_bytes=VMEM_LIMIT),
    )(xp, xs, attn_p, attn_s, gla_p, gla_s, mod, wout, nf, wfi, wfo, fn)


def _rope_tables(n_tokens):
    t = np.arange(n_tokens)
    row = (t // GRID_W).astype(np.float32)
    col = (t % GRID_W).astype(np.float32)
    half = MLA_ROPE // 2
    inv = (np.float32(ROPE_BASE) ** (-np.arange(0, half, 2, dtype=np.float32) / np.float32(half))).astype(np.float32)
    ang_r = row[:, None] * inv
    ang_c = col[:, None] * inv
    ang = np.concatenate([ang_r, ang_r, ang_c, ang_c], axis=-1).astype(np.float32)
    cos, sin = np.cos(ang), np.sin(ang)
    first = (np.arange(MLA_ROPE) % half) < (half // 2)
    cos_t = np.ones((n_tokens, LANES), np.float32)
    sa_t = np.zeros((n_tokens, LANES), np.float32)
    sb_t = np.zeros((n_tokens, LANES), np.float32)
    cos_t[:, ROPE_LANE0:ROPE_LANE0 + MLA_ROPE] = cos
    sa_t[:, ROPE_LANE0:ROPE_LANE0 + MLA_ROPE] = np.where(first, -sin, 0.0)
    sb_t[:, ROPE_LANE0:ROPE_LANE0 + MLA_ROPE] = np.where(first, 0.0, sin)
    return jnp.asarray(cos_t), jnp.asarray(sa_t), jnp.asarray(sb_t)


def kernel(x_prompt, x_sample, cache_kv_latent, cache_k_rope, state_gla_fwd, state_gla_bwd, c, c_ctx, w_ada, b_ada, norm_attn, w_in, mla_q_norm, w_uq, mla_kv_norm, w_ukv, w_gate_f, b_gate_f, w_gate_b, b_gate_b, gla_norm, w_out, norm_ffn, w_ffn_in, w_ffn_out, final_norm):
    batch, seq, d = x_prompt.shape
    dec_batch, dec_seq, _ = x_sample.shape
    assert w_ada.shape[0] == 1 and w_in.shape[-1] == W_COLS and w_ffn_in.shape[-1] == 2 * D_FF
    l = 0

    mod = _ada(c_ctx, c, w_ada[l], b_ada[l])

    win, wuq, wk, wvt, wg = _prep_in_weights(w_in, w_uq, w_ukv, w_gate_f, w_gate_b)
    in_w = (norm_attn[l].reshape(1, d), win, mla_q_norm[l].reshape(1, Q_LORA), wuq,
            mla_kv_norm[l].reshape(1, KV_LORA), wk, wvt, wg, b_gate_f, b_gate_b)
    gn = gla_norm[l].reshape(1, GLA_DV)
    tm, tm_ffn = 512, 512
    r3 = lambda a, b_, t: a.reshape(b_, t, a.shape[-1])

    xp = x_prompt.reshape(batch * seq, d)
    (q, k, vt, gq, gk, gv, gf, gb, go, ckv, kr_t) = _inproj(xp, mod, lambda i: 0, in_w, None, tm, seq)
    (attn_p,) = _attention(r3(q, batch, seq), r3(k, batch, seq), vt, None, seq, ATTN_CTX_SEQS)
    gla_p, sf, sb = _gla(r3(gq, batch, seq), r3(gk, batch, seq), r3(gv, batch, seq), r3(gf, batch, seq),
                         r3(gb, batch, seq), r3(go, batch, seq), None, gn, GLA_CTX_SEQS)

    xs = x_sample.reshape(dec_batch * dec_seq, d)
    tiles = dec_seq // tm
    (q, k, vt, gq, gk, gv, gf, gb, go) = _inproj(xs, mod, lambda i: 1 + i // tiles, in_w,
                                                  _rope_tables(dec_seq), tm, dec_seq)
    kc, vct = _decomp(cache_kv_latent[:, l], jnp.swapaxes(cache_k_rope[:, l], 1, 2), wk, wvt)
    attn_s, wout, wfi, wfo = _attention(r3(q, dec_batch, dec_seq), r3(k, dec_batch, dec_seq), vt, (kc, vct),
                                        ATTN_LAT_QUERIES, 1, (w_out, w_ffn_in, w_ffn_out))
    gla_s, _, _ = _gla(r3(gq, dec_batch, dec_seq), r3(gk, dec_batch, dec_seq), r3(gv, dec_batch, dec_seq),
                       r3(gf, dec_batch, dec_seq), r3(gb, dec_batch, dec_seq), r3(go, dec_batch, dec_seq),
                       (state_gla_fwd[:, l].astype(F32), state_gla_bwd[:, l].astype(F32)), gn, 1)

    flat = lambda a: a.reshape(-1, a.shape[-1])
    y_prompt, y_sample = _ffn(xp, xs, flat(attn_p), flat(attn_s), flat(gla_p), flat(gla_s), mod,
                              wout, norm_ffn[l].reshape(1, d), wfi, wfo,
                              final_norm.reshape(1, d), tm_ffn, dec_seq // tm_ffn)
    y_prompt = y_prompt.reshape(batch, seq, d)
    y_sample = y_sample.reshape(dec_batch, dec_seq, d)

    new_kv_latent = ckv.reshape(batch, 1, seq, KV_LORA)
    new_k_rope = jnp.swapaxes(kr_t, 1, 2).reshape(batch, 1, seq, MLA_ROPE)
    new_state_fwd = sf.reshape(batch, 1, GLA_HEADS, GLA_DK, GLA_DV).astype(x_prompt.dtype)
    new_state_bwd = sb.reshape(batch, 1, GLA_HEADS, GLA_DK, GLA_DV).astype(x_prompt.dtype)
    return (y_prompt, y_sample, new_kv_latent, new_k_rope, new_state_fwd, new_state_bwd)
```

```python
import functools

import numpy as np
import jax
import jax.numpy as jnp
from jax import lax
from jax.experimental import pallas as pl
from jax.experimental.pallas import tpu as pltpu

F32 = jnp.float32
BF16 = jnp.bfloat16

GRID_W = 64
MLA_HEADS = 8
MLA_NOPE = 64
MLA_ROPE = 32
MLA_QK = MLA_NOPE + MLA_ROPE
MLA_V = 64
Q_LORA = 384
KV_LORA = 256
GLA_HEADS = 4
GLA_DK = 64
GLA_DV = 128
GATE_RANK = 16
GATE_NORM = 16.0
CHUNK = 64
D_FF = 2816
ROPE_BASE = 10000.0
EPS = 1e-6
LOG2_E = 1.4426950408889634

LANES = 128
HEAD_PAD = LANES
ROPE_LANE0 = MLA_NOPE
GQK = GLA_HEADS * GLA_DK
GV = GLA_HEADS * GLA_DV
QPAD = MLA_HEADS * HEAD_PAD
VALL = MLA_HEADS * MLA_V
ONES_ROWS = 16
KEY_BLOCK = 1024

W_KR = Q_LORA + KV_LORA
W_GQ = W_KR + MLA_ROPE
W_GF = W_GQ + 2 * GQK + GV
W_GO = W_GF + 2 * GATE_RANK
W_COLS = W_GO + GV

Z_Q = 0
Z_KV = Z_Q + Q_LORA
Z_GQ = Z_KV + KV_LORA
Z_GK = Z_GQ + GQK
Z_GV = Z_GK + GQK
Z_GO = Z_GV + GV
Z_MISC = Z_GO + GV
Z_COLS = Z_MISC + LANES

FF_CHUNK = 256
N_FF_CHUNKS = D_FF // FF_CHUNK

GLA_TILE = 256
CHUNKS_PER_TILE = GLA_TILE // CHUNK
ADA_ROWS = 128
ADA_STREAMS = 3
INPROJ_SUB = 512
Q_TILE = 256
ATTN_LAT_QUERIES = 512
ATTN_CTX_SEQS = 4
GLA_STATIC_TILES = 8
GLA_CTX_SEQS = 4

VMEM_LIMIT = 56 * 1024 * 1024

_NT = (((1,), (1,)), ((), ()))


def _rms(x, w):
    return x * lax.rsqrt(jnp.mean(x * x, axis=-1, keepdims=True) + EPS) * w


def _sigmoid(x):
    return 1.0 / (1.0 + jnp.exp(-x))


def _log_sigmoid(x):
    return jnp.minimum(x, 0.0) - jnp.log1p(jnp.exp(-jnp.abs(x)))


def _interleave(chains):
    pending, active = list(chains), []
    while pending or active:
        if pending:
            active.append(pending.pop(0))
        for chain in list(active):
            try:
                next(chain)
            except StopIteration:
                active.remove(chain)


def _const_spec(shape):
    nd = len(shape)
    return pl.BlockSpec(shape, lambda *_: (0,) * nd, pipeline_mode=pl.Buffered(1))


def _mod_rows(mod_ref, r):
    return [mod_ref[k, pl.ds(r, 1), :] for k in range(mod_ref.shape[0])]


def _ada_kernel(cctx_ref, c_ref, *refs):
    w_refs, b_ref, o_ref = refs[:ADA_STREAMS], refs[ADA_STREAMS], refs[ADA_STREAMS + 1]
    k = pl.program_id(0)
    d = o_ref.shape[2]
    row = lax.broadcasted_iota(jnp.int32, (8, cctx_ref.shape[1]), 0)
    cond = jnp.where(row == 0, cctx_ref[...], 0.0)
    for r in range(c_ref.shape[0]):
        cond = jnp.where(row == 1 + r, c_ref[r:r + 1, :], cond)
    s = (cond * _sigmoid(cond)).astype(BF16)
    part = jnp.concatenate([jnp.dot(s, w_ref[...].astype(BF16), preferred_element_type=F32)
                            for w_ref in w_refs], axis=1)
    for j in range(o_ref.shape[0]):
        sl = slice(j * d, (j + 1) * d)

        @pl.when(k == 0)
        def _():
            o_ref[j] = part[:, sl] + b_ref[:, sl]

        @pl.when(k > 0)
        def _():
            o_ref[j] += part[:, sl]


def _ada(c_ctx, c, w_ada, b_ada):
    d = w_ada.shape[0]
    n = w_ada.shape[1]
    assert 1 + c.shape[0] <= 8
    return pl.pallas_call(
        _ada_kernel,
        out_shape=jax.ShapeDtypeStruct((n // d, 8, d), F32),
        grid=(d // ADA_ROWS,),
        in_specs=[pl.BlockSpec((1, ADA_ROWS), lambda k: (0, k)),
                  pl.BlockSpec((c.shape[0], ADA_ROWS), lambda k: (0, k)),
                  *[pl.BlockSpec((ADA_ROWS, n // ADA_STREAMS), lambda k, j=j: (k, j))
                    for j in range(ADA_STREAMS)],
                  pl.BlockSpec((1, n), lambda k: (0, 0))],
        out_specs=pl.BlockSpec((n // d, 8, d), lambda k: (0, 0, 0)),
        name="ada_mod",
        compiler_params=pltpu.CompilerParams(dimension_semantics=("arbitrary",)),
    )(c_ctx.reshape(1, d), c, *([w_ada] * ADA_STREAMS), b_ada.reshape(1, n))


def _prep_kernel(wint_ref, wuq_ref, wukv_ref, wgf_ref, wgb_ref, win_o, wuq_o, wk_o, wvt_o, wg_o):
    cols = wint_ref.shape[1]
    for dst, src, n in ((Z_Q, 0, W_KR), (Z_GQ, W_GQ, W_GF - W_GQ), (Z_GO, W_GO, GV)):
        win_o[:, dst:dst + n] = wint_ref[src:src + n, :].T.astype(BF16)
    z32 = jnp.zeros((32, cols), F32)
    misc_t = jnp.concatenate([wint_ref[W_GF:W_GO, :], z32, wint_ref[W_KR:W_GQ, :], z32], axis=0)
    win_o[:, Z_MISC:Z_COLS] = misc_t.T.astype(BF16)

    u = wuq_ref[...]
    zq = jnp.zeros((u.shape[0], HEAD_PAD - MLA_QK), F32)
    for hd in range(MLA_HEADS):
        blk = jnp.concatenate([u[:, hd * MLA_QK:(hd + 1) * MLA_QK], zq], axis=1)
        wuq_o[:, hd * HEAD_PAD:(hd + 1) * HEAD_PAD] = blk.astype(BF16)

    @pl.when(pl.program_id(0) == 0)
    def _():
        kv = wukv_ref[...]
        per = MLA_NOPE + MLA_V
        lane = lax.broadcasted_iota(jnp.int32, (kv.shape[0], per), 1)
        for hd in range(MLA_HEADS):
            blk = kv[:, hd * per:(hd + 1) * per]
            wk_o[:, hd * HEAD_PAD:(hd + 1) * HEAD_PAD] = jnp.where(lane < MLA_NOPE, blk, 0.0).astype(BF16)
        wv = jnp.concatenate([kv[:, hd * per + MLA_NOPE:(hd + 1) * per] for hd in range(MLA_HEADS)], axis=1)
        wvt_o[...] = wv.T.astype(BF16)

        wg_o[...] = jnp.zeros(wg_o.shape, BF16)
        wg_o[0:GATE_RANK, 0:GQK] = wgf_ref[...].astype(BF16)
        wg_o[GATE_RANK:2 * GATE_RANK, GQK:2 * GQK] = wgb_ref[...].astype(BF16)


def _prep_in_weights(w_in, w_uq, w_ukv, w_gate_f, w_gate_b):
    d = w_in.shape[1]
    steps = 4
    w_in_t = jnp.swapaxes(w_in, 1, 2)
    rb3 = lambda r, c: pl.BlockSpec((None, r // steps, c), lambda i: (0, i, 0))
    rb = lambda r, c: pl.BlockSpec((r // steps, c), lambda i: (i, 0))
    full3 = lambda shape: pl.BlockSpec((None,) + tuple(shape[1:]), lambda i: (0, 0, 0))
    full = lambda shape: pl.BlockSpec(shape, lambda i: (0, 0))
    return pl.pallas_call(
        _prep_kernel,
        out_shape=[jax.ShapeDtypeStruct((d, Z_COLS), BF16),
                   jax.ShapeDtypeStruct((Q_LORA, QPAD), BF16),
                   jax.ShapeDtypeStruct((KV_LORA, QPAD), BF16),
                   jax.ShapeDtypeStruct((VALL, KV_LORA), BF16),
                   jax.ShapeDtypeStruct((LANES, 2 * GQK), BF16)],
        grid=(steps,),
        in_specs=[pl.BlockSpec((None, W_COLS, d // steps), lambda i: (0, 0, i)),
                  rb3(Q_LORA, MLA_HEADS * MLA_QK), full3(w_ukv.shape),
                  full3(w_gate_f.shape), full3(w_gate_b.shape)],
        out_specs=[rb(d, Z_COLS), rb(Q_LORA, QPAD), full((KV_LORA, QPAD)), full((VALL, KV_LORA)),
                   full((LANES, 2 * GQK))],
        name="weight_prep",
        compiler_params=pltpu.CompilerParams(dimension_semantics=("arbitrary",)),
    )(w_in_t, w_uq, w_ukv, w_gate_f, w_gate_b)


def _inproj_kernel(*refs, latent, mod_row):
    (x_ref, mod_ref, nw_ref, win_ref, qn_ref, wuq_ref, kvn_ref, wk_ref, wvt_ref, wg_ref, bgf_ref, bgb_ref) = refs[:12]
    if latent:
        cos_ref, sa_ref, sb_ref = refs[12:15]
        outs = refs[15:]
    else:
        outs = refs[12:]
    q_ref, k_ref, vt_ref, gq_ref, gk_ref, gv_ref, gf_ref, gb_ref, go_ref = outs[:9]

    sh1, sc1 = _mod_rows(mod_ref, mod_row(pl.program_id(0)))[:2]
    scale = MLA_QK ** -0.5 * LOG2_E
    lane = lax.broadcasted_iota(jnp.int32, (INPROJ_SUB, LANES), 1)
    in_rope = (lane >= ROPE_LANE0) & (lane < ROPE_LANE0 + MLA_ROPE)

    def sub_tile(r0):
        rows = slice(r0, r0 + INPROJ_SUB)
        h = (_rms(x_ref[rows, :], nw_ref[...]) * (1.0 + sc1) + sh1).astype(BF16)
        yield
        z_all = jnp.dot(h, win_ref[...], preferred_element_type=F32)
        z = lambda lo, n: z_all[:, lo:lo + n]
        yield
        qn = _rms(z(Z_Q, Q_LORA), qn_ref[...]).astype(BF16)
        ckv = _rms(z(Z_KV, KV_LORA), kvn_ref[...])
        ckv_b = ckv.astype(BF16)
        misc = z(Z_MISC, LANES)
        yield
        q = jnp.dot(qn, wuq_ref[...], preferred_element_type=F32)
        kn = jnp.dot(ckv_b, wk_ref[...], preferred_element_type=F32)
        vt_ref[:, rows] = lax.dot_general(wvt_ref[...], ckv_b, _NT,
                                          preferred_element_type=F32).astype(BF16)
        gpre = jnp.dot(misc.astype(BF16), wg_ref[...], preferred_element_type=F32)
        yield
        if latent:
            cos, sa, sb = cos_ref[rows, :], sa_ref[rows, :], sb_ref[rows, :]

            def rope(t):
                return t * cos + pltpu.roll(t, LANES - 8, 1) * sa + pltpu.roll(t, 8, 1) * sb
        else:
            def rope(t):
                return t

        krope = rope(misc)
        for hd in range(MLA_HEADS):
            sl = slice(hd * HEAD_PAD, (hd + 1) * HEAD_PAD)
            q_ref[rows, sl] = (rope(q[:, sl]) * scale).astype(BF16)
            k_ref[rows, sl] = jnp.where(in_rope, krope, kn[:, sl]).astype(BF16)
        gq_ref[rows, :] = z(Z_GQ, GQK)
        gk_ref[rows, :] = z(Z_GK, GQK)
        gv_ref[rows, :] = z(Z_GV, GV).astype(BF16)
        go_ref[rows, :] = z(Z_GO, GV)
        gf_ref[rows, :] = _log_sigmoid(gpre[:, :GQK] + bgf_ref[...]) * (1.0 / GATE_NORM)
        gb_ref[rows, :] = _log_sigmoid(gpre[:, GQK:] + bgb_ref[...]) * (1.0 / GATE_NORM)
        if not latent:
            ckv_ref, krt_ref = outs[9:]
            ckv_ref[rows, :] = ckv
            misc_t = misc.T
            n = krt_ref.shape[2]
            for b in range(INPROJ_SUB // n):
                krt_ref[r0 // n + b] = misc_t[ROPE_LANE0:ROPE_LANE0 + MLA_ROPE, b * n:(b + 1) * n]

    _interleave([sub_tile(r0) for r0 in range(0, x_ref.shape[0], INPROJ_SUB)])


def _inproj(x2d, mod, mod_row, weights, rope_tabs, tm, seq_len):
    n_tok, d = x2d.shape
    latent = rope_tabs is not None
    tiles_per_seq = max(seq_len // tm, 1)
    nw, win, qn, wuq, kvn, wk, wvt, wg, bgf, bgb = weights
    row = lambda i: (i, 0)
    in_specs = [pl.BlockSpec((tm, d), row), _const_spec(mod.shape),
                _const_spec(nw.shape), _const_spec(win.shape), _const_spec(qn.shape),
                _const_spec(wuq.shape), _const_spec(kvn.shape), _const_spec(wk.shape),
                _const_spec(wvt.shape), _const_spec(wg.shape), _const_spec(bgf.shape),
                _const_spec(bgb.shape)]
    args = [x2d, mod, nw, win, qn, wuq, kvn, wk, wvt, wg, bgf, bgb]
    if latent:
        tab = pl.BlockSpec((tm, LANES), lambda i: (i % tiles_per_seq, 0))
        in_specs += [tab, tab, tab]
        args += list(rope_tabs)
    out_cols = [(QPAD, BF16), (QPAD, BF16), None, (GQK, F32), (GQK, F32), (GV, BF16),
                (GQK, F32), (GQK, F32), (GV, F32)]
    if not latent:
        out_cols += [(KV_LORA, F32)]
    out_shape = [jax.ShapeDtypeStruct((n_tok, oc[0]), oc[1]) if oc else
                 jax.ShapeDtypeStruct((VALL, n_tok), BF16) for oc in out_cols]
    out_specs = [pl.BlockSpec((tm, oc[0]), row) if oc else
                 pl.BlockSpec((VALL, tm), lambda i: (0, i)) for oc in out_cols]
    if not latent:
        seq = seq_len
        out_shape.append(jax.ShapeDtypeStruct((n_tok // seq, MLA_ROPE, seq), F32))
        out_specs.append(pl.BlockSpec((tm // seq, MLA_ROPE, seq), lambda i: (i, 0, 0)))
    return pl.pallas_call(
        functools.partial(_inproj_kernel, latent=latent, mod_row=mod_row),
        out_shape=out_shape,
        grid=(n_tok // tm,),
        in_specs=in_specs,
        out_specs=out_specs,
        name="inproj_lat" if latent else "inproj_ctx",
        compiler_params=pltpu.CompilerParams(dimension_semantics=("arbitrary",),
                                             vmem_limit_bytes=VMEM_LIMIT),
    )(*args)


def _decomp_kernel(ckv_ref, krt_ref, wk_ref, wvt_ref, k_ref, vt_ref):
    ckv_b = ckv_ref[...].astype(BF16)
    kn = jnp.dot(ckv_b, wk_ref[...], preferred_element_type=F32)
    n_keys = krt_ref.shape[1]
    kr = jnp.concatenate([jnp.zeros((ROPE_LANE0, n_keys), F32), krt_ref[...],
                          jnp.zeros((LANES - ROPE_LANE0 - MLA_ROPE, n_keys), F32)], axis=0).T
    lane = lax.broadcasted_iota(jnp.int32, kr.shape, 1)
    in_rope = (lane >= ROPE_LANE0) & (lane < ROPE_LANE0 + MLA_ROPE)
    for hd in range(MLA_HEADS):
        sl = slice(hd * HEAD_PAD, (hd + 1) * HEAD_PAD)
        k_ref[:, sl] = jnp.where(in_rope, kr, kn[:, sl]).astype(BF16)
    vt_ref[...] = lax.dot_general(wvt_ref[...], ckv_b, _NT, preferred_element_type=F32).astype(BF16)


def _decomp(ckv, kr_t, wk, wvt):
    b, s, _ = ckv.shape
    return pl.pallas_call(
        _decomp_kernel,
        out_shape=[jax.ShapeDtypeStruct((b, s, QPAD), BF16), jax.ShapeDtypeStruct((VALL, b * s), BF16)],
        grid=(b,),
        in_specs=[pl.BlockSpec((None, s, KV_LORA), lambda i: (i, 0, 0)),
                  pl.BlockSpec((None, MLA_ROPE, s), lambda i: (i, 0, 0)),
                  _const_spec(wk.shape), _const_spec(wvt.shape)],
        out_specs=[pl.BlockSpec((None, s, QPAD), lambda i: (i, 0, 0)),
                   pl.BlockSpec((VALL, s), lambda i: (0, i))],
        name="ctx_decompress",
        compiler_params=pltpu.CompilerParams(dimension_semantics=("arbitrary",)),
    )(ckv, kr_t, wk, wvt)


def _attn_kernel(*refs, has_ctx, n_seqs, n_side):
    n_in = 5 if has_ctx else 3
    side_in, refs = refs[n_in:n_in + n_side], refs[:n_in] + refs[n_in + n_side:]
    side_out, refs = refs[n_in + 1:n_in + 1 + n_side], refs[:n_in + 1] + refs[n_in + 1 + n_side:]
    if has_ctx:
        q_ref, kc_ref, vct_ref, k_ref, vt_ref, o_ref, st_ref, p_ref = refs
    else:
        q_ref, k_ref, vt_ref, o_ref, st_ref, p_ref = refs
    tq = Q_TILE

    for src, dst in zip(side_in, side_out):
        dst[...] = src[...].astype(BF16)

    def key_blocks(bi):
        srcs = [(kc_ref, vct_ref)] if has_ctx else []
        blocks, row0 = [], 0
        for kr, vr in srcs + [(k_ref, vt_ref)]:
            n_keys = kr.shape[1]
            size = min(KEY_BLOCK, n_keys)
            for r in range(0, n_keys, size):
                blocks.append((kr, vr, r, bi * n_keys + r, size, row0))
                row0 += size
        return blocks

    units = [(bi, slice(q0, q0 + tq), hd) for bi in range(n_seqs)
             for q0 in range(0, q_ref.shape[1], tq) for hd in range(MLA_HEADS)]
    col_max = [None] * len(units)
    pair = []
    for stage in range(len(units) + 2):
        ua, ub, uc = stage, stage - 1, stage - 2
        run_max = None
        acc = jnp.zeros((MLA_V + ONES_ROWS, tq), F32)
        for j in range(len(key_blocks(0))):
            if ua < len(units):
                bi, qrows, hd = units[ua]
                kr, _, r0, _, size, srow = key_blocks(bi)[j]
                sl = slice(hd * HEAD_PAD, (hd + 1) * HEAD_PAD)
                st = lax.dot_general(kr[bi, r0:r0 + size, sl], q_ref[bi, qrows, sl], _NT,
                                     preferred_element_type=F32)
                st_ref[ua % 2, srow:srow + size, :] = st
                blk_max = jnp.max(st.reshape(size // 8, 8, tq), axis=0)
                run_max = blk_max if run_max is None else jnp.maximum(run_max, blk_max)
            if 0 <= ub < len(units):
                _, _, _, _, size, srow = key_blocks(0)[j]
                p_ref[ub % 2, srow:srow + size, :] = jnp.exp2(
                    st_ref[ub % 2, srow:srow + size, :] - col_max[ub]).astype(BF16)
            if uc >= 0:
                bi, _, hd = units[uc]
                _, vr, _, c0, size, srow = key_blocks(bi)[j]
                v_aug = jnp.concatenate([vr[hd * MLA_V:(hd + 1) * MLA_V, c0:c0 + size],
                                         jnp.ones((ONES_ROWS, size), BF16)], axis=0)
                acc = acc + jnp.dot(v_aug, p_ref[uc % 2, srow:srow + size, :],
                                    preferred_element_type=F32)
        if ua < len(units):
            col_max[ua] = jnp.max(run_max, axis=0, keepdims=True)
        if uc >= 0:
            bi, qrows, hd = units[uc]
            pair.append(acc[:MLA_V, :] / acc[MLA_V:MLA_V + 1, :])
            if len(pair) == 2:
                o_ref[bi, qrows, (hd - 1) * MLA_V:(hd + 1) * MLA_V] = (
                    jnp.concatenate(pair, axis=0).T.astype(BF16))
                pair = []


def _attention(q, k, vt, ctx_kv, tq, n_seqs, side_weights=()):
    b, t, _ = q.shape
    steps = (b // n_seqs) * (t // tq)
    assert (n_seqs == 1 or tq == t) and tq % Q_TILE == 0
    has_ctx = ctx_kv is not None
    in_specs = [pl.BlockSpec((n_seqs, tq, QPAD), lambda i, j: (i, j, 0))]
    args = [q]
    if has_ctx:
        kc, vct = ctx_kv
        s = kc.shape[1]
        in_specs += [pl.BlockSpec((n_seqs, s, QPAD), lambda i, j: (i, 0, 0)),
                     pl.BlockSpec((VALL, n_seqs * s), lambda i, j: (0, i))]
        args += [kc, vct]
    in_specs += [pl.BlockSpec((n_seqs, t, QPAD), lambda i, j: (i, 0, 0)),
                 pl.BlockSpec((VALL, n_seqs * t), lambda i, j: (0, i))]
    args += [k, vt]
    out_shape = [jax.ShapeDtypeStruct((b, t, VALL), BF16)]
    out_specs = [pl.BlockSpec((n_seqs, tq, VALL), lambda i, j: (i, j, 0))]
    nj = t // tq
    for w in side_weights:
        _, rows, cols = w.shape
        assert rows % (16 * steps) == 0
        in_specs.append(pl.BlockSpec((None, rows // steps, cols), lambda i, j: (0, i * nj + j, 0)))
        args.append(w)
        out_shape.append(jax.ShapeDtypeStruct((rows, cols), BF16))
        out_specs.append(pl.BlockSpec((rows // steps, cols), lambda i, j: (i * nj + j, 0)))
    return pl.pallas_call(
        functools.partial(_attn_kernel, has_ctx=has_ctx, n_seqs=n_seqs, n_side=len(side_weights)),
        out_shape=out_shape,
        grid=(b // n_seqs, t // tq),
        in_specs=in_specs,
        out_specs=out_specs,
        scratch_shapes=[pltpu.VMEM((2, t + (s if has_ctx else 0), Q_TILE), F32),
                        pltpu.VMEM((2, t + (s if has_ctx else 0), Q_TILE), BF16)],
        name="mla_attn_lat" if has_ctx else "mla_attn_ctx",
        compiler_params=pltpu.CompilerParams(dimension_semantics=("arbitrary", "arbitrary"),
                                             vmem_limit_bytes=VMEM_LIMIT),
    )(*args)


def _gla_kernel(*refs, n_tiles, n_seqs, zero_init):
    gq_ref, gk_ref, gv_ref, gf_ref, gb_ref, go_ref = refs[:6]
    if zero_init:
        gn_ref, o_ref, sf_ref, sb_ref, oacc_ref, bdqk_ref, tri_ref, hm_ref = refs[6:]
    else:
        (sf0_ref, sb0_ref, gn_ref, o_ref, sf_ref, sb_ref, oacc_ref,
         bdqk_ref, tri_ref, hm_ref) = refs[6:]
    g_refs = (gf_ref, gb_ref)
    state_refs = (sf_ref, sb_ref)

    @pl.when(pl.program_id(0) == 0)
    def _():
        ri = lax.broadcasted_iota(jnp.int32, (GLA_TILE, GLA_TILE), 0)
        ci = lax.broadcasted_iota(jnp.int32, (GLA_TILE, GLA_TILE), 1)
        same_chunk = (ri // CHUNK) == (ci // CHUNK)
        bdqk_ref[...] = jnp.where(same_chunk, 1.0, 0.0).astype(BF16)
        tri_ref[0] = jnp.where(same_chunk & (ri >= ci), 1.0, 0.0)
        tri_ref[1] = jnp.where(same_chunk & (ci >= ri), 1.0, 0.0)
        hm_ref[...] = jnp.where(
            lax.broadcasted_iota(jnp.int32, (GLA_HEADS * GLA_TILE, GQK), 0) // GLA_TILE
            == lax.broadcasted_iota(jnp.int32, (GLA_HEADS * GLA_TILE, GQK), 1) // GLA_DK,
            1.0, 0.0).astype(BF16)

    row8 = lax.broadcasted_iota(jnp.int32, (8, GQK), 0)

    def row_slice(start, size):
        return pl.ds(start if isinstance(start, int) else pl.multiple_of(start, size), size)

    def tile_rows(t):
        return row_slice(t * GLA_TILE, GLA_TILE)

    finished = set()

    def total_row(c, d):
        return c * CHUNK + (CHUNK - 1 if d == 0 else 0)

    def tile_dir(b, t, d):
        rows = tile_rows(t)
        g = g_refs[d][b, rows, :]
        g_hi = g.astype(BF16)
        g_lo = (g - g_hi.astype(F32)).astype(BF16)
        tri_b = tri_ref[d].astype(BF16)
        cum = (jnp.dot(tri_b, g_hi, preferred_element_type=F32)
               + jnp.dot(tri_b, g_lo, preferred_element_type=F32))
        yield
        totals = [cum[total_row(c, d):total_row(c, d) + 1, :] for c in range(CHUNKS_PER_TILE)]
        tot8 = jnp.zeros((8, GQK), F32)
        for c in range(CHUNKS_PER_TILE):
            tot8 = jnp.where(row8 == c, totals[c], tot8)
        dec_t = jnp.concatenate([jnp.exp(tot8), jnp.zeros((LANES - 8, GQK), F32)], axis=0).T
        q = gq_ref[b, rows, :] * (GLA_DK ** -0.5)
        k = gk_ref[b, rows, :]
        v = gv_ref[b, rows, :]
        tot = jnp.concatenate([jnp.broadcast_to(tc, (CHUNK, GQK)) for tc in totals], axis=0)
        qe = (q * jnp.exp(cum)).astype(BF16)
        ke = (k * jnp.exp(-cum)).astype(BF16)
        kd_t = (k * jnp.exp(tot - cum)).T.astype(BF16)
        bd_qk = bdqk_ref[...] > 0
        tri = tri_ref[d] > 0

        qm = jnp.where(hm_ref[...] > 0, jnp.tile(qe, (GLA_HEADS, 1)), 0.0)
        yield
        att = lax.dot_general(qm, ke, _NT, preferred_element_type=F32)

        yield
        intra, upd = [], []
        for hd in range(GLA_HEADS):
            vh = v[:, hd * GLA_DV:(hd + 1) * GLA_DV]
            a_h = jnp.where(tri, att[hd * GLA_TILE:(hd + 1) * GLA_TILE, :], 0.0).astype(BF16)
            intra.append(jnp.dot(a_h, vh, preferred_element_type=F32))
            kd_h = jnp.tile(kd_t[hd * GLA_DK:(hd + 1) * GLA_DK, :], (CHUNKS_PER_TILE, 1))
            upd.append(jnp.dot(jnp.where(bd_qk, kd_h, 0.0), vh, preferred_element_type=F32))

        yield
        state = [state_refs[d][b, hd] for hd in range(GLA_HEADS)]
        order = range(CHUNKS_PER_TILE) if d == 0 else range(CHUNKS_PER_TILE - 1, -1, -1)
        seen = {}
        for c in order:
            seen[c] = jnp.concatenate(state, axis=0).astype(BF16)
            decay = jnp.broadcast_to(dec_t[:, c:c + 1], (GQK, GLA_DV))
            for hd in range(GLA_HEADS):
                ks = slice(hd * GLA_DK, (hd + 1) * GLA_DK)
                state[hd] = decay[ks, :] * state[hd] + upd[hd][c * CHUNK:(c + 1) * CHUNK, :]
        for hd in range(GLA_HEADS):
            state_refs[d][b, hd] = state[hd]

        yield
        for c in range(CHUNKS_PER_TILE):
            cr = slice(c * CHUNK, (c + 1) * CHUNK)
            q_c = jnp.concatenate([qm[hd * GLA_TILE + c * CHUNK:hd * GLA_TILE + (c + 1) * CHUNK, :]
                                   for hd in range(GLA_HEADS)], axis=0)
            inter = jnp.dot(q_c, seen[c], preferred_element_type=F32)
            o = jnp.concatenate([intra[hd][cr, :] + inter[hd * CHUNK:(hd + 1) * CHUNK, :]
                                 for hd in range(GLA_HEADS)], axis=1)
            oacc_ref[d, b, row_slice(t * GLA_TILE + c * CHUNK, CHUNK), :] = o
        if isinstance(t, int):
            finished.add((b, t, d))

    if zero_init:
        sf_ref[...] = jnp.zeros(sf_ref.shape, F32)
        sb_ref[...] = jnp.zeros(sb_ref.shape, F32)
    else:
        sf_ref[...] = sf0_ref[...]
        sb_ref[...] = sb0_ref[...]

    gn = gn_ref[...]

    def epilogue(t):
        while isinstance(t, int) and not all((b, t, d) in finished for b in range(n_seqs) for d in range(2)):
            yield
        rows = tile_rows(t)
        for b in range(n_seqs):
            for hd in range(GLA_HEADS):
                vs = slice(hd * GLA_DV, (hd + 1) * GLA_DV)
                o = _rms(oacc_ref[0, b, rows, vs] + oacc_ref[1, b, rows, vs], gn)
                go = go_ref[b, rows, vs]
                o_ref[b, rows, vs] = (o * (go * _sigmoid(go))).astype(BF16)
        return
        yield

    def main_chains(first_tile, n):
        chains = []
        for t in [first_tile + u for u in range(n)]:
            for b in range(n_seqs):
                chains += [tile_dir(b, t, 0), tile_dir(b, n_tiles - 1 - t, 1)]
        return chains

    if n_tiles <= GLA_STATIC_TILES:
        _interleave(main_chains(0, n_tiles) + [epilogue(t) for t in range(n_tiles)])
    else:
        per_step = next(c for c in (4, 2, 1) if n_tiles % c == 0)

        def main_body(i, carry):
            _interleave(main_chains(i * per_step, per_step))
            return carry

        lax.fori_loop(0, n_tiles // per_step, main_body, 0)

        def epilogue_body(t, carry):
            _interleave([epilogue(t)])
            return carry

        lax.fori_loop(0, n_tiles, epilogue_body, 0)


def _gla(gq, gk, gv, gf, gb, go, init_states, gn, n_seqs):
    b, t, _ = gq.shape
    n_tiles = t // GLA_TILE
    zero_init = init_states is None
    seq = lambda c: pl.BlockSpec((n_seqs, t, c), lambda i: (i, 0, 0))
    st = pl.BlockSpec((n_seqs, GLA_HEADS, GLA_DK, GLA_DV), lambda i: (i, 0, 0, 0))
    in_specs = [seq(GQK), seq(GQK), seq(GV), seq(GQK), seq(GQK), seq(GV)]
    args = [gq, gk, gv, gf, gb, go]
    if not zero_init:
        in_specs += [st, st]
        args += list(init_states)
    in_specs.append(_const_spec(gn.shape))
    args.append(gn)
    return pl.pallas_call(
        functools.partial(_gla_kernel, n_tiles=n_tiles, n_seqs=n_seqs, zero_init=zero_init),
        out_shape=[jax.ShapeDtypeStruct((b, t, GV), BF16),
                   jax.ShapeDtypeStruct((b, GLA_HEADS, GLA_DK, GLA_DV), F32),
                   jax.ShapeDtypeStruct((b, GLA_HEADS, GLA_DK, GLA_DV), F32)],
        grid=(b // n_seqs,),
        in_specs=in_specs,
        out_specs=[seq(GV), st, st],
        scratch_shapes=[pltpu.VMEM((2, n_seqs, t, GV), F32),
                        pltpu.VMEM((GQK, GLA_TILE), BF16),
                        pltpu.VMEM((2, GLA_TILE, GLA_TILE), F32),
                        pltpu.VMEM((GLA_HEADS * GLA_TILE, GQK), BF16)],
        name="gla_%d" % t,
        compiler_params=pltpu.CompilerParams(dimension_semantics=("arbitrary",),
                                             vmem_limit_bytes=VMEM_LIMIT),
    )(*args)


def _ffn_kernel(xp_ref, xs_ref, atp_ref, ats_ref, glp_ref, gls_ref, mod_ref, wout_ref, nf_ref,
                wfi_ref, wfo_ref, fn_ref, yp_ref, ys_ref, act_ref, *, ctx_tiles, tiles_per_seq):
    def tile(x_ref, at_ref, gl_ref, y_ref, mod_row):
        gt1, sh2, sc2, gt2 = _mod_rows(mod_ref, mod_row)[2:]
        mix = (jnp.dot(at_ref[...], wout_ref[0:VALL, :], preferred_element_type=F32)
               + jnp.dot(gl_ref[...], wout_ref[VALL:, :], preferred_element_type=F32))
        x1 = x_ref[...] + gt1 * mix
        h2 = (_rms(x1, nf_ref[...]) * (1.0 + sc2) + sh2).astype(BF16)
        for j in range(N_FF_CHUNKS):
            cs = slice(j * FF_CHUNK, (j + 1) * FF_CHUNK)
            a = jnp.dot(h2, wfi_ref[:, cs], preferred_element_type=F32)
            g = jnp.dot(h2, wfi_ref[:, D_FF + j * FF_CHUNK:D_FF + (j + 1) * FF_CHUNK],
                        preferred_element_type=F32)
            act_ref[:, cs] = (a * _sigmoid(a) * g).astype(BF16)
        ff = jnp.dot(act_ref[...], wfo_ref[...], preferred_element_type=F32)
        x2 = x1 + gt2 * ff
        y_ref[...] = _rms(x2, fn_ref[...])

    t = pl.program_id(0)

    @pl.when(t < ctx_tiles)
    def _():
        tile(xp_ref, atp_ref, glp_ref, yp_ref, 0)

    @pl.when(t >= ctx_tiles)
    def _():
        tile(xs_ref, ats_ref, gls_ref, ys_ref, 1 + (t - ctx_tiles) // tiles_per_seq)


def _ffn(xp, xs, attn_p, attn_s, gla_p, gla_s, mod, wout, nf, wfi, wfo, fn, tm, tiles_per_seq):
    d = xp.shape[1]
    ctx_tiles = xp.shape[0] // tm
    lat_tiles = xs.shape[0] // tm
    ctx_map = lambda s: (jnp.minimum(s, ctx_tiles - 1), 0)
    lat_map = lambda s: (jnp.maximum(s - ctx_tiles, 0), 0)
    tile = lambda c, m: pl.BlockSpec((tm, c), m)
    return pl.pallas_call(
        functools.partial(_ffn_kernel, ctx_tiles=ctx_tiles, tiles_per_seq=tiles_per_seq),
        out_shape=[jax.ShapeDtypeStruct(xp.shape, F32), jax.ShapeDtypeStruct(xs.shape, F32)],
        grid=(ctx_tiles + lat_tiles,),
        in_specs=[tile(d, ctx_map), tile(d, lat_map), tile(VALL, ctx_map), tile(VALL, lat_map),
                  tile(GV, ctx_map), tile(GV, lat_map), _const_spec(mod.shape),
                  _const_spec(wout.shape), _const_spec(nf.shape), _const_spec(wfi.shape),
                  _const_spec(wfo.shape), _const_spec(fn.shape)],
        out_specs=[tile(d, ctx_map), tile(d, lat_map)],
        scratch_shapes=[pltpu.VMEM((tm, D_FF), BF16)],
        name="out_ffn",
        compiler_params=pltpu.CompilerParams(dimension_semantics=("arbitrary",),
                                             vmem_limit_bytes=VMEM_LIMIT),
    )(xp, xs, attn_p, attn_s, gla_p, gla_s, mod, wout, nf, wfi, wfo, fn)


def _rope_tables(n_tokens):
    t = np.arange(n_tokens)
    row = (t // GRID_W).astype(np.float32)
    col = (t % GRID_W).astype(np.float32)
    half = MLA_ROPE // 2
    inv = (np.float32(ROPE_BASE) ** (-np.arange(0, half, 2, dtype=np.float32) / np.float32(half))).astype(np.float32)
    ang_r = row[:, None] * inv
    ang_c = col[:, None] * inv
    ang = np.concatenate([ang_r, ang_r, ang_c, ang_c], axis=-1).astype(np.float32)
    cos, sin = np.cos(ang), np.sin(ang)
    first = (np.arange(MLA_ROPE) % half) < (half // 2)
    cos_t = np.ones((n_tokens, LANES), np.float32)
    sa_t = np.zeros((n_tokens, LANES), np.float32)
    sb_t = np.zeros((n_tokens, LANES), np.float32)
    cos_t[:, ROPE_LANE0:ROPE_LANE0 + MLA_ROPE] = cos
    sa_t[:, ROPE_LANE0:ROPE_LANE0 + MLA_ROPE] = np.where(first, -sin, 0.0)
    sb_t[:, ROPE_LANE0:ROPE_LANE0 + MLA_ROPE] = np.where(first, 0.0, sin)
    return jnp.asarray(cos_t), jnp.asarray(sa_t), jnp.asarray(sb_t)


def kernel(x_prompt, x_sample, cache_kv_latent, cache_k_rope, state_gla_fwd, state_gla_bwd, c, c_ctx, w_ada, b_ada, norm_attn, w_in, mla_q_norm, w_uq, mla_kv_norm, w_ukv, w_gate_f, b_gate_f, w_gate_b, b_gate_b, gla_norm, w_out, norm_ffn, w_ffn_in, w_ffn_out, final_norm):
    batch, seq, d = x_prompt.shape
    dec_batch, dec_seq, _ = x_sample.shape
    assert w_ada.shape[0] == 1 and w_in.shape[-1] == W_COLS and w_ffn_in.shape[-1] == 2 * D_FF
    l = 0

    mod = _ada(c_ctx, c, w_ada[l], b_ada[l])

    win, wuq, wk, wvt, wg = _prep_in_weights(w_in, w_uq, w_ukv, w_gate_f, w_gate_b)
    in_w = (norm_attn[l].reshape(1, d), win, mla_q_norm[l].reshape(1, Q_LORA), wuq,
            mla_kv_norm[l].reshape(1, KV_LORA), wk, wvt, wg, b_gate_f, b_gate_b)
    gn = gla_norm[l].reshape(1, GLA_DV)
    tm, tm_ffn = 512, 512
    r3 = lambda a, b_, t: a.reshape(b_, t, a.shape[-1])

    xp = x_prompt.reshape(batch * seq, d)
    (q, k, vt, gq, gk, gv, gf, gb, go, ckv, kr_t) = _inproj(xp, mod, lambda i: 0, in_w, None, tm, seq)
    (attn_p,) = _attention(r3(q, batch, seq), r3(k, batch, seq), vt, None, seq, ATTN_CTX_SEQS)
    gla_p, sf, sb = _gla(r3(gq, batch, seq), r3(gk, batch, seq), r3(gv, batch, seq), r3(gf, batch, seq),
                         r3(gb, batch, seq), r3(go, batch, seq), None, gn, GLA_CTX_SEQS)

    xs = x_sample.reshape(dec_batch * dec_seq, d)
    tiles = dec_seq // tm
    (q, k, vt, gq, gk, gv, gf, gb, go) = _inproj(xs, mod, lambda i: 1 + i // tiles, in_w,
                                                  _rope_tables(dec_seq), tm, dec_seq)
    kc, vct = _decomp(cache_kv_latent[:, l], jnp.swapaxes(cache_k_rope[:, l], 1, 2), wk, wvt)
    attn_s, wout, wfi, wfo = _attention(r3(q, dec_batch, dec_seq), r3(k, dec_batch, dec_seq), vt, (kc, vct),
                                        ATTN_LAT_QUERIES, 1, (w_out, w_ffn_in, w_ffn_out))
    gla_s, _, _ = _gla(r3(gq, dec_batch, dec_seq), r3(gk, dec_batch, dec_seq), r3(gv, dec_batch, dec_seq),
                       r3(gf, dec_batch, dec_seq), r3(gb, dec_batch, dec_seq), r3(go, dec_batch, dec_seq),
                       (state_gla_fwd[:, l].astype(F32), state_gla_bwd[:, l].astype(F32)), gn, 1)

    flat = lambda a: a.reshape(-1, a.shape[-1])
    y_prompt, y_sample = _ffn(xp, xs, flat(attn_p), flat(attn_s), flat(gla_p), flat(gla_s), mod,
                              wout, norm_ffn[l].reshape(1, d), wfi, wfo,
                              final_norm.reshape(1, d), tm_ffn, dec_seq // tm_ffn)
    y_prompt = y_prompt.reshape(batch, seq, d)
    y_sample = y_sample.reshape(dec_batch, dec_seq, d)

    new_kv_latent = ckv.reshape(batch, 1, seq, KV_LORA)
    new_k_rope = jnp.swapaxes(kr_t, 1, 2).reshape(batch, 1, seq, MLA_ROPE)
    new_state_fwd = sf.reshape(batch, 1, GLA_HEADS, GLA_DK, GLA_DV).astype(x_prompt.dtype)
    new_state_bwd = sb.reshape(batch, 1, GLA_HEADS, GLA_DK, GLA_DV).astype(x_prompt.dtype)
    return (y_prompt, y_sample, new_kv_latent, new_k_rope, new_state_fwd, new_state_bwd)
```

```python
import functools

import numpy as np
import jax
import jax.numpy as jnp
from jax import lax
from jax.experimental import pallas as pl
from jax.experimental.pallas import tpu as pltpu

F32 = jnp.float32
BF16 = jnp.bfloat16

GRID_W = 64
MLA_HEADS = 8
MLA_NOPE = 64
MLA_ROPE = 32
MLA_QK = MLA_NOPE + MLA_ROPE
MLA_V = 64
Q_LORA = 384
KV_LORA = 256
GLA_HEADS = 4
GLA_DK = 64
GLA_DV = 128
GATE_RANK = 16
GATE_NORM = 16.0
CHUNK = 64
D_FF = 2816
ROPE_BASE = 10000.0
EPS = 1e-6
LOG2_E = 1.4426950408889634

LANES = 128
HEAD_PAD = LANES
ROPE_LANE0 = MLA_NOPE
GQK = GLA_HEADS * GLA_DK
GV = GLA_HEADS * GLA_DV
QPAD = MLA_HEADS * HEAD_PAD
VALL = MLA_HEADS * MLA_V
ONES_ROWS = 16
KEY_BLOCK = 1024

W_KR = Q_LORA + KV_LORA
W_GQ = W_KR + MLA_ROPE
W_GF = W_GQ + 2 * GQK + GV
W_GO = W_GF + 2 * GATE_RANK
W_COLS = W_GO + GV

Z_Q = 0
Z_KV = Z_Q + Q_LORA
Z_GQ = Z_KV + KV_LORA
Z_GK = Z_GQ + GQK
Z_GV = Z_GK + GQK
Z_GO = Z_GV + GV
Z_MISC = Z_GO + GV
Z_COLS = Z_MISC + LANES

FF_CHUNK = 256
N_FF_CHUNKS = D_FF // FF_CHUNK
FFN_SUB = 256
FFN_PHASE_ENDS = (3, 7, 10)

GLA_TILE = 256
CHUNKS_PER_TILE = GLA_TILE // CHUNK
ADA_ROWS = 128
INPROJ_SUB = 512
Q_TILE = 256
ATTN_LAT_QUERIES = 512
ATTN_CTX_SEQS = 4
GLA_STATIC_TILES = 8
GLA_CTX_SEQS = 4

VMEM_LIMIT = 56 * 1024 * 1024

_NT = (((1,), (1,)), ((), ()))


def _rms(x, w):
    return x * lax.rsqrt(jnp.mean(x * x, axis=-1, keepdims=True) + EPS) * w


def _sigmoid(x):
    return 1.0 / (1.0 + jnp.exp(-x))


def _log_sigmoid(x):
    return jnp.minimum(x, 0.0) - jnp.log1p(jnp.exp(-jnp.abs(x)))


def _interleave(chains):
    pending, active = list(chains), []
    while pending or active:
        if pending:
            active.append(pending.pop(0))
        for chain in list(active):
            try:
                next(chain)
            except StopIteration:
                active.remove(chain)


def _const_spec(shape):
    nd = len(shape)
    return pl.BlockSpec(shape, lambda *_: (0,) * nd, pipeline_mode=pl.Buffered(1))


def _mod_rows(mod_ref, r):
    return [mod_ref[k, pl.ds(r, 1), :] for k in range(mod_ref.shape[0])]


def _ada_kernel(cctx_ref, c_ref, w_ref, b_ref, o_ref):
    k = pl.program_id(0)
    d = o_ref.shape[2]
    row = lax.broadcasted_iota(jnp.int32, (8, cctx_ref.shape[1]), 0)
    cond = jnp.where(row == 0, cctx_ref[...], 0.0)
    for r in range(c_ref.shape[0]):
        cond = jnp.where(row == 1 + r, c_ref[r:r + 1, :], cond)
    s = (cond * _sigmoid(cond)).astype(BF16)
    part = jnp.dot(s, w_ref[...].astype(BF16), preferred_element_type=F32)
    for j in range(o_ref.shape[0]):
        sl = slice(j * d, (j + 1) * d)

        @pl.when(k == 0)
        def _():
            o_ref[j] = part[:, sl] + b_ref[:, sl]

        @pl.when(k > 0)
        def _():
            o_ref[j] += part[:, sl]


def _ada(c_ctx, c, w_ada, b_ada):
    d = w_ada.shape[0]
    n = w_ada.shape[1]
    assert 1 + c.shape[0] <= 8
    return pl.pallas_call(
        _ada_kernel,
        out_shape=jax.ShapeDtypeStruct((n // d, 8, d), F32),
        grid=(d // ADA_ROWS,),
        in_specs=[pl.BlockSpec((1, ADA_ROWS), lambda k: (0, k)),
                  pl.BlockSpec((c.shape[0], ADA_ROWS), lambda k: (0, k)),
                  pl.BlockSpec((ADA_ROWS, n), lambda k: (k, 0)),
                  pl.BlockSpec((1, n), lambda k: (0, 0))],
        out_specs=pl.BlockSpec((n // d, 8, d), lambda k: (0, 0, 0)),
        name="ada_mod",
        compiler_params=pltpu.CompilerParams(dimension_semantics=("arbitrary",)),
    )(c_ctx.reshape(1, d), c, w_ada, b_ada.reshape(1, n))


def _prep_kernel(wint_ref, wuq_ref, wukv_ref, wgf_ref, wgb_ref, win_o, wuq_o, wk_o, wvt_o, wg_o):
    cols = wint_ref.shape[1]
    for dst, src, n in ((Z_Q, 0, W_KR), (Z_GQ, W_GQ, W_GF - W_GQ), (Z_GO, W_GO, GV)):
        win_o[:, dst:dst + n] = wint_ref[src:src + n, :].T.astype(BF16)
    z32 = jnp.zeros((32, cols), F32)
    misc_t = jnp.concatenate([wint_ref[W_GF:W_GO, :], z32, wint_ref[W_KR:W_GQ, :], z32], axis=0)
    win_o[:, Z_MISC:Z_COLS] = misc_t.T.astype(BF16)

    u = wuq_ref[...]
    zq = jnp.zeros((u.shape[0], HEAD_PAD - MLA_QK), F32)
    for hd in range(MLA_HEADS):
        blk = jnp.concatenate([u[:, hd * MLA_QK:(hd + 1) * MLA_QK], zq], axis=1)
        wuq_o[:, hd * HEAD_PAD:(hd + 1) * HEAD_PAD] = blk.astype(BF16)

    @pl.when(pl.program_id(0) == 0)
    def _():
        kv = wukv_ref[...]
        per = MLA_NOPE + MLA_V
        lane = lax.broadcasted_iota(jnp.int32, (kv.shape[0], per), 1)
        for hd in range(MLA_HEADS):
            blk = kv[:, hd * per:(hd + 1) * per]
            wk_o[:, hd * HEAD_PAD:(hd + 1) * HEAD_PAD] = jnp.where(lane < MLA_NOPE, blk, 0.0).astype(BF16)
        wv = jnp.concatenate([kv[:, hd * per + MLA_NOPE:(hd + 1) * per] for hd in range(MLA_HEADS)], axis=1)
        wvt_o[...] = wv.T.astype(BF16)

        wg_o[...] = jnp.zeros(wg_o.shape, BF16)
        wg_o[0:GATE_RANK, 0:GQK] = wgf_ref[...].astype(BF16)
        wg_o[GATE_RANK:2 * GATE_RANK, GQK:2 * GQK] = wgb_ref[...].astype(BF16)


def _prep_in_weights(w_in, w_uq, w_ukv, w_gate_f, w_gate_b):
    d = w_in.shape[1]
    steps = 4
    w_in_t = jnp.swapaxes(w_in, 1, 2)
    rb3 = lambda r, c: pl.BlockSpec((None, r // steps, c), lambda i: (0, i, 0))
    rb = lambda r, c: pl.BlockSpec((r // steps, c), lambda i: (i, 0))
    full3 = lambda shape: pl.BlockSpec((None,) + tuple(shape[1:]), lambda i: (0, 0, 0))
    full = lambda shape: pl.BlockSpec(shape, lambda i: (0, 0))
    return pl.pallas_call(
        _prep_kernel,
        out_shape=[jax.ShapeDtypeStruct((d, Z_COLS), BF16),
                   jax.ShapeDtypeStruct((Q_LORA, QPAD), BF16),
                   jax.ShapeDtypeStruct((KV_LORA, QPAD), BF16),
                   jax.ShapeDtypeStruct((VALL, KV_LORA), BF16),
                   jax.ShapeDtypeStruct((LANES, 2 * GQK), BF16)],
        grid=(steps,),
        in_specs=[pl.BlockSpec((None, W_COLS, d // steps), lambda i: (0, 0, i)),
                  rb3(Q_LORA, MLA_HEADS * MLA_QK), full3(w_ukv.shape),
                  full3(w_gate_f.shape), full3(w_gate_b.shape)],
        out_specs=[rb(d, Z_COLS), rb(Q_LORA, QPAD), full((KV_LORA, QPAD)), full((VALL, KV_LORA)),
                   full((LANES, 2 * GQK))],
        name="weight_prep",
        compiler_params=pltpu.CompilerParams(dimension_semantics=("arbitrary",)),
    )(w_in_t, w_uq, w_ukv, w_gate_f, w_gate_b)


def _inproj_kernel(*refs, latent, mod_row):
    (x_ref, mod_ref, nw_ref, win_ref, qn_ref, wuq_ref, kvn_ref, wk_ref, wvt_ref, wg_ref, bgf_ref, bgb_ref) = refs[:12]
    if latent:
        cos_ref, sa_ref, sb_ref = refs[12:15]
        outs = refs[15:]
    else:
        outs = refs[12:]
    q_ref, k_ref, vt_ref, gq_ref, gk_ref, gv_ref, gf_ref, gb_ref, go_ref = outs[:9]

    sh1, sc1 = _mod_rows(mod_ref, mod_row(pl.program_id(0)))[:2]
    scale = MLA_QK ** -0.5 * LOG2_E
    lane = lax.broadcasted_iota(jnp.int32, (INPROJ_SUB, LANES), 1)
    in_rope = (lane >= ROPE_LANE0) & (lane < ROPE_LANE0 + MLA_ROPE)

    def sub_tile(r0):
        rows = slice(r0, r0 + INPROJ_SUB)
        h = (_rms(x_ref[rows, :], nw_ref[...]) * (1.0 + sc1) + sh1).astype(BF16)
        yield
        z_all = jnp.dot(h, win_ref[...], preferred_element_type=F32)
        z = lambda lo, n: z_all[:, lo:lo + n]
        yield
        qn = _rms(z(Z_Q, Q_LORA), qn_ref[...]).astype(BF16)
        ckv = _rms(z(Z_KV, KV_LORA), kvn_ref[...])
        ckv_b = ckv.astype(BF16)
        misc = z(Z_MISC, LANES)
        yield
        q = jnp.dot(qn, wuq_ref[...], preferred_element_type=F32)
        kn = jnp.dot(ckv_b, wk_ref[...], preferred_element_type=F32)
        vt_ref[:, rows] = lax.dot_general(wvt_ref[...], ckv_b, _NT,
                                          preferred_element_type=F32).astype(BF16)
        gpre = jnp.dot(misc.astype(BF16), wg_ref[...], preferred_element_type=F32)
        yield
        if latent:
            cos, sa, sb = cos_ref[rows, :], sa_ref[rows, :], sb_ref[rows, :]

            def rope(t):
                return t * cos + pltpu.roll(t, LANES - 8, 1) * sa + pltpu.roll(t, 8, 1) * sb
        else:
            def rope(t):
                return t

        krope = rope(misc)
        for hd in range(MLA_HEADS):
            sl = slice(hd * HEAD_PAD, (hd + 1) * HEAD_PAD)
            q_ref[rows, sl] = (rope(q[:, sl]) * scale).astype(BF16)
            k_ref[rows, sl] = jnp.where(in_rope, krope, kn[:, sl]).astype(BF16)
        gq_ref[rows, :] = z(Z_GQ, GQK)
        gk_ref[rows, :] = z(Z_GK, GQK)
        gv_ref[rows, :] = z(Z_GV, GV).astype(BF16)
        go_ref[rows, :] = z(Z_GO, GV)
        gf_ref[rows, :] = _log_sigmoid(gpre[:, :GQK] + bgf_ref[...]) * (1.0 / GATE_NORM)
        gb_ref[rows, :] = _log_sigmoid(gpre[:, GQK:] + bgb_ref[...]) * (1.0 / GATE_NORM)
        if not latent:
            ckv_ref, krt_ref = outs[9:]
            ckv_ref[rows, :] = ckv
            misc_t = misc.T
            n = krt_ref.shape[2]
            for b in range(INPROJ_SUB // n):
                krt_ref[r0 // n + b] = misc_t[ROPE_LANE0:ROPE_LANE0 + MLA_ROPE, b * n:(b + 1) * n]

    _interleave([sub_tile(r0) for r0 in range(0, x_ref.shape[0], INPROJ_SUB)])


def _inproj(x2d, mod, mod_row, weights, rope_tabs, tm, seq_len):
    n_tok, d = x2d.shape
    latent = rope_tabs is not None
    tiles_per_seq = max(seq_len // tm, 1)
    nw, win, qn, wuq, kvn, wk, wvt, wg, bgf, bgb = weights
    row = lambda i: (i, 0)
    in_specs = [pl.BlockSpec((tm, d), row), _const_spec(mod.shape),
                _const_spec(nw.shape), _const_spec(win.shape), _const_spec(qn.shape),
                _const_spec(wuq.shape), _const_spec(kvn.shape), _const_spec(wk.shape),
                _const_spec(wvt.shape), _const_spec(wg.shape), _const_spec(bgf.shape),
                _const_spec(bgb.shape)]
    args = [x2d, mod, nw, win, qn, wuq, kvn, wk, wvt, wg, bgf, bgb]
    if latent:
        tab = pl.BlockSpec((tm, LANES), lambda i: (i % tiles_per_seq, 0))
        in_specs += [tab, tab, tab]
        args += list(rope_tabs)
    out_cols = [(QPAD, BF16), (QPAD, BF16), None, (GQK, F32), (GQK, F32), (GV, BF16),
                (GQK, F32), (GQK, F32), (GV, F32)]
    if not latent:
        out_cols += [(KV_LORA, F32)]
    out_shape = [jax.ShapeDtypeStruct((n_tok, oc[0]), oc[1]) if oc else
                 jax.ShapeDtypeStruct((VALL, n_tok), BF16) for oc in out_cols]
    out_specs = [pl.BlockSpec((tm, oc[0]), row) if oc else
                 pl.BlockSpec((VALL, tm), lambda i: (0, i)) for oc in out_cols]
    if not latent:
        seq = seq_len
        out_shape.append(jax.ShapeDtypeStruct((n_tok // seq, MLA_ROPE, seq), F32))
        out_specs.append(pl.BlockSpec((tm // seq, MLA_ROPE, seq), lambda i: (i, 0, 0)))
    return pl.pallas_call(
        functools.partial(_inproj_kernel, latent=latent, mod_row=mod_row),
        out_shape=out_shape,
        grid=(n_tok // tm,),
        in_specs=in_specs,
        out_specs=out_specs,
        name="inproj_lat" if latent else "inproj_ctx",
        compiler_params=pltpu.CompilerParams(dimension_semantics=("arbitrary",),
                                             vmem_limit_bytes=VMEM_LIMIT),
    )(*args)


def _decomp_kernel(ckv_ref, krt_ref, wk_ref, wvt_ref, k_ref, vt_ref):
    ckv_b = ckv_ref[...].astype(BF16)
    kn = jnp.dot(ckv_b, wk_ref[...], preferred_element_type=F32)
    n_keys = krt_ref.shape[1]
    kr = jnp.concatenate([jnp.zeros((ROPE_LANE0, n_keys), F32), krt_ref[...],
                          jnp.zeros((LANES - ROPE_LANE0 - MLA_ROPE, n_keys), F32)], axis=0).T
    lane = lax.broadcasted_iota(jnp.int32, kr.shape, 1)
    in_rope = (lane >= ROPE_LANE0) & (lane < ROPE_LANE0 + MLA_ROPE)
    for hd in range(MLA_HEADS):
        sl = slice(hd * HEAD_PAD, (hd + 1) * HEAD_PAD)
        k_ref[:, sl] = jnp.where(in_rope, kr, kn[:, sl]).astype(BF16)
    vt_ref[...] = lax.dot_general(wvt_ref[...], ckv_b, _NT, preferred_element_type=F32).astype(BF16)


def _decomp(ckv, kr_t, wk, wvt):
    b, s, _ = ckv.shape
    return pl.pallas_call(
        _decomp_kernel,
        out_shape=[jax.ShapeDtypeStruct((b, s, QPAD), BF16), jax.ShapeDtypeStruct((VALL, b * s), BF16)],
        grid=(b,),
        in_specs=[pl.BlockSpec((None, s, KV_LORA), lambda i: (i, 0, 0)),
                  pl.BlockSpec((None, MLA_ROPE, s), lambda i: (i, 0, 0)),
                  _const_spec(wk.shape), _const_spec(wvt.shape)],
        out_specs=[pl.BlockSpec((None, s, QPAD), lambda i: (i, 0, 0)),
                   pl.BlockSpec((VALL, s), lambda i: (0, i))],
        name="ctx_decompress",
        compiler_params=pltpu.CompilerParams(dimension_semantics=("arbitrary",)),
    )(ckv, kr_t, wk, wvt)


def _attn_kernel(*refs, has_ctx, n_seqs, n_side):
    n_in = 5 if has_ctx else 3
    side_in, refs = refs[n_in:n_in + n_side], refs[:n_in] + refs[n_in + n_side:]
    side_out, refs = refs[n_in + 1:n_in + 1 + n_side], refs[:n_in + 1] + refs[n_in + 1 + n_side:]
    if has_ctx:
        q_ref, kc_ref, vct_ref, k_ref, vt_ref, o_ref, st_ref, p_ref = refs
    else:
        q_ref, k_ref, vt_ref, o_ref, st_ref, p_ref = refs
    tq = Q_TILE

    for src, dst in zip(side_in, side_out):
        dst[...] = src[...].astype(BF16)

    def key_blocks(bi):
        srcs = [(kc_ref, vct_ref)] if has_ctx else []
        blocks, row0 = [], 0
        for kr, vr in srcs + [(k_ref, vt_ref)]:
            n_keys = kr.shape[1]
            size = min(KEY_BLOCK, n_keys)
            for r in range(0, n_keys, size):
                blocks.append((kr, vr, r, bi * n_keys + r, size, row0))
                row0 += size
        return blocks

    units = [(bi, slice(q0, q0 + tq), hd) for bi in range(n_seqs)
             for q0 in range(0, q_ref.shape[1], tq) for hd in range(MLA_HEADS)]
    col_max = [None] * len(units)
    pair = []
    for stage in range(len(units) + 2):
        ua, ub, uc = stage, stage - 1, stage - 2
        run_max = None
        acc = jnp.zeros((MLA_V + ONES_ROWS, tq), F32)
        for j in range(len(key_blocks(0))):
            if ua < len(units):
                bi, qrows, hd = units[ua]
                kr, _, r0, _, size, srow = key_blocks(bi)[j]
                sl = slice(hd * HEAD_PAD, (hd + 1) * HEAD_PAD)
                st = lax.dot_general(kr[bi, r0:r0 + size, sl], q_ref[bi, qrows, sl], _NT,
                                     preferred_element_type=F32)
                st_ref[ua % 2, srow:srow + size, :] = st
                blk_max = jnp.max(st.reshape(size // 8, 8, tq), axis=0)
                run_max = blk_max if run_max is None else jnp.maximum(run_max, blk_max)
            if 0 <= ub < len(units):
                _, _, _, _, size, srow = key_blocks(0)[j]
                p_ref[ub % 2, srow:srow + size, :] = jnp.exp2(
                    st_ref[ub % 2, srow:srow + size, :] - col_max[ub]).astype(BF16)
            if uc >= 0:
                bi, _, hd = units[uc]
                _, vr, _, c0, size, srow = key_blocks(bi)[j]
                v_aug = jnp.concatenate([vr[hd * MLA_V:(hd + 1) * MLA_V, c0:c0 + size],
                                         jnp.ones((ONES_ROWS, size), BF16)], axis=0)
                acc = acc + jnp.dot(v_aug, p_ref[uc % 2, srow:srow + size, :],
                                    preferred_element_type=F32)
        if ua < len(units):
            col_max[ua] = jnp.max(run_max, axis=0, keepdims=True)
        if uc >= 0:
            bi, qrows, hd = units[uc]
            pair.append(acc[:MLA_V, :] / acc[MLA_V:MLA_V + 1, :])
            if len(pair) == 2:
                o_ref[bi, qrows, (hd - 1) * MLA_V:(hd + 1) * MLA_V] = (
                    jnp.concatenate(pair, axis=0).T.astype(BF16))
                pair = []


def _attention(q, k, vt, ctx_kv, tq, n_seqs, side_weights=()):
    b, t, _ = q.shape
    steps = (b // n_seqs) * (t // tq)
    assert (n_seqs == 1 or tq == t) and tq % Q_TILE == 0
    has_ctx = ctx_kv is not None
    in_specs = [pl.BlockSpec((n_seqs, tq, QPAD), lambda i, j: (i, j, 0))]
    args = [q]
    if has_ctx:
        kc, vct = ctx_kv
        s = kc.shape[1]
        in_specs += [pl.BlockSpec((n_seqs, s, QPAD), lambda i, j: (i, 0, 0)),
                     pl.BlockSpec((VALL, n_seqs * s), lambda i, j: (0, i))]
        args += [kc, vct]
    in_specs += [pl.BlockSpec((n_seqs, t, QPAD), lambda i, j: (i, 0, 0)),
                 pl.BlockSpec((VALL, n_seqs * t), lambda i, j: (0, i))]
    args += [k, vt]
    out_shape = [jax.ShapeDtypeStruct((b, t, VALL), BF16)]
    out_specs = [pl.BlockSpec((n_seqs, tq, VALL), lambda i, j: (i, j, 0))]
    nj = t // tq
    for w in side_weights:
        _, rows, cols = w.shape
        assert rows % (16 * steps) == 0
        in_specs.append(pl.BlockSpec((None, rows // steps, cols), lambda i, j: (0, i * nj + j, 0)))
        args.append(w)
        out_shape.append(jax.ShapeDtypeStruct((rows, cols), BF16))
        out_specs.append(pl.BlockSpec((rows // steps, cols), lambda i, j: (i * nj + j, 0)))
    return pl.pallas_call(
        functools.partial(_attn_kernel, has_ctx=has_ctx, n_seqs=n_seqs, n_side=len(side_weights)),
        out_shape=out_shape,
        grid=(b // n_seqs, t // tq),
        in_specs=in_specs,
        out_specs=out_specs,
        scratch_shapes=[pltpu.VMEM((2, t + (s if has_ctx else 0), Q_TILE), F32),
                        pltpu.VMEM((2, t + (s if has_ctx else 0), Q_TILE), BF16)],
        name="mla_attn_lat" if has_ctx else "mla_attn_ctx",
        compiler_params=pltpu.CompilerParams(dimension_semantics=("arbitrary", "arbitrary"),
                                             vmem_limit_bytes=VMEM_LIMIT),
    )(*args)


def _gla_kernel(*refs, n_tiles, n_seqs, zero_init):
    gq_ref, gk_ref, gv_ref, gf_ref, gb_ref, go_ref = refs[:6]
    if zero_init:
        gn_ref, o_ref, sf_ref, sb_ref, oacc_ref, bdqk_ref, tri_ref, hm_ref = refs[6:]
    else:
        (sf0_ref, sb0_ref, gn_ref, o_ref, sf_ref, sb_ref, oacc_ref,
         bdqk_ref, tri_ref, hm_ref) = refs[6:]
    g_refs = (gf_ref, gb_ref)
    state_refs = (sf_ref, sb_ref)

    @pl.when(pl.program_id(0) == 0)
    def _():
        ri = lax.broadcasted_iota(jnp.int32, (GLA_TILE, GLA_TILE), 0)
        ci = lax.broadcasted_iota(jnp.int32, (GLA_TILE, GLA_TILE), 1)
        same_chunk = (ri // CHUNK) == (ci // CHUNK)
        bdqk_ref[...] = jnp.where(same_chunk, 1.0, 0.0).astype(BF16)
        tri_ref[0] = jnp.where(same_chunk & (ri >= ci), 1.0, 0.0)
        tri_ref[1] = jnp.where(same_chunk & (ci >= ri), 1.0, 0.0)
        hm_ref[...] = jnp.where(
            lax.broadcasted_iota(jnp.int32, (GLA_HEADS * GLA_TILE, GQK), 0) // GLA_TILE
            == lax.broadcasted_iota(jnp.int32, (GLA_HEADS * GLA_TILE, GQK), 1) // GLA_DK,
            1.0, 0.0).astype(BF16)

    row8 = lax.broadcasted_iota(jnp.int32, (8, GQK), 0)

    def row_slice(start, size):
        return pl.ds(start if isinstance(start, int) else pl.multiple_of(start, size), size)

    def tile_rows(t):
        return row_slice(t * GLA_TILE, GLA_TILE)

    finished = set()

    def total_row(c, d):
        return c * CHUNK + (CHUNK - 1 if d == 0 else 0)

    def tile_dir(b, t, d):
        rows = tile_rows(t)
        g = g_refs[d][b, rows, :]
        g_hi = g.astype(BF16)
        g_lo = (g - g_hi.astype(F32)).astype(BF16)
        tri_b = tri_ref[d].astype(BF16)
        cum = (jnp.dot(tri_b, g_hi, preferred_element_type=F32)
               + jnp.dot(tri_b, g_lo, preferred_element_type=F32))
        yield
        totals = [cum[total_row(c, d):total_row(c, d) + 1, :] for c in range(CHUNKS_PER_TILE)]
        tot8 = jnp.zeros((8, GQK), F32)
        for c in range(CHUNKS_PER_TILE):
            tot8 = jnp.where(row8 == c, totals[c], tot8)
        dec_t = jnp.concatenate([jnp.exp(tot8), jnp.zeros((LANES - 8, GQK), F32)], axis=0).T
        q = gq_ref[b, rows, :] * (GLA_DK ** -0.5)
        k = gk_ref[b, rows, :]
        v = gv_ref[b, rows, :]
        tot = jnp.concatenate([jnp.broadcast_to(tc, (CHUNK, GQK)) for tc in totals], axis=0)
        qe = (q * jnp.exp(cum)).astype(BF16)
        ke = (k * jnp.exp(-cum)).astype(BF16)
        kd_t = (k * jnp.exp(tot - cum)).T.astype(BF16)
        bd_qk = bdqk_ref[...] > 0
        tri = tri_ref[d] > 0

        qm = jnp.where(hm_ref[...] > 0, jnp.tile(qe, (GLA_HEADS, 1)), 0.0)
        yield
        att = lax.dot_general(qm, ke, _NT, preferred_element_type=F32)

        yield
        intra, upd = [], []
        for hd in range(GLA_HEADS):
            vh = v[:, hd * GLA_DV:(hd + 1) * GLA_DV]
            a_h = jnp.where(tri, att[hd * GLA_TILE:(hd + 1) * GLA_TILE, :], 0.0).astype(BF16)
            intra.append(jnp.dot(a_h, vh, preferred_element_type=F32))
            kd_h = jnp.tile(kd_t[hd * GLA_DK:(hd + 1) * GLA_DK, :], (CHUNKS_PER_TILE, 1))
            upd.append(jnp.dot(jnp.where(bd_qk, kd_h, 0.0), vh, preferred_element_type=F32))

        yield
        state = [state_refs[d][b, hd] for hd in range(GLA_HEADS)]
        order = range(CHUNKS_PER_TILE) if d == 0 else range(CHUNKS_PER_TILE - 1, -1, -1)
        seen = {}
        for c in order:
            seen[c] = jnp.concatenate(state, axis=0).astype(BF16)
            decay = jnp.broadcast_to(dec_t[:, c:c + 1], (GQK, GLA_DV))
            for hd in range(GLA_HEADS):
                ks = slice(hd * GLA_DK, (hd + 1) * GLA_DK)
                state[hd] = decay[ks, :] * state[hd] + upd[hd][c * CHUNK:(c + 1) * CHUNK, :]
        for hd in range(GLA_HEADS):
            state_refs[d][b, hd] = state[hd]

        yield
        for c in range(CHUNKS_PER_TILE):
            cr = slice(c * CHUNK, (c + 1) * CHUNK)
            q_c = jnp.concatenate([qm[hd * GLA_TILE + c * CHUNK:hd * GLA_TILE + (c + 1) * CHUNK, :]
                                   for hd in range(GLA_HEADS)], axis=0)
            inter = jnp.dot(q_c, seen[c], preferred_element_type=F32)
            o = jnp.concatenate([intra[hd][cr, :] + inter[hd * CHUNK:(hd + 1) * CHUNK, :]
                                 for hd in range(GLA_HEADS)], axis=1)
            oacc_ref[d, b, row_slice(t * GLA_TILE + c * CHUNK, CHUNK), :] = o
        if isinstance(t, int):
            finished.add((b, t, d))

    if zero_init:
        sf_ref[...] = jnp.zeros(sf_ref.shape, F32)
        sb_ref[...] = jnp.zeros(sb_ref.shape, F32)
    else:
        sf_ref[...] = sf0_ref[...]
        sb_ref[...] = sb0_ref[...]

    gn = gn_ref[...]

    def epilogue(t):
        while isinstance(t, int) and not all((b, t, d) in finished for b in range(n_seqs) for d in range(2)):
            yield
        rows = tile_rows(t)
        for b in range(n_seqs):
            for hd in range(GLA_HEADS):
                vs = slice(hd * GLA_DV, (hd + 1) * GLA_DV)
                o = _rms(oacc_ref[0, b, rows, vs] + oacc_ref[1, b, rows, vs], gn)
                go = go_ref[b, rows, vs]
                o_ref[b, rows, vs] = (o * (go * _sigmoid(go))).astype(BF16)
        return
        yield

    def main_chains(first_tile, n):
        chains = []
        for t in [first_tile + u for u in range(n)]:
            for b in range(n_seqs):
                chains += [tile_dir(b, t, 0), tile_dir(b, n_tiles - 1 - t, 1)]
        return chains

    if n_tiles <= GLA_STATIC_TILES:
        _interleave(main_chains(0, n_tiles) + [epilogue(t) for t in range(n_tiles)])
    else:
        per_step = next(c for c in (4, 2, 1) if n_tiles % c == 0)

        def main_body(i, carry):
            _interleave(main_chains(i * per_step, per_step))
            return carry

        lax.fori_loop(0, n_tiles // per_step, main_body, 0)

        def epilogue_body(t, carry):
            _interleave([epilogue(t)])
            return carry

        lax.fori_loop(0, n_tiles, epilogue_body, 0)


def _gla(gq, gk, gv, gf, gb, go, init_states, gn, n_seqs):
    b, t, _ = gq.shape
    n_tiles = t // GLA_TILE
    zero_init = init_states is None
    seq = lambda c: pl.BlockSpec((n_seqs, t, c), lambda i: (i, 0, 0))
    st = pl.BlockSpec((n_seqs, GLA_HEADS, GLA_DK, GLA_DV), lambda i: (i, 0, 0, 0))
    in_specs = [seq(GQK), seq(GQK), seq(GV), seq(GQK), seq(GQK), seq(GV)]
    args = [gq, gk, gv, gf, gb, go]
    if not zero_init:
        in_specs += [st, st]
        args += list(init_states)
    in_specs.append(_const_spec(gn.shape))
    args.append(gn)
    return pl.pallas_call(
        functools.partial(_gla_kernel, n_tiles=n_tiles, n_seqs=n_seqs, zero_init=zero_init),
        out_shape=[jax.ShapeDtypeStruct((b, t, GV), BF16),
                   jax.ShapeDtypeStruct((b, GLA_HEADS, GLA_DK, GLA_DV), F32),
                   jax.ShapeDtypeStruct((b, GLA_HEADS, GLA_DK, GLA_DV), F32)],
        grid=(b // n_seqs,),
        in_specs=in_specs,
        out_specs=[seq(GV), st, st],
        scratch_shapes=[pltpu.VMEM((2, n_seqs, t, GV), F32),
                        pltpu.VMEM((GQK, GLA_TILE), BF16),
                        pltpu.VMEM((2, GLA_TILE, GLA_TILE), F32),
                        pltpu.VMEM((GLA_HEADS * GLA_TILE, GQK), BF16)],
        name="gla_%d" % t,
        compiler_params=pltpu.CompilerParams(dimension_semantics=("arbitrary",),
                                             vmem_limit_bytes=VMEM_LIMIT),
    )(*args)


def _ffn_kernel(xp_ref, xs_ref, atp_ref, ats_ref, glp_ref, gls_ref, mod_ref, wout_ref, nf_ref,
                wfi_ref, wfo_ref, fn_ref, yp_ref, ys_ref, act_ref, *, ctx_tiles, tiles_per_seq):
    def sub_tile(x_ref, at_ref, gl_ref, y_ref, mod_row, r0):
        rows = slice(r0, r0 + FFN_SUB)
        gt1, sh2, sc2, gt2 = _mod_rows(mod_ref, mod_row)[2:]
        mix = (jnp.dot(at_ref[rows, :], wout_ref[0:VALL, :], preferred_element_type=F32)
               + jnp.dot(gl_ref[rows, :], wout_ref[VALL:, :], preferred_element_type=F32))
        yield
        x1 = x_ref[rows, :] + gt1 * mix
        h2 = (_rms(x1, nf_ref[...]) * (1.0 + sc2) + sh2).astype(BF16)
        yield
        for j in range(N_FF_CHUNKS):
            cs = slice(j * FF_CHUNK, (j + 1) * FF_CHUNK)
            a = jnp.dot(h2, wfi_ref[:, cs], preferred_element_type=F32)
            g = jnp.dot(h2, wfi_ref[:, D_FF + j * FF_CHUNK:D_FF + (j + 1) * FF_CHUNK],
                        preferred_element_type=F32)
            act_ref[rows, cs] = (a * _sigmoid(a) * g).astype(BF16)
            if j in FFN_PHASE_ENDS:
                yield
        ff = jnp.dot(act_ref[rows, :], wfo_ref[...], preferred_element_type=F32)
        yield
        x2 = x1 + gt2 * ff
        y_ref[rows, :] = _rms(x2, fn_ref[...])

    def tile(x_ref, at_ref, gl_ref, y_ref, mod_row):
        _interleave([sub_tile(x_ref, at_ref, gl_ref, y_ref, mod_row, r0)
                     for r0 in range(0, x_ref.shape[0], FFN_SUB)])

    t = pl.program_id(0)

    @pl.when(t < ctx_tiles)
    def _():
        tile(xp_ref, atp_ref, glp_ref, yp_ref, 0)

    @pl.when(t >= ctx_tiles)
    def _():
        tile(xs_ref, ats_ref, gls_ref, ys_ref, 1 + (t - ctx_tiles) // tiles_per_seq)


def _ffn(xp, xs, attn_p, attn_s, gla_p, gla_s, mod, wout, nf, wfi, wfo, fn, tm, tiles_per_seq):
    d = xp.shape[1]
    ctx_tiles = xp.shape[0] // tm
    lat_tiles = xs.shape[0] // tm
    ctx_map = lambda s: (jnp.minimum(s, ctx_tiles - 1), 0)
    lat_map = lambda s: (jnp.maximum(s - ctx_tiles, 0), 0)
    tile = lambda c, m: pl.BlockSpec((tm, c), m)
    return pl.pallas_call(
        functools.partial(_ffn_kernel, ctx_tiles=ctx_tiles, tiles_per_seq=tiles_per_seq),
        out_shape=[jax.ShapeDtypeStruct(xp.shape, F32), jax.ShapeDtypeStruct(xs.shape, F32)],
        grid=(ctx_tiles + lat_tiles,),
        in_specs=[tile(d, ctx_map), tile(d, lat_map), tile(VALL, ctx_map), tile(VALL, lat_map),
                  tile(GV, ctx_map), tile(GV, lat_map), _const_spec(mod.shape),
                  _const_spec(wout.shape), _const_spec(nf.shape), _const_spec(wfi.shape),
                  _const_spec(wfo.shape), _const_spec(fn.shape)],
        out_specs=[tile(d, ctx_map), tile(d, lat_map)],
        scratch_shapes=[pltpu.VMEM((tm, D_FF), BF16)],
        name="out_ffn",
        compiler_params=pltpu.CompilerParams(dimension_semantics=("arbitrary",),
                                             vmem_limit_bytes=VMEM_LIMIT),
    )(xp, xs, attn_p, attn_s, gla_p, gla_s, mod, wout, nf, wfi, wfo, fn)


def _rope_tables(n_tokens):
    t = np.arange(n_tokens)
    row = (t // GRID_W).astype(np.float32)
    col = (t % GRID_W).astype(np.float32)
    half = MLA_ROPE // 2
    inv = (np.float32(ROPE_BASE) ** (-np.arange(0, half, 2, dtype=np.float32) / np.float32(half))).astype(np.float32)
    ang_r = row[:, None] * inv
    ang_c = col[:, None] * inv
    ang = np.concatenate([ang_r, ang_r, ang_c, ang_c], axis=-1).astype(np.float32)
    cos, sin = np.cos(ang), np.sin(ang)
    first = (np.arange(MLA_ROPE) % half) < (half // 2)
    cos_t = np.ones((n_tokens, LANES), np.float32)
    sa_t = np.zeros((n_tokens, LANES), np.float32)
    sb_t = np.zeros((n_tokens, LANES), np.float32)
    cos_t[:, ROPE_LANE0:ROPE_LANE0 + MLA_ROPE] = cos
    sa_t[:, ROPE_LANE0:ROPE_LANE0 + MLA_ROPE] = np.where(first, -sin, 0.0)
    sb_t[:, ROPE_LANE0:ROPE_LANE0 + MLA_ROPE] = np.where(first, 0.0, sin)
    return jnp.asarray(cos_t), jnp.asarray(sa_t), jnp.asarray(sb_t)


def kernel(x_prompt, x_sample, cache_kv_latent, cache_k_rope, state_gla_fwd, state_gla_bwd, c, c_ctx, w_ada, b_ada, norm_attn, w_in, mla_q_norm, w_uq, mla_kv_norm, w_ukv, w_gate_f, b_gate_f, w_gate_b, b_gate_b, gla_norm, w_out, norm_ffn, w_ffn_in, w_ffn_out, final_norm):
    batch, seq, d = x_prompt.shape
    dec_batch, dec_seq, _ = x_sample.shape
    assert w_ada.shape[0] == 1 and w_in.shape[-1] == W_COLS and w_ffn_in.shape[-1] == 2 * D_FF
    l = 0

    mod = _ada(c_ctx, c, w_ada[l], b_ada[l])

    win, wuq, wk, wvt, wg = _prep_in_weights(w_in, w_uq, w_ukv, w_gate_f, w_gate_b)
    in_w = (norm_attn[l].reshape(1, d), win, mla_q_norm[l].reshape(1, Q_LORA), wuq,
            mla_kv_norm[l].reshape(1, KV_LORA), wk, wvt, wg, b_gate_f, b_gate_b)
    gn = gla_norm[l].reshape(1, GLA_DV)
    tm, tm_ffn = 512, 512
    r3 = lambda a, b_, t: a.reshape(b_, t, a.shape[-1])

    xp = x_prompt.reshape(batch * seq, d)
    (q, k, vt, gq, gk, gv, gf, gb, go, ckv, kr_t) = _inproj(xp, mod, lambda i: 0, in_w, None, tm, seq)
    (attn_p,) = _attention(r3(q, batch, seq), r3(k, batch, seq), vt, None, seq, ATTN_CTX_SEQS)
    gla_p, sf, sb = _gla(r3(gq, batch, seq), r3(gk, batch, seq), r3(gv, batch, seq), r3(gf, batch, seq),
                         r3(gb, batch, seq), r3(go, batch, seq), None, gn, GLA_CTX_SEQS)

    xs = x_sample.reshape(dec_batch * dec_seq, d)
    tiles = dec_seq // tm
    (q, k, vt, gq, gk, gv, gf, gb, go) = _inproj(xs, mod, lambda i: 1 + i // tiles, in_w,
                                                  _rope_tables(dec_seq), tm, dec_seq)
    kc, vct = _decomp(cache_kv_latent[:, l], jnp.swapaxes(cache_k_rope[:, l], 1, 2), wk, wvt)
    attn_s, wout, wfi, wfo = _attention(r3(q, dec_batch, dec_seq), r3(k, dec_batch, dec_seq), vt, (kc, vct),
                                        ATTN_LAT_QUERIES, 1, (w_out, w_ffn_in, w_ffn_out))
    gla_s, _, _ = _gla(r3(gq, dec_batch, dec_seq), r3(gk, dec_batch, dec_seq), r3(gv, dec_batch, dec_seq),
                       r3(gf, dec_batch, dec_seq), r3(gb, dec_batch, dec_seq), r3(go, dec_batch, dec_seq),
                       (state_gla_fwd[:, l].astype(F32), state_gla_bwd[:, l].astype(F32)), gn, 1)

    flat = lambda a: a.reshape(-1, a.shape[-1])
    y_prompt, y_sample = _ffn(xp, xs, flat(attn_p), flat(attn_s), flat(gla_p), flat(gla_s), mod,
                              wout, norm_ffn[l].reshape(1, d), wfi, wfo,
                              final_norm.reshape(1, d), tm_ffn, dec_seq // tm_ffn)
    y_prompt = y_prompt.reshape(batch, seq, d)
    y_sample = y_sample.reshape(dec_batch, dec_seq, d)

    new_kv_latent = ckv.reshape(batch, 1, seq, KV_LORA)
    new_k_rope = jnp.swapaxes(kr_t, 1, 2).reshape(batch, 1, seq, MLA_ROPE)
    new_state_fwd = sf.reshape(batch, 1, GLA_HEADS, GLA_DK, GLA_DV).astype(x_prompt.dtype)
    new_state_bwd = sb.reshape(batch, 1, GLA_HEADS, GLA_DK, GLA_DV).astype(x_prompt.dtype)
    return (y_prompt, y_sample, new_kv_latent, new_k_rope, new_state_fwd, new_state_bwd)
```

```python
import functools

import numpy as np
import jax
import jax.numpy as jnp
from jax import lax
from jax.experimental import pallas as pl
from jax.experimental.pallas import tpu as pltpu

F32 = jnp.float32
BF16 = jnp.bfloat16

GRID_W = 64
MLA_HEADS = 8
MLA_NOPE = 64
MLA_ROPE = 32
MLA_QK = MLA_NOPE + MLA_ROPE
MLA_V = 64
Q_LORA = 384
KV_LORA = 256
GLA_HEADS = 4
GLA_DK = 64
GLA_DV = 128
GATE_RANK = 16
GATE_NORM = 16.0
CHUNK = 64
D_FF = 2816
ROPE_BASE = 10000.0
EPS = 1e-6
LOG2_E = 1.4426950408889634

LANES = 128
HEAD_PAD = LANES
ROPE_LANE0 = MLA_NOPE
GQK = GLA_HEADS * GLA_DK
GV = GLA_HEADS * GLA_DV
QPAD = MLA_HEADS * HEAD_PAD
VALL = MLA_HEADS * MLA_V
ONES_ROWS = 16
KEY_BLOCK = 1024

W_KR = Q_LORA + KV_LORA
W_GQ = W_KR + MLA_ROPE
W_GF = W_GQ + 2 * GQK + GV
W_GO = W_GF + 2 * GATE_RANK
W_COLS = W_GO + GV

Z_Q = 0
Z_KV = Z_Q + Q_LORA
Z_GQ = Z_KV + KV_LORA
Z_GK = Z_GQ + GQK
Z_GV = Z_GK + GQK
Z_GO = Z_GV + GV
Z_MISC = Z_GO + GV
Z_COLS = Z_MISC + LANES

FF_CHUNK = 256
N_FF_CHUNKS = D_FF // FF_CHUNK
FFN_SUB = 256
FFN_PHASE_ENDS = (3, 7, 10)

GLA_TILE = 256
CHUNKS_PER_TILE = GLA_TILE // CHUNK
ADA_ROWS = 128
INPROJ_SUB = 512
Q_TILE = 256
ATTN_LAT_QUERIES = 512
ATTN_CTX_SEQS = 4
GLA_STATIC_TILES = 8
GLA_CTX_SEQS = 4

VMEM_LIMIT = 56 * 1024 * 1024

_NT = (((1,), (1,)), ((), ()))


def _rms(x, w):
    return x * lax.rsqrt(jnp.mean(x * x, axis=-1, keepdims=True) + EPS) * w


def _sigmoid(x):
    return 1.0 / (1.0 + jnp.exp(-x))


def _log_sigmoid(x):
    return jnp.minimum(x, 0.0) - jnp.log1p(jnp.exp(-jnp.abs(x)))


def _interleave(chains):
    pending, active = list(chains), []
    while pending or active:
        if pending:
            active.append(pending.pop(0))
        for chain in list(active):
            try:
                next(chain)
            except StopIteration:
                active.remove(chain)


def _const_spec(shape):
    nd = len(shape)
    return pl.BlockSpec(shape, lambda *_: (0,) * nd, pipeline_mode=pl.Buffered(1))


def _mod_rows(mod_ref, r):
    return [mod_ref[k, pl.ds(r, 1), :] for k in range(mod_ref.shape[0])]


def _ada_kernel(cctx_ref, c_ref, w_ref, b_ref, o_ref):
    k = pl.program_id(0)
    d = o_ref.shape[2]
    row = lax.broadcasted_iota(jnp.int32, (8, cctx_ref.shape[1]), 0)
    cond = jnp.where(row == 0, cctx_ref[...], 0.0)
    for r in range(c_ref.shape[0]):
        cond = jnp.where(row == 1 + r, c_ref[r:r + 1, :], cond)
    s = (cond * _sigmoid(cond)).astype(BF16)
    part = jnp.dot(s, w_ref[...].astype(BF16), preferred_element_type=F32)
    for j in range(o_ref.shape[0]):
        sl = slice(j * d, (j + 1) * d)

        @pl.when(k == 0)
        def _():
            o_ref[j] = part[:, sl] + b_ref[:, sl]

        @pl.when(k > 0)
        def _():
            o_ref[j] += part[:, sl]


def _ada(c_ctx, c, w_ada, b_ada):
    d = w_ada.shape[0]
    n = w_ada.shape[1]
    assert 1 + c.shape[0] <= 8
    return pl.pallas_call(
        _ada_kernel,
        out_shape=jax.ShapeDtypeStruct((n // d, 8, d), F32),
        grid=(d // ADA_ROWS,),
        in_specs=[pl.BlockSpec((1, ADA_ROWS), lambda k: (0, k)),
                  pl.BlockSpec((c.shape[0], ADA_ROWS), lambda k: (0, k)),
                  pl.BlockSpec((ADA_ROWS, n), lambda k: (k, 0)),
                  pl.BlockSpec((1, n), lambda k: (0, 0))],
        out_specs=pl.BlockSpec((n // d, 8, d), lambda k: (0, 0, 0)),
        name="ada_mod",
        compiler_params=pltpu.CompilerParams(dimension_semantics=("arbitrary",)),
    )(c_ctx.reshape(1, d), c, w_ada, b_ada.reshape(1, n))


def _prep_kernel(wint_ref, wuq_ref, wukv_ref, wgf_ref, wgb_ref, win_o, wuq_o, wk_o, wvt_o, wg_o):
    cols = wint_ref.shape[1]
    for dst, src, n in ((Z_Q, 0, W_KR), (Z_GQ, W_GQ, W_GF - W_GQ), (Z_GO, W_GO, GV)):
        win_o[:, dst:dst + n] = wint_ref[src:src + n, :].T.astype(BF16)
    z32 = jnp.zeros((32, cols), F32)
    misc_t = jnp.concatenate([wint_ref[W_GF:W_GO, :], z32, wint_ref[W_KR:W_GQ, :], z32], axis=0)
    win_o[:, Z_MISC:Z_COLS] = misc_t.T.astype(BF16)

    u = wuq_ref[...]
    zq = jnp.zeros((u.shape[0], HEAD_PAD - MLA_QK), F32)
    for hd in range(MLA_HEADS):
        blk = jnp.concatenate([u[:, hd * MLA_QK:(hd + 1) * MLA_QK], zq], axis=1)
        wuq_o[:, hd * HEAD_PAD:(hd + 1) * HEAD_PAD] = blk.astype(BF16)

    @pl.when(pl.program_id(0) == 0)
    def _():
        kv = wukv_ref[...]
        per = MLA_NOPE + MLA_V
        lane = lax.broadcasted_iota(jnp.int32, (kv.shape[0], per), 1)
        for hd in range(MLA_HEADS):
            blk = kv[:, hd * per:(hd + 1) * per]
            wk_o[:, hd * HEAD_PAD:(hd + 1) * HEAD_PAD] = jnp.where(lane < MLA_NOPE, blk, 0.0).astype(BF16)
        wv = jnp.concatenate([kv[:, hd * per + MLA_NOPE:(hd + 1) * per] for hd in range(MLA_HEADS)], axis=1)
        wvt_o[...] = wv.T.astype(BF16)

        wg_o[...] = jnp.zeros(wg_o.shape, BF16)
        wg_o[0:GATE_RANK, 0:GQK] = wgf_ref[...].astype(BF16)
        wg_o[GATE_RANK:2 * GATE_RANK, GQK:2 * GQK] = wgb_ref[...].astype(BF16)


def _prep_in_weights(w_in, w_uq, w_ukv, w_gate_f, w_gate_b):
    d = w_in.shape[1]
    steps = 4
    w_in_t = jnp.swapaxes(w_in, 1, 2)
    rb3 = lambda r, c: pl.BlockSpec((None, r // steps, c), lambda i: (0, i, 0))
    rb = lambda r, c: pl.BlockSpec((r // steps, c), lambda i: (i, 0))
    full3 = lambda shape: pl.BlockSpec((None,) + tuple(shape[1:]), lambda i: (0, 0, 0))
    full = lambda shape: pl.BlockSpec(shape, lambda i: (0, 0))
    return pl.pallas_call(
        _prep_kernel,
        out_shape=[jax.ShapeDtypeStruct((d, Z_COLS), BF16),
                   jax.ShapeDtypeStruct((Q_LORA, QPAD), BF16),
                   jax.ShapeDtypeStruct((KV_LORA, QPAD), BF16),
                   jax.ShapeDtypeStruct((VALL, KV_LORA), BF16),
                   jax.ShapeDtypeStruct((LANES, 2 * GQK), BF16)],
        grid=(steps,),
        in_specs=[pl.BlockSpec((None, W_COLS, d // steps), lambda i: (0, 0, i)),
                  rb3(Q_LORA, MLA_HEADS * MLA_QK), full3(w_ukv.shape),
                  full3(w_gate_f.shape), full3(w_gate_b.shape)],
        out_specs=[rb(d, Z_COLS), rb(Q_LORA, QPAD), full((KV_LORA, QPAD)), full((VALL, KV_LORA)),
                   full((LANES, 2 * GQK))],
        name="weight_prep",
        compiler_params=pltpu.CompilerParams(dimension_semantics=("arbitrary",)),
    )(w_in_t, w_uq, w_ukv, w_gate_f, w_gate_b)


def _inproj_kernel(*refs, latent, mod_row):
    (x_ref, mod_ref, nw_ref, win_ref, qn_ref, wuq_ref, kvn_ref, wk_ref, wvt_ref, wg_ref, bgf_ref, bgb_ref) = refs[:12]
    if latent:
        cos_ref, sa_ref, sb_ref = refs[12:15]
        outs = refs[15:]
    else:
        outs = refs[12:]
    q_ref, k_ref, vt_ref, gq_ref, gk_ref, gv_ref, gf_ref, gb_ref, go_ref = outs[:9]

    sh1, sc1 = _mod_rows(mod_ref, mod_row(pl.program_id(0)))[:2]
    scale = MLA_QK ** -0.5 * LOG2_E
    lane = lax.broadcasted_iota(jnp.int32, (INPROJ_SUB, LANES), 1)
    in_rope = (lane >= ROPE_LANE0) & (lane < ROPE_LANE0 + MLA_ROPE)

    def sub_tile(r0):
        rows = slice(r0, r0 + INPROJ_SUB)
        h = (_rms(x_ref[rows, :], nw_ref[...]) * (1.0 + sc1) + sh1).astype(BF16)
        yield
        z_all = jnp.dot(h, win_ref[...], preferred_element_type=F32)
        z = lambda lo, n: z_all[:, lo:lo + n]
        yield
        qn = _rms(z(Z_Q, Q_LORA), qn_ref[...]).astype(BF16)
        ckv = _rms(z(Z_KV, KV_LORA), kvn_ref[...])
        ckv_b = ckv.astype(BF16)
        misc = z(Z_MISC, LANES)
        yield
        q = jnp.dot(qn, wuq_ref[...], preferred_element_type=F32)
        kn = jnp.dot(ckv_b, wk_ref[...], preferred_element_type=F32)
        vt_ref[:, rows] = lax.dot_general(wvt_ref[...], ckv_b, _NT,
                                          preferred_element_type=F32).astype(BF16)
        gpre = jnp.dot(misc.astype(BF16), wg_ref[...], preferred_element_type=F32)
        yield
        if latent:
            cos, sa, sb = cos_ref[rows, :], sa_ref[rows, :], sb_ref[rows, :]

            def rope(t):
                return t * cos + pltpu.roll(t, LANES - 8, 1) * sa + pltpu.roll(t, 8, 1) * sb
        else:
            def rope(t):
                return t

        krope = rope(misc)
        for hd in range(MLA_HEADS):
            sl = slice(hd * HEAD_PAD, (hd + 1) * HEAD_PAD)
            q_ref[rows, sl] = (rope(q[:, sl]) * scale).astype(BF16)
            k_ref[rows, sl] = jnp.where(in_rope, krope, kn[:, sl]).astype(BF16)
        gq_ref[rows, :] = z(Z_GQ, GQK)
        gk_ref[rows, :] = z(Z_GK, GQK)
        gv_ref[rows, :] = z(Z_GV, GV).astype(BF16)
        go_ref[rows, :] = z(Z_GO, GV)
        gf_ref[rows, :] = _log_sigmoid(gpre[:, :GQK] + bgf_ref[...]) * (1.0 / GATE_NORM)
        gb_ref[rows, :] = _log_sigmoid(gpre[:, GQK:] + bgb_ref[...]) * (1.0 / GATE_NORM)
        if not latent:
            ckv_ref, krt_ref = outs[9:]
            ckv_ref[rows, :] = ckv
            misc_t = misc.T
            n = krt_ref.shape[2]
            for b in range(INPROJ_SUB // n):
                krt_ref[r0 // n + b] = misc_t[ROPE_LANE0:ROPE_LANE0 + MLA_ROPE, b * n:(b + 1) * n]

    _interleave([sub_tile(r0) for r0 in range(0, x_ref.shape[0], INPROJ_SUB)])


def _inproj(x2d, mod, mod_row, weights, rope_tabs, tm, seq_len):
    n_tok, d = x2d.shape
    latent = rope_tabs is not None
    tiles_per_seq = max(seq_len // tm, 1)
    nw, win, qn, wuq, kvn, wk, wvt, wg, bgf, bgb = weights
    row = lambda i: (i, 0)
    in_specs = [pl.BlockSpec((tm, d), row), _const_spec(mod.shape),
                _const_spec(nw.shape), _const_spec(win.shape), _const_spec(qn.shape),
                _const_spec(wuq.shape), _const_spec(kvn.shape), _const_spec(wk.shape),
                _const_spec(wvt.shape), _const_spec(wg.shape), _const_spec(bgf.shape),
                _const_spec(bgb.shape)]
    args = [x2d, mod, nw, win, qn, wuq, kvn, wk, wvt, wg, bgf, bgb]
    if latent:
        tab = pl.BlockSpec((tm, LANES), lambda i: (i % tiles_per_seq, 0))
        in_specs += [tab, tab, tab]
        args += list(rope_tabs)
    out_cols = [(QPAD, BF16), (QPAD, BF16), None, (GQK, F32), (GQK, F32), (GV, BF16),
                (GQK, F32), (GQK, F32), (GV, F32)]
    if not latent:
        out_cols += [(KV_LORA, F32)]
    out_shape = [jax.ShapeDtypeStruct((n_tok, oc[0]), oc[1]) if oc else
                 jax.ShapeDtypeStruct((VALL, n_tok), BF16) for oc in out_cols]
    out_specs = [pl.BlockSpec((tm, oc[0]), row) if oc else
                 pl.BlockSpec((VALL, tm), lambda i: (0, i)) for oc in out_cols]
    if not latent:
        seq = seq_len
        out_shape.append(jax.ShapeDtypeStruct((n_tok // seq, MLA_ROPE, seq), F32))
        out_specs.append(pl.BlockSpec((tm // seq, MLA_ROPE, seq), lambda i: (i, 0, 0)))
    return pl.pallas_call(
        functools.partial(_inproj_kernel, latent=latent, mod_row=mod_row),
        out_shape=out_shape,
        grid=(n_tok // tm,),
        in_specs=in_specs,
        out_specs=out_specs,
        name="inproj_lat" if latent else "inproj_ctx",
        compiler_params=pltpu.CompilerParams(dimension_semantics=("arbitrary",),
                                             vmem_limit_bytes=VMEM_LIMIT),
    )(*args)


def _decomp_kernel(ckv_ref, krt_ref, wk_ref, wvt_ref, k_ref, vt_ref):
    ckv_b = ckv_ref[...].astype(BF16)
    kn = jnp.dot(ckv_b, wk_ref[...], preferred_element_type=F32)
    n_keys = krt_ref.shape[1]
    kr = jnp.concatenate([jnp.zeros((ROPE_LANE0, n_keys), F32), krt_ref[...],
                          jnp.zeros((LANES - ROPE_LANE0 - MLA_ROPE, n_keys), F32)], axis=0).T
    lane = lax.broadcasted_iota(jnp.int32, kr.shape, 1)
    in_rope = (lane >= ROPE_LANE0) & (lane < ROPE_LANE0 + MLA_ROPE)
    for hd in range(MLA_HEADS):
        sl = slice(hd * HEAD_PAD, (hd + 1) * HEAD_PAD)
        k_ref[:, sl] = jnp.where(in_rope, kr, kn[:, sl]).astype(BF16)
    vt_ref[...] = lax.dot_general(wvt_ref[...], ckv_b, _NT, preferred_element_type=F32).astype(BF16)


def _decomp(ckv, kr_t, wk, wvt):
    b, s, _ = ckv.shape
    return pl.pallas_call(
        _decomp_kernel,
        out_shape=[jax.ShapeDtypeStruct((b, s, QPAD), BF16), jax.ShapeDtypeStruct((VALL, b * s), BF16)],
        grid=(b,),
        in_specs=[pl.BlockSpec((None, s, KV_LORA), lambda i: (i, 0, 0)),
                  pl.BlockSpec((None, MLA_ROPE, s), lambda i: (i, 0, 0)),
                  _const_spec(wk.shape), _const_spec(wvt.shape)],
        out_specs=[pl.BlockSpec((None, s, QPAD), lambda i: (i, 0, 0)),
                   pl.BlockSpec((VALL, s), lambda i: (0, i))],
        name="ctx_decompress",
        compiler_params=pltpu.CompilerParams(dimension_semantics=("arbitrary",)),
    )(ckv, kr_t, wk, wvt)


def _attn_kernel(*refs, has_ctx, n_seqs, n_side):
    n_in = 5 if has_ctx else 3
    side_in, refs = refs[n_in:n_in + n_side], refs[:n_in] + refs[n_in + n_side:]
    side_out, refs = refs[n_in + 1:n_in + 1 + n_side], refs[:n_in + 1] + refs[n_in + 1 + n_side:]
    if has_ctx:
        q_ref, kc_ref, vct_ref, k_ref, vt_ref, o_ref, st_ref, p_ref = refs
    else:
        q_ref, k_ref, vt_ref, o_ref, st_ref, p_ref = refs
    tq = Q_TILE

    for src, dst in zip(side_in, side_out):
        dst[...] = src[...].astype(BF16)

    def key_blocks(bi):
        srcs = [(kc_ref, vct_ref)] if has_ctx else []
        blocks, row0 = [], 0
        for kr, vr in srcs + [(k_ref, vt_ref)]:
            n_keys = kr.shape[1]
            size = min(KEY_BLOCK, n_keys)
            for r in range(0, n_keys, size):
                blocks.append((kr, vr, r, bi * n_keys + r, size, row0))
                row0 += size
        return blocks

    units = [(bi, slice(q0, q0 + tq), hd) for bi in range(n_seqs)
             for q0 in range(0, q_ref.shape[1], tq) for hd in range(MLA_HEADS)]
    col_max = [None] * len(units)
    pair = []
    for stage in range(len(units) + 2):
        ua, ub, uc = stage, stage - 1, stage - 2
        run_max = None
        acc = jnp.zeros((MLA_V + ONES_ROWS, tq), F32)
        for j in range(len(key_blocks(0))):
            if ua < len(units):
                bi, qrows, hd = units[ua]
                kr, _, r0, _, size, srow = key_blocks(bi)[j]
                sl = slice(hd * HEAD_PAD, (hd + 1) * HEAD_PAD)
                st = lax.dot_general(kr[bi, r0:r0 + size, sl], q_ref[bi, qrows, sl], _NT,
                                     preferred_element_type=F32)
                st_ref[ua % 2, srow:srow + size, :] = st
                blk_max = jnp.max(st.reshape(size // 8, 8, tq), axis=0)
                run_max = blk_max if run_max is None else jnp.maximum(run_max, blk_max)
            if 0 <= ub < len(units):
                _, _, _, _, size, srow = key_blocks(0)[j]
                p_ref[ub % 2, srow:srow + size, :] = jnp.exp2(
                    st_ref[ub % 2, srow:srow + size, :] - col_max[ub]).astype(BF16)
        for j in range(len(key_blocks(0))):
            if uc >= 0:
                bi, _, hd = units[uc]
                _, vr, _, c0, size, srow = key_blocks(bi)[j]
                v_aug = jnp.concatenate([vr[hd * MLA_V:(hd + 1) * MLA_V, c0:c0 + size],
                                         jnp.ones((ONES_ROWS, size), BF16)], axis=0)
                acc = acc + jnp.dot(v_aug, p_ref[uc % 2, srow:srow + size, :],
                                    preferred_element_type=F32)
        if ua < len(units):
            col_max[ua] = jnp.max(run_max, axis=0, keepdims=True)
        if uc >= 0:
            bi, qrows, hd = units[uc]
            pair.append(acc[:MLA_V, :] / acc[MLA_V:MLA_V + 1, :])
            if len(pair) == 2:
                o_ref[bi, qrows, (hd - 1) * MLA_V:(hd + 1) * MLA_V] = (
                    jnp.concatenate(pair, axis=0).T.astype(BF16))
                pair = []


def _attention(q, k, vt, ctx_kv, tq, n_seqs, side_weights=()):
    b, t, _ = q.shape
    steps = (b // n_seqs) * (t // tq)
    assert (n_seqs == 1 or tq == t) and tq % Q_TILE == 0
    has_ctx = ctx_kv is not None
    in_specs = [pl.BlockSpec((n_seqs, tq, QPAD), lambda i, j: (i, j, 0))]
    args = [q]
    if has_ctx:
        kc, vct = ctx_kv
        s = kc.shape[1]
        in_specs += [pl.BlockSpec((n_seqs, s, QPAD), lambda i, j: (i, 0, 0)),
                     pl.BlockSpec((VALL, n_seqs * s), lambda i, j: (0, i))]
        args += [kc, vct]
    in_specs += [pl.BlockSpec((n_seqs, t, QPAD), lambda i, j: (i, 0, 0)),
                 pl.BlockSpec((VALL, n_seqs * t), lambda i, j: (0, i))]
    args += [k, vt]
    out_shape = [jax.ShapeDtypeStruct((b, t, VALL), BF16)]
    out_specs = [pl.BlockSpec((n_seqs, tq, VALL), lambda i, j: (i, j, 0))]
    nj = t // tq
    for w in side_weights:
        _, rows, cols = w.shape
        assert rows % (16 * steps) == 0
        in_specs.append(pl.BlockSpec((None, rows // steps, cols), lambda i, j: (0, i * nj + j, 0)))
        args.append(w)
        out_shape.append(jax.ShapeDtypeStruct((rows, cols), BF16))
        out_specs.append(pl.BlockSpec((rows // steps, cols), lambda i, j: (i * nj + j, 0)))
    return pl.pallas_call(
        functools.partial(_attn_kernel, has_ctx=has_ctx, n_seqs=n_seqs, n_side=len(side_weights)),
        out_shape=out_shape,
        grid=(b // n_seqs, t // tq),
        in_specs=in_specs,
        out_specs=out_specs,
        scratch_shapes=[pltpu.VMEM((2, t + (s if has_ctx else 0), Q_TILE), F32),
                        pltpu.VMEM((2, t + (s if has_ctx else 0), Q_TILE), BF16)],
        name="mla_attn_lat" if has_ctx else "mla_attn_ctx",
        compiler_params=pltpu.CompilerParams(dimension_semantics=("arbitrary", "arbitrary"),
                                             vmem_limit_bytes=VMEM_LIMIT),
    )(*args)


def _gla_kernel(*refs, n_tiles, n_seqs, zero_init):
    gq_ref, gk_ref, gv_ref, gf_ref, gb_ref, go_ref = refs[:6]
    if zero_init:
        gn_ref, o_ref, sf_ref, sb_ref, oacc_ref, bdqk_ref, tri_ref, hm_ref = refs[6:]
    else:
        (sf0_ref, sb0_ref, gn_ref, o_ref, sf_ref, sb_ref, oacc_ref,
         bdqk_ref, tri_ref, hm_ref) = refs[6:]
    g_refs = (gf_ref, gb_ref)
    state_refs = (sf_ref, sb_ref)

    @pl.when(pl.program_id(0) == 0)
    def _():
        ri = lax.broadcasted_iota(jnp.int32, (GLA_TILE, GLA_TILE), 0)
        ci = lax.broadcasted_iota(jnp.int32, (GLA_TILE, GLA_TILE), 1)
        same_chunk = (ri // CHUNK) == (ci // CHUNK)
        bdqk_ref[...] = jnp.where(same_chunk, 1.0, 0.0).astype(BF16)
        tri_ref[0] = jnp.where(same_chunk & (ri >= ci), 1.0, 0.0)
        tri_ref[1] = jnp.where(same_chunk & (ci >= ri), 1.0, 0.0)
        hm_ref[...] = jnp.where(
            lax.broadcasted_iota(jnp.int32, (GLA_HEADS * GLA_TILE, GQK), 0) // GLA_TILE
            == lax.broadcasted_iota(jnp.int32, (GLA_HEADS * GLA_TILE, GQK), 1) // GLA_DK,
            1.0, 0.0).astype(BF16)

    row8 = lax.broadcasted_iota(jnp.int32, (8, GQK), 0)

    def row_slice(start, size):
        return pl.ds(start if isinstance(start, int) else pl.multiple_of(start, size), size)

    def tile_rows(t):
        return row_slice(t * GLA_TILE, GLA_TILE)

    finished = set()

    def total_row(c, d):
        return c * CHUNK + (CHUNK - 1 if d == 0 else 0)

    def tile_dir(b, t, d):
        rows = tile_rows(t)
        g = g_refs[d][b, rows, :]
        g_hi = g.astype(BF16)
        g_lo = (g - g_hi.astype(F32)).astype(BF16)
        tri_b = tri_ref[d].astype(BF16)
        cum = (jnp.dot(tri_b, g_hi, preferred_element_type=F32)
               + jnp.dot(tri_b, g_lo, preferred_element_type=F32))
        yield
        totals = [cum[total_row(c, d):total_row(c, d) + 1, :] for c in range(CHUNKS_PER_TILE)]
        tot8 = jnp.zeros((8, GQK), F32)
        for c in range(CHUNKS_PER_TILE):
            tot8 = jnp.where(row8 == c, totals[c], tot8)
        dec_t = jnp.concatenate([jnp.exp(tot8), jnp.zeros((LANES - 8, GQK), F32)], axis=0).T
        q = gq_ref[b, rows, :] * (GLA_DK ** -0.5)
        k = gk_ref[b, rows, :]
        v = gv_ref[b, rows, :]
        tot = jnp.concatenate([jnp.broadcast_to(tc, (CHUNK, GQK)) for tc in totals], axis=0)
        qe = (q * jnp.exp(cum)).astype(BF16)
        ke = (k * jnp.exp(-cum)).astype(BF16)
        kd_t = (k * jnp.exp(tot - cum)).T.astype(BF16)
        bd_qk = bdqk_ref[...] > 0
        tri = tri_ref[d] > 0

        qm = jnp.where(hm_ref[...] > 0, jnp.tile(qe, (GLA_HEADS, 1)), 0.0)
        yield
        att = lax.dot_general(qm, ke, _NT, preferred_element_type=F32)

        yield
        intra, upd = [], []
        for hd in range(GLA_HEADS):
            vh = v[:, hd * GLA_DV:(hd + 1) * GLA_DV]
            a_h = jnp.where(tri, att[hd * GLA_TILE:(hd + 1) * GLA_TILE, :], 0.0).astype(BF16)
            intra.append(jnp.dot(a_h, vh, preferred_element_type=F32))
            kd_h = jnp.tile(kd_t[hd * GLA_DK:(hd + 1) * GLA_DK, :], (CHUNKS_PER_TILE, 1))
            upd.append(jnp.dot(jnp.where(bd_qk, kd_h, 0.0), vh, preferred_element_type=F32))

        yield
        state = [state_refs[d][b, hd] for hd in range(GLA_HEADS)]
        order = range(CHUNKS_PER_TILE) if d == 0 else range(CHUNKS_PER_TILE - 1, -1, -1)
        seen = {}
        for c in order:
            seen[c] = jnp.concatenate(state, axis=0).astype(BF16)
            decay = jnp.broadcast_to(dec_t[:, c:c + 1], (GQK, GLA_DV))
            for hd in range(GLA_HEADS):
                ks = slice(hd * GLA_DK, (hd + 1) * GLA_DK)
                state[hd] = decay[ks, :] * state[hd] + upd[hd][c * CHUNK:(c + 1) * CHUNK, :]
        for hd in range(GLA_HEADS):
            state_refs[d][b, hd] = state[hd]

        yield
        for c in range(CHUNKS_PER_TILE):
            cr = slice(c * CHUNK, (c + 1) * CHUNK)
            q_c = jnp.concatenate([qm[hd * GLA_TILE + c * CHUNK:hd * GLA_TILE + (c + 1) * CHUNK, :]
                                   for hd in range(GLA_HEADS)], axis=0)
            inter = jnp.dot(q_c, seen[c], preferred_element_type=F32)
            o = jnp.concatenate([intra[hd][cr, :] + inter[hd * CHUNK:(hd + 1) * CHUNK, :]
                                 for hd in range(GLA_HEADS)], axis=1)
            oacc_ref[d, b, row_slice(t * GLA_TILE + c * CHUNK, CHUNK), :] = o
        if isinstance(t, int):
            finished.add((b, t, d))

    if zero_init:
        sf_ref[...] = jnp.zeros(sf_ref.shape, F32)
        sb_ref[...] = jnp.zeros(sb_ref.shape, F32)
    else:
        sf_ref[...] = sf0_ref[...]
        sb_ref[...] = sb0_ref[...]

    gn = gn_ref[...]

    def epilogue(t):
        while isinstance(t, int) and not all((b, t, d) in finished for b in range(n_seqs) for d in range(2)):
            yield
        rows = tile_rows(t)
        for b in range(n_seqs):
            for hd in range(GLA_HEADS):
                vs = slice(hd * GLA_DV, (hd + 1) * GLA_DV)
                o = _rms(oacc_ref[0, b, rows, vs] + oacc_ref[1, b, rows, vs], gn)
                go = go_ref[b, rows, vs]
                o_ref[b, rows, vs] = (o * (go * _sigmoid(go))).astype(BF16)
        return
        yield

    def main_chains(first_tile, n):
        chains = []
        for t in [first_tile + u for u in range(n)]:
            for b in range(n_seqs):
                chains += [tile_dir(b, t, 0), tile_dir(b, n_tiles - 1 - t, 1)]
        return chains

    if n_tiles <= GLA_STATIC_TILES:
        _interleave(main_chains(0, n_tiles) + [epilogue(t) for t in range(n_tiles)])
    else:
        per_step = next(c for c in (4, 2, 1) if n_tiles % c == 0)

        def main_body(i, carry):
            _interleave(main_chains(i * per_step, per_step))
            return carry

        lax.fori_loop(0, n_tiles // per_step, main_body, 0)

        def epilogue_body(t, carry):
            _interleave([epilogue(t)])
            return carry

        lax.fori_loop(0, n_tiles, epilogue_body, 0)


def _gla(gq, gk, gv, gf, gb, go, init_states, gn, n_seqs):
    b, t, _ = gq.shape
    n_tiles = t // GLA_TILE
    zero_init = init_states is None
    seq = lambda c: pl.BlockSpec((n_seqs, t, c), lambda i: (i, 0, 0))
    st = pl.BlockSpec((n_seqs, GLA_HEADS, GLA_DK, GLA_DV), lambda i: (i, 0, 0, 0))
    in_specs = [seq(GQK), seq(GQK), seq(GV), seq(GQK), seq(GQK), seq(GV)]
    args = [gq, gk, gv, gf, gb, go]
    if not zero_init:
        in_specs += [st, st]
        args += list(init_states)
    in_specs.append(_const_spec(gn.shape))
    args.append(gn)
    return pl.pallas_call(
        functools.partial(_gla_kernel, n_tiles=n_tiles, n_seqs=n_seqs, zero_init=zero_init),
        out_shape=[jax.ShapeDtypeStruct((b, t, GV), BF16),
                   jax.ShapeDtypeStruct((b, GLA_HEADS, GLA_DK, GLA_DV), F32),
                   jax.ShapeDtypeStruct((b, GLA_HEADS, GLA_DK, GLA_DV), F32)],
        grid=(b // n_seqs,),
        in_specs=in_specs,
        out_specs=[seq(GV), st, st],
        scratch_shapes=[pltpu.VMEM((2, n_seqs, t, GV), F32),
                        pltpu.VMEM((GQK, GLA_TILE), BF16),
                        pltpu.VMEM((2, GLA_TILE, GLA_TILE), F32),
                        pltpu.VMEM((GLA_HEADS * GLA_TILE, GQK), BF16)],
        name="gla_%d" % t,
        compiler_params=pltpu.CompilerParams(dimension_semantics=("arbitrary",),
                                             vmem_limit_bytes=VMEM_LIMIT),
    )(*args)


def _ffn_kernel(xp_ref, xs_ref, atp_ref, ats_ref, glp_ref, gls_ref, mod_ref, wout_ref, nf_ref,
                wfi_ref, wfo_ref, fn_ref, yp_ref, ys_ref, act_ref, *, ctx_tiles, tiles_per_seq):
    def sub_tile(x_ref, at_ref, gl_ref, y_ref, mod_row, r0):
        rows = slice(r0, r0 + FFN_SUB)
        gt1, sh2, sc2, gt2 = _mod_rows(mod_ref, mod_row)[2:]
        mix = (jnp.dot(at_ref[rows, :], wout_ref[0:VALL, :], preferred_element_type=F32)
               + jnp.dot(gl_ref[rows, :], wout_ref[VALL:, :], preferred_element_type=F32))
        yield
        x1 = x_ref[rows, :] + gt1 * mix
        h2 = (_rms(x1, nf_ref[...]) * (1.0 + sc2) + sh2).astype(BF16)
        yield
        for j in range(N_FF_CHUNKS):
            cs = slice(j * FF_CHUNK, (j + 1) * FF_CHUNK)
            a = jnp.dot(h2, wfi_ref[:, cs], preferred_element_type=F32)
            g = jnp.dot(h2, wfi_ref[:, D_FF + j * FF_CHUNK:D_FF + (j + 1) * FF_CHUNK],
                        preferred_element_type=F32)
            act_ref[rows, cs] = (a * _sigmoid(a) * g).astype(BF16)
            if j in FFN_PHASE_ENDS:
                yield
        ff = jnp.dot(act_ref[rows, :], wfo_ref[...], preferred_element_type=F32)
        yield
        x2 = x1 + gt2 * ff
        y_ref[rows, :] = _rms(x2, fn_ref[...])

    def tile(x_ref, at_ref, gl_ref, y_ref, mod_row):
        _interleave([sub_tile(x_ref, at_ref, gl_ref, y_ref, mod_row, r0)
                     for r0 in range(0, x_ref.shape[0], FFN_SUB)])

    t = pl.program_id(0)

    @pl.when(t < ctx_tiles)
    def _():
        tile(xp_ref, atp_ref, glp_ref, yp_ref, 0)

    @pl.when(t >= ctx_tiles)
    def _():
        tile(xs_ref, ats_ref, gls_ref, ys_ref, 1 + (t - ctx_tiles) // tiles_per_seq)


def _ffn(xp, xs, attn_p, attn_s, gla_p, gla_s, mod, wout, nf, wfi, wfo, fn, tm, tiles_per_seq):
    d = xp.shape[1]
    ctx_tiles = xp.shape[0] // tm
    lat_tiles = xs.shape[0] // tm
    ctx_map = lambda s: (jnp.minimum(s, ctx_tiles - 1), 0)
    lat_map = lambda s: (jnp.maximum(s - ctx_tiles, 0), 0)
    tile = lambda c, m: pl.BlockSpec((tm, c), m)
    return pl.pallas_call(
        functools.partial(_ffn_kernel, ctx_tiles=ctx_tiles, tiles_per_seq=tiles_per_seq),
        out_shape=[jax.ShapeDtypeStruct(xp.shape, F32), jax.ShapeDtypeStruct(xs.shape, F32)],
        grid=(ctx_tiles + lat_tiles,),
        in_specs=[tile(d, ctx_map), tile(d, lat_map), tile(VALL, ctx_map), tile(VALL, lat_map),
                  tile(GV, ctx_map), tile(GV, lat_map), _const_spec(mod.shape),
                  _const_spec(wout.shape), _const_spec(nf.shape), _const_spec(wfi.shape),
                  _const_spec(wfo.shape), _const_spec(fn.shape)],
        out_specs=[tile(d, ctx_map), tile(d, lat_map)],
        scratch_shapes=[pltpu.VMEM((tm, D_FF), BF16)],
        name="out_ffn",
        compiler_params=pltpu.CompilerParams(dimension_semantics=("arbitrary",),
                                             vmem_limit_bytes=VMEM_LIMIT),
    )(xp, xs, attn_p, attn_s, gla_p, gla_s, mod, wout, nf, wfi, wfo, fn)


def _rope_tables(n_tokens):
    t = np.arange(n_tokens)
    row = (t // GRID_W).astype(np.float32)
    col = (t % GRID_W).astype(np.float32)
    half = MLA_ROPE // 2
    inv = (np.float32(ROPE_BASE) ** (-np.arange(0, half, 2, dtype=np.float32) / np.float32(half))).astype(np.float32)
    ang_r = row[:, None] * inv
    ang_c = col[:, None] * inv
    ang = np.concatenate([ang_r, ang_r, ang_c, ang_c], axis=-1).astype(np.float32)
    cos, sin = np.cos(ang), np.sin(ang)
    first = (np.arange(MLA_ROPE) % half) < (half // 2)
    cos_t = np.ones((n_tokens, LANES), np.float32)
    sa_t = np.zeros((n_tokens, LANES), np.float32)
    sb_t = np.zeros((n_tokens, LANES), np.float32)
    cos_t[:, ROPE_LANE0:ROPE_LANE0 + MLA_ROPE] = cos
    sa_t[:, ROPE_LANE0:ROPE_LANE0 + MLA_ROPE] = np.where(first, -sin, 0.0)
    sb_t[:, ROPE_LANE0:ROPE_LANE0 + MLA_ROPE] = np.where(first, 0.0, sin)
    return jnp.asarray(cos_t), jnp.asarray(sa_t), jnp.asarray(sb_t)


def kernel(x_prompt, x_sample, cache_kv_latent, cache_k_rope, state_gla_fwd, state_gla_bwd, c, c_ctx, w_ada, b_ada, norm_attn, w_in, mla_q_norm, w_uq, mla_kv_norm, w_ukv, w_gate_f, b_gate_f, w_gate_b, b_gate_b, gla_norm, w_out, norm_ffn, w_ffn_in, w_ffn_out, final_norm):
    batch, seq, d = x_prompt.shape
    dec_batch, dec_seq, _ = x_sample.shape
    assert w_ada.shape[0] == 1 and w_in.shape[-1] == W_COLS and w_ffn_in.shape[-1] == 2 * D_FF
    l = 0

    mod = _ada(c_ctx, c, w_ada[l], b_ada[l])

    win, wuq, wk, wvt, wg = _prep_in_weights(w_in, w_uq, w_ukv, w_gate_f, w_gate_b)
    in_w = (norm_attn[l].reshape(1, d), win, mla_q_norm[l].reshape(1, Q_LORA), wuq,
            mla_kv_norm[l].reshape(1, KV_LORA), wk, wvt, wg, b_gate_f, b_gate_b)
    gn = gla_norm[l].reshape(1, GLA_DV)
    tm, tm_ffn = 512, 512
    r3 = lambda a, b_, t: a.reshape(b_, t, a.shape[-1])

    xp = x_prompt.reshape(batch * seq, d)
    (q, k, vt, gq, gk, gv, gf, gb, go, ckv, kr_t) = _inproj(xp, mod, lambda i: 0, in_w, None, tm, seq)
    (attn_p,) = _attention(r3(q, batch, seq), r3(k, batch, seq), vt, None, seq, ATTN_CTX_SEQS)
    gla_p, sf, sb = _gla(r3(gq, batch, seq), r3(gk, batch, seq), r3(gv, batch, seq), r3(gf, batch, seq),
                         r3(gb, batch, seq), r3(go, batch, seq), None, gn, GLA_CTX_SEQS)

    xs = x_sample.reshape(dec_batch * dec_seq, d)
    tiles = dec_seq // tm
    (q, k, vt, gq, gk, gv, gf, gb, go) = _inproj(xs, mod, lambda i: 1 + i // tiles, in_w,
                                                  _rope_tables(dec_seq), tm, dec_seq)
    kc, vct = _decomp(cache_kv_latent[:, l], jnp.swapaxes(cache_k_rope[:, l], 1, 2), wk, wvt)
    attn_s, wout, wfi, wfo = _attention(r3(q, dec_batch, dec_seq), r3(k, dec_batch, dec_seq), vt, (kc, vct),
                                        ATTN_LAT_QUERIES, 1, (w_out, w_ffn_in, w_ffn_out))
    gla_s, _, _ = _gla(r3(gq, dec_batch, dec_seq), r3(gk, dec_batch, dec_seq), r3(gv, dec_batch, dec_seq),
                       r3(gf, dec_batch, dec_seq), r3(gb, dec_batch, dec_seq), r3(go, dec_batch, dec_seq),
                       (state_gla_fwd[:, l].astype(F32), state_gla_bwd[:, l].astype(F32)), gn, 1)

    flat = lambda a: a.reshape(-1, a.shape[-1])
    y_prompt, y_sample = _ffn(xp, xs, flat(attn_p), flat(attn_s), flat(gla_p), flat(gla_s), mod,
                              wout, norm_ffn[l].reshape(1, d), wfi, wfo,
                              final_norm.reshape(1, d), tm_ffn, dec_seq // tm_ffn)
    y_prompt = y_prompt.reshape(batch, seq, d)
    y_sample = y_sample.reshape(dec_batch, dec_seq, d)

    new_kv_latent = ckv.reshape(batch, 1, seq, KV_LORA)
    new_k_rope = jnp.swapaxes(kr_t, 1, 2).reshape(batch, 1, seq, MLA_ROPE)
    new_state_fwd = sf.reshape(batch, 1, GLA_HEADS, GLA_DK, GLA_DV).astype(x_prompt.dtype)
    new_state_bwd = sb.reshape(batch, 1, GLA_HEADS, GLA_DK, GLA_DV).astype(x_prompt.dtype)
    return (y_prompt, y_sample, new_kv_latent, new_k_rope, new_state_fwd, new_state_bwd)
```

```python
import functools

import numpy as np
import jax
import jax.numpy as jnp
from jax import lax
from jax.experimental import pallas as pl
from jax.experimental.pallas import tpu as pltpu

F32 = jnp.float32
BF16 = jnp.bfloat16

GRID_W = 64
MLA_HEADS = 8
MLA_NOPE = 64
MLA_ROPE = 32
MLA_QK = MLA_NOPE + MLA_ROPE
MLA_V = 64
Q_LORA = 384
KV_LORA = 256
GLA_HEADS = 4
GLA_DK = 64
GLA_DV = 128
GATE_RANK = 16
GATE_NORM = 16.0
CHUNK = 64
D_FF = 2816
ROPE_BASE = 10000.0
EPS = 1e-6
LOG2_E = 1.4426950408889634

LANES = 128
HEAD_PAD = LANES
ROPE_LANE0 = MLA_NOPE
GQK = GLA_HEADS * GLA_DK
GV = GLA_HEADS * GLA_DV
QPAD = MLA_HEADS * HEAD_PAD
VALL = MLA_HEADS * MLA_V
ONES_ROWS = 16
KEY_BLOCK = 1024

W_KR = Q_LORA + KV_LORA
W_GQ = W_KR + MLA_ROPE
W_GF = W_GQ + 2 * GQK + GV
W_GO = W_GF + 2 * GATE_RANK
W_COLS = W_GO + GV

Z_Q = 0
Z_KV = Z_Q + Q_LORA
Z_GQ = Z_KV + KV_LORA
Z_GK = Z_GQ + GQK
Z_GV = Z_GK + GQK
Z_GO = Z_GV + GV
Z_MISC = Z_GO + GV
Z_COLS = Z_MISC + LANES

FF_CHUNK = 256
N_FF_CHUNKS = D_FF // FF_CHUNK
FFN_SUB = 256
FFN_PHASE_ENDS = (3, 7, 10)

GLA_TILE = 256
CHUNKS_PER_TILE = GLA_TILE // CHUNK
ADA_ROWS = 128
INPROJ_SUB = 512
Q_TILE = 256
ATTN_LAT_QUERIES = 512
ATTN_CTX_SEQS = 4
GLA_STATIC_TILES = 8
GLA_CTX_SEQS = 4

VMEM_LIMIT = 56 * 1024 * 1024

_NT = (((1,), (1,)), ((), ()))


def _rms(x, w):
    return x * lax.rsqrt(jnp.mean(x * x, axis=-1, keepdims=True) + EPS) * w


def _sigmoid(x):
    return 1.0 / (1.0 + jnp.exp(-x))


def _log_sigmoid(x):
    return jnp.minimum(x, 0.0) - jnp.log1p(jnp.exp(-jnp.abs(x)))


def _interleave(chains):
    pending, active = list(chains), []
    while pending or active:
        if pending:
            active.append(pending.pop(0))
        for chain in list(active):
            try:
                next(chain)
            except StopIteration:
                active.remove(chain)


def _const_spec(shape):
    nd = len(shape)
    return pl.BlockSpec(shape, lambda *_: (0,) * nd, pipeline_mode=pl.Buffered(1))


def _mod_rows(mod_ref, r):
    return [mod_ref[k, pl.ds(r, 1), :] for k in range(mod_ref.shape[0])]


def _ada_kernel(cctx_ref, c_ref, w_hbm, b_ref, o_ref, w_buf, sems):
    d = o_ref.shape[2]
    n_blocks = w_buf.shape[0] // ADA_ROWS

    def block_copy(k):
        rows = pl.ds(k * ADA_ROWS, ADA_ROWS)
        return pltpu.make_async_copy(w_hbm.at[rows, :], w_buf.at[rows, :], sems.at[k])

    for k in range(n_blocks):
        block_copy(k).start()
    row = lax.broadcasted_iota(jnp.int32, (8, d), 0)
    cond = jnp.where(row == 0, cctx_ref[...], 0.0)
    for r in range(c_ref.shape[0]):
        cond = jnp.where(row == 1 + r, c_ref[r:r + 1, :], cond)
    s = (cond * _sigmoid(cond)).astype(BF16)
    acc = jnp.broadcast_to(b_ref[...], (8, b_ref.shape[1]))
    for k in range(n_blocks):
        block_copy(k).wait()
        rows = slice(k * ADA_ROWS, (k + 1) * ADA_ROWS)
        acc = acc + jnp.dot(s[:, rows], w_buf[rows, :].astype(BF16), preferred_element_type=F32)
    for j in range(o_ref.shape[0]):
        o_ref[j] = acc[:, j * d:(j + 1) * d]


def _ada(c_ctx, c, w_ada, b_ada):
    d = w_ada.shape[0]
    n = w_ada.shape[1]
    assert 1 + c.shape[0] <= 8
    whole = lambda shape: pl.BlockSpec(shape, lambda: (0,) * len(shape))
    return pl.pallas_call(
        _ada_kernel,
        out_shape=jax.ShapeDtypeStruct((n // d, 8, d), F32),
        in_specs=[whole((1, d)), whole(c.shape), pl.BlockSpec(memory_space=pl.ANY), whole((1, n))],
        out_specs=whole((n // d, 8, d)),
        scratch_shapes=[pltpu.VMEM((d, n), F32), pltpu.SemaphoreType.DMA((d // ADA_ROWS,))],
        name="ada_mod",
        compiler_params=pltpu.CompilerParams(vmem_limit_bytes=VMEM_LIMIT),
    )(c_ctx.reshape(1, d), c, w_ada, b_ada.reshape(1, n))


def _prep_kernel(wint_ref, wuq_ref, wukv_ref, wgf_ref, wgb_ref, win_o, wuq_o, wk_o, wvt_o, wg_o):
    cols = wint_ref.shape[1]
    for dst, src, n in ((Z_Q, 0, W_KR), (Z_GQ, W_GQ, W_GF - W_GQ), (Z_GO, W_GO, GV)):
        win_o[:, dst:dst + n] = wint_ref[src:src + n, :].T.astype(BF16)
    z32 = jnp.zeros((32, cols), F32)
    misc_t = jnp.concatenate([wint_ref[W_GF:W_GO, :], z32, wint_ref[W_KR:W_GQ, :], z32], axis=0)
    win_o[:, Z_MISC:Z_COLS] = misc_t.T.astype(BF16)

    u = wuq_ref[...]
    zq = jnp.zeros((u.shape[0], HEAD_PAD - MLA_QK), F32)
    for hd in range(MLA_HEADS):
        blk = jnp.concatenate([u[:, hd * MLA_QK:(hd + 1) * MLA_QK], zq], axis=1)
        wuq_o[:, hd * HEAD_PAD:(hd + 1) * HEAD_PAD] = blk.astype(BF16)

    @pl.when(pl.program_id(0) == 0)
    def _():
        kv = wukv_ref[...]
        per = MLA_NOPE + MLA_V
        lane = lax.broadcasted_iota(jnp.int32, (kv.shape[0], per), 1)
        for hd in range(MLA_HEADS):
            blk = kv[:, hd * per:(hd + 1) * per]
            wk_o[:, hd * HEAD_PAD:(hd + 1) * HEAD_PAD] = jnp.where(lane < MLA_NOPE, blk, 0.0).astype(BF16)
        wv = jnp.concatenate([kv[:, hd * per + MLA_NOPE:(hd + 1) * per] for hd in range(MLA_HEADS)], axis=1)
        wvt_o[...] = wv.T.astype(BF16)

        wg_o[...] = jnp.zeros(wg_o.shape, BF16)
        wg_o[0:GATE_RANK, 0:GQK] = wgf_ref[...].astype(BF16)
        wg_o[GATE_RANK:2 * GATE_RANK, GQK:2 * GQK] = wgb_ref[...].astype(BF16)


def _prep_in_weights(w_in, w_uq, w_ukv, w_gate_f, w_gate_b):
    d = w_in.shape[1]
    steps = 4
    w_in_t = jnp.swapaxes(w_in, 1, 2)
    rb3 = lambda r, c: pl.BlockSpec((None, r // steps, c), lambda i: (0, i, 0))
    rb = lambda r, c: pl.BlockSpec((r // steps, c), lambda i: (i, 0))
    full3 = lambda shape: pl.BlockSpec((None,) + tuple(shape[1:]), lambda i: (0, 0, 0))
    full = lambda shape: pl.BlockSpec(shape, lambda i: (0, 0))
    return pl.pallas_call(
        _prep_kernel,
        out_shape=[jax.ShapeDtypeStruct((d, Z_COLS), BF16),
                   jax.ShapeDtypeStruct((Q_LORA, QPAD), BF16),
                   jax.ShapeDtypeStruct((KV_LORA, QPAD), BF16),
                   jax.ShapeDtypeStruct((VALL, KV_LORA), BF16),
                   jax.ShapeDtypeStruct((LANES, 2 * GQK), BF16)],
        grid=(steps,),
        in_specs=[pl.BlockSpec((None, W_COLS, d // steps), lambda i: (0, 0, i)),
                  rb3(Q_LORA, MLA_HEADS * MLA_QK), full3(w_ukv.shape),
                  full3(w_gate_f.shape), full3(w_gate_b.shape)],
        out_specs=[rb(d, Z_COLS), rb(Q_LORA, QPAD), full((KV_LORA, QPAD)), full((VALL, KV_LORA)),
                   full((LANES, 2 * GQK))],
        name="weight_prep",
        compiler_params=pltpu.CompilerParams(dimension_semantics=("arbitrary",)),
    )(w_in_t, w_uq, w_ukv, w_gate_f, w_gate_b)


def _inproj_kernel(*refs, latent, mod_row):
    (x_ref, mod_ref, nw_ref, win_ref, qn_ref, wuq_ref, kvn_ref, wk_ref, wvt_ref, wg_ref, bgf_ref, bgb_ref) = refs[:12]
    if latent:
        cos_ref, sa_ref, sb_ref = refs[12:15]
        outs = refs[15:]
    else:
        outs = refs[12:]
    q_ref, k_ref, vt_ref, gq_ref, gk_ref, gv_ref, gf_ref, gb_ref, go_ref = outs[:9]

    sh1, sc1 = _mod_rows(mod_ref, mod_row(pl.program_id(0)))[:2]
    scale = MLA_QK ** -0.5 * LOG2_E
    lane = lax.broadcasted_iota(jnp.int32, (INPROJ_SUB, LANES), 1)
    in_rope = (lane >= ROPE_LANE0) & (lane < ROPE_LANE0 + MLA_ROPE)

    def sub_tile(r0):
        rows = slice(r0, r0 + INPROJ_SUB)
        h = (_rms(x_ref[rows, :], nw_ref[...]) * (1.0 + sc1) + sh1).astype(BF16)
        yield
        z_all = jnp.dot(h, win_ref[...], preferred_element_type=F32)
        z = lambda lo, n: z_all[:, lo:lo + n]
        yield
        qn = _rms(z(Z_Q, Q_LORA), qn_ref[...]).astype(BF16)
        ckv = _rms(z(Z_KV, KV_LORA), kvn_ref[...])
        ckv_b = ckv.astype(BF16)
        misc = z(Z_MISC, LANES)
        yield
        q = jnp.dot(qn, wuq_ref[...], preferred_element_type=F32)
        kn = jnp.dot(ckv_b, wk_ref[...], preferred_element_type=F32)
        vt_ref[:, rows] = lax.dot_general(wvt_ref[...], ckv_b, _NT,
                                          preferred_element_type=F32).astype(BF16)
        gpre = jnp.dot(misc.astype(BF16), wg_ref[...], preferred_element_type=F32)
        yield
        if latent:
            cos, sa, sb = cos_ref[rows, :], sa_ref[rows, :], sb_ref[rows, :]

            def rope(t):
                return t * cos + pltpu.roll(t, LANES - 8, 1) * sa + pltpu.roll(t, 8, 1) * sb
        else:
            def rope(t):
                return t

        krope = rope(misc)
        for hd in range(MLA_HEADS):
            sl = slice(hd * HEAD_PAD, (hd + 1) * HEAD_PAD)
            q_ref[rows, sl] = (rope(q[:, sl]) * scale).astype(BF16)
            k_ref[rows, sl] = jnp.where(in_rope, krope, kn[:, sl]).astype(BF16)
        gq_ref[rows, :] = z(Z_GQ, GQK)
        gk_ref[rows, :] = z(Z_GK, GQK)
        gv_ref[rows, :] = z(Z_GV, GV).astype(BF16)
        go_ref[rows, :] = z(Z_GO, GV)
        gf_ref[rows, :] = _log_sigmoid(gpre[:, :GQK] + bgf_ref[...]) * (1.0 / GATE_NORM)
        gb_ref[rows, :] = _log_sigmoid(gpre[:, GQK:] + bgb_ref[...]) * (1.0 / GATE_NORM)
        if not latent:
            ckv_ref, krt_ref = outs[9:]
            ckv_ref[rows, :] = ckv
            misc_t = misc.T
            n = krt_ref.shape[2]
            for b in range(INPROJ_SUB // n):
                krt_ref[r0 // n + b] = misc_t[ROPE_LANE0:ROPE_LANE0 + MLA_ROPE, b * n:(b + 1) * n]

    _interleave([sub_tile(r0) for r0 in range(0, x_ref.shape[0], INPROJ_SUB)])


def _inproj(x2d, mod, mod_row, weights, rope_tabs, tm, seq_len):
    n_tok, d = x2d.shape
    latent = rope_tabs is not None
    tiles_per_seq = max(seq_len // tm, 1)
    nw, win, qn, wuq, kvn, wk, wvt, wg, bgf, bgb = weights
    row = lambda i: (i, 0)
    in_specs = [pl.BlockSpec((tm, d), row), _const_spec(mod.shape),
                _const_spec(nw.shape), _const_spec(win.shape), _const_spec(qn.shape),
                _const_spec(wuq.shape), _const_spec(kvn.shape), _const_spec(wk.shape),
                _const_spec(wvt.shape), _const_spec(wg.shape), _const_spec(bgf.shape),
                _const_spec(bgb.shape)]
    args = [x2d, mod, nw, win, qn, wuq, kvn, wk, wvt, wg, bgf, bgb]
    if latent:
        tab = pl.BlockSpec((tm, LANES), lambda i: (i % tiles_per_seq, 0))
        in_specs += [tab, tab, tab]
        args += list(rope_tabs)
    out_cols = [(QPAD, BF16), (QPAD, BF16), None, (GQK, F32), (GQK, F32), (GV, BF16),
                (GQK, F32), (GQK, F32), (GV, F32)]
    if not latent:
        out_cols += [(KV_LORA, F32)]
    out_shape = [jax.ShapeDtypeStruct((n_tok, oc[0]), oc[1]) if oc else
                 jax.ShapeDtypeStruct((VALL, n_tok), BF16) for oc in out_cols]
    out_specs = [pl.BlockSpec((tm, oc[0]), row) if oc else
                 pl.BlockSpec((VALL, tm), lambda i: (0, i)) for oc in out_cols]
    if not latent:
        seq = seq_len
        out_shape.append(jax.ShapeDtypeStruct((n_tok // seq, MLA_ROPE, seq), F32))
        out_specs.append(pl.BlockSpec((tm // seq, MLA_ROPE, seq), lambda i: (i, 0, 0)))
    return pl.pallas_call(
        functools.partial(_inproj_kernel, latent=latent, mod_row=mod_row),
        out_shape=out_shape,
        grid=(n_tok // tm,),
        in_specs=in_specs,
        out_specs=out_specs,
        name="inproj_lat" if latent else "inproj_ctx",
        compiler_params=pltpu.CompilerParams(dimension_semantics=("arbitrary",),
                                             vmem_limit_bytes=VMEM_LIMIT),
    )(*args)


def _decomp_kernel(ckv_ref, krt_ref, wk_ref, wvt_ref, k_ref, vt_ref):
    ckv_b = ckv_ref[...].astype(BF16)
    kn = jnp.dot(ckv_b, wk_ref[...], preferred_element_type=F32)
    n_keys = krt_ref.shape[1]
    kr = jnp.concatenate([jnp.zeros((ROPE_LANE0, n_keys), F32), krt_ref[...],
                          jnp.zeros((LANES - ROPE_LANE0 - MLA_ROPE, n_keys), F32)], axis=0).T
    lane = lax.broadcasted_iota(jnp.int32, kr.shape, 1)
    in_rope = (lane >= ROPE_LANE0) & (lane < ROPE_LANE0 + MLA_ROPE)
    for hd in range(MLA_HEADS):
        sl = slice(hd * HEAD_PAD, (hd + 1) * HEAD_PAD)
        k_ref[:, sl] = jnp.where(in_rope, kr, kn[:, sl]).astype(BF16)
    vt_ref[...] = lax.dot_general(wvt_ref[...], ckv_b, _NT, preferred_element_type=F32).astype(BF16)


def _decomp(ckv, kr_t, wk, wvt):
    b, s, _ = ckv.shape
    return pl.pallas_call(
        _decomp_kernel,
        out_shape=[jax.ShapeDtypeStruct((b, s, QPAD), BF16), jax.ShapeDtypeStruct((VALL, b * s), BF16)],
        grid=(b,),
        in_specs=[pl.BlockSpec((None, s, KV_LORA), lambda i: (i, 0, 0)),
                  pl.BlockSpec((None, MLA_ROPE, s), lambda i: (i, 0, 0)),
                  _const_spec(wk.shape), _const_spec(wvt.shape)],
        out_specs=[pl.BlockSpec((None, s, QPAD), lambda i: (i, 0, 0)),
                   pl.BlockSpec((VALL, s), lambda i: (0, i))],
        name="ctx_decompress",
        compiler_params=pltpu.CompilerParams(dimension_semantics=("arbitrary",)),
    )(ckv, kr_t, wk, wvt)


def _attn_kernel(*refs, has_ctx, n_seqs, n_side):
    n_in = 5 if has_ctx else 3
    side_in, refs = refs[n_in:n_in + n_side], refs[:n_in] + refs[n_in + n_side:]
    side_out, refs = refs[n_in + 1:n_in + 1 + n_side], refs[:n_in + 1] + refs[n_in + 1 + n_side:]
    if has_ctx:
        q_ref, kc_ref, vct_ref, k_ref, vt_ref, o_ref, st_ref, p_ref = refs
    else:
        q_ref, k_ref, vt_ref, o_ref, st_ref, p_ref = refs
    tq = Q_TILE

    for src, dst in zip(side_in, side_out):
        dst[...] = src[...].astype(BF16)

    def key_blocks(bi):
        srcs = [(kc_ref, vct_ref)] if has_ctx else []
        blocks, row0 = [], 0
        for kr, vr in srcs + [(k_ref, vt_ref)]:
            n_keys = kr.shape[1]
            size = min(KEY_BLOCK, n_keys)
            for r in range(0, n_keys, size):
                blocks.append((kr, vr, r, bi * n_keys + r, size, row0))
                row0 += size
        return blocks

    units = [(bi, slice(q0, q0 + tq), hd) for bi in range(n_seqs)
             for q0 in range(0, q_ref.shape[1], tq) for hd in range(MLA_HEADS)]
    col_max = [None] * len(units)
    pair = []
    for stage in range(len(units) + 2):
        ua, ub, uc = stage, stage - 1, stage - 2
        run_max = None
        acc = jnp.zeros((MLA_V + ONES_ROWS, tq), F32)
        for j in range(len(key_blocks(0))):
            if ua < len(units):
                bi, qrows, hd = units[ua]
                kr, _, r0, _, size, srow = key_blocks(bi)[j]
                sl = slice(hd * HEAD_PAD, (hd + 1) * HEAD_PAD)
                st = lax.dot_general(kr[bi, r0:r0 + size, sl], q_ref[bi, qrows, sl], _NT,
                                     preferred_element_type=F32)
                st_ref[ua % 2, srow:srow + size, :] = st
                blk_max = jnp.max(st.reshape(size // 8, 8, tq), axis=0)
                run_max = blk_max if run_max is None else jnp.maximum(run_max, blk_max)
            if 0 <= ub < len(units):
                _, _, _, _, size, srow = key_blocks(0)[j]
                p_ref[ub % 2, srow:srow + size, :] = jnp.exp2(
                    st_ref[ub % 2, srow:srow + size, :] - col_max[ub]).astype(BF16)
            if uc >= 0:
                bi, _, hd = units[uc]
                _, vr, _, c0, size, srow = key_blocks(bi)[j]
                v_aug = jnp.concatenate([vr[hd * MLA_V:(hd + 1) * MLA_V, c0:c0 + size],
                                         jnp.ones((ONES_ROWS, size), BF16)], axis=0)
                acc = acc + jnp.dot(v_aug, p_ref[uc % 2, srow:srow + size, :],
                                    preferred_element_type=F32)
        if ua < len(units):
            col_max[ua] = jnp.max(run_max, axis=0, keepdims=True)
        if uc >= 0:
            bi, qrows, hd = units[uc]
            pair.append(acc[:MLA_V, :] / acc[MLA_V:MLA_V + 1, :])
            if len(pair) == 2:
                o_ref[bi, qrows, (hd - 1) * MLA_V:(hd + 1) * MLA_V] = (
                    jnp.concatenate(pair, axis=0).T.astype(BF16))
                pair = []


def _attention(q, k, vt, ctx_kv, tq, n_seqs, side_weights=()):
    b, t, _ = q.shape
    steps = (b // n_seqs) * (t // tq)
    assert (n_seqs == 1 or tq == t) and tq % Q_TILE == 0
    has_ctx = ctx_kv is not None
    in_specs = [pl.BlockSpec((n_seqs, tq, QPAD), lambda i, j: (i, j, 0))]
    args = [q]
    if has_ctx:
        kc, vct = ctx_kv
        s = kc.shape[1]
        in_specs += [pl.BlockSpec((n_seqs, s, QPAD), lambda i, j: (i, 0, 0)),
                     pl.BlockSpec((VALL, n_seqs * s), lambda i, j: (0, i))]
        args += [kc, vct]
    in_specs += [pl.BlockSpec((n_seqs, t, QPAD), lambda i, j: (i, 0, 0)),
                 pl.BlockSpec((VALL, n_seqs * t), lambda i, j: (0, i))]
    args += [k, vt]
    out_shape = [jax.ShapeDtypeStruct((b, t, VALL), BF16)]
    out_specs = [pl.BlockSpec((n_seqs, tq, VALL), lambda i, j: (i, j, 0))]
    nj = t // tq
    for w in side_weights:
        _, rows, cols = w.shape
        assert rows % (16 * steps) == 0
        in_specs.append(pl.BlockSpec((None, rows // steps, cols), lambda i, j: (0, i * nj + j, 0)))
        args.append(w)
        out_shape.append(jax.ShapeDtypeStruct((rows, cols), BF16))
        out_specs.append(pl.BlockSpec((rows // steps, cols), lambda i, j: (i * nj + j, 0)))
    return pl.pallas_call(
        functools.partial(_attn_kernel, has_ctx=has_ctx, n_seqs=n_seqs, n_side=len(side_weights)),
        out_shape=out_shape,
        grid=(b // n_seqs, t // tq),
        in_specs=in_specs,
        out_specs=out_specs,
        scratch_shapes=[pltpu.VMEM((2, t + (s if has_ctx else 0), Q_TILE), F32),
                        pltpu.VMEM((2, t + (s if has_ctx else 0), Q_TILE), BF16)],
        name="mla_attn_lat" if has_ctx else "mla_attn_ctx",
        compiler_params=pltpu.CompilerParams(dimension_semantics=("arbitrary", "arbitrary"),
                                             vmem_limit_bytes=VMEM_LIMIT),
    )(*args)


def _gla_kernel(*refs, n_tiles, n_seqs, zero_init):
    gq_ref, gk_ref, gv_ref, gf_ref, gb_ref, go_ref = refs[:6]
    if zero_init:
        gn_ref, o_ref, sf_ref, sb_ref, oacc_ref, bdqk_ref, tri_ref, hm_ref = refs[6:]
    else:
        (sf0_ref, sb0_ref, gn_ref, o_ref, sf_ref, sb_ref, oacc_ref,
         bdqk_ref, tri_ref, hm_ref) = refs[6:]
    g_refs = (gf_ref, gb_ref)
    state_refs = (sf_ref, sb_ref)

    @pl.when(pl.program_id(0) == 0)
    def _():
        ri = lax.broadcasted_iota(jnp.int32, (GLA_TILE, GLA_TILE), 0)
        ci = lax.broadcasted_iota(jnp.int32, (GLA_TILE, GLA_TILE), 1)
        same_chunk = (ri // CHUNK) == (ci // CHUNK)
        bdqk_ref[...] = jnp.where(same_chunk, 1.0, 0.0).astype(BF16)
        tri_ref[0] = jnp.where(same_chunk & (ri >= ci), 1.0, 0.0)
        tri_ref[1] = jnp.where(same_chunk & (ci >= ri), 1.0, 0.0)
        hm_ref[...] = jnp.where(
            lax.broadcasted_iota(jnp.int32, (GLA_HEADS * GLA_TILE, GQK), 0) // GLA_TILE
            == lax.broadcasted_iota(jnp.int32, (GLA_HEADS * GLA_TILE, GQK), 1) // GLA_DK,
            1.0, 0.0).astype(BF16)

    row8 = lax.broadcasted_iota(jnp.int32, (8, GQK), 0)

    def row_slice(start, size):
        return pl.ds(start if isinstance(start, int) else pl.multiple_of(start, size), size)

    def tile_rows(t):
        return row_slice(t * GLA_TILE, GLA_TILE)

    finished = set()

    def total_row(c, d):
        return c * CHUNK + (CHUNK - 1 if d == 0 else 0)

    def tile_dir(b, t, d):
        rows = tile_rows(t)
        g = g_refs[d][b, rows, :]
        g_hi = g.astype(BF16)
        g_lo = (g - g_hi.astype(F32)).astype(BF16)
        tri_b = tri_ref[d].astype(BF16)
        cum = (jnp.dot(tri_b, g_hi, preferred_element_type=F32)
               + jnp.dot(tri_b, g_lo, preferred_element_type=F32))
        yield
        totals = [cum[total_row(c, d):total_row(c, d) + 1, :] for c in range(CHUNKS_PER_TILE)]
        tot8 = jnp.zeros((8, GQK), F32)
        for c in range(CHUNKS_PER_TILE):
            tot8 = jnp.where(row8 == c, totals[c], tot8)
        dec_t = jnp.concatenate([jnp.exp(tot8), jnp.zeros((LANES - 8, GQK), F32)], axis=0).T
        q = gq_ref[b, rows, :] * (GLA_DK ** -0.5)
        k = gk_ref[b, rows, :]
        v = gv_ref[b, rows, :]
        tot = jnp.concatenate([jnp.broadcast_to(tc, (CHUNK, GQK)) for tc in totals], axis=0)
        qe = (q * jnp.exp(cum)).astype(BF16)
        ke = (k * jnp.exp(-cum)).astype(BF16)
        kd_t = (k * jnp.exp(tot - cum)).T.astype(BF16)
        bd_qk = bdqk_ref[...] > 0
        tri = tri_ref[d] > 0

        qm = jnp.where(hm_ref[...] > 0, jnp.tile(qe, (GLA_HEADS, 1)), 0.0)
        yield
        att = lax.dot_general(qm, ke, _NT, preferred_element_type=F32)

        yield
        intra, upd = [], []
        for hd in range(GLA_HEADS):
            vh = v[:, hd * GLA_DV:(hd + 1) * GLA_DV]
            a_h = jnp.where(tri, att[hd * GLA_TILE:(hd + 1) * GLA_TILE, :], 0.0).astype(BF16)
            intra.append(jnp.dot(a_h, vh, preferred_element_type=F32))
            kd_h = jnp.tile(kd_t[hd * GLA_DK:(hd + 1) * GLA_DK, :], (CHUNKS_PER_TILE, 1))
            upd.append(jnp.dot(jnp.where(bd_qk, kd_h, 0.0), vh, preferred_element_type=F32))

        yield
        state = [state_refs[d][b, hd] for hd in range(GLA_HEADS)]
        order = range(CHUNKS_PER_TILE) if d == 0 else range(CHUNKS_PER_TILE - 1, -1, -1)
        seen = {}
        for c in order:
            seen[c] = jnp.concatenate(state, axis=0).astype(BF16)
            decay = jnp.broadcast_to(dec_t[:, c:c + 1], (GQK, GLA_DV))
            for hd in range(GLA_HEADS):
                ks = slice(hd * GLA_DK, (hd + 1) * GLA_DK)
                state[hd] = decay[ks, :] * state[hd] + upd[hd][c * CHUNK:(c + 1) * CHUNK, :]
        for hd in range(GLA_HEADS):
            state_refs[d][b, hd] = state[hd]

        yield
        for c in range(CHUNKS_PER_TILE):
            cr = slice(c * CHUNK, (c + 1) * CHUNK)
            q_c = jnp.concatenate([qm[hd * GLA_TILE + c * CHUNK:hd * GLA_TILE + (c + 1) * CHUNK, :]
                                   for hd in range(GLA_HEADS)], axis=0)
            inter = jnp.dot(q_c, seen[c], preferred_element_type=F32)
            o = jnp.concatenate([intra[hd][cr, :] + inter[hd * CHUNK:(hd + 1) * CHUNK, :]
                                 for hd in range(GLA_HEADS)], axis=1)
            oacc_ref[d, b, row_slice(t * GLA_TILE + c * CHUNK, CHUNK), :] = o
        if isinstance(t, int):
            finished.add((b, t, d))

    if zero_init:
        sf_ref[...] = jnp.zeros(sf_ref.shape, F32)
        sb_ref[...] = jnp.zeros(sb_ref.shape, F32)
    else:
        sf_ref[...] = sf0_ref[...]
        sb_ref[...] = sb0_ref[...]

    gn = gn_ref[...]

    def epilogue(t):
        while isinstance(t, int) and not all((b, t, d) in finished for b in range(n_seqs) for d in range(2)):
            yield
        rows = tile_rows(t)
        for b in range(n_seqs):
            for hd in range(GLA_HEADS):
                vs = slice(hd * GLA_DV, (hd + 1) * GLA_DV)
                o = _rms(oacc_ref[0, b, rows, vs] + oacc_ref[1, b, rows, vs], gn)
                go = go_ref[b, rows, vs]
                o_ref[b, rows, vs] = (o * (go * _sigmoid(go))).astype(BF16)
        return
        yield

    def main_chains(first_tile, n):
        chains = []
        for t in [first_tile + u for u in range(n)]:
            for b in range(n_seqs):
                chains += [tile_dir(b, t, 0), tile_dir(b, n_tiles - 1 - t, 1)]
        return chains

    if n_tiles <= GLA_STATIC_TILES:
        _interleave(main_chains(0, n_tiles) + [epilogue(t) for t in range(n_tiles)])
    else:
        per_step = next(c for c in (4, 2, 1) if n_tiles % c == 0)

        def main_body(i, carry):
            _interleave(main_chains(i * per_step, per_step))
            return carry

        lax.fori_loop(0, n_tiles // per_step, main_body, 0)

        def epilogue_body(t, carry):
            _interleave([epilogue(t)])
            return carry

        lax.fori_loop(0, n_tiles, epilogue_body, 0)


def _gla(gq, gk, gv, gf, gb, go, init_states, gn, n_seqs):
    b, t, _ = gq.shape
    n_tiles = t // GLA_TILE
    zero_init = init_states is None
    seq = lambda c: pl.BlockSpec((n_seqs, t, c), lambda i: (i, 0, 0))
    st = pl.BlockSpec((n_seqs, GLA_HEADS, GLA_DK, GLA_DV), lambda i: (i, 0, 0, 0))
    in_specs = [seq(GQK), seq(GQK), seq(GV), seq(GQK), seq(GQK), seq(GV)]
    args = [gq, gk, gv, gf, gb, go]
    if not zero_init:
        in_specs += [st, st]
        args += list(init_states)
    in_specs.append(_const_spec(gn.shape))
    args.append(gn)
    return pl.pallas_call(
        functools.partial(_gla_kernel, n_tiles=n_tiles, n_seqs=n_seqs, zero_init=zero_init),
        out_shape=[jax.ShapeDtypeStruct((b, t, GV), BF16),
                   jax.ShapeDtypeStruct((b, GLA_HEADS, GLA_DK, GLA_DV), F32),
                   jax.ShapeDtypeStruct((b, GLA_HEADS, GLA_DK, GLA_DV), F32)],
        grid=(b // n_seqs,),
        in_specs=in_specs,
        out_specs=[seq(GV), st, st],
        scratch_shapes=[pltpu.VMEM((2, n_seqs, t, GV), F32),
                        pltpu.VMEM((GQK, GLA_TILE), BF16),
                        pltpu.VMEM((2, GLA_TILE, GLA_TILE), F32),
                        pltpu.VMEM((GLA_HEADS * GLA_TILE, GQK), BF16)],
        name="gla_%d" % t,
        compiler_params=pltpu.CompilerParams(dimension_semantics=("arbitrary",),
                                             vmem_limit_bytes=VMEM_LIMIT),
    )(*args)


def _ffn_kernel(xp_ref, xs_ref, atp_ref, ats_ref, glp_ref, gls_ref, mod_ref, wout_ref, nf_ref,
                wfi_ref, wfo_ref, fn_ref, yp_ref, ys_ref, act_ref, *, ctx_tiles, tiles_per_seq):
    def sub_tile(x_ref, at_ref, gl_ref, y_ref, mod_row, r0):
        rows = slice(r0, r0 + FFN_SUB)
        gt1, sh2, sc2, gt2 = _mod_rows(mod_ref, mod_row)[2:]
        mix = (jnp.dot(at_ref[rows, :], wout_ref[0:VALL, :], preferred_element_type=F32)
               + jnp.dot(gl_ref[rows, :], wout_ref[VALL:, :], preferred_element_type=F32))
        yield
        x1 = x_ref[rows, :] + gt1 * mix
        h2 = (_rms(x1, nf_ref[...]) * (1.0 + sc2) + sh2).astype(BF16)
        yield
        for j in range(N_FF_CHUNKS):
            cs = slice(j * FF_CHUNK, (j + 1) * FF_CHUNK)
            a = jnp.dot(h2, wfi_ref[:, cs], preferred_element_type=F32)
            g = jnp.dot(h2, wfi_ref[:, D_FF + j * FF_CHUNK:D_FF + (j + 1) * FF_CHUNK],
                        preferred_element_type=F32)
            act_ref[rows, cs] = (a * _sigmoid(a) * g).astype(BF16)
            if j in FFN_PHASE_ENDS:
                yield
        ff = jnp.dot(act_ref[rows, :], wfo_ref[...], preferred_element_type=F32)
        yield
        x2 = x1 + gt2 * ff
        y_ref[rows, :] = _rms(x2, fn_ref[...])

    def tile(x_ref, at_ref, gl_ref, y_ref, mod_row):
        _interleave([sub_tile(x_ref, at_ref, gl_ref, y_ref, mod_row, r0)
                     for r0 in range(0, x_ref.shape[0], FFN_SUB)])

    t = pl.program_id(0)

    @pl.when(t < ctx_tiles)
    def _():
        tile(xp_ref, atp_ref, glp_ref, yp_ref, 0)

    @pl.when(t >= ctx_tiles)
    def _():
        tile(xs_ref, ats_ref, gls_ref, ys_ref, 1 + (t - ctx_tiles) // tiles_per_seq)


def _ffn(xp, xs, attn_p, attn_s, gla_p, gla_s, mod, wout, nf, wfi, wfo, fn, tm, tiles_per_seq):
    d = xp.shape[1]
    ctx_tiles = xp.shape[0] // tm
    lat_tiles = xs.shape[0] // tm
    ctx_map = lambda s: (jnp.minimum(s, ctx_tiles - 1), 0)
    lat_map = lambda s: (jnp.maximum(s - ctx_tiles, 0), 0)
    tile = lambda c, m: pl.BlockSpec((tm, c), m)
    return pl.pallas_call(
        functools.partial(_ffn_kernel, ctx_tiles=ctx_tiles, tiles_per_seq=tiles_per_seq),
        out_shape=[jax.ShapeDtypeStruct(xp.shape, F32), jax.ShapeDtypeStruct(xs.shape, F32)],
        grid=(ctx_tiles + lat_tiles,),
        in_specs=[tile(d, ctx_map), tile(d, lat_map), tile(VALL, ctx_map), tile(VALL, lat_map),
                  tile(GV, ctx_map), tile(GV, lat_map), _const_spec(mod.shape),
                  _const_spec(wout.shape), _const_spec(nf.shape), _const_spec(wfi.shape),
                  _const_spec(wfo.shape), _const_spec(fn.shape)],
        out_specs=[tile(d, ctx_map), tile(d, lat_map)],
        scratch_shapes=[pltpu.VMEM((tm, D_FF), BF16)],
        name="out_ffn",
        compiler_params=pltpu.CompilerParams(dimension_semantics=("arbitrary",),
                                             vmem_limit_bytes=VMEM_LIMIT),
    )(xp, xs, attn_p, attn_s, gla_p, gla_s, mod, wout, nf, wfi, wfo, fn)


def _rope_tables(n_tokens):
    t = np.arange(n_tokens)
    row = (t // GRID_W).astype(np.float32)
    col = (t % GRID_W).astype(np.float32)
    half = MLA_ROPE // 2
    inv = (np.float32(ROPE_BASE) ** (-np.arange(0, half, 2, dtype=np.float32) / np.float32(half))).astype(np.float32)
    ang_r = row[:, None] * inv
    ang_c = col[:, None] * inv
    ang = np.concatenate([ang_r, ang_r, ang_c, ang_c], axis=-1).astype(np.float32)
    cos, sin = np.cos(ang), np.sin(ang)
    first = (np.arange(MLA_ROPE) % half) < (half // 2)
    cos_t = np.ones((n_tokens, LANES), np.float32)
    sa_t = np.zeros((n_tokens, LANES), np.float32)
    sb_t = np.zeros((n_tokens, LANES), np.float32)
    cos_t[:, ROPE_LANE0:ROPE_LANE0 + MLA_ROPE] = cos
    sa_t[:, ROPE_LANE0:ROPE_LANE0 + MLA_ROPE] = np.where(first, -sin, 0.0)
    sb_t[:, ROPE_LANE0:ROPE_LANE0 + MLA_ROPE] = np.where(first, 0.0, sin)
    return jnp.asarray(cos_t), jnp.asarray(sa_t), jnp.asarray(sb_t)


def kernel(x_prompt, x_sample, cache_kv_latent, cache_k_rope, state_gla_fwd, state_gla_bwd, c, c_ctx, w_ada, b_ada, norm_attn, w_in, mla_q_norm, w_uq, mla_kv_norm, w_ukv, w_gate_f, b_gate_f, w_gate_b, b_gate_b, gla_norm, w_out, norm_ffn, w_ffn_in, w_ffn_out, final_norm):
    batch, seq, d = x_prompt.shape
    dec_batch, dec_seq, _ = x_sample.shape
    assert w_ada.shape[0] == 1 and w_in.shape[-1] == W_COLS and w_ffn_in.shape[-1] == 2 * D_FF
    l = 0

    mod = _ada(c_ctx, c, w_ada[l], b_ada[l])

    win, wuq, wk, wvt, wg = _prep_in_weights(w_in, w_uq, w_ukv, w_gate_f, w_gate_b)
    in_w = (norm_attn[l].reshape(1, d), win, mla_q_norm[l].reshape(1, Q_LORA), wuq,
            mla_kv_norm[l].reshape(1, KV_LORA), wk, wvt, wg, b_gate_f, b_gate_b)
    gn = gla_norm[l].reshape(1, GLA_DV)
    tm, tm_ffn = 512, 512
    r3 = lambda a, b_, t: a.reshape(b_, t, a.shape[-1])

    xp = x_prompt.reshape(batch * seq, d)
    (q, k, vt, gq, gk, gv, gf, gb, go, ckv, kr_t) = _inproj(xp, mod, lambda i: 0, in_w, None, tm, seq)
    (attn_p,) = _attention(r3(q, batch, seq), r3(k, batch, seq), vt, None, seq, ATTN_CTX_SEQS)
    gla_p, sf, sb = _gla(r3(gq, batch, seq), r3(gk, batch, seq), r3(gv, batch, seq), r3(gf, batch, seq),
                         r3(gb, batch, seq), r3(go, batch, seq), None, gn, GLA_CTX_SEQS)

    xs = x_sample.reshape(dec_batch * dec_seq, d)
    tiles = dec_seq // tm
    (q, k, vt, gq, gk, gv, gf, gb, go) = _inproj(xs, mod, lambda i: 1 + i // tiles, in_w,
                                                  _rope_tables(dec_seq), tm, dec_seq)
    kc, vct = _decomp(cache_kv_latent[:, l], jnp.swapaxes(cache_k_rope[:, l], 1, 2), wk, wvt)
    attn_s, wout, wfi, wfo = _attention(r3(q, dec_batch, dec_seq), r3(k, dec_batch, dec_seq), vt, (kc, vct),
                                        ATTN_LAT_QUERIES, 1, (w_out, w_ffn_in, w_ffn_out))
    gla_s, _, _ = _gla(r3(gq, dec_batch, dec_seq), r3(gk, dec_batch, dec_seq), r3(gv, dec_batch, dec_seq),
                       r3(gf, dec_batch, dec_seq), r3(gb, dec_batch, dec_seq), r3(go, dec_batch, dec_seq),
                       (state_gla_fwd[:, l].astype(F32), state_gla_bwd[:, l].astype(F32)), gn, 1)

    flat = lambda a: a.reshape(-1, a.shape[-1])
    y_prompt, y_sample = _ffn(xp, xs, flat(attn_p), flat(attn_s), flat(gla_p), flat(gla_s), mod,
                              wout, norm_ffn[l].reshape(1, d), wfi, wfo,
                              final_norm.reshape(1, d), tm_ffn, dec_seq // tm_ffn)
    y_prompt = y_prompt.reshape(batch, seq, d)
    y_sample = y_sample.reshape(dec_batch, dec_seq, d)

    new_kv_latent = ckv.reshape(batch, 1, seq, KV_LORA)
    new_k_rope = jnp.swapaxes(kr_t, 1, 2).reshape(batch, 1, seq, MLA_ROPE)
    new_state_fwd = sf.reshape(batch, 1, GLA_HEADS, GLA_DK, GLA_DV).astype(x_prompt.dtype)
    new_state_bwd = sb.reshape(batch, 1, GLA_HEADS, GLA_DK, GLA_DV).astype(x_prompt.dtype)
    return (y_prompt, y_sample, new_kv_latent, new_k_rope, new_state_fwd, new_state_bwd)
```

```python
import functools

import numpy as np
import jax
import jax.numpy as jnp
from jax import lax
from jax.experimental import pallas as pl
from jax.experimental.pallas import tpu as pltpu

F32 = jnp.float32
BF16 = jnp.bfloat16

GRID_W = 64
MLA_HEADS = 8
MLA_NOPE = 64
MLA_ROPE = 32
MLA_QK = MLA_NOPE + MLA_ROPE
MLA_V = 64
Q_LORA = 384
KV_LORA = 256
GLA_HEADS = 4
GLA_DK = 64
GLA_DV = 128
GATE_RANK = 16
GATE_NORM = 16.0
CHUNK = 64
D_FF = 2816
ROPE_BASE = 10000.0
EPS = 1e-6
LOG2_E = 1.4426950408889634

LANES = 128
HEAD_PAD = LANES
ROPE_LANE0 = MLA_NOPE
GQK = GLA_HEADS * GLA_DK
GV = GLA_HEADS * GLA_DV
QPAD = MLA_HEADS * HEAD_PAD
VALL = MLA_HEADS * MLA_V
ONES_ROWS = 16
KEY_BLOCK = 1024

W_KR = Q_LORA + KV_LORA
W_GQ = W_KR + MLA_ROPE
W_GF = W_GQ + 2 * GQK + GV
W_GO = W_GF + 2 * GATE_RANK
W_COLS = W_GO + GV

Z_Q = 0
Z_KV = Z_Q + Q_LORA
Z_GQ = Z_KV + KV_LORA
Z_GK = Z_GQ + GQK
Z_GV = Z_GK + GQK
Z_GO = Z_GV + GV
Z_MISC = Z_GO + GV
Z_COLS = Z_MISC + LANES

FF_CHUNK = 256
N_FF_CHUNKS = D_FF // FF_CHUNK
FFN_SUB = 256
FFN_PHASE_ENDS = (3, 7, 10)

GLA_TILE = 256
CHUNKS_PER_TILE = GLA_TILE // CHUNK
ADA_ROWS = 128
INPROJ_SUB = 512
Q_TILE = 256
ATTN_LAT_QUERIES = 512
GLA_STATIC_TILES = 8
GLA_CTX_SEQS = 4

VMEM_LIMIT = 56 * 1024 * 1024

_NT = (((1,), (1,)), ((), ()))


def _rms(x, w):
    return x * lax.rsqrt(jnp.mean(x * x, axis=-1, keepdims=True) + EPS) * w


def _sigmoid(x):
    return 1.0 / (1.0 + jnp.exp(-x))


def _log_sigmoid(x):
    return jnp.minimum(x, 0.0) - jnp.log1p(jnp.exp(-jnp.abs(x)))


def _interleave(chains):
    pending, active = list(chains), []
    while pending or active:
        if pending:
            active.append(pending.pop(0))
        for chain in list(active):
            try:
                next(chain)
            except StopIteration:
                active.remove(chain)


def _const_spec(shape):
    nd = len(shape)
    return pl.BlockSpec(shape, lambda *_: (0,) * nd, pipeline_mode=pl.Buffered(1))


def _mod_rows(mod_ref, r):
    return [mod_ref[k, pl.ds(r, 1), :] for k in range(mod_ref.shape[0])]


def _ada_kernel(cctx_ref, c_ref, w_ref, b_ref, o_ref):
    k = pl.program_id(0)
    d = o_ref.shape[2]
    row = lax.broadcasted_iota(jnp.int32, (8, cctx_ref.shape[1]), 0)
    cond = jnp.where(row == 0, cctx_ref[...], 0.0)
    for r in range(c_ref.shape[0]):
        cond = jnp.where(row == 1 + r, c_ref[r:r + 1, :], cond)
    s = (cond * _sigmoid(cond)).astype(BF16)
    part = jnp.dot(s, w_ref[...].astype(BF16), preferred_element_type=F32)
    for j in range(o_ref.shape[0]):
        sl = slice(j * d, (j + 1) * d)

        @pl.when(k == 0)
        def _():
            o_ref[j] = part[:, sl] + b_ref[:, sl]

        @pl.when(k > 0)
        def _():
            o_ref[j] += part[:, sl]


def _ada(c_ctx, c, w_ada, b_ada):
    d = w_ada.shape[0]
    n = w_ada.shape[1]
    assert 1 + c.shape[0] <= 8
    return pl.pallas_call(
        _ada_kernel,
        out_shape=jax.ShapeDtypeStruct((n // d, 8, d), F32),
        grid=(d // ADA_ROWS,),
        in_specs=[pl.BlockSpec((1, ADA_ROWS), lambda k: (0, k)),
                  pl.BlockSpec((c.shape[0], ADA_ROWS), lambda k: (0, k)),
                  pl.BlockSpec((ADA_ROWS, n), lambda k: (k, 0)),
                  pl.BlockSpec((1, n), lambda k: (0, 0))],
        out_specs=pl.BlockSpec((n // d, 8, d), lambda k: (0, 0, 0)),
        name="ada_mod",
        compiler_params=pltpu.CompilerParams(dimension_semantics=("arbitrary",)),
    )(c_ctx.reshape(1, d), c, w_ada, b_ada.reshape(1, n))


def _prep_kernel(wint_ref, wuq_ref, wukv_ref, wgf_ref, wgb_ref, win_o, wuq_o, wk_o, wvt_o, wg_o):
    cols = wint_ref.shape[1]
    for dst, src, n in ((Z_Q, 0, W_KR), (Z_GQ, W_GQ, W_GF - W_GQ), (Z_GO, W_GO, GV)):
        win_o[:, dst:dst + n] = wint_ref[src:src + n, :].T.astype(BF16)
    z32 = jnp.zeros((32, cols), F32)
    misc_t = jnp.concatenate([wint_ref[W_GF:W_GO, :], z32, wint_ref[W_KR:W_GQ, :], z32], axis=0)
    win_o[:, Z_MISC:Z_COLS] = misc_t.T.astype(BF16)

    u = wuq_ref[...]
    zq = jnp.zeros((u.shape[0], HEAD_PAD - MLA_QK), F32)
    for hd in range(MLA_HEADS):
        blk = jnp.concatenate([u[:, hd * MLA_QK:(hd + 1) * MLA_QK], zq], axis=1)
        wuq_o[:, hd * HEAD_PAD:(hd + 1) * HEAD_PAD] = blk.astype(BF16)

    @pl.when(pl.program_id(0) == 0)
    def _():
        kv = wukv_ref[...]
        per = MLA_NOPE + MLA_V
        lane = lax.broadcasted_iota(jnp.int32, (kv.shape[0], per), 1)
        for hd in range(MLA_HEADS):
            blk = kv[:, hd * per:(hd + 1) * per]
            wk_o[:, hd * HEAD_PAD:(hd + 1) * HEAD_PAD] = jnp.where(lane < MLA_NOPE, blk, 0.0).astype(BF16)
        wv = jnp.concatenate([kv[:, hd * per + MLA_NOPE:(hd + 1) * per] for hd in range(MLA_HEADS)], axis=1)
        wvt_o[...] = wv.T.astype(BF16)

        wg_o[...] = jnp.zeros(wg_o.shape, BF16)
        wg_o[0:GATE_RANK, 0:GQK] = wgf_ref[...].astype(BF16)
        wg_o[GATE_RANK:2 * GATE_RANK, GQK:2 * GQK] = wgb_ref[...].astype(BF16)


def _prep_in_weights(w_in, w_uq, w_ukv, w_gate_f, w_gate_b):
    d = w_in.shape[1]
    steps = 4
    w_in_t = jnp.swapaxes(w_in, 1, 2)
    rb3 = lambda r, c: pl.BlockSpec((None, r // steps, c), lambda i: (0, i, 0))
    rb = lambda r, c: pl.BlockSpec((r // steps, c), lambda i: (i, 0))
    full3 = lambda shape: pl.BlockSpec((None,) + tuple(shape[1:]), lambda i: (0, 0, 0))
    full = lambda shape: pl.BlockSpec(shape, lambda i: (0, 0))
    return pl.pallas_call(
        _prep_kernel,
        out_shape=[jax.ShapeDtypeStruct((d, Z_COLS), BF16),
                   jax.ShapeDtypeStruct((Q_LORA, QPAD), BF16),
                   jax.ShapeDtypeStruct((KV_LORA, QPAD), BF16),
                   jax.ShapeDtypeStruct((VALL, KV_LORA), BF16),
                   jax.ShapeDtypeStruct((LANES, 2 * GQK), BF16)],
        grid=(steps,),
        in_specs=[pl.BlockSpec((None, W_COLS, d // steps), lambda i: (0, 0, i)),
                  rb3(Q_LORA, MLA_HEADS * MLA_QK), full3(w_ukv.shape),
                  full3(w_gate_f.shape), full3(w_gate_b.shape)],
        out_specs=[rb(d, Z_COLS), rb(Q_LORA, QPAD), full((KV_LORA, QPAD)), full((VALL, KV_LORA)),
                   full((LANES, 2 * GQK))],
        name="weight_prep",
        compiler_params=pltpu.CompilerParams(dimension_semantics=("arbitrary",)),
    )(w_in_t, w_uq, w_ukv, w_gate_f, w_gate_b)


def _inproj_kernel(*refs, latent, mod_row):
    (x_ref, mod_ref, nw_ref, win_ref, qn_ref, wuq_ref, kvn_ref, wk_ref, wvt_ref, wg_ref, bgf_ref, bgb_ref) = refs[:12]
    if latent:
        cos_ref, sa_ref, sb_ref = refs[12:15]
        q_ref, k_ref, vt_ref, gq_ref, gk_ref, gv_ref, gf_ref, gb_ref, go_ref = refs[15:]
    else:
        (attn_ref, gq_ref, gk_ref, gv_ref, gf_ref, gb_ref, go_ref, ckv_ref, krt_ref,
         q_ref, k_ref, vt_ref, st_ref, p_ref) = refs[12:]
    seq = q_ref.shape[1]

    sh1, sc1 = _mod_rows(mod_ref, mod_row(pl.program_id(0)))[:2]
    scale = MLA_QK ** -0.5 * LOG2_E
    lane = lax.broadcasted_iota(jnp.int32, (INPROJ_SUB, LANES), 1)
    in_rope = (lane >= ROPE_LANE0) & (lane < ROPE_LANE0 + MLA_ROPE)

    def sub_tile(r0):
        rows = slice(r0, r0 + INPROJ_SUB)
        h = (_rms(x_ref[rows, :], nw_ref[...]) * (1.0 + sc1) + sh1).astype(BF16)
        yield
        z_all = jnp.dot(h, win_ref[...], preferred_element_type=F32)
        z = lambda lo, n: z_all[:, lo:lo + n]
        yield
        qn = _rms(z(Z_Q, Q_LORA), qn_ref[...]).astype(BF16)
        ckv = _rms(z(Z_KV, KV_LORA), kvn_ref[...])
        ckv_b = ckv.astype(BF16)
        misc = z(Z_MISC, LANES)
        yield
        q = jnp.dot(qn, wuq_ref[...], preferred_element_type=F32)
        kn = jnp.dot(ckv_b, wk_ref[...], preferred_element_type=F32)
        vt_ref[:, rows] = lax.dot_general(wvt_ref[...], ckv_b, _NT,
                                          preferred_element_type=F32).astype(BF16)
        gpre = jnp.dot(misc.astype(BF16), wg_ref[...], preferred_element_type=F32)
        yield
        if latent:
            cos, sa, sb = cos_ref[rows, :], sa_ref[rows, :], sb_ref[rows, :]

            def rope(t):
                return t * cos + pltpu.roll(t, LANES - 8, 1) * sa + pltpu.roll(t, 8, 1) * sb
        else:
            def rope(t):
                return t

        krope = rope(misc)
        for hd in range(MLA_HEADS):
            sl = slice(hd * HEAD_PAD, (hd + 1) * HEAD_PAD)
            q_h = (rope(q[:, sl]) * scale).astype(BF16)
            k_h = jnp.where(in_rope, krope, kn[:, sl]).astype(BF16)
            for b in range(INPROJ_SUB // seq):
                q_ref[r0 // seq + b, :, sl] = q_h[b * seq:(b + 1) * seq, :]
                k_ref[r0 // seq + b, :, sl] = k_h[b * seq:(b + 1) * seq, :]
        gq_ref[rows, :] = z(Z_GQ, GQK)
        gk_ref[rows, :] = z(Z_GK, GQK)
        gv_ref[rows, :] = z(Z_GV, GV).astype(BF16)
        go_ref[rows, :] = z(Z_GO, GV)
        gf_ref[rows, :] = _log_sigmoid(gpre[:, :GQK] + bgf_ref[...]) * (1.0 / GATE_NORM)
        gb_ref[rows, :] = _log_sigmoid(gpre[:, GQK:] + bgb_ref[...]) * (1.0 / GATE_NORM)
        if not latent:
            ckv_ref[rows, :] = ckv
            misc_t = misc.T
            n = krt_ref.shape[2]
            for b in range(INPROJ_SUB // n):
                krt_ref[r0 // n + b] = misc_t[ROPE_LANE0:ROPE_LANE0 + MLA_ROPE, b * n:(b + 1) * n]

    _interleave([sub_tile(r0) for r0 in range(0, x_ref.shape[0], INPROJ_SUB)])
    if not latent:
        _attn_pipeline(q_ref, k_ref, vt_ref, attn_ref, st_ref, p_ref, None, q_ref.shape[0])


def _inproj(x2d, mod, mod_row, weights, rope_tabs, tm, seq_len):
    n_tok, d = x2d.shape
    latent = rope_tabs is not None
    tiles_per_seq = max(seq_len // tm, 1)
    nw, win, qn, wuq, kvn, wk, wvt, wg, bgf, bgb = weights
    row = lambda i: (i, 0)
    in_specs = [pl.BlockSpec((tm, d), row), _const_spec(mod.shape),
                _const_spec(nw.shape), _const_spec(win.shape), _const_spec(qn.shape),
                _const_spec(wuq.shape), _const_spec(kvn.shape), _const_spec(wk.shape),
                _const_spec(wvt.shape), _const_spec(wg.shape), _const_spec(bgf.shape),
                _const_spec(bgb.shape)]
    args = [x2d, mod, nw, win, qn, wuq, kvn, wk, wvt, wg, bgf, bgb]
    if latent:
        tab = pl.BlockSpec((tm, LANES), lambda i: (i % tiles_per_seq, 0))
        in_specs += [tab, tab, tab]
        args += list(rope_tabs)
    per_token = lambda c, dt: (jax.ShapeDtypeStruct((n_tok, c), dt), pl.BlockSpec((tm, c), row))
    seq = tm if latent else seq_len
    per_seq = lambda r, c, dt: (jax.ShapeDtypeStruct((n_tok // seq, r, c), dt),
                                pl.BlockSpec((tm // seq, r, c), lambda i: (i, 0, 0)))
    gla_outs = [per_token(GQK, F32), per_token(GQK, F32), per_token(GV, BF16),
                per_token(GQK, F32), per_token(GQK, F32), per_token(GV, F32)]
    scratch = []
    if latent:
        outs = [per_seq(seq, QPAD, BF16), per_seq(seq, QPAD, BF16),
                (jax.ShapeDtypeStruct((VALL, n_tok), BF16), pl.BlockSpec((VALL, tm), lambda i: (0, i)))]
        outs += gla_outs
    else:
        outs = [per_seq(seq, VALL, BF16)] + gla_outs + [per_token(KV_LORA, F32), per_seq(MLA_ROPE, seq, F32)]
        scratch = [pltpu.VMEM((tm // seq, seq, QPAD), BF16), pltpu.VMEM((tm // seq, seq, QPAD), BF16),
                   pltpu.VMEM((VALL, tm), BF16), pltpu.VMEM((2, seq, Q_TILE), F32),
                   pltpu.VMEM((2, seq, Q_TILE), BF16)]
    return pl.pallas_call(
        functools.partial(_inproj_kernel, latent=latent, mod_row=mod_row),
        out_shape=[o[0] for o in outs],
        grid=(n_tok // tm,),
        in_specs=in_specs,
        out_specs=[o[1] for o in outs],
        scratch_shapes=scratch,
        name="inproj_lat" if latent else "inproj_ctx",
        compiler_params=pltpu.CompilerParams(dimension_semantics=("arbitrary",),
                                             vmem_limit_bytes=VMEM_LIMIT),
    )(*args)


def _decomp_kernel(ckv_ref, krt_ref, wk_ref, wvt_ref, k_ref, vt_ref):
    ckv_b = ckv_ref[...].astype(BF16)
    kn = jnp.dot(ckv_b, wk_ref[...], preferred_element_type=F32)
    n_keys = krt_ref.shape[1]
    kr = jnp.concatenate([jnp.zeros((ROPE_LANE0, n_keys), F32), krt_ref[...],
                          jnp.zeros((LANES - ROPE_LANE0 - MLA_ROPE, n_keys), F32)], axis=0).T
    lane = lax.broadcasted_iota(jnp.int32, kr.shape, 1)
    in_rope = (lane >= ROPE_LANE0) & (lane < ROPE_LANE0 + MLA_ROPE)
    for hd in range(MLA_HEADS):
        sl = slice(hd * HEAD_PAD, (hd + 1) * HEAD_PAD)
        k_ref[:, sl] = jnp.where(in_rope, kr, kn[:, sl]).astype(BF16)
    vt_ref[...] = lax.dot_general(wvt_ref[...], ckv_b, _NT, preferred_element_type=F32).astype(BF16)


def _decomp(ckv, kr_t, wk, wvt):
    b, s, _ = ckv.shape
    return pl.pallas_call(
        _decomp_kernel,
        out_shape=[jax.ShapeDtypeStruct((b, s, QPAD), BF16), jax.ShapeDtypeStruct((VALL, b * s), BF16)],
        grid=(b,),
        in_specs=[pl.BlockSpec((None, s, KV_LORA), lambda i: (i, 0, 0)),
                  pl.BlockSpec((None, MLA_ROPE, s), lambda i: (i, 0, 0)),
                  _const_spec(wk.shape), _const_spec(wvt.shape)],
        out_specs=[pl.BlockSpec((None, s, QPAD), lambda i: (i, 0, 0)),
                   pl.BlockSpec((VALL, s), lambda i: (0, i))],
        name="ctx_decompress",
        compiler_params=pltpu.CompilerParams(dimension_semantics=("arbitrary",)),
    )(ckv, kr_t, wk, wvt)


def _attn_kernel(*refs, has_ctx, n_seqs, n_side):
    n_in = 5 if has_ctx else 3
    side_in, refs = refs[n_in:n_in + n_side], refs[:n_in] + refs[n_in + n_side:]
    side_out, refs = refs[n_in + 1:n_in + 1 + n_side], refs[:n_in + 1] + refs[n_in + 1 + n_side:]
    if has_ctx:
        q_ref, kc_ref, vct_ref, k_ref, vt_ref, o_ref, st_ref, p_ref = refs
    else:
        q_ref, k_ref, vt_ref, o_ref, st_ref, p_ref = refs

    for src, dst in zip(side_in, side_out):
        dst[...] = src[...].astype(BF16)
    _attn_pipeline(q_ref, k_ref, vt_ref, o_ref, st_ref, p_ref,
                   (kc_ref, vct_ref) if has_ctx else None, n_seqs)


def _attn_pipeline(q_ref, k_ref, vt_ref, o_ref, st_ref, p_ref, ctx_refs, n_seqs):
    tq = Q_TILE

    def key_blocks(bi):
        srcs = [ctx_refs] if ctx_refs is not None else []
        blocks, row0 = [], 0
        for kr, vr in srcs + [(k_ref, vt_ref)]:
            n_keys = kr.shape[1]
            size = min(KEY_BLOCK, n_keys)
            for r in range(0, n_keys, size):
                blocks.append((kr, vr, r, bi * n_keys + r, size, row0))
                row0 += size
        return blocks

    units = [(bi, slice(q0, q0 + tq), hd) for bi in range(n_seqs)
             for q0 in range(0, q_ref.shape[1], tq) for hd in range(MLA_HEADS)]
    col_max = [None] * len(units)
    pair = []
    for stage in range(len(units) + 2):
        ua, ub, uc = stage, stage - 1, stage - 2
        run_max = None
        acc = jnp.zeros((MLA_V + ONES_ROWS, tq), F32)
        for j in range(len(key_blocks(0))):
            if ua < len(units):
                bi, qrows, hd = units[ua]
                kr, _, r0, _, size, srow = key_blocks(bi)[j]
                sl = slice(hd * HEAD_PAD, (hd + 1) * HEAD_PAD)
                st = lax.dot_general(kr[bi, r0:r0 + size, sl], q_ref[bi, qrows, sl], _NT,
                                     preferred_element_type=F32)
                st_ref[ua % 2, srow:srow + size, :] = st
                blk_max = jnp.max(st.reshape(size // 8, 8, tq), axis=0)
                run_max = blk_max if run_max is None else jnp.maximum(run_max, blk_max)
            if 0 <= ub < len(units):
                _, _, _, _, size, srow = key_blocks(0)[j]
                p_ref[ub % 2, srow:srow + size, :] = jnp.exp2(
                    st_ref[ub % 2, srow:srow + size, :] - col_max[ub]).astype(BF16)
            if uc >= 0:
                bi, _, hd = units[uc]
                _, vr, _, c0, size, srow = key_blocks(bi)[j]
                v_aug = jnp.concatenate([vr[hd * MLA_V:(hd + 1) * MLA_V, c0:c0 + size],
                                         jnp.ones((ONES_ROWS, size), BF16)], axis=0)
                acc = acc + jnp.dot(v_aug, p_ref[uc % 2, srow:srow + size, :],
                                    preferred_element_type=F32)
        if ua < len(units):
            col_max[ua] = jnp.max(run_max, axis=0, keepdims=True)
        if uc >= 0:
            bi, qrows, hd = units[uc]
            pair.append(acc[:MLA_V, :] / acc[MLA_V:MLA_V + 1, :])
            if len(pair) == 2:
                o_ref[bi, qrows, (hd - 1) * MLA_V:(hd + 1) * MLA_V] = (
                    jnp.concatenate(pair, axis=0).T.astype(BF16))
                pair = []


def _attention(q, k, vt, ctx_kv, tq, n_seqs, side_weights=()):
    b, t, _ = q.shape
    steps = (b // n_seqs) * (t // tq)
    assert (n_seqs == 1 or tq == t) and tq % Q_TILE == 0
    has_ctx = ctx_kv is not None
    in_specs = [pl.BlockSpec((n_seqs, tq, QPAD), lambda i, j: (i, j, 0))]
    args = [q]
    if has_ctx:
        kc, vct = ctx_kv
        s = kc.shape[1]
        in_specs += [pl.BlockSpec((n_seqs, s, QPAD), lambda i, j: (i, 0, 0)),
                     pl.BlockSpec((VALL, n_seqs * s), lambda i, j: (0, i))]
        args += [kc, vct]
    in_specs += [pl.BlockSpec((n_seqs, t, QPAD), lambda i, j: (i, 0, 0)),
                 pl.BlockSpec((VALL, n_seqs * t), lambda i, j: (0, i))]
    args += [k, vt]
    out_shape = [jax.ShapeDtypeStruct((b, t, VALL), BF16)]
    out_specs = [pl.BlockSpec((n_seqs, tq, VALL), lambda i, j: (i, j, 0))]
    nj = t // tq
    for w in side_weights:
        _, rows, cols = w.shape
        assert rows % (16 * steps) == 0
        in_specs.append(pl.BlockSpec((None, rows // steps, cols), lambda i, j: (0, i * nj + j, 0)))
        args.append(w)
        out_shape.append(jax.ShapeDtypeStruct((rows, cols), BF16))
        out_specs.append(pl.BlockSpec((rows // steps, cols), lambda i, j: (i * nj + j, 0)))
    return pl.pallas_call(
        functools.partial(_attn_kernel, has_ctx=has_ctx, n_seqs=n_seqs, n_side=len(side_weights)),
        out_shape=out_shape,
        grid=(b // n_seqs, t // tq),
        in_specs=in_specs,
        out_specs=out_specs,
        scratch_shapes=[pltpu.VMEM((2, t + (s if has_ctx else 0), Q_TILE), F32),
                        pltpu.VMEM((2, t + (s if has_ctx else 0), Q_TILE), BF16)],
        name="mla_attn_lat" if has_ctx else "mla_attn_ctx",
        compiler_params=pltpu.CompilerParams(dimension_semantics=("arbitrary", "arbitrary"),
                                             vmem_limit_bytes=VMEM_LIMIT),
    )(*args)


def _gla_kernel(*refs, n_tiles, n_seqs, zero_init):
    gq_ref, gk_ref, gv_ref, gf_ref, gb_ref, go_ref = refs[:6]
    if zero_init:
        gn_ref, o_ref, sf_ref, sb_ref, oacc_ref, bdqk_ref, tri_ref, hm_ref = refs[6:]
    else:
        (sf0_ref, sb0_ref, gn_ref, o_ref, sf_ref, sb_ref, oacc_ref,
         bdqk_ref, tri_ref, hm_ref) = refs[6:]
    g_refs = (gf_ref, gb_ref)
    state_refs = (sf_ref, sb_ref)

    @pl.when(pl.program_id(0) == 0)
    def _():
        ri = lax.broadcasted_iota(jnp.int32, (GLA_TILE, GLA_TILE), 0)
        ci = lax.broadcasted_iota(jnp.int32, (GLA_TILE, GLA_TILE), 1)
        same_chunk = (ri // CHUNK) == (ci // CHUNK)
        bdqk_ref[...] = jnp.where(same_chunk, 1.0, 0.0).astype(BF16)
        tri_ref[0] = jnp.where(same_chunk & (ri >= ci), 1.0, 0.0)
        tri_ref[1] = jnp.where(same_chunk & (ci >= ri), 1.0, 0.0)
        hm_ref[...] = jnp.where(
            lax.broadcasted_iota(jnp.int32, (GLA_HEADS * GLA_TILE, GQK), 0) // GLA_TILE
            == lax.broadcasted_iota(jnp.int32, (GLA_HEADS * GLA_TILE, GQK), 1) // GLA_DK,
            1.0, 0.0).astype(BF16)

    row8 = lax.broadcasted_iota(jnp.int32, (8, GQK), 0)

    def row_slice(start, size):
        return pl.ds(start if isinstance(start, int) else pl.multiple_of(start, size), size)

    def tile_rows(t):
        return row_slice(t * GLA_TILE, GLA_TILE)

    finished = set()

    def total_row(c, d):
        return c * CHUNK + (CHUNK - 1 if d == 0 else 0)

    def tile_dir(b, t, d):
        rows = tile_rows(t)
        g = g_refs[d][b, rows, :]
        g_hi = g.astype(BF16)
        g_lo = (g - g_hi.astype(F32)).astype(BF16)
        tri_b = tri_ref[d].astype(BF16)
        cum = (jnp.dot(tri_b, g_hi, preferred_element_type=F32)
               + jnp.dot(tri_b, g_lo, preferred_element_type=F32))
        yield
        totals = [cum[total_row(c, d):total_row(c, d) + 1, :] for c in range(CHUNKS_PER_TILE)]
        tot8 = jnp.zeros((8, GQK), F32)
        for c in range(CHUNKS_PER_TILE):
            tot8 = jnp.where(row8 == c, totals[c], tot8)
        dec_t = jnp.concatenate([jnp.exp(tot8), jnp.zeros((LANES - 8, GQK), F32)], axis=0).T
        q = gq_ref[b, rows, :] * (GLA_DK ** -0.5)
        k = gk_ref[b, rows, :]
        v = gv_ref[b, rows, :]
        tot = jnp.concatenate([jnp.broadcast_to(tc, (CHUNK, GQK)) for tc in totals], axis=0)
        qe = (q * jnp.exp(cum)).astype(BF16)
        ke = (k * jnp.exp(-cum)).astype(BF16)
        kd_t = (k * jnp.exp(tot - cum)).T.astype(BF16)
        bd_qk = bdqk_ref[...] > 0
        tri = tri_ref[d] > 0

        qm = jnp.where(hm_ref[...] > 0, jnp.tile(qe, (GLA_HEADS, 1)), 0.0)
        yield
        att = lax.dot_general(qm, ke, _NT, preferred_element_type=F32)

        yield
        intra, upd = [], []
        for hd in range(GLA_HEADS):
            vh = v[:, hd * GLA_DV:(hd + 1) * GLA_DV]
            a_h = jnp.where(tri, att[hd * GLA_TILE:(hd + 1) * GLA_TILE, :], 0.0).astype(BF16)
            intra.append(jnp.dot(a_h, vh, preferred_element_type=F32))
            kd_h = jnp.tile(kd_t[hd * GLA_DK:(hd + 1) * GLA_DK, :], (CHUNKS_PER_TILE, 1))
            upd.append(jnp.dot(jnp.where(bd_qk, kd_h, 0.0), vh, preferred_element_type=F32))

        yield
        state = [state_refs[d][b, hd] for hd in range(GLA_HEADS)]
        order = range(CHUNKS_PER_TILE) if d == 0 else range(CHUNKS_PER_TILE - 1, -1, -1)
        seen = {}
        for c in order:
            seen[c] = jnp.concatenate(state, axis=0).astype(BF16)
            decay = jnp.broadcast_to(dec_t[:, c:c + 1], (GQK, GLA_DV))
            for hd in range(GLA_HEADS):
                ks = slice(hd * GLA_DK, (hd + 1) * GLA_DK)
                state[hd] = decay[ks, :] * state[hd] + upd[hd][c * CHUNK:(c + 1) * CHUNK, :]
        for hd in range(GLA_HEADS):
            state_refs[d][b, hd] = state[hd]

        yield
        for c in range(CHUNKS_PER_TILE):
            cr = slice(c * CHUNK, (c + 1) * CHUNK)
            q_c = jnp.concatenate([qm[hd * GLA_TILE + c * CHUNK:hd * GLA_TILE + (c + 1) * CHUNK, :]
                                   for hd in range(GLA_HEADS)], axis=0)
            inter = jnp.dot(q_c, seen[c], preferred_element_type=F32)
            o = jnp.concatenate([intra[hd][cr, :] + inter[hd * CHUNK:(hd + 1) * CHUNK, :]
                                 for hd in range(GLA_HEADS)], axis=1)
            oacc_ref[d, b, row_slice(t * GLA_TILE + c * CHUNK, CHUNK), :] = o
        if isinstance(t, int):
            finished.add((b, t, d))

    if zero_init:
        sf_ref[...] = jnp.zeros(sf_ref.shape, F32)
        sb_ref[...] = jnp.zeros(sb_ref.shape, F32)
    else:
        sf_ref[...] = sf0_ref[...]
        sb_ref[...] = sb0_ref[...]

    gn = gn_ref[...]

    def epilogue(t):
        while isinstance(t, int) and not all((b, t, d) in finished for b in range(n_seqs) for d in range(2)):
            yield
        rows = tile_rows(t)
        for b in range(n_seqs):
            for hd in range(GLA_HEADS):
                vs = slice(hd * GLA_DV, (hd + 1) * GLA_DV)
                o = _rms(oacc_ref[0, b, rows, vs] + oacc_ref[1, b, rows, vs], gn)
                go = go_ref[b, rows, vs]
                o_ref[b, rows, vs] = (o * (go * _sigmoid(go))).astype(BF16)
        return
        yield

    def main_chains(first_tile, n):
        chains = []
        for t in [first_tile + u for u in range(n)]:
            for b in range(n_seqs):
                chains += [tile_dir(b, t, 0), tile_dir(b, n_tiles - 1 - t, 1)]
        return chains

    if n_tiles <= GLA_STATIC_TILES:
        _interleave(main_chains(0, n_tiles) + [epilogue(t) for t in range(n_tiles)])
    else:
        per_step = next(c for c in (4, 2, 1) if n_tiles % c == 0)

        def main_body(i, carry):
            _interleave(main_chains(i * per_step, per_step))
            return carry

        lax.fori_loop(0, n_tiles // per_step, main_body, 0)

        def epilogue_body(t, carry):
            _interleave([epilogue(t)])
            return carry

        lax.fori_loop(0, n_tiles, epilogue_body, 0)


def _gla(gq, gk, gv, gf, gb, go, init_states, gn, n_seqs):
    b, t, _ = gq.shape
    n_tiles = t // GLA_TILE
    zero_init = init_states is None
    seq = lambda c: pl.BlockSpec((n_seqs, t, c), lambda i: (i, 0, 0))
    st = pl.BlockSpec((n_seqs, GLA_HEADS, GLA_DK, GLA_DV), lambda i: (i, 0, 0, 0))
    in_specs = [seq(GQK), seq(GQK), seq(GV), seq(GQK), seq(GQK), seq(GV)]
    args = [gq, gk, gv, gf, gb, go]
    if not zero_init:
        in_specs += [st, st]
        args += list(init_states)
    in_specs.append(_const_spec(gn.shape))
    args.append(gn)
    return pl.pallas_call(
        functools.partial(_gla_kernel, n_tiles=n_tiles, n_seqs=n_seqs, zero_init=zero_init),
        out_shape=[jax.ShapeDtypeStruct((b, t, GV), BF16),
                   jax.ShapeDtypeStruct((b, GLA_HEADS, GLA_DK, GLA_DV), F32),
                   jax.ShapeDtypeStruct((b, GLA_HEADS, GLA_DK, GLA_DV), F32)],
        grid=(b // n_seqs,),
        in_specs=in_specs,
        out_specs=[seq(GV), st, st],
        scratch_shapes=[pltpu.VMEM((2, n_seqs, t, GV), F32),
                        pltpu.VMEM((GQK, GLA_TILE), BF16),
                        pltpu.VMEM((2, GLA_TILE, GLA_TILE), F32),
                        pltpu.VMEM((GLA_HEADS * GLA_TILE, GQK), BF16)],
        name="gla_%d" % t,
        compiler_params=pltpu.CompilerParams(dimension_semantics=("arbitrary",),
                                             vmem_limit_bytes=VMEM_LIMIT),
    )(*args)


def _ffn_kernel(xp_ref, xs_ref, atp_ref, ats_ref, glp_ref, gls_ref, mod_ref, wout_ref, nf_ref,
                wfi_ref, wfo_ref, fn_ref, yp_ref, ys_ref, act_ref, *, ctx_tiles, tiles_per_seq):
    def sub_tile(x_ref, at_ref, gl_ref, y_ref, mod_row, r0):
        rows = slice(r0, r0 + FFN_SUB)
        gt1, sh2, sc2, gt2 = _mod_rows(mod_ref, mod_row)[2:]
        mix = (jnp.dot(at_ref[rows, :], wout_ref[0:VALL, :], preferred_element_type=F32)
               + jnp.dot(gl_ref[rows, :], wout_ref[VALL:, :], preferred_element_type=F32))
        yield
        x1 = x_ref[rows, :] + gt1 * mix
        h2 = (_rms(x1, nf_ref[...]) * (1.0 + sc2) + sh2).astype(BF16)
        yield
        for j in range(N_FF_CHUNKS):
            cs = slice(j * FF_CHUNK, (j + 1) * FF_CHUNK)
            a = jnp.dot(h2, wfi_ref[:, cs], preferred_element_type=F32)
            g = jnp.dot(h2, wfi_ref[:, D_FF + j * FF_CHUNK:D_FF + (j + 1) * FF_CHUNK],
                        preferred_element_type=F32)
            act_ref[rows, cs] = (a * _sigmoid(a) * g).astype(BF16)
            if j in FFN_PHASE_ENDS:
                yield
        ff = jnp.dot(act_ref[rows, :], wfo_ref[...], preferred_element_type=F32)
        yield
        x2 = x1 + gt2 * ff
        y_ref[rows, :] = _rms(x2, fn_ref[...])

    def tile(x_ref, at_ref, gl_ref, y_ref, mod_row):
        _interleave([sub_tile(x_ref, at_ref, gl_ref, y_ref, mod_row, r0)
                     for r0 in range(0, x_ref.shape[0], FFN_SUB)])

    t = pl.program_id(0)

    @pl.when(t < ctx_tiles)
    def _():
        tile(xp_ref, atp_ref, glp_ref, yp_ref, 0)

    @pl.when(t >= ctx_tiles)
    def _():
        tile(xs_ref, ats_ref, gls_ref, ys_ref, 1 + (t - ctx_tiles) // tiles_per_seq)


def _ffn(xp, xs, attn_p, attn_s, gla_p, gla_s, mod, wout, nf, wfi, wfo, fn, tm, tiles_per_seq):
    d = xp.shape[1]
    ctx_tiles = xp.shape[0] // tm
    lat_tiles = xs.shape[0] // tm
    ctx_map = lambda s: (jnp.minimum(s, ctx_tiles - 1), 0)
    lat_map = lambda s: (jnp.maximum(s - ctx_tiles, 0), 0)
    tile = lambda c, m: pl.BlockSpec((tm, c), m)
    return pl.pallas_call(
        functools.partial(_ffn_kernel, ctx_tiles=ctx_tiles, tiles_per_seq=tiles_per_seq),
        out_shape=[jax.ShapeDtypeStruct(xp.shape, F32), jax.ShapeDtypeStruct(xs.shape, F32)],
        grid=(ctx_tiles + lat_tiles,),
        in_specs=[tile(d, ctx_map), tile(d, lat_map), tile(VALL, ctx_map), tile(VALL, lat_map),
                  tile(GV, ctx_map), tile(GV, lat_map), _const_spec(mod.shape),
                  _const_spec(wout.shape), _const_spec(nf.shape), _const_spec(wfi.shape),
                  _const_spec(wfo.shape), _const_spec(fn.shape)],
        out_specs=[tile(d, ctx_map), tile(d, lat_map)],
        scratch_shapes=[pltpu.VMEM((tm, D_FF), BF16)],
        name="out_ffn",
        compiler_params=pltpu.CompilerParams(dimension_semantics=("arbitrary",),
                                             vmem_limit_bytes=VMEM_LIMIT),
    )(xp, xs, attn_p, attn_s, gla_p, gla_s, mod, wout, nf, wfi, wfo, fn)


def _rope_tables(n_tokens):
    t = np.arange(n_tokens)
    row = (t // GRID_W).astype(np.float32)
    col = (t % GRID_W).astype(np.float32)
    half = MLA_ROPE // 2
    inv = (np.float32(ROPE_BASE) ** (-np.arange(0, half, 2, dtype=np.float32) / np.float32(half))).astype(np.float32)
    ang_r = row[:, None] * inv
    ang_c = col[:, None] * inv
    ang = np.concatenate([ang_r, ang_r, ang_c, ang_c], axis=-1).astype(np.float32)
    cos, sin = np.cos(ang), np.sin(ang)
    first = (np.arange(MLA_ROPE) % half) < (half // 2)
    cos_t = np.ones((n_tokens, LANES), np.float32)
    sa_t = np.zeros((n_tokens, LANES), np.float32)
    sb_t = np.zeros((n_tokens, LANES), np.float32)
    cos_t[:, ROPE_LANE0:ROPE_LANE0 + MLA_ROPE] = cos
    sa_t[:, ROPE_LANE0:ROPE_LANE0 + MLA_ROPE] = np.where(first, -sin, 0.0)
    sb_t[:, ROPE_LANE0:ROPE_LANE0 + MLA_ROPE] = np.where(first, 0.0, sin)
    return jnp.asarray(cos_t), jnp.asarray(sa_t), jnp.asarray(sb_t)


def kernel(x_prompt, x_sample, cache_kv_latent, cache_k_rope, state_gla_fwd, state_gla_bwd, c, c_ctx, w_ada, b_ada, norm_attn, w_in, mla_q_norm, w_uq, mla_kv_norm, w_ukv, w_gate_f, b_gate_f, w_gate_b, b_gate_b, gla_norm, w_out, norm_ffn, w_ffn_in, w_ffn_out, final_norm):
    batch, seq, d = x_prompt.shape
    dec_batch, dec_seq, _ = x_sample.shape
    assert w_ada.shape[0] == 1 and w_in.shape[-1] == W_COLS and w_ffn_in.shape[-1] == 2 * D_FF
    l = 0

    mod = _ada(c_ctx, c, w_ada[l], b_ada[l])

    win, wuq, wk, wvt, wg = _prep_in_weights(w_in, w_uq, w_ukv, w_gate_f, w_gate_b)
    in_w = (norm_attn[l].reshape(1, d), win, mla_q_norm[l].reshape(1, Q_LORA), wuq,
            mla_kv_norm[l].reshape(1, KV_LORA), wk, wvt, wg, b_gate_f, b_gate_b)
    gn = gla_norm[l].reshape(1, GLA_DV)
    tm, tm_ffn = 512, 512
    r3 = lambda a, b_, t: a.reshape(b_, t, a.shape[-1])

    xp = x_prompt.reshape(batch * seq, d)
    (attn_p, gq, gk, gv, gf, gb, go, ckv, kr_t) = _inproj(xp, mod, lambda i: 0, in_w, None, tm, seq)
    gla_p, sf, sb = _gla(r3(gq, batch, seq), r3(gk, batch, seq), r3(gv, batch, seq), r3(gf, batch, seq),
                         r3(gb, batch, seq), r3(go, batch, seq), None, gn, GLA_CTX_SEQS)

    xs = x_sample.reshape(dec_batch * dec_seq, d)
    tiles = dec_seq // tm
    (q, k, vt, gq, gk, gv, gf, gb, go) = _inproj(xs, mod, lambda i: 1 + i // tiles, in_w,
                                                  _rope_tables(dec_seq), tm, dec_seq)
    kc, vct = _decomp(cache_kv_latent[:, l], jnp.swapaxes(cache_k_rope[:, l], 1, 2), wk, wvt)
    attn_s, wout, wfi, wfo = _attention(r3(q, dec_batch, dec_seq), r3(k, dec_batch, dec_seq), vt, (kc, vct),
                                        ATTN_LAT_QUERIES, 1, (w_out, w_ffn_in, w_ffn_out))
    gla_s, _, _ = _gla(r3(gq, dec_batch, dec_seq), r3(gk, dec_batch, dec_seq), r3(gv, dec_batch, dec_seq),
                       r3(gf, dec_batch, dec_seq), r3(gb, dec_batch, dec_seq), r3(go, dec_batch, dec_seq),
                       (state_gla_fwd[:, l].astype(F32), state_gla_bwd[:, l].astype(F32)), gn, 1)

    flat = lambda a: a.reshape(-1, a.shape[-1])
    y_prompt, y_sample = _ffn(xp, xs, flat(attn_p), flat(attn_s), flat(gla_p), flat(gla_s), mod,
                              wout, norm_ffn[l].reshape(1, d), wfi, wfo,
                              final_norm.reshape(1, d), tm_ffn, dec_seq // tm_ffn)
    y_prompt = y_prompt.reshape(batch, seq, d)
    y_sample = y_sample.reshape(dec_batch, dec_seq, d)

    new_kv_latent = ckv.reshape(batch, 1, seq, KV_LORA)
    new_k_rope = jnp.swapaxes(kr_t, 1, 2).reshape(batch, 1, seq, MLA_ROPE)
    new_state_fwd = sf.reshape(batch, 1, GLA_HEADS, GLA_DK, GLA_DV).astype(x_prompt.dtype)
    new_state_bwd = sb.reshape(batch, 1, GLA_HEADS, GLA_DK, GLA_DV).astype(x_prompt.dtype)
    return (y_prompt, y_sample, new_kv_latent, new_k_rope, new_state_fwd, new_state_bwd)
```

```python
import functools

import numpy as np
import jax
import jax.numpy as jnp
from jax import lax
from jax.experimental import pallas as pl
from jax.experimental.pallas import tpu as pltpu

F32 = jnp.float32
BF16 = jnp.bfloat16

GRID_W = 64
MLA_HEADS = 8
MLA_NOPE = 64
MLA_ROPE = 32
MLA_QK = MLA_NOPE + MLA_ROPE
MLA_V = 64
Q_LORA = 384
KV_LORA = 256
GLA_HEADS = 4
GLA_DK = 64
GLA_DV = 128
GATE_RANK = 16
GATE_NORM = 16.0
CHUNK = 64
D_FF = 2816
ROPE_BASE = 10000.0
EPS = 1e-6
LOG2_E = 1.4426950408889634

LANES = 128
HEAD_PAD = LANES
ROPE_LANE0 = MLA_NOPE
GQK = GLA_HEADS * GLA_DK
GV = GLA_HEADS * GLA_DV
QPAD = MLA_HEADS * HEAD_PAD
VALL = MLA_HEADS * MLA_V
ONES_ROWS = 16
KEY_BLOCK = 1024

W_KR = Q_LORA + KV_LORA
W_GQ = W_KR + MLA_ROPE
W_GF = W_GQ + 2 * GQK + GV
W_GO = W_GF + 2 * GATE_RANK
W_COLS = W_GO + GV

Z_Q = 0
Z_KV = Z_Q + Q_LORA
Z_GQ = Z_KV + KV_LORA
Z_GK = Z_GQ + GQK
Z_GV = Z_GK + GQK
Z_GO = Z_GV + GV
Z_MISC = Z_GO + GV
Z_COLS = Z_MISC + LANES

FF_CHUNK = 256
N_FF_CHUNKS = D_FF // FF_CHUNK
FFN_SUB = 256
FFN_PHASE_ENDS = (3, 7, 10)

GLA_TILE = 256
CHUNKS_PER_TILE = GLA_TILE // CHUNK
ADA_ROWS = 128
INPROJ_SUB = 512
Q_TILE = 256
ATTN_LAT_QUERIES = 512
GLA_STATIC_TILES = 8

VMEM_LIMIT = 56 * 1024 * 1024

_NT = (((1,), (1,)), ((), ()))


def _rms(x, w):
    return x * lax.rsqrt(jnp.mean(x * x, axis=-1, keepdims=True) + EPS) * w


def _sigmoid(x):
    return 1.0 / (1.0 + jnp.exp(-x))


def _log_sigmoid(x):
    return jnp.minimum(x, 0.0) - jnp.log1p(jnp.exp(-jnp.abs(x)))


def _interleave(chains):
    pending, active = list(chains), []
    while pending or active:
        if pending:
            active.append(pending.pop(0))
        for chain in list(active):
            try:
                next(chain)
            except StopIteration:
                active.remove(chain)


def _const_spec(shape):
    nd = len(shape)
    return pl.BlockSpec(shape, lambda *_: (0,) * nd, pipeline_mode=pl.Buffered(1))


def _mod_rows(mod_ref, r):
    return [mod_ref[k, pl.ds(r, 1), :] for k in range(mod_ref.shape[0])]


def _ada_kernel(cctx_ref, c_ref, w_ref, b_ref, o_ref):
    k = pl.program_id(0)
    d = o_ref.shape[2]
    row = lax.broadcasted_iota(jnp.int32, (8, cctx_ref.shape[1]), 0)
    cond = jnp.where(row == 0, cctx_ref[...], 0.0)
    for r in range(c_ref.shape[0]):
        cond = jnp.where(row == 1 + r, c_ref[r:r + 1, :], cond)
    s = (cond * _sigmoid(cond)).astype(BF16)
    part = jnp.dot(s, w_ref[...].astype(BF16), preferred_element_type=F32)
    for j in range(o_ref.shape[0]):
        sl = slice(j * d, (j + 1) * d)

        @pl.when(k == 0)
        def _():
            o_ref[j] = part[:, sl] + b_ref[:, sl]

        @pl.when(k > 0)
        def _():
            o_ref[j] += part[:, sl]


def _ada(c_ctx, c, w_ada, b_ada):
    d = w_ada.shape[0]
    n = w_ada.shape[1]
    assert 1 + c.shape[0] <= 8
    return pl.pallas_call(
        _ada_kernel,
        out_shape=jax.ShapeDtypeStruct((n // d, 8, d), F32),
        grid=(d // ADA_ROWS,),
        in_specs=[pl.BlockSpec((1, ADA_ROWS), lambda k: (0, k)),
                  pl.BlockSpec((c.shape[0], ADA_ROWS), lambda k: (0, k)),
                  pl.BlockSpec((ADA_ROWS, n), lambda k: (k, 0)),
                  pl.BlockSpec((1, n), lambda k: (0, 0))],
        out_specs=pl.BlockSpec((n // d, 8, d), lambda k: (0, 0, 0)),
        name="ada_mod",
        compiler_params=pltpu.CompilerParams(dimension_semantics=("arbitrary",)),
    )(c_ctx.reshape(1, d), c, w_ada, b_ada.reshape(1, n))


def _prep_kernel(wint_ref, wuq_ref, wukv_ref, wgf_ref, wgb_ref, win_o, wuq_o, wk_o, wvt_o, wg_o):
    cols = wint_ref.shape[1]
    for dst, src, n in ((Z_Q, 0, W_KR), (Z_GQ, W_GQ, W_GF - W_GQ), (Z_GO, W_GO, GV)):
        win_o[:, dst:dst + n] = wint_ref[src:src + n, :].T.astype(BF16)
    z32 = jnp.zeros((32, cols), F32)
    misc_t = jnp.concatenate([wint_ref[W_GF:W_GO, :], z32, wint_ref[W_KR:W_GQ, :], z32], axis=0)
    win_o[:, Z_MISC:Z_COLS] = misc_t.T.astype(BF16)

    u = wuq_ref[...]
    zq = jnp.zeros((u.shape[0], HEAD_PAD - MLA_QK), F32)
    for hd in range(MLA_HEADS):
        blk = jnp.concatenate([u[:, hd * MLA_QK:(hd + 1) * MLA_QK], zq], axis=1)
        wuq_o[:, hd * HEAD_PAD:(hd + 1) * HEAD_PAD] = blk.astype(BF16)

    @pl.when(pl.program_id(0) == 0)
    def _():
        kv = wukv_ref[...]
        per = MLA_NOPE + MLA_V
        lane = lax.broadcasted_iota(jnp.int32, (kv.shape[0], per), 1)
        for hd in range(MLA_HEADS):
            blk = kv[:, hd * per:(hd + 1) * per]
            wk_o[:, hd * HEAD_PAD:(hd + 1) * HEAD_PAD] = jnp.where(lane < MLA_NOPE, blk, 0.0).astype(BF16)
        wv = jnp.concatenate([kv[:, hd * per + MLA_NOPE:(hd + 1) * per] for hd in range(MLA_HEADS)], axis=1)
        wvt_o[...] = wv.T.astype(BF16)

        wg_o[...] = jnp.zeros(wg_o.shape, BF16)
        wg_o[0:GATE_RANK, 0:GQK] = wgf_ref[...].astype(BF16)
        wg_o[GATE_RANK:2 * GATE_RANK, GQK:2 * GQK] = wgb_ref[...].astype(BF16)


def _prep_in_weights(w_in, w_uq, w_ukv, w_gate_f, w_gate_b):
    d = w_in.shape[1]
    steps = 4
    w_in_t = jnp.swapaxes(w_in, 1, 2)
    rb3 = lambda r, c: pl.BlockSpec((None, r // steps, c), lambda i: (0, i, 0))
    rb = lambda r, c: pl.BlockSpec((r // steps, c), lambda i: (i, 0))
    full3 = lambda shape: pl.BlockSpec((None,) + tuple(shape[1:]), lambda i: (0, 0, 0))
    full = lambda shape: pl.BlockSpec(shape, lambda i: (0, 0))
    return pl.pallas_call(
        _prep_kernel,
        out_shape=[jax.ShapeDtypeStruct((d, Z_COLS), BF16),
                   jax.ShapeDtypeStruct((Q_LORA, QPAD), BF16),
                   jax.ShapeDtypeStruct((KV_LORA, QPAD), BF16),
                   jax.ShapeDtypeStruct((VALL, KV_LORA), BF16),
                   jax.ShapeDtypeStruct((LANES, 2 * GQK), BF16)],
        grid=(steps,),
        in_specs=[pl.BlockSpec((None, W_COLS, d // steps), lambda i: (0, 0, i)),
                  rb3(Q_LORA, MLA_HEADS * MLA_QK), full3(w_ukv.shape),
                  full3(w_gate_f.shape), full3(w_gate_b.shape)],
        out_specs=[rb(d, Z_COLS), rb(Q_LORA, QPAD), full((KV_LORA, QPAD)), full((VALL, KV_LORA)),
                   full((LANES, 2 * GQK))],
        name="weight_prep",
        compiler_params=pltpu.CompilerParams(dimension_semantics=("arbitrary",)),
    )(w_in_t, w_uq, w_ukv, w_gate_f, w_gate_b)


def _inproj_kernel(*refs, latent, mod_row):
    (x_ref, mod_ref, nw_ref, win_ref, qn_ref, wuq_ref, kvn_ref, wk_ref, wvt_ref, wg_ref, bgf_ref, bgb_ref) = refs[:12]
    if latent:
        cos_ref, sa_ref, sb_ref = refs[12:15]
        q_ref, k_ref, vt_ref, gq_ref, gk_ref, gv_ref, gf_ref, gb_ref, go_ref = refs[15:]
    else:
        (gn_ref, attn_ref, gla_ref, sf_ref, sb_ref, ckv_ref, krt_ref, q_ref, k_ref, vt_ref, st_ref, p_ref,
         gq_ref, gk_ref, gv_ref, gf_ref, gb_ref, go_ref) = refs[12:30]
        gla_scratch = refs[30:]
    seq = q_ref.shape[1]

    sh1, sc1 = _mod_rows(mod_ref, mod_row(pl.program_id(0)))[:2]
    scale = MLA_QK ** -0.5 * LOG2_E
    lane = lax.broadcasted_iota(jnp.int32, (INPROJ_SUB, LANES), 1)
    in_rope = (lane >= ROPE_LANE0) & (lane < ROPE_LANE0 + MLA_ROPE)

    def sub_tile(r0):
        rows = slice(r0, r0 + INPROJ_SUB)
        h = (_rms(x_ref[rows, :], nw_ref[...]) * (1.0 + sc1) + sh1).astype(BF16)
        yield
        z_all = jnp.dot(h, win_ref[...], preferred_element_type=F32)
        z = lambda lo, n: z_all[:, lo:lo + n]
        yield
        qn = _rms(z(Z_Q, Q_LORA), qn_ref[...]).astype(BF16)
        ckv = _rms(z(Z_KV, KV_LORA), kvn_ref[...])
        ckv_b = ckv.astype(BF16)
        misc = z(Z_MISC, LANES)
        yield
        q = jnp.dot(qn, wuq_ref[...], preferred_element_type=F32)
        kn = jnp.dot(ckv_b, wk_ref[...], preferred_element_type=F32)
        vt_ref[:, rows] = lax.dot_general(wvt_ref[...], ckv_b, _NT,
                                          preferred_element_type=F32).astype(BF16)
        gpre = jnp.dot(misc.astype(BF16), wg_ref[...], preferred_element_type=F32)
        yield
        if latent:
            cos, sa, sb = cos_ref[rows, :], sa_ref[rows, :], sb_ref[rows, :]

            def rope(t):
                return t * cos + pltpu.roll(t, LANES - 8, 1) * sa + pltpu.roll(t, 8, 1) * sb
        else:
            def rope(t):
                return t

        def put(ref, val, cols=slice(None)):
            if len(ref.shape) == 2:
                ref[rows, cols] = val
            else:
                for b in range(INPROJ_SUB // seq):
                    ref[r0 // seq + b, :, cols] = val[b * seq:(b + 1) * seq, :]

        krope = rope(misc)
        for hd in range(MLA_HEADS):
            sl = slice(hd * HEAD_PAD, (hd + 1) * HEAD_PAD)
            put(q_ref, (rope(q[:, sl]) * scale).astype(BF16), sl)
            put(k_ref, jnp.where(in_rope, krope, kn[:, sl]).astype(BF16), sl)
        put(gq_ref, z(Z_GQ, GQK))
        put(gk_ref, z(Z_GK, GQK))
        put(gv_ref, z(Z_GV, GV).astype(BF16))
        put(go_ref, z(Z_GO, GV))
        put(gf_ref, _log_sigmoid(gpre[:, :GQK] + bgf_ref[...]) * (1.0 / GATE_NORM))
        put(gb_ref, _log_sigmoid(gpre[:, GQK:] + bgb_ref[...]) * (1.0 / GATE_NORM))
        if not latent:
            ckv_ref[rows, :] = ckv
            misc_t = misc.T
            n = krt_ref.shape[2]
            for b in range(INPROJ_SUB // n):
                krt_ref[r0 // n + b] = misc_t[ROPE_LANE0:ROPE_LANE0 + MLA_ROPE, b * n:(b + 1) * n]

    _interleave([sub_tile(r0) for r0 in range(0, x_ref.shape[0], INPROJ_SUB)])
    if not latent:
        n_seqs = q_ref.shape[0]
        _attn_pipeline(q_ref, k_ref, vt_ref, attn_ref, st_ref, p_ref, None, n_seqs)
        _gla_body((gq_ref, gk_ref, gv_ref, gf_ref, gb_ref, go_ref), None, gn_ref,
                  (gla_ref, sf_ref, sb_ref), gla_scratch, seq // GLA_TILE, n_seqs)


def _inproj(x2d, mod, mod_row, weights, rope_tabs, tm, seq_len, gla_norm=None):
    n_tok, d = x2d.shape
    latent = rope_tabs is not None
    tiles_per_seq = max(seq_len // tm, 1)
    nw, win, qn, wuq, kvn, wk, wvt, wg, bgf, bgb = weights
    row = lambda i: (i, 0)
    in_specs = [pl.BlockSpec((tm, d), row), _const_spec(mod.shape),
                _const_spec(nw.shape), _const_spec(win.shape), _const_spec(qn.shape),
                _const_spec(wuq.shape), _const_spec(kvn.shape), _const_spec(wk.shape),
                _const_spec(wvt.shape), _const_spec(wg.shape), _const_spec(bgf.shape),
                _const_spec(bgb.shape)]
    args = [x2d, mod, nw, win, qn, wuq, kvn, wk, wvt, wg, bgf, bgb]
    if latent:
        tab = pl.BlockSpec((tm, LANES), lambda i: (i % tiles_per_seq, 0))
        in_specs += [tab, tab, tab]
        args += list(rope_tabs)
    per_token = lambda c, dt: (jax.ShapeDtypeStruct((n_tok, c), dt), pl.BlockSpec((tm, c), row))
    seq = tm if latent else seq_len
    per_seq = lambda r, c, dt: (jax.ShapeDtypeStruct((n_tok // seq, r, c), dt),
                                pl.BlockSpec((tm // seq, r, c), lambda i: (i, 0, 0)))
    gla_outs = [per_token(GQK, F32), per_token(GQK, F32), per_token(GV, BF16),
                per_token(GQK, F32), per_token(GQK, F32), per_token(GV, F32)]
    scratch = []
    if latent:
        outs = [per_seq(seq, QPAD, BF16), per_seq(seq, QPAD, BF16),
                (jax.ShapeDtypeStruct((VALL, n_tok), BF16), pl.BlockSpec((VALL, tm), lambda i: (0, i)))]
        outs += gla_outs
    else:
        in_specs.append(_const_spec(gla_norm.shape))
        args.append(gla_norm)
        state = (jax.ShapeDtypeStruct((n_tok // seq, GLA_HEADS, GLA_DK, GLA_DV), F32),
                 pl.BlockSpec((tm // seq, GLA_HEADS, GLA_DK, GLA_DV), lambda i: (i, 0, 0, 0)))
        outs = [per_seq(seq, VALL, BF16), per_seq(seq, GV, BF16), state, state,
                per_token(KV_LORA, F32), per_seq(MLA_ROPE, seq, F32)]
        vmem = lambda c, dt: pltpu.VMEM((tm // seq, seq, c), dt)
        scratch = [vmem(QPAD, BF16), vmem(QPAD, BF16), pltpu.VMEM((VALL, tm), BF16),
                   pltpu.VMEM((2, seq, Q_TILE), F32), pltpu.VMEM((2, seq, Q_TILE), BF16),
                   vmem(GQK, F32), vmem(GQK, F32), vmem(GV, BF16), vmem(GQK, F32), vmem(GQK, F32),
                   vmem(GV, F32)] + _gla_scratch(tm // seq, seq)
    return pl.pallas_call(
        functools.partial(_inproj_kernel, latent=latent, mod_row=mod_row),
        out_shape=[o[0] for o in outs],
        grid=(n_tok // tm,),
        in_specs=in_specs,
        out_specs=[o[1] for o in outs],
        scratch_shapes=scratch,
        name="inproj_lat" if latent else "inproj_ctx",
        compiler_params=pltpu.CompilerParams(dimension_semantics=("arbitrary",),
                                             vmem_limit_bytes=VMEM_LIMIT),
    )(*args)


def _decomp_kernel(ckv_ref, krt_ref, wk_ref, wvt_ref, k_ref, vt_ref):
    ckv_b = ckv_ref[...].astype(BF16)
    kn = jnp.dot(ckv_b, wk_ref[...], preferred_element_type=F32)
    n_keys = krt_ref.shape[1]
    kr = jnp.concatenate([jnp.zeros((ROPE_LANE0, n_keys), F32), krt_ref[...],
                          jnp.zeros((LANES - ROPE_LANE0 - MLA_ROPE, n_keys), F32)], axis=0).T
    lane = lax.broadcasted_iota(jnp.int32, kr.shape, 1)
    in_rope = (lane >= ROPE_LANE0) & (lane < ROPE_LANE0 + MLA_ROPE)
    for hd in range(MLA_HEADS):
        sl = slice(hd * HEAD_PAD, (hd + 1) * HEAD_PAD)
        k_ref[:, sl] = jnp.where(in_rope, kr, kn[:, sl]).astype(BF16)
    vt_ref[...] = lax.dot_general(wvt_ref[...], ckv_b, _NT, preferred_element_type=F32).astype(BF16)


def _decomp(ckv, kr_t, wk, wvt):
    b, s, _ = ckv.shape
    return pl.pallas_call(
        _decomp_kernel,
        out_shape=[jax.ShapeDtypeStruct((b, s, QPAD), BF16), jax.ShapeDtypeStruct((VALL, b * s), BF16)],
        grid=(b,),
        in_specs=[pl.BlockSpec((None, s, KV_LORA), lambda i: (i, 0, 0)),
                  pl.BlockSpec((None, MLA_ROPE, s), lambda i: (i, 0, 0)),
                  _const_spec(wk.shape), _const_spec(wvt.shape)],
        out_specs=[pl.BlockSpec((None, s, QPAD), lambda i: (i, 0, 0)),
                   pl.BlockSpec((VALL, s), lambda i: (0, i))],
        name="ctx_decompress",
        compiler_params=pltpu.CompilerParams(dimension_semantics=("arbitrary",)),
    )(ckv, kr_t, wk, wvt)


def _attn_kernel(*refs, has_ctx, n_seqs, n_side):
    n_in = 5 if has_ctx else 3
    side_in, refs = refs[n_in:n_in + n_side], refs[:n_in] + refs[n_in + n_side:]
    side_out, refs = refs[n_in + 1:n_in + 1 + n_side], refs[:n_in + 1] + refs[n_in + 1 + n_side:]
    if has_ctx:
        q_ref, kc_ref, vct_ref, k_ref, vt_ref, o_ref, st_ref, p_ref = refs
    else:
        q_ref, k_ref, vt_ref, o_ref, st_ref, p_ref = refs

    for src, dst in zip(side_in, side_out):
        dst[...] = src[...].astype(BF16)
    _attn_pipeline(q_ref, k_ref, vt_ref, o_ref, st_ref, p_ref,
                   (kc_ref, vct_ref) if has_ctx else None, n_seqs)


def _attn_pipeline(q_ref, k_ref, vt_ref, o_ref, st_ref, p_ref, ctx_refs, n_seqs):
    tq = Q_TILE

    def key_blocks(bi):
        srcs = [ctx_refs] if ctx_refs is not None else []
        blocks, row0 = [], 0
        for kr, vr in srcs + [(k_ref, vt_ref)]:
            n_keys = kr.shape[1]
            size = min(KEY_BLOCK, n_keys)
            for r in range(0, n_keys, size):
                blocks.append((kr, vr, r, bi * n_keys + r, size, row0))
                row0 += size
        return blocks

    units = [(bi, slice(q0, q0 + tq), hd) for bi in range(n_seqs)
             for q0 in range(0, q_ref.shape[1], tq) for hd in range(MLA_HEADS)]
    col_max = [None] * len(units)
    pair = []
    for stage in range(len(units) + 2):
        ua, ub, uc = stage, stage - 1, stage - 2
        run_max = None
        acc = jnp.zeros((MLA_V + ONES_ROWS, tq), F32)
        for j in range(len(key_blocks(0))):
            if ua < len(units):
                bi, qrows, hd = units[ua]
                kr, _, r0, _, size, srow = key_blocks(bi)[j]
                sl = slice(hd * HEAD_PAD, (hd + 1) * HEAD_PAD)
                st = lax.dot_general(kr[bi, r0:r0 + size, sl], q_ref[bi, qrows, sl], _NT,
                                     preferred_element_type=F32)
                st_ref[ua % 2, srow:srow + size, :] = st
                blk_max = jnp.max(st.reshape(size // 8, 8, tq), axis=0)
                run_max = blk_max if run_max is None else jnp.maximum(run_max, blk_max)
            if 0 <= ub < len(units):
                _, _, _, _, size, srow = key_blocks(0)[j]
                p_ref[ub % 2, srow:srow + size, :] = jnp.exp2(
                    st_ref[ub % 2, srow:srow + size, :] - col_max[ub]).astype(BF16)
            if uc >= 0:
                bi, _, hd = units[uc]
                _, vr, _, c0, size, srow = key_blocks(bi)[j]
                v_aug = jnp.concatenate([vr[hd * MLA_V:(hd + 1) * MLA_V, c0:c0 + size],
                                         jnp.ones((ONES_ROWS, size), BF16)], axis=0)
                acc = acc + jnp.dot(v_aug, p_ref[uc % 2, srow:srow + size, :],
                                    preferred_element_type=F32)
        if ua < len(units):
            col_max[ua] = jnp.max(run_max, axis=0, keepdims=True)
        if uc >= 0:
            bi, qrows, hd = units[uc]
            pair.append(acc[:MLA_V, :] / acc[MLA_V:MLA_V + 1, :])
            if len(pair) == 2:
                o_ref[bi, qrows, (hd - 1) * MLA_V:(hd + 1) * MLA_V] = (
                    jnp.concatenate(pair, axis=0).T.astype(BF16))
                pair = []


def _attention(q, k, vt, ctx_kv, tq, n_seqs, side_weights=()):
    b, t, _ = q.shape
    steps = (b // n_seqs) * (t // tq)
    assert (n_seqs == 1 or tq == t) and tq % Q_TILE == 0
    has_ctx = ctx_kv is not None
    in_specs = [pl.BlockSpec((n_seqs, tq, QPAD), lambda i, j: (i, j, 0))]
    args = [q]
    if has_ctx:
        kc, vct = ctx_kv
        s = kc.shape[1]
        in_specs += [pl.BlockSpec((n_seqs, s, QPAD), lambda i, j: (i, 0, 0)),
                     pl.BlockSpec((VALL, n_seqs * s), lambda i, j: (0, i))]
        args += [kc, vct]
    in_specs += [pl.BlockSpec((n_seqs, t, QPAD), lambda i, j: (i, 0, 0)),
                 pl.BlockSpec((VALL, n_seqs * t), lambda i, j: (0, i))]
    args += [k, vt]
    out_shape = [jax.ShapeDtypeStruct((b, t, VALL), BF16)]
    out_specs = [pl.BlockSpec((n_seqs, tq, VALL), lambda i, j: (i, j, 0))]
    nj = t // tq
    for w in side_weights:
        _, rows, cols = w.shape
        assert rows % (16 * steps) == 0
        in_specs.append(pl.BlockSpec((None, rows // steps, cols), lambda i, j: (0, i * nj + j, 0)))
        args.append(w)
        out_shape.append(jax.ShapeDtypeStruct((rows, cols), BF16))
        out_specs.append(pl.BlockSpec((rows // steps, cols), lambda i, j: (i * nj + j, 0)))
    return pl.pallas_call(
        functools.partial(_attn_kernel, has_ctx=has_ctx, n_seqs=n_seqs, n_side=len(side_weights)),
        out_shape=out_shape,
        grid=(b // n_seqs, t // tq),
        in_specs=in_specs,
        out_specs=out_specs,
        scratch_shapes=[pltpu.VMEM((2, t + (s if has_ctx else 0), Q_TILE), F32),
                        pltpu.VMEM((2, t + (s if has_ctx else 0), Q_TILE), BF16)],
        name="mla_attn_lat" if has_ctx else "mla_attn_ctx",
        compiler_params=pltpu.CompilerParams(dimension_semantics=("arbitrary", "arbitrary"),
                                             vmem_limit_bytes=VMEM_LIMIT),
    )(*args)


def _gla_kernel(*refs, n_tiles, n_seqs, zero_init):
    init_refs, rest = (None, refs[6:]) if zero_init else (refs[6:8], refs[8:])
    _gla_body(refs[:6], init_refs, rest[0], rest[1:4], rest[4:], n_tiles, n_seqs)


def _gla_scratch(n_seqs, t):
    return [pltpu.VMEM((2, n_seqs, t, GV), F32),
            pltpu.VMEM((GQK, GLA_TILE), BF16),
            pltpu.VMEM((2, GLA_TILE, GLA_TILE), F32),
            pltpu.VMEM((GLA_HEADS * GLA_TILE, GQK), BF16)]


def _gla_body(in_refs, init_refs, gn_ref, out_refs, scratch_refs, n_tiles, n_seqs):
    gq_ref, gk_ref, gv_ref, gf_ref, gb_ref, go_ref = in_refs
    o_ref, sf_ref, sb_ref = out_refs
    oacc_ref, bdqk_ref, tri_ref, hm_ref = scratch_refs
    g_refs = (gf_ref, gb_ref)
    state_refs = (sf_ref, sb_ref)

    @pl.when(pl.program_id(0) == 0)
    def _():
        ri = lax.broadcasted_iota(jnp.int32, (GLA_TILE, GLA_TILE), 0)
        ci = lax.broadcasted_iota(jnp.int32, (GLA_TILE, GLA_TILE), 1)
        same_chunk = (ri // CHUNK) == (ci // CHUNK)
        bdqk_ref[...] = jnp.where(same_chunk, 1.0, 0.0).astype(BF16)
        tri_ref[0] = jnp.where(same_chunk & (ri >= ci), 1.0, 0.0)
        tri_ref[1] = jnp.where(same_chunk & (ci >= ri), 1.0, 0.0)
        hm_ref[...] = jnp.where(
            lax.broadcasted_iota(jnp.int32, (GLA_HEADS * GLA_TILE, GQK), 0) // GLA_TILE
            == lax.broadcasted_iota(jnp.int32, (GLA_HEADS * GLA_TILE, GQK), 1) // GLA_DK,
            1.0, 0.0).astype(BF16)

    row8 = lax.broadcasted_iota(jnp.int32, (8, GQK), 0)

    def row_slice(start, size):
        return pl.ds(start if isinstance(start, int) else pl.multiple_of(start, size), size)

    def tile_rows(t):
        return row_slice(t * GLA_TILE, GLA_TILE)

    finished = set()

    def total_row(c, d):
        return c * CHUNK + (CHUNK - 1 if d == 0 else 0)

    def tile_dir(b, t, d):
        rows = tile_rows(t)
        g = g_refs[d][b, rows, :]
        g_hi = g.astype(BF16)
        g_lo = (g - g_hi.astype(F32)).astype(BF16)
        tri_b = tri_ref[d].astype(BF16)
        cum = (jnp.dot(tri_b, g_hi, preferred_element_type=F32)
               + jnp.dot(tri_b, g_lo, preferred_element_type=F32))
        yield
        totals = [cum[total_row(c, d):total_row(c, d) + 1, :] for c in range(CHUNKS_PER_TILE)]
        tot8 = jnp.zeros((8, GQK), F32)
        for c in range(CHUNKS_PER_TILE):
            tot8 = jnp.where(row8 == c, totals[c], tot8)
        dec_t = jnp.concatenate([jnp.exp(tot8), jnp.zeros((LANES - 8, GQK), F32)], axis=0).T
        q = gq_ref[b, rows, :] * (GLA_DK ** -0.5)
        k = gk_ref[b, rows, :]
        v = gv_ref[b, rows, :]
        tot = jnp.concatenate([jnp.broadcast_to(tc, (CHUNK, GQK)) for tc in totals], axis=0)
        qe = (q * jnp.exp(cum)).astype(BF16)
        ke = (k * jnp.exp(-cum)).astype(BF16)
        kd_t = (k * jnp.exp(tot - cum)).T.astype(BF16)
        bd_qk = bdqk_ref[...] > 0
        tri = tri_ref[d] > 0

        qm = jnp.where(hm_ref[...] > 0, jnp.tile(qe, (GLA_HEADS, 1)), 0.0)
        yield
        att = lax.dot_general(qm, ke, _NT, preferred_element_type=F32)

        yield
        intra, upd = [], []
        for hd in range(GLA_HEADS):
            vh = v[:, hd * GLA_DV:(hd + 1) * GLA_DV]
            a_h = jnp.where(tri, att[hd * GLA_TILE:(hd + 1) * GLA_TILE, :], 0.0).astype(BF16)
            intra.append(jnp.dot(a_h, vh, preferred_element_type=F32))
            kd_h = jnp.tile(kd_t[hd * GLA_DK:(hd + 1) * GLA_DK, :], (CHUNKS_PER_TILE, 1))
            upd.append(jnp.dot(jnp.where(bd_qk, kd_h, 0.0), vh, preferred_element_type=F32))

        yield
        state = [state_refs[d][b, hd] for hd in range(GLA_HEADS)]
        order = range(CHUNKS_PER_TILE) if d == 0 else range(CHUNKS_PER_TILE - 1, -1, -1)
        seen = {}
        for c in order:
            seen[c] = jnp.concatenate(state, axis=0).astype(BF16)
            decay = jnp.broadcast_to(dec_t[:, c:c + 1], (GQK, GLA_DV))
            for hd in range(GLA_HEADS):
                ks = slice(hd * GLA_DK, (hd + 1) * GLA_DK)
                state[hd] = decay[ks, :] * state[hd] + upd[hd][c * CHUNK:(c + 1) * CHUNK, :]
        for hd in range(GLA_HEADS):
            state_refs[d][b, hd] = state[hd]

        yield
        for c in range(CHUNKS_PER_TILE):
            cr = slice(c * CHUNK, (c + 1) * CHUNK)
            q_c = jnp.concatenate([qm[hd * GLA_TILE + c * CHUNK:hd * GLA_TILE + (c + 1) * CHUNK, :]
                                   for hd in range(GLA_HEADS)], axis=0)
            inter = jnp.dot(q_c, seen[c], preferred_element_type=F32)
            o = jnp.concatenate([intra[hd][cr, :] + inter[hd * CHUNK:(hd + 1) * CHUNK, :]
                                 for hd in range(GLA_HEADS)], axis=1)
            oacc_ref[d, b, row_slice(t * GLA_TILE + c * CHUNK, CHUNK), :] = o
        if isinstance(t, int):
            finished.add((b, t, d))

    if init_refs is None:
        sf_ref[...] = jnp.zeros(sf_ref.shape, F32)
        sb_ref[...] = jnp.zeros(sb_ref.shape, F32)
    else:
        sf_ref[...] = init_refs[0][...]
        sb_ref[...] = init_refs[1][...]

    gn = gn_ref[...]

    def epilogue(t):
        while isinstance(t, int) and not all((b, t, d) in finished for b in range(n_seqs) for d in range(2)):
            yield
        rows = tile_rows(t)
        for b in range(n_seqs):
            for hd in range(GLA_HEADS):
                vs = slice(hd * GLA_DV, (hd + 1) * GLA_DV)
                o = _rms(oacc_ref[0, b, rows, vs] + oacc_ref[1, b, rows, vs], gn)
                go = go_ref[b, rows, vs]
                o_ref[b, rows, vs] = (o * (go * _sigmoid(go))).astype(BF16)
        return
        yield

    def main_chains(first_tile, n):
        chains = []
        for t in [first_tile + u for u in range(n)]:
            for b in range(n_seqs):
                chains += [tile_dir(b, t, 0), tile_dir(b, n_tiles - 1 - t, 1)]
        return chains

    if n_tiles <= GLA_STATIC_TILES:
        _interleave(main_chains(0, n_tiles) + [epilogue(t) for t in range(n_tiles)])
    else:
        per_step = next(c for c in (4, 2, 1) if n_tiles % c == 0)

        def main_body(i, carry):
            _interleave(main_chains(i * per_step, per_step))
            return carry

        lax.fori_loop(0, n_tiles // per_step, main_body, 0)

        def epilogue_body(t, carry):
            _interleave([epilogue(t)])
            return carry

        lax.fori_loop(0, n_tiles, epilogue_body, 0)


def _gla(gq, gk, gv, gf, gb, go, init_states, gn, n_seqs):
    b, t, _ = gq.shape
    n_tiles = t // GLA_TILE
    zero_init = init_states is None
    seq = lambda c: pl.BlockSpec((n_seqs, t, c), lambda i: (i, 0, 0))
    st = pl.BlockSpec((n_seqs, GLA_HEADS, GLA_DK, GLA_DV), lambda i: (i, 0, 0, 0))
    in_specs = [seq(GQK), seq(GQK), seq(GV), seq(GQK), seq(GQK), seq(GV)]
    args = [gq, gk, gv, gf, gb, go]
    if not zero_init:
        in_specs += [st, st]
        args += list(init_states)
    in_specs.append(_const_spec(gn.shape))
    args.append(gn)
    return pl.pallas_call(
        functools.partial(_gla_kernel, n_tiles=n_tiles, n_seqs=n_seqs, zero_init=zero_init),
        out_shape=[jax.ShapeDtypeStruct((b, t, GV), BF16),
                   jax.ShapeDtypeStruct((b, GLA_HEADS, GLA_DK, GLA_DV), F32),
                   jax.ShapeDtypeStruct((b, GLA_HEADS, GLA_DK, GLA_DV), F32)],
        grid=(b // n_seqs,),
        in_specs=in_specs,
        out_specs=[seq(GV), st, st],
        scratch_shapes=_gla_scratch(n_seqs, t),
        name="gla_%d" % t,
        compiler_params=pltpu.CompilerParams(dimension_semantics=("arbitrary",),
                                             vmem_limit_bytes=VMEM_LIMIT),
    )(*args)


def _ffn_kernel(xp_ref, xs_ref, atp_ref, ats_ref, glp_ref, gls_ref, mod_ref, wout_ref, nf_ref,
                wfi_ref, wfo_ref, fn_ref, yp_ref, ys_ref, act_ref, *, ctx_tiles, tiles_per_seq):
    def sub_tile(x_ref, at_ref, gl_ref, y_ref, mod_row, r0):
        rows = slice(r0, r0 + FFN_SUB)
        gt1, sh2, sc2, gt2 = _mod_rows(mod_ref, mod_row)[2:]
        mix = (jnp.dot(at_ref[rows, :], wout_ref[0:VALL, :], preferred_element_type=F32)
               + jnp.dot(gl_ref[rows, :], wout_ref[VALL:, :], preferred_element_type=F32))
        yield
        x1 = x_ref[rows, :] + gt1 * mix
        h2 = (_rms(x1, nf_ref[...]) * (1.0 + sc2) + sh2).astype(BF16)
        yield
        for j in range(N_FF_CHUNKS):
            cs = slice(j * FF_CHUNK, (j + 1) * FF_CHUNK)
            a = jnp.dot(h2, wfi_ref[:, cs], preferred_element_type=F32)
            g = jnp.dot(h2, wfi_ref[:, D_FF + j * FF_CHUNK:D_FF + (j + 1) * FF_CHUNK],
                        preferred_element_type=F32)
            act_ref[rows, cs] = (a * _sigmoid(a) * g).astype(BF16)
            if j in FFN_PHASE_ENDS:
                yield
        ff = jnp.dot(act_ref[rows, :], wfo_ref[...], preferred_element_type=F32)
        yield
        x2 = x1 + gt2 * ff
        y_ref[rows, :] = _rms(x2, fn_ref[...])

    def tile(x_ref, at_ref, gl_ref, y_ref, mod_row):
        _interleave([sub_tile(x_ref, at_ref, gl_ref, y_ref, mod_row, r0)
                     for r0 in range(0, x_ref.shape[0], FFN_SUB)])

    t = pl.program_id(0)

    @pl.when(t < ctx_tiles)
    def _():
        tile(xp_ref, atp_ref, glp_ref, yp_ref, 0)

    @pl.when(t >= ctx_tiles)
    def _():
        tile(xs_ref, ats_ref, gls_ref, ys_ref, 1 + (t - ctx_tiles) // tiles_per_seq)


def _ffn(xp, xs, attn_p, attn_s, gla_p, gla_s, mod, wout, nf, wfi, wfo, fn, tm, tiles_per_seq):
    d = xp.shape[1]
    ctx_tiles = xp.shape[0] // tm
    lat_tiles = xs.shape[0] // tm
    ctx_map = lambda s: (jnp.minimum(s, ctx_tiles - 1), 0)
    lat_map = lambda s: (jnp.maximum(s - ctx_tiles, 0), 0)
    tile = lambda c, m: pl.BlockSpec((tm, c), m)
    return pl.pallas_call(
        functools.partial(_ffn_kernel, ctx_tiles=ctx_tiles, tiles_per_seq=tiles_per_seq),
        out_shape=[jax.ShapeDtypeStruct(xp.shape, F32), jax.ShapeDtypeStruct(xs.shape, F32)],
        grid=(ctx_tiles + lat_tiles,),
        in_specs=[tile(d, ctx_map), tile(d, lat_map), tile(VALL, ctx_map), tile(VALL, lat_map),
                  tile(GV, ctx_map), tile(GV, lat_map), _const_spec(mod.shape),
                  _const_spec(wout.shape), _const_spec(nf.shape), _const_spec(wfi.shape),
                  _const_spec(wfo.shape), _const_spec(fn.shape)],
        out_specs=[tile(d, ctx_map), tile(d, lat_map)],
        scratch_shapes=[pltpu.VMEM((tm, D_FF), BF16)],
        name="out_ffn",
        compiler_params=pltpu.CompilerParams(dimension_semantics=("arbitrary",),
                                             vmem_limit_bytes=VMEM_LIMIT),
    )(xp, xs, attn_p, attn_s, gla_p, gla_s, mod, wout, nf, wfi, wfo, fn)


def _rope_tables(n_tokens):
    t = np.arange(n_tokens)
    row = (t // GRID_W).astype(np.float32)
    col = (t % GRID_W).astype(np.float32)
    half = MLA_ROPE // 2
    inv = (np.float32(ROPE_BASE) ** (-np.arange(0, half, 2, dtype=np.float32) / np.float32(half))).astype(np.float32)
    ang_r = row[:, None] * inv
    ang_c = col[:, None] * inv
    ang = np.concatenate([ang_r, ang_r, ang_c, ang_c], axis=-1).astype(np.float32)
    cos, sin = np.cos(ang), np.sin(ang)
    first = (np.arange(MLA_ROPE) % half) < (half // 2)
    cos_t = np.ones((n_tokens, LANES), np.float32)
    sa_t = np.zeros((n_tokens, LANES), np.float32)
    sb_t = np.zeros((n_tokens, LANES), np.float32)
    cos_t[:, ROPE_LANE0:ROPE_LANE0 + MLA_ROPE] = cos
    sa_t[:, ROPE_LANE0:ROPE_LANE0 + MLA_ROPE] = np.where(first, -sin, 0.0)
    sb_t[:, ROPE_LANE0:ROPE_LANE0 + MLA_ROPE] = np.where(first, 0.0, sin)
    return jnp.asarray(cos_t), jnp.asarray(sa_t), jnp.asarray(sb_t)


def kernel(x_prompt, x_sample, cache_kv_latent, cache_k_rope, state_gla_fwd, state_gla_bwd, c, c_ctx, w_ada, b_ada, norm_attn, w_in, mla_q_norm, w_uq, mla_kv_norm, w_ukv, w_gate_f, b_gate_f, w_gate_b, b_gate_b, gla_norm, w_out, norm_ffn, w_ffn_in, w_ffn_out, final_norm):
    batch, seq, d = x_prompt.shape
    dec_batch, dec_seq, _ = x_sample.shape
    assert w_ada.shape[0] == 1 and w_in.shape[-1] == W_COLS and w_ffn_in.shape[-1] == 2 * D_FF
    l = 0

    mod = _ada(c_ctx, c, w_ada[l], b_ada[l])

    win, wuq, wk, wvt, wg = _prep_in_weights(w_in, w_uq, w_ukv, w_gate_f, w_gate_b)
    in_w = (norm_attn[l].reshape(1, d), win, mla_q_norm[l].reshape(1, Q_LORA), wuq,
            mla_kv_norm[l].reshape(1, KV_LORA), wk, wvt, wg, b_gate_f, b_gate_b)
    gn = gla_norm[l].reshape(1, GLA_DV)
    tm, tm_ffn = 512, 512
    r3 = lambda a, b_, t: a.reshape(b_, t, a.shape[-1])

    xp = x_prompt.reshape(batch * seq, d)
    (attn_p, gla_p, sf, sb, ckv, kr_t) = _inproj(xp, mod, lambda i: 0, in_w, None, tm, seq, gn)

    xs = x_sample.reshape(dec_batch * dec_seq, d)
    tiles = dec_seq // tm
    (q, k, vt, gq, gk, gv, gf, gb, go) = _inproj(xs, mod, lambda i: 1 + i // tiles, in_w,
                                                  _rope_tables(dec_seq), tm, dec_seq)
    kc, vct = _decomp(cache_kv_latent[:, l], jnp.swapaxes(cache_k_rope[:, l], 1, 2), wk, wvt)
    attn_s, wout, wfi, wfo = _attention(r3(q, dec_batch, dec_seq), r3(k, dec_batch, dec_seq), vt, (kc, vct),
                                        ATTN_LAT_QUERIES, 1, (w_out, w_ffn_in, w_ffn_out))
    gla_s, _, _ = _gla(r3(gq, dec_batch, dec_seq), r3(gk, dec_batch, dec_seq), r3(gv, dec_batch, dec_seq),
                       r3(gf, dec_batch, dec_seq), r3(gb, dec_batch, dec_seq), r3(go, dec_batch, dec_seq),
                       (state_gla_fwd[:, l].astype(F32), state_gla_bwd[:, l].astype(F32)), gn, 1)

    flat = lambda a: a.reshape(-1, a.shape[-1])
    y_prompt, y_sample = _ffn(xp, xs, flat(attn_p), flat(attn_s), flat(gla_p), flat(gla_s), mod,
                              wout, norm_ffn[l].reshape(1, d), wfi, wfo,
                              final_norm.reshape(1, d), tm_ffn, dec_seq // tm_ffn)
    y_prompt = y_prompt.reshape(batch, seq, d)
    y_sample = y_sample.reshape(dec_batch, dec_seq, d)

    new_kv_latent = ckv.reshape(batch, 1, seq, KV_LORA)
    new_k_rope = jnp.swapaxes(kr_t, 1, 2).reshape(batch, 1, seq, MLA_ROPE)
    new_state_fwd = sf.reshape(batch, 1, GLA_HEADS, GLA_DK, GLA_DV).astype(x_prompt.dtype)
    new_state_bwd = sb.reshape(batch, 1, GLA_HEADS, GLA_DK, GLA_DV).astype(x_prompt.dtype)
    return (y_prompt, y_sample, new_kv_latent, new_k_rope, new_state_fwd, new_state_bwd)
```

```python
import functools

import numpy as np
import jax
import jax.numpy as jnp
from jax import lax
from jax.experimental import pallas as pl
from jax.experimental.pallas import tpu as pltpu

F32 = jnp.float32
BF16 = jnp.bfloat16

GRID_W = 64
MLA_HEADS = 8
MLA_NOPE = 64
MLA_ROPE = 32
MLA_QK = MLA_NOPE + MLA_ROPE
MLA_V = 64
Q_LORA = 384
KV_LORA = 256
GLA_HEADS = 4
GLA_DK = 64
GLA_DV = 128
GATE_RANK = 16
GATE_NORM = 16.0
CHUNK = 64
D_FF = 2816
ROPE_BASE = 10000.0
EPS = 1e-6
LOG2_E = 1.4426950408889634

LANES = 128
HEAD_PAD = LANES
ROPE_LANE0 = MLA_NOPE
GQK = GLA_HEADS * GLA_DK
GV = GLA_HEADS * GLA_DV
QPAD = MLA_HEADS * HEAD_PAD
VALL = MLA_HEADS * MLA_V
ONES_ROWS = 16
KEY_BLOCK = 1024

W_KR = Q_LORA + KV_LORA
W_GQ = W_KR + MLA_ROPE
W_GF = W_GQ + 2 * GQK + GV
W_GO = W_GF + 2 * GATE_RANK
W_COLS = W_GO + GV

Z_Q = 0
Z_KV = Z_Q + Q_LORA
Z_GQ = Z_KV + KV_LORA
Z_GK = Z_GQ + GQK
Z_GV = Z_GK + GQK
Z_GO = Z_GV + GV
Z_MISC = Z_GO + GV
Z_COLS = Z_MISC + LANES

FF_CHUNK = 256
N_FF_CHUNKS = D_FF // FF_CHUNK
FFN_SUB = 256
FFN_PHASE_ENDS = (3, 7, 10)

GLA_TILE = 256
CHUNKS_PER_TILE = GLA_TILE // CHUNK
ADA_ROWS = 128
INPROJ_SUB = 512
CTX_STEP_SEQS = 4
Q_TILE = 256
ATTN_LAT_QUERIES = 512
GLA_STATIC_TILES = 8

VMEM_LIMIT = 56 * 1024 * 1024

_NT = (((1,), (1,)), ((), ()))


def _rms(x, w):
    return x * lax.rsqrt(jnp.mean(x * x, axis=-1, keepdims=True) + EPS) * w


def _sigmoid(x):
    return 1.0 / (1.0 + jnp.exp(-x))


def _log_sigmoid(x):
    return jnp.minimum(x, 0.0) - jnp.log1p(jnp.exp(-jnp.abs(x)))


def _interleave(chains):
    pending, active = list(chains), []
    while pending or active:
        if pending:
            active.append(pending.pop(0))
        for chain in list(active):
            try:
                next(chain)
            except StopIteration:
                active.remove(chain)


def _const_spec(shape):
    nd = len(shape)
    return pl.BlockSpec(shape, lambda *_: (0,) * nd, pipeline_mode=pl.Buffered(1))


def _mod_rows(mod_ref, r):
    return [mod_ref[k, pl.ds(r, 1), :] for k in range(mod_ref.shape[0])]


def _ada_kernel(cctx_ref, c_ref, w_ref, b_ref, o_ref):
    k = pl.program_id(0)
    d = o_ref.shape[2]
    row = lax.broadcasted_iota(jnp.int32, (8, cctx_ref.shape[1]), 0)
    cond = jnp.where(row == 0, cctx_ref[...], 0.0)
    for r in range(c_ref.shape[0]):
        cond = jnp.where(row == 1 + r, c_ref[r:r + 1, :], cond)
    s = (cond * _sigmoid(cond)).astype(BF16)
    part = jnp.dot(s, w_ref[...].astype(BF16), preferred_element_type=F32)
    for j in range(o_ref.shape[0]):
        sl = slice(j * d, (j + 1) * d)

        @pl.when(k == 0)
        def _():
            o_ref[j] = part[:, sl] + b_ref[:, sl]

        @pl.when(k > 0)
        def _():
            o_ref[j] += part[:, sl]


def _ada(c_ctx, c, w_ada, b_ada):
    d = w_ada.shape[0]
    n = w_ada.shape[1]
    assert 1 + c.shape[0] <= 8
    return pl.pallas_call(
        _ada_kernel,
        out_shape=jax.ShapeDtypeStruct((n // d, 8, d), F32),
        grid=(d // ADA_ROWS,),
        in_specs=[pl.BlockSpec((1, ADA_ROWS), lambda k: (0, k)),
                  pl.BlockSpec((c.shape[0], ADA_ROWS), lambda k: (0, k)),
                  pl.BlockSpec((ADA_ROWS, n), lambda k: (k, 0)),
                  pl.BlockSpec((1, n), lambda k: (0, 0))],
        out_specs=pl.BlockSpec((n // d, 8, d), lambda k: (0, 0, 0)),
        name="ada_mod",
        compiler_params=pltpu.CompilerParams(dimension_semantics=("arbitrary",)),
    )(c_ctx.reshape(1, d), c, w_ada, b_ada.reshape(1, n))


def _prep_kernel(wint_ref, wuq_ref, wukv_ref, wgf_ref, wgb_ref, win_o, wuq_o, wk_o, wvt_o, wg_o):
    cols = wint_ref.shape[1]
    for dst, src, n in ((Z_Q, 0, W_KR), (Z_GQ, W_GQ, W_GF - W_GQ), (Z_GO, W_GO, GV)):
        win_o[:, dst:dst + n] = wint_ref[src:src + n, :].T.astype(BF16)
    z32 = jnp.zeros((32, cols), F32)
    misc_t = jnp.concatenate([wint_ref[W_GF:W_GO, :], z32, wint_ref[W_KR:W_GQ, :], z32], axis=0)
    win_o[:, Z_MISC:Z_COLS] = misc_t.T.astype(BF16)

    u = wuq_ref[...]
    zq = jnp.zeros((u.shape[0], HEAD_PAD - MLA_QK), F32)
    for hd in range(MLA_HEADS):
        blk = jnp.concatenate([u[:, hd * MLA_QK:(hd + 1) * MLA_QK], zq], axis=1)
        wuq_o[:, hd * HEAD_PAD:(hd + 1) * HEAD_PAD] = blk.astype(BF16)

    @pl.when(pl.program_id(0) == 0)
    def _():
        kv = wukv_ref[...]
        per = MLA_NOPE + MLA_V
        lane = lax.broadcasted_iota(jnp.int32, (kv.shape[0], per), 1)
        for hd in range(MLA_HEADS):
            blk = kv[:, hd * per:(hd + 1) * per]
            wk_o[:, hd * HEAD_PAD:(hd + 1) * HEAD_PAD] = jnp.where(lane < MLA_NOPE, blk, 0.0).astype(BF16)
        wv = jnp.concatenate([kv[:, hd * per + MLA_NOPE:(hd + 1) * per] for hd in range(MLA_HEADS)], axis=1)
        wvt_o[...] = wv.T.astype(BF16)

        wg_o[...] = jnp.zeros(wg_o.shape, BF16)
        wg_o[0:GATE_RANK, 0:GQK] = wgf_ref[...].astype(BF16)
        wg_o[GATE_RANK:2 * GATE_RANK, GQK:2 * GQK] = wgb_ref[...].astype(BF16)


def _prep_in_weights(w_in, w_uq, w_ukv, w_gate_f, w_gate_b):
    d = w_in.shape[1]
    steps = 4
    w_in_t = jnp.swapaxes(w_in, 1, 2)
    rb3 = lambda r, c: pl.BlockSpec((None, r // steps, c), lambda i: (0, i, 0))
    rb = lambda r, c: pl.BlockSpec((r // steps, c), lambda i: (i, 0))
    full3 = lambda shape: pl.BlockSpec((None,) + tuple(shape[1:]), lambda i: (0, 0, 0))
    full = lambda shape: pl.BlockSpec(shape, lambda i: (0, 0))
    return pl.pallas_call(
        _prep_kernel,
        out_shape=[jax.ShapeDtypeStruct((d, Z_COLS), BF16),
                   jax.ShapeDtypeStruct((Q_LORA, QPAD), BF16),
                   jax.ShapeDtypeStruct((KV_LORA, QPAD), BF16),
                   jax.ShapeDtypeStruct((VALL, KV_LORA), BF16),
                   jax.ShapeDtypeStruct((LANES, 2 * GQK), BF16)],
        grid=(steps,),
        in_specs=[pl.BlockSpec((None, W_COLS, d // steps), lambda i: (0, 0, i)),
                  rb3(Q_LORA, MLA_HEADS * MLA_QK), full3(w_ukv.shape),
                  full3(w_gate_f.shape), full3(w_gate_b.shape)],
        out_specs=[rb(d, Z_COLS), rb(Q_LORA, QPAD), full((KV_LORA, QPAD)), full((VALL, KV_LORA)),
                   full((LANES, 2 * GQK))],
        name="weight_prep",
        compiler_params=pltpu.CompilerParams(dimension_semantics=("arbitrary",)),
    )(w_in_t, w_uq, w_ukv, w_gate_f, w_gate_b)


def _inproj_kernel(*refs, latent, mod_row):
    (x_ref, mod_ref, nw_ref, win_ref, qn_ref, wuq_ref, kvn_ref, wk_ref, wvt_ref, wg_ref, bgf_ref, bgb_ref) = refs[:12]
    if latent:
        cos_ref, sa_ref, sb_ref = refs[12:15]
        q_ref, k_ref, vt_ref, gq_ref, gk_ref, gv_ref, gf_ref, gb_ref, go_ref = refs[15:]
    else:
        (gn_ref, attn_ref, gla_ref, sf_ref, sb_ref, ckv_ref, krt_ref, q_ref, k_ref, vt_ref, st_ref, p_ref,
         gq_ref, gk_ref, gv_ref, gf_ref, gb_ref, go_ref) = refs[12:30]
        gla_scratch = refs[30:]
    seq = q_ref.shape[1]

    sh1, sc1 = _mod_rows(mod_ref, mod_row(pl.program_id(0)))[:2]
    scale = MLA_QK ** -0.5 * LOG2_E
    lane = lax.broadcasted_iota(jnp.int32, (INPROJ_SUB, LANES), 1)
    in_rope = (lane >= ROPE_LANE0) & (lane < ROPE_LANE0 + MLA_ROPE)

    def sub_tile(r0):
        rows = slice(r0, r0 + INPROJ_SUB)
        h = (_rms(x_ref[rows, :], nw_ref[...]) * (1.0 + sc1) + sh1).astype(BF16)
        yield
        z_all = jnp.dot(h, win_ref[...], preferred_element_type=F32)
        z = lambda lo, n: z_all[:, lo:lo + n]
        yield
        qn = _rms(z(Z_Q, Q_LORA), qn_ref[...]).astype(BF16)
        ckv = _rms(z(Z_KV, KV_LORA), kvn_ref[...])
        ckv_b = ckv.astype(BF16)
        misc = z(Z_MISC, LANES)
        yield
        q = jnp.dot(qn, wuq_ref[...], preferred_element_type=F32)
        kn = jnp.dot(ckv_b, wk_ref[...], preferred_element_type=F32)
        vt_ref[:, rows] = lax.dot_general(wvt_ref[...], ckv_b, _NT,
                                          preferred_element_type=F32).astype(BF16)
        gpre = jnp.dot(misc.astype(BF16), wg_ref[...], preferred_element_type=F32)
        yield
        if latent:
            cos, sa, sb = cos_ref[rows, :], sa_ref[rows, :], sb_ref[rows, :]

            def rope(t):
                return t * cos + pltpu.roll(t, LANES - 8, 1) * sa + pltpu.roll(t, 8, 1) * sb
        else:
            def rope(t):
                return t

        def put(ref, val, cols=slice(None)):
            if len(ref.shape) == 2:
                ref[rows, cols] = val
            else:
                for b in range(INPROJ_SUB // seq):
                    ref[r0 // seq + b, :, cols] = val[b * seq:(b + 1) * seq, :]

        krope = rope(misc)
        for hd in range(MLA_HEADS):
            sl = slice(hd * HEAD_PAD, (hd + 1) * HEAD_PAD)
            put(q_ref, (rope(q[:, sl]) * scale).astype(BF16), sl)
            put(k_ref, jnp.where(in_rope, krope, kn[:, sl]).astype(BF16), sl)
        put(gq_ref, z(Z_GQ, GQK))
        put(gk_ref, z(Z_GK, GQK))
        put(gv_ref, z(Z_GV, GV).astype(BF16))
        put(go_ref, z(Z_GO, GV))
        put(gf_ref, _log_sigmoid(gpre[:, :GQK] + bgf_ref[...]) * (1.0 / GATE_NORM))
        put(gb_ref, _log_sigmoid(gpre[:, GQK:] + bgb_ref[...]) * (1.0 / GATE_NORM))
        if not latent:
            ckv_ref[rows, :] = ckv
            misc_t = misc.T
            n = krt_ref.shape[2]
            for b in range(INPROJ_SUB // n):
                krt_ref[r0 // n + b] = misc_t[ROPE_LANE0:ROPE_LANE0 + MLA_ROPE, b * n:(b + 1) * n]

    _interleave([sub_tile(r0) for r0 in range(0, x_ref.shape[0], INPROJ_SUB)])
    if not latent:
        n_seqs = q_ref.shape[0]
        _attn_pipeline(q_ref, k_ref, vt_ref, attn_ref, st_ref, p_ref, None, n_seqs)
        _gla_body((gq_ref, gk_ref, gv_ref, gf_ref, gb_ref, go_ref), None, gn_ref,
                  (gla_ref, sf_ref, sb_ref), gla_scratch, seq // GLA_TILE, n_seqs)


def _inproj(x2d, mod, mod_row, weights, rope_tabs, tm, seq_len, gla_norm=None):
    n_tok, d = x2d.shape
    latent = rope_tabs is not None
    tiles_per_seq = max(seq_len // tm, 1)
    nw, win, qn, wuq, kvn, wk, wvt, wg, bgf, bgb = weights
    row = lambda i: (i, 0)
    in_specs = [pl.BlockSpec((tm, d), row), _const_spec(mod.shape),
                _const_spec(nw.shape), _const_spec(win.shape), _const_spec(qn.shape),
                _const_spec(wuq.shape), _const_spec(kvn.shape), _const_spec(wk.shape),
                _const_spec(wvt.shape), _const_spec(wg.shape), _const_spec(bgf.shape),
                _const_spec(bgb.shape)]
    args = [x2d, mod, nw, win, qn, wuq, kvn, wk, wvt, wg, bgf, bgb]
    if latent:
        tab = pl.BlockSpec((tm, LANES), lambda i: (i % tiles_per_seq, 0))
        in_specs += [tab, tab, tab]
        args += list(rope_tabs)
    per_token = lambda c, dt: (jax.ShapeDtypeStruct((n_tok, c), dt), pl.BlockSpec((tm, c), row))
    seq = tm if latent else seq_len
    per_seq = lambda r, c, dt: (jax.ShapeDtypeStruct((n_tok // seq, r, c), dt),
                                pl.BlockSpec((tm // seq, r, c), lambda i: (i, 0, 0)))
    gla_outs = [per_token(GQK, F32), per_token(GQK, F32), per_token(GV, BF16),
                per_token(GQK, F32), per_token(GQK, F32), per_token(GV, F32)]
    scratch = []
    if latent:
        outs = [per_seq(seq, QPAD, BF16), per_seq(seq, QPAD, BF16),
                (jax.ShapeDtypeStruct((VALL, n_tok), BF16), pl.BlockSpec((VALL, tm), lambda i: (0, i)))]
        outs += gla_outs
    else:
        in_specs.append(_const_spec(gla_norm.shape))
        args.append(gla_norm)
        state = (jax.ShapeDtypeStruct((n_tok // seq, GLA_HEADS, GLA_DK, GLA_DV), F32),
                 pl.BlockSpec((tm // seq, GLA_HEADS, GLA_DK, GLA_DV), lambda i: (i, 0, 0, 0)))
        outs = [per_seq(seq, VALL, BF16), per_seq(seq, GV, BF16), state, state,
                per_token(KV_LORA, F32), per_seq(MLA_ROPE, seq, F32)]
        vmem = lambda c, dt: pltpu.VMEM((tm // seq, seq, c), dt)
        scratch = [vmem(QPAD, BF16), vmem(QPAD, BF16), pltpu.VMEM((VALL, tm), BF16),
                   pltpu.VMEM((2, seq, Q_TILE), F32), pltpu.VMEM((2, seq, Q_TILE), BF16),
                   vmem(GQK, F32), vmem(GQK, F32), vmem(GV, BF16), vmem(GQK, F32), vmem(GQK, F32),
                   vmem(GV, F32)] + _gla_scratch(tm // seq, seq)
    return pl.pallas_call(
        functools.partial(_inproj_kernel, latent=latent, mod_row=mod_row),
        out_shape=[o[0] for o in outs],
        grid=(n_tok // tm,),
        in_specs=in_specs,
        out_specs=[o[1] for o in outs],
        scratch_shapes=scratch,
        name="inproj_lat" if latent else "inproj_ctx",
        compiler_params=pltpu.CompilerParams(dimension_semantics=("arbitrary",),
                                             vmem_limit_bytes=VMEM_LIMIT),
    )(*args)


def _decomp_kernel(ckv_ref, krt_ref, wk_ref, wvt_ref, k_ref, vt_ref):
    ckv_b = ckv_ref[...].astype(BF16)
    kn = jnp.dot(ckv_b, wk_ref[...], preferred_element_type=F32)
    n_keys = krt_ref.shape[1]
    kr = jnp.concatenate([jnp.zeros((ROPE_LANE0, n_keys), F32), krt_ref[...],
                          jnp.zeros((LANES - ROPE_LANE0 - MLA_ROPE, n_keys), F32)], axis=0).T
    lane = lax.broadcasted_iota(jnp.int32, kr.shape, 1)
    in_rope = (lane >= ROPE_LANE0) & (lane < ROPE_LANE0 + MLA_ROPE)
    for hd in range(MLA_HEADS):
        sl = slice(hd * HEAD_PAD, (hd + 1) * HEAD_PAD)
        k_ref[:, sl] = jnp.where(in_rope, kr, kn[:, sl]).astype(BF16)
    vt_ref[...] = lax.dot_general(wvt_ref[...], ckv_b, _NT, preferred_element_type=F32).astype(BF16)


def _decomp(ckv, kr_t, wk, wvt):
    b, s, _ = ckv.shape
    return pl.pallas_call(
        _decomp_kernel,
        out_shape=[jax.ShapeDtypeStruct((b, s, QPAD), BF16), jax.ShapeDtypeStruct((VALL, b * s), BF16)],
        grid=(b,),
        in_specs=[pl.BlockSpec((None, s, KV_LORA), lambda i: (i, 0, 0)),
                  pl.BlockSpec((None, MLA_ROPE, s), lambda i: (i, 0, 0)),
                  _const_spec(wk.shape), _const_spec(wvt.shape)],
        out_specs=[pl.BlockSpec((None, s, QPAD), lambda i: (i, 0, 0)),
                   pl.BlockSpec((VALL, s), lambda i: (0, i))],
        name="ctx_decompress",
        compiler_params=pltpu.CompilerParams(dimension_semantics=("arbitrary",)),
    )(ckv, kr_t, wk, wvt)


def _attn_kernel(*refs, has_ctx, n_seqs, n_side):
    n_in = 5 if has_ctx else 3
    side_in, refs = refs[n_in:n_in + n_side], refs[:n_in] + refs[n_in + n_side:]
    side_out, refs = refs[n_in + 1:n_in + 1 + n_side], refs[:n_in + 1] + refs[n_in + 1 + n_side:]
    if has_ctx:
        q_ref, kc_ref, vct_ref, k_ref, vt_ref, o_ref, st_ref, p_ref = refs
    else:
        q_ref, k_ref, vt_ref, o_ref, st_ref, p_ref = refs

    for src, dst in zip(side_in, side_out):
        dst[...] = src[...].astype(BF16)
    _attn_pipeline(q_ref, k_ref, vt_ref, o_ref, st_ref, p_ref,
                   (kc_ref, vct_ref) if has_ctx else None, n_seqs)


def _attn_pipeline(q_ref, k_ref, vt_ref, o_ref, st_ref, p_ref, ctx_refs, n_seqs):
    tq = Q_TILE

    def key_blocks(bi):
        srcs = [ctx_refs] if ctx_refs is not None else []
        blocks, row0 = [], 0
        for kr, vr in srcs + [(k_ref, vt_ref)]:
            n_keys = kr.shape[1]
            size = min(KEY_BLOCK, n_keys)
            for r in range(0, n_keys, size):
                blocks.append((kr, vr, r, bi * n_keys + r, size, row0))
                row0 += size
        return blocks

    units = [(bi, slice(q0, q0 + tq), hd) for bi in range(n_seqs)
             for q0 in range(0, q_ref.shape[1], tq) for hd in range(MLA_HEADS)]
    col_max = [None] * len(units)
    pair = []
    for stage in range(len(units) + 2):
        ua, ub, uc = stage, stage - 1, stage - 2
        run_max = None
        acc = jnp.zeros((MLA_V + ONES_ROWS, tq), F32)
        for j in range(len(key_blocks(0))):
            if ua < len(units):
                bi, qrows, hd = units[ua]
                kr, _, r0, _, size, srow = key_blocks(bi)[j]
                sl = slice(hd * HEAD_PAD, (hd + 1) * HEAD_PAD)
                st = lax.dot_general(kr[bi, r0:r0 + size, sl], q_ref[bi, qrows, sl], _NT,
                                     preferred_element_type=F32)
                st_ref[ua % 2, srow:srow + size, :] = st
                blk_max = jnp.max(st.reshape(size // 8, 8, tq), axis=0)
                run_max = blk_max if run_max is None else jnp.maximum(run_max, blk_max)
            if 0 <= ub < len(units):
                _, _, _, _, size, srow = key_blocks(0)[j]
                p_ref[ub % 2, srow:srow + size, :] = jnp.exp2(
                    st_ref[ub % 2, srow:srow + size, :] - col_max[ub]).astype(BF16)
            if uc >= 0:
                bi, _, hd = units[uc]
                _, vr, _, c0, size, srow = key_blocks(bi)[j]
                v_aug = jnp.concatenate([vr[hd * MLA_V:(hd + 1) * MLA_V, c0:c0 + size],
                                         jnp.ones((ONES_ROWS, size), BF16)], axis=0)
                acc = acc + jnp.dot(v_aug, p_ref[uc % 2, srow:srow + size, :],
                                    preferred_element_type=F32)
        if ua < len(units):
            col_max[ua] = jnp.max(run_max, axis=0, keepdims=True)
        if uc >= 0:
            bi, qrows, hd = units[uc]
            pair.append(acc[:MLA_V, :] / acc[MLA_V:MLA_V + 1, :])
            if len(pair) == 2:
                o_ref[bi, qrows, (hd - 1) * MLA_V:(hd + 1) * MLA_V] = (
                    jnp.concatenate(pair, axis=0).T.astype(BF16))
                pair = []


def _attention(q, k, vt, ctx_kv, tq, n_seqs, side_weights=()):
    b, t, _ = q.shape
    steps = (b // n_seqs) * (t // tq)
    assert (n_seqs == 1 or tq == t) and tq % Q_TILE == 0
    has_ctx = ctx_kv is not None
    in_specs = [pl.BlockSpec((n_seqs, tq, QPAD), lambda i, j: (i, j, 0))]
    args = [q]
    if has_ctx:
        kc, vct = ctx_kv
        s = kc.shape[1]
        in_specs += [pl.BlockSpec((n_seqs, s, QPAD), lambda i, j: (i, 0, 0)),
                     pl.BlockSpec((VALL, n_seqs * s), lambda i, j: (0, i))]
        args += [kc, vct]
    in_specs += [pl.BlockSpec((n_seqs, t, QPAD), lambda i, j: (i, 0, 0)),
                 pl.BlockSpec((VALL, n_seqs * t), lambda i, j: (0, i))]
    args += [k, vt]
    out_shape = [jax.ShapeDtypeStruct((b, t, VALL), BF16)]
    out_specs = [pl.BlockSpec((n_seqs, tq, VALL), lambda i, j: (i, j, 0))]
    nj = t // tq
    for w in side_weights:
        _, rows, cols = w.shape
        assert rows % (16 * steps) == 0
        in_specs.append(pl.BlockSpec((None, rows // steps, cols), lambda i, j: (0, i * nj + j, 0)))
        args.append(w)
        out_shape.append(jax.ShapeDtypeStruct((rows, cols), BF16))
        out_specs.append(pl.BlockSpec((rows // steps, cols), lambda i, j: (i * nj + j, 0)))
    return pl.pallas_call(
        functools.partial(_attn_kernel, has_ctx=has_ctx, n_seqs=n_seqs, n_side=len(side_weights)),
        out_shape=out_shape,
        grid=(b // n_seqs, t // tq),
        in_specs=in_specs,
        out_specs=out_specs,
        scratch_shapes=[pltpu.VMEM((2, t + (s if has_ctx else 0), Q_TILE), F32),
                        pltpu.VMEM((2, t + (s if has_ctx else 0), Q_TILE), BF16)],
        name="mla_attn_lat" if has_ctx else "mla_attn_ctx",
        compiler_params=pltpu.CompilerParams(dimension_semantics=("arbitrary", "arbitrary"),
                                             vmem_limit_bytes=VMEM_LIMIT),
    )(*args)


def _gla_kernel(*refs, n_tiles, n_seqs, zero_init):
    init_refs, rest = (None, refs[6:]) if zero_init else (refs[6:8], refs[8:])
    _gla_body(refs[:6], init_refs, rest[0], rest[1:4], rest[4:], n_tiles, n_seqs)


def _gla_scratch(n_seqs, t):
    return [pltpu.VMEM((2, n_seqs, t, GV), F32),
            pltpu.VMEM((GQK, GLA_TILE), BF16),
            pltpu.VMEM((2, GLA_TILE, GLA_TILE), F32),
            pltpu.VMEM((GLA_HEADS * GLA_TILE, GQK), BF16)]


def _gla_body(in_refs, init_refs, gn_ref, out_refs, scratch_refs, n_tiles, n_seqs):
    gq_ref, gk_ref, gv_ref, gf_ref, gb_ref, go_ref = in_refs
    o_ref, sf_ref, sb_ref = out_refs
    oacc_ref, bdqk_ref, tri_ref, hm_ref = scratch_refs
    g_refs = (gf_ref, gb_ref)
    state_refs = (sf_ref, sb_ref)

    @pl.when(pl.program_id(0) == 0)
    def _():
        ri = lax.broadcasted_iota(jnp.int32, (GLA_TILE, GLA_TILE), 0)
        ci = lax.broadcasted_iota(jnp.int32, (GLA_TILE, GLA_TILE), 1)
        same_chunk = (ri // CHUNK) == (ci // CHUNK)
        bdqk_ref[...] = jnp.where(same_chunk, 1.0, 0.0).astype(BF16)
        tri_ref[0] = jnp.where(same_chunk & (ri >= ci), 1.0, 0.0)
        tri_ref[1] = jnp.where(same_chunk & (ci >= ri), 1.0, 0.0)
        hm_ref[...] = jnp.where(
            lax.broadcasted_iota(jnp.int32, (GLA_HEADS * GLA_TILE, GQK), 0) // GLA_TILE
            == lax.broadcasted_iota(jnp.int32, (GLA_HEADS * GLA_TILE, GQK), 1) // GLA_DK,
            1.0, 0.0).astype(BF16)

    row8 = lax.broadcasted_iota(jnp.int32, (8, GQK), 0)

    def row_slice(start, size):
        return pl.ds(start if isinstance(start, int) else pl.multiple_of(start, size), size)

    def tile_rows(t):
        return row_slice(t * GLA_TILE, GLA_TILE)

    finished = set()

    def total_row(c, d):
        return c * CHUNK + (CHUNK - 1 if d == 0 else 0)

    def tile_dir(b, t, d):
        rows = tile_rows(t)
        g = g_refs[d][b, rows, :]
        g_hi = g.astype(BF16)
        g_lo = (g - g_hi.astype(F32)).astype(BF16)
        tri_b = tri_ref[d].astype(BF16)
        cum = (jnp.dot(tri_b, g_hi, preferred_element_type=F32)
               + jnp.dot(tri_b, g_lo, preferred_element_type=F32))
        yield
        totals = [cum[total_row(c, d):total_row(c, d) + 1, :] for c in range(CHUNKS_PER_TILE)]
        tot8 = jnp.zeros((8, GQK), F32)
        for c in range(CHUNKS_PER_TILE):
            tot8 = jnp.where(row8 == c, totals[c], tot8)
        dec_t = jnp.concatenate([jnp.exp(tot8), jnp.zeros((LANES - 8, GQK), F32)], axis=0).T
        q = gq_ref[b, rows, :] * (GLA_DK ** -0.5)
        k = gk_ref[b, rows, :]
        v = gv_ref[b, rows, :]
        tot = jnp.concatenate([jnp.broadcast_to(tc, (CHUNK, GQK)) for tc in totals], axis=0)
        qe = (q * jnp.exp(cum)).astype(BF16)
        ke = (k * jnp.exp(-cum)).astype(BF16)
        kd_t = (k * jnp.exp(tot - cum)).T.astype(BF16)
        bd_qk = bdqk_ref[...] > 0
        tri = tri_ref[d] > 0

        qm = jnp.where(hm_ref[...] > 0, jnp.tile(qe, (GLA_HEADS, 1)), 0.0)
        yield
        att = lax.dot_general(qm, ke, _NT, preferred_element_type=F32)

        yield
        intra, upd = [], []
        for hd in range(GLA_HEADS):
            vh = v[:, hd * GLA_DV:(hd + 1) * GLA_DV]
            a_h = jnp.where(tri, att[hd * GLA_TILE:(hd + 1) * GLA_TILE, :], 0.0).astype(BF16)
            intra.append(jnp.dot(a_h, vh, preferred_element_type=F32))
            kd_h = jnp.tile(kd_t[hd * GLA_DK:(hd + 1) * GLA_DK, :], (CHUNKS_PER_TILE, 1))
            upd.append(jnp.dot(jnp.where(bd_qk, kd_h, 0.0), vh, preferred_element_type=F32))

        yield
        state = [state_refs[d][b, hd] for hd in range(GLA_HEADS)]
        order = range(CHUNKS_PER_TILE) if d == 0 else range(CHUNKS_PER_TILE - 1, -1, -1)
        seen = {}
        for c in order:
            seen[c] = jnp.concatenate(state, axis=0).astype(BF16)
            decay = jnp.broadcast_to(dec_t[:, c:c + 1], (GQK, GLA_DV))
            for hd in range(GLA_HEADS):
                ks = slice(hd * GLA_DK, (hd + 1) * GLA_DK)
                state[hd] = decay[ks, :] * state[hd] + upd[hd][c * CHUNK:(c + 1) * CHUNK, :]
        for hd in range(GLA_HEADS):
            state_refs[d][b, hd] = state[hd]

        yield
        for c in range(CHUNKS_PER_TILE):
            cr = slice(c * CHUNK, (c + 1) * CHUNK)
            q_c = jnp.concatenate([qm[hd * GLA_TILE + c * CHUNK:hd * GLA_TILE + (c + 1) * CHUNK, :]
                                   for hd in range(GLA_HEADS)], axis=0)
            inter = jnp.dot(q_c, seen[c], preferred_element_type=F32)
            o = jnp.concatenate([intra[hd][cr, :] + inter[hd * CHUNK:(hd + 1) * CHUNK, :]
                                 for hd in range(GLA_HEADS)], axis=1)
            oacc_ref[d, b, row_slice(t * GLA_TILE + c * CHUNK, CHUNK), :] = o
        if isinstance(t, int):
            finished.add((b, t, d))

    if init_refs is None:
        sf_ref[...] = jnp.zeros(sf_ref.shape, F32)
        sb_ref[...] = jnp.zeros(sb_ref.shape, F32)
    else:
        sf_ref[...] = init_refs[0][...]
        sb_ref[...] = init_refs[1][...]

    gn = gn_ref[...]

    def epilogue(t):
        while isinstance(t, int) and not all((b, t, d) in finished for b in range(n_seqs) for d in range(2)):
            yield
        rows = tile_rows(t)
        for b in range(n_seqs):
            for hd in range(GLA_HEADS):
                vs = slice(hd * GLA_DV, (hd + 1) * GLA_DV)
                o = _rms(oacc_ref[0, b, rows, vs] + oacc_ref[1, b, rows, vs], gn)
                go = go_ref[b, rows, vs]
                o_ref[b, rows, vs] = (o * (go * _sigmoid(go))).astype(BF16)
        return
        yield

    def main_chains(first_tile, n):
        chains = []
        for t in [first_tile + u for u in range(n)]:
            for b in range(n_seqs):
                chains += [tile_dir(b, t, 0), tile_dir(b, n_tiles - 1 - t, 1)]
        return chains

    if n_tiles <= GLA_STATIC_TILES:
        _interleave(main_chains(0, n_tiles) + [epilogue(t) for t in range(n_tiles)])
    else:
        per_step = next(c for c in (4, 2, 1) if n_tiles % c == 0)

        def main_body(i, carry):
            _interleave(main_chains(i * per_step, per_step))
            return carry

        lax.fori_loop(0, n_tiles // per_step, main_body, 0)

        def epilogue_body(t, carry):
            _interleave([epilogue(t)])
            return carry

        lax.fori_loop(0, n_tiles, epilogue_body, 0)


def _gla(gq, gk, gv, gf, gb, go, init_states, gn, n_seqs):
    b, t, _ = gq.shape
    n_tiles = t // GLA_TILE
    zero_init = init_states is None
    seq = lambda c: pl.BlockSpec((n_seqs, t, c), lambda i: (i, 0, 0))
    st = pl.BlockSpec((n_seqs, GLA_HEADS, GLA_DK, GLA_DV), lambda i: (i, 0, 0, 0))
    in_specs = [seq(GQK), seq(GQK), seq(GV), seq(GQK), seq(GQK), seq(GV)]
    args = [gq, gk, gv, gf, gb, go]
    if not zero_init:
        in_specs += [st, st]
        args += list(init_states)
    in_specs.append(_const_spec(gn.shape))
    args.append(gn)
    return pl.pallas_call(
        functools.partial(_gla_kernel, n_tiles=n_tiles, n_seqs=n_seqs, zero_init=zero_init),
        out_shape=[jax.ShapeDtypeStruct((b, t, GV), BF16),
                   jax.ShapeDtypeStruct((b, GLA_HEADS, GLA_DK, GLA_DV), F32),
                   jax.ShapeDtypeStruct((b, GLA_HEADS, GLA_DK, GLA_DV), F32)],
        grid=(b // n_seqs,),
        in_specs=in_specs,
        out_specs=[seq(GV), st, st],
        scratch_shapes=_gla_scratch(n_seqs, t),
        name="gla_%d" % t,
        compiler_params=pltpu.CompilerParams(dimension_semantics=("arbitrary",),
                                             vmem_limit_bytes=VMEM_LIMIT),
    )(*args)


def _ffn_kernel(xp_ref, xs_ref, atp_ref, ats_ref, glp_ref, gls_ref, mod_ref, wout_ref, nf_ref,
                wfi_ref, wfo_ref, fn_ref, yp_ref, ys_ref, act_ref, *, ctx_tiles, tiles_per_seq):
    def sub_tile(x_ref, at_ref, gl_ref, y_ref, mod_row, r0):
        rows = slice(r0, r0 + FFN_SUB)
        gt1, sh2, sc2, gt2 = _mod_rows(mod_ref, mod_row)[2:]
        mix = (jnp.dot(at_ref[rows, :], wout_ref[0:VALL, :], preferred_element_type=F32)
               + jnp.dot(gl_ref[rows, :], wout_ref[VALL:, :], preferred_element_type=F32))
        yield
        x1 = x_ref[rows, :] + gt1 * mix
        h2 = (_rms(x1, nf_ref[...]) * (1.0 + sc2) + sh2).astype(BF16)
        yield
        for j in range(N_FF_CHUNKS):
            cs = slice(j * FF_CHUNK, (j + 1) * FF_CHUNK)
            a = jnp.dot(h2, wfi_ref[:, cs], preferred_element_type=F32)
            g = jnp.dot(h2, wfi_ref[:, D_FF + j * FF_CHUNK:D_FF + (j + 1) * FF_CHUNK],
                        preferred_element_type=F32)
            act_ref[rows, cs] = (a * _sigmoid(a) * g).astype(BF16)
            if j in FFN_PHASE_ENDS:
                yield
        ff = jnp.dot(act_ref[rows, :], wfo_ref[...], preferred_element_type=F32)
        yield
        x2 = x1 + gt2 * ff
        y_ref[rows, :] = _rms(x2, fn_ref[...])

    def tile(x_ref, at_ref, gl_ref, y_ref, mod_row):
        _interleave([sub_tile(x_ref, at_ref, gl_ref, y_ref, mod_row, r0)
                     for r0 in range(0, x_ref.shape[0], FFN_SUB)])

    t = pl.program_id(0)

    @pl.when(t < ctx_tiles)
    def _():
        tile(xp_ref, atp_ref, glp_ref, yp_ref, 0)

    @pl.when(t >= ctx_tiles)
    def _():
        tile(xs_ref, ats_ref, gls_ref, ys_ref, 1 + (t - ctx_tiles) // tiles_per_seq)


def _ffn(xp, xs, attn_p, attn_s, gla_p, gla_s, mod, wout, nf, wfi, wfo, fn, tm, tiles_per_seq):
    d = xp.shape[1]
    ctx_tiles = xp.shape[0] // tm
    lat_tiles = xs.shape[0] // tm
    ctx_map = lambda s: (jnp.minimum(s, ctx_tiles - 1), 0)
    lat_map = lambda s: (jnp.maximum(s - ctx_tiles, 0), 0)
    tile = lambda c, m: pl.BlockSpec((tm, c), m)
    return pl.pallas_call(
        functools.partial(_ffn_kernel, ctx_tiles=ctx_tiles, tiles_per_seq=tiles_per_seq),
        out_shape=[jax.ShapeDtypeStruct(xp.shape, F32), jax.ShapeDtypeStruct(xs.shape, F32)],
        grid=(ctx_tiles + lat_tiles,),
        in_specs=[tile(d, ctx_map), tile(d, lat_map), tile(VALL, ctx_map), tile(VALL, lat_map),
                  tile(GV, ctx_map), tile(GV, lat_map), _const_spec(mod.shape),
                  _const_spec(wout.shape), _const_spec(nf.shape), _const_spec(wfi.shape),
                  _const_spec(wfo.shape), _const_spec(fn.shape)],
        out_specs=[tile(d, ctx_map), tile(d, lat_map)],
        scratch_shapes=[pltpu.VMEM((tm, D_FF), BF16)],
        name="out_ffn",
        compiler_params=pltpu.CompilerParams(dimension_semantics=("arbitrary",),
                                             vmem_limit_bytes=VMEM_LIMIT),
    )(xp, xs, attn_p, attn_s, gla_p, gla_s, mod, wout, nf, wfi, wfo, fn)


def _rope_tables(n_tokens):
    t = np.arange(n_tokens)
    row = (t // GRID_W).astype(np.float32)
    col = (t % GRID_W).astype(np.float32)
    half = MLA_ROPE // 2
    inv = (np.float32(ROPE_BASE) ** (-np.arange(0, half, 2, dtype=np.float32) / np.float32(half))).astype(np.float32)
    ang_r = row[:, None] * inv
    ang_c = col[:, None] * inv
    ang = np.concatenate([ang_r, ang_r, ang_c, ang_c], axis=-1).astype(np.float32)
    cos, sin = np.cos(ang), np.sin(ang)
    first = (np.arange(MLA_ROPE) % half) < (half // 2)
    cos_t = np.ones((n_tokens, LANES), np.float32)
    sa_t = np.zeros((n_tokens, LANES), np.float32)
    sb_t = np.zeros((n_tokens, LANES), np.float32)
    cos_t[:, ROPE_LANE0:ROPE_LANE0 + MLA_ROPE] = cos
    sa_t[:, ROPE_LANE0:ROPE_LANE0 + MLA_ROPE] = np.where(first, -sin, 0.0)
    sb_t[:, ROPE_LANE0:ROPE_LANE0 + MLA_ROPE] = np.where(first, 0.0, sin)
    return jnp.asarray(cos_t), jnp.asarray(sa_t), jnp.asarray(sb_t)


def kernel(x_prompt, x_sample, cache_kv_latent, cache_k_rope, state_gla_fwd, state_gla_bwd, c, c_ctx, w_ada, b_ada, norm_attn, w_in, mla_q_norm, w_uq, mla_kv_norm, w_ukv, w_gate_f, b_gate_f, w_gate_b, b_gate_b, gla_norm, w_out, norm_ffn, w_ffn_in, w_ffn_out, final_norm):
    batch, seq, d = x_prompt.shape
    dec_batch, dec_seq, _ = x_sample.shape
    assert w_ada.shape[0] == 1 and w_in.shape[-1] == W_COLS and w_ffn_in.shape[-1] == 2 * D_FF
    l = 0

    mod = _ada(c_ctx, c, w_ada[l], b_ada[l])

    win, wuq, wk, wvt, wg = _prep_in_weights(w_in, w_uq, w_ukv, w_gate_f, w_gate_b)
    in_w = (norm_attn[l].reshape(1, d), win, mla_q_norm[l].reshape(1, Q_LORA), wuq,
            mla_kv_norm[l].reshape(1, KV_LORA), wk, wvt, wg, b_gate_f, b_gate_b)
    gn = gla_norm[l].reshape(1, GLA_DV)
    tm, tm_ffn = 512, 512
    r3 = lambda a, b_, t: a.reshape(b_, t, a.shape[-1])

    xp = x_prompt.reshape(batch * seq, d)
    (attn_p, gla_p, sf, sb, ckv, kr_t) = _inproj(xp, mod, lambda i: 0, in_w, None, CTX_STEP_SEQS * seq, seq, gn)

    xs = x_sample.reshape(dec_batch * dec_seq, d)
    tiles = dec_seq // tm
    (q, k, vt, gq, gk, gv, gf, gb, go) = _inproj(xs, mod, lambda i: 1 + i // tiles, in_w,
                                                  _rope_tables(dec_seq), tm, dec_seq)
    kc, vct = _decomp(cache_kv_latent[:, l], jnp.swapaxes(cache_k_rope[:, l], 1, 2), wk, wvt)
    attn_s, wout, wfi, wfo = _attention(r3(q, dec_batch, dec_seq), r3(k, dec_batch, dec_seq), vt, (kc, vct),
                                        ATTN_LAT_QUERIES, 1, (w_out, w_ffn_in, w_ffn_out))
    gla_s, _, _ = _gla(r3(gq, dec_batch, dec_seq), r3(gk, dec_batch, dec_seq), r3(gv, dec_batch, dec_seq),
                       r3(gf, dec_batch, dec_seq), r3(gb, dec_batch, dec_seq), r3(go, dec_batch, dec_seq),
                       (state_gla_fwd[:, l].astype(F32), state_gla_bwd[:, l].astype(F32)), gn, 1)

    flat = lambda a: a.reshape(-1, a.shape[-1])
    y_prompt, y_sample = _ffn(xp, xs, flat(attn_p), flat(attn_s), flat(gla_p), flat(gla_s), mod,
                              wout, norm_ffn[l].reshape(1, d), wfi, wfo,
                              final_norm.reshape(1, d), tm_ffn, dec_seq // tm_ffn)
    y_prompt = y_prompt.reshape(batch, seq, d)
    y_sample = y_sample.reshape(dec_batch, dec_seq, d)

    new_kv_latent = ckv.reshape(batch, 1, seq, KV_LORA)
    new_k_rope = jnp.swapaxes(kr_t, 1, 2).reshape(batch, 1, seq, MLA_ROPE)
    new_state_fwd = sf.reshape(batch, 1, GLA_HEADS, GLA_DK, GLA_DV).astype(x_prompt.dtype)
    new_state_bwd = sb.reshape(batch, 1, GLA_HEADS, GLA_DK, GLA_DV).astype(x_prompt.dtype)
    return (y_prompt, y_sample, new_kv_latent, new_k_rope, new_state_fwd, new_state_bwd)
```

```python
import functools

import numpy as np
import jax
import jax.numpy as jnp
from jax import lax
from jax.experimental import pallas as pl
from jax.experimental.pallas import tpu as pltpu

F32 = jnp.float32
BF16 = jnp.bfloat16

GRID_W = 64
MLA_HEADS = 8
MLA_NOPE = 64
MLA_ROPE = 32
MLA_QK = MLA_NOPE + MLA_ROPE
MLA_V = 64
Q_LORA = 384
KV_LORA = 256
GLA_HEADS = 4
GLA_DK = 64
GLA_DV = 128
GATE_RANK = 16
GATE_NORM = 16.0
CHUNK = 64
D_FF = 2816
ROPE_BASE = 10000.0
EPS = 1e-6
LOG2_E = 1.4426950408889634

LANES = 128
HEAD_PAD = LANES
ROPE_LANE0 = MLA_NOPE
GQK = GLA_HEADS * GLA_DK
GV = GLA_HEADS * GLA_DV
QPAD = MLA_HEADS * HEAD_PAD
VALL = MLA_HEADS * MLA_V
ONES_ROWS = 16
KEY_BLOCK = 1024

W_KR = Q_LORA + KV_LORA
W_GQ = W_KR + MLA_ROPE
W_GF = W_GQ + 2 * GQK + GV
W_GO = W_GF + 2 * GATE_RANK
W_COLS = W_GO + GV

Z_Q = 0
Z_KV = Z_Q + Q_LORA
Z_GQ = Z_KV + KV_LORA
Z_GK = Z_GQ + GQK
Z_GV = Z_GK + GQK
Z_GO = Z_GV + GV
Z_MISC = Z_GO + GV
Z_COLS = Z_MISC + LANES

FF_CHUNK = 256
N_FF_CHUNKS = D_FF // FF_CHUNK
FFN_SUB = 256
FFN_PHASE_ENDS = (3, 7, 10)

GLA_TILE = 256
CHUNKS_PER_TILE = GLA_TILE // CHUNK
ADA_ROWS = 128
INPROJ_SUB = 512
Q_TILE = 256
ATTN_LAT_QUERIES = 512
GLA_STATIC_TILES = 8

VMEM_LIMIT = 56 * 1024 * 1024

_NT = (((1,), (1,)), ((), ()))


def _rms(x, w):
    return x * lax.rsqrt(jnp.mean(x * x, axis=-1, keepdims=True) + EPS) * w


def _sigmoid(x):
    return 1.0 / (1.0 + jnp.exp(-x))


def _log_sigmoid(x):
    return jnp.minimum(x, 0.0) - jnp.log1p(jnp.exp(-jnp.abs(x)))


def _interleave(chains):
    pending, active = list(chains), []
    while pending or active:
        if pending:
            active.append(pending.pop(0))
        for chain in list(active):
            try:
                next(chain)
            except StopIteration:
                active.remove(chain)


def _const_spec(shape):
    nd = len(shape)
    return pl.BlockSpec(shape, lambda *_: (0,) * nd, pipeline_mode=pl.Buffered(1))


def _mod_rows(mod_ref, r):
    return [mod_ref[k, pl.ds(r, 1), :] for k in range(mod_ref.shape[0])]


def _ada_kernel(cctx_ref, c_ref, w_ref, b_ref, o_ref):
    k = pl.program_id(0)
    d = o_ref.shape[2]
    row = lax.broadcasted_iota(jnp.int32, (8, cctx_ref.shape[1]), 0)
    cond = jnp.where(row == 0, cctx_ref[...], 0.0)
    for r in range(c_ref.shape[0]):
        cond = jnp.where(row == 1 + r, c_ref[r:r + 1, :], cond)
    s = (cond * _sigmoid(cond)).astype(BF16)
    part = jnp.dot(s, w_ref[...].astype(BF16), preferred_element_type=F32)
    for j in range(o_ref.shape[0]):
        sl = slice(j * d, (j + 1) * d)

        @pl.when(k == 0)
        def _():
            o_ref[j] = part[:, sl] + b_ref[:, sl]

        @pl.when(k > 0)
        def _():
            o_ref[j] += part[:, sl]


def _ada(c_ctx, c, w_ada, b_ada):
    d = w_ada.shape[0]
    n = w_ada.shape[1]
    assert 1 + c.shape[0] <= 8
    return pl.pallas_call(
        _ada_kernel,
        out_shape=jax.ShapeDtypeStruct((n // d, 8, d), F32),
        grid=(d // ADA_ROWS,),
        in_specs=[pl.BlockSpec((1, ADA_ROWS), lambda k: (0, k)),
                  pl.BlockSpec((c.shape[0], ADA_ROWS), lambda k: (0, k)),
                  pl.BlockSpec((ADA_ROWS, n), lambda k: (k, 0)),
                  pl.BlockSpec((1, n), lambda k: (0, 0))],
        out_specs=pl.BlockSpec((n // d, 8, d), lambda k: (0, 0, 0)),
        name="ada_mod",
        compiler_params=pltpu.CompilerParams(dimension_semantics=("arbitrary",)),
    )(c_ctx.reshape(1, d), c, w_ada, b_ada.reshape(1, n))


def _prep_kernel(wint_ref, wuq_ref, wukv_ref, wgf_ref, wgb_ref, win_o, wuq_o, wk_o, wvt_o, wg_o):
    cols = wint_ref.shape[1]
    for dst, src, n in ((Z_Q, 0, W_KR), (Z_GQ, W_GQ, W_GF - W_GQ), (Z_GO, W_GO, GV)):
        win_o[:, dst:dst + n] = wint_ref[src:src + n, :].T.astype(BF16)
    z32 = jnp.zeros((32, cols), F32)
    misc_t = jnp.concatenate([wint_ref[W_GF:W_GO, :], z32, wint_ref[W_KR:W_GQ, :], z32], axis=0)
    win_o[:, Z_MISC:Z_COLS] = misc_t.T.astype(BF16)

    u = wuq_ref[...]
    zq = jnp.zeros((u.shape[0], HEAD_PAD - MLA_QK), F32)
    for hd in range(MLA_HEADS):
        blk = jnp.concatenate([u[:, hd * MLA_QK:(hd + 1) * MLA_QK], zq], axis=1)
        wuq_o[:, hd * HEAD_PAD:(hd + 1) * HEAD_PAD] = blk.astype(BF16)

    @pl.when(pl.program_id(0) == 0)
    def _():
        kv = wukv_ref[...]
        per = MLA_NOPE + MLA_V
        lane = lax.broadcasted_iota(jnp.int32, (kv.shape[0], per), 1)
        for hd in range(MLA_HEADS):
            blk = kv[:, hd * per:(hd + 1) * per]
            wk_o[:, hd * HEAD_PAD:(hd + 1) * HEAD_PAD] = jnp.where(lane < MLA_NOPE, blk, 0.0).astype(BF16)
        wv = jnp.concatenate([kv[:, hd * per + MLA_NOPE:(hd + 1) * per] for hd in range(MLA_HEADS)], axis=1)
        wvt_o[...] = wv.T.astype(BF16)

        wg_o[...] = jnp.zeros(wg_o.shape, BF16)
        wg_o[0:GATE_RANK, 0:GQK] = wgf_ref[...].astype(BF16)
        wg_o[GATE_RANK:2 * GATE_RANK, GQK:2 * GQK] = wgb_ref[...].astype(BF16)


def _prep_in_weights(w_in, w_uq, w_ukv, w_gate_f, w_gate_b):
    d = w_in.shape[1]
    steps = 4
    w_in_t = jnp.swapaxes(w_in, 1, 2)
    rb3 = lambda r, c: pl.BlockSpec((None, r // steps, c), lambda i: (0, i, 0))
    rb = lambda r, c: pl.BlockSpec((r // steps, c), lambda i: (i, 0))
    full3 = lambda shape: pl.BlockSpec((None,) + tuple(shape[1:]), lambda i: (0, 0, 0))
    full = lambda shape: pl.BlockSpec(shape, lambda i: (0, 0))
    return pl.pallas_call(
        _prep_kernel,
        out_shape=[jax.ShapeDtypeStruct((d, Z_COLS), BF16),
                   jax.ShapeDtypeStruct((Q_LORA, QPAD), BF16),
                   jax.ShapeDtypeStruct((KV_LORA, QPAD), BF16),
                   jax.ShapeDtypeStruct((VALL, KV_LORA), BF16),
                   jax.ShapeDtypeStruct((LANES, 2 * GQK), BF16)],
        grid=(steps,),
        in_specs=[pl.BlockSpec((None, W_COLS, d // steps), lambda i: (0, 0, i)),
                  rb3(Q_LORA, MLA_HEADS * MLA_QK), full3(w_ukv.shape),
                  full3(w_gate_f.shape), full3(w_gate_b.shape)],
        out_specs=[rb(d, Z_COLS), rb(Q_LORA, QPAD), full((KV_LORA, QPAD)), full((VALL, KV_LORA)),
                   full((LANES, 2 * GQK))],
        name="weight_prep",
        compiler_params=pltpu.CompilerParams(dimension_semantics=("arbitrary",)),
    )(w_in_t, w_uq, w_ukv, w_gate_f, w_gate_b)


def _inproj_kernel(*refs, latent, mod_row):
    (x_ref, mod_ref, nw_ref, win_ref, qn_ref, wuq_ref, kvn_ref, wk_ref, wvt_ref, wg_ref, bgf_ref, bgb_ref) = refs[:12]
    if latent:
        cos_ref, sa_ref, sb_ref = refs[12:15]
        q_ref, k_ref, vt_ref, gq_ref, gk_ref, gv_ref, gf_ref, gb_ref, go_ref = refs[15:]
    else:
        (gn_ref, attn_ref, gla_ref, sf_ref, sb_ref, ckv_ref, krt_ref, q_ref, k_ref, vt_ref, st_ref, p_ref,
         gq_ref, gk_ref, gv_ref, gf_ref, gb_ref, go_ref) = refs[12:30]
        gla_scratch = refs[30:]
    seq = q_ref.shape[1]

    sh1, sc1 = _mod_rows(mod_ref, mod_row(pl.program_id(0)))[:2]
    scale = MLA_QK ** -0.5 * LOG2_E
    lane = lax.broadcasted_iota(jnp.int32, (INPROJ_SUB, LANES), 1)
    in_rope = (lane >= ROPE_LANE0) & (lane < ROPE_LANE0 + MLA_ROPE)

    def sub_tile(r0):
        rows = slice(r0, r0 + INPROJ_SUB)
        h = (_rms(x_ref[rows, :], nw_ref[...]) * (1.0 + sc1) + sh1).astype(BF16)
        yield
        z_all = jnp.dot(h, win_ref[...], preferred_element_type=F32)
        z = lambda lo, n: z_all[:, lo:lo + n]
        yield
        qn = _rms(z(Z_Q, Q_LORA), qn_ref[...]).astype(BF16)
        ckv = _rms(z(Z_KV, KV_LORA), kvn_ref[...])
        ckv_b = ckv.astype(BF16)
        misc = z(Z_MISC, LANES)
        yield
        q = jnp.dot(qn, wuq_ref[...], preferred_element_type=F32)
        kn = jnp.dot(ckv_b, wk_ref[...], preferred_element_type=F32)
        vt_ref[:, rows] = lax.dot_general(wvt_ref[...], ckv_b, _NT,
                                          preferred_element_type=F32).astype(BF16)
        gpre = jnp.dot(misc.astype(BF16), wg_ref[...], preferred_element_type=F32)
        yield
        if latent:
            cos, sa, sb = cos_ref[rows, :], sa_ref[rows, :], sb_ref[rows, :]

            def rope(t):
                return t * cos + pltpu.roll(t, LANES - 8, 1) * sa + pltpu.roll(t, 8, 1) * sb
        else:
            def rope(t):
                return t

        def put(ref, val, cols=slice(None)):
            if len(ref.shape) == 2:
                ref[rows, cols] = val
            else:
                for b in range(INPROJ_SUB // seq):
                    ref[r0 // seq + b, :, cols] = val[b * seq:(b + 1) * seq, :]

        krope = rope(misc)
        for hd in range(MLA_HEADS):
            sl = slice(hd * HEAD_PAD, (hd + 1) * HEAD_PAD)
            put(q_ref, (rope(q[:, sl]) * scale).astype(BF16), sl)
            put(k_ref, jnp.where(in_rope, krope, kn[:, sl]).astype(BF16), sl)
        put(gq_ref, z(Z_GQ, GQK))
        put(gk_ref, z(Z_GK, GQK))
        put(gv_ref, z(Z_GV, GV).astype(BF16))
        put(go_ref, z(Z_GO, GV))
        put(gf_ref, _log_sigmoid(gpre[:, :GQK] + bgf_ref[...]) * (1.0 / GATE_NORM))
        put(gb_ref, _log_sigmoid(gpre[:, GQK:] + bgb_ref[...]) * (1.0 / GATE_NORM))
        if not latent:
            ckv_ref[rows, :] = ckv
            misc_t = misc.T
            n = krt_ref.shape[2]
            for b in range(INPROJ_SUB // n):
                krt_ref[r0 // n + b] = misc_t[ROPE_LANE0:ROPE_LANE0 + MLA_ROPE, b * n:(b + 1) * n]

    _interleave([sub_tile(r0) for r0 in range(0, x_ref.shape[0], INPROJ_SUB)])
    if not latent:
        n_seqs = q_ref.shape[0]
        _gla_body((gq_ref, gk_ref, gv_ref, gf_ref, gb_ref, go_ref), None, gn_ref,
                  (gla_ref, sf_ref, sb_ref), gla_scratch, seq // GLA_TILE, n_seqs,
                  [_attn_pipeline(q_ref, k_ref, vt_ref, attn_ref, st_ref, p_ref, None, n_seqs)])


def _inproj(x2d, mod, mod_row, weights, rope_tabs, tm, seq_len, gla_norm=None):
    n_tok, d = x2d.shape
    latent = rope_tabs is not None
    tiles_per_seq = max(seq_len // tm, 1)
    nw, win, qn, wuq, kvn, wk, wvt, wg, bgf, bgb = weights
    row = lambda i: (i, 0)
    in_specs = [pl.BlockSpec((tm, d), row), _const_spec(mod.shape),
                _const_spec(nw.shape), _const_spec(win.shape), _const_spec(qn.shape),
                _const_spec(wuq.shape), _const_spec(kvn.shape), _const_spec(wk.shape),
                _const_spec(wvt.shape), _const_spec(wg.shape), _const_spec(bgf.shape),
                _const_spec(bgb.shape)]
    args = [x2d, mod, nw, win, qn, wuq, kvn, wk, wvt, wg, bgf, bgb]
    if latent:
        tab = pl.BlockSpec((tm, LANES), lambda i: (i % tiles_per_seq, 0))
        in_specs += [tab, tab, tab]
        args += list(rope_tabs)
    per_token = lambda c, dt: (jax.ShapeDtypeStruct((n_tok, c), dt), pl.BlockSpec((tm, c), row))
    seq = tm if latent else seq_len
    per_seq = lambda r, c, dt: (jax.ShapeDtypeStruct((n_tok // seq, r, c), dt),
                                pl.BlockSpec((tm // seq, r, c), lambda i: (i, 0, 0)))
    gla_outs = [per_token(GQK, F32), per_token(GQK, F32), per_token(GV, BF16),
                per_token(GQK, F32), per_token(GQK, F32), per_token(GV, F32)]
    scratch = []
    if latent:
        outs = [per_seq(seq, QPAD, BF16), per_seq(seq, QPAD, BF16),
                (jax.ShapeDtypeStruct((VALL, n_tok), BF16), pl.BlockSpec((VALL, tm), lambda i: (0, i)))]
        outs += gla_outs
    else:
        in_specs.append(_const_spec(gla_norm.shape))
        args.append(gla_norm)
        state = (jax.ShapeDtypeStruct((n_tok // seq, GLA_HEADS, GLA_DK, GLA_DV), F32),
                 pl.BlockSpec((tm // seq, GLA_HEADS, GLA_DK, GLA_DV), lambda i: (i, 0, 0, 0)))
        outs = [per_seq(seq, VALL, BF16), per_seq(seq, GV, BF16), state, state,
                per_token(KV_LORA, F32), per_seq(MLA_ROPE, seq, F32)]
        vmem = lambda c, dt: pltpu.VMEM((tm // seq, seq, c), dt)
        scratch = [vmem(QPAD, BF16), vmem(QPAD, BF16), pltpu.VMEM((VALL, tm), BF16),
                   pltpu.VMEM((2, seq, Q_TILE), F32), pltpu.VMEM((2, seq, Q_TILE), BF16),
                   vmem(GQK, F32), vmem(GQK, F32), vmem(GV, BF16), vmem(GQK, F32), vmem(GQK, F32),
                   vmem(GV, F32)] + _gla_scratch(tm // seq, seq)
    return pl.pallas_call(
        functools.partial(_inproj_kernel, latent=latent, mod_row=mod_row),
        out_shape=[o[0] for o in outs],
        grid=(n_tok // tm,),
        in_specs=in_specs,
        out_specs=[o[1] for o in outs],
        scratch_shapes=scratch,
        name="inproj_lat" if latent else "inproj_ctx",
        compiler_params=pltpu.CompilerParams(dimension_semantics=("arbitrary",),
                                             vmem_limit_bytes=VMEM_LIMIT),
    )(*args)


def _decomp_kernel(ckv_ref, krt_ref, wk_ref, wvt_ref, k_ref, vt_ref):
    ckv_b = ckv_ref[...].astype(BF16)
    kn = jnp.dot(ckv_b, wk_ref[...], preferred_element_type=F32)
    n_keys = krt_ref.shape[1]
    kr = jnp.concatenate([jnp.zeros((ROPE_LANE0, n_keys), F32), krt_ref[...],
                          jnp.zeros((LANES - ROPE_LANE0 - MLA_ROPE, n_keys), F32)], axis=0).T
    lane = lax.broadcasted_iota(jnp.int32, kr.shape, 1)
    in_rope = (lane >= ROPE_LANE0) & (lane < ROPE_LANE0 + MLA_ROPE)
    for hd in range(MLA_HEADS):
        sl = slice(hd * HEAD_PAD, (hd + 1) * HEAD_PAD)
        k_ref[:, sl] = jnp.where(in_rope, kr, kn[:, sl]).astype(BF16)
    vt_ref[...] = lax.dot_general(wvt_ref[...], ckv_b, _NT, preferred_element_type=F32).astype(BF16)


def _decomp(ckv, kr_t, wk, wvt):
    b, s, _ = ckv.shape
    return pl.pallas_call(
        _decomp_kernel,
        out_shape=[jax.ShapeDtypeStruct((b, s, QPAD), BF16), jax.ShapeDtypeStruct((VALL, b * s), BF16)],
        grid=(b,),
        in_specs=[pl.BlockSpec((None, s, KV_LORA), lambda i: (i, 0, 0)),
                  pl.BlockSpec((None, MLA_ROPE, s), lambda i: (i, 0, 0)),
                  _const_spec(wk.shape), _const_spec(wvt.shape)],
        out_specs=[pl.BlockSpec((None, s, QPAD), lambda i: (i, 0, 0)),
                   pl.BlockSpec((VALL, s), lambda i: (0, i))],
        name="ctx_decompress",
        compiler_params=pltpu.CompilerParams(dimension_semantics=("arbitrary",)),
    )(ckv, kr_t, wk, wvt)


def _attn_kernel(*refs, has_ctx, n_seqs, n_side):
    n_in = 5 if has_ctx else 3
    side_in, refs = refs[n_in:n_in + n_side], refs[:n_in] + refs[n_in + n_side:]
    side_out, refs = refs[n_in + 1:n_in + 1 + n_side], refs[:n_in + 1] + refs[n_in + 1 + n_side:]
    if has_ctx:
        q_ref, kc_ref, vct_ref, k_ref, vt_ref, o_ref, st_ref, p_ref = refs
    else:
        q_ref, k_ref, vt_ref, o_ref, st_ref, p_ref = refs

    for src, dst in zip(side_in, side_out):
        dst[...] = src[...].astype(BF16)
    _interleave([_attn_pipeline(q_ref, k_ref, vt_ref, o_ref, st_ref, p_ref,
                                (kc_ref, vct_ref) if has_ctx else None, n_seqs)])


def _attn_pipeline(q_ref, k_ref, vt_ref, o_ref, st_ref, p_ref, ctx_refs, n_seqs):
    tq = Q_TILE

    def key_blocks(bi):
        srcs = [ctx_refs] if ctx_refs is not None else []
        blocks, row0 = [], 0
        for kr, vr in srcs + [(k_ref, vt_ref)]:
            n_keys = kr.shape[1]
            size = min(KEY_BLOCK, n_keys)
            for r in range(0, n_keys, size):
                blocks.append((kr, vr, r, bi * n_keys + r, size, row0))
                row0 += size
        return blocks

    units = [(bi, slice(q0, q0 + tq), hd) for bi in range(n_seqs)
             for q0 in range(0, q_ref.shape[1], tq) for hd in range(MLA_HEADS)]
    col_max = [None] * len(units)
    pair = []
    for stage in range(len(units) + 2):
        ua, ub, uc = stage, stage - 1, stage - 2
        run_max = None
        acc = jnp.zeros((MLA_V + ONES_ROWS, tq), F32)
        for j in range(len(key_blocks(0))):
            if ua < len(units):
                bi, qrows, hd = units[ua]
                kr, _, r0, _, size, srow = key_blocks(bi)[j]
                sl = slice(hd * HEAD_PAD, (hd + 1) * HEAD_PAD)
                st = lax.dot_general(kr[bi, r0:r0 + size, sl], q_ref[bi, qrows, sl], _NT,
                                     preferred_element_type=F32)
                st_ref[ua % 2, srow:srow + size, :] = st
                blk_max = jnp.max(st.reshape(size // 8, 8, tq), axis=0)
                run_max = blk_max if run_max is None else jnp.maximum(run_max, blk_max)
            if 0 <= ub < len(units):
                _, _, _, _, size, srow = key_blocks(0)[j]
                p_ref[ub % 2, srow:srow + size, :] = jnp.exp2(
                    st_ref[ub % 2, srow:srow + size, :] - col_max[ub]).astype(BF16)
            if uc >= 0:
                bi, _, hd = units[uc]
                _, vr, _, c0, size, srow = key_blocks(bi)[j]
                v_aug = jnp.concatenate([vr[hd * MLA_V:(hd + 1) * MLA_V, c0:c0 + size],
                                         jnp.ones((ONES_ROWS, size), BF16)], axis=0)
                acc = acc + jnp.dot(v_aug, p_ref[uc % 2, srow:srow + size, :],
                                    preferred_element_type=F32)
        if ua < len(units):
            col_max[ua] = jnp.max(run_max, axis=0, keepdims=True)
        if uc >= 0:
            bi, qrows, hd = units[uc]
            pair.append(acc[:MLA_V, :] / acc[MLA_V:MLA_V + 1, :])
            if len(pair) == 2:
                o_ref[bi, qrows, (hd - 1) * MLA_V:(hd + 1) * MLA_V] = (
                    jnp.concatenate(pair, axis=0).T.astype(BF16))
                pair = []
        yield


def _attention(q, k, vt, ctx_kv, tq, n_seqs, side_weights=()):
    b, t, _ = q.shape
    steps = (b // n_seqs) * (t // tq)
    assert (n_seqs == 1 or tq == t) and tq % Q_TILE == 0
    has_ctx = ctx_kv is not None
    in_specs = [pl.BlockSpec((n_seqs, tq, QPAD), lambda i, j: (i, j, 0))]
    args = [q]
    if has_ctx:
        kc, vct = ctx_kv
        s = kc.shape[1]
        in_specs += [pl.BlockSpec((n_seqs, s, QPAD), lambda i, j: (i, 0, 0)),
                     pl.BlockSpec((VALL, n_seqs * s), lambda i, j: (0, i))]
        args += [kc, vct]
    in_specs += [pl.BlockSpec((n_seqs, t, QPAD), lambda i, j: (i, 0, 0)),
                 pl.BlockSpec((VALL, n_seqs * t), lambda i, j: (0, i))]
    args += [k, vt]
    out_shape = [jax.ShapeDtypeStruct((b, t, VALL), BF16)]
    out_specs = [pl.BlockSpec((n_seqs, tq, VALL), lambda i, j: (i, j, 0))]
    nj = t // tq
    for w in side_weights:
        _, rows, cols = w.shape
        assert rows % (16 * steps) == 0
        in_specs.append(pl.BlockSpec((None, rows // steps, cols), lambda i, j: (0, i * nj + j, 0)))
        args.append(w)
        out_shape.append(jax.ShapeDtypeStruct((rows, cols), BF16))
        out_specs.append(pl.BlockSpec((rows // steps, cols), lambda i, j: (i * nj + j, 0)))
    return pl.pallas_call(
        functools.partial(_attn_kernel, has_ctx=has_ctx, n_seqs=n_seqs, n_side=len(side_weights)),
        out_shape=out_shape,
        grid=(b // n_seqs, t // tq),
        in_specs=in_specs,
        out_specs=out_specs,
        scratch_shapes=[pltpu.VMEM((2, t + (s if has_ctx else 0), Q_TILE), F32),
                        pltpu.VMEM((2, t + (s if has_ctx else 0), Q_TILE), BF16)],
        name="mla_attn_lat" if has_ctx else "mla_attn_ctx",
        compiler_params=pltpu.CompilerParams(dimension_semantics=("arbitrary", "arbitrary"),
                                             vmem_limit_bytes=VMEM_LIMIT),
    )(*args)


def _gla_kernel(*refs, n_tiles, n_seqs, zero_init):
    init_refs, rest = (None, refs[6:]) if zero_init else (refs[6:8], refs[8:])
    _gla_body(refs[:6], init_refs, rest[0], rest[1:4], rest[4:], n_tiles, n_seqs)


def _gla_scratch(n_seqs, t):
    return [pltpu.VMEM((2, n_seqs, t, GV), F32),
            pltpu.VMEM((GQK, GLA_TILE), BF16),
            pltpu.VMEM((2, GLA_TILE, GLA_TILE), F32),
            pltpu.VMEM((GLA_HEADS * GLA_TILE, GQK), BF16)]


def _gla_body(in_refs, init_refs, gn_ref, out_refs, scratch_refs, n_tiles, n_seqs, side_chains=()):
    gq_ref, gk_ref, gv_ref, gf_ref, gb_ref, go_ref = in_refs
    o_ref, sf_ref, sb_ref = out_refs
    oacc_ref, bdqk_ref, tri_ref, hm_ref = scratch_refs
    g_refs = (gf_ref, gb_ref)
    state_refs = (sf_ref, sb_ref)

    @pl.when(pl.program_id(0) == 0)
    def _():
        ri = lax.broadcasted_iota(jnp.int32, (GLA_TILE, GLA_TILE), 0)
        ci = lax.broadcasted_iota(jnp.int32, (GLA_TILE, GLA_TILE), 1)
        same_chunk = (ri // CHUNK) == (ci // CHUNK)
        bdqk_ref[...] = jnp.where(same_chunk, 1.0, 0.0).astype(BF16)
        tri_ref[0] = jnp.where(same_chunk & (ri >= ci), 1.0, 0.0)
        tri_ref[1] = jnp.where(same_chunk & (ci >= ri), 1.0, 0.0)
        hm_ref[...] = jnp.where(
            lax.broadcasted_iota(jnp.int32, (GLA_HEADS * GLA_TILE, GQK), 0) // GLA_TILE
            == lax.broadcasted_iota(jnp.int32, (GLA_HEADS * GLA_TILE, GQK), 1) // GLA_DK,
            1.0, 0.0).astype(BF16)

    row8 = lax.broadcasted_iota(jnp.int32, (8, GQK), 0)

    def row_slice(start, size):
        return pl.ds(start if isinstance(start, int) else pl.multiple_of(start, size), size)

    def tile_rows(t):
        return row_slice(t * GLA_TILE, GLA_TILE)

    finished = set()

    def total_row(c, d):
        return c * CHUNK + (CHUNK - 1 if d == 0 else 0)

    def tile_dir(b, t, d):
        rows = tile_rows(t)
        g = g_refs[d][b, rows, :]
        g_hi = g.astype(BF16)
        g_lo = (g - g_hi.astype(F32)).astype(BF16)
        tri_b = tri_ref[d].astype(BF16)
        cum = (jnp.dot(tri_b, g_hi, preferred_element_type=F32)
               + jnp.dot(tri_b, g_lo, preferred_element_type=F32))
        yield
        totals = [cum[total_row(c, d):total_row(c, d) + 1, :] for c in range(CHUNKS_PER_TILE)]
        tot8 = jnp.zeros((8, GQK), F32)
        for c in range(CHUNKS_PER_TILE):
            tot8 = jnp.where(row8 == c, totals[c], tot8)
        dec_t = jnp.concatenate([jnp.exp(tot8), jnp.zeros((LANES - 8, GQK), F32)], axis=0).T
        q = gq_ref[b, rows, :] * (GLA_DK ** -0.5)
        k = gk_ref[b, rows, :]
        v = gv_ref[b, rows, :]
        tot = jnp.concatenate([jnp.broadcast_to(tc, (CHUNK, GQK)) for tc in totals], axis=0)
        qe = (q * jnp.exp(cum)).astype(BF16)
        ke = (k * jnp.exp(-cum)).astype(BF16)
        kd_t = (k * jnp.exp(tot - cum)).T.astype(BF16)
        bd_qk = bdqk_ref[...] > 0
        tri = tri_ref[d] > 0

        qm = jnp.where(hm_ref[...] > 0, jnp.tile(qe, (GLA_HEADS, 1)), 0.0)
        yield
        att = lax.dot_general(qm, ke, _NT, preferred_element_type=F32)

        yield
        intra, upd = [], []
        for hd in range(GLA_HEADS):
            vh = v[:, hd * GLA_DV:(hd + 1) * GLA_DV]
            a_h = jnp.where(tri, att[hd * GLA_TILE:(hd + 1) * GLA_TILE, :], 0.0).astype(BF16)
            intra.append(jnp.dot(a_h, vh, preferred_element_type=F32))
            kd_h = jnp.tile(kd_t[hd * GLA_DK:(hd + 1) * GLA_DK, :], (CHUNKS_PER_TILE, 1))
            upd.append(jnp.dot(jnp.where(bd_qk, kd_h, 0.0), vh, preferred_element_type=F32))

        yield
        state = [state_refs[d][b, hd] for hd in range(GLA_HEADS)]
        order = range(CHUNKS_PER_TILE) if d == 0 else range(CHUNKS_PER_TILE - 1, -1, -1)
        seen = {}
        for c in order:
            seen[c] = jnp.concatenate(state, axis=0).astype(BF16)
            decay = jnp.broadcast_to(dec_t[:, c:c + 1], (GQK, GLA_DV))
            for hd in range(GLA_HEADS):
                ks = slice(hd * GLA_DK, (hd + 1) * GLA_DK)
                state[hd] = decay[ks, :] * state[hd] + upd[hd][c * CHUNK:(c + 1) * CHUNK, :]
        for hd in range(GLA_HEADS):
            state_refs[d][b, hd] = state[hd]

        yield
        for c in range(CHUNKS_PER_TILE):
            cr = slice(c * CHUNK, (c + 1) * CHUNK)
            q_c = jnp.concatenate([qm[hd * GLA_TILE + c * CHUNK:hd * GLA_TILE + (c + 1) * CHUNK, :]
                                   for hd in range(GLA_HEADS)], axis=0)
            inter = jnp.dot(q_c, seen[c], preferred_element_type=F32)
            o = jnp.concatenate([intra[hd][cr, :] + inter[hd * CHUNK:(hd + 1) * CHUNK, :]
                                 for hd in range(GLA_HEADS)], axis=1)
            oacc_ref[d, b, row_slice(t * GLA_TILE + c * CHUNK, CHUNK), :] = o
        if isinstance(t, int):
            finished.add((b, t, d))

    if init_refs is None:
        sf_ref[...] = jnp.zeros(sf_ref.shape, F32)
        sb_ref[...] = jnp.zeros(sb_ref.shape, F32)
    else:
        sf_ref[...] = init_refs[0][...]
        sb_ref[...] = init_refs[1][...]

    gn = gn_ref[...]

    def epilogue(t):
        while isinstance(t, int) and not all((b, t, d) in finished for b in range(n_seqs) for d in range(2)):
            yield
        rows = tile_rows(t)
        for b in range(n_seqs):
            for hd in range(GLA_HEADS):
                vs = slice(hd * GLA_DV, (hd + 1) * GLA_DV)
                o = _rms(oacc_ref[0, b, rows, vs] + oacc_ref[1, b, rows, vs], gn)
                go = go_ref[b, rows, vs]
                o_ref[b, rows, vs] = (o * (go * _sigmoid(go))).astype(BF16)
        return
        yield

    def main_chains(first_tile, n):
        chains = []
        for t in [first_tile + u for u in range(n)]:
            for b in range(n_seqs):
                chains += [tile_dir(b, t, 0), tile_dir(b, n_tiles - 1 - t, 1)]
        return chains

    if n_tiles <= GLA_STATIC_TILES:
        _interleave(list(side_chains) + main_chains(0, n_tiles) + [epilogue(t) for t in range(n_tiles)])
    else:
        assert not side_chains
        per_step = next(c for c in (4, 2, 1) if n_tiles % c == 0)

        def main_body(i, carry):
            _interleave(main_chains(i * per_step, per_step))
            return carry

        lax.fori_loop(0, n_tiles // per_step, main_body, 0)

        def epilogue_body(t, carry):
            _interleave([epilogue(t)])
            return carry

        lax.fori_loop(0, n_tiles, epilogue_body, 0)


def _gla(gq, gk, gv, gf, gb, go, init_states, gn, n_seqs):
    b, t, _ = gq.shape
    n_tiles = t // GLA_TILE
    zero_init = init_states is None
    seq = lambda c: pl.BlockSpec((n_seqs, t, c), lambda i: (i, 0, 0))
    st = pl.BlockSpec((n_seqs, GLA_HEADS, GLA_DK, GLA_DV), lambda i: (i, 0, 0, 0))
    in_specs = [seq(GQK), seq(GQK), seq(GV), seq(GQK), seq(GQK), seq(GV)]
    args = [gq, gk, gv, gf, gb, go]
    if not zero_init:
        in_specs += [st, st]
        args += list(init_states)
    in_specs.append(_const_spec(gn.shape))
    args.append(gn)
    return pl.pallas_call(
        functools.partial(_gla_kernel, n_tiles=n_tiles, n_seqs=n_seqs, zero_init=zero_init),
        out_shape=[jax.ShapeDtypeStruct((b, t, GV), BF16),
                   jax.ShapeDtypeStruct((b, GLA_HEADS, GLA_DK, GLA_DV), F32),
                   jax.ShapeDtypeStruct((b, GLA_HEADS, GLA_DK, GLA_DV), F32)],
        grid=(b // n_seqs,),
        in_specs=in_specs,
        out_specs=[seq(GV), st, st],
        scratch_shapes=_gla_scratch(n_seqs, t),
        name="gla_%d" % t,
        compiler_params=pltpu.CompilerParams(dimension_semantics=("arbitrary",),
                                             vmem_limit_bytes=VMEM_LIMIT),
    )(*args)


def _ffn_kernel(xp_ref, xs_ref, atp_ref, ats_ref, glp_ref, gls_ref, mod_ref, wout_ref, nf_ref,
                wfi_ref, wfo_ref, fn_ref, yp_ref, ys_ref, act_ref, *, ctx_tiles, tiles_per_seq):
    def sub_tile(x_ref, at_ref, gl_ref, y_ref, mod_row, r0):
        rows = slice(r0, r0 + FFN_SUB)
        gt1, sh2, sc2, gt2 = _mod_rows(mod_ref, mod_row)[2:]
        mix = (jnp.dot(at_ref[rows, :], wout_ref[0:VALL, :], preferred_element_type=F32)
               + jnp.dot(gl_ref[rows, :], wout_ref[VALL:, :], preferred_element_type=F32))
        yield
        x1 = x_ref[rows, :] + gt1 * mix
        h2 = (_rms(x1, nf_ref[...]) * (1.0 + sc2) + sh2).astype(BF16)
        yield
        for j in range(N_FF_CHUNKS):
            cs = slice(j * FF_CHUNK, (j + 1) * FF_CHUNK)
            a = jnp.dot(h2, wfi_ref[:, cs], preferred_element_type=F32)
            g = jnp.dot(h2, wfi_ref[:, D_FF + j * FF_CHUNK:D_FF + (j + 1) * FF_CHUNK],
                        preferred_element_type=F32)
            act_ref[rows, cs] = (a * _sigmoid(a) * g).astype(BF16)
            if j in FFN_PHASE_ENDS:
                yield
        ff = jnp.dot(act_ref[rows, :], wfo_ref[...], preferred_element_type=F32)
        yield
        x2 = x1 + gt2 * ff
        y_ref[rows, :] = _rms(x2, fn_ref[...])

    def tile(x_ref, at_ref, gl_ref, y_ref, mod_row):
        _interleave([sub_tile(x_ref, at_ref, gl_ref, y_ref, mod_row, r0)
                     for r0 in range(0, x_ref.shape[0], FFN_SUB)])

    t = pl.program_id(0)

    @pl.when(t < ctx_tiles)
    def _():
        tile(xp_ref, atp_ref, glp_ref, yp_ref, 0)

    @pl.when(t >= ctx_tiles)
    def _():
        tile(xs_ref, ats_ref, gls_ref, ys_ref, 1 + (t - ctx_tiles) // tiles_per_seq)


def _ffn(xp, xs, attn_p, attn_s, gla_p, gla_s, mod, wout, nf, wfi, wfo, fn, tm, tiles_per_seq):
    d = xp.shape[1]
    ctx_tiles = xp.shape[0] // tm
    lat_tiles = xs.shape[0] // tm
    ctx_map = lambda s: (jnp.minimum(s, ctx_tiles - 1), 0)
    lat_map = lambda s: (jnp.maximum(s - ctx_tiles, 0), 0)
    tile = lambda c, m: pl.BlockSpec((tm, c), m)
    return pl.pallas_call(
        functools.partial(_ffn_kernel, ctx_tiles=ctx_tiles, tiles_per_seq=tiles_per_seq),
        out_shape=[jax.ShapeDtypeStruct(xp.shape, F32), jax.ShapeDtypeStruct(xs.shape, F32)],
        grid=(ctx_tiles + lat_tiles,),
        in_specs=[tile(d, ctx_map), tile(d, lat_map), tile(VALL, ctx_map), tile(VALL, lat_map),
                  tile(GV, ctx_map), tile(GV, lat_map), _const_spec(mod.shape),
                  _const_spec(wout.shape), _const_spec(nf.shape), _const_spec(wfi.shape),
                  _const_spec(wfo.shape), _const_spec(fn.shape)],
        out_specs=[tile(d, ctx_map), tile(d, lat_map)],
        scratch_shapes=[pltpu.VMEM((tm, D_FF), BF16)],
        name="out_ffn",
        compiler_params=pltpu.CompilerParams(dimension_semantics=("arbitrary",),
                                             vmem_limit_bytes=VMEM_LIMIT),
    )(xp, xs, attn_p, attn_s, gla_p, gla_s, mod, wout, nf, wfi, wfo, fn)


def _rope_tables(n_tokens):
    t = np.arange(n_tokens)
    row = (t // GRID_W).astype(np.float32)
    col = (t % GRID_W).astype(np.float32)
    half = MLA_ROPE // 2
    inv = (np.float32(ROPE_BASE) ** (-np.arange(0, half, 2, dtype=np.float32) / np.float32(half))).astype(np.float32)
    ang_r = row[:, None] * inv
    ang_c = col[:, None] * inv
    ang = np.concatenate([ang_r, ang_r, ang_c, ang_c], axis=-1).astype(np.float32)
    cos, sin = np.cos(ang), np.sin(ang)
    first = (np.arange(MLA_ROPE) % half) < (half // 2)
    cos_t = np.ones((n_tokens, LANES), np.float32)
    sa_t = np.zeros((n_tokens, LANES), np.float32)
    sb_t = np.zeros((n_tokens, LANES), np.float32)
    cos_t[:, ROPE_LANE0:ROPE_LANE0 + MLA_ROPE] = cos
    sa_t[:, ROPE_LANE0:ROPE_LANE0 + MLA_ROPE] = np.where(first, -sin, 0.0)
    sb_t[:, ROPE_LANE0:ROPE_LANE0 + MLA_ROPE] = np.where(first, 0.0, sin)
    return jnp.asarray(cos_t), jnp.asarray(sa_t), jnp.asarray(sb_t)


def kernel(x_prompt, x_sample, cache_kv_latent, cache_k_rope, state_gla_fwd, state_gla_bwd, c, c_ctx, w_ada, b_ada, norm_attn, w_in, mla_q_norm, w_uq, mla_kv_norm, w_ukv, w_gate_f, b_gate_f, w_gate_b, b_gate_b, gla_norm, w_out, norm_ffn, w_ffn_in, w_ffn_out, final_norm):
    batch, seq, d = x_prompt.shape
    dec_batch, dec_seq, _ = x_sample.shape
    assert w_ada.shape[0] == 1 and w_in.shape[-1] == W_COLS and w_ffn_in.shape[-1] == 2 * D_FF
    l = 0

    mod = _ada(c_ctx, c, w_ada[l], b_ada[l])

    win, wuq, wk, wvt, wg = _prep_in_weights(w_in, w_uq, w_ukv, w_gate_f, w_gate_b)
    in_w = (norm_attn[l].reshape(1, d), win, mla_q_norm[l].reshape(1, Q_LORA), wuq,
            mla_kv_norm[l].reshape(1, KV_LORA), wk, wvt, wg, b_gate_f, b_gate_b)
    gn = gla_norm[l].reshape(1, GLA_DV)
    tm, tm_ffn = 512, 512
    r3 = lambda a, b_, t: a.reshape(b_, t, a.shape[-1])

    xp = x_prompt.reshape(batch * seq, d)
    (attn_p, gla_p, sf, sb, ckv, kr_t) = _inproj(xp, mod, lambda i: 0, in_w, None, tm, seq, gn)

    xs = x_sample.reshape(dec_batch * dec_seq, d)
    tiles = dec_seq // tm
    (q, k, vt, gq, gk, gv, gf, gb, go) = _inproj(xs, mod, lambda i: 1 + i // tiles, in_w,
                                                  _rope_tables(dec_seq), tm, dec_seq)
    kc, vct = _decomp(cache_kv_latent[:, l], jnp.swapaxes(cache_k_rope[:, l], 1, 2), wk, wvt)
    attn_s, wout, wfi, wfo = _attention(r3(q, dec_batch, dec_seq), r3(k, dec_batch, dec_seq), vt, (kc, vct),
                                        ATTN_LAT_QUERIES, 1, (w_out, w_ffn_in, w_ffn_out))
    gla_s, _, _ = _gla(r3(gq, dec_batch, dec_seq), r3(gk, dec_batch, dec_seq), r3(gv, dec_batch, dec_seq),
                       r3(gf, dec_batch, dec_seq), r3(gb, dec_batch, dec_seq), r3(go, dec_batch, dec_seq),
                       (state_gla_fwd[:, l].astype(F32), state_gla_bwd[:, l].astype(F32)), gn, 1)

    flat = lambda a: a.reshape(-1, a.shape[-1])
    y_prompt, y_sample = _ffn(xp, xs, flat(attn_p), flat(attn_s), flat(gla_p), flat(gla_s), mod,
                              wout, norm_ffn[l].reshape(1, d), wfi, wfo,
                              final_norm.reshape(1, d), tm_ffn, dec_seq // tm_ffn)
    y_prompt = y_prompt.reshape(batch, seq, d)
    y_sample = y_sample.reshape(dec_batch, dec_seq, d)

    new_kv_latent = ckv.reshape(batch, 1, seq, KV_LORA)
    new_k_rope = jnp.swapaxes(kr_t, 1, 2).reshape(batch, 1, seq, MLA_ROPE)
    new_state_fwd = sf.reshape(batch, 1, GLA_HEADS, GLA_DK, GLA_DV).astype(x_prompt.dtype)
    new_state_bwd = sb.reshape(batch, 1, GLA_HEADS, GLA_DK, GLA_DV).astype(x_prompt.dtype)
    return (y_prompt, y_sample, new_kv_latent, new_k_rope, new_state_fwd, new_state_bwd)
```

```python
import functools

import numpy as np
import jax
import jax.numpy as jnp
from jax import lax
from jax.experimental import pallas as pl
from jax.experimental.pallas import tpu as pltpu

F32 = jnp.float32
BF16 = jnp.bfloat16

GRID_W = 64
MLA_HEADS = 8
MLA_NOPE = 64
MLA_ROPE = 32
MLA_QK = MLA_NOPE + MLA_ROPE
MLA_V = 64
Q_LORA = 384
KV_LORA = 256
GLA_HEADS = 4
GLA_DK = 64
GLA_DV = 128
GATE_RANK = 16
GATE_NORM = 16.0
CHUNK = 64
D_FF = 2816
ROPE_BASE = 10000.0
EPS = 1e-6
LOG2_E = 1.4426950408889634

LANES = 128
HEAD_PAD = LANES
ROPE_LANE0 = MLA_NOPE
GQK = GLA_HEADS * GLA_DK
GV = GLA_HEADS * GLA_DV
QPAD = MLA_HEADS * HEAD_PAD
VALL = MLA_HEADS * MLA_V
ONES_ROWS = 16
KEY_BLOCK = 1024

W_KR = Q_LORA + KV_LORA
W_GQ = W_KR + MLA_ROPE
W_GF = W_GQ + 2 * GQK + GV
W_GO = W_GF + 2 * GATE_RANK
W_COLS = W_GO + GV

Z_Q = 0
Z_KV = Z_Q + Q_LORA
Z_GQ = Z_KV + KV_LORA
Z_GK = Z_GQ + GQK
Z_GV = Z_GK + GQK
Z_GO = Z_GV + GV
Z_MISC = Z_GO + GV
Z_COLS = Z_MISC + LANES

FF_CHUNK = 256
N_FF_CHUNKS = D_FF // FF_CHUNK
FFN_SUB = 256
FFN_PHASE_ENDS = (3, 7, 10)

GLA_TILE = 256
CHUNKS_PER_TILE = GLA_TILE // CHUNK
ADA_ROWS = 128
INPROJ_SUB = 512
Q_TILE = 256
ATTN_LAT_QUERIES = 512
GLA_STATIC_TILES = 8

VMEM_LIMIT = 56 * 1024 * 1024

_NT = (((1,), (1,)), ((), ()))


def _rms(x, w):
    return x * lax.rsqrt(jnp.mean(x * x, axis=-1, keepdims=True) + EPS) * w


def _sigmoid(x):
    return 1.0 / (1.0 + jnp.exp(-x))


def _log_sigmoid(x):
    return jnp.minimum(x, 0.0) - jnp.log1p(jnp.exp(-jnp.abs(x)))


def _interleave(chains):
    pending, active = list(chains), []
    while pending or active:
        if pending:
            active.append(pending.pop(0))
        for chain in list(active):
            try:
                next(chain)
            except StopIteration:
                active.remove(chain)


def _const_spec(shape):
    nd = len(shape)
    return pl.BlockSpec(shape, lambda *_: (0,) * nd, pipeline_mode=pl.Buffered(1))


def _mod_rows(mod_ref, r):
    return [mod_ref[k, pl.ds(r, 1), :] for k in range(mod_ref.shape[0])]


def _ada_kernel(cctx_ref, c_ref, w_ref, b_ref, o_ref):
    k = pl.program_id(0)
    d = o_ref.shape[2]
    row = lax.broadcasted_iota(jnp.int32, (8, cctx_ref.shape[1]), 0)
    cond = jnp.where(row == 0, cctx_ref[...], 0.0)
    for r in range(c_ref.shape[0]):
        cond = jnp.where(row == 1 + r, c_ref[r:r + 1, :], cond)
    s = (cond * _sigmoid(cond)).astype(BF16)
    part = jnp.dot(s, w_ref[...].astype(BF16), preferred_element_type=F32)
    for j in range(o_ref.shape[0]):
        sl = slice(j * d, (j + 1) * d)

        @pl.when(k == 0)
        def _():
            o_ref[j] = part[:, sl] + b_ref[:, sl]

        @pl.when(k > 0)
        def _():
            o_ref[j] += part[:, sl]


def _ada(c_ctx, c, w_ada, b_ada):
    d = w_ada.shape[0]
    n = w_ada.shape[1]
    assert 1 + c.shape[0] <= 8
    return pl.pallas_call(
        _ada_kernel,
        out_shape=jax.ShapeDtypeStruct((n // d, 8, d), F32),
        grid=(d // ADA_ROWS,),
        in_specs=[pl.BlockSpec((1, ADA_ROWS), lambda k: (0, k)),
                  pl.BlockSpec((c.shape[0], ADA_ROWS), lambda k: (0, k)),
                  pl.BlockSpec((ADA_ROWS, n), lambda k: (k, 0)),
                  pl.BlockSpec((1, n), lambda k: (0, 0))],
        out_specs=pl.BlockSpec((n // d, 8, d), lambda k: (0, 0, 0)),
        name="ada_mod",
        compiler_params=pltpu.CompilerParams(dimension_semantics=("arbitrary",)),
    )(c_ctx.reshape(1, d), c, w_ada, b_ada.reshape(1, n))


def _prep_kernel(wint_ref, wuq_ref, wukv_ref, wgf_ref, wgb_ref, win_o, wuq_o, wk_o, wvt_o, wg_o):
    cols = wint_ref.shape[1]
    for dst, src, n in ((Z_Q, 0, W_KR), (Z_GQ, W_GQ, W_GF - W_GQ), (Z_GO, W_GO, GV)):
        win_o[:, dst:dst + n] = wint_ref[src:src + n, :].T.astype(BF16)
    z32 = jnp.zeros((32, cols), F32)
    misc_t = jnp.concatenate([wint_ref[W_GF:W_GO, :], z32, wint_ref[W_KR:W_GQ, :], z32], axis=0)
    win_o[:, Z_MISC:Z_COLS] = misc_t.T.astype(BF16)

    u = wuq_ref[...]
    zq = jnp.zeros((u.shape[0], HEAD_PAD - MLA_QK), F32)
    for hd in range(MLA_HEADS):
        blk = jnp.concatenate([u[:, hd * MLA_QK:(hd + 1) * MLA_QK], zq], axis=1)
        wuq_o[:, hd * HEAD_PAD:(hd + 1) * HEAD_PAD] = blk.astype(BF16)

    @pl.when(pl.program_id(0) == 0)
    def _():
        kv = wukv_ref[...]
        per = MLA_NOPE + MLA_V
        lane = lax.broadcasted_iota(jnp.int32, (kv.shape[0], per), 1)
        for hd in range(MLA_HEADS):
            blk = kv[:, hd * per:(hd + 1) * per]
            wk_o[:, hd * HEAD_PAD:(hd + 1) * HEAD_PAD] = jnp.where(lane < MLA_NOPE, blk, 0.0).astype(BF16)
        wv = jnp.concatenate([kv[:, hd * per + MLA_NOPE:(hd + 1) * per] for hd in range(MLA_HEADS)], axis=1)
        wvt_o[...] = wv.T.astype(BF16)

        wg_o[...] = jnp.zeros(wg_o.shape, BF16)
        wg_o[0:GATE_RANK, 0:GQK] = wgf_ref[...].astype(BF16)
        wg_o[GATE_RANK:2 * GATE_RANK, GQK:2 * GQK] = wgb_ref[...].astype(BF16)


def _prep_in_weights(w_in, w_uq, w_ukv, w_gate_f, w_gate_b):
    d = w_in.shape[1]
    steps = 4
    w_in_t = jnp.swapaxes(w_in, 1, 2)
    rb3 = lambda r, c: pl.BlockSpec((None, r // steps, c), lambda i: (0, i, 0))
    rb = lambda r, c: pl.BlockSpec((r // steps, c), lambda i: (i, 0))
    full3 = lambda shape: pl.BlockSpec((None,) + tuple(shape[1:]), lambda i: (0, 0, 0))
    full = lambda shape: pl.BlockSpec(shape, lambda i: (0, 0))
    return pl.pallas_call(
        _prep_kernel,
        out_shape=[jax.ShapeDtypeStruct((d, Z_COLS), BF16),
                   jax.ShapeDtypeStruct((Q_LORA, QPAD), BF16),
                   jax.ShapeDtypeStruct((KV_LORA, QPAD), BF16),
                   jax.ShapeDtypeStruct((VALL, KV_LORA), BF16),
                   jax.ShapeDtypeStruct((LANES, 2 * GQK), BF16)],
        grid=(steps,),
        in_specs=[pl.BlockSpec((None, W_COLS, d // steps), lambda i: (0, 0, i)),
                  rb3(Q_LORA, MLA_HEADS * MLA_QK), full3(w_ukv.shape),
                  full3(w_gate_f.shape), full3(w_gate_b.shape)],
        out_specs=[rb(d, Z_COLS), rb(Q_LORA, QPAD), full((KV_LORA, QPAD)), full((VALL, KV_LORA)),
                   full((LANES, 2 * GQK))],
        name="weight_prep",
        compiler_params=pltpu.CompilerParams(dimension_semantics=("arbitrary",)),
    )(w_in_t, w_uq, w_ukv, w_gate_f, w_gate_b)


def _inproj_kernel(*refs, latent, mod_row):
    (x_ref, mod_ref, nw_ref, win_ref, qn_ref, wuq_ref, kvn_ref, wk_ref, wvt_ref, wg_ref, bgf_ref, bgb_ref) = refs[:12]
    if latent:
        cos_ref, sa_ref, sb_ref = refs[12:15]
        q_ref, k_ref, vt_ref, gq_ref, gk_ref, gv_ref, gf_ref, gb_ref, go_ref = refs[15:]
    else:
        (gn_ref, attn_ref, gla_ref, sf_ref, sb_ref, ckv_ref, krt_ref, q_ref, k_ref, vt_ref, st_ref, p_ref,
         gq_ref, gk_ref, gv_ref, gf_ref, gb_ref, go_ref) = refs[12:30]
        gla_scratch = refs[30:]
    seq = q_ref.shape[1]

    sh1, sc1 = _mod_rows(mod_ref, mod_row(pl.program_id(0)))[:2]
    scale = MLA_QK ** -0.5 * LOG2_E
    lane = lax.broadcasted_iota(jnp.int32, (INPROJ_SUB, LANES), 1)
    in_rope = (lane >= ROPE_LANE0) & (lane < ROPE_LANE0 + MLA_ROPE)

    def sub_tile(r0):
        rows = slice(r0, r0 + INPROJ_SUB)
        h = (_rms(x_ref[rows, :], nw_ref[...]) * (1.0 + sc1) + sh1).astype(BF16)
        yield
        z_all = jnp.dot(h, win_ref[...], preferred_element_type=F32)
        z = lambda lo, n: z_all[:, lo:lo + n]
        yield
        qn = _rms(z(Z_Q, Q_LORA), qn_ref[...]).astype(BF16)
        ckv = _rms(z(Z_KV, KV_LORA), kvn_ref[...])
        ckv_b = ckv.astype(BF16)
        misc = z(Z_MISC, LANES)
        yield
        q = jnp.dot(qn, wuq_ref[...], preferred_element_type=F32)
        kn = jnp.dot(ckv_b, wk_ref[...], preferred_element_type=F32)
        vt_ref[:, rows] = lax.dot_general(wvt_ref[...], ckv_b, _NT,
                                          preferred_element_type=F32).astype(BF16)
        gpre = jnp.dot(misc.astype(BF16), wg_ref[...], preferred_element_type=F32)
        yield
        if latent:
            cos, sa, sb = cos_ref[rows, :], sa_ref[rows, :], sb_ref[rows, :]

            def rope(t):
                return t * cos + pltpu.roll(t, LANES - 8, 1) * sa + pltpu.roll(t, 8, 1) * sb
        else:
            def rope(t):
                return t

        def put(ref, val, cols=slice(None)):
            if len(ref.shape) == 2:
                ref[rows, cols] = val
            else:
                for b in range(INPROJ_SUB // seq):
                    ref[r0 // seq + b, :, cols] = val[b * seq:(b + 1) * seq, :]

        krope = rope(misc)
        for hd in range(MLA_HEADS):
            sl = slice(hd * HEAD_PAD, (hd + 1) * HEAD_PAD)
            put(q_ref, (rope(q[:, sl]) * scale).astype(BF16), sl)
            put(k_ref, jnp.where(in_rope, krope, kn[:, sl]).astype(BF16), sl)
        put(gq_ref, z(Z_GQ, GQK))
        put(gk_ref, z(Z_GK, GQK))
        put(gv_ref, z(Z_GV, GV).astype(BF16))
        put(go_ref, z(Z_GO, GV))
        put(gf_ref, _log_sigmoid(gpre[:, :GQK] + bgf_ref[...]) * (1.0 / GATE_NORM))
        put(gb_ref, _log_sigmoid(gpre[:, GQK:] + bgb_ref[...]) * (1.0 / GATE_NORM))
        if not latent:
            ckv_ref[rows, :] = ckv
            misc_t = misc.T
            n = krt_ref.shape[2]
            for b in range(INPROJ_SUB // n):
                krt_ref[r0 // n + b] = misc_t[ROPE_LANE0:ROPE_LANE0 + MLA_ROPE, b * n:(b + 1) * n]

    _interleave([sub_tile(r0) for r0 in range(0, x_ref.shape[0], INPROJ_SUB)])
    if not latent:
        n_seqs = q_ref.shape[0]
        _gla_body((gq_ref, gk_ref, gv_ref, gf_ref, gb_ref, go_ref), None, gn_ref,
                  (gla_ref, sf_ref, sb_ref), gla_scratch, seq // GLA_TILE, n_seqs,
                  [_attn_pipeline(q_ref, k_ref, vt_ref, attn_ref, st_ref, p_ref, None, [b], 2 * b)
                   for b in range(n_seqs)])


def _inproj(x2d, mod, mod_row, weights, rope_tabs, tm, seq_len, gla_norm=None):
    n_tok, d = x2d.shape
    latent = rope_tabs is not None
    tiles_per_seq = max(seq_len // tm, 1)
    nw, win, qn, wuq, kvn, wk, wvt, wg, bgf, bgb = weights
    row = lambda i: (i, 0)
    in_specs = [pl.BlockSpec((tm, d), row), _const_spec(mod.shape),
                _const_spec(nw.shape), _const_spec(win.shape), _const_spec(qn.shape),
                _const_spec(wuq.shape), _const_spec(kvn.shape), _const_spec(wk.shape),
                _const_spec(wvt.shape), _const_spec(wg.shape), _const_spec(bgf.shape),
                _const_spec(bgb.shape)]
    args = [x2d, mod, nw, win, qn, wuq, kvn, wk, wvt, wg, bgf, bgb]
    if latent:
        tab = pl.BlockSpec((tm, LANES), lambda i: (i % tiles_per_seq, 0))
        in_specs += [tab, tab, tab]
        args += list(rope_tabs)
    per_token = lambda c, dt: (jax.ShapeDtypeStruct((n_tok, c), dt), pl.BlockSpec((tm, c), row))
    seq = tm if latent else seq_len
    per_seq = lambda r, c, dt: (jax.ShapeDtypeStruct((n_tok // seq, r, c), dt),
                                pl.BlockSpec((tm // seq, r, c), lambda i: (i, 0, 0)))
    gla_outs = [per_token(GQK, F32), per_token(GQK, F32), per_token(GV, BF16),
                per_token(GQK, F32), per_token(GQK, F32), per_token(GV, F32)]
    scratch = []
    if latent:
        outs = [per_seq(seq, QPAD, BF16), per_seq(seq, QPAD, BF16),
                (jax.ShapeDtypeStruct((VALL, n_tok), BF16), pl.BlockSpec((VALL, tm), lambda i: (0, i)))]
        outs += gla_outs
    else:
        in_specs.append(_const_spec(gla_norm.shape))
        args.append(gla_norm)
        state = (jax.ShapeDtypeStruct((n_tok // seq, GLA_HEADS, GLA_DK, GLA_DV), F32),
                 pl.BlockSpec((tm // seq, GLA_HEADS, GLA_DK, GLA_DV), lambda i: (i, 0, 0, 0)))
        outs = [per_seq(seq, VALL, BF16), per_seq(seq, GV, BF16), state, state,
                per_token(KV_LORA, F32), per_seq(MLA_ROPE, seq, F32)]
        vmem = lambda c, dt: pltpu.VMEM((tm // seq, seq, c), dt)
        scratch = [vmem(QPAD, BF16), vmem(QPAD, BF16), pltpu.VMEM((VALL, tm), BF16),
                   pltpu.VMEM((2 * (tm // seq), seq, Q_TILE), F32),
                   pltpu.VMEM((2 * (tm // seq), seq, Q_TILE), BF16),
                   vmem(GQK, F32), vmem(GQK, F32), vmem(GV, BF16), vmem(GQK, F32), vmem(GQK, F32),
                   vmem(GV, F32)] + _gla_scratch(tm // seq, seq)
    return pl.pallas_call(
        functools.partial(_inproj_kernel, latent=latent, mod_row=mod_row),
        out_shape=[o[0] for o in outs],
        grid=(n_tok // tm,),
        in_specs=in_specs,
        out_specs=[o[1] for o in outs],
        scratch_shapes=scratch,
        name="inproj_lat" if latent else "inproj_ctx",
        compiler_params=pltpu.CompilerParams(dimension_semantics=("arbitrary",),
                                             vmem_limit_bytes=VMEM_LIMIT),
    )(*args)


def _decomp_kernel(ckv_ref, krt_ref, wk_ref, wvt_ref, k_ref, vt_ref):
    ckv_b = ckv_ref[...].astype(BF16)
    kn = jnp.dot(ckv_b, wk_ref[...], preferred_element_type=F32)
    n_keys = krt_ref.shape[1]
    kr = jnp.concatenate([jnp.zeros((ROPE_LANE0, n_keys), F32), krt_ref[...],
                          jnp.zeros((LANES - ROPE_LANE0 - MLA_ROPE, n_keys), F32)], axis=0).T
    lane = lax.broadcasted_iota(jnp.int32, kr.shape, 1)
    in_rope = (lane >= ROPE_LANE0) & (lane < ROPE_LANE0 + MLA_ROPE)
    for hd in range(MLA_HEADS):
        sl = slice(hd * HEAD_PAD, (hd + 1) * HEAD_PAD)
        k_ref[:, sl] = jnp.where(in_rope, kr, kn[:, sl]).astype(BF16)
    vt_ref[...] = lax.dot_general(wvt_ref[...], ckv_b, _NT, preferred_element_type=F32).astype(BF16)


def _decomp(ckv, kr_t, wk, wvt):
    b, s, _ = ckv.shape
    return pl.pallas_call(
        _decomp_kernel,
        out_shape=[jax.ShapeDtypeStruct((b, s, QPAD), BF16), jax.ShapeDtypeStruct((VALL, b * s), BF16)],
        grid=(b,),
        in_specs=[pl.BlockSpec((None, s, KV_LORA), lambda i: (i, 0, 0)),
                  pl.BlockSpec((None, MLA_ROPE, s), lambda i: (i, 0, 0)),
                  _const_spec(wk.shape), _const_spec(wvt.shape)],
        out_specs=[pl.BlockSpec((None, s, QPAD), lambda i: (i, 0, 0)),
                   pl.BlockSpec((VALL, s), lambda i: (0, i))],
        name="ctx_decompress",
        compiler_params=pltpu.CompilerParams(dimension_semantics=("arbitrary",)),
    )(ckv, kr_t, wk, wvt)


def _attn_kernel(*refs, has_ctx, n_seqs, n_side):
    n_in = 5 if has_ctx else 3
    side_in, refs = refs[n_in:n_in + n_side], refs[:n_in] + refs[n_in + n_side:]
    side_out, refs = refs[n_in + 1:n_in + 1 + n_side], refs[:n_in + 1] + refs[n_in + 1 + n_side:]
    if has_ctx:
        q_ref, kc_ref, vct_ref, k_ref, vt_ref, o_ref, st_ref, p_ref = refs
    else:
        q_ref, k_ref, vt_ref, o_ref, st_ref, p_ref = refs

    for src, dst in zip(side_in, side_out):
        dst[...] = src[...].astype(BF16)
    _interleave([_attn_pipeline(q_ref, k_ref, vt_ref, o_ref, st_ref, p_ref,
                                (kc_ref, vct_ref) if has_ctx else None, range(n_seqs))])


def _attn_pipeline(q_ref, k_ref, vt_ref, o_ref, st_ref, p_ref, ctx_refs, seqs, slot0=0):
    tq = Q_TILE

    def key_blocks(bi):
        srcs = [ctx_refs] if ctx_refs is not None else []
        blocks, row0 = [], 0
        for kr, vr in srcs + [(k_ref, vt_ref)]:
            n_keys = kr.shape[1]
            size = min(KEY_BLOCK, n_keys)
            for r in range(0, n_keys, size):
                blocks.append((kr, vr, r, bi * n_keys + r, size, row0))
                row0 += size
        return blocks

    units = [(bi, slice(q0, q0 + tq), hd) for bi in seqs
             for q0 in range(0, q_ref.shape[1], tq) for hd in range(MLA_HEADS)]
    col_max = [None] * len(units)
    pair = []
    for stage in range(len(units) + 2):
        ua, ub, uc = stage, stage - 1, stage - 2
        run_max = None
        acc = jnp.zeros((MLA_V + ONES_ROWS, tq), F32)
        for j in range(len(key_blocks(0))):
            if ua < len(units):
                bi, qrows, hd = units[ua]
                kr, _, r0, _, size, srow = key_blocks(bi)[j]
                sl = slice(hd * HEAD_PAD, (hd + 1) * HEAD_PAD)
                st = lax.dot_general(kr[bi, r0:r0 + size, sl], q_ref[bi, qrows, sl], _NT,
                                     preferred_element_type=F32)
                st_ref[slot0 + ua % 2, srow:srow + size, :] = st
                blk_max = jnp.max(st.reshape(size // 8, 8, tq), axis=0)
                run_max = blk_max if run_max is None else jnp.maximum(run_max, blk_max)
            if 0 <= ub < len(units):
                _, _, _, _, size, srow = key_blocks(0)[j]
                p_ref[slot0 + ub % 2, srow:srow + size, :] = jnp.exp2(
                    st_ref[slot0 + ub % 2, srow:srow + size, :] - col_max[ub]).astype(BF16)
            if uc >= 0:
                bi, _, hd = units[uc]
                _, vr, _, c0, size, srow = key_blocks(bi)[j]
                v_aug = jnp.concatenate([vr[hd * MLA_V:(hd + 1) * MLA_V, c0:c0 + size],
                                         jnp.ones((ONES_ROWS, size), BF16)], axis=0)
                acc = acc + jnp.dot(v_aug, p_ref[slot0 + uc % 2, srow:srow + size, :],
                                    preferred_element_type=F32)
        if ua < len(units):
            col_max[ua] = jnp.max(run_max, axis=0, keepdims=True)
        if uc >= 0:
            bi, qrows, hd = units[uc]
            pair.append(acc[:MLA_V, :] / acc[MLA_V:MLA_V + 1, :])
            if len(pair) == 2:
                o_ref[bi, qrows, (hd - 1) * MLA_V:(hd + 1) * MLA_V] = (
                    jnp.concatenate(pair, axis=0).T.astype(BF16))
                pair = []
        yield


def _attention(q, k, vt, ctx_kv, tq, n_seqs, side_weights=()):
    b, t, _ = q.shape
    steps = (b // n_seqs) * (t // tq)
    assert (n_seqs == 1 or tq == t) and tq % Q_TILE == 0
    has_ctx = ctx_kv is not None
    in_specs = [pl.BlockSpec((n_seqs, tq, QPAD), lambda i, j: (i, j, 0))]
    args = [q]
    if has_ctx:
        kc, vct = ctx_kv
        s = kc.shape[1]
        in_specs += [pl.BlockSpec((n_seqs, s, QPAD), lambda i, j: (i, 0, 0)),
                     pl.BlockSpec((VALL, n_seqs * s), lambda i, j: (0, i))]
        args += [kc, vct]
    in_specs += [pl.BlockSpec((n_seqs, t, QPAD), lambda i, j: (i, 0, 0)),
                 pl.BlockSpec((VALL, n_seqs * t), lambda i, j: (0, i))]
    args += [k, vt]
    out_shape = [jax.ShapeDtypeStruct((b, t, VALL), BF16)]
    out_specs = [pl.BlockSpec((n_seqs, tq, VALL), lambda i, j: (i, j, 0))]
    nj = t // tq
    for w in side_weights:
        _, rows, cols = w.shape
        assert rows % (16 * steps) == 0
        in_specs.append(pl.BlockSpec((None, rows // steps, cols), lambda i, j: (0, i * nj + j, 0)))
        args.append(w)
        out_shape.append(jax.ShapeDtypeStruct((rows, cols), BF16))
        out_specs.append(pl.BlockSpec((rows // steps, cols), lambda i, j: (i * nj + j, 0)))
    return pl.pallas_call(
        functools.partial(_attn_kernel, has_ctx=has_ctx, n_seqs=n_seqs, n_side=len(side_weights)),
        out_shape=out_shape,
        grid=(b // n_seqs, t // tq),
        in_specs=in_specs,
        out_specs=out_specs,
        scratch_shapes=[pltpu.VMEM((2, t + (s if has_ctx else 0), Q_TILE), F32),
                        pltpu.VMEM((2, t + (s if has_ctx else 0), Q_TILE), BF16)],
        name="mla_attn_lat" if has_ctx else "mla_attn_ctx",
        compiler_params=pltpu.CompilerParams(dimension_semantics=("arbitrary", "arbitrary"),
                                             vmem_limit_bytes=VMEM_LIMIT),
    )(*args)


def _gla_kernel(*refs, n_tiles, n_seqs, zero_init):
    init_refs, rest = (None, refs[6:]) if zero_init else (refs[6:8], refs[8:])
    _gla_body(refs[:6], init_refs, rest[0], rest[1:4], rest[4:], n_tiles, n_seqs)


def _gla_scratch(n_seqs, t):
    return [pltpu.VMEM((2, n_seqs, t, GV), F32),
            pltpu.VMEM((GQK, GLA_TILE), BF16),
            pltpu.VMEM((2, GLA_TILE, GLA_TILE), F32),
            pltpu.VMEM((GLA_HEADS * GLA_TILE, GQK), BF16)]


def _gla_body(in_refs, init_refs, gn_ref, out_refs, scratch_refs, n_tiles, n_seqs, side_chains=()):
    gq_ref, gk_ref, gv_ref, gf_ref, gb_ref, go_ref = in_refs
    o_ref, sf_ref, sb_ref = out_refs
    oacc_ref, bdqk_ref, tri_ref, hm_ref = scratch_refs
    g_refs = (gf_ref, gb_ref)
    state_refs = (sf_ref, sb_ref)

    @pl.when(pl.program_id(0) == 0)
    def _():
        ri = lax.broadcasted_iota(jnp.int32, (GLA_TILE, GLA_TILE), 0)
        ci = lax.broadcasted_iota(jnp.int32, (GLA_TILE, GLA_TILE), 1)
        same_chunk = (ri // CHUNK) == (ci // CHUNK)
        bdqk_ref[...] = jnp.where(same_chunk, 1.0, 0.0).astype(BF16)
        tri_ref[0] = jnp.where(same_chunk & (ri >= ci), 1.0, 0.0)
        tri_ref[1] = jnp.where(same_chunk & (ci >= ri), 1.0, 0.0)
        hm_ref[...] = jnp.where(
            lax.broadcasted_iota(jnp.int32, (GLA_HEADS * GLA_TILE, GQK), 0) // GLA_TILE
            == lax.broadcasted_iota(jnp.int32, (GLA_HEADS * GLA_TILE, GQK), 1) // GLA_DK,
            1.0, 0.0).astype(BF16)

    row8 = lax.broadcasted_iota(jnp.int32, (8, GQK), 0)

    def row_slice(start, size):
        return pl.ds(start if isinstance(start, int) else pl.multiple_of(start, size), size)

    def tile_rows(t):
        return row_slice(t * GLA_TILE, GLA_TILE)

    finished = set()

    def total_row(c, d):
        return c * CHUNK + (CHUNK - 1 if d == 0 else 0)

    def tile_dir(b, t, d):
        rows = tile_rows(t)
        g = g_refs[d][b, rows, :]
        g_hi = g.astype(BF16)
        g_lo = (g - g_hi.astype(F32)).astype(BF16)
        tri_b = tri_ref[d].astype(BF16)
        cum = (jnp.dot(tri_b, g_hi, preferred_element_type=F32)
               + jnp.dot(tri_b, g_lo, preferred_element_type=F32))
        yield
        totals = [cum[total_row(c, d):total_row(c, d) + 1, :] for c in range(CHUNKS_PER_TILE)]
        tot8 = jnp.zeros((8, GQK), F32)
        for c in range(CHUNKS_PER_TILE):
            tot8 = jnp.where(row8 == c, totals[c], tot8)
        dec_t = jnp.concatenate([jnp.exp(tot8), jnp.zeros((LANES - 8, GQK), F32)], axis=0).T
        q = gq_ref[b, rows, :] * (GLA_DK ** -0.5)
        k = gk_ref[b, rows, :]
        v = gv_ref[b, rows, :]
        tot = jnp.concatenate([jnp.broadcast_to(tc, (CHUNK, GQK)) for tc in totals], axis=0)
        qe = (q * jnp.exp(cum)).astype(BF16)
        ke = (k * jnp.exp(-cum)).astype(BF16)
        kd_t = (k * jnp.exp(tot - cum)).T.astype(BF16)
        bd_qk = bdqk_ref[...] > 0
        tri = tri_ref[d] > 0

        qm = jnp.where(hm_ref[...] > 0, jnp.tile(qe, (GLA_HEADS, 1)), 0.0)
        yield
        att = lax.dot_general(qm, ke, _NT, preferred_element_type=F32)

        yield
        intra, upd = [], []
        for hd in range(GLA_HEADS):
            vh = v[:, hd * GLA_DV:(hd + 1) * GLA_DV]
            a_h = jnp.where(tri, att[hd * GLA_TILE:(hd + 1) * GLA_TILE, :], 0.0).astype(BF16)
            intra.append(jnp.dot(a_h, vh, preferred_element_type=F32))
            kd_h = jnp.tile(kd_t[hd * GLA_DK:(hd + 1) * GLA_DK, :], (CHUNKS_PER_TILE, 1))
            upd.append(jnp.dot(jnp.where(bd_qk, kd_h, 0.0), vh, preferred_element_type=F32))

        yield
        state = [state_refs[d][b, hd] for hd in range(GLA_HEADS)]
        order = range(CHUNKS_PER_TILE) if d == 0 else range(CHUNKS_PER_TILE - 1, -1, -1)
        seen = {}
        for c in order:
            seen[c] = jnp.concatenate(state, axis=0).astype(BF16)
            decay = jnp.broadcast_to(dec_t[:, c:c + 1], (GQK, GLA_DV))
            for hd in range(GLA_HEADS):
                ks = slice(hd * GLA_DK, (hd + 1) * GLA_DK)
                state[hd] = decay[ks, :] * state[hd] + upd[hd][c * CHUNK:(c + 1) * CHUNK, :]
        for hd in range(GLA_HEADS):
            state_refs[d][b, hd] = state[hd]

        yield
        for c in range(CHUNKS_PER_TILE):
            cr = slice(c * CHUNK, (c + 1) * CHUNK)
            q_c = jnp.concatenate([qm[hd * GLA_TILE + c * CHUNK:hd * GLA_TILE + (c + 1) * CHUNK, :]
                                   for hd in range(GLA_HEADS)], axis=0)
            inter = jnp.dot(q_c, seen[c], preferred_element_type=F32)
            o = jnp.concatenate([intra[hd][cr, :] + inter[hd * CHUNK:(hd + 1) * CHUNK, :]
                                 for hd in range(GLA_HEADS)], axis=1)
            oacc_ref[d, b, row_slice(t * GLA_TILE + c * CHUNK, CHUNK), :] = o
        if isinstance(t, int):
            finished.add((b, t, d))

    if init_refs is None:
        sf_ref[...] = jnp.zeros(sf_ref.shape, F32)
        sb_ref[...] = jnp.zeros(sb_ref.shape, F32)
    else:
        sf_ref[...] = init_refs[0][...]
        sb_ref[...] = init_refs[1][...]

    gn = gn_ref[...]

    def epilogue(t):
        while isinstance(t, int) and not all((b, t, d) in finished for b in range(n_seqs) for d in range(2)):
            yield
        rows = tile_rows(t)
        for b in range(n_seqs):
            for hd in range(GLA_HEADS):
                vs = slice(hd * GLA_DV, (hd + 1) * GLA_DV)
                o = _rms(oacc_ref[0, b, rows, vs] + oacc_ref[1, b, rows, vs], gn)
                go = go_ref[b, rows, vs]
                o_ref[b, rows, vs] = (o * (go * _sigmoid(go))).astype(BF16)
        return
        yield

    def main_chains(first_tile, n):
        chains = []
        for t in [first_tile + u for u in range(n)]:
            for b in range(n_seqs):
                chains += [tile_dir(b, t, 0), tile_dir(b, n_tiles - 1 - t, 1)]
        return chains

    if n_tiles <= GLA_STATIC_TILES:
        _interleave(list(side_chains) + main_chains(0, n_tiles) + [epilogue(t) for t in range(n_tiles)])
    else:
        assert not side_chains
        per_step = next(c for c in (4, 2, 1) if n_tiles % c == 0)

        def main_body(i, carry):
            _interleave(main_chains(i * per_step, per_step))
            return carry

        lax.fori_loop(0, n_tiles // per_step, main_body, 0)

        def epilogue_body(t, carry):
            _interleave([epilogue(t)])
            return carry

        lax.fori_loop(0, n_tiles, epilogue_body, 0)


def _gla(gq, gk, gv, gf, gb, go, init_states, gn, n_seqs):
    b, t, _ = gq.shape
    n_tiles = t // GLA_TILE
    zero_init = init_states is None
    seq = lambda c: pl.BlockSpec((n_seqs, t, c), lambda i: (i, 0, 0))
    st = pl.BlockSpec((n_seqs, GLA_HEADS, GLA_DK, GLA_DV), lambda i: (i, 0, 0, 0))
    in_specs = [seq(GQK), seq(GQK), seq(GV), seq(GQK), seq(GQK), seq(GV)]
    args = [gq, gk, gv, gf, gb, go]
    if not zero_init:
        in_specs += [st, st]
        args += list(init_states)
    in_specs.append(_const_spec(gn.shape))
    args.append(gn)
    return pl.pallas_call(
        functools.partial(_gla_kernel, n_tiles=n_tiles, n_seqs=n_seqs, zero_init=zero_init),
        out_shape=[jax.ShapeDtypeStruct((b, t, GV), BF16),
                   jax.ShapeDtypeStruct((b, GLA_HEADS, GLA_DK, GLA_DV), F32),
                   jax.ShapeDtypeStruct((b, GLA_HEADS, GLA_DK, GLA_DV), F32)],
        grid=(b // n_seqs,),
        in_specs=in_specs,
        out_specs=[seq(GV), st, st],
        scratch_shapes=_gla_scratch(n_seqs, t),
        name="gla_%d" % t,
        compiler_params=pltpu.CompilerParams(dimension_semantics=("arbitrary",),
                                             vmem_limit_bytes=VMEM_LIMIT),
    )(*args)


def _ffn_kernel(xp_ref, xs_ref, atp_ref, ats_ref, glp_ref, gls_ref, mod_ref, wout_ref, nf_ref,
                wfi_ref, wfo_ref, fn_ref, yp_ref, ys_ref, act_ref, *, ctx_tiles, tiles_per_seq):
    def sub_tile(x_ref, at_ref, gl_ref, y_ref, mod_row, r0):
        rows = slice(r0, r0 + FFN_SUB)
        gt1, sh2, sc2, gt2 = _mod_rows(mod_ref, mod_row)[2:]
        mix = (jnp.dot(at_ref[rows, :], wout_ref[0:VALL, :], preferred_element_type=F32)
               + jnp.dot(gl_ref[rows, :], wout_ref[VALL:, :], preferred_element_type=F32))
        yield
        x1 = x_ref[rows, :] + gt1 * mix
        h2 = (_rms(x1, nf_ref[...]) * (1.0 + sc2) + sh2).astype(BF16)
        yield
        for j in range(N_FF_CHUNKS):
            cs = slice(j * FF_CHUNK, (j + 1) * FF_CHUNK)
            a = jnp.dot(h2, wfi_ref[:, cs], preferred_element_type=F32)
            g = jnp.dot(h2, wfi_ref[:, D_FF + j * FF_CHUNK:D_FF + (j + 1) * FF_CHUNK],
                        preferred_element_type=F32)
            act_ref[rows, cs] = (a * _sigmoid(a) * g).astype(BF16)
            if j in FFN_PHASE_ENDS:
                yield
        ff = jnp.dot(act_ref[rows, :], wfo_ref[...], preferred_element_type=F32)
        yield
        x2 = x1 + gt2 * ff
        y_ref[rows, :] = _rms(x2, fn_ref[...])

    def tile(x_ref, at_ref, gl_ref, y_ref, mod_row):
        _interleave([sub_tile(x_ref, at_ref, gl_ref, y_ref, mod_row, r0)
                     for r0 in range(0, x_ref.shape[0], FFN_SUB)])

    t = pl.program_id(0)

    @pl.when(t < ctx_tiles)
    def _():
        tile(xp_ref, atp_ref, glp_ref, yp_ref, 0)

    @pl.when(t >= ctx_tiles)
    def _():
        tile(xs_ref, ats_ref, gls_ref, ys_ref, 1 + (t - ctx_tiles) // tiles_per_seq)


def _ffn(xp, xs, attn_p, attn_s, gla_p, gla_s, mod, wout, nf, wfi, wfo, fn, tm, tiles_per_seq):
    d = xp.shape[1]
    ctx_tiles = xp.shape[0] // tm
    lat_tiles = xs.shape[0] // tm
    ctx_map = lambda s: (jnp.minimum(s, ctx_tiles - 1), 0)
    lat_map = lambda s: (jnp.maximum(s - ctx_tiles, 0), 0)
    tile = lambda c, m: pl.BlockSpec((tm, c), m)
    return pl.pallas_call(
        functools.partial(_ffn_kernel, ctx_tiles=ctx_tiles, tiles_per_seq=tiles_per_seq),
        out_shape=[jax.ShapeDtypeStruct(xp.shape, F32), jax.ShapeDtypeStruct(xs.shape, F32)],
        grid=(ctx_tiles + lat_tiles,),
        in_specs=[tile(d, ctx_map), tile(d, lat_map), tile(VALL, ctx_map), tile(VALL, lat_map),
                  tile(GV, ctx_map), tile(GV, lat_map), _const_spec(mod.shape),
                  _const_spec(wout.shape), _const_spec(nf.shape), _const_spec(wfi.shape),
                  _const_spec(wfo.shape), _const_spec(fn.shape)],
        out_specs=[tile(d, ctx_map), tile(d, lat_map)],
        scratch_shapes=[pltpu.VMEM((tm, D_FF), BF16)],
        name="out_ffn",
        compiler_params=pltpu.CompilerParams(dimension_semantics=("arbitrary",),
                                             vmem_limit_bytes=VMEM_LIMIT),
    )(xp, xs, attn_p, attn_s, gla_p, gla_s, mod, wout, nf, wfi, wfo, fn)


def _rope_tables(n_tokens):
    t = np.arange(n_tokens)
    row = (t // GRID_W).astype(np.float32)
    col = (t % GRID_W).astype(np.float32)
    half = MLA_ROPE // 2
    inv = (np.float32(ROPE_BASE) ** (-np.arange(0, half, 2, dtype=np.float32) / np.float32(half))).astype(np.float32)
    ang_r = row[:, None] * inv
    ang_c = col[:, None] * inv
    ang = np.concatenate([ang_r, ang_r, ang_c, ang_c], axis=-1).astype(np.float32)
    cos, sin = np.cos(ang), np.sin(ang)
    first = (np.arange(MLA_ROPE) % half) < (half // 2)
    cos_t = np.ones((n_tokens, LANES), np.float32)
    sa_t = np.zeros((n_tokens, LANES), np.float32)
    sb_t = np.zeros((n_tokens, LANES), np.float32)
    cos_t[:, ROPE_LANE0:ROPE_LANE0 + MLA_ROPE] = cos
    sa_t[:, ROPE_LANE0:ROPE_LANE0 + MLA_ROPE] = np.where(first, -sin, 0.0)
    sb_t[:, ROPE_LANE0:ROPE_LANE0 + MLA_ROPE] = np.where(first, 0.0, sin)
    return jnp.asarray(cos_t), jnp.asarray(sa_t), jnp.asarray(sb_t)


def kernel(x_prompt, x_sample, cache_kv_latent, cache_k_rope, state_gla_fwd, state_gla_bwd, c, c_ctx, w_ada, b_ada, norm_attn, w_in, mla_q_norm, w_uq, mla_kv_norm, w_ukv, w_gate_f, b_gate_f, w_gate_b, b_gate_b, gla_norm, w_out, norm_ffn, w_ffn_in, w_ffn_out, final_norm):
    batch, seq, d = x_prompt.shape
    dec_batch, dec_seq, _ = x_sample.shape
    assert w_ada.shape[0] == 1 and w_in.shape[-1] == W_COLS and w_ffn_in.shape[-1] == 2 * D_FF
    l = 0

    mod = _ada(c_ctx, c, w_ada[l], b_ada[l])

    win, wuq, wk, wvt, wg = _prep_in_weights(w_in, w_uq, w_ukv, w_gate_f, w_gate_b)
    in_w = (norm_attn[l].reshape(1, d), win, mla_q_norm[l].reshape(1, Q_LORA), wuq,
            mla_kv_norm[l].reshape(1, KV_LORA), wk, wvt, wg, b_gate_f, b_gate_b)
    gn = gla_norm[l].reshape(1, GLA_DV)
    tm, tm_ffn = 512, 512
    r3 = lambda a, b_, t: a.reshape(b_, t, a.shape[-1])

    xp = x_prompt.reshape(batch * seq, d)
    (attn_p, gla_p, sf, sb, ckv, kr_t) = _inproj(xp, mod, lambda i: 0, in_w, None, tm, seq, gn)

    xs = x_sample.reshape(dec_batch * dec_seq, d)
    tiles = dec_seq // tm
    (q, k, vt, gq, gk, gv, gf, gb, go) = _inproj(xs, mod, lambda i: 1 + i // tiles, in_w,
                                                  _rope_tables(dec_seq), tm, dec_seq)
    kc, vct = _decomp(cache_kv_latent[:, l], jnp.swapaxes(cache_k_rope[:, l], 1, 2), wk, wvt)
    attn_s, wout, wfi, wfo = _attention(r3(q, dec_batch, dec_seq), r3(k, dec_batch, dec_seq), vt, (kc, vct),
                                        ATTN_LAT_QUERIES, 1, (w_out, w_ffn_in, w_ffn_out))
    gla_s, _, _ = _gla(r3(gq, dec_batch, dec_seq), r3(gk, dec_batch, dec_seq), r3(gv, dec_batch, dec_seq),
                       r3(gf, dec_batch, dec_seq), r3(gb, dec_batch, dec_seq), r3(go, dec_batch, dec_seq),
                       (state_gla_fwd[:, l].astype(F32), state_gla_bwd[:, l].astype(F32)), gn, 1)

    flat = lambda a: a.reshape(-1, a.shape[-1])
    y_prompt, y_sample = _ffn(xp, xs, flat(attn_p), flat(attn_s), flat(gla_p), flat(gla_s), mod,
                              wout, norm_ffn[l].reshape(1, d), wfi, wfo,
                              final_norm.reshape(1, d), tm_ffn, dec_seq // tm_ffn)
    y_prompt = y_prompt.reshape(batch, seq, d)
    y_sample = y_sample.reshape(dec_batch, dec_seq, d)

    new_kv_latent = ckv.reshape(batch, 1, seq, KV_LORA)
    new_k_rope = jnp.swapaxes(kr_t, 1, 2).reshape(batch, 1, seq, MLA_ROPE)
    new_state_fwd = sf.reshape(batch, 1, GLA_HEADS, GLA_DK, GLA_DV).astype(x_prompt.dtype)
    new_state_bwd = sb.reshape(batch, 1, GLA_HEADS, GLA_DK, GLA_DV).astype(x_prompt.dtype)
    return (y_prompt, y_sample, new_kv_latent, new_k_rope, new_state_fwd, new_state_bwd)
```

```python
import functools

import numpy as np
import jax
import jax.numpy as jnp
from jax import lax
from jax.experimental import pallas as pl
from jax.experimental.pallas import tpu as pltpu

F32 = jnp.float32
BF16 = jnp.bfloat16

GRID_W = 64
MLA_HEADS = 8
MLA_NOPE = 64
MLA_ROPE = 32
MLA_QK = MLA_NOPE + MLA_ROPE
MLA_V = 64
Q_LORA = 384
KV_LORA = 256
GLA_HEADS = 4
GLA_DK = 64
GLA_DV = 128
GATE_RANK = 16
GATE_NORM = 16.0
CHUNK = 64
D_FF = 2816
ROPE_BASE = 10000.0
EPS = 1e-6
LOG2_E = 1.4426950408889634

LANES = 128
HEAD_PAD = LANES
ROPE_LANE0 = MLA_NOPE
GQK = GLA_HEADS * GLA_DK
GV = GLA_HEADS * GLA_DV
QPAD = MLA_HEADS * HEAD_PAD
VALL = MLA_HEADS * MLA_V
ONES_ROWS = 16
KEY_BLOCK = 1024

W_KR = Q_LORA + KV_LORA
W_GQ = W_KR + MLA_ROPE
W_GF = W_GQ + 2 * GQK + GV
W_GO = W_GF + 2 * GATE_RANK
W_COLS = W_GO + GV

Z_Q = 0
Z_KV = Z_Q + Q_LORA
Z_GQ = Z_KV + KV_LORA
Z_GK = Z_GQ + GQK
Z_GV = Z_GK + GQK
Z_GO = Z_GV + GV
Z_MISC = Z_GO + GV
Z_COLS = Z_MISC + LANES

FF_CHUNK = 256
N_FF_CHUNKS = D_FF // FF_CHUNK
FFN_SUB = 256
FFN_PHASE_ENDS = (3, 7, 10)

GLA_TILE = 256
CHUNKS_PER_TILE = GLA_TILE // CHUNK
ADA_ROWS = 128
INPROJ_SUB = 512
Q_TILE = 256
ATTN_LAT_QUERIES = 512
GLA_STATIC_TILES = 8

VMEM_LIMIT = 56 * 1024 * 1024

_NT = (((1,), (1,)), ((), ()))


def _rms(x, w):
    return x * lax.rsqrt(jnp.mean(x * x, axis=-1, keepdims=True) + EPS) * w


def _sigmoid(x):
    return 1.0 / (1.0 + jnp.exp(-x))


def _log_sigmoid(x):
    return jnp.minimum(x, 0.0) - jnp.log1p(jnp.exp(-jnp.abs(x)))


def _interleave(chains):
    pending, active = list(chains), []
    while pending or active:
        if pending:
            active.append(pending.pop(0))
        for chain in list(active):
            try:
                next(chain)
            except StopIteration:
                active.remove(chain)


def _const_spec(shape):
    nd = len(shape)
    return pl.BlockSpec(shape, lambda *_: (0,) * nd, pipeline_mode=pl.Buffered(1))


def _mod_rows(mod_ref, r):
    return [mod_ref[k, pl.ds(r, 1), :] for k in range(mod_ref.shape[0])]


def _ada_kernel(cctx_ref, c_ref, w_ref, b_ref, o_ref):
    k = pl.program_id(0)
    d = o_ref.shape[2]
    row = lax.broadcasted_iota(jnp.int32, (8, cctx_ref.shape[1]), 0)
    cond = jnp.where(row == 0, cctx_ref[...], 0.0)
    for r in range(c_ref.shape[0]):
        cond = jnp.where(row == 1 + r, c_ref[r:r + 1, :], cond)
    s = (cond * _sigmoid(cond)).astype(BF16)
    part = jnp.dot(s, w_ref[...].astype(BF16), preferred_element_type=F32)
    for j in range(o_ref.shape[0]):
        sl = slice(j * d, (j + 1) * d)

        @pl.when(k == 0)
        def _():
            o_ref[j] = part[:, sl] + b_ref[:, sl]

        @pl.when(k > 0)
        def _():
            o_ref[j] += part[:, sl]


def _ada(c_ctx, c, w_ada, b_ada):
    d = w_ada.shape[0]
    n = w_ada.shape[1]
    assert 1 + c.shape[0] <= 8
    return pl.pallas_call(
        _ada_kernel,
        out_shape=jax.ShapeDtypeStruct((n // d, 8, d), F32),
        grid=(d // ADA_ROWS,),
        in_specs=[pl.BlockSpec((1, ADA_ROWS), lambda k: (0, k)),
                  pl.BlockSpec((c.shape[0], ADA_ROWS), lambda k: (0, k)),
                  pl.BlockSpec((ADA_ROWS, n), lambda k: (k, 0)),
                  pl.BlockSpec((1, n), lambda k: (0, 0))],
        out_specs=pl.BlockSpec((n // d, 8, d), lambda k: (0, 0, 0)),
        name="ada_mod",
        compiler_params=pltpu.CompilerParams(dimension_semantics=("arbitrary",)),
    )(c_ctx.reshape(1, d), c, w_ada, b_ada.reshape(1, n))


def _prep_kernel(wint_ref, wuq_ref, wukv_ref, wgf_ref, wgb_ref, win_o, wuq_o, wk_o, wvt_o, wg_o):
    cols = wint_ref.shape[1]
    for dst, src, n in ((Z_Q, 0, W_KR), (Z_GQ, W_GQ, W_GF - W_GQ), (Z_GO, W_GO, GV)):
        win_o[:, dst:dst + n] = wint_ref[src:src + n, :].T.astype(BF16)
    z32 = jnp.zeros((32, cols), F32)
    misc_t = jnp.concatenate([wint_ref[W_GF:W_GO, :], z32, wint_ref[W_KR:W_GQ, :], z32], axis=0)
    win_o[:, Z_MISC:Z_COLS] = misc_t.T.astype(BF16)

    u = wuq_ref[...]
    zq = jnp.zeros((u.shape[0], HEAD_PAD - MLA_QK), F32)
    for hd in range(MLA_HEADS):
        blk = jnp.concatenate([u[:, hd * MLA_QK:(hd + 1) * MLA_QK], zq], axis=1)
        wuq_o[:, hd * HEAD_PAD:(hd + 1) * HEAD_PAD] = blk.astype(BF16)

    @pl.when(pl.program_id(0) == 0)
    def _():
        kv = wukv_ref[...]
        per = MLA_NOPE + MLA_V
        lane = lax.broadcasted_iota(jnp.int32, (kv.shape[0], per), 1)
        for hd in range(MLA_HEADS):
            blk = kv[:, hd * per:(hd + 1) * per]
            wk_o[:, hd * HEAD_PAD:(hd + 1) * HEAD_PAD] = jnp.where(lane < MLA_NOPE, blk, 0.0).astype(BF16)
        wv = jnp.concatenate([kv[:, hd * per + MLA_NOPE:(hd + 1) * per] for hd in range(MLA_HEADS)], axis=1)
        wvt_o[...] = wv.T.astype(BF16)

        wg_o[...] = jnp.zeros(wg_o.shape, BF16)
        wg_o[0:GATE_RANK, 0:GQK] = wgf_ref[...].astype(BF16)
        wg_o[GATE_RANK:2 * GATE_RANK, GQK:2 * GQK] = wgb_ref[...].astype(BF16)


def _prep_in_weights(w_in, w_uq, w_ukv, w_gate_f, w_gate_b):
    d = w_in.shape[1]
    steps = 4
    w_in_t = jnp.swapaxes(w_in, 1, 2)
    rb3 = lambda r, c: pl.BlockSpec((None, r // steps, c), lambda i: (0, i, 0))
    rb = lambda r, c: pl.BlockSpec((r // steps, c), lambda i: (i, 0))
    full3 = lambda shape: pl.BlockSpec((None,) + tuple(shape[1:]), lambda i: (0, 0, 0))
    full = lambda shape: pl.BlockSpec(shape, lambda i: (0, 0))
    return pl.pallas_call(
        _prep_kernel,
        out_shape=[jax.ShapeDtypeStruct((d, Z_COLS), BF16),
                   jax.ShapeDtypeStruct((Q_LORA, QPAD), BF16),
                   jax.ShapeDtypeStruct((KV_LORA, QPAD), BF16),
                   jax.ShapeDtypeStruct((VALL, KV_LORA), BF16),
                   jax.ShapeDtypeStruct((LANES, 2 * GQK), BF16)],
        grid=(steps,),
        in_specs=[pl.BlockSpec((None, W_COLS, d // steps), lambda i: (0, 0, i)),
                  rb3(Q_LORA, MLA_HEADS * MLA_QK), full3(w_ukv.shape),
                  full3(w_gate_f.shape), full3(w_gate_b.shape)],
        out_specs=[rb(d, Z_COLS), rb(Q_LORA, QPAD), full((KV_LORA, QPAD)), full((VALL, KV_LORA)),
                   full((LANES, 2 * GQK))],
        name="weight_prep",
        compiler_params=pltpu.CompilerParams(dimension_semantics=("arbitrary",)),
    )(w_in_t, w_uq, w_ukv, w_gate_f, w_gate_b)


def _inproj_kernel(*refs, latent, mod_row):
    (x_ref, mod_ref, nw_ref, win_ref, qn_ref, wuq_ref, kvn_ref, wk_ref, wvt_ref, wg_ref, bgf_ref, bgb_ref) = refs[:12]
    if latent:
        cos_ref, sa_ref, sb_ref = refs[12:15]
        q_ref, k_ref, vt_ref, gq_ref, gk_ref, gv_ref, gf_ref, gb_ref, go_ref = refs[15:]
    else:
        (gn_ref, attn_ref, gla_ref, sf_ref, sb_ref, ckv_ref, krt_ref, q_ref, k_ref, vt_ref, st_ref, p_ref,
         gq_ref, gk_ref, gv_ref, gf_ref, gb_ref, go_ref) = refs[12:30]
        gla_scratch = refs[30:]
    seq = q_ref.shape[1]

    sh1, sc1 = _mod_rows(mod_ref, mod_row(pl.program_id(0)))[:2]
    scale = MLA_QK ** -0.5 * LOG2_E
    lane = lax.broadcasted_iota(jnp.int32, (INPROJ_SUB, LANES), 1)
    in_rope = (lane >= ROPE_LANE0) & (lane < ROPE_LANE0 + MLA_ROPE)

    def sub_tile(r0):
        rows = slice(r0, r0 + INPROJ_SUB)
        h = (_rms(x_ref[rows, :], nw_ref[...]) * (1.0 + sc1) + sh1).astype(BF16)
        yield
        z_all = jnp.dot(h, win_ref[...], preferred_element_type=F32)
        z = lambda lo, n: z_all[:, lo:lo + n]
        yield
        qn = _rms(z(Z_Q, Q_LORA), qn_ref[...]).astype(BF16)
        ckv = _rms(z(Z_KV, KV_LORA), kvn_ref[...])
        ckv_b = ckv.astype(BF16)
        misc = z(Z_MISC, LANES)
        yield
        q = jnp.dot(qn, wuq_ref[...], preferred_element_type=F32)
        kn = jnp.dot(ckv_b, wk_ref[...], preferred_element_type=F32)
        vt_ref[:, rows] = lax.dot_general(wvt_ref[...], ckv_b, _NT,
                                          preferred_element_type=F32).astype(BF16)
        gpre = jnp.dot(misc.astype(BF16), wg_ref[...], preferred_element_type=F32)
        yield
        if latent:
            cos, sa, sb = cos_ref[rows, :], sa_ref[rows, :], sb_ref[rows, :]

            def rope(t):
                return t * cos + pltpu.roll(t, LANES - 8, 1) * sa + pltpu.roll(t, 8, 1) * sb
        else:
            def rope(t):
                return t

        def put(ref, val, cols=slice(None)):
            if len(ref.shape) == 2:
                ref[rows, cols] = val
            else:
                for b in range(INPROJ_SUB // seq):
                    ref[r0 // seq + b, :, cols] = val[b * seq:(b + 1) * seq, :]

        krope = rope(misc)
        for hd in range(MLA_HEADS):
            sl = slice(hd * HEAD_PAD, (hd + 1) * HEAD_PAD)
            put(q_ref, (rope(q[:, sl]) * scale).astype(BF16), sl)
            put(k_ref, jnp.where(in_rope, krope, kn[:, sl]).astype(BF16), sl)
        put(gq_ref, z(Z_GQ, GQK))
        put(gk_ref, z(Z_GK, GQK))
        put(gv_ref, z(Z_GV, GV).astype(BF16))
        put(go_ref, z(Z_GO, GV))
        put(gf_ref, _log_sigmoid(gpre[:, :GQK] + bgf_ref[...]) * (1.0 / GATE_NORM))
        put(gb_ref, _log_sigmoid(gpre[:, GQK:] + bgb_ref[...]) * (1.0 / GATE_NORM))
        if not latent:
            ckv_ref[rows, :] = ckv
            misc_t = misc.T
            n = krt_ref.shape[2]
            for b in range(INPROJ_SUB // n):
                krt_ref[r0 // n + b] = misc_t[ROPE_LANE0:ROPE_LANE0 + MLA_ROPE, b * n:(b + 1) * n]

    _interleave([sub_tile(r0) for r0 in range(0, x_ref.shape[0], INPROJ_SUB)])
    if not latent:
        n_seqs = q_ref.shape[0]
        _gla_body((gq_ref, gk_ref, gv_ref, gf_ref, gb_ref, go_ref), None, gn_ref,
                  (gla_ref, sf_ref, sb_ref), gla_scratch, seq // GLA_TILE, n_seqs,
                  [_attn_pipeline(q_ref, k_ref, vt_ref, attn_ref, st_ref, p_ref, None,
                                  [(b, q0) for q0 in range(0, seq, Q_TILE)], 2 * b) for b in range(n_seqs)])


def _inproj(x2d, mod, mod_row, weights, rope_tabs, tm, seq_len, gla_norm=None):
    n_tok, d = x2d.shape
    latent = rope_tabs is not None
    tiles_per_seq = max(seq_len // tm, 1)
    nw, win, qn, wuq, kvn, wk, wvt, wg, bgf, bgb = weights
    row = lambda i: (i, 0)
    in_specs = [pl.BlockSpec((tm, d), row), _const_spec(mod.shape),
                _const_spec(nw.shape), _const_spec(win.shape), _const_spec(qn.shape),
                _const_spec(wuq.shape), _const_spec(kvn.shape), _const_spec(wk.shape),
                _const_spec(wvt.shape), _const_spec(wg.shape), _const_spec(bgf.shape),
                _const_spec(bgb.shape)]
    args = [x2d, mod, nw, win, qn, wuq, kvn, wk, wvt, wg, bgf, bgb]
    if latent:
        tab = pl.BlockSpec((tm, LANES), lambda i: (i % tiles_per_seq, 0))
        in_specs += [tab, tab, tab]
        args += list(rope_tabs)
    per_token = lambda c, dt: (jax.ShapeDtypeStruct((n_tok, c), dt), pl.BlockSpec((tm, c), row))
    seq = tm if latent else seq_len
    per_seq = lambda r, c, dt: (jax.ShapeDtypeStruct((n_tok // seq, r, c), dt),
                                pl.BlockSpec((tm // seq, r, c), lambda i: (i, 0, 0)))
    gla_outs = [per_token(GQK, F32), per_token(GQK, F32), per_token(GV, BF16),
                per_token(GQK, F32), per_token(GQK, F32), per_token(GV, F32)]
    scratch = []
    if latent:
        outs = [per_seq(seq, QPAD, BF16), per_seq(seq, QPAD, BF16),
                (jax.ShapeDtypeStruct((VALL, n_tok), BF16), pl.BlockSpec((VALL, tm), lambda i: (0, i)))]
        outs += gla_outs
    else:
        in_specs.append(_const_spec(gla_norm.shape))
        args.append(gla_norm)
        state = (jax.ShapeDtypeStruct((n_tok // seq, GLA_HEADS, GLA_DK, GLA_DV), F32),
                 pl.BlockSpec((tm // seq, GLA_HEADS, GLA_DK, GLA_DV), lambda i: (i, 0, 0, 0)))
        outs = [per_seq(seq, VALL, BF16), per_seq(seq, GV, BF16), state, state,
                per_token(KV_LORA, F32), per_seq(MLA_ROPE, seq, F32)]
        vmem = lambda c, dt: pltpu.VMEM((tm // seq, seq, c), dt)
        scratch = [vmem(QPAD, BF16), vmem(QPAD, BF16), pltpu.VMEM((VALL, tm), BF16),
                   pltpu.VMEM((2 * (tm // seq), seq, Q_TILE), F32),
                   pltpu.VMEM((2 * (tm // seq), seq, Q_TILE), BF16),
                   vmem(GQK, F32), vmem(GQK, F32), vmem(GV, BF16), vmem(GQK, F32), vmem(GQK, F32),
                   vmem(GV, F32)] + _gla_scratch(tm // seq, seq)
    return pl.pallas_call(
        functools.partial(_inproj_kernel, latent=latent, mod_row=mod_row),
        out_shape=[o[0] for o in outs],
        grid=(n_tok // tm,),
        in_specs=in_specs,
        out_specs=[o[1] for o in outs],
        scratch_shapes=scratch,
        name="inproj_lat" if latent else "inproj_ctx",
        compiler_params=pltpu.CompilerParams(dimension_semantics=("arbitrary",),
                                             vmem_limit_bytes=VMEM_LIMIT),
    )(*args)


def _decomp_kernel(ckv_ref, krt_ref, wk_ref, wvt_ref, k_ref, vt_ref):
    ckv_b = ckv_ref[...].astype(BF16)
    kn = jnp.dot(ckv_b, wk_ref[...], preferred_element_type=F32)
    n_keys = krt_ref.shape[1]
    kr = jnp.concatenate([jnp.zeros((ROPE_LANE0, n_keys), F32), krt_ref[...],
                          jnp.zeros((LANES - ROPE_LANE0 - MLA_ROPE, n_keys), F32)], axis=0).T
    lane = lax.broadcasted_iota(jnp.int32, kr.shape, 1)
    in_rope = (lane >= ROPE_LANE0) & (lane < ROPE_LANE0 + MLA_ROPE)
    for hd in range(MLA_HEADS):
        sl = slice(hd * HEAD_PAD, (hd + 1) * HEAD_PAD)
        k_ref[:, sl] = jnp.where(in_rope, kr, kn[:, sl]).astype(BF16)
    vt_ref[...] = lax.dot_general(wvt_ref[...], ckv_b, _NT, preferred_element_type=F32).astype(BF16)


def _decomp(ckv, kr_t, wk, wvt):
    b, s, _ = ckv.shape
    return pl.pallas_call(
        _decomp_kernel,
        out_shape=[jax.ShapeDtypeStruct((b, s, QPAD), BF16), jax.ShapeDtypeStruct((VALL, b * s), BF16)],
        grid=(b,),
        in_specs=[pl.BlockSpec((None, s, KV_LORA), lambda i: (i, 0, 0)),
                  pl.BlockSpec((None, MLA_ROPE, s), lambda i: (i, 0, 0)),
                  _const_spec(wk.shape), _const_spec(wvt.shape)],
        out_specs=[pl.BlockSpec((None, s, QPAD), lambda i: (i, 0, 0)),
                   pl.BlockSpec((VALL, s), lambda i: (0, i))],
        name="ctx_decompress",
        compiler_params=pltpu.CompilerParams(dimension_semantics=("arbitrary",)),
    )(ckv, kr_t, wk, wvt)


def _attn_kernel(*refs, has_ctx, n_seqs, n_side):
    n_in = 5 if has_ctx else 3
    side_in, refs = refs[n_in:n_in + n_side], refs[:n_in] + refs[n_in + n_side:]
    side_out, refs = refs[n_in + 1:n_in + 1 + n_side], refs[:n_in + 1] + refs[n_in + 1 + n_side:]
    if has_ctx:
        q_ref, kc_ref, vct_ref, k_ref, vt_ref, o_ref, st_ref, p_ref = refs
    else:
        q_ref, k_ref, vt_ref, o_ref, st_ref, p_ref = refs

    for src, dst in zip(side_in, side_out):
        dst[...] = src[...].astype(BF16)
    tiles = [(b, q0) for b in range(n_seqs) for q0 in range(0, q_ref.shape[1], Q_TILE)]
    _interleave([_attn_pipeline(q_ref, k_ref, vt_ref, o_ref, st_ref, p_ref,
                                (kc_ref, vct_ref) if has_ctx else None, [tile], 2 * c)
                 for c, tile in enumerate(tiles)])


def _attn_pipeline(q_ref, k_ref, vt_ref, o_ref, st_ref, p_ref, ctx_refs, tiles, slot0):
    tq = Q_TILE

    def key_blocks(bi):
        srcs = [ctx_refs] if ctx_refs is not None else []
        blocks, row0 = [], 0
        for kr, vr in srcs + [(k_ref, vt_ref)]:
            n_keys = kr.shape[1]
            size = min(KEY_BLOCK, n_keys)
            for r in range(0, n_keys, size):
                blocks.append((kr, vr, r, bi * n_keys + r, size, row0))
                row0 += size
        return blocks

    units = [(bi, slice(q0, q0 + tq), hd) for bi, q0 in tiles for hd in range(MLA_HEADS)]
    col_max = [None] * len(units)
    pair = []
    for stage in range(len(units) + 2):
        ua, ub, uc = stage, stage - 1, stage - 2
        run_max = None
        acc = jnp.zeros((MLA_V + ONES_ROWS, tq), F32)
        for j in range(len(key_blocks(0))):
            if ua < len(units):
                bi, qrows, hd = units[ua]
                kr, _, r0, _, size, srow = key_blocks(bi)[j]
                sl = slice(hd * HEAD_PAD, (hd + 1) * HEAD_PAD)
                st = lax.dot_general(kr[bi, r0:r0 + size, sl], q_ref[bi, qrows, sl], _NT,
                                     preferred_element_type=F32)
                st_ref[slot0 + ua % 2, srow:srow + size, :] = st
                blk_max = jnp.max(st.reshape(size // 8, 8, tq), axis=0)
                run_max = blk_max if run_max is None else jnp.maximum(run_max, blk_max)
            if 0 <= ub < len(units):
                _, _, _, _, size, srow = key_blocks(0)[j]
                p_ref[slot0 + ub % 2, srow:srow + size, :] = jnp.exp2(
                    st_ref[slot0 + ub % 2, srow:srow + size, :] - col_max[ub]).astype(BF16)
            if uc >= 0:
                bi, _, hd = units[uc]
                _, vr, _, c0, size, srow = key_blocks(bi)[j]
                v_aug = jnp.concatenate([vr[hd * MLA_V:(hd + 1) * MLA_V, c0:c0 + size],
                                         jnp.ones((ONES_ROWS, size), BF16)], axis=0)
                acc = acc + jnp.dot(v_aug, p_ref[slot0 + uc % 2, srow:srow + size, :],
                                    preferred_element_type=F32)
        if ua < len(units):
            col_max[ua] = jnp.max(run_max, axis=0, keepdims=True)
        if uc >= 0:
            bi, qrows, hd = units[uc]
            pair.append(acc[:MLA_V, :] / acc[MLA_V:MLA_V + 1, :])
            if len(pair) == 2:
                o_ref[bi, qrows, (hd - 1) * MLA_V:(hd + 1) * MLA_V] = (
                    jnp.concatenate(pair, axis=0).T.astype(BF16))
                pair = []
        yield


def _attention(q, k, vt, ctx_kv, tq, n_seqs, side_weights=()):
    b, t, _ = q.shape
    steps = (b // n_seqs) * (t // tq)
    assert (n_seqs == 1 or tq == t) and tq % Q_TILE == 0
    has_ctx = ctx_kv is not None
    in_specs = [pl.BlockSpec((n_seqs, tq, QPAD), lambda i, j: (i, j, 0))]
    args = [q]
    if has_ctx:
        kc, vct = ctx_kv
        s = kc.shape[1]
        in_specs += [pl.BlockSpec((n_seqs, s, QPAD), lambda i, j: (i, 0, 0)),
                     pl.BlockSpec((VALL, n_seqs * s), lambda i, j: (0, i))]
        args += [kc, vct]
    in_specs += [pl.BlockSpec((n_seqs, t, QPAD), lambda i, j: (i, 0, 0)),
                 pl.BlockSpec((VALL, n_seqs * t), lambda i, j: (0, i))]
    args += [k, vt]
    out_shape = [jax.ShapeDtypeStruct((b, t, VALL), BF16)]
    out_specs = [pl.BlockSpec((n_seqs, tq, VALL), lambda i, j: (i, j, 0))]
    nj = t // tq
    for w in side_weights:
        _, rows, cols = w.shape
        assert rows % (16 * steps) == 0
        in_specs.append(pl.BlockSpec((None, rows // steps, cols), lambda i, j: (0, i * nj + j, 0)))
        args.append(w)
        out_shape.append(jax.ShapeDtypeStruct((rows, cols), BF16))
        out_specs.append(pl.BlockSpec((rows // steps, cols), lambda i, j: (i * nj + j, 0)))
    return pl.pallas_call(
        functools.partial(_attn_kernel, has_ctx=has_ctx, n_seqs=n_seqs, n_side=len(side_weights)),
        out_shape=out_shape,
        grid=(b // n_seqs, t // tq),
        in_specs=in_specs,
        out_specs=out_specs,
        scratch_shapes=[pltpu.VMEM((2 * n_seqs * (tq // Q_TILE), t + (s if has_ctx else 0), Q_TILE), F32),
                        pltpu.VMEM((2 * n_seqs * (tq // Q_TILE), t + (s if has_ctx else 0), Q_TILE), BF16)],
        name="mla_attn_lat" if has_ctx else "mla_attn_ctx",
        compiler_params=pltpu.CompilerParams(dimension_semantics=("arbitrary", "arbitrary"),
                                             vmem_limit_bytes=VMEM_LIMIT),
    )(*args)


def _gla_kernel(*refs, n_tiles, n_seqs, zero_init):
    init_refs, rest = (None, refs[6:]) if zero_init else (refs[6:8], refs[8:])
    _gla_body(refs[:6], init_refs, rest[0], rest[1:4], rest[4:], n_tiles, n_seqs)


def _gla_scratch(n_seqs, t):
    return [pltpu.VMEM((2, n_seqs, t, GV), F32),
            pltpu.VMEM((GQK, GLA_TILE), BF16),
            pltpu.VMEM((2, GLA_TILE, GLA_TILE), F32),
            pltpu.VMEM((GLA_HEADS * GLA_TILE, GQK), BF16)]


def _gla_body(in_refs, init_refs, gn_ref, out_refs, scratch_refs, n_tiles, n_seqs, side_chains=()):
    gq_ref, gk_ref, gv_ref, gf_ref, gb_ref, go_ref = in_refs
    o_ref, sf_ref, sb_ref = out_refs
    oacc_ref, bdqk_ref, tri_ref, hm_ref = scratch_refs
    g_refs = (gf_ref, gb_ref)
    state_refs = (sf_ref, sb_ref)

    @pl.when(pl.program_id(0) == 0)
    def _():
        ri = lax.broadcasted_iota(jnp.int32, (GLA_TILE, GLA_TILE), 0)
        ci = lax.broadcasted_iota(jnp.int32, (GLA_TILE, GLA_TILE), 1)
        same_chunk = (ri // CHUNK) == (ci // CHUNK)
        bdqk_ref[...] = jnp.where(same_chunk, 1.0, 0.0).astype(BF16)
        tri_ref[0] = jnp.where(same_chunk & (ri >= ci), 1.0, 0.0)
        tri_ref[1] = jnp.where(same_chunk & (ci >= ri), 1.0, 0.0)
        hm_ref[...] = jnp.where(
            lax.broadcasted_iota(jnp.int32, (GLA_HEADS * GLA_TILE, GQK), 0) // GLA_TILE
            == lax.broadcasted_iota(jnp.int32, (GLA_HEADS * GLA_TILE, GQK), 1) // GLA_DK,
            1.0, 0.0).astype(BF16)

    row8 = lax.broadcasted_iota(jnp.int32, (8, GQK), 0)

    def row_slice(start, size):
        return pl.ds(start if isinstance(start, int) else pl.multiple_of(start, size), size)

    def tile_rows(t):
        return row_slice(t * GLA_TILE, GLA_TILE)

    finished = set()

    def total_row(c, d):
        return c * CHUNK + (CHUNK - 1 if d == 0 else 0)

    def tile_dir(b, t, d):
        rows = tile_rows(t)
        g = g_refs[d][b, rows, :]
        g_hi = g.astype(BF16)
        g_lo = (g - g_hi.astype(F32)).astype(BF16)
        tri_b = tri_ref[d].astype(BF16)
        cum = (jnp.dot(tri_b, g_hi, preferred_element_type=F32)
               + jnp.dot(tri_b, g_lo, preferred_element_type=F32))
        yield
        totals = [cum[total_row(c, d):total_row(c, d) + 1, :] for c in range(CHUNKS_PER_TILE)]
        tot8 = jnp.zeros((8, GQK), F32)
        for c in range(CHUNKS_PER_TILE):
            tot8 = jnp.where(row8 == c, totals[c], tot8)
        dec_t = jnp.concatenate([jnp.exp(tot8), jnp.zeros((LANES - 8, GQK), F32)], axis=0).T
        q = gq_ref[b, rows, :] * (GLA_DK ** -0.5)
        k = gk_ref[b, rows, :]
        v = gv_ref[b, rows, :]
        tot = jnp.concatenate([jnp.broadcast_to(tc, (CHUNK, GQK)) for tc in totals], axis=0)
        qe = (q * jnp.exp(cum)).astype(BF16)
        ke = (k * jnp.exp(-cum)).astype(BF16)
        kd_t = (k * jnp.exp(tot - cum)).T.astype(BF16)
        bd_qk = bdqk_ref[...] > 0
        tri = tri_ref[d] > 0

        qm = jnp.where(hm_ref[...] > 0, jnp.tile(qe, (GLA_HEADS, 1)), 0.0)
        yield
        att = lax.dot_general(qm, ke, _NT, preferred_element_type=F32)

        yield
        intra, upd = [], []
        for hd in range(GLA_HEADS):
            vh = v[:, hd * GLA_DV:(hd + 1) * GLA_DV]
            a_h = jnp.where(tri, att[hd * GLA_TILE:(hd + 1) * GLA_TILE, :], 0.0).astype(BF16)
            intra.append(jnp.dot(a_h, vh, preferred_element_type=F32))
            kd_h = jnp.tile(kd_t[hd * GLA_DK:(hd + 1) * GLA_DK, :], (CHUNKS_PER_TILE, 1))
            upd.append(jnp.dot(jnp.where(bd_qk, kd_h, 0.0), vh, preferred_element_type=F32))

        yield
        state = [state_refs[d][b, hd] for hd in range(GLA_HEADS)]
        order = range(CHUNKS_PER_TILE) if d == 0 else range(CHUNKS_PER_TILE - 1, -1, -1)
        seen = {}
        for c in order:
            seen[c] = jnp.concatenate(state, axis=0).astype(BF16)
            decay = jnp.broadcast_to(dec_t[:, c:c + 1], (GQK, GLA_DV))
            for hd in range(GLA_HEADS):
                ks = slice(hd * GLA_DK, (hd + 1) * GLA_DK)
                state[hd] = decay[ks, :] * state[hd] + upd[hd][c * CHUNK:(c + 1) * CHUNK, :]
        for hd in range(GLA_HEADS):
            state_refs[d][b, hd] = state[hd]

        yield
        for c in range(CHUNKS_PER_TILE):
            cr = slice(c * CHUNK, (c + 1) * CHUNK)
            q_c = jnp.concatenate([qm[hd * GLA_TILE + c * CHUNK:hd * GLA_TILE + (c + 1) * CHUNK, :]
                                   for hd in range(GLA_HEADS)], axis=0)
            inter = jnp.dot(q_c, seen[c], preferred_element_type=F32)
            o = jnp.concatenate([intra[hd][cr, :] + inter[hd * CHUNK:(hd + 1) * CHUNK, :]
                                 for hd in range(GLA_HEADS)], axis=1)
            oacc_ref[d, b, row_slice(t * GLA_TILE + c * CHUNK, CHUNK), :] = o
        if isinstance(t, int):
            finished.add((b, t, d))

    if init_refs is None:
        sf_ref[...] = jnp.zeros(sf_ref.shape, F32)
        sb_ref[...] = jnp.zeros(sb_ref.shape, F32)
    else:
        sf_ref[...] = init_refs[0][...]
        sb_ref[...] = init_refs[1][...]

    gn = gn_ref[...]

    def epilogue(t):
        while isinstance(t, int) and not all((b, t, d) in finished for b in range(n_seqs) for d in range(2)):
            yield
        rows = tile_rows(t)
        for b in range(n_seqs):
            for hd in range(GLA_HEADS):
                vs = slice(hd * GLA_DV, (hd + 1) * GLA_DV)
                o = _rms(oacc_ref[0, b, rows, vs] + oacc_ref[1, b, rows, vs], gn)
                go = go_ref[b, rows, vs]
                o_ref[b, rows, vs] = (o * (go * _sigmoid(go))).astype(BF16)
        return
        yield

    def main_chains(first_tile, n):
        chains = []
        for t in [first_tile + u for u in range(n)]:
            for b in range(n_seqs):
                chains += [tile_dir(b, t, 0), tile_dir(b, n_tiles - 1 - t, 1)]
        return chains

    if n_tiles <= GLA_STATIC_TILES:
        _interleave(list(side_chains) + main_chains(0, n_tiles) + [epilogue(t) for t in range(n_tiles)])
    else:
        assert not side_chains
        per_step = next(c for c in (4, 2, 1) if n_tiles % c == 0)

        def main_body(i, carry):
            _interleave(main_chains(i * per_step, per_step))
            return carry

        lax.fori_loop(0, n_tiles // per_step, main_body, 0)

        def epilogue_body(t, carry):
            _interleave([epilogue(t)])
            return carry

        lax.fori_loop(0, n_tiles, epilogue_body, 0)


def _gla(gq, gk, gv, gf, gb, go, init_states, gn, n_seqs):
    b, t, _ = gq.shape
    n_tiles = t // GLA_TILE
    zero_init = init_states is None
    seq = lambda c: pl.BlockSpec((n_seqs, t, c), lambda i: (i, 0, 0))
    st = pl.BlockSpec((n_seqs, GLA_HEADS, GLA_DK, GLA_DV), lambda i: (i, 0, 0, 0))
    in_specs = [seq(GQK), seq(GQK), seq(GV), seq(GQK), seq(GQK), seq(GV)]
    args = [gq, gk, gv, gf, gb, go]
    if not zero_init:
        in_specs += [st, st]
        args += list(init_states)
    in_specs.append(_const_spec(gn.shape))
    args.append(gn)
    return pl.pallas_call(
        functools.partial(_gla_kernel, n_tiles=n_tiles, n_seqs=n_seqs, zero_init=zero_init),
        out_shape=[jax.ShapeDtypeStruct((b, t, GV), BF16),
                   jax.ShapeDtypeStruct((b, GLA_HEADS, GLA_DK, GLA_DV), F32),
                   jax.ShapeDtypeStruct((b, GLA_HEADS, GLA_DK, GLA_DV), F32)],
        grid=(b // n_seqs,),
        in_specs=in_specs,
        out_specs=[seq(GV), st, st],
        scratch_shapes=_gla_scratch(n_seqs, t),
        name="gla_%d" % t,
        compiler_params=pltpu.CompilerParams(dimension_semantics=("arbitrary",),
                                             vmem_limit_bytes=VMEM_LIMIT),
    )(*args)


def _ffn_kernel(xp_ref, xs_ref, atp_ref, ats_ref, glp_ref, gls_ref, mod_ref, wout_ref, nf_ref,
                wfi_ref, wfo_ref, fn_ref, yp_ref, ys_ref, act_ref, *, ctx_tiles, tiles_per_seq):
    def sub_tile(x_ref, at_ref, gl_ref, y_ref, mod_row, r0):
        rows = slice(r0, r0 + FFN_SUB)
        gt1, sh2, sc2, gt2 = _mod_rows(mod_ref, mod_row)[2:]
        mix = (jnp.dot(at_ref[rows, :], wout_ref[0:VALL, :], preferred_element_type=F32)
               + jnp.dot(gl_ref[rows, :], wout_ref[VALL:, :], preferred_element_type=F32))
        yield
        x1 = x_ref[rows, :] + gt1 * mix
        h2 = (_rms(x1, nf_ref[...]) * (1.0 + sc2) + sh2).astype(BF16)
        yield
        for j in range(N_FF_CHUNKS):
            cs = slice(j * FF_CHUNK, (j + 1) * FF_CHUNK)
            a = jnp.dot(h2, wfi_ref[:, cs], preferred_element_type=F32)
            g = jnp.dot(h2, wfi_ref[:, D_FF + j * FF_CHUNK:D_FF + (j + 1) * FF_CHUNK],
                        preferred_element_type=F32)
            act_ref[rows, cs] = (a * _sigmoid(a) * g).astype(BF16)
            if j in FFN_PHASE_ENDS:
                yield
        ff = jnp.dot(act_ref[rows, :], wfo_ref[...], preferred_element_type=F32)
        yield
        x2 = x1 + gt2 * ff
        y_ref[rows, :] = _rms(x2, fn_ref[...])

    def tile(x_ref, at_ref, gl_ref, y_ref, mod_row):
        _interleave([sub_tile(x_ref, at_ref, gl_ref, y_ref, mod_row, r0)
                     for r0 in range(0, x_ref.shape[0], FFN_SUB)])

    t = pl.program_id(0)

    @pl.when(t < ctx_tiles)
    def _():
        tile(xp_ref, atp_ref, glp_ref, yp_ref, 0)

    @pl.when(t >= ctx_tiles)
    def _():
        tile(xs_ref, ats_ref, gls_ref, ys_ref, 1 + (t - ctx_tiles) // tiles_per_seq)


def _ffn(xp, xs, attn_p, attn_s, gla_p, gla_s, mod, wout, nf, wfi, wfo, fn, tm, tiles_per_seq):
    d = xp.shape[1]
    ctx_tiles = xp.shape[0] // tm
    lat_tiles = xs.shape[0] // tm
    ctx_map = lambda s: (jnp.minimum(s, ctx_tiles - 1), 0)
    lat_map = lambda s: (jnp.maximum(s - ctx_tiles, 0), 0)
    tile = lambda c, m: pl.BlockSpec((tm, c), m)
    return pl.pallas_call(
        functools.partial(_ffn_kernel, ctx_tiles=ctx_tiles, tiles_per_seq=tiles_per_seq),
        out_shape=[jax.ShapeDtypeStruct(xp.shape, F32), jax.ShapeDtypeStruct(xs.shape, F32)],
        grid=(ctx_tiles + lat_tiles,),
        in_specs=[tile(d, ctx_map), tile(d, lat_map), tile(VALL, ctx_map), tile(VALL, lat_map),
                  tile(GV, ctx_map), tile(GV, lat_map), _const_spec(mod.shape),
                  _const_spec(wout.shape), _const_spec(nf.shape), _const_spec(wfi.shape),
                  _const_spec(wfo.shape), _const_spec(fn.shape)],
        out_specs=[tile(d, ctx_map), tile(d, lat_map)],
        scratch_shapes=[pltpu.VMEM((tm, D_FF), BF16)],
        name="out_ffn",
        compiler_params=pltpu.CompilerParams(dimension_semantics=("arbitrary",),
                                             vmem_limit_bytes=VMEM_LIMIT),
    )(xp, xs, attn_p, attn_s, gla_p, gla_s, mod, wout, nf, wfi, wfo, fn)


def _rope_tables(n_tokens):
    t = np.arange(n_tokens)
    row = (t // GRID_W).astype(np.float32)
    col = (t % GRID_W).astype(np.float32)
    half = MLA_ROPE // 2
    inv = (np.float32(ROPE_BASE) ** (-np.arange(0, half, 2, dtype=np.float32) / np.float32(half))).astype(np.float32)
    ang_r = row[:, None] * inv
    ang_c = col[:, None] * inv
    ang = np.concatenate([ang_r, ang_r, ang_c, ang_c], axis=-1).astype(np.float32)
    cos, sin = np.cos(ang), np.sin(ang)
    first = (np.arange(MLA_ROPE) % half) < (half // 2)
    cos_t = np.ones((n_tokens, LANES), np.float32)
    sa_t = np.zeros((n_tokens, LANES), np.float32)
    sb_t = np.zeros((n_tokens, LANES), np.float32)
    cos_t[:, ROPE_LANE0:ROPE_LANE0 + MLA_ROPE] = cos
    sa_t[:, ROPE_LANE0:ROPE_LANE0 + MLA_ROPE] = np.where(first, -sin, 0.0)
    sb_t[:, ROPE_LANE0:ROPE_LANE0 + MLA_ROPE] = np.where(first, 0.0, sin)
    return jnp.asarray(cos_t), jnp.asarray(sa_t), jnp.asarray(sb_t)


def kernel(x_prompt, x_sample, cache_kv_latent, cache_k_rope, state_gla_fwd, state_gla_bwd, c, c_ctx, w_ada, b_ada, norm_attn, w_in, mla_q_norm, w_uq, mla_kv_norm, w_ukv, w_gate_f, b_gate_f, w_gate_b, b_gate_b, gla_norm, w_out, norm_ffn, w_ffn_in, w_ffn_out, final_norm):
    batch, seq, d = x_prompt.shape
    dec_batch, dec_seq, _ = x_sample.shape
    assert w_ada.shape[0] == 1 and w_in.shape[-1] == W_COLS and w_ffn_in.shape[-1] == 2 * D_FF
    l = 0

    mod = _ada(c_ctx, c, w_ada[l], b_ada[l])

    win, wuq, wk, wvt, wg = _prep_in_weights(w_in, w_uq, w_ukv, w_gate_f, w_gate_b)
    in_w = (norm_attn[l].reshape(1, d), win, mla_q_norm[l].reshape(1, Q_LORA), wuq,
            mla_kv_norm[l].reshape(1, KV_LORA), wk, wvt, wg, b_gate_f, b_gate_b)
    gn = gla_norm[l].reshape(1, GLA_DV)
    tm, tm_ffn = 512, 512
    r3 = lambda a, b_, t: a.reshape(b_, t, a.shape[-1])

    xp = x_prompt.reshape(batch * seq, d)
    (attn_p, gla_p, sf, sb, ckv, kr_t) = _inproj(xp, mod, lambda i: 0, in_w, None, tm, seq, gn)

    xs = x_sample.reshape(dec_batch * dec_seq, d)
    tiles = dec_seq // tm
    (q, k, vt, gq, gk, gv, gf, gb, go) = _inproj(xs, mod, lambda i: 1 + i // tiles, in_w,
                                                  _rope_tables(dec_seq), tm, dec_seq)
    kc, vct = _decomp(cache_kv_latent[:, l], jnp.swapaxes(cache_k_rope[:, l], 1, 2), wk, wvt)
    attn_s, wout, wfi, wfo = _attention(r3(q, dec_batch, dec_seq), r3(k, dec_batch, dec_seq), vt, (kc, vct),
                                        ATTN_LAT_QUERIES, 1, (w_out, w_ffn_in, w_ffn_out))
    gla_s, _, _ = _gla(r3(gq, dec_batch, dec_seq), r3(gk, dec_batch, dec_seq), r3(gv, dec_batch, dec_seq),
                       r3(gf, dec_batch, dec_seq), r3(gb, dec_batch, dec_seq), r3(go, dec_batch, dec_seq),
                       (state_gla_fwd[:, l].astype(F32), state_gla_bwd[:, l].astype(F32)), gn, 1)

    flat = lambda a: a.reshape(-1, a.shape[-1])
    y_prompt, y_sample = _ffn(xp, xs, flat(attn_p), flat(attn_s), flat(gla_p), flat(gla_s), mod,
                              wout, norm_ffn[l].reshape(1, d), wfi, wfo,
                              final_norm.reshape(1, d), tm_ffn, dec_seq // tm_ffn)
    y_prompt = y_prompt.reshape(batch, seq, d)
    y_sample = y_sample.reshape(dec_batch, dec_seq, d)

    new_kv_latent = ckv.reshape(batch, 1, seq, KV_LORA)
    new_k_rope = jnp.swapaxes(kr_t, 1, 2).reshape(batch, 1, seq, MLA_ROPE)
    new_state_fwd = sf.reshape(batch, 1, GLA_HEADS, GLA_DK, GLA_DV).astype(x_prompt.dtype)
    new_state_bwd = sb.reshape(batch, 1, GLA_HEADS, GLA_DK, GLA_DV).astype(x_prompt.dtype)
    return (y_prompt, y_sample, new_kv_latent, new_k_rope, new_state_fwd, new_state_bwd)
```

```python
import functools

import numpy as np
import jax
import jax.numpy as jnp
from jax import lax
from jax.experimental import pallas as pl
from jax.experimental.pallas import tpu as pltpu

F32 = jnp.float32
BF16 = jnp.bfloat16

GRID_W = 64
MLA_HEADS = 8
MLA_NOPE = 64
MLA_ROPE = 32
MLA_QK = MLA_NOPE + MLA_ROPE
MLA_V = 64
Q_LORA = 384
KV_LORA = 256
GLA_HEADS = 4
GLA_DK = 64
GLA_DV = 128
GATE_RANK = 16
GATE_NORM = 16.0
CHUNK = 64
D_FF = 2816
ROPE_BASE = 10000.0
EPS = 1e-6
LOG2_E = 1.4426950408889634

LANES = 128
HEAD_PAD = LANES
ROPE_LANE0 = MLA_NOPE
GQK = GLA_HEADS * GLA_DK
GV = GLA_HEADS * GLA_DV
QPAD = MLA_HEADS * HEAD_PAD
VALL = MLA_HEADS * MLA_V
ONES_ROWS = 16
KEY_BLOCK = 2048

W_KR = Q_LORA + KV_LORA
W_GQ = W_KR + MLA_ROPE
W_GF = W_GQ + 2 * GQK + GV
W_GO = W_GF + 2 * GATE_RANK
W_COLS = W_GO + GV

Z_Q = 0
Z_KV = Z_Q + Q_LORA
Z_GQ = Z_KV + KV_LORA
Z_GK = Z_GQ + GQK
Z_GV = Z_GK + GQK
Z_GO = Z_GV + GV
Z_MISC = Z_GO + GV
Z_COLS = Z_MISC + LANES

FF_CHUNK = 256
N_FF_CHUNKS = D_FF // FF_CHUNK
FFN_SUB = 256
FFN_PHASE_ENDS = (3, 7, 10)

GLA_TILE = 256
CHUNKS_PER_TILE = GLA_TILE // CHUNK
ADA_ROWS = 128
INPROJ_SUB = 512
Q_TILE = 256
ATTN_LAT_QUERIES = 512
GLA_STATIC_TILES = 8

VMEM_LIMIT = 56 * 1024 * 1024

_NT = (((1,), (1,)), ((), ()))


def _rms(x, w):
    return x * lax.rsqrt(jnp.mean(x * x, axis=-1, keepdims=True) + EPS) * w


def _sigmoid(x):
    return 1.0 / (1.0 + jnp.exp(-x))


def _log_sigmoid(x):
    return jnp.minimum(x, 0.0) - jnp.log1p(jnp.exp(-jnp.abs(x)))


def _interleave(chains):
    pending, active = list(chains), []
    while pending or active:
        if pending:
            active.append(pending.pop(0))
        for chain in list(active):
            try:
                next(chain)
            except StopIteration:
                active.remove(chain)


def _const_spec(shape):
    nd = len(shape)
    return pl.BlockSpec(shape, lambda *_: (0,) * nd, pipeline_mode=pl.Buffered(1))


def _mod_rows(mod_ref, r):
    return [mod_ref[k, pl.ds(r, 1), :] for k in range(mod_ref.shape[0])]


def _ada_kernel(cctx_ref, c_ref, w_ref, b_ref, o_ref):
    k = pl.program_id(0)
    d = o_ref.shape[2]
    row = lax.broadcasted_iota(jnp.int32, (8, cctx_ref.shape[1]), 0)
    cond = jnp.where(row == 0, cctx_ref[...], 0.0)
    for r in range(c_ref.shape[0]):
        cond = jnp.where(row == 1 + r, c_ref[r:r + 1, :], cond)
    s = (cond * _sigmoid(cond)).astype(BF16)
    part = jnp.dot(s, w_ref[...].astype(BF16), preferred_element_type=F32)
    for j in range(o_ref.shape[0]):
        sl = slice(j * d, (j + 1) * d)

        @pl.when(k == 0)
        def _():
            o_ref[j] = part[:, sl] + b_ref[:, sl]

        @pl.when(k > 0)
        def _():
            o_ref[j] += part[:, sl]


def _ada(c_ctx, c, w_ada, b_ada):
    d = w_ada.shape[0]
    n = w_ada.shape[1]
    assert 1 + c.shape[0] <= 8
    return pl.pallas_call(
        _ada_kernel,
        out_shape=jax.ShapeDtypeStruct((n // d, 8, d), F32),
        grid=(d // ADA_ROWS,),
        in_specs=[pl.BlockSpec((1, ADA_ROWS), lambda k: (0, k)),
                  pl.BlockSpec((c.shape[0], ADA_ROWS), lambda k: (0, k)),
                  pl.BlockSpec((ADA_ROWS, n), lambda k: (k, 0)),
                  pl.BlockSpec((1, n), lambda k: (0, 0))],
        out_specs=pl.BlockSpec((n // d, 8, d), lambda k: (0, 0, 0)),
        name="ada_mod",
        compiler_params=pltpu.CompilerParams(dimension_semantics=("arbitrary",)),
    )(c_ctx.reshape(1, d), c, w_ada, b_ada.reshape(1, n))


def _prep_kernel(wint_ref, wuq_ref, wukv_ref, wgf_ref, wgb_ref, win_o, wuq_o, wk_o, wvt_o, wg_o):
    cols = wint_ref.shape[1]
    for dst, src, n in ((Z_Q, 0, W_KR), (Z_GQ, W_GQ, W_GF - W_GQ), (Z_GO, W_GO, GV)):
        win_o[:, dst:dst + n] = wint_ref[src:src + n, :].T.astype(BF16)
    z32 = jnp.zeros((32, cols), F32)
    misc_t = jnp.concatenate([wint_ref[W_GF:W_GO, :], z32, wint_ref[W_KR:W_GQ, :], z32], axis=0)
    win_o[:, Z_MISC:Z_COLS] = misc_t.T.astype(BF16)

    u = wuq_ref[...]
    zq = jnp.zeros((u.shape[0], HEAD_PAD - MLA_QK), F32)
    for hd in range(MLA_HEADS):
        blk = jnp.concatenate([u[:, hd * MLA_QK:(hd + 1) * MLA_QK], zq], axis=1)
        wuq_o[:, hd * HEAD_PAD:(hd + 1) * HEAD_PAD] = blk.astype(BF16)

    @pl.when(pl.program_id(0) == 0)
    def _():
        kv = wukv_ref[...]
        per = MLA_NOPE + MLA_V
        lane = lax.broadcasted_iota(jnp.int32, (kv.shape[0], per), 1)
        for hd in range(MLA_HEADS):
            blk = kv[:, hd * per:(hd + 1) * per]
            wk_o[:, hd * HEAD_PAD:(hd + 1) * HEAD_PAD] = jnp.where(lane < MLA_NOPE, blk, 0.0).astype(BF16)
        wv = jnp.concatenate([kv[:, hd * per + MLA_NOPE:(hd + 1) * per] for hd in range(MLA_HEADS)], axis=1)
        wvt_o[...] = wv.T.astype(BF16)

        wg_o[...] = jnp.zeros(wg_o.shape, BF16)
        wg_o[0:GATE_RANK, 0:GQK] = wgf_ref[...].astype(BF16)
        wg_o[GATE_RANK:2 * GATE_RANK, GQK:2 * GQK] = wgb_ref[...].astype(BF16)


def _prep_in_weights(w_in, w_uq, w_ukv, w_gate_f, w_gate_b):
    d = w_in.shape[1]
    steps = 4
    w_in_t = jnp.swapaxes(w_in, 1, 2)
    rb3 = lambda r, c: pl.BlockSpec((None, r // steps, c), lambda i: (0, i, 0))
    rb = lambda r, c: pl.BlockSpec((r // steps, c), lambda i: (i, 0))
    full3 = lambda shape: pl.BlockSpec((None,) + tuple(shape[1:]), lambda i: (0, 0, 0))
    full = lambda shape: pl.BlockSpec(shape, lambda i: (0, 0))
    return pl.pallas_call(
        _prep_kernel,
        out_shape=[jax.ShapeDtypeStruct((d, Z_COLS), BF16),
                   jax.ShapeDtypeStruct((Q_LORA, QPAD), BF16),
                   jax.ShapeDtypeStruct((KV_LORA, QPAD), BF16),
                   jax.ShapeDtypeStruct((VALL, KV_LORA), BF16),
                   jax.ShapeDtypeStruct((LANES, 2 * GQK), BF16)],
        grid=(steps,),
        in_specs=[pl.BlockSpec((None, W_COLS, d // steps), lambda i: (0, 0, i)),
                  rb3(Q_LORA, MLA_HEADS * MLA_QK), full3(w_ukv.shape),
                  full3(w_gate_f.shape), full3(w_gate_b.shape)],
        out_specs=[rb(d, Z_COLS), rb(Q_LORA, QPAD), full((KV_LORA, QPAD)), full((VALL, KV_LORA)),
                   full((LANES, 2 * GQK))],
        name="weight_prep",
        compiler_params=pltpu.CompilerParams(dimension_semantics=("arbitrary",)),
    )(w_in_t, w_uq, w_ukv, w_gate_f, w_gate_b)


def _inproj_kernel(*refs, latent, mod_row):
    (x_ref, mod_ref, nw_ref, win_ref, qn_ref, wuq_ref, kvn_ref, wk_ref, wvt_ref, wg_ref, bgf_ref, bgb_ref) = refs[:12]
    if latent:
        cos_ref, sa_ref, sb_ref = refs[12:15]
        q_ref, k_ref, vt_ref, gq_ref, gk_ref, gv_ref, gf_ref, gb_ref, go_ref = refs[15:]
    else:
        (gn_ref, attn_ref, gla_ref, sf_ref, sb_ref, ckv_ref, krt_ref, q_ref, k_ref, vt_ref, st_ref, p_ref,
         gq_ref, gk_ref, gv_ref, gf_ref, gb_ref, go_ref) = refs[12:30]
        gla_scratch = refs[30:]
    seq = q_ref.shape[1]

    sh1, sc1 = _mod_rows(mod_ref, mod_row(pl.program_id(0)))[:2]
    scale = MLA_QK ** -0.5 * LOG2_E
    lane = lax.broadcasted_iota(jnp.int32, (INPROJ_SUB, LANES), 1)
    in_rope = (lane >= ROPE_LANE0) & (lane < ROPE_LANE0 + MLA_ROPE)

    def sub_tile(r0):
        rows = slice(r0, r0 + INPROJ_SUB)
        h = (_rms(x_ref[rows, :], nw_ref[...]) * (1.0 + sc1) + sh1).astype(BF16)
        yield
        z_all = jnp.dot(h, win_ref[...], preferred_element_type=F32)
        z = lambda lo, n: z_all[:, lo:lo + n]
        yield
        qn = _rms(z(Z_Q, Q_LORA), qn_ref[...]).astype(BF16)
        ckv = _rms(z(Z_KV, KV_LORA), kvn_ref[...])
        ckv_b = ckv.astype(BF16)
        misc = z(Z_MISC, LANES)
        yield
        q = jnp.dot(qn, wuq_ref[...], preferred_element_type=F32)
        kn = jnp.dot(ckv_b, wk_ref[...], preferred_element_type=F32)
        vt_ref[:, rows] = lax.dot_general(wvt_ref[...], ckv_b, _NT,
                                          preferred_element_type=F32).astype(BF16)
        gpre = jnp.dot(misc.astype(BF16), wg_ref[...], preferred_element_type=F32)
        yield
        if latent:
            cos, sa, sb = cos_ref[rows, :], sa_ref[rows, :], sb_ref[rows, :]

            def rope(t):
                return t * cos + pltpu.roll(t, LANES - 8, 1) * sa + pltpu.roll(t, 8, 1) * sb
        else:
            def rope(t):
                return t

        def put(ref, val, cols=slice(None)):
            if len(ref.shape) == 2:
                ref[rows, cols] = val
            else:
                for b in range(INPROJ_SUB // seq):
                    ref[r0 // seq + b, :, cols] = val[b * seq:(b + 1) * seq, :]

        krope = rope(misc)
        for hd in range(MLA_HEADS):
            sl = slice(hd * HEAD_PAD, (hd + 1) * HEAD_PAD)
            put(q_ref, (rope(q[:, sl]) * scale).astype(BF16), sl)
            put(k_ref, jnp.where(in_rope, krope, kn[:, sl]).astype(BF16), sl)
        put(gq_ref, z(Z_GQ, GQK))
        put(gk_ref, z(Z_GK, GQK))
        put(gv_ref, z(Z_GV, GV).astype(BF16))
        put(go_ref, z(Z_GO, GV))
        put(gf_ref, _log_sigmoid(gpre[:, :GQK] + bgf_ref[...]) * (1.0 / GATE_NORM))
        put(gb_ref, _log_sigmoid(gpre[:, GQK:] + bgb_ref[...]) * (1.0 / GATE_NORM))
        if not latent:
            ckv_ref[rows, :] = ckv
            misc_t = misc.T
            n = krt_ref.shape[2]
            for b in range(INPROJ_SUB // n):
                krt_ref[r0 // n + b] = misc_t[ROPE_LANE0:ROPE_LANE0 + MLA_ROPE, b * n:(b + 1) * n]

    _interleave([sub_tile(r0) for r0 in range(0, x_ref.shape[0], INPROJ_SUB)])
    if not latent:
        n_seqs = q_ref.shape[0]
        _gla_body((gq_ref, gk_ref, gv_ref, gf_ref, gb_ref, go_ref), None, gn_ref,
                  (gla_ref, sf_ref, sb_ref), gla_scratch, seq // GLA_TILE, n_seqs,
                  [_attn_pipeline(q_ref, k_ref, vt_ref, attn_ref, st_ref, p_ref, None, [b], 2 * b)
                   for b in range(n_seqs)])


def _inproj(x2d, mod, mod_row, weights, rope_tabs, tm, seq_len, gla_norm=None):
    n_tok, d = x2d.shape
    latent = rope_tabs is not None
    tiles_per_seq = max(seq_len // tm, 1)
    nw, win, qn, wuq, kvn, wk, wvt, wg, bgf, bgb = weights
    row = lambda i: (i, 0)
    in_specs = [pl.BlockSpec((tm, d), row), _const_spec(mod.shape),
                _const_spec(nw.shape), _const_spec(win.shape), _const_spec(qn.shape),
                _const_spec(wuq.shape), _const_spec(kvn.shape), _const_spec(wk.shape),
                _const_spec(wvt.shape), _const_spec(wg.shape), _const_spec(bgf.shape),
                _const_spec(bgb.shape)]
    args = [x2d, mod, nw, win, qn, wuq, kvn, wk, wvt, wg, bgf, bgb]
    if latent:
        tab = pl.BlockSpec((tm, LANES), lambda i: (i % tiles_per_seq, 0))
        in_specs += [tab, tab, tab]
        args += list(rope_tabs)
    per_token = lambda c, dt: (jax.ShapeDtypeStruct((n_tok, c), dt), pl.BlockSpec((tm, c), row))
    seq = tm if latent else seq_len
    per_seq = lambda r, c, dt: (jax.ShapeDtypeStruct((n_tok // seq, r, c), dt),
                                pl.BlockSpec((tm // seq, r, c), lambda i: (i, 0, 0)))
    gla_outs = [per_token(GQK, F32), per_token(GQK, F32), per_token(GV, BF16),
                per_token(GQK, F32), per_token(GQK, F32), per_token(GV, F32)]
    scratch = []
    if latent:
        outs = [per_seq(seq, QPAD, BF16), per_seq(seq, QPAD, BF16),
                (jax.ShapeDtypeStruct((VALL, n_tok), BF16), pl.BlockSpec((VALL, tm), lambda i: (0, i)))]
        outs += gla_outs
    else:
        in_specs.append(_const_spec(gla_norm.shape))
        args.append(gla_norm)
        state = (jax.ShapeDtypeStruct((n_tok // seq, GLA_HEADS, GLA_DK, GLA_DV), F32),
                 pl.BlockSpec((tm // seq, GLA_HEADS, GLA_DK, GLA_DV), lambda i: (i, 0, 0, 0)))
        outs = [per_seq(seq, VALL, BF16), per_seq(seq, GV, BF16), state, state,
                per_token(KV_LORA, F32), per_seq(MLA_ROPE, seq, F32)]
        vmem = lambda c, dt: pltpu.VMEM((tm // seq, seq, c), dt)
        scratch = [vmem(QPAD, BF16), vmem(QPAD, BF16), pltpu.VMEM((VALL, tm), BF16),
                   pltpu.VMEM((2 * (tm // seq), seq, Q_TILE), F32),
                   pltpu.VMEM((2 * (tm // seq), seq, Q_TILE), BF16),
                   vmem(GQK, F32), vmem(GQK, F32), vmem(GV, BF16), vmem(GQK, F32), vmem(GQK, F32),
                   vmem(GV, F32)] + _gla_scratch(tm // seq, seq)
    return pl.pallas_call(
        functools.partial(_inproj_kernel, latent=latent, mod_row=mod_row),
        out_shape=[o[0] for o in outs],
        grid=(n_tok // tm,),
        in_specs=in_specs,
        out_specs=[o[1] for o in outs],
        scratch_shapes=scratch,
        name="inproj_lat" if latent else "inproj_ctx",
        compiler_params=pltpu.CompilerParams(dimension_semantics=("arbitrary",),
                                             vmem_limit_bytes=VMEM_LIMIT),
    )(*args)


def _decomp_kernel(ckv_ref, krt_ref, wk_ref, wvt_ref, k_ref, vt_ref):
    ckv_b = ckv_ref[...].astype(BF16)
    kn = jnp.dot(ckv_b, wk_ref[...], preferred_element_type=F32)
    n_keys = krt_ref.shape[1]
    kr = jnp.concatenate([jnp.zeros((ROPE_LANE0, n_keys), F32), krt_ref[...],
                          jnp.zeros((LANES - ROPE_LANE0 - MLA_ROPE, n_keys), F32)], axis=0).T
    lane = lax.broadcasted_iota(jnp.int32, kr.shape, 1)
    in_rope = (lane >= ROPE_LANE0) & (lane < ROPE_LANE0 + MLA_ROPE)
    for hd in range(MLA_HEADS):
        sl = slice(hd * HEAD_PAD, (hd + 1) * HEAD_PAD)
        k_ref[:, sl] = jnp.where(in_rope, kr, kn[:, sl]).astype(BF16)
    vt_ref[...] = lax.dot_general(wvt_ref[...], ckv_b, _NT, preferred_element_type=F32).astype(BF16)


def _decomp(ckv, kr_t, wk, wvt):
    b, s, _ = ckv.shape
    return pl.pallas_call(
        _decomp_kernel,
        out_shape=[jax.ShapeDtypeStruct((b, s, QPAD), BF16), jax.ShapeDtypeStruct((VALL, b * s), BF16)],
        grid=(b,),
        in_specs=[pl.BlockSpec((None, s, KV_LORA), lambda i: (i, 0, 0)),
                  pl.BlockSpec((None, MLA_ROPE, s), lambda i: (i, 0, 0)),
                  _const_spec(wk.shape), _const_spec(wvt.shape)],
        out_specs=[pl.BlockSpec((None, s, QPAD), lambda i: (i, 0, 0)),
                   pl.BlockSpec((VALL, s), lambda i: (0, i))],
        name="ctx_decompress",
        compiler_params=pltpu.CompilerParams(dimension_semantics=("arbitrary",)),
    )(ckv, kr_t, wk, wvt)


def _attn_kernel(*refs, has_ctx, n_seqs, n_side):
    n_in = 5 if has_ctx else 3
    side_in, refs = refs[n_in:n_in + n_side], refs[:n_in] + refs[n_in + n_side:]
    side_out, refs = refs[n_in + 1:n_in + 1 + n_side], refs[:n_in + 1] + refs[n_in + 1 + n_side:]
    if has_ctx:
        q_ref, kc_ref, vct_ref, k_ref, vt_ref, o_ref, st_ref, p_ref = refs
    else:
        q_ref, k_ref, vt_ref, o_ref, st_ref, p_ref = refs

    for src, dst in zip(side_in, side_out):
        dst[...] = src[...].astype(BF16)
    _interleave([_attn_pipeline(q_ref, k_ref, vt_ref, o_ref, st_ref, p_ref,
                                (kc_ref, vct_ref) if has_ctx else None, range(n_seqs))])


def _attn_pipeline(q_ref, k_ref, vt_ref, o_ref, st_ref, p_ref, ctx_refs, seqs, slot0=0):
    tq = Q_TILE

    def key_blocks(bi):
        srcs = [ctx_refs] if ctx_refs is not None else []
        blocks, row0 = [], 0
        for kr, vr in srcs + [(k_ref, vt_ref)]:
            n_keys = kr.shape[1]
            size = min(KEY_BLOCK, n_keys)
            for r in range(0, n_keys, size):
                blocks.append((kr, vr, r, bi * n_keys + r, size, row0))
                row0 += size
        return blocks

    units = [(bi, slice(q0, q0 + tq), hd) for bi in seqs
             for q0 in range(0, q_ref.shape[1], tq) for hd in range(MLA_HEADS)]
    col_max = [None] * len(units)
    pair = []
    for stage in range(len(units) + 2):
        ua, ub, uc = stage, stage - 1, stage - 2
        run_max = None
        acc = jnp.zeros((MLA_V + ONES_ROWS, tq), F32)
        for j in range(len(key_blocks(0))):
            if ua < len(units):
                bi, qrows, hd = units[ua]
                kr, _, r0, _, size, srow = key_blocks(bi)[j]
                sl = slice(hd * HEAD_PAD, (hd + 1) * HEAD_PAD)
                st = lax.dot_general(kr[bi, r0:r0 + size, sl], q_ref[bi, qrows, sl], _NT,
                                     preferred_element_type=F32)
                st_ref[slot0 + ua % 2, srow:srow + size, :] = st
                blk_max = jnp.max(st.reshape(size // 8, 8, tq), axis=0)
                run_max = blk_max if run_max is None else jnp.maximum(run_max, blk_max)
            if 0 <= ub < len(units):
                _, _, _, _, size, srow = key_blocks(0)[j]
                p_ref[slot0 + ub % 2, srow:srow + size, :] = jnp.exp2(
                    st_ref[slot0 + ub % 2, srow:srow + size, :] - col_max[ub]).astype(BF16)
            if uc >= 0:
                bi, _, hd = units[uc]
                _, vr, _, c0, size, srow = key_blocks(bi)[j]
                v_aug = jnp.concatenate([vr[hd * MLA_V:(hd + 1) * MLA_V, c0:c0 + size],
                                         jnp.ones((ONES_ROWS, size), BF16)], axis=0)
                acc = acc + jnp.dot(v_aug, p_ref[slot0 + uc % 2, srow:srow + size, :],
                                    preferred_element_type=F32)
        if ua < len(units):
            col_max[ua] = jnp.max(run_max, axis=0, keepdims=True)
        if uc >= 0:
            bi, qrows, hd = units[uc]
            pair.append(acc[:MLA_V, :] / acc[MLA_V:MLA_V + 1, :])
            if len(pair) == 2:
                o_ref[bi, qrows, (hd - 1) * MLA_V:(hd + 1) * MLA_V] = (
                    jnp.concatenate(pair, axis=0).T.astype(BF16))
                pair = []
        yield


def _attention(q, k, vt, ctx_kv, tq, n_seqs, side_weights=()):
    b, t, _ = q.shape
    steps = (b // n_seqs) * (t // tq)
    assert (n_seqs == 1 or tq == t) and tq % Q_TILE == 0
    has_ctx = ctx_kv is not None
    in_specs = [pl.BlockSpec((n_seqs, tq, QPAD), lambda i, j: (i, j, 0))]
    args = [q]
    if has_ctx:
        kc, vct = ctx_kv
        s = kc.shape[1]
        in_specs += [pl.BlockSpec((n_seqs, s, QPAD), lambda i, j: (i, 0, 0)),
                     pl.BlockSpec((VALL, n_seqs * s), lambda i, j: (0, i))]
        args += [kc, vct]
    in_specs += [pl.BlockSpec((n_seqs, t, QPAD), lambda i, j: (i, 0, 0)),
                 pl.BlockSpec((VALL, n_seqs * t), lambda i, j: (0, i))]
    args += [k, vt]
    out_shape = [jax.ShapeDtypeStruct((b, t, VALL), BF16)]
    out_specs = [pl.BlockSpec((n_seqs, tq, VALL), lambda i, j: (i, j, 0))]
    nj = t // tq
    for w in side_weights:
        _, rows, cols = w.shape
        assert rows % (16 * steps) == 0
        in_specs.append(pl.BlockSpec((None, rows // steps, cols), lambda i, j: (0, i * nj + j, 0)))
        args.append(w)
        out_shape.append(jax.ShapeDtypeStruct((rows, cols), BF16))
        out_specs.append(pl.BlockSpec((rows // steps, cols), lambda i, j: (i * nj + j, 0)))
    return pl.pallas_call(
        functools.partial(_attn_kernel, has_ctx=has_ctx, n_seqs=n_seqs, n_side=len(side_weights)),
        out_shape=out_shape,
        grid=(b // n_seqs, t // tq),
        in_specs=in_specs,
        out_specs=out_specs,
        scratch_shapes=[pltpu.VMEM((2, t + (s if has_ctx else 0), Q_TILE), F32),
                        pltpu.VMEM((2, t + (s if has_ctx else 0), Q_TILE), BF16)],
        name="mla_attn_lat" if has_ctx else "mla_attn_ctx",
        compiler_params=pltpu.CompilerParams(dimension_semantics=("arbitrary", "arbitrary"),
                                             vmem_limit_bytes=VMEM_LIMIT),
    )(*args)


def _gla_kernel(*refs, n_tiles, n_seqs, zero_init):
    init_refs, rest = (None, refs[6:]) if zero_init else (refs[6:8], refs[8:])
    _gla_body(refs[:6], init_refs, rest[0], rest[1:4], rest[4:], n_tiles, n_seqs)


def _gla_scratch(n_seqs, t):
    return [pltpu.VMEM((2, n_seqs, t, GV), F32),
            pltpu.VMEM((GQK, GLA_TILE), BF16),
            pltpu.VMEM((2, GLA_TILE, GLA_TILE), F32),
            pltpu.VMEM((GLA_HEADS * GLA_TILE, GQK), BF16)]


def _gla_body(in_refs, init_refs, gn_ref, out_refs, scratch_refs, n_tiles, n_seqs, side_chains=()):
    gq_ref, gk_ref, gv_ref, gf_ref, gb_ref, go_ref = in_refs
    o_ref, sf_ref, sb_ref = out_refs
    oacc_ref, bdqk_ref, tri_ref, hm_ref = scratch_refs
    g_refs = (gf_ref, gb_ref)
    state_refs = (sf_ref, sb_ref)

    @pl.when(pl.program_id(0) == 0)
    def _():
        ri = lax.broadcasted_iota(jnp.int32, (GLA_TILE, GLA_TILE), 0)
        ci = lax.broadcasted_iota(jnp.int32, (GLA_TILE, GLA_TILE), 1)
        same_chunk = (ri // CHUNK) == (ci // CHUNK)
        bdqk_ref[...] = jnp.where(same_chunk, 1.0, 0.0).astype(BF16)
        tri_ref[0] = jnp.where(same_chunk & (ri >= ci), 1.0, 0.0)
        tri_ref[1] = jnp.where(same_chunk & (ci >= ri), 1.0, 0.0)
        hm_ref[...] = jnp.where(
            lax.broadcasted_iota(jnp.int32, (GLA_HEADS * GLA_TILE, GQK), 0) // GLA_TILE
            == lax.broadcasted_iota(jnp.int32, (GLA_HEADS * GLA_TILE, GQK), 1) // GLA_DK,
            1.0, 0.0).astype(BF16)

    row8 = lax.broadcasted_iota(jnp.int32, (8, GQK), 0)

    def row_slice(start, size):
        return pl.ds(start if isinstance(start, int) else pl.multiple_of(start, size), size)

    def tile_rows(t):
        return row_slice(t * GLA_TILE, GLA_TILE)

    finished = set()

    def total_row(c, d):
        return c * CHUNK + (CHUNK - 1 if d == 0 else 0)

    def tile_dir(b, t, d):
        rows = tile_rows(t)
        g = g_refs[d][b, rows, :]
        g_hi = g.astype(BF16)
        g_lo = (g - g_hi.astype(F32)).astype(BF16)
        tri_b = tri_ref[d].astype(BF16)
        cum = (jnp.dot(tri_b, g_hi, preferred_element_type=F32)
               + jnp.dot(tri_b, g_lo, preferred_element_type=F32))
        yield
        totals = [cum[total_row(c, d):total_row(c, d) + 1, :] for c in range(CHUNKS_PER_TILE)]
        tot8 = jnp.zeros((8, GQK), F32)
        for c in range(CHUNKS_PER_TILE):
            tot8 = jnp.where(row8 == c, totals[c], tot8)
        dec_t = jnp.concatenate([jnp.exp(tot8), jnp.zeros((LANES - 8, GQK), F32)], axis=0).T
        q = gq_ref[b, rows, :] * (GLA_DK ** -0.5)
        k = gk_ref[b, rows, :]
        v = gv_ref[b, rows, :]
        tot = jnp.concatenate([jnp.broadcast_to(tc, (CHUNK, GQK)) for tc in totals], axis=0)
        qe = (q * jnp.exp(cum)).astype(BF16)
        ke = (k * jnp.exp(-cum)).astype(BF16)
        kd_t = (k * jnp.exp(tot - cum)).T.astype(BF16)
        bd_qk = bdqk_ref[...] > 0
        tri = tri_ref[d] > 0

        qm = jnp.where(hm_ref[...] > 0, jnp.tile(qe, (GLA_HEADS, 1)), 0.0)
        yield
        att = lax.dot_general(qm, ke, _NT, preferred_element_type=F32)

        yield
        intra, upd = [], []
        for hd in range(GLA_HEADS):
            vh = v[:, hd * GLA_DV:(hd + 1) * GLA_DV]
            a_h = jnp.where(tri, att[hd * GLA_TILE:(hd + 1) * GLA_TILE, :], 0.0).astype(BF16)
            intra.append(jnp.dot(a_h, vh, preferred_element_type=F32))
            kd_h = jnp.tile(kd_t[hd * GLA_DK:(hd + 1) * GLA_DK, :], (CHUNKS_PER_TILE, 1))
            upd.append(jnp.dot(jnp.where(bd_qk, kd_h, 0.0), vh, preferred_element_type=F32))

        yield
        state = [state_refs[d][b, hd] for hd in range(GLA_HEADS)]
        order = range(CHUNKS_PER_TILE) if d == 0 else range(CHUNKS_PER_TILE - 1, -1, -1)
        seen = {}
        for c in order:
            seen[c] = jnp.concatenate(state, axis=0).astype(BF16)
            decay = jnp.broadcast_to(dec_t[:, c:c + 1], (GQK, GLA_DV))
            for hd in range(GLA_HEADS):
                ks = slice(hd * GLA_DK, (hd + 1) * GLA_DK)
                state[hd] = decay[ks, :] * state[hd] + upd[hd][c * CHUNK:(c + 1) * CHUNK, :]
        for hd in range(GLA_HEADS):
            state_refs[d][b, hd] = state[hd]

        yield
        for c in range(CHUNKS_PER_TILE):
            cr = slice(c * CHUNK, (c + 1) * CHUNK)
            q_c = jnp.concatenate([qm[hd * GLA_TILE + c * CHUNK:hd * GLA_TILE + (c + 1) * CHUNK, :]
                                   for hd in range(GLA_HEADS)], axis=0)
            inter = jnp.dot(q_c, seen[c], preferred_element_type=F32)
            o = jnp.concatenate([intra[hd][cr, :] + inter[hd * CHUNK:(hd + 1) * CHUNK, :]
                                 for hd in range(GLA_HEADS)], axis=1)
            oacc_ref[d, b, row_slice(t * GLA_TILE + c * CHUNK, CHUNK), :] = o
        if isinstance(t, int):
            finished.add((b, t, d))

    if init_refs is None:
        sf_ref[...] = jnp.zeros(sf_ref.shape, F32)
        sb_ref[...] = jnp.zeros(sb_ref.shape, F32)
    else:
        sf_ref[...] = init_refs[0][...]
        sb_ref[...] = init_refs[1][...]

    gn = gn_ref[...]

    def epilogue(t):
        while isinstance(t, int) and not all((b, t, d) in finished for b in range(n_seqs) for d in range(2)):
            yield
        rows = tile_rows(t)
        for b in range(n_seqs):
            for hd in range(GLA_HEADS):
                vs = slice(hd * GLA_DV, (hd + 1) * GLA_DV)
                o = _rms(oacc_ref[0, b, rows, vs] + oacc_ref[1, b, rows, vs], gn)
                go = go_ref[b, rows, vs]
                o_ref[b, rows, vs] = (o * (go * _sigmoid(go))).astype(BF16)
        return
        yield

    def main_chains(first_tile, n):
        chains = []
        for t in [first_tile + u for u in range(n)]:
            for b in range(n_seqs):
                chains += [tile_dir(b, t, 0), tile_dir(b, n_tiles - 1 - t, 1)]
        return chains

    if n_tiles <= GLA_STATIC_TILES:
        _interleave(list(side_chains) + main_chains(0, n_tiles) + [epilogue(t) for t in range(n_tiles)])
    else:
        assert not side_chains
        per_step = next(c for c in (4, 2, 1) if n_tiles % c == 0)

        def main_body(i, carry):
            _interleave(main_chains(i * per_step, per_step))
            return carry

        lax.fori_loop(0, n_tiles // per_step, main_body, 0)

        def epilogue_body(t, carry):
            _interleave([epilogue(t)])
            return carry

        lax.fori_loop(0, n_tiles, epilogue_body, 0)


def _gla(gq, gk, gv, gf, gb, go, init_states, gn, n_seqs):
    b, t, _ = gq.shape
    n_tiles = t // GLA_TILE
    zero_init = init_states is None
    seq = lambda c: pl.BlockSpec((n_seqs, t, c), lambda i: (i, 0, 0))
    st = pl.BlockSpec((n_seqs, GLA_HEADS, GLA_DK, GLA_DV), lambda i: (i, 0, 0, 0))
    in_specs = [seq(GQK), seq(GQK), seq(GV), seq(GQK), seq(GQK), seq(GV)]
    args = [gq, gk, gv, gf, gb, go]
    if not zero_init:
        in_specs += [st, st]
        args += list(init_states)
    in_specs.append(_const_spec(gn.shape))
    args.append(gn)
    return pl.pallas_call(
        functools.partial(_gla_kernel, n_tiles=n_tiles, n_seqs=n_seqs, zero_init=zero_init),
        out_shape=[jax.ShapeDtypeStruct((b, t, GV), BF16),
                   jax.ShapeDtypeStruct((b, GLA_HEADS, GLA_DK, GLA_DV), F32),
                   jax.ShapeDtypeStruct((b, GLA_HEADS, GLA_DK, GLA_DV), F32)],
        grid=(b // n_seqs,),
        in_specs=in_specs,
        out_specs=[seq(GV), st, st],
        scratch_shapes=_gla_scratch(n_seqs, t),
        name="gla_%d" % t,
        compiler_params=pltpu.CompilerParams(dimension_semantics=("arbitrary",),
                                             vmem_limit_bytes=VMEM_LIMIT),
    )(*args)


def _ffn_kernel(xp_ref, xs_ref, atp_ref, ats_ref, glp_ref, gls_ref, mod_ref, wout_ref, nf_ref,
                wfi_ref, wfo_ref, fn_ref, yp_ref, ys_ref, act_ref, *, ctx_tiles, tiles_per_seq):
    def sub_tile(x_ref, at_ref, gl_ref, y_ref, mod_row, r0):
        rows = slice(r0, r0 + FFN_SUB)
        gt1, sh2, sc2, gt2 = _mod_rows(mod_ref, mod_row)[2:]
        mix = (jnp.dot(at_ref[rows, :], wout_ref[0:VALL, :], preferred_element_type=F32)
               + jnp.dot(gl_ref[rows, :], wout_ref[VALL:, :], preferred_element_type=F32))
        yield
        x1 = x_ref[rows, :] + gt1 * mix
        h2 = (_rms(x1, nf_ref[...]) * (1.0 + sc2) + sh2).astype(BF16)
        yield
        for j in range(N_FF_CHUNKS):
            cs = slice(j * FF_CHUNK, (j + 1) * FF_CHUNK)
            a = jnp.dot(h2, wfi_ref[:, cs], preferred_element_type=F32)
            g = jnp.dot(h2, wfi_ref[:, D_FF + j * FF_CHUNK:D_FF + (j + 1) * FF_CHUNK],
                        preferred_element_type=F32)
            act_ref[rows, cs] = (a * _sigmoid(a) * g).astype(BF16)
            if j in FFN_PHASE_ENDS:
                yield
        ff = jnp.dot(act_ref[rows, :], wfo_ref[...], preferred_element_type=F32)
        yield
        x2 = x1 + gt2 * ff
        y_ref[rows, :] = _rms(x2, fn_ref[...])

    def tile(x_ref, at_ref, gl_ref, y_ref, mod_row):
        _interleave([sub_tile(x_ref, at_ref, gl_ref, y_ref, mod_row, r0)
                     for r0 in range(0, x_ref.shape[0], FFN_SUB)])

    t = pl.program_id(0)

    @pl.when(t < ctx_tiles)
    def _():
        tile(xp_ref, atp_ref, glp_ref, yp_ref, 0)

    @pl.when(t >= ctx_tiles)
    def _():
        tile(xs_ref, ats_ref, gls_ref, ys_ref, 1 + (t - ctx_tiles) // tiles_per_seq)


def _ffn(xp, xs, attn_p, attn_s, gla_p, gla_s, mod, wout, nf, wfi, wfo, fn, tm, tiles_per_seq):
    d = xp.shape[1]
    ctx_tiles = xp.shape[0] // tm
    lat_tiles = xs.shape[0] // tm
    ctx_map = lambda s: (jnp.minimum(s, ctx_tiles - 1), 0)
    lat_map = lambda s: (jnp.maximum(s - ctx_tiles, 0), 0)
    tile = lambda c, m: pl.BlockSpec((tm, c), m)
    return pl.pallas_call(
        functools.partial(_ffn_kernel, ctx_tiles=ctx_tiles, tiles_per_seq=tiles_per_seq),
        out_shape=[jax.ShapeDtypeStruct(xp.shape, F32), jax.ShapeDtypeStruct(xs.shape, F32)],
        grid=(ctx_tiles + lat_tiles,),
        in_specs=[tile(d, ctx_map), tile(d, lat_map), tile(VALL, ctx_map), tile(VALL, lat_map),
                  tile(GV, ctx_map), tile(GV, lat_map), _const_spec(mod.shape),
                  _const_spec(wout.shape), _const_spec(nf.shape), _const_spec(wfi.shape),
                  _const_spec(wfo.shape), _const_spec(fn.shape)],
        out_specs=[tile(d, ctx_map), tile(d, lat_map)],
        scratch_shapes=[pltpu.VMEM((tm, D_FF), BF16)],
        name="out_ffn",
        compiler_params=pltpu.CompilerParams(dimension_semantics=("arbitrary",),
                                             vmem_limit_bytes=VMEM_LIMIT),
    )(xp, xs, attn_p, attn_s, gla_p, gla_s, mod, wout, nf, wfi, wfo, fn)


def _rope_tables(n_tokens):
    t = np.arange(n_tokens)
    row = (t // GRID_W).astype(np.float32)
    col = (t % GRID_W).astype(np.float32)
    half = MLA_ROPE // 2
    inv = (np.float32(ROPE_BASE) ** (-np.arange(0, half, 2, dtype=np.float32) / np.float32(half))).astype(np.float32)
    ang_r = row[:, None] * inv
    ang_c = col[:, None] * inv
    ang = np.concatenate([ang_r, ang_r, ang_c, ang_c], axis=-1).astype(np.float32)
    cos, sin = np.cos(ang), np.sin(ang)
    first = (np.arange(MLA_ROPE) % half) < (half // 2)
    cos_t = np.ones((n_tokens, LANES), np.float32)
    sa_t = np.zeros((n_tokens, LANES), np.float32)
    sb_t = np.zeros((n_tokens, LANES), np.float32)
    cos_t[:, ROPE_LANE0:ROPE_LANE0 + MLA_ROPE] = cos
    sa_t[:, ROPE_LANE0:ROPE_LANE0 + MLA_ROPE] = np.where(first, -sin, 0.0)
    sb_t[:, ROPE_LANE0:ROPE_LANE0 + MLA_ROPE] = np.where(first, 0.0, sin)
    return jnp.asarray(cos_t), jnp.asarray(sa_t), jnp.asarray(sb_t)


def kernel(x_prompt, x_sample, cache_kv_latent, cache_k_rope, state_gla_fwd, state_gla_bwd, c, c_ctx, w_ada, b_ada, norm_attn, w_in, mla_q_norm, w_uq, mla_kv_norm, w_ukv, w_gate_f, b_gate_f, w_gate_b, b_gate_b, gla_norm, w_out, norm_ffn, w_ffn_in, w_ffn_out, final_norm):
    batch, seq, d = x_prompt.shape
    dec_batch, dec_seq, _ = x_sample.shape
    assert w_ada.shape[0] == 1 and w_in.shape[-1] == W_COLS and w_ffn_in.shape[-1] == 2 * D_FF
    l = 0

    mod = _ada(c_ctx, c, w_ada[l], b_ada[l])

    win, wuq, wk, wvt, wg = _prep_in_weights(w_in, w_uq, w_ukv, w_gate_f, w_gate_b)
    in_w = (norm_attn[l].reshape(1, d), win, mla_q_norm[l].reshape(1, Q_LORA), wuq,
            mla_kv_norm[l].reshape(1, KV_LORA), wk, wvt, wg, b_gate_f, b_gate_b)
    gn = gla_norm[l].reshape(1, GLA_DV)
    tm, tm_ffn = 512, 512
    r3 = lambda a, b_, t: a.reshape(b_, t, a.shape[-1])

    xp = x_prompt.reshape(batch * seq, d)
    (attn_p, gla_p, sf, sb, ckv, kr_t) = _inproj(xp, mod, lambda i: 0, in_w, None, tm, seq, gn)

    xs = x_sample.reshape(dec_batch * dec_seq, d)
    tiles = dec_seq // tm
    (q, k, vt, gq, gk, gv, gf, gb, go) = _inproj(xs, mod, lambda i: 1 + i // tiles, in_w,
                                                  _rope_tables(dec_seq), tm, dec_seq)
    kc, vct = _decomp(cache_kv_latent[:, l], jnp.swapaxes(cache_k_rope[:, l], 1, 2), wk, wvt)
    attn_s, wout, wfi, wfo = _attention(r3(q, dec_batch, dec_seq), r3(k, dec_batch, dec_seq), vt, (kc, vct),
                                        ATTN_LAT_QUERIES, 1, (w_out, w_ffn_in, w_ffn_out))
    gla_s, _, _ = _gla(r3(gq, dec_batch, dec_seq), r3(gk, dec_batch, dec_seq), r3(gv, dec_batch, dec_seq),
                       r3(gf, dec_batch, dec_seq), r3(gb, dec_batch, dec_seq), r3(go, dec_batch, dec_seq),
                       (state_gla_fwd[:, l].astype(F32), state_gla_bwd[:, l].astype(F32)), gn, 1)

    flat = lambda a: a.reshape(-1, a.shape[-1])
    y_prompt, y_sample = _ffn(xp, xs, flat(attn_p), flat(attn_s), flat(gla_p), flat(gla_s), mod,
                              wout, norm_ffn[l].reshape(1, d), wfi, wfo,
                              final_norm.reshape(1, d), tm_ffn, dec_seq // tm_ffn)
    y_prompt = y_prompt.reshape(batch, seq, d)
    y_sample = y_sample.reshape(dec_batch, dec_seq, d)

    new_kv_latent = ckv.reshape(batch, 1, seq, KV_LORA)
    new_k_rope = jnp.swapaxes(kr_t, 1, 2).reshape(batch, 1, seq, MLA_ROPE)
    new_state_fwd = sf.reshape(batch, 1, GLA_HEADS, GLA_DK, GLA_DV).astype(x_prompt.dtype)
    new_state_bwd = sb.reshape(batch, 1, GLA_HEADS, GLA_DK, GLA_DV).astype(x_prompt.dtype)
    return (y_prompt, y_sample, new_kv_latent, new_k_rope, new_state_fwd, new_state_bwd)
```

```python
import functools

import numpy as np
import jax
import jax.numpy as jnp
from jax import lax
from jax.experimental import pallas as pl
from jax.experimental.pallas import tpu as pltpu

F32 = jnp.float32
BF16 = jnp.bfloat16

GRID_W = 64
MLA_HEADS = 8
MLA_NOPE = 64
MLA_ROPE = 32
MLA_QK = MLA_NOPE + MLA_ROPE
MLA_V = 64
Q_LORA = 384
KV_LORA = 256
GLA_HEADS = 4
GLA_DK = 64
GLA_DV = 128
GATE_RANK = 16
GATE_NORM = 16.0
CHUNK = 64
D_FF = 2816
ROPE_BASE = 10000.0
EPS = 1e-6
LOG2_E = 1.4426950408889634

LANES = 128
HEAD_PAD = LANES
ROPE_LANE0 = MLA_NOPE
GQK = GLA_HEADS * GLA_DK
GV = GLA_HEADS * GLA_DV
QPAD = MLA_HEADS * HEAD_PAD
VALL = MLA_HEADS * MLA_V
ONES_ROWS = 16
KEY_BLOCK = 1024

W_KR = Q_LORA + KV_LORA
W_GQ = W_KR + MLA_ROPE
W_GF = W_GQ + 2 * GQK + GV
W_GO = W_GF + 2 * GATE_RANK
W_COLS = W_GO + GV

Z_Q = 0
Z_KV = Z_Q + Q_LORA
Z_GQ = Z_KV + KV_LORA
Z_GK = Z_GQ + GQK
Z_GV = Z_GK + GQK
Z_GO = Z_GV + GV
Z_MISC = Z_GO + GV
Z_COLS = Z_MISC + LANES

FF_CHUNK = 256
N_FF_CHUNKS = D_FF // FF_CHUNK
FFN_SUB = 256
FFN_PHASE_ENDS = (3, 7, 10)

GLA_TILE = 256
CHUNKS_PER_TILE = GLA_TILE // CHUNK
ADA_ROWS = 128
INPROJ_SUB = 512
Q_TILE = 256
ATTN_LAT_QUERIES = 512
GLA_STATIC_TILES = 8

VMEM_LIMIT = 56 * 1024 * 1024

_NT = (((1,), (1,)), ((), ()))


def _rms(x, w):
    return x * lax.rsqrt(jnp.mean(x * x, axis=-1, keepdims=True) + EPS) * w


def _sigmoid(x):
    return 1.0 / (1.0 + jnp.exp(-x))


def _log_sigmoid(x):
    return jnp.minimum(x, 0.0) - jnp.log1p(jnp.exp(-jnp.abs(x)))


def _interleave(chains):
    pending, active = list(chains), []
    while pending or active:
        if pending:
            active.append(pending.pop(0))
        for chain in list(active):
            try:
                next(chain)
            except StopIteration:
                active.remove(chain)


def _const_spec(shape):
    nd = len(shape)
    return pl.BlockSpec(shape, lambda *_: (0,) * nd, pipeline_mode=pl.Buffered(1))


def _mod_rows(mod_ref, r):
    return [mod_ref[k, pl.ds(r, 1), :] for k in range(mod_ref.shape[0])]


def _ada_kernel(cctx_ref, c_ref, w_ref, b_ref, o_ref):
    k = pl.program_id(0)
    d = o_ref.shape[2]
    row = lax.broadcasted_iota(jnp.int32, (8, cctx_ref.shape[1]), 0)
    cond = jnp.where(row == 0, cctx_ref[...], 0.0)
    for r in range(c_ref.shape[0]):
        cond = jnp.where(row == 1 + r, c_ref[r:r + 1, :], cond)
    s = (cond * _sigmoid(cond)).astype(BF16)
    part = jnp.dot(s, w_ref[...].astype(BF16), preferred_element_type=F32)
    for j in range(o_ref.shape[0]):
        sl = slice(j * d, (j + 1) * d)

        @pl.when(k == 0)
        def _():
            o_ref[j] = part[:, sl] + b_ref[:, sl]

        @pl.when(k > 0)
        def _():
            o_ref[j] += part[:, sl]


def _ada(c_ctx, c, w_ada, b_ada):
    d = w_ada.shape[0]
    n = w_ada.shape[1]
    assert 1 + c.shape[0] <= 8
    return pl.pallas_call(
        _ada_kernel,
        out_shape=jax.ShapeDtypeStruct((n // d, 8, d), F32),
        grid=(d // ADA_ROWS,),
        in_specs=[pl.BlockSpec((1, ADA_ROWS), lambda k: (0, k)),
                  pl.BlockSpec((c.shape[0], ADA_ROWS), lambda k: (0, k)),
                  pl.BlockSpec((ADA_ROWS, n), lambda k: (k, 0)),
                  pl.BlockSpec((1, n), lambda k: (0, 0))],
        out_specs=pl.BlockSpec((n // d, 8, d), lambda k: (0, 0, 0)),
        name="ada_mod",
        compiler_params=pltpu.CompilerParams(dimension_semantics=("arbitrary",)),
    )(c_ctx.reshape(1, d), c, w_ada, b_ada.reshape(1, n))


def _prep_kernel(wint_ref, wuq_ref, wukv_ref, wgf_ref, wgb_ref, win_o, wuq_o, wk_o, wvt_o, wg_o):
    cols = wint_ref.shape[1]
    for dst, src, n in ((Z_Q, 0, W_KR), (Z_GQ, W_GQ, W_GF - W_GQ), (Z_GO, W_GO, GV)):
        win_o[:, dst:dst + n] = wint_ref[src:src + n, :].T.astype(BF16)
    z32 = jnp.zeros((32, cols), F32)
    misc_t = jnp.concatenate([wint_ref[W_GF:W_GO, :], z32, wint_ref[W_KR:W_GQ, :], z32], axis=0)
    win_o[:, Z_MISC:Z_COLS] = misc_t.T.astype(BF16)

    u = wuq_ref[...]
    zq = jnp.zeros((u.shape[0], HEAD_PAD - MLA_QK), F32)
    for hd in range(MLA_HEADS):
        blk = jnp.concatenate([u[:, hd * MLA_QK:(hd + 1) * MLA_QK], zq], axis=1)
        wuq_o[:, hd * HEAD_PAD:(hd + 1) * HEAD_PAD] = blk.astype(BF16)

    @pl.when(pl.program_id(0) == 0)
    def _():
        kv = wukv_ref[...]
        per = MLA_NOPE + MLA_V
        lane = lax.broadcasted_iota(jnp.int32, (kv.shape[0], per), 1)
        for hd in range(MLA_HEADS):
            blk = kv[:, hd * per:(hd + 1) * per]
            wk_o[:, hd * HEAD_PAD:(hd + 1) * HEAD_PAD] = jnp.where(lane < MLA_NOPE, blk, 0.0).astype(BF16)
        wv = jnp.concatenate([kv[:, hd * per + MLA_NOPE:(hd + 1) * per] for hd in range(MLA_HEADS)], axis=1)
        wvt_o[...] = wv.T.astype(BF16)

        wg_o[...] = jnp.zeros(wg_o.shape, BF16)
        wg_o[0:GATE_RANK, 0:GQK] = wgf_ref[...].astype(BF16)
        wg_o[GATE_RANK:2 * GATE_RANK, GQK:2 * GQK] = wgb_ref[...].astype(BF16)


def _prep_in_weights(w_in, w_uq, w_ukv, w_gate_f, w_gate_b):
    d = w_in.shape[1]
    steps = 4
    w_in_t = jnp.swapaxes(w_in, 1, 2)
    rb3 = lambda r, c: pl.BlockSpec((None, r // steps, c), lambda i: (0, i, 0))
    rb = lambda r, c: pl.BlockSpec((r // steps, c), lambda i: (i, 0))
    full3 = lambda shape: pl.BlockSpec((None,) + tuple(shape[1:]), lambda i: (0, 0, 0))
    full = lambda shape: pl.BlockSpec(shape, lambda i: (0, 0))
    return pl.pallas_call(
        _prep_kernel,
        out_shape=[jax.ShapeDtypeStruct((d, Z_COLS), BF16),
                   jax.ShapeDtypeStruct((Q_LORA, QPAD), BF16),
                   jax.ShapeDtypeStruct((KV_LORA, QPAD), BF16),
                   jax.ShapeDtypeStruct((VALL, KV_LORA), BF16),
                   jax.ShapeDtypeStruct((LANES, 2 * GQK), BF16)],
        grid=(steps,),
        in_specs=[pl.BlockSpec((None, W_COLS, d // steps), lambda i: (0, 0, i)),
                  rb3(Q_LORA, MLA_HEADS * MLA_QK), full3(w_ukv.shape),
                  full3(w_gate_f.shape), full3(w_gate_b.shape)],
        out_specs=[rb(d, Z_COLS), rb(Q_LORA, QPAD), full((KV_LORA, QPAD)), full((VALL, KV_LORA)),
                   full((LANES, 2 * GQK))],
        name="weight_prep",
        compiler_params=pltpu.CompilerParams(dimension_semantics=("arbitrary",)),
    )(w_in_t, w_uq, w_ukv, w_gate_f, w_gate_b)


def _inproj_kernel(*refs, latent, mod_row):
    (x_ref, mod_ref, nw_ref, win_ref, qn_ref, wuq_ref, kvn_ref, wk_ref, wvt_ref, wg_ref, bgf_ref, bgb_ref) = refs[:12]
    if latent:
        cos_ref, sa_ref, sb_ref = refs[12:15]
        q_ref, k_ref, vt_ref, gq_ref, gk_ref, gv_ref, gf_ref, gb_ref, go_ref = refs[15:]
    else:
        (gn_ref, attn_ref, gla_ref, sf_ref, sb_ref, ckv_ref, krt_ref, q_ref, k_ref, vt_ref, st_ref, p_ref,
         gq_ref, gk_ref, gv_ref, gf_ref, gb_ref, go_ref) = refs[12:30]
        gla_scratch = refs[30:]
    seq = q_ref.shape[1]

    sh1, sc1 = _mod_rows(mod_ref, mod_row(pl.program_id(0)))[:2]
    scale = MLA_QK ** -0.5 * LOG2_E
    lane = lax.broadcasted_iota(jnp.int32, (INPROJ_SUB, LANES), 1)
    in_rope = (lane >= ROPE_LANE0) & (lane < ROPE_LANE0 + MLA_ROPE)

    def sub_tile(r0):
        rows = slice(r0, r0 + INPROJ_SUB)
        h = (_rms(x_ref[rows, :], nw_ref[...]) * (1.0 + sc1) + sh1).astype(BF16)
        yield
        z_all = jnp.dot(h, win_ref[...], preferred_element_type=F32)
        z = lambda lo, n: z_all[:, lo:lo + n]
        yield
        qn = _rms(z(Z_Q, Q_LORA), qn_ref[...]).astype(BF16)
        ckv = _rms(z(Z_KV, KV_LORA), kvn_ref[...])
        ckv_b = ckv.astype(BF16)
        misc = z(Z_MISC, LANES)
        yield
        q = jnp.dot(qn, wuq_ref[...], preferred_element_type=F32)
        kn = jnp.dot(ckv_b, wk_ref[...], preferred_element_type=F32)
        vt_ref[:, rows] = lax.dot_general(wvt_ref[...], ckv_b, _NT,
                                          preferred_element_type=F32).astype(BF16)
        gpre = jnp.dot(misc.astype(BF16), wg_ref[...], preferred_element_type=F32)
        yield
        if latent:
            cos, sa, sb = cos_ref[rows, :], sa_ref[rows, :], sb_ref[rows, :]

            def rope(t):
                return t * cos + pltpu.roll(t, LANES - 8, 1) * sa + pltpu.roll(t, 8, 1) * sb
        else:
            def rope(t):
                return t

        def put(ref, val, cols=slice(None)):
            if len(ref.shape) == 2:
                ref[rows, cols] = val
            else:
                for b in range(INPROJ_SUB // seq):
                    ref[r0 // seq + b, :, cols] = val[b * seq:(b + 1) * seq, :]

        krope = rope(misc)
        for hd in range(MLA_HEADS):
            sl = slice(hd * HEAD_PAD, (hd + 1) * HEAD_PAD)
            put(q_ref, (rope(q[:, sl]) * scale).astype(BF16), sl)
            put(k_ref, jnp.where(in_rope, krope, kn[:, sl]).astype(BF16), sl)
        put(gq_ref, z(Z_GQ, GQK))
        put(gk_ref, z(Z_GK, GQK))
        put(gv_ref, z(Z_GV, GV).astype(BF16))
        put(go_ref, z(Z_GO, GV))
        put(gf_ref, _log_sigmoid(gpre[:, :GQK] + bgf_ref[...]) * (1.0 / GATE_NORM))
        put(gb_ref, _log_sigmoid(gpre[:, GQK:] + bgb_ref[...]) * (1.0 / GATE_NORM))
        if not latent:
            ckv_ref[rows, :] = ckv
            misc_t = misc.T
            n = krt_ref.shape[2]
            for b in range(INPROJ_SUB // n):
                krt_ref[r0 // n + b] = misc_t[ROPE_LANE0:ROPE_LANE0 + MLA_ROPE, b * n:(b + 1) * n]

    _interleave([sub_tile(r0) for r0 in range(0, x_ref.shape[0], INPROJ_SUB)])
    if not latent:
        n_seqs = q_ref.shape[0]
        _gla_body((gq_ref, gk_ref, gv_ref, gf_ref, gb_ref, go_ref), None, gn_ref,
                  (gla_ref, sf_ref, sb_ref), gla_scratch, seq // GLA_TILE, n_seqs,
                  [_attn_pipeline(q_ref, k_ref, vt_ref, attn_ref, st_ref, p_ref, None, [b], 2 * b)
                   for b in range(n_seqs)])


def _inproj(x2d, mod, mod_row, weights, rope_tabs, tm, seq_len, gla_norm=None):
    n_tok, d = x2d.shape
    latent = rope_tabs is not None
    tiles_per_seq = max(seq_len // tm, 1)
    nw, win, qn, wuq, kvn, wk, wvt, wg, bgf, bgb = weights
    row = lambda i: (i, 0)
    in_specs = [pl.BlockSpec((tm, d), row), _const_spec(mod.shape),
                _const_spec(nw.shape), _const_spec(win.shape), _const_spec(qn.shape),
                _const_spec(wuq.shape), _const_spec(kvn.shape), _const_spec(wk.shape),
                _const_spec(wvt.shape), _const_spec(wg.shape), _const_spec(bgf.shape),
                _const_spec(bgb.shape)]
    args = [x2d, mod, nw, win, qn, wuq, kvn, wk, wvt, wg, bgf, bgb]
    if latent:
        tab = pl.BlockSpec((tm, LANES), lambda i: (i % tiles_per_seq, 0))
        in_specs += [tab, tab, tab]
        args += list(rope_tabs)
    per_token = lambda c, dt: (jax.ShapeDtypeStruct((n_tok, c), dt), pl.BlockSpec((tm, c), row))
    seq = tm if latent else seq_len
    per_seq = lambda r, c, dt: (jax.ShapeDtypeStruct((n_tok // seq, r, c), dt),
                                pl.BlockSpec((tm // seq, r, c), lambda i: (i, 0, 0)))
    gla_outs = [per_token(GQK, F32), per_token(GQK, F32), per_token(GV, BF16),
                per_token(GQK, F32), per_token(GQK, F32), per_token(GV, F32)]
    scratch = []
    if latent:
        outs = [per_seq(seq, QPAD, BF16), per_seq(seq, QPAD, BF16),
                (jax.ShapeDtypeStruct((VALL, n_tok), BF16), pl.BlockSpec((VALL, tm), lambda i: (0, i)))]
        outs += gla_outs
    else:
        in_specs.append(_const_spec(gla_norm.shape))
        args.append(gla_norm)
        state = (jax.ShapeDtypeStruct((n_tok // seq, GLA_HEADS, GLA_DK, GLA_DV), F32),
                 pl.BlockSpec((tm // seq, GLA_HEADS, GLA_DK, GLA_DV), lambda i: (i, 0, 0, 0)))
        outs = [per_seq(seq, VALL, BF16), per_seq(seq, GV, BF16), state, state,
                per_token(KV_LORA, F32), per_seq(MLA_ROPE, seq, F32)]
        vmem = lambda c, dt: pltpu.VMEM((tm // seq, seq, c), dt)
        scratch = [vmem(QPAD, BF16), vmem(QPAD, BF16), pltpu.VMEM((VALL, tm), BF16),
                   pltpu.VMEM((2 * (tm // seq), seq, Q_TILE), F32),
                   pltpu.VMEM((2 * (tm // seq), seq, Q_TILE), BF16),
                   vmem(GQK, F32), vmem(GQK, F32), vmem(GV, BF16), vmem(GQK, F32), vmem(GQK, F32),
                   vmem(GV, F32)] + _gla_scratch(tm // seq, seq)
    return pl.pallas_call(
        functools.partial(_inproj_kernel, latent=latent, mod_row=mod_row),
        out_shape=[o[0] for o in outs],
        grid=(n_tok // tm,),
        in_specs=in_specs,
        out_specs=[o[1] for o in outs],
        scratch_shapes=scratch,
        name="inproj_lat" if latent else "inproj_ctx",
        compiler_params=pltpu.CompilerParams(dimension_semantics=("arbitrary",),
                                             vmem_limit_bytes=VMEM_LIMIT),
    )(*args)


def _decomp_kernel(ckv_ref, krt_ref, wk_ref, wvt_ref, k_ref, vt_ref):
    ckv_b = ckv_ref[...].astype(BF16)
    kn = jnp.dot(ckv_b, wk_ref[...], preferred_element_type=F32)
    n_keys = krt_ref.shape[1]
    kr = jnp.concatenate([jnp.zeros((ROPE_LANE0, n_keys), F32), krt_ref[...],
                          jnp.zeros((LANES - ROPE_LANE0 - MLA_ROPE, n_keys), F32)], axis=0).T
    lane = lax.broadcasted_iota(jnp.int32, kr.shape, 1)
    in_rope = (lane >= ROPE_LANE0) & (lane < ROPE_LANE0 + MLA_ROPE)
    for hd in range(MLA_HEADS):
        sl = slice(hd * HEAD_PAD, (hd + 1) * HEAD_PAD)
        k_ref[:, sl] = jnp.where(in_rope, kr, kn[:, sl]).astype(BF16)
    vt_ref[...] = lax.dot_general(wvt_ref[...], ckv_b, _NT, preferred_element_type=F32).astype(BF16)


def _decomp(ckv, kr_t, wk, wvt):
    b, s, _ = ckv.shape
    return pl.pallas_call(
        _decomp_kernel,
        out_shape=[jax.ShapeDtypeStruct((b, s, QPAD), BF16), jax.ShapeDtypeStruct((VALL, b * s), BF16)],
        grid=(b,),
        in_specs=[pl.BlockSpec((None, s, KV_LORA), lambda i: (i, 0, 0)),
                  pl.BlockSpec((None, MLA_ROPE, s), lambda i: (i, 0, 0)),
                  _const_spec(wk.shape), _const_spec(wvt.shape)],
        out_specs=[pl.BlockSpec((None, s, QPAD), lambda i: (i, 0, 0)),
                   pl.BlockSpec((VALL, s), lambda i: (0, i))],
        name="ctx_decompress",
        compiler_params=pltpu.CompilerParams(dimension_semantics=("arbitrary",)),
    )(ckv, kr_t, wk, wvt)


def _attn_kernel(*refs, has_ctx, n_seqs, n_side):
    n_in = 5 if has_ctx else 3
    side_in, refs = refs[n_in:n_in + n_side], refs[:n_in] + refs[n_in + n_side:]
    side_out, refs = refs[n_in + 1:n_in + 1 + n_side], refs[:n_in + 1] + refs[n_in + 1 + n_side:]
    if has_ctx:
        q_ref, kc_ref, vct_ref, k_ref, vt_ref, o_ref, st_ref, p_ref = refs
    else:
        q_ref, k_ref, vt_ref, o_ref, st_ref, p_ref = refs

    for src, dst in zip(side_in, side_out):
        dst[...] = src[...].astype(BF16)
    _interleave([_attn_pipeline(q_ref, k_ref, vt_ref, o_ref, st_ref, p_ref,
                                (kc_ref, vct_ref) if has_ctx else None, range(n_seqs))])


def _attn_pipeline(q_ref, k_ref, vt_ref, o_ref, st_ref, p_ref, ctx_refs, seqs, slot0=0):
    tq = Q_TILE

    def key_blocks(bi):
        srcs = [ctx_refs] if ctx_refs is not None else []
        blocks, row0 = [], 0
        for kr, vr in srcs + [(k_ref, vt_ref)]:
            n_keys = kr.shape[1]
            size = min(KEY_BLOCK, n_keys)
            for r in range(0, n_keys, size):
                blocks.append((kr, vr, r, bi * n_keys + r, size, row0))
                row0 += size
        return blocks

    units = [(bi, slice(q0, q0 + tq), hd) for bi in seqs
             for q0 in range(0, q_ref.shape[1], tq) for hd in range(MLA_HEADS)]
    col_max = [None] * len(units)
    pair = []
    for stage in range(len(units) + 2):
        ua, ub, uc = stage, stage - 1, stage - 2
        run_max = None
        acc = jnp.zeros((MLA_V + ONES_ROWS, tq), F32)
        for j in range(len(key_blocks(0))):
            if ua < len(units):
                bi, qrows, hd = units[ua]
                kr, _, r0, _, size, srow = key_blocks(bi)[j]
                sl = slice(hd * HEAD_PAD, (hd + 1) * HEAD_PAD)
                st = lax.dot_general(kr[bi, r0:r0 + size, sl], q_ref[bi, qrows, sl], _NT,
                                     preferred_element_type=F32)
                st_ref[slot0 + ua % 2, srow:srow + size, :] = st
                blk_max = jnp.max(st.reshape(size // 8, 8, tq), axis=0)
                run_max = blk_max if run_max is None else jnp.maximum(run_max, blk_max)
            if 0 <= ub < len(units):
                _, _, _, _, size, srow = key_blocks(0)[j]
                p_ref[slot0 + ub % 2, srow:srow + size, :] = jnp.exp2(
                    st_ref[slot0 + ub % 2, srow:srow + size, :] - col_max[ub]).astype(BF16)
            if uc >= 0:
                bi, _, hd = units[uc]
                _, vr, _, c0, size, srow = key_blocks(bi)[j]
                v_aug = jnp.concatenate([vr[hd * MLA_V:(hd + 1) * MLA_V, c0:c0 + size],
                                         jnp.ones((ONES_ROWS, size), BF16)], axis=0)
                acc = acc + jnp.dot(v_aug, p_ref[slot0 + uc % 2, srow:srow + size, :],
                                    preferred_element_type=F32)
        if ua < len(units):
            col_max[ua] = jnp.max(run_max, axis=0, keepdims=True)
        if uc >= 0:
            bi, qrows, hd = units[uc]
            pair.append(acc[:MLA_V, :] / acc[MLA_V:MLA_V + 1, :])
            if len(pair) == 2:
                o_ref[bi, qrows, (hd - 1) * MLA_V:(hd + 1) * MLA_V] = (
                    jnp.concatenate(pair, axis=0).T.astype(BF16))
                pair = []
        yield


def _attention(q, k, vt, ctx_kv, tq, n_seqs, side_weights=()):
    b, t, _ = q.shape
    steps = (b // n_seqs) * (t // tq)
    assert (n_seqs == 1 or tq == t) and tq % Q_TILE == 0
    has_ctx = ctx_kv is not None
    in_specs = [pl.BlockSpec((n_seqs, tq, QPAD), lambda i, j: (i, j, 0))]
    args = [q]
    if has_ctx:
        kc, vct = ctx_kv
        s = kc.shape[1]
        in_specs += [pl.BlockSpec((n_seqs, s, QPAD), lambda i, j: (i, 0, 0)),
                     pl.BlockSpec((VALL, n_seqs * s), lambda i, j: (0, i))]
        args += [kc, vct]
    in_specs += [pl.BlockSpec((n_seqs, t, QPAD), lambda i, j: (i, 0, 0)),
                 pl.BlockSpec((VALL, n_seqs * t), lambda i, j: (0, i))]
    args += [k, vt]
    out_shape = [jax.ShapeDtypeStruct((b, t, VALL), BF16)]
    out_specs = [pl.BlockSpec((n_seqs, tq, VALL), lambda i, j: (i, j, 0))]
    nj = t // tq
    for w in side_weights:
        _, rows, cols = w.shape
        assert rows % (16 * steps) == 0
        in_specs.append(pl.BlockSpec((None, rows // steps, cols), lambda i, j: (0, i * nj + j, 0)))
        args.append(w)
        out_shape.append(jax.ShapeDtypeStruct((rows, cols), BF16))
        out_specs.append(pl.BlockSpec((rows // steps, cols), lambda i, j: (i * nj + j, 0)))
    return pl.pallas_call(
        functools.partial(_attn_kernel, has_ctx=has_ctx, n_seqs=n_seqs, n_side=len(side_weights)),
        out_shape=out_shape,
        grid=(b // n_seqs, t // tq),
        in_specs=in_specs,
        out_specs=out_specs,
        scratch_shapes=[pltpu.VMEM((2, t + (s if has_ctx else 0), Q_TILE), F32),
                        pltpu.VMEM((2, t + (s if has_ctx else 0), Q_TILE), BF16)],
        name="mla_attn_lat" if has_ctx else "mla_attn_ctx",
        compiler_params=pltpu.CompilerParams(dimension_semantics=("arbitrary", "arbitrary"),
                                             vmem_limit_bytes=VMEM_LIMIT),
    )(*args)


def _gla_kernel(*refs, n_tiles, n_seqs, zero_init):
    init_refs, rest = (None, refs[6:]) if zero_init else (refs[6:8], refs[8:])
    _gla_body(refs[:6], init_refs, rest[0], rest[1:4], rest[4:], n_tiles, n_seqs)


def _gla_scratch(n_seqs, t):
    return [pltpu.VMEM((2, n_seqs, t, GV), F32),
            pltpu.VMEM((GQK, GLA_TILE), BF16),
            pltpu.VMEM((2, GLA_TILE, GLA_TILE), F32),
            pltpu.VMEM((GLA_HEADS * GLA_TILE, GQK), BF16)]


def _gla_body(in_refs, init_refs, gn_ref, out_refs, scratch_refs, n_tiles, n_seqs, side_chains=(),
              before_tile=None):
    gq_ref, gk_ref, gv_ref, gf_ref, gb_ref, go_ref = in_refs
    o_ref, sf_ref, sb_ref = out_refs
    oacc_ref, bdqk_ref, tri_ref, hm_ref = scratch_refs
    g_refs = (gf_ref, gb_ref)
    state_refs = (sf_ref, sb_ref)

    @pl.when(pl.program_id(0) == 0)
    def _():
        ri = lax.broadcasted_iota(jnp.int32, (GLA_TILE, GLA_TILE), 0)
        ci = lax.broadcasted_iota(jnp.int32, (GLA_TILE, GLA_TILE), 1)
        same_chunk = (ri // CHUNK) == (ci // CHUNK)
        bdqk_ref[...] = jnp.where(same_chunk, 1.0, 0.0).astype(BF16)
        tri_ref[0] = jnp.where(same_chunk & (ri >= ci), 1.0, 0.0)
        tri_ref[1] = jnp.where(same_chunk & (ci >= ri), 1.0, 0.0)
        hm_ref[...] = jnp.where(
            lax.broadcasted_iota(jnp.int32, (GLA_HEADS * GLA_TILE, GQK), 0) // GLA_TILE
            == lax.broadcasted_iota(jnp.int32, (GLA_HEADS * GLA_TILE, GQK), 1) // GLA_DK,
            1.0, 0.0).astype(BF16)

    row8 = lax.broadcasted_iota(jnp.int32, (8, GQK), 0)

    def row_slice(start, size):
        return pl.ds(start if isinstance(start, int) else pl.multiple_of(start, size), size)

    def tile_rows(t):
        return row_slice(t * GLA_TILE, GLA_TILE)

    finished = set()

    def total_row(c, d):
        return c * CHUNK + (CHUNK - 1 if d == 0 else 0)

    def tile_dir(b, t, d):
        if before_tile is not None:
            before_tile(b, t)
        rows = tile_rows(t)
        g = g_refs[d][b, rows, :]
        g_hi = g.astype(BF16)
        g_lo = (g - g_hi.astype(F32)).astype(BF16)
        tri_b = tri_ref[d].astype(BF16)
        cum = (jnp.dot(tri_b, g_hi, preferred_element_type=F32)
               + jnp.dot(tri_b, g_lo, preferred_element_type=F32))
        yield
        totals = [cum[total_row(c, d):total_row(c, d) + 1, :] for c in range(CHUNKS_PER_TILE)]
        tot8 = jnp.zeros((8, GQK), F32)
        for c in range(CHUNKS_PER_TILE):
            tot8 = jnp.where(row8 == c, totals[c], tot8)
        dec_t = jnp.concatenate([jnp.exp(tot8), jnp.zeros((LANES - 8, GQK), F32)], axis=0).T
        q = gq_ref[b, rows, :] * (GLA_DK ** -0.5)
        k = gk_ref[b, rows, :]
        v = gv_ref[b, rows, :]
        tot = jnp.concatenate([jnp.broadcast_to(tc, (CHUNK, GQK)) for tc in totals], axis=0)
        qe = (q * jnp.exp(cum)).astype(BF16)
        ke = (k * jnp.exp(-cum)).astype(BF16)
        kd_t = (k * jnp.exp(tot - cum)).T.astype(BF16)
        bd_qk = bdqk_ref[...] > 0
        tri = tri_ref[d] > 0

        qm = jnp.where(hm_ref[...] > 0, jnp.tile(qe, (GLA_HEADS, 1)), 0.0)
        yield
        att = lax.dot_general(qm, ke, _NT, preferred_element_type=F32)

        yield
        intra, upd = [], []
        for hd in range(GLA_HEADS):
            vh = v[:, hd * GLA_DV:(hd + 1) * GLA_DV]
            a_h = jnp.where(tri, att[hd * GLA_TILE:(hd + 1) * GLA_TILE, :], 0.0).astype(BF16)
            intra.append(jnp.dot(a_h, vh, preferred_element_type=F32))
            kd_h = jnp.tile(kd_t[hd * GLA_DK:(hd + 1) * GLA_DK, :], (CHUNKS_PER_TILE, 1))
            upd.append(jnp.dot(jnp.where(bd_qk, kd_h, 0.0), vh, preferred_element_type=F32))

        yield
        state = [state_refs[d][b, hd] for hd in range(GLA_HEADS)]
        order = range(CHUNKS_PER_TILE) if d == 0 else range(CHUNKS_PER_TILE - 1, -1, -1)
        seen = {}
        for c in order:
            seen[c] = jnp.concatenate(state, axis=0).astype(BF16)
            decay = jnp.broadcast_to(dec_t[:, c:c + 1], (GQK, GLA_DV))
            for hd in range(GLA_HEADS):
                ks = slice(hd * GLA_DK, (hd + 1) * GLA_DK)
                state[hd] = decay[ks, :] * state[hd] + upd[hd][c * CHUNK:(c + 1) * CHUNK, :]
        for hd in range(GLA_HEADS):
            state_refs[d][b, hd] = state[hd]

        yield
        for c in range(CHUNKS_PER_TILE):
            cr = slice(c * CHUNK, (c + 1) * CHUNK)
            q_c = jnp.concatenate([qm[hd * GLA_TILE + c * CHUNK:hd * GLA_TILE + (c + 1) * CHUNK, :]
                                   for hd in range(GLA_HEADS)], axis=0)
            inter = jnp.dot(q_c, seen[c], preferred_element_type=F32)
            o = jnp.concatenate([intra[hd][cr, :] + inter[hd * CHUNK:(hd + 1) * CHUNK, :]
                                 for hd in range(GLA_HEADS)], axis=1)
            oacc_ref[d, b, row_slice(t * GLA_TILE + c * CHUNK, CHUNK), :] = o
        if isinstance(t, int):
            finished.add((b, t, d))

    if init_refs is None:
        sf_ref[...] = jnp.zeros(sf_ref.shape, F32)
        sb_ref[...] = jnp.zeros(sb_ref.shape, F32)
    else:
        sf_ref[...] = init_refs[0][...]
        sb_ref[...] = init_refs[1][...]

    gn = gn_ref[...]

    def epilogue(t):
        while isinstance(t, int) and not all((b, t, d) in finished for b in range(n_seqs) for d in range(2)):
            yield
        rows = tile_rows(t)
        for b in range(n_seqs):
            for hd in range(GLA_HEADS):
                vs = slice(hd * GLA_DV, (hd + 1) * GLA_DV)
                o = _rms(oacc_ref[0, b, rows, vs] + oacc_ref[1, b, rows, vs], gn)
                go = go_ref[b, rows, vs]
                o_ref[b, rows, vs] = (o * (go * _sigmoid(go))).astype(BF16)
        return
        yield

    def main_chains(first_tile, n):
        chains = []
        for t in [first_tile + u for u in range(n)]:
            for b in range(n_seqs):
                chains += [tile_dir(b, t, 0), tile_dir(b, n_tiles - 1 - t, 1)]
        return chains

    if n_tiles <= GLA_STATIC_TILES:
        _interleave(list(side_chains) + main_chains(0, n_tiles) + [epilogue(t) for t in range(n_tiles)])
    else:
        assert not side_chains
        per_step = next(c for c in (4, 2, 1) if n_tiles % c == 0)

        def main_body(i, carry):
            _interleave(main_chains(i * per_step, per_step))
            return carry

        lax.fori_loop(0, n_tiles // per_step, main_body, 0)

        def epilogue_body(t, carry):
            _interleave([epilogue(t)])
            return carry

        lax.fori_loop(0, n_tiles, epilogue_body, 0)


def _gla_stream_kernel(*refs, n_tiles, n_seqs):
    hbm = refs[:6]
    sf0_ref, sb0_ref, gn_ref, o_ref, sf_ref, sb_ref = refs[6:12]
    bufs = refs[12:18]
    sem = refs[18]
    gla_scratch = refs[19:]

    def slab_copies(b, t):
        rows = pl.ds(t * GLA_TILE, GLA_TILE)
        return [pltpu.make_async_copy(h.at[b, rows, :], v.at[b, rows, :], sem.at[b, t, a])
                for a, (h, v) in enumerate(zip(hbm, bufs))]

    need_order = [t for u in range(n_tiles // 2) for t in (u, n_tiles - 1 - u)]
    for b in range(n_seqs):
        for t in need_order:
            for cp in slab_copies(b, t):
                cp.start()

    for b in range(n_seqs):
        waited = set()

        def before_tile(_, t, b=b, waited=waited):
            if t not in waited:
                waited.add(t)
                for cp in slab_copies(b, t):
                    cp.wait()

        one = lambda r: r.at[pl.ds(b, 1)]
        _gla_body([one(v) for v in bufs], (one(sf0_ref), one(sb0_ref)), gn_ref,
                  (one(o_ref), one(sf_ref), one(sb_ref)), gla_scratch, n_tiles, 1, (), before_tile)


def _gla_stream(gq, gk, gv, gf, gb, go, init_states, gn):
    b, t, _ = gq.shape
    n_tiles = t // GLA_TILE
    assert n_tiles % 2 == 0 and n_tiles <= GLA_STATIC_TILES
    ins = [gq, gk, gv, gf, gb, go]
    small = list(init_states) + [gn]
    whole = lambda shape: pl.BlockSpec(shape, lambda i: (0,) * len(shape))
    outs = [jax.ShapeDtypeStruct((b, t, GV), BF16),
            jax.ShapeDtypeStruct((b, GLA_HEADS, GLA_DK, GLA_DV), F32),
            jax.ShapeDtypeStruct((b, GLA_HEADS, GLA_DK, GLA_DV), F32)]
    return pl.pallas_call(
        functools.partial(_gla_stream_kernel, n_tiles=n_tiles, n_seqs=b),
        out_shape=outs,
        grid=(1,),
        in_specs=[pl.BlockSpec(memory_space=pl.ANY)] * len(ins) + [whole(a.shape) for a in small],
        out_specs=[whole(o.shape) for o in outs],
        scratch_shapes=([pltpu.VMEM(a.shape, a.dtype) for a in ins]
                        + [pltpu.SemaphoreType.DMA((b, n_tiles, len(ins)))] + _gla_scratch(1, t)),
        name="gla_%d" % t,
        compiler_params=pltpu.CompilerParams(dimension_semantics=("arbitrary",),
                                             vmem_limit_bytes=VMEM_LIMIT),
    )(*ins, *small)


def _gla(gq, gk, gv, gf, gb, go, init_states, gn, n_seqs):
    b, t, _ = gq.shape
    n_tiles = t // GLA_TILE
    zero_init = init_states is None
    seq = lambda c: pl.BlockSpec((n_seqs, t, c), lambda i: (i, 0, 0))
    st = pl.BlockSpec((n_seqs, GLA_HEADS, GLA_DK, GLA_DV), lambda i: (i, 0, 0, 0))
    in_specs = [seq(GQK), seq(GQK), seq(GV), seq(GQK), seq(GQK), seq(GV)]
    args = [gq, gk, gv, gf, gb, go]
    if not zero_init:
        in_specs += [st, st]
        args += list(init_states)
    in_specs.append(_const_spec(gn.shape))
    args.append(gn)
    return pl.pallas_call(
        functools.partial(_gla_kernel, n_tiles=n_tiles, n_seqs=n_seqs, zero_init=zero_init),
        out_shape=[jax.ShapeDtypeStruct((b, t, GV), BF16),
                   jax.ShapeDtypeStruct((b, GLA_HEADS, GLA_DK, GLA_DV), F32),
                   jax.ShapeDtypeStruct((b, GLA_HEADS, GLA_DK, GLA_DV), F32)],
        grid=(b // n_seqs,),
        in_specs=in_specs,
        out_specs=[seq(GV), st, st],
        scratch_shapes=_gla_scratch(n_seqs, t),
        name="gla_%d" % t,
        compiler_params=pltpu.CompilerParams(dimension_semantics=("arbitrary",),
                                             vmem_limit_bytes=VMEM_LIMIT),
    )(*args)


def _ffn_kernel(xp_ref, xs_ref, atp_ref, ats_ref, glp_ref, gls_ref, mod_ref, wout_ref, nf_ref,
                wfi_ref, wfo_ref, fn_ref, yp_ref, ys_ref, act_ref, *, ctx_tiles, tiles_per_seq):
    def sub_tile(x_ref, at_ref, gl_ref, y_ref, mod_row, r0):
        rows = slice(r0, r0 + FFN_SUB)
        gt1, sh2, sc2, gt2 = _mod_rows(mod_ref, mod_row)[2:]
        mix = (jnp.dot(at_ref[rows, :], wout_ref[0:VALL, :], preferred_element_type=F32)
               + jnp.dot(gl_ref[rows, :], wout_ref[VALL:, :], preferred_element_type=F32))
        yield
        x1 = x_ref[rows, :] + gt1 * mix
        h2 = (_rms(x1, nf_ref[...]) * (1.0 + sc2) + sh2).astype(BF16)
        yield
        for j in range(N_FF_CHUNKS):
            cs = slice(j * FF_CHUNK, (j + 1) * FF_CHUNK)
            a = jnp.dot(h2, wfi_ref[:, cs], preferred_element_type=F32)
            g = jnp.dot(h2, wfi_ref[:, D_FF + j * FF_CHUNK:D_FF + (j + 1) * FF_CHUNK],
                        preferred_element_type=F32)
            act_ref[rows, cs] = (a * _sigmoid(a) * g).astype(BF16)
            if j in FFN_PHASE_ENDS:
                yield
        ff = jnp.dot(act_ref[rows, :], wfo_ref[...], preferred_element_type=F32)
        yield
        x2 = x1 + gt2 * ff
        y_ref[rows, :] = _rms(x2, fn_ref[...])

    def tile(x_ref, at_ref, gl_ref, y_ref, mod_row):
        _interleave([sub_tile(x_ref, at_ref, gl_ref, y_ref, mod_row, r0)
                     for r0 in range(0, x_ref.shape[0], FFN_SUB)])

    t = pl.program_id(0)

    @pl.when(t < ctx_tiles)
    def _():
        tile(xp_ref, atp_ref, glp_ref, yp_ref, 0)

    @pl.when(t >= ctx_tiles)
    def _():
        tile(xs_ref, ats_ref, gls_ref, ys_ref, 1 + (t - ctx_tiles) // tiles_per_seq)


def _ffn(xp, xs, attn_p, attn_s, gla_p, gla_s, mod, wout, nf, wfi, wfo, fn, tm, tiles_per_seq):
    d = xp.shape[1]
    ctx_tiles = xp.shape[0] // tm
    lat_tiles = xs.shape[0] // tm
    ctx_map = lambda s: (jnp.minimum(s, ctx_tiles - 1), 0)
    lat_map = lambda s: (jnp.maximum(s - ctx_tiles, 0), 0)
    tile = lambda c, m: pl.BlockSpec((tm, c), m)
    return pl.pallas_call(
        functools.partial(_ffn_kernel, ctx_tiles=ctx_tiles, tiles_per_seq=tiles_per_seq),
        out_shape=[jax.ShapeDtypeStruct(xp.shape, F32), jax.ShapeDtypeStruct(xs.shape, F32)],
        grid=(ctx_tiles + lat_tiles,),
        in_specs=[tile(d, ctx_map), tile(d, lat_map), tile(VALL, ctx_map), tile(VALL, lat_map),
                  tile(GV, ctx_map), tile(GV, lat_map), _const_spec(mod.shape),
                  _const_spec(wout.shape), _const_spec(nf.shape), _const_spec(wfi.shape),
                  _const_spec(wfo.shape), _const_spec(fn.shape)],
        out_specs=[tile(d, ctx_map), tile(d, lat_map)],
        scratch_shapes=[pltpu.VMEM((tm, D_FF), BF16)],
        name="out_ffn",
        compiler_params=pltpu.CompilerParams(dimension_semantics=("arbitrary",),
                                             vmem_limit_bytes=VMEM_LIMIT),
    )(xp, xs, attn_p, attn_s, gla_p, gla_s, mod, wout, nf, wfi, wfo, fn)


def _rope_tables(n_tokens):
    t = np.arange(n_tokens)
    row = (t // GRID_W).astype(np.float32)
    col = (t % GRID_W).astype(np.float32)
    half = MLA_ROPE // 2
    inv = (np.float32(ROPE_BASE) ** (-np.arange(0, half, 2, dtype=np.float32) / np.float32(half))).astype(np.float32)
    ang_r = row[:, None] * inv
    ang_c = col[:, None] * inv
    ang = np.concatenate([ang_r, ang_r, ang_c, ang_c], axis=-1).astype(np.float32)
    cos, sin = np.cos(ang), np.sin(ang)
    first = (np.arange(MLA_ROPE) % half) < (half // 2)
    cos_t = np.ones((n_tokens, LANES), np.float32)
    sa_t = np.zeros((n_tokens, LANES), np.float32)
    sb_t = np.zeros((n_tokens, LANES), np.float32)
    cos_t[:, ROPE_LANE0:ROPE_LANE0 + MLA_ROPE] = cos
    sa_t[:, ROPE_LANE0:ROPE_LANE0 + MLA_ROPE] = np.where(first, -sin, 0.0)
    sb_t[:, ROPE_LANE0:ROPE_LANE0 + MLA_ROPE] = np.where(first, 0.0, sin)
    return jnp.asarray(cos_t), jnp.asarray(sa_t), jnp.asarray(sb_t)


def kernel(x_prompt, x_sample, cache_kv_latent, cache_k_rope, state_gla_fwd, state_gla_bwd, c, c_ctx, w_ada, b_ada, norm_attn, w_in, mla_q_norm, w_uq, mla_kv_norm, w_ukv, w_gate_f, b_gate_f, w_gate_b, b_gate_b, gla_norm, w_out, norm_ffn, w_ffn_in, w_ffn_out, final_norm):
    batch, seq, d = x_prompt.shape
    dec_batch, dec_seq, _ = x_sample.shape
    assert w_ada.shape[0] == 1 and w_in.shape[-1] == W_COLS and w_ffn_in.shape[-1] == 2 * D_FF
    l = 0

    mod = _ada(c_ctx, c, w_ada[l], b_ada[l])

    win, wuq, wk, wvt, wg = _prep_in_weights(w_in, w_uq, w_ukv, w_gate_f, w_gate_b)
    in_w = (norm_attn[l].reshape(1, d), win, mla_q_norm[l].reshape(1, Q_LORA), wuq,
            mla_kv_norm[l].reshape(1, KV_LORA), wk, wvt, wg, b_gate_f, b_gate_b)
    gn = gla_norm[l].reshape(1, GLA_DV)
    tm, tm_ffn = 512, 512
    r3 = lambda a, b_, t: a.reshape(b_, t, a.shape[-1])

    xp = x_prompt.reshape(batch * seq, d)
    (attn_p, gla_p, sf, sb, ckv, kr_t) = _inproj(xp, mod, lambda i: 0, in_w, None, tm, seq, gn)

    xs = x_sample.reshape(dec_batch * dec_seq, d)
    tiles = dec_seq // tm
    (q, k, vt, gq, gk, gv, gf, gb, go) = _inproj(xs, mod, lambda i: 1 + i // tiles, in_w,
                                                  _rope_tables(dec_seq), tm, dec_seq)
    kc, vct = _decomp(cache_kv_latent[:, l], jnp.swapaxes(cache_k_rope[:, l], 1, 2), wk, wvt)
    attn_s, wout, wfi, wfo = _attention(r3(q, dec_batch, dec_seq), r3(k, dec_batch, dec_seq), vt, (kc, vct),
                                        ATTN_LAT_QUERIES, 1, (w_out, w_ffn_in, w_ffn_out))
    gla_s, _, _ = _gla_stream(r3(gq, dec_batch, dec_seq), r3(gk, dec_batch, dec_seq), r3(gv, dec_batch, dec_seq),
                              r3(gf, dec_batch, dec_seq), r3(gb, dec_batch, dec_seq), r3(go, dec_batch, dec_seq),
                              (state_gla_fwd[:, l].astype(F32), state_gla_bwd[:, l].astype(F32)), gn)

    flat = lambda a: a.reshape(-1, a.shape[-1])
    y_prompt, y_sample = _ffn(xp, xs, flat(attn_p), flat(attn_s), flat(gla_p), flat(gla_s), mod,
                              wout, norm_ffn[l].reshape(1, d), wfi, wfo,
                              final_norm.reshape(1, d), tm_ffn, dec_seq // tm_ffn)
    y_prompt = y_prompt.reshape(batch, seq, d)
    y_sample = y_sample.reshape(dec_batch, dec_seq, d)

    new_kv_latent = ckv.reshape(batch, 1, seq, KV_LORA)
    new_k_rope = jnp.swapaxes(kr_t, 1, 2).reshape(batch, 1, seq, MLA_ROPE)
    new_state_fwd = sf.reshape(batch, 1, GLA_HEADS, GLA_DK, GLA_DV).astype(x_prompt.dtype)
    new_state_bwd = sb.reshape(batch, 1, GLA_HEADS, GLA_DK, GLA_DV).astype(x_prompt.dtype)
    return (y_prompt, y_sample, new_kv_latent, new_k_rope, new_state_fwd, new_state_bwd)
```
